```python
import math
import jax, jax.numpy as jnp
from jax import lax
import numpy as np

D_MODEL = 1024
BATCH = 4
SEQ = 8192
DEPTH = 1

CONV_WIDTH = D_MODEL
CONV_K = 3
HEAD_DIM = 64
N_HEADS = 16
N_KV_HEADS = 4
GROUP = N_HEADS // N_KV_HEADS
ATTN_WIDTH = N_HEADS * HEAD_DIM
KV_WIDTH = N_KV_HEADS * HEAD_DIM
WINDOW = 128
BLOCK = 128
N_GROUPS = 4
EXPERTS_PER_GROUP = 8
N_EXPERTS = N_GROUPS * EXPERTS_PER_GROUP
TOP_K = 2
D_FF_EXPERT = 512
EPS = 1e-6

SPLIT_SIZES = [CONV_WIDTH, CONV_WIDTH, CONV_WIDTH,
               ATTN_WIDTH, KV_WIDTH, KV_WIDTH,
               D_MODEL, D_MODEL]
SPLITS = np.cumsum(SPLIT_SIZES)[:-1].tolist()
IN_COLS = int(sum(SPLIT_SIZES))

kernel_name = "hybrid_conv_swa_alibi_hmoe_block"


def rmsnorm(x, g):
    xf = x.astype(jnp.float32)
    r = lax.rsqrt(jnp.mean(xf * xf, axis=-1, keepdims=True) + EPS)
    return (xf * r).astype(x.dtype) * g


def alibi_slopes():
    return jnp.exp2(-8.0 * jnp.arange(1, N_HEADS + 1, dtype=jnp.float32) / N_HEADS)


def short_conv_mixer(b_gate, c_gate, u, conv_w, conv_b):
    cu = c_gate * u
    rhs = conv_w[:, None, :].astype(cu.dtype)
    y = lax.conv_general_dilated(cu, rhs, window_strides=(1,), padding=[(CONV_K - 1, 0)],
                                 dimension_numbers=('NWC', 'WIO', 'NWC'),
                                 feature_group_count=CONV_WIDTH)
    return b_gate * (y + conv_b)


def sliding_window_gqa(q, k, v, sinks):
    bsz, s = q.shape[0], q.shape[1]
    n = s // BLOCK
    qb = q.reshape(bsz, n, BLOCK, N_KV_HEADS, GROUP, HEAD_DIM)
    kb = k.reshape(bsz, n, BLOCK, N_KV_HEADS, HEAD_DIM)
    vb = v.reshape(bsz, n, BLOCK, N_KV_HEADS, HEAD_DIM)
    pad = ((0, 0), (1, 0), (0, 0), (0, 0), (0, 0))
    kw = jnp.concatenate([jnp.pad(kb, pad)[:, :-1], kb], axis=2)
    vw = jnp.concatenate([jnp.pad(vb, pad)[:, :-1], vb], axis=2)
    logits = jnp.einsum('bnqkgd,bnskd->bnkgqs', qb, kw).astype(jnp.float32)
    logits = logits * (1.0 / math.sqrt(HEAD_DIM))
    qi = jnp.arange(BLOCK)[:, None]
    sj = jnp.arange(2 * BLOCK)[None, :]
    dist = (qi - sj + BLOCK).astype(jnp.float32)
    key_pos = jnp.arange(n)[:, None] * BLOCK - BLOCK + sj
    valid = ((dist >= 0) & (dist < WINDOW))[None] & (key_pos >= 0)[:, None, :]
    slopes = alibi_slopes().reshape(N_KV_HEADS, GROUP)
    logits = logits - slopes[:, :, None, None] * dist
    logits = jnp.where(valid[None, :, None, None], logits, -jnp.inf)
    sink = sinks.astype(jnp.float32).reshape(N_KV_HEADS, GROUP)[None, None, :, :, None, None]
    m = jnp.maximum(jnp.max(logits, axis=-1, keepdims=True), sink)
    p = jnp.exp(logits - m)
    denom = jnp.sum(p, axis=-1, keepdims=True) + jnp.exp(sink - m)
    probs = (p / denom).astype(v.dtype)
    out = jnp.einsum('bnkgqs,bnskd->bnqkgd', probs, vw)
    return out.reshape(bsz, s, ATTN_WIDTH)


def hierarchical_moe(h, w_group, b_group, w_expert, b_expert, w_gate, w_up, w_down):
    bsz, s, d = h.shape
    t = bsz * s
    hf = h.reshape(t, d)
    gl = (hf @ w_group + b_group).astype(jnp.float32)
    gp = jax.nn.softmax(gl, axis=-1)
    g_idx = jnp.argmax(gl, axis=-1)
    p_g = jnp.take_along_axis(gp, g_idx[:, None], axis=-1)
    el = (hf @ w_expert + b_expert).astype(jnp.float32).reshape(t, N_GROUPS, EXPERTS_PER_GROUP)
    sel = jnp.take_along_axis(el, g_idx[:, None, None], axis=1)[:, 0]
    top_v, top_i = lax.top_k(sel, TOP_K)
    weights = p_g * jax.nn.softmax(top_v, axis=-1)
    expert_id = (g_idx[:, None] * EXPERTS_PER_GROUP + top_i).reshape(-1)
    order = jnp.argsort(expert_id)
    token_of = order // TOP_K
    sizes = jnp.bincount(expert_id, length=N_EXPERTS).astype(jnp.int32)
    xs = hf[token_of]
    a = lax.ragged_dot(xs, w_gate, sizes)
    b = lax.ragged_dot(xs, w_up, sizes)
    out = lax.ragged_dot(jax.nn.silu(a) * b, w_down, sizes)
    w_sorted = weights.reshape(-1)[order].astype(out.dtype)
    y = jnp.zeros((t, d), out.dtype).at[token_of].add(out * w_sorted[:, None])
    return y.reshape(bsz, s, d)


def setup_inputs(seed: int = 0) -> dict:
    key = jax.random.key(seed)
    ks = jax.random.split(key, 20)
    f32 = jnp.float32

    def nrm(k, shape, fan_in):
        return jax.random.normal(k, shape, f32) * (fan_in ** -0.5)

    def gain(k, n):
        return 1.0 + 0.02 * jax.random.normal(k, (n,), f32)

    return {
        "x": jax.random.normal(ks[0], (BATCH, SEQ, D_MODEL), f32),
        "norm_mix": gain(ks[1], D_MODEL),
        "w_in": nrm(ks[2], (D_MODEL, IN_COLS), D_MODEL),
        "conv_w": nrm(ks[3], (CONV_K, CONV_WIDTH), CONV_K),
        "conv_b": 0.02 * jax.random.normal(ks[4], (CONV_WIDTH,), f32),
        "w_a_out": nrm(ks[5], (CONV_WIDTH, D_MODEL), CONV_WIDTH),
        "sinks": 0.5 * jax.random.normal(ks[6], (N_HEADS,), f32),
        "w_b_out": nrm(ks[7], (ATTN_WIDTH, D_MODEL), ATTN_WIDTH),
        "w_o": nrm(ks[8], (D_MODEL, D_MODEL), D_MODEL),
        "norm_ffn": gain(ks[9], D_MODEL),
        "w_group": nrm(ks[10], (D_MODEL, N_GROUPS), D_MODEL),
        "b_group": 0.01 * jax.random.normal(ks[11], (N_GROUPS,), f32),
        "w_expert": nrm(ks[12], (D_MODEL, N_EXPERTS), D_MODEL),
        "b_expert": 0.01 * jax.random.normal(ks[13], (N_EXPERTS,), f32),
        "w_gate": nrm(ks[14], (N_EXPERTS, D_MODEL, D_FF_EXPERT), D_MODEL),
        "w_up": nrm(ks[15], (N_EXPERTS, D_MODEL, D_FF_EXPERT), D_MODEL),
        "w_down": nrm(ks[16], (N_EXPERTS, D_FF_EXPERT, D_MODEL), D_FF_EXPERT),
        "norm_final": gain(ks[17], D_MODEL),
    }


def reference(x, norm_mix, w_in, conv_w, conv_b, w_a_out, sinks, w_b_out, w_o,
              norm_ffn, w_group, b_group, w_expert, b_expert, w_gate, w_up, w_down,
              norm_final):
    bsz, s, _ = x.shape
    for _layer in range(DEPTH):
        h = rmsnorm(x, norm_mix)
        proj = h @ w_in
        b_gate, c_gate, u, q, k, v, g_a, g_b = jnp.split(proj, SPLITS, axis=-1)
        y_a = short_conv_mixer(b_gate, c_gate, u, conv_w, conv_b) @ w_a_out
        attn = sliding_window_gqa(q.reshape(bsz, s, N_HEADS, HEAD_DIM),
                                  k.reshape(bsz, s, N_KV_HEADS, HEAD_DIM),
                                  v.reshape(bsz, s, N_KV_HEADS, HEAD_DIM), sinks)
        y_b = attn @ w_b_out
        merged = jax.nn.sigmoid(g_a) * y_a + jax.nn.sigmoid(g_b) * y_b
        x = x + merged @ w_o
        x = x + hierarchical_moe(rmsnorm(x, norm_ffn), w_group, b_group, w_expert,
                                 b_expert, w_gate, w_up, w_down)
    return rmsnorm(x, norm_final)
```

```python
import functools
import math

import jax
import jax.numpy as jnp
from jax import lax
from jax.experimental import pallas as pl
from jax.experimental.pallas import tpu as pltpu

F32 = jnp.float32
BF16 = jnp.bfloat16
I32 = jnp.int32

D_MODEL = 1024
HEAD_DIM = 64
N_HEADS = 16
N_KV_HEADS = 4
GROUP = N_HEADS // N_KV_HEADS
KV_WIDTH = N_KV_HEADS * HEAD_DIM
WINDOW = 128
N_GROUPS = 4
EXPERTS_PER_GROUP = 8
N_EXPERTS = N_GROUPS * EXPERTS_PER_GROUP
TOP_K = 2
D_FF = 512
EPS = 1e-6
LANES = 128

REF_SPLITS = (1024, 2048, 3072, 4096, 4352, 4608, 5632)
IN_COLS = 6656
COL_B, COL_C, COL_U, COL_GA, COL_GB, COL_Q = 0, 1, 2, 3, 4, 5
COL_K, COL_V = 24, 25

TM_PROJ = 1024
TN_PROJ = 1664
TM_CONV = 512
TQ_ATTN = 512
TM_MIX = 512
TM_EXP = 256
TM_CMB = TM_MIX
HALO_ROWS = 16
VMEM_LIMIT = 56 * 1024 * 1024


def _rms(x, g):
    r = lax.rsqrt(jnp.mean(x * x, axis=-1, keepdims=True) + EPS)
    return (x * r) * g


def _inproj_kernel(x_ref, g_ref, w_ref, o_ref, h_ref):
    @pl.when(pl.program_id(1) == 0)
    def _():
        h_ref[...] = _rms(x_ref[...], g_ref[...]).astype(BF16)

    o_ref[...] = jnp.dot(h_ref[...], w_ref[...], preferred_element_type=F32).astype(BF16)


def _inproj(xf, g, w):
    t = xf.shape[0]
    return pl.pallas_call(
        _inproj_kernel,
        grid=(t // TM_PROJ, IN_COLS // TN_PROJ),
        in_specs=[
            pl.BlockSpec((TM_PROJ, D_MODEL), lambda i, j: (i, 0)),
            pl.BlockSpec((1, D_MODEL), lambda i, j: (0, 0)),
            pl.BlockSpec((D_MODEL, TN_PROJ), lambda i, j: (0, j)),
        ],
        out_specs=pl.BlockSpec((TM_PROJ, TN_PROJ), lambda i, j: (i, j)),
        out_shape=jax.ShapeDtypeStruct((t, IN_COLS), BF16),
        scratch_shapes=[pltpu.VMEM((TM_PROJ, D_MODEL), BF16)],
        compiler_params=pltpu.CompilerParams(
            dimension_semantics=("arbitrary", "arbitrary"), vmem_limit_bytes=VMEM_LIMIT),
    )(xf, g, w)


def _conv_kernel(b_ref, c_ref, u_ref, ga_ref, cp_ref, up_ref, cw_ref, cb_ref, wa_ref, o_ref,
                 *, tiles_per_seq):
    first = (pl.program_id(0) % tiles_per_seq) == 0
    cu = c_ref[...].astype(F32) * u_ref[...].astype(F32)
    cup = cp_ref[...].astype(F32) * up_ref[...].astype(F32)
    cup = jnp.where(first, 0.0, cup)
    prev1 = cup[HALO_ROWS - 1:HALO_ROWS]
    prev2 = cup[HALO_ROWS - 2:HALO_ROWS - 1]
    row = lax.broadcasted_iota(I32, cu.shape, 0)
    cu1 = jnp.where(row == 0, prev1, pltpu.roll(cu, 1, 0))
    cu2 = jnp.where(row == 0, prev2, jnp.where(row == 1, prev1, pltpu.roll(cu, 2, 0)))
    cw = cw_ref[...]
    y = cw[0:1] * cu2 + cw[1:2] * cu1 + cw[2:3] * cu + cb_ref[...]
    ya = (b_ref[...].astype(F32) * y).astype(BF16)
    z = jnp.dot(ya, wa_ref[...], preferred_element_type=F32)
    o_ref[...] = (jax.nn.sigmoid(ga_ref[...].astype(F32)) * z).astype(BF16)


def _conv_mixer(proj, conv_w, conv_b, wa, seq):
    t = proj.shape[0]
    halo_per_tile = TM_CONV // HALO_ROWS

    def col(c):
        return pl.BlockSpec((TM_CONV, D_MODEL), lambda i: (i, c))

    def halo(c):
        return pl.BlockSpec((HALO_ROWS, D_MODEL),
                            lambda i: (jnp.maximum(i * halo_per_tile - 1, 0), c))

    return pl.pallas_call(
        functools.partial(_conv_kernel, tiles_per_seq=seq // TM_CONV),
        grid=(t // TM_CONV,),
        in_specs=[
            col(COL_B), col(COL_C), col(COL_U), col(COL_GA), halo(COL_C), halo(COL_U),
            pl.BlockSpec((3, D_MODEL), lambda i: (0, 0)),
            pl.BlockSpec((1, D_MODEL), lambda i: (0, 0)),
            pl.BlockSpec((D_MODEL, D_MODEL), lambda i: (0, 0)),
        ],
        out_specs=pl.BlockSpec((TM_CONV, D_MODEL), lambda i: (i, 0)),
        out_shape=jax.ShapeDtypeStruct((t, D_MODEL), BF16),
        compiler_params=pltpu.CompilerParams(
            dimension_semantics=("arbitrary",), vmem_limit_bytes=VMEM_LIMIT),
    )(proj, proj, proj, proj, proj, proj, conv_w, conv_b, wa)


def _attn_kernel(sink_ref, q_ref, k_ref, v_ref, kp_ref, vp_ref, o_ref, *, tiles_per_seq):
    first_tile = (pl.program_id(0) % tiles_per_seq) == 0
    qi = lax.broadcasted_iota(I32, (WINDOW, WINDOW), 0)
    sj = lax.broadcasted_iota(I32, (WINDOW, WINDOW), 1)
    tri = sj <= qi
    dist = jnp.where(tri, qi - sj, qi - sj + WINDOW).astype(F32)
    has_prev0 = jnp.logical_or(tri, jnp.logical_not(first_tile))
    scale = 1.0 / math.sqrt(HEAD_DIM)
    nt = (((1,), (1,)), ((), ()))
    for sb in range(TQ_ATTN // WINDOW):
        rows = slice(sb * WINDOW, (sb + 1) * WINDOW)
        for kh in range(N_KV_HEADS):
            cols = slice(kh * HEAD_DIM, (kh + 1) * HEAD_DIM)
            k_c = k_ref[rows, cols]
            v_c = v_ref[rows, cols]
            if sb == 0:
                k_p = kp_ref[:, cols]
                v_p = vp_ref[:, cols]
            else:
                prow = slice((sb - 1) * WINDOW, sb * WINDOW)
                k_p = k_ref[prow, cols]
                v_p = v_ref[prow, cols]
            for g in range(GROUP):
                h = kh * GROUP + g
                slope = 2.0 ** (-8.0 * (h + 1) / N_HEADS)
                sink = sink_ref[h]
                q_h = q_ref[rows, h * HEAD_DIM:(h + 1) * HEAD_DIM]
                s_c = lax.dot_general(q_h, k_c, nt, preferred_element_type=F32)
                s_p = lax.dot_general(q_h, k_p, nt, preferred_element_type=F32)
                s = jnp.where(tri, s_c, s_p) * scale - slope * dist
                if sb == 0:
                    s = jnp.where(has_prev0, s, -jnp.inf)
                m = jnp.maximum(jnp.max(s, axis=-1, keepdims=True), sink)
                p = jnp.exp(s - m)
                den = jnp.sum(p, axis=-1, keepdims=True) + jnp.exp(sink - m)
                p_c = jnp.where(tri, p, 0.0).astype(BF16)
                p_p = jnp.where(tri, 0.0, p).astype(BF16)
                o = (jnp.dot(p_c, v_c, preferred_element_type=F32)
                     + jnp.dot(p_p, v_p, preferred_element_type=F32))
                o_ref[rows, h * HEAD_DIM:(h + 1) * HEAD_DIM] = (o / den).astype(BF16)


def _attention(proj, sinks, seq):
    t = proj.shape[0]
    sub = TQ_ATTN // WINDOW
    return pl.pallas_call(
        functools.partial(_attn_kernel, tiles_per_seq=seq // TQ_ATTN),
        grid=(t // TQ_ATTN,),
        in_specs=[
            pl.BlockSpec(memory_space=pltpu.SMEM),
            pl.BlockSpec((TQ_ATTN, D_MODEL), lambda i: (i, COL_Q)),
            pl.BlockSpec((TQ_ATTN, KV_WIDTH), lambda i: (i, COL_K)),
            pl.BlockSpec((TQ_ATTN, KV_WIDTH), lambda i: (i, COL_V)),
            pl.BlockSpec((WINDOW, KV_WIDTH), lambda i: (jnp.maximum(i * sub - 1, 0), COL_K)),
            pl.BlockSpec((WINDOW, KV_WIDTH), lambda i: (jnp.maximum(i * sub - 1, 0), COL_V)),
        ],
        out_specs=pl.BlockSpec((TQ_ATTN, D_MODEL), lambda i: (i, 0)),
        out_shape=jax.ShapeDtypeStruct((t, D_MODEL), BF16),
        compiler_params=pltpu.CompilerParams(
            dimension_semantics=("arbitrary",), vmem_limit_bytes=VMEM_LIMIT),
    )(sinks, proj, proj, proj, proj, proj)


def _mix_kernel(attn_ref, gb_ref, za_ref, x_ref, wb_ref, wo_ref, g_ref, wrh_ref, wrl_ref, br_ref,
                xmid_ref, h_ref, route_ref, cnt_ref):
    yb = jnp.dot(attn_ref[...], wb_ref[...], preferred_element_type=F32)
    merged = za_ref[...].astype(F32) + jax.nn.sigmoid(gb_ref[...].astype(F32)) * yb
    xm = x_ref[...] + jnp.dot(merged.astype(BF16), wo_ref[...], preferred_element_type=F32)
    xmid_ref[...] = xm
    h = _rms(xm, g_ref[...])
    h_ref[...] = h
    h_hi = h.astype(BF16)
    h_lo = (h - h_hi.astype(F32)).astype(BF16)
    wrh = wrh_ref[...]
    logits = (jnp.dot(h_hi, wrh, preferred_element_type=F32)
              + jnp.dot(h_lo, wrh, preferred_element_type=F32)
              + jnp.dot(h_hi, wrl_ref[...], preferred_element_type=F32)) + br_ref[...]
    lane = lax.broadcasted_iota(I32, logits.shape, 1)
    neg = -jnp.inf
    gl = jnp.where(lane < N_GROUPS, logits, neg)
    gmax = jnp.max(gl, axis=-1, keepdims=True)
    g_idx = jnp.min(jnp.where(gl == gmax, lane, LANES), axis=-1, keepdims=True)
    p_g = 1.0 / jnp.sum(jnp.exp(gl - gmax), axis=-1, keepdims=True)
    start = N_GROUPS + EXPERTS_PER_GROUP * g_idx
    el = jnp.where((lane >= start) & (lane < start + EXPERTS_PER_GROUP), logits, neg)
    v1 = jnp.max(el, axis=-1, keepdims=True)
    i1 = jnp.min(jnp.where(el == v1, lane, LANES), axis=-1, keepdims=True)
    el2 = jnp.where(lane == i1, neg, el)
    v2 = jnp.max(el2, axis=-1, keepdims=True)
    i2 = jnp.min(jnp.where(el2 == v2, lane, LANES), axis=-1, keepdims=True)
    e21 = jnp.exp(v2 - v1)
    w1 = p_g / (1.0 + e21)
    w2 = p_g * e21 / (1.0 + e21)
    e1 = i1 - N_GROUPS
    e2 = i2 - N_GROUPS
    route = jnp.where(lane == 0, e1.astype(F32),
                      jnp.where(lane == 1, e2.astype(F32),
                                jnp.where(lane == 2, w1, jnp.where(lane == 3, w2, 0.0))))
    route_ref[...] = route
    onehot = ((lane == e1) | (lane == e2)).astype(F32)
    cnt = jnp.sum(onehot, axis=0, keepdims=True)
    cnt_ref[...] = jnp.broadcast_to(cnt, cnt_ref.shape)


def _mix(attn, proj, za, xf, wb, wo, g, wr_hi, wr_lo, br):
    t = xf.shape[0]
    n_tiles = t // TM_MIX
    full = lambda shape: pl.BlockSpec(shape, lambda i: (0, 0))
    tile = lambda w=D_MODEL: pl.BlockSpec((TM_MIX, w), lambda i: (i, 0))
    return pl.pallas_call(
        _mix_kernel,
        grid=(n_tiles,),
        in_specs=[
            tile(),
            pl.BlockSpec((TM_MIX, D_MODEL), lambda i: (i, COL_GB)),
            tile(), tile(),
            full((D_MODEL, D_MODEL)), full((D_MODEL, D_MODEL)), full((1, D_MODEL)),
            full((D_MODEL, LANES)), full((D_MODEL, LANES)), full((1, LANES)),
        ],
        out_specs=[tile(), tile(), tile(LANES), pl.BlockSpec((8, LANES), lambda i: (i, 0))],
        out_shape=[
            jax.ShapeDtypeStruct((t, D_MODEL), F32),
            jax.ShapeDtypeStruct((t, D_MODEL), F32),
            jax.ShapeDtypeStruct((t, LANES), F32),
            jax.ShapeDtypeStruct((n_tiles * 8, LANES), F32),
        ],
        compiler_params=pltpu.CompilerParams(
            dimension_semantics=("arbitrary",), vmem_limit_bytes=VMEM_LIMIT),
    )(attn, proj, za, xf, wb, wo, g, wr_hi, wr_lo, br)


def _pos_kernel(route_ref, base_ref, pos_ref):
    route = route_ref[...]
    lane = lax.broadcasted_iota(I32, route.shape, 1)
    e1 = jnp.sum(jnp.where(lane == 0, route, 0.0), axis=-1, keepdims=True).astype(I32)
    e2 = jnp.sum(jnp.where(lane == 1, route, 0.0), axis=-1, keepdims=True).astype(I32)
    onehot = ((lane == e1) | (lane == e2)).astype(BF16)
    r = lax.broadcasted_iota(I32, (TM_MIX, TM_MIX), 0)
    c = lax.broadcasted_iota(I32, (TM_MIX, TM_MIX), 1)
    lower = (c < r).astype(BF16)
    before = jnp.dot(lower, onehot, preferred_element_type=F32) + base_ref[0]
    p1 = jnp.sum(jnp.where(lane == e1, before, 0.0), axis=-1, keepdims=True)
    p2 = jnp.sum(jnp.where(lane == e2, before, 0.0), axis=-1, keepdims=True)
    packed = jnp.where(lane == 0, p1, jnp.where(lane == 1, p2, 0.0))
    pos_ref[0] = jnp.transpose(packed)[0:TOP_K, :].astype(I32)


def _positions(route, base):
    t = route.shape[0]
    n_tiles = t // TM_MIX
    return pl.pallas_call(
        _pos_kernel,
        grid=(n_tiles,),
        in_specs=[
            pl.BlockSpec((TM_MIX, LANES), lambda i: (i, 0)),
            pl.BlockSpec((1, 1, LANES), lambda i: (i, 0, 0)),
        ],
        out_specs=pl.BlockSpec((1, TOP_K, TM_MIX), lambda i: (i, 0, 0)),
        out_shape=jax.ShapeDtypeStruct((n_tiles, TOP_K, TM_MIX), I32),
        compiler_params=pltpu.CompilerParams(dimension_semantics=("arbitrary",)),
    )(route, base)


def _dispatch_kernel(pos_ref, h_hbm, xs_in, xs_hbm, sem):
    del xs_in
    i = pl.program_id(0)

    def row_copy(r, k):
        dst = pos_ref[0, k, r]
        return pltpu.make_async_copy(h_hbm.at[pl.ds(i * TM_MIX + r, 1), :],
                                     xs_hbm.at[pl.ds(dst, 1), :], sem)

    def issue(r, carry):
        row_copy(r, 0).start()
        row_copy(r, 1).start()
        return carry

    def drain(r, carry):
        row_copy(r, 0).wait()
        row_copy(r, 1).wait()
        return carry

    lax.fori_loop(0, TM_MIX, issue, 0)
    lax.fori_loop(0, TM_MIX, drain, 0)


def _dispatch(pos, h, rows):
    n_tiles = pos.shape[0]
    xs0 = jnp.zeros((rows, D_MODEL), F32)
    return pl.pallas_call(
        _dispatch_kernel,
        grid=(n_tiles,),
        in_specs=[
            pl.BlockSpec((1, TOP_K, TM_MIX), lambda i: (i, 0, 0), memory_space=pltpu.SMEM),
            pl.BlockSpec(memory_space=pl.ANY),
            pl.BlockSpec(memory_space=pl.ANY),
        ],
        out_specs=pl.BlockSpec(memory_space=pl.ANY),
        out_shape=jax.ShapeDtypeStruct((rows, D_MODEL), F32),
        scratch_shapes=[pltpu.SemaphoreType.DMA],
        input_output_aliases={2: 0},
        compiler_params=pltpu.CompilerParams(
            dimension_semantics=("arbitrary",), has_side_effects=True),
    )(pos, h, xs0)


def _expert_kernel(te_ref, ts_ref, na_ref, xs_ref, wg_ref, wu_ref, wd_ref, o_ref):
    del te_ref, ts_ref
    active = pl.program_id(0) < na_ref[0]

    @pl.when(active)
    def _():
        x = xs_ref[...].astype(BF16)
        a = jnp.dot(x, wg_ref[0], preferred_element_type=F32)
        b = jnp.dot(x, wu_ref[0], preferred_element_type=F32)
        hmid = (a * jax.nn.sigmoid(a) * b).astype(BF16)
        o_ref[...] = jnp.dot(hmid, wd_ref[0], preferred_element_type=F32)

    @pl.when(jnp.logical_not(active))
    def _():
        o_ref[...] = jnp.zeros(o_ref.shape, o_ref.dtype)


def _experts(tile_expert, tile_src, n_active, xs, wg, wu, wd):
    rows = xs.shape[0]
    grid_spec = pltpu.PrefetchScalarGridSpec(
        num_scalar_prefetch=3,
        grid=(rows // TM_EXP,),
        in_specs=[
            pl.BlockSpec((TM_EXP, D_MODEL), lambda i, te, ts, na: (ts[i], 0)),
            pl.BlockSpec((1, D_MODEL, D_FF), lambda i, te, ts, na: (te[i], 0, 0)),
            pl.BlockSpec((1, D_MODEL, D_FF), lambda i, te, ts, na: (te[i], 0, 0)),
            pl.BlockSpec((1, D_FF, D_MODEL), lambda i, te, ts, na: (te[i], 0, 0)),
        ],
        out_specs=pl.BlockSpec((TM_EXP, D_MODEL), lambda i, te, ts, na: (i, 0)),
    )
    return pl.pallas_call(
        _expert_kernel,
        grid_spec=grid_spec,
        out_shape=jax.ShapeDtypeStruct((rows, D_MODEL), F32),
        compiler_params=pltpu.CompilerParams(
            dimension_semantics=("arbitrary",), vmem_limit_bytes=VMEM_LIMIT),
    )(tile_expert, tile_src, n_active, xs, wg, wu, wd)


def _combine_kernel(pos_ref, route_ref, xmid_ref, g_ref, ys_hbm, o_ref, buf, sem):
    def row_copy(r, k):
        src = pos_ref[0, k, r]
        return pltpu.make_async_copy(ys_hbm.at[pl.ds(src, 1), :], buf.at[k, pl.ds(r, 1), :], sem)

    def issue(r, carry):
        row_copy(r, 0).start()
        row_copy(r, 1).start()
        return carry

    def drain(r, carry):
        row_copy(r, 0).wait()
        row_copy(r, 1).wait()
        return carry

    lax.fori_loop(0, TM_CMB, issue, 0)
    lax.fori_loop(0, TM_CMB, drain, 0)
    route = route_ref[...]
    lane = lax.broadcasted_iota(I32, route.shape, 1)
    w1 = jnp.sum(jnp.where(lane == 2, route, 0.0), axis=-1, keepdims=True)
    w2 = jnp.sum(jnp.where(lane == 3, route, 0.0), axis=-1, keepdims=True)
    y = buf[0] * w1 + buf[1] * w2
    o_ref[...] = _rms(xmid_ref[...] + y, g_ref[...])


def _combine(pos, route, xmid, g, ys):
    t = xmid.shape[0]
    return pl.pallas_call(
        _combine_kernel,
        grid=(t // TM_CMB,),
        in_specs=[
            pl.BlockSpec((1, TOP_K, TM_CMB), lambda i: (i, 0, 0), memory_space=pltpu.SMEM),
            pl.BlockSpec((TM_CMB, LANES), lambda i: (i, 0)),
            pl.BlockSpec((TM_CMB, D_MODEL), lambda i: (i, 0)),
            pl.BlockSpec((1, D_MODEL), lambda i: (0, 0)),
            pl.BlockSpec(memory_space=pl.ANY),
        ],
        out_specs=pl.BlockSpec((TM_CMB, D_MODEL), lambda i: (i, 0)),
        out_shape=jax.ShapeDtypeStruct((t, D_MODEL), F32),
        scratch_shapes=[pltpu.VMEM((TOP_K, TM_CMB, D_MODEL), F32), pltpu.SemaphoreType.DMA],
        compiler_params=pltpu.CompilerParams(
            dimension_semantics=("arbitrary",), vmem_limit_bytes=VMEM_LIMIT),
    )(pos, route, xmid, g, ys)


def _split_bf16(w):
    hi = w.astype(BF16)
    lo = (w - hi.astype(F32)).astype(BF16)
    return hi, lo


def kernel(x, norm_mix, w_in, conv_w, conv_b, w_a_out, sinks, w_b_out, w_o, norm_ffn, w_group,
           b_group, w_expert, b_expert, w_gate, w_up, w_down, norm_final):
    bsz, seq, d = x.shape
    t = bsz * seq
    assert d == D_MODEL and seq % TM_PROJ == 0 and seq % TQ_ATTN == 0 and seq % TM_CONV == 0
    xf = x.reshape(t, d)
    row = lambda v: v.reshape(1, -1)

    w_b, w_c, w_u, w_q, w_k, w_v, w_ga, w_gb = jnp.split(w_in, REF_SPLITS, axis=1)
    w_in_r = jnp.concatenate([w_b, w_c, w_u, w_ga, w_gb, w_q, w_k, w_v], axis=1).astype(BF16)

    proj = _inproj(xf, row(norm_mix), w_in_r)
    za = _conv_mixer(proj, conv_w, row(conv_b), w_a_out.astype(BF16), seq)
    attn = _attention(proj, sinks, seq)

    pad = LANES - N_GROUPS - N_EXPERTS
    w_r = jnp.concatenate([w_group, w_expert, jnp.zeros((d, pad), F32)], axis=1)
    b_r = jnp.concatenate([b_group, b_expert, jnp.zeros((pad,), F32)]).reshape(1, LANES)
    wr_hi, wr_lo = _split_bf16(w_r)
    xmid, h2, route, cnt = _mix(attn, proj, za, xf, w_b_out.astype(BF16), w_o.astype(BF16),
                                row(norm_ffn), wr_hi, wr_lo, b_r)

    n_tiles = t // TM_MIX
    cnt = cnt.reshape(n_tiles, 8, LANES)[:, 0, :N_EXPERTS].astype(I32)
    totals = jnp.sum(cnt, axis=0)
    tiles_e = (totals + TM_EXP - 1) // TM_EXP
    tile_end = jnp.cumsum(tiles_e)
    offset = (tile_end - tiles_e) * TM_EXP
    base = offset[None, :] + jnp.cumsum(cnt, axis=0) - cnt
    base = jnp.pad(base, ((0, 0), (0, LANES - N_EXPERTS))).astype(F32).reshape(n_tiles, 1, LANES)
    rows = t * TOP_K + N_EXPERTS * TM_EXP
    n_active = tile_end[-1]
    tile_id = jnp.arange(rows // TM_EXP, dtype=I32)
    tile_src = jnp.minimum(tile_id, n_active - 1).astype(I32)
    tile_expert = jnp.minimum(
        jnp.searchsorted(tile_end, tile_src, side="right"), N_EXPERTS - 1).astype(I32)

    pos = _positions(route, base)
    xs = _dispatch(pos, h2, rows)
    ys = _experts(tile_expert, tile_src, n_active.reshape(1).astype(I32), xs,
                  w_gate.astype(BF16), w_up.astype(BF16), w_down.astype(BF16))
    out = _combine(pos, route, xmid, row(norm_final), ys)
    return out.reshape(bsz, seq, d)
```

```python
import functools
import math

import jax
import jax.numpy as jnp
from jax import lax
from jax.experimental import pallas as pl
from jax.experimental.pallas import tpu as pltpu
from jax.experimental.pallas import tpu_sc as plsc

F32 = jnp.float32
BF16 = jnp.bfloat16
I32 = jnp.int32

D_MODEL = 1024
HEAD_DIM = 64
N_HEADS = 16
N_KV_HEADS = 4
GROUP = N_HEADS // N_KV_HEADS
KV_WIDTH = N_KV_HEADS * HEAD_DIM
WINDOW = 128
N_GROUPS = 4
EXPERTS_PER_GROUP = 8
N_EXPERTS = N_GROUPS * EXPERTS_PER_GROUP
TOP_K = 2
D_FF = 512
EPS = 1e-6
LANES = 128

REF_SPLITS = (1024, 2048, 3072, 4096, 4352, 4608, 5632)
IN_COLS = 6656
COL_B, COL_C, COL_U, COL_GA, COL_GB, COL_Q = 0, 1, 2, 3, 4, 5
COL_K, COL_V = 24, 25

TM_PROJ = 1024
TN_PROJ = 1664
TM_CONV = 512
TQ_ATTN = 512
TM_MIX = 512
TM_EXP = 256
TM_CMB = TM_MIX
HALO_ROWS = 16
VMEM_LIMIT = 56 * 1024 * 1024
PACKED = D_MODEL // 2

SC_CORES = 2
SC_SUBCORES = 16
SC_WORKERS = SC_CORES * SC_SUBCORES
SC_CHUNK = 64


def _rms(x, g):
    r = lax.rsqrt(jnp.mean(x * x, axis=-1, keepdims=True) + EPS)
    return (x * r) * g


def _pack_bf16_pairs(x):
    n = x.shape[1] // 2
    lo = lax.bitcast_convert_type(x[:, :n].astype(BF16).astype(F32), I32)
    hi = lax.bitcast_convert_type(x[:, n:].astype(BF16).astype(F32), I32)
    return (hi & jnp.int32(-65536)) | lax.shift_right_logical(lo, 16)


def _unpack_bf16_pairs(p):
    lo = lax.bitcast_convert_type(lax.shift_left(p, 16), F32)
    hi = lax.bitcast_convert_type(p & jnp.int32(-65536), F32)
    return lo, hi


def _inproj_kernel(x_ref, g_ref, w_ref, o_ref, h_ref):
    @pl.when(pl.program_id(1) == 0)
    def _():
        h_ref[...] = _rms(x_ref[...], g_ref[...]).astype(BF16)

    o_ref[...] = jnp.dot(h_ref[...], w_ref[...], preferred_element_type=F32).astype(BF16)


def _inproj(xf, g, w):
    t = xf.shape[0]
    return pl.pallas_call(
        _inproj_kernel,
        grid=(t // TM_PROJ, IN_COLS // TN_PROJ),
        in_specs=[
            pl.BlockSpec((TM_PROJ, D_MODEL), lambda i, j: (i, 0)),
            pl.BlockSpec((1, D_MODEL), lambda i, j: (0, 0)),
            pl.BlockSpec((D_MODEL, TN_PROJ), lambda i, j: (0, j)),
        ],
        out_specs=pl.BlockSpec((TM_PROJ, TN_PROJ), lambda i, j: (i, j)),
        out_shape=jax.ShapeDtypeStruct((t, IN_COLS), BF16),
        scratch_shapes=[pltpu.VMEM((TM_PROJ, D_MODEL), BF16)],
        compiler_params=pltpu.CompilerParams(
            dimension_semantics=("arbitrary", "arbitrary"), vmem_limit_bytes=VMEM_LIMIT),
    )(xf, g, w)


def _conv_kernel(b_ref, c_ref, u_ref, ga_ref, cp_ref, up_ref, cw_ref, cb_ref, wa_ref, o_ref,
                 *, tiles_per_seq):
    first = (pl.program_id(0) % tiles_per_seq) == 0
    cu = c_ref[...].astype(F32) * u_ref[...].astype(F32)
    cup = cp_ref[...].astype(F32) * up_ref[...].astype(F32)
    cup = jnp.where(first, 0.0, cup)
    prev1 = cup[HALO_ROWS - 1:HALO_ROWS]
    prev2 = cup[HALO_ROWS - 2:HALO_ROWS - 1]
    row = lax.broadcasted_iota(I32, cu.shape, 0)
    cu1 = jnp.where(row == 0, prev1, pltpu.roll(cu, 1, 0))
    cu2 = jnp.where(row == 0, prev2, jnp.where(row == 1, prev1, pltpu.roll(cu, 2, 0)))
    cw = cw_ref[...]
    y = cw[0:1] * cu2 + cw[1:2] * cu1 + cw[2:3] * cu + cb_ref[...]
    ya = (b_ref[...].astype(F32) * y).astype(BF16)
    z = jnp.dot(ya, wa_ref[...], preferred_element_type=F32)
    o_ref[...] = (jax.nn.sigmoid(ga_ref[...].astype(F32)) * z).astype(BF16)


def _conv_mixer(proj, conv_w, conv_b, wa, seq):
    t = proj.shape[0]
    halo_per_tile = TM_CONV // HALO_ROWS

    def col(c):
        return pl.BlockSpec((TM_CONV, D_MODEL), lambda i: (i, c))

    def halo(c):
        return pl.BlockSpec((HALO_ROWS, D_MODEL),
                            lambda i: (jnp.maximum(i * halo_per_tile - 1, 0), c))

    return pl.pallas_call(
        functools.partial(_conv_kernel, tiles_per_seq=seq // TM_CONV),
        grid=(t // TM_CONV,),
        in_specs=[
            col(COL_B), col(COL_C), col(COL_U), col(COL_GA), halo(COL_C), halo(COL_U),
            pl.BlockSpec((3, D_MODEL), lambda i: (0, 0)),
            pl.BlockSpec((1, D_MODEL), lambda i: (0, 0)),
            pl.BlockSpec((D_MODEL, D_MODEL), lambda i: (0, 0)),
        ],
        out_specs=pl.BlockSpec((TM_CONV, D_MODEL), lambda i: (i, 0)),
        out_shape=jax.ShapeDtypeStruct((t, D_MODEL), BF16),
        compiler_params=pltpu.CompilerParams(
            dimension_semantics=("arbitrary",), vmem_limit_bytes=VMEM_LIMIT),
    )(proj, proj, proj, proj, proj, proj, conv_w, conv_b, wa)


def _attn_kernel(sink_ref, q_ref, k_ref, v_ref, kp_ref, vp_ref, o_ref, *, tiles_per_seq):
    first_tile = (pl.program_id(0) % tiles_per_seq) == 0
    qi = lax.broadcasted_iota(I32, (WINDOW, WINDOW), 0)
    sj = lax.broadcasted_iota(I32, (WINDOW, WINDOW), 1)
    tri = sj <= qi
    dist = jnp.where(tri, qi - sj, qi - sj + WINDOW).astype(F32)
    has_prev0 = jnp.logical_or(tri, jnp.logical_not(first_tile))
    scale = 1.0 / math.sqrt(HEAD_DIM)
    nt = (((1,), (1,)), ((), ()))
    for sb in range(TQ_ATTN // WINDOW):
        rows = slice(sb * WINDOW, (sb + 1) * WINDOW)
        for kh in range(N_KV_HEADS):
            cols = slice(kh * HEAD_DIM, (kh + 1) * HEAD_DIM)
            k_c = k_ref[rows, cols]
            v_c = v_ref[rows, cols]
            if sb == 0:
                k_p = kp_ref[:, cols]
                v_p = vp_ref[:, cols]
            else:
                prow = slice((sb - 1) * WINDOW, sb * WINDOW)
                k_p = k_ref[prow, cols]
                v_p = v_ref[prow, cols]
            for g in range(GROUP):
                h = kh * GROUP + g
                slope = 2.0 ** (-8.0 * (h + 1) / N_HEADS)
                sink = sink_ref[h]
                q_h = q_ref[rows, h * HEAD_DIM:(h + 1) * HEAD_DIM]
                s_c = lax.dot_general(q_h, k_c, nt, preferred_element_type=F32)
                s_p = lax.dot_general(q_h, k_p, nt, preferred_element_type=F32)
                s = jnp.where(tri, s_c, s_p) * scale - slope * dist
                if sb == 0:
                    s = jnp.where(has_prev0, s, -jnp.inf)
                m = jnp.maximum(jnp.max(s, axis=-1, keepdims=True), sink)
                p = jnp.exp(s - m)
                den = jnp.sum(p, axis=-1, keepdims=True) + jnp.exp(sink - m)
                p_c = jnp.where(tri, p, 0.0).astype(BF16)
                p_p = jnp.where(tri, 0.0, p).astype(BF16)
                o = (jnp.dot(p_c, v_c, preferred_element_type=F32)
                     + jnp.dot(p_p, v_p, preferred_element_type=F32))
                o_ref[rows, h * HEAD_DIM:(h + 1) * HEAD_DIM] = (o / den).astype(BF16)


def _attention(proj, sinks, seq):
    t = proj.shape[0]
    sub = TQ_ATTN // WINDOW
    return pl.pallas_call(
        functools.partial(_attn_kernel, tiles_per_seq=seq // TQ_ATTN),
        grid=(t // TQ_ATTN,),
        in_specs=[
            pl.BlockSpec(memory_space=pltpu.SMEM),
            pl.BlockSpec((TQ_ATTN, D_MODEL), lambda i: (i, COL_Q)),
            pl.BlockSpec((TQ_ATTN, KV_WIDTH), lambda i: (i, COL_K)),
            pl.BlockSpec((TQ_ATTN, KV_WIDTH), lambda i: (i, COL_V)),
            pl.BlockSpec((WINDOW, KV_WIDTH), lambda i: (jnp.maximum(i * sub - 1, 0), COL_K)),
            pl.BlockSpec((WINDOW, KV_WIDTH), lambda i: (jnp.maximum(i * sub - 1, 0), COL_V)),
        ],
        out_specs=pl.BlockSpec((TQ_ATTN, D_MODEL), lambda i: (i, 0)),
        out_shape=jax.ShapeDtypeStruct((t, D_MODEL), BF16),
        compiler_params=pltpu.CompilerParams(
            dimension_semantics=("arbitrary",), vmem_limit_bytes=VMEM_LIMIT),
    )(sinks, proj, proj, proj, proj, proj)


def _mix_kernel(attn_ref, gb_ref, za_ref, x_ref, wb_ref, wo_ref, g_ref, wrh_ref, wrl_ref, br_ref,
                xmid_ref, h_ref, route_ref, cnt_ref):
    yb = jnp.dot(attn_ref[...], wb_ref[...], preferred_element_type=F32)
    merged = za_ref[...].astype(F32) + jax.nn.sigmoid(gb_ref[...].astype(F32)) * yb
    xm = x_ref[...] + jnp.dot(merged.astype(BF16), wo_ref[...], preferred_element_type=F32)
    xmid_ref[...] = xm
    h = _rms(xm, g_ref[...])
    h_ref[...] = _pack_bf16_pairs(h)
    h_hi = h.astype(BF16)
    h_lo = (h - h_hi.astype(F32)).astype(BF16)
    wrh = wrh_ref[...]
    logits = (jnp.dot(h_hi, wrh, preferred_element_type=F32)
              + jnp.dot(h_lo, wrh, preferred_element_type=F32)
              + jnp.dot(h_hi, wrl_ref[...], preferred_element_type=F32)) + br_ref[...]
    lane = lax.broadcasted_iota(I32, logits.shape, 1)
    neg = -jnp.inf
    gl = jnp.where(lane < N_GROUPS, logits, neg)
    gmax = jnp.max(gl, axis=-1, keepdims=True)
    g_idx = jnp.min(jnp.where(gl == gmax, lane, LANES), axis=-1, keepdims=True)
    p_g = 1.0 / jnp.sum(jnp.exp(gl - gmax), axis=-1, keepdims=True)
    start = N_GROUPS + EXPERTS_PER_GROUP * g_idx
    el = jnp.where((lane >= start) & (lane < start + EXPERTS_PER_GROUP), logits, neg)
    v1 = jnp.max(el, axis=-1, keepdims=True)
    i1 = jnp.min(jnp.where(el == v1, lane, LANES), axis=-1, keepdims=True)
    el2 = jnp.where(lane == i1, neg, el)
    v2 = jnp.max(el2, axis=-1, keepdims=True)
    i2 = jnp.min(jnp.where(el2 == v2, lane, LANES), axis=-1, keepdims=True)
    e21 = jnp.exp(v2 - v1)
    w1 = p_g / (1.0 + e21)
    w2 = p_g * e21 / (1.0 + e21)
    e1 = i1 - N_GROUPS
    e2 = i2 - N_GROUPS
    route = jnp.where(lane == 0, e1.astype(F32),
                      jnp.where(lane == 1, e2.astype(F32),
                                jnp.where(lane == 2, w1, jnp.where(lane == 3, w2, 0.0))))
    route_ref[...] = route
    onehot = ((lane == e1) | (lane == e2)).astype(F32)
    cnt = jnp.sum(onehot, axis=0, keepdims=True)
    cnt_ref[...] = jnp.broadcast_to(cnt, cnt_ref.shape)


def _mix(attn, proj, za, xf, wb, wo, g, wr_hi, wr_lo, br):
    t = xf.shape[0]
    n_tiles = t // TM_MIX
    full = lambda shape: pl.BlockSpec(shape, lambda i: (0, 0))
    tile = lambda w=D_MODEL: pl.BlockSpec((TM_MIX, w), lambda i: (i, 0))
    return pl.pallas_call(
        _mix_kernel,
        grid=(n_tiles,),
        in_specs=[
            tile(),
            pl.BlockSpec((TM_MIX, D_MODEL), lambda i: (i, COL_GB)),
            tile(), tile(),
            full((D_MODEL, D_MODEL)), full((D_MODEL, D_MODEL)), full((1, D_MODEL)),
            full((D_MODEL, LANES)), full((D_MODEL, LANES)), full((1, LANES)),
        ],
        out_specs=[tile(), tile(PACKED), tile(LANES), pl.BlockSpec((8, LANES), lambda i: (i, 0))],
        out_shape=[
            jax.ShapeDtypeStruct((t, D_MODEL), F32),
            jax.ShapeDtypeStruct((t, PACKED), I32),
            jax.ShapeDtypeStruct((t, LANES), F32),
            jax.ShapeDtypeStruct((n_tiles * 8, LANES), F32),
        ],
        compiler_params=pltpu.CompilerParams(
            dimension_semantics=("arbitrary",), vmem_limit_bytes=VMEM_LIMIT),
    )(attn, proj, za, xf, wb, wo, g, wr_hi, wr_lo, br)


def _pos_kernel(route_ref, base_ref, pos_ref):
    route = route_ref[...]
    lane = lax.broadcasted_iota(I32, route.shape, 1)
    e1 = jnp.sum(jnp.where(lane == 0, route, 0.0), axis=-1, keepdims=True).astype(I32)
    e2 = jnp.sum(jnp.where(lane == 1, route, 0.0), axis=-1, keepdims=True).astype(I32)
    onehot = ((lane == e1) | (lane == e2)).astype(BF16)
    r = lax.broadcasted_iota(I32, (TM_MIX, TM_MIX), 0)
    c = lax.broadcasted_iota(I32, (TM_MIX, TM_MIX), 1)
    lower = (c < r).astype(BF16)
    before = jnp.dot(lower, onehot, preferred_element_type=F32) + base_ref[0]
    p1 = jnp.sum(jnp.where(lane == e1, before, 0.0), axis=-1, keepdims=True)
    p2 = jnp.sum(jnp.where(lane == e2, before, 0.0), axis=-1, keepdims=True)
    packed = jnp.where(lane == 0, p1, jnp.where(lane == 1, p2, 0.0))
    pos_ref[...] = jnp.transpose(packed)[0:TOP_K, :].astype(I32)


def _positions(route, base):
    t = route.shape[0]
    n_tiles = t // TM_MIX
    return pl.pallas_call(
        _pos_kernel,
        grid=(n_tiles,),
        in_specs=[
            pl.BlockSpec((TM_MIX, LANES), lambda i: (i, 0)),
            pl.BlockSpec((1, 1, LANES), lambda i: (i, 0, 0)),
        ],
        out_specs=pl.BlockSpec((TOP_K, TM_MIX), lambda i: (0, i)),
        out_shape=jax.ShapeDtypeStruct((TOP_K, t), I32),
        compiler_params=pltpu.CompilerParams(dimension_semantics=("arbitrary",)),
    )(route, base)


def _sc_mesh():
    return plsc.VectorSubcoreMesh(core_axis_name="c", subcore_axis_name="s",
                                  num_cores=SC_CORES, num_subcores=SC_SUBCORES)


def _sc_worker():
    return lax.axis_index("s") * SC_CORES + lax.axis_index("c")


def _dispatch(pos, hp, rows):
    t = hp.shape[0]
    per_w = t // SC_WORKERS
    n_ch = per_w // SC_CHUNK
    pos4 = pos.reshape(TOP_K, SC_WORKERS, n_ch, SC_CHUNK)

    @functools.partial(
        pl.kernel, mesh=_sc_mesh(),
        out_type=jax.ShapeDtypeStruct((rows, PACKED), I32),
        scratch_types=[pltpu.VMEM((TOP_K, n_ch, SC_CHUNK), I32),
                       pltpu.VMEM((SC_CHUNK, PACKED), I32)])
    def scatter(hp_hbm, pos_hbm, xs_hbm, idx_v, rows_v):
        wid = _sc_worker()
        for k in range(TOP_K):
            pltpu.sync_copy(pos_hbm.at[k, wid], idx_v.at[k])

        def body(c, carry):
            start = pl.multiple_of(wid * per_w + c * SC_CHUNK, SC_CHUNK)
            pltpu.sync_copy(hp_hbm.at[pl.ds(start, SC_CHUNK)], rows_v)
            for k in range(TOP_K):
                pltpu.sync_copy(rows_v, xs_hbm.at[idx_v.at[k, c]])
            return carry

        lax.fori_loop(0, n_ch, body, 0)

    return scatter(hp, pos4)


def _gather_rows(table, idx):
    n = idx.shape[0]
    per_w = n // SC_WORKERS
    n_ch = per_w // SC_CHUNK
    idx3 = idx.reshape(SC_WORKERS, n_ch, SC_CHUNK)

    @functools.partial(
        pl.kernel, mesh=_sc_mesh(),
        out_type=jax.ShapeDtypeStruct((n, PACKED), I32),
        scratch_types=[pltpu.VMEM((n_ch, SC_CHUNK), I32),
                       pltpu.VMEM((SC_CHUNK, PACKED), I32)])
    def gather(table_hbm, idx_hbm, out_hbm, idx_v, rows_v):
        wid = _sc_worker()
        pltpu.sync_copy(idx_hbm.at[wid], idx_v)

        def body(c, carry):
            start = pl.multiple_of(wid * per_w + c * SC_CHUNK, SC_CHUNK)
            pltpu.sync_copy(table_hbm.at[idx_v.at[c]], rows_v)
            pltpu.sync_copy(rows_v, out_hbm.at[pl.ds(start, SC_CHUNK)])
            return carry

        lax.fori_loop(0, n_ch, body, 0)

    return gather(table, idx3)


def _expert_kernel(te_ref, ts_ref, tv_ref, xs_ref, wg_ref, wu_ref, wd_ref, o_ref):
    del te_ref, ts_ref
    n_valid = tv_ref[pl.program_id(0)]

    @pl.when(n_valid > 0)
    def _():
        rid = lax.broadcasted_iota(I32, xs_ref.shape, 0)
        lo, hi = _unpack_bf16_pairs(jnp.where(rid < n_valid, xs_ref[...], 0))
        lo = lo.astype(BF16)
        hi = hi.astype(BF16)
        wg = wg_ref[0]
        wu = wu_ref[0]
        a = (jnp.dot(lo, wg[:PACKED], preferred_element_type=F32)
             + jnp.dot(hi, wg[PACKED:], preferred_element_type=F32))
        b = (jnp.dot(lo, wu[:PACKED], preferred_element_type=F32)
             + jnp.dot(hi, wu[PACKED:], preferred_element_type=F32))
        hmid = (a * jax.nn.sigmoid(a) * b).astype(BF16)
        o_ref[...] = _pack_bf16_pairs(jnp.dot(hmid, wd_ref[0], preferred_element_type=F32))

    @pl.when(n_valid <= 0)
    def _():
        o_ref[...] = jnp.zeros(o_ref.shape, o_ref.dtype)


def _experts(tile_expert, tile_src, tile_valid, xs, wg, wu, wd):
    rows = xs.shape[0]
    grid_spec = pltpu.PrefetchScalarGridSpec(
        num_scalar_prefetch=3,
        grid=(rows // TM_EXP,),
        in_specs=[
            pl.BlockSpec((TM_EXP, PACKED), lambda i, te, ts, tv: (ts[i], 0)),
            pl.BlockSpec((1, D_MODEL, D_FF), lambda i, te, ts, tv: (te[i], 0, 0)),
            pl.BlockSpec((1, D_MODEL, D_FF), lambda i, te, ts, tv: (te[i], 0, 0)),
            pl.BlockSpec((1, D_FF, D_MODEL), lambda i, te, ts, tv: (te[i], 0, 0)),
        ],
        out_specs=pl.BlockSpec((TM_EXP, PACKED), lambda i, te, ts, tv: (i, 0)),
    )
    return pl.pallas_call(
        _expert_kernel,
        grid_spec=grid_spec,
        out_shape=jax.ShapeDtypeStruct((rows, PACKED), I32),
        compiler_params=pltpu.CompilerParams(
            dimension_semantics=("arbitrary",), vmem_limit_bytes=VMEM_LIMIT),
    )(tile_expert, tile_src, tile_valid, xs, wg, wu, wd)


def _combine_kernel(y1_ref, y2_ref, route_ref, xmid_ref, g_ref, o_ref):
    route = route_ref[...]
    lane = lax.broadcasted_iota(I32, route.shape, 1)
    w1 = jnp.sum(jnp.where(lane == 2, route, 0.0), axis=-1, keepdims=True)
    w2 = jnp.sum(jnp.where(lane == 3, route, 0.0), axis=-1, keepdims=True)
    lo1, hi1 = _unpack_bf16_pairs(y1_ref[...])
    lo2, hi2 = _unpack_bf16_pairs(y2_ref[...])
    y = jnp.concatenate([lo1 * w1 + lo2 * w2, hi1 * w1 + hi2 * w2], axis=1)
    o_ref[...] = _rms(xmid_ref[...] + y, g_ref[...])


def _combine(yg, route, xmid, g):
    t = xmid.shape[0]
    n_tiles = t // TM_CMB
    return pl.pallas_call(
        _combine_kernel,
        grid=(n_tiles,),
        in_specs=[
            pl.BlockSpec((TM_CMB, PACKED), lambda i: (i, 0)),
            pl.BlockSpec((TM_CMB, PACKED), lambda i: (n_tiles + i, 0)),
            pl.BlockSpec((TM_CMB, LANES), lambda i: (i, 0)),
            pl.BlockSpec((TM_CMB, D_MODEL), lambda i: (i, 0)),
            pl.BlockSpec((1, D_MODEL), lambda i: (0, 0)),
        ],
        out_specs=pl.BlockSpec((TM_CMB, D_MODEL), lambda i: (i, 0)),
        out_shape=jax.ShapeDtypeStruct((t, D_MODEL), F32),
        compiler_params=pltpu.CompilerParams(
            dimension_semantics=("arbitrary",), vmem_limit_bytes=VMEM_LIMIT),
    )(yg, yg, route, xmid, g)


def _split_bf16(w):
    hi = w.astype(BF16)
    lo = (w - hi.astype(F32)).astype(BF16)
    return hi, lo


def kernel(x, norm_mix, w_in, conv_w, conv_b, w_a_out, sinks, w_b_out, w_o, norm_ffn, w_group,
           b_group, w_expert, b_expert, w_gate, w_up, w_down, norm_final):
    bsz, seq, d = x.shape
    t = bsz * seq
    assert d == D_MODEL and seq % TM_PROJ == 0 and seq % TQ_ATTN == 0 and seq % TM_CONV == 0
    xf = x.reshape(t, d)
    row = lambda v: v.reshape(1, -1)

    w_b, w_c, w_u, w_q, w_k, w_v, w_ga, w_gb = jnp.split(w_in, REF_SPLITS, axis=1)
    w_in_r = jnp.concatenate([w_b, w_c, w_u, w_ga, w_gb, w_q, w_k, w_v], axis=1).astype(BF16)

    proj = _inproj(xf, row(norm_mix), w_in_r)
    za = _conv_mixer(proj, conv_w, row(conv_b), w_a_out.astype(BF16), seq)
    attn = _attention(proj, sinks, seq)

    pad = LANES - N_GROUPS - N_EXPERTS
    w_r = jnp.concatenate([w_group, w_expert, jnp.zeros((d, pad), F32)], axis=1)
    b_r = jnp.concatenate([b_group, b_expert, jnp.zeros((pad,), F32)]).reshape(1, LANES)
    wr_hi, wr_lo = _split_bf16(w_r)
    xmid, h2, route, cnt = _mix(attn, proj, za, xf, w_b_out.astype(BF16), w_o.astype(BF16),
                                row(norm_ffn), wr_hi, wr_lo, b_r)

    n_tiles = t // TM_MIX
    cnt = cnt.reshape(n_tiles, 8, LANES)[:, 0, :N_EXPERTS].astype(I32)
    totals = jnp.sum(cnt, axis=0)
    tiles_e = (totals + TM_EXP - 1) // TM_EXP
    tile_end = jnp.cumsum(tiles_e)
    offset = (tile_end - tiles_e) * TM_EXP
    base = offset[None, :] + jnp.cumsum(cnt, axis=0) - cnt
    base = jnp.pad(base, ((0, 0), (0, LANES - N_EXPERTS))).astype(F32).reshape(n_tiles, 1, LANES)
    rows = t * TOP_K + N_EXPERTS * TM_EXP
    n_active = tile_end[-1]
    tile_id = jnp.arange(rows // TM_EXP, dtype=I32)
    tile_src = jnp.minimum(tile_id, n_active - 1)
    tile_expert = jnp.sum((tile_src[:, None] >= tile_end[None, :]).astype(I32), axis=1)
    tile_expert = jnp.minimum(tile_expert, N_EXPERTS - 1)
    row_in_expert = (tile_id - (tile_end - tiles_e)[tile_expert]) * TM_EXP
    tile_valid = jnp.clip(totals[tile_expert] - row_in_expert, 0, TM_EXP)
    tile_valid = jnp.where(tile_id < n_active, tile_valid, 0).astype(I32)

    pos = _positions(route, base)
    xs = _dispatch(pos, h2, rows)
    ys = _experts(tile_expert.astype(I32), tile_src.astype(I32), tile_valid, xs,
                  w_gate.astype(BF16), w_up.astype(BF16), w_down.astype(BF16))
    yg = _gather_rows(ys, pos.reshape(TOP_K * t))
    out = _combine(yg, route, xmid, row(norm_final))
    return out.reshape(bsz, seq, d)
```

```python
import functools
import math

import jax
import jax.numpy as jnp
from jax import lax
from jax.experimental import pallas as pl
from jax.experimental.pallas import tpu as pltpu
from jax.experimental.pallas import tpu_sc as plsc

F32 = jnp.float32
BF16 = jnp.bfloat16
I32 = jnp.int32

D_MODEL = 1024
HEAD_DIM = 64
N_HEADS = 16
N_KV_HEADS = 4
GROUP = N_HEADS // N_KV_HEADS
KV_WIDTH = N_KV_HEADS * HEAD_DIM
WINDOW = 128
N_GROUPS = 4
EXPERTS_PER_GROUP = 8
N_EXPERTS = N_GROUPS * EXPERTS_PER_GROUP
TOP_K = 2
D_FF = 512
EPS = 1e-6
LANES = 128

REF_SPLITS = (1024, 2048, 3072, 4096, 4352, 4608, 5632)
IN_COLS = 6656
COL_B, COL_C, COL_U, COL_GA, COL_GB, COL_Q = 0, 1, 2, 3, 4, 5
COL_K, COL_V = 24, 25

TM_PROJ = 1024
TN_PROJ = 3328
TM_CONV = 512
TQ_ATTN = 512
TM_MIX = 512
TM_EXP = 512
TM_CMB = TM_MIX
HALO_ROWS = 16
VMEM_LIMIT = 56 * 1024 * 1024
PACKED = D_MODEL // 2

SC_CORES = 2
SC_SUBCORES = 16
SC_WORKERS = SC_CORES * SC_SUBCORES
SC_CHUNK = 64


def _rms(x, g):
    r = lax.rsqrt(jnp.mean(x * x, axis=-1, keepdims=True) + EPS)
    return (x * r) * g


def _pack_bf16_pairs(x):
    n = x.shape[1] // 2
    lo = lax.bitcast_convert_type(x[:, :n].astype(BF16).astype(F32), I32)
    hi = lax.bitcast_convert_type(x[:, n:].astype(BF16).astype(F32), I32)
    return (hi & jnp.int32(-65536)) | lax.shift_right_logical(lo, 16)


def _unpack_bf16_pairs(p):
    lo = lax.bitcast_convert_type(lax.shift_left(p, 16), F32)
    hi = lax.bitcast_convert_type(p & jnp.int32(-65536), F32)
    return lo, hi


def _inproj_kernel(x_ref, g_ref, w_ref, o_ref, h_ref):
    @pl.when(pl.program_id(1) == 0)
    def _():
        h_ref[...] = _rms(x_ref[...], g_ref[...]).astype(BF16)

    o_ref[...] = jnp.dot(h_ref[...], w_ref[...], preferred_element_type=F32).astype(BF16)


def _inproj(xf, g, w):
    t = xf.shape[0]
    return pl.pallas_call(
        _inproj_kernel,
        grid=(t // TM_PROJ, IN_COLS // TN_PROJ),
        in_specs=[
            pl.BlockSpec((TM_PROJ, D_MODEL), lambda i, j: (i, 0)),
            pl.BlockSpec((1, D_MODEL), lambda i, j: (0, 0)),
            pl.BlockSpec((D_MODEL, TN_PROJ), lambda i, j: (0, j)),
        ],
        out_specs=pl.BlockSpec((TM_PROJ, TN_PROJ), lambda i, j: (i, j)),
        out_shape=jax.ShapeDtypeStruct((t, IN_COLS), BF16),
        scratch_shapes=[pltpu.VMEM((TM_PROJ, D_MODEL), BF16)],
        compiler_params=pltpu.CompilerParams(
            dimension_semantics=("arbitrary", "arbitrary"), vmem_limit_bytes=VMEM_LIMIT),
    )(xf, g, w)


def _conv_kernel(b_ref, c_ref, u_ref, ga_ref, cp_ref, up_ref, cw_ref, cb_ref, wa_ref, o_ref,
                 *, tiles_per_seq):
    first = (pl.program_id(0) % tiles_per_seq) == 0
    cu = c_ref[...].astype(F32) * u_ref[...].astype(F32)
    cup = cp_ref[...].astype(F32) * up_ref[...].astype(F32)
    cup = jnp.where(first, 0.0, cup)
    prev1 = cup[HALO_ROWS - 1:HALO_ROWS]
    prev2 = cup[HALO_ROWS - 2:HALO_ROWS - 1]
    row = lax.broadcasted_iota(I32, cu.shape, 0)
    cu1 = jnp.where(row == 0, prev1, pltpu.roll(cu, 1, 0))
    cu2 = jnp.where(row == 0, prev2, jnp.where(row == 1, prev1, pltpu.roll(cu, 2, 0)))
    cw = cw_ref[...]
    y = cw[0:1] * cu2 + cw[1:2] * cu1 + cw[2:3] * cu + cb_ref[...]
    ya = (b_ref[...].astype(F32) * y).astype(BF16)
    z = jnp.dot(ya, wa_ref[...], preferred_element_type=F32)
    o_ref[...] = (jax.nn.sigmoid(ga_ref[...].astype(F32)) * z).astype(BF16)


def _conv_mixer(proj, conv_w, conv_b, wa, seq):
    t = proj.shape[0]
    halo_per_tile = TM_CONV // HALO_ROWS

    def col(c):
        return pl.BlockSpec((TM_CONV, D_MODEL), lambda i: (i, c))

    def halo(c):
        return pl.BlockSpec((HALO_ROWS, D_MODEL),
                            lambda i: (jnp.maximum(i * halo_per_tile - 1, 0), c))

    return pl.pallas_call(
        functools.partial(_conv_kernel, tiles_per_seq=seq // TM_CONV),
        grid=(t // TM_CONV,),
        in_specs=[
            col(COL_B), col(COL_C), col(COL_U), col(COL_GA), halo(COL_C), halo(COL_U),
            pl.BlockSpec((3, D_MODEL), lambda i: (0, 0)),
            pl.BlockSpec((1, D_MODEL), lambda i: (0, 0)),
            pl.BlockSpec((D_MODEL, D_MODEL), lambda i: (0, 0)),
        ],
        out_specs=pl.BlockSpec((TM_CONV, D_MODEL), lambda i: (i, 0)),
        out_shape=jax.ShapeDtypeStruct((t, D_MODEL), BF16),
        compiler_params=pltpu.CompilerParams(
            dimension_semantics=("arbitrary",), vmem_limit_bytes=VMEM_LIMIT),
    )(proj, proj, proj, proj, proj, proj, conv_w, conv_b, wa)


def _attn_kernel(sink_ref, q_ref, k_ref, v_ref, kp_ref, vp_ref, o_ref, *, tiles_per_seq):
    first_tile = (pl.program_id(0) % tiles_per_seq) == 0
    ks = lax.broadcasted_iota(I32, (WINDOW, WINDOW), 0)
    qq = lax.broadcasted_iota(I32, (WINDOW, WINDOW), 1)
    own = ks <= qq
    dist = jnp.where(own, qq - ks, qq - ks + WINDOW).astype(F32)
    visible0 = jnp.logical_or(own, jnp.logical_not(first_tile))
    log2e = math.log2(math.e)
    c_scale = log2e / math.sqrt(HEAD_DIM)
    nt = (((1,), (1,)), ((), ()))
    zk = jnp.zeros((2 * WINDOW, HEAD_DIM), BF16)
    zv = jnp.zeros((HEAD_DIM, 2 * WINDOW), BF16)

    def transposed(v_blk):
        return jnp.transpose(v_blk.astype(F32)).astype(BF16)

    prev_k = kp_ref[...]
    prev_vt = transposed(vp_ref[...])
    for sb in range(TQ_ATTN // WINDOW):
        rows = slice(sb * WINDOW, (sb + 1) * WINDOW)
        cur_k = k_ref[rows, :]
        cur_vt = transposed(v_ref[rows, :])
        out_t = []
        for kh in range(N_KV_HEADS):
            cols = slice(kh * HEAD_DIM, (kh + 1) * HEAD_DIM)
            kcat = jnp.concatenate([prev_k[:, cols], cur_k[:, cols]], axis=0)
            vcat = jnp.concatenate([prev_vt[cols, :], cur_vt[cols, :]], axis=1)
            qg = jnp.concatenate([q_ref[rows, (2 * kh) * LANES:(2 * kh + 1) * LANES],
                                  q_ref[rows, (2 * kh + 1) * LANES:(2 * kh + 2) * LANES]], axis=0)
            probs = [[None, None], [None, None]]
            for pos in range(2):
                k_pad = jnp.concatenate([kcat, zk] if pos == 0 else [zk, kcat], axis=1)
                st = lax.dot_general(k_pad, qg, nt, preferred_element_type=F32)
                for half in range(2):
                    h = kh * GROUP + 2 * half + pos
                    slope = 2.0 ** (-8.0 * (h + 1) / N_HEADS)
                    qcols = slice(half * WINDOW, (half + 1) * WINDOW)
                    s = (jnp.where(own, st[WINDOW:, qcols], st[:WINDOW, qcols]) * c_scale
                         - (slope * log2e) * dist)
                    if sb == 0:
                        s = jnp.where(visible0, s, -jnp.inf)
                    m = jnp.max(s, axis=0, keepdims=True)
                    p = jnp.exp2(s - m)
                    den = jnp.sum(p, axis=0, keepdims=True) + jnp.exp2(sink_ref[h] * log2e - m)
                    pn = p * (1.0 / den)
                    probs[pos][half] = jnp.concatenate(
                        [jnp.where(own, 0.0, pn).astype(BF16), jnp.where(own, pn, 0.0).astype(BF16)],
                        axis=0)
            v_lo = jnp.concatenate([vcat, zv], axis=0)
            v_hi = jnp.concatenate([zv, vcat], axis=0)
            for half in range(2):
                out_t.append(jnp.dot(v_lo, probs[0][half], preferred_element_type=F32)
                             + jnp.dot(v_hi, probs[1][half], preferred_element_type=F32))
        o_ref[rows, :] = jnp.transpose(jnp.concatenate(out_t, axis=0)).astype(BF16)
        prev_k, prev_vt = cur_k, cur_vt


def _attention(proj, sinks, seq):
    t = proj.shape[0]
    sub = TQ_ATTN // WINDOW
    return pl.pallas_call(
        functools.partial(_attn_kernel, tiles_per_seq=seq // TQ_ATTN),
        grid=(t // TQ_ATTN,),
        in_specs=[
            pl.BlockSpec(memory_space=pltpu.SMEM),
            pl.BlockSpec((TQ_ATTN, D_MODEL), lambda i: (i, COL_Q)),
            pl.BlockSpec((TQ_ATTN, KV_WIDTH), lambda i: (i, COL_K)),
            pl.BlockSpec((TQ_ATTN, KV_WIDTH), lambda i: (i, COL_V)),
            pl.BlockSpec((WINDOW, KV_WIDTH), lambda i: (jnp.maximum(i * sub - 1, 0), COL_K)),
            pl.BlockSpec((WINDOW, KV_WIDTH), lambda i: (jnp.maximum(i * sub - 1, 0), COL_V)),
        ],
        out_specs=pl.BlockSpec((TQ_ATTN, D_MODEL), lambda i: (i, 0)),
        out_shape=jax.ShapeDtypeStruct((t, D_MODEL), BF16),
        compiler_params=pltpu.CompilerParams(
            dimension_semantics=("arbitrary",), vmem_limit_bytes=VMEM_LIMIT),
    )(sinks, proj, proj, proj, proj, proj)


def _mix_kernel(attn_ref, gb_ref, za_ref, x_ref, wb_ref, wo_ref, g_ref, wrh_ref, wrl_ref, br_ref,
                xmid_ref, h_ref, route_ref, cnt_ref):
    yb = jnp.dot(attn_ref[...], wb_ref[...], preferred_element_type=F32)
    merged = za_ref[...].astype(F32) + jax.nn.sigmoid(gb_ref[...].astype(F32)) * yb
    xm = x_ref[...] + jnp.dot(merged.astype(BF16), wo_ref[...], preferred_element_type=F32)
    xmid_ref[...] = xm
    h = _rms(xm, g_ref[...])
    h_ref[...] = _pack_bf16_pairs(h)
    h_hi = h.astype(BF16)
    h_lo = (h - h_hi.astype(F32)).astype(BF16)
    wrh = wrh_ref[...]
    logits = (jnp.dot(h_hi, wrh, preferred_element_type=F32)
              + jnp.dot(h_lo, wrh, preferred_element_type=F32)
              + jnp.dot(h_hi, wrl_ref[...], preferred_element_type=F32)) + br_ref[...]
    lane = lax.broadcasted_iota(I32, logits.shape, 1)
    neg = -jnp.inf
    gl = jnp.where(lane < N_GROUPS, logits, neg)
    gmax = jnp.max(gl, axis=-1, keepdims=True)
    g_idx = jnp.min(jnp.where(gl == gmax, lane, LANES), axis=-1, keepdims=True)
    p_g = 1.0 / jnp.sum(jnp.exp(gl - gmax), axis=-1, keepdims=True)
    start = N_GROUPS + EXPERTS_PER_GROUP * g_idx
    el = jnp.where((lane >= start) & (lane < start + EXPERTS_PER_GROUP), logits, neg)
    v1 = jnp.max(el, axis=-1, keepdims=True)
    i1 = jnp.min(jnp.where(el == v1, lane, LANES), axis=-1, keepdims=True)
    el2 = jnp.where(lane == i1, neg, el)
    v2 = jnp.max(el2, axis=-1, keepdims=True)
    i2 = jnp.min(jnp.where(el2 == v2, lane, LANES), axis=-1, keepdims=True)
    e21 = jnp.exp(v2 - v1)
    w1 = p_g / (1.0 + e21)
    w2 = p_g * e21 / (1.0 + e21)
    e1 = i1 - N_GROUPS
    e2 = i2 - N_GROUPS
    route = jnp.where(lane == 0, e1.astype(F32),
                      jnp.where(lane == 1, e2.astype(F32),
                                jnp.where(lane == 2, w1, jnp.where(lane == 3, w2, 0.0))))
    route_ref[...] = route
    onehot = ((lane == e1) | (lane == e2)).astype(F32)
    cnt = jnp.sum(onehot, axis=0, keepdims=True)
    cnt_ref[...] = jnp.broadcast_to(cnt, cnt_ref.shape)


def _mix(attn, proj, za, xf, wb, wo, g, wr_hi, wr_lo, br):
    t = xf.shape[0]
    n_tiles = t // TM_MIX
    full = lambda shape: pl.BlockSpec(shape, lambda i: (0, 0))
    tile = lambda w=D_MODEL: pl.BlockSpec((TM_MIX, w), lambda i: (i, 0))
    return pl.pallas_call(
        _mix_kernel,
        grid=(n_tiles,),
        in_specs=[
            tile(),
            pl.BlockSpec((TM_MIX, D_MODEL), lambda i: (i, COL_GB)),
            tile(), tile(),
            full((D_MODEL, D_MODEL)), full((D_MODEL, D_MODEL)), full((1, D_MODEL)),
            full((D_MODEL, LANES)), full((D_MODEL, LANES)), full((1, LANES)),
        ],
        out_specs=[tile(), tile(PACKED), tile(LANES), pl.BlockSpec((8, LANES), lambda i: (i, 0))],
        out_shape=[
            jax.ShapeDtypeStruct((t, D_MODEL), F32),
            jax.ShapeDtypeStruct((t, PACKED), I32),
            jax.ShapeDtypeStruct((t, LANES), F32),
            jax.ShapeDtypeStruct((n_tiles * 8, LANES), F32),
        ],
        compiler_params=pltpu.CompilerParams(
            dimension_semantics=("arbitrary",), vmem_limit_bytes=VMEM_LIMIT),
    )(attn, proj, za, xf, wb, wo, g, wr_hi, wr_lo, br)


def _pos_kernel(route_ref, base_ref, pos_ref):
    route = route_ref[...]
    lane = lax.broadcasted_iota(I32, route.shape, 1)
    e1 = jnp.sum(jnp.where(lane == 0, route, 0.0), axis=-1, keepdims=True).astype(I32)
    e2 = jnp.sum(jnp.where(lane == 1, route, 0.0), axis=-1, keepdims=True).astype(I32)
    onehot = ((lane == e1) | (lane == e2)).astype(BF16)
    r = lax.broadcasted_iota(I32, (TM_MIX, TM_MIX), 0)
    c = lax.broadcasted_iota(I32, (TM_MIX, TM_MIX), 1)
    lower = (c < r).astype(BF16)
    before = jnp.dot(lower, onehot, preferred_element_type=F32) + base_ref[0]
    p1 = jnp.sum(jnp.where(lane == e1, before, 0.0), axis=-1, keepdims=True)
    p2 = jnp.sum(jnp.where(lane == e2, before, 0.0), axis=-1, keepdims=True)
    packed = jnp.where(lane == 0, p1, jnp.where(lane == 1, p2, 0.0))
    pos_ref[...] = jnp.transpose(packed)[0:TOP_K, :].astype(I32)


def _positions(route, base):
    t = route.shape[0]
    n_tiles = t // TM_MIX
    return pl.pallas_call(
        _pos_kernel,
        grid=(n_tiles,),
        in_specs=[
            pl.BlockSpec((TM_MIX, LANES), lambda i: (i, 0)),
            pl.BlockSpec((1, 1, LANES), lambda i: (i, 0, 0)),
        ],
        out_specs=pl.BlockSpec((TOP_K, TM_MIX), lambda i: (0, i)),
        out_shape=jax.ShapeDtypeStruct((TOP_K, t), I32),
        compiler_params=pltpu.CompilerParams(dimension_semantics=("arbitrary",)),
    )(route, base)


def _sc_mesh():
    return plsc.VectorSubcoreMesh(core_axis_name="c", subcore_axis_name="s",
                                  num_cores=SC_CORES, num_subcores=SC_SUBCORES)


def _sc_worker():
    return lax.axis_index("s") * SC_CORES + lax.axis_index("c")


def _dispatch(pos, hp, rows):
    t = hp.shape[0]
    per_w = t // SC_WORKERS
    n_ch = per_w // SC_CHUNK
    pos4 = pos.reshape(TOP_K, SC_WORKERS, n_ch, SC_CHUNK)

    @functools.partial(
        pl.kernel, mesh=_sc_mesh(),
        out_type=jax.ShapeDtypeStruct((rows, PACKED), I32),
        scratch_types=[pltpu.VMEM((TOP_K, n_ch, SC_CHUNK), I32),
                       pltpu.VMEM((SC_CHUNK, PACKED), I32)])
    def scatter(hp_hbm, pos_hbm, xs_hbm, idx_v, rows_v):
        wid = _sc_worker()
        for k in range(TOP_K):
            pltpu.sync_copy(pos_hbm.at[k, wid], idx_v.at[k])

        def body(c, carry):
            start = pl.multiple_of(wid * per_w + c * SC_CHUNK, SC_CHUNK)
            pltpu.sync_copy(hp_hbm.at[pl.ds(start, SC_CHUNK)], rows_v)
            for k in range(TOP_K):
                pltpu.sync_copy(rows_v, xs_hbm.at[idx_v.at[k, c]])
            return carry

        lax.fori_loop(0, n_ch, body, 0)

    return scatter(hp, pos4)


def _gather_rows(table, idx):
    n = idx.shape[0]
    per_w = n // SC_WORKERS
    n_ch = per_w // SC_CHUNK
    idx3 = idx.reshape(SC_WORKERS, n_ch, SC_CHUNK)

    @functools.partial(
        pl.kernel, mesh=_sc_mesh(),
        out_type=jax.ShapeDtypeStruct((n, PACKED), I32),
        scratch_types=[pltpu.VMEM((n_ch, SC_CHUNK), I32),
                       pltpu.VMEM((SC_CHUNK, PACKED), I32)])
    def gather(table_hbm, idx_hbm, out_hbm, idx_v, rows_v):
        wid = _sc_worker()
        pltpu.sync_copy(idx_hbm.at[wid], idx_v)

        def body(c, carry):
            start = pl.multiple_of(wid * per_w + c * SC_CHUNK, SC_CHUNK)
            pltpu.sync_copy(table_hbm.at[idx_v.at[c]], rows_v)
            pltpu.sync_copy(rows_v, out_hbm.at[pl.ds(start, SC_CHUNK)])
            return carry

        lax.fori_loop(0, n_ch, body, 0)

    return gather(table, idx3)


def _expert_kernel(te_ref, ts_ref, tv_ref, xs_ref, wg_ref, wu_ref, wd_ref, o_ref):
    del te_ref, ts_ref
    n_valid = tv_ref[pl.program_id(0)]

    @pl.when(n_valid > 0)
    def _():
        rid = lax.broadcasted_iota(I32, xs_ref.shape, 0)
        lo, hi = _unpack_bf16_pairs(jnp.where(rid < n_valid, xs_ref[...], 0))
        lo = lo.astype(BF16)
        hi = hi.astype(BF16)
        wg = wg_ref[0]
        wu = wu_ref[0]
        a = (jnp.dot(lo, wg[:PACKED], preferred_element_type=F32)
             + jnp.dot(hi, wg[PACKED:], preferred_element_type=F32))
        b = (jnp.dot(lo, wu[:PACKED], preferred_element_type=F32)
             + jnp.dot(hi, wu[PACKED:], preferred_element_type=F32))
        hmid = (a * jax.nn.sigmoid(a) * b).astype(BF16)
        o_ref[...] = _pack_bf16_pairs(jnp.dot(hmid, wd_ref[0], preferred_element_type=F32))

    @pl.when(n_valid <= 0)
    def _():
        o_ref[...] = jnp.zeros(o_ref.shape, o_ref.dtype)


def _experts(tile_expert, tile_src, tile_valid, xs, wg, wu, wd):
    rows = xs.shape[0]
    grid_spec = pltpu.PrefetchScalarGridSpec(
        num_scalar_prefetch=3,
        grid=(rows // TM_EXP,),
        in_specs=[
            pl.BlockSpec((TM_EXP, PACKED), lambda i, te, ts, tv: (ts[i], 0)),
            pl.BlockSpec((1, D_MODEL, D_FF), lambda i, te, ts, tv: (te[i], 0, 0)),
            pl.BlockSpec((1, D_MODEL, D_FF), lambda i, te, ts, tv: (te[i], 0, 0)),
            pl.BlockSpec((1, D_FF, D_MODEL), lambda i, te, ts, tv: (te[i], 0, 0)),
        ],
        out_specs=pl.BlockSpec((TM_EXP, PACKED), lambda i, te, ts, tv: (i, 0)),
    )
    return pl.pallas_call(
        _expert_kernel,
        grid_spec=grid_spec,
        out_shape=jax.ShapeDtypeStruct((rows, PACKED), I32),
        compiler_params=pltpu.CompilerParams(
            dimension_semantics=("arbitrary",), vmem_limit_bytes=VMEM_LIMIT),
    )(tile_expert, tile_src, tile_valid, xs, wg, wu, wd)


def _combine_kernel(y1_ref, y2_ref, route_ref, xmid_ref, g_ref, o_ref):
    route = route_ref[...]
    lane = lax.broadcasted_iota(I32, route.shape, 1)
    w1 = jnp.sum(jnp.where(lane == 2, route, 0.0), axis=-1, keepdims=True)
    w2 = jnp.sum(jnp.where(lane == 3, route, 0.0), axis=-1, keepdims=True)
    lo1, hi1 = _unpack_bf16_pairs(y1_ref[...])
    lo2, hi2 = _unpack_bf16_pairs(y2_ref[...])
    y = jnp.concatenate([lo1 * w1 + lo2 * w2, hi1 * w1 + hi2 * w2], axis=1)
    o_ref[...] = _rms(xmid_ref[...] + y, g_ref[...])


def _combine(yg, route, xmid, g):
    t = xmid.shape[0]
    n_tiles = t // TM_CMB
    return pl.pallas_call(
        _combine_kernel,
        grid=(n_tiles,),
        in_specs=[
            pl.BlockSpec((TM_CMB, PACKED), lambda i: (i, 0)),
            pl.BlockSpec((TM_CMB, PACKED), lambda i: (n_tiles + i, 0)),
            pl.BlockSpec((TM_CMB, LANES), lambda i: (i, 0)),
            pl.BlockSpec((TM_CMB, D_MODEL), lambda i: (i, 0)),
            pl.BlockSpec((1, D_MODEL), lambda i: (0, 0)),
        ],
        out_specs=pl.BlockSpec((TM_CMB, D_MODEL), lambda i: (i, 0)),
        out_shape=jax.ShapeDtypeStruct((t, D_MODEL), F32),
        compiler_params=pltpu.CompilerParams(
            dimension_semantics=("arbitrary",), vmem_limit_bytes=VMEM_LIMIT),
    )(yg, yg, route, xmid, g)


def _split_bf16(w):
    hi = w.astype(BF16)
    lo = (w - hi.astype(F32)).astype(BF16)
    return hi, lo


def kernel(x, norm_mix, w_in, conv_w, conv_b, w_a_out, sinks, w_b_out, w_o, norm_ffn, w_group,
           b_group, w_expert, b_expert, w_gate, w_up, w_down, norm_final):
    bsz, seq, d = x.shape
    t = bsz * seq
    assert d == D_MODEL and seq % TM_PROJ == 0 and seq % TQ_ATTN == 0 and seq % TM_CONV == 0
    xf = x.reshape(t, d)
    row = lambda v: v.reshape(1, -1)

    w_b, w_c, w_u, w_q, w_k, w_v, w_ga, w_gb = jnp.split(w_in, REF_SPLITS, axis=1)
    w_in_r = jnp.concatenate([w_b, w_c, w_u, w_ga, w_gb, w_q, w_k, w_v], axis=1).astype(BF16)

    proj = _inproj(xf, row(norm_mix), w_in_r)
    za = _conv_mixer(proj, conv_w, row(conv_b), w_a_out.astype(BF16), seq)
    attn = _attention(proj, sinks, seq)

    pad = LANES - N_GROUPS - N_EXPERTS
    w_r = jnp.concatenate([w_group, w_expert, jnp.zeros((d, pad), F32)], axis=1)
    b_r = jnp.concatenate([b_group, b_expert, jnp.zeros((pad,), F32)]).reshape(1, LANES)
    wr_hi, wr_lo = _split_bf16(w_r)
    xmid, h2, route, cnt = _mix(attn, proj, za, xf, w_b_out.astype(BF16), w_o.astype(BF16),
                                row(norm_ffn), wr_hi, wr_lo, b_r)

    n_tiles = t // TM_MIX
    cnt = cnt.reshape(n_tiles, 8, LANES)[:, 0, :N_EXPERTS].astype(I32)
    totals = jnp.sum(cnt, axis=0)
    tiles_e = (totals + TM_EXP - 1) // TM_EXP
    tile_end = jnp.cumsum(tiles_e)
    offset = (tile_end - tiles_e) * TM_EXP
    base = offset[None, :] + jnp.cumsum(cnt, axis=0) - cnt
    base = jnp.pad(base, ((0, 0), (0, LANES - N_EXPERTS))).astype(F32).reshape(n_tiles, 1, LANES)
    rows = t * TOP_K + N_EXPERTS * TM_EXP
    n_active = tile_end[-1]
    tile_id = jnp.arange(rows // TM_EXP, dtype=I32)
    tile_src = jnp.minimum(tile_id, n_active - 1)
    tile_expert = jnp.sum((tile_src[:, None] >= tile_end[None, :]).astype(I32), axis=1)
    tile_expert = jnp.minimum(tile_expert, N_EXPERTS - 1)
    row_in_expert = (tile_id - (tile_end - tiles_e)[tile_expert]) * TM_EXP
    tile_valid = jnp.clip(totals[tile_expert] - row_in_expert, 0, TM_EXP)
    tile_valid = jnp.where(tile_id < n_active, tile_valid, 0).astype(I32)

    pos = _positions(route, base)
    xs = _dispatch(pos, h2, rows)
    ys = _experts(tile_expert.astype(I32), tile_src.astype(I32), tile_valid, xs,
                  w_gate.astype(BF16), w_up.astype(BF16), w_down.astype(BF16))
    yg = _gather_rows(ys, pos.reshape(TOP_K * t))
    out = _combine(yg, route, xmid, row(norm_final))
    return out.reshape(bsz, seq, d)
```

```python
import functools
import math

import jax
import jax.numpy as jnp
from jax import lax
from jax.experimental import pallas as pl
from jax.experimental.pallas import tpu as pltpu
from jax.experimental.pallas import tpu_sc as plsc

F32 = jnp.float32
BF16 = jnp.bfloat16
I32 = jnp.int32

D_MODEL = 1024
HEAD_DIM = 64
N_HEADS = 16
N_KV_HEADS = 4
GROUP = N_HEADS // N_KV_HEADS
KV_WIDTH = N_KV_HEADS * HEAD_DIM
WINDOW = 128
N_GROUPS = 4
EXPERTS_PER_GROUP = 8
N_EXPERTS = N_GROUPS * EXPERTS_PER_GROUP
TOP_K = 2
D_FF = 512
EPS = 1e-6
LANES = 128

REF_SPLITS = (1024, 2048, 3072, 4096, 4352, 4608, 5632)
IN_COLS = 6656
COL_B, COL_C, COL_U, COL_GA, COL_GB, COL_Q = 0, 1, 2, 3, 4, 5
COL_K, COL_V = 24, 25

TM_PROJ = 1024
TN_PROJ = 3328
TM_CONV = 512
TQ_ATTN = 512
TM_MIX = 512
SUB_MIX = 256
MOE_CHUNKS = 2
TM_EXP = 512
TM_CMB = TM_MIX
HALO_ROWS = 16
VMEM_LIMIT = 56 * 1024 * 1024
PACKED = D_MODEL // 2

SC_CORES = 2
SC_SUBCORES = 16
SC_WORKERS = SC_CORES * SC_SUBCORES
SC_CHUNK = 64


def _rms(x, g):
    r = lax.rsqrt(jnp.mean(x * x, axis=-1, keepdims=True) + EPS)
    return (x * r) * g


def _pack_bf16_pairs(x):
    n = x.shape[1] // 2
    lo = lax.bitcast_convert_type(x[:, :n].astype(BF16).astype(F32), I32)
    hi = lax.bitcast_convert_type(x[:, n:].astype(BF16).astype(F32), I32)
    return (hi & jnp.int32(-65536)) | lax.shift_right_logical(lo, 16)


def _unpack_bf16_pairs(p):
    lo = lax.bitcast_convert_type(lax.shift_left(p, 16), F32)
    hi = lax.bitcast_convert_type(p & jnp.int32(-65536), F32)
    return lo, hi


def _inproj_kernel(x_ref, g_ref, w_ref, o_ref, h_ref):
    @pl.when(pl.program_id(1) == 0)
    def _():
        h_ref[...] = _rms(x_ref[...], g_ref[...]).astype(BF16)

    o_ref[...] = jnp.dot(h_ref[...], w_ref[...], preferred_element_type=F32).astype(BF16)


def _inproj(xf, g, w):
    t = xf.shape[0]
    return pl.pallas_call(
        _inproj_kernel,
        name="inproj",
        grid=(t // TM_PROJ, IN_COLS // TN_PROJ),
        in_specs=[
            pl.BlockSpec((TM_PROJ, D_MODEL), lambda i, j: (i, 0)),
            pl.BlockSpec((1, D_MODEL), lambda i, j: (0, 0)),
            pl.BlockSpec((D_MODEL, TN_PROJ), lambda i, j: (0, j)),
        ],
        out_specs=pl.BlockSpec((TM_PROJ, TN_PROJ), lambda i, j: (i, j)),
        out_shape=jax.ShapeDtypeStruct((t, IN_COLS), BF16),
        scratch_shapes=[pltpu.VMEM((TM_PROJ, D_MODEL), BF16)],
        compiler_params=pltpu.CompilerParams(
            dimension_semantics=("arbitrary", "arbitrary"), vmem_limit_bytes=VMEM_LIMIT),
    )(xf, g, w)


def _conv_kernel(b_ref, c_ref, u_ref, ga_ref, cp_ref, up_ref, cw_ref, cb_ref, wa_ref, o_ref,
                 *, tiles_per_seq):
    first = (pl.program_id(0) % tiles_per_seq) == 0
    cu = c_ref[...].astype(F32) * u_ref[...].astype(F32)
    cup = cp_ref[...].astype(F32) * up_ref[...].astype(F32)
    cup = jnp.where(first, 0.0, cup)
    prev1 = cup[HALO_ROWS - 1:HALO_ROWS]
    prev2 = cup[HALO_ROWS - 2:HALO_ROWS - 1]
    row = lax.broadcasted_iota(I32, cu.shape, 0)
    cu1 = jnp.where(row == 0, prev1, pltpu.roll(cu, 1, 0))
    cu2 = jnp.where(row == 0, prev2, jnp.where(row == 1, prev1, pltpu.roll(cu, 2, 0)))
    cw = cw_ref[...]
    y = cw[0:1] * cu2 + cw[1:2] * cu1 + cw[2:3] * cu + cb_ref[...]
    ya = (b_ref[...].astype(F32) * y).astype(BF16)
    z = jnp.dot(ya, wa_ref[...], preferred_element_type=F32)
    o_ref[...] = (jax.nn.sigmoid(ga_ref[...].astype(F32)) * z).astype(BF16)


def _conv_mixer(proj, conv_w, conv_b, wa, seq):
    t = proj.shape[0]
    halo_per_tile = TM_CONV // HALO_ROWS

    def col(c):
        return pl.BlockSpec((TM_CONV, D_MODEL), lambda i: (i, c))

    def halo(c):
        return pl.BlockSpec((HALO_ROWS, D_MODEL),
                            lambda i: (jnp.maximum(i * halo_per_tile - 1, 0), c))

    return pl.pallas_call(
        functools.partial(_conv_kernel, tiles_per_seq=seq // TM_CONV),
        name="convmix",
        grid=(t // TM_CONV,),
        in_specs=[
            col(COL_B), col(COL_C), col(COL_U), col(COL_GA), halo(COL_C), halo(COL_U),
            pl.BlockSpec((3, D_MODEL), lambda i: (0, 0)),
            pl.BlockSpec((1, D_MODEL), lambda i: (0, 0)),
            pl.BlockSpec((D_MODEL, D_MODEL), lambda i: (0, 0)),
        ],
        out_specs=pl.BlockSpec((TM_CONV, D_MODEL), lambda i: (i, 0)),
        out_shape=jax.ShapeDtypeStruct((t, D_MODEL), BF16),
        compiler_params=pltpu.CompilerParams(
            dimension_semantics=("arbitrary",), vmem_limit_bytes=VMEM_LIMIT),
    )(proj, proj, proj, proj, proj, proj, conv_w, conv_b, wa)


def _attn_kernel(sink_ref, q_ref, k_ref, v_ref, kp_ref, vp_ref, o_ref, *, tiles_per_seq):
    first_tile = (pl.program_id(0) % tiles_per_seq) == 0
    ks = lax.broadcasted_iota(I32, (WINDOW, WINDOW), 0)
    qq = lax.broadcasted_iota(I32, (WINDOW, WINDOW), 1)
    own = ks <= qq
    dist = jnp.where(own, qq - ks, qq - ks + WINDOW).astype(F32)
    visible0 = jnp.logical_or(own, jnp.logical_not(first_tile))
    log2e = math.log2(math.e)
    c_scale = log2e / math.sqrt(HEAD_DIM)
    nt = (((1,), (1,)), ((), ()))
    zk = jnp.zeros((2 * WINDOW, HEAD_DIM), BF16)
    zv = jnp.zeros((HEAD_DIM, 2 * WINDOW), BF16)

    def transposed(v_blk):
        return jnp.transpose(v_blk.astype(F32)).astype(BF16)

    prev_k = kp_ref[...]
    prev_vt = transposed(vp_ref[...])
    for sb in range(TQ_ATTN // WINDOW):
        rows = slice(sb * WINDOW, (sb + 1) * WINDOW)
        cur_k = k_ref[rows, :]
        cur_vt = transposed(v_ref[rows, :])
        out_t = []
        for kh in range(N_KV_HEADS):
            cols = slice(kh * HEAD_DIM, (kh + 1) * HEAD_DIM)
            kcat = jnp.concatenate([prev_k[:, cols], cur_k[:, cols]], axis=0)
            vcat = jnp.concatenate([prev_vt[cols, :], cur_vt[cols, :]], axis=1)
            qg = jnp.concatenate([q_ref[rows, (2 * kh) * LANES:(2 * kh + 1) * LANES],
                                  q_ref[rows, (2 * kh + 1) * LANES:(2 * kh + 2) * LANES]], axis=0)
            probs = [[None, None], [None, None]]
            for pos in range(2):
                k_pad = jnp.concatenate([kcat, zk] if pos == 0 else [zk, kcat], axis=1)
                st = lax.dot_general(k_pad, qg, nt, preferred_element_type=F32)
                for half in range(2):
                    h = kh * GROUP + 2 * half + pos
                    slope = 2.0 ** (-8.0 * (h + 1) / N_HEADS)
                    qcols = slice(half * WINDOW, (half + 1) * WINDOW)
                    s = (jnp.where(own, st[WINDOW:, qcols], st[:WINDOW, qcols]) * c_scale
                         - (slope * log2e) * dist)
                    if sb == 0:
                        s = jnp.where(visible0, s, -jnp.inf)
                    m = jnp.max(s, axis=0, keepdims=True)
                    p = jnp.exp2(s - m)
                    den = jnp.sum(p, axis=0, keepdims=True) + jnp.exp2(sink_ref[h] * log2e - m)
                    pn = p * (1.0 / den)
                    probs[pos][half] = jnp.concatenate(
                        [jnp.where(own, 0.0, pn).astype(BF16), jnp.where(own, pn, 0.0).astype(BF16)],
                        axis=0)
            v_lo = jnp.concatenate([vcat, zv], axis=0)
            v_hi = jnp.concatenate([zv, vcat], axis=0)
            for half in range(2):
                out_t.append(jnp.dot(v_lo, probs[0][half], preferred_element_type=F32)
                             + jnp.dot(v_hi, probs[1][half], preferred_element_type=F32))
        o_ref[rows, :] = jnp.transpose(jnp.concatenate(out_t, axis=0)).astype(BF16)
        prev_k, prev_vt = cur_k, cur_vt


def _attention(proj, sinks, seq):
    t = proj.shape[0]
    sub = TQ_ATTN // WINDOW
    return pl.pallas_call(
        functools.partial(_attn_kernel, tiles_per_seq=seq // TQ_ATTN),
        name="swattn",
        grid=(t // TQ_ATTN,),
        in_specs=[
            pl.BlockSpec(memory_space=pltpu.SMEM),
            pl.BlockSpec((TQ_ATTN, D_MODEL), lambda i: (i, COL_Q)),
            pl.BlockSpec((TQ_ATTN, KV_WIDTH), lambda i: (i, COL_K)),
            pl.BlockSpec((TQ_ATTN, KV_WIDTH), lambda i: (i, COL_V)),
            pl.BlockSpec((WINDOW, KV_WIDTH), lambda i: (jnp.maximum(i * sub - 1, 0), COL_K)),
            pl.BlockSpec((WINDOW, KV_WIDTH), lambda i: (jnp.maximum(i * sub - 1, 0), COL_V)),
        ],
        out_specs=pl.BlockSpec((TQ_ATTN, D_MODEL), lambda i: (i, 0)),
        out_shape=jax.ShapeDtypeStruct((t, D_MODEL), BF16),
        compiler_params=pltpu.CompilerParams(
            dimension_semantics=("arbitrary",), vmem_limit_bytes=VMEM_LIMIT),
    )(sinks, proj, proj, proj, proj, proj)


def _mix_kernel(attn_ref, gb_ref, za_ref, x_ref, wb_ref, wo_ref, g_ref, wr_ref, br_ref,
                xmid_ref, h_ref, route_ref, cnt_ref):
    cnt = jnp.zeros((1, LANES), F32)
    for sub in range(TM_MIX // SUB_MIX):
        rows = slice(sub * SUB_MIX, (sub + 1) * SUB_MIX)
        cnt = cnt + _mix_rows(rows, attn_ref, gb_ref, za_ref, x_ref, wb_ref, wo_ref, g_ref, wr_ref,
                              br_ref, xmid_ref, h_ref, route_ref)
    cnt_ref[...] = jnp.broadcast_to(cnt, cnt_ref.shape)


def _mix_rows(rows, attn_ref, gb_ref, za_ref, x_ref, wb_ref, wo_ref, g_ref, wr_ref, br_ref,
              xmid_ref, h_ref, route_ref):
    yb = jnp.dot(attn_ref[rows, :], wb_ref[...], preferred_element_type=F32)
    merged = za_ref[rows, :].astype(F32) + jax.nn.sigmoid(gb_ref[rows, :].astype(F32)) * yb
    xm = x_ref[rows, :] + jnp.dot(merged.astype(BF16), wo_ref[...], preferred_element_type=F32)
    xmid_ref[rows, :] = xm
    h = _rms(xm, g_ref[...])
    h_ref[rows, :] = _pack_bf16_pairs(h)
    h_hi = h.astype(BF16)
    h_lo = (h - h_hi.astype(F32)).astype(BF16)
    wr = wr_ref[...]
    both = jnp.dot(h_hi, wr, preferred_element_type=F32)
    logits = (both[:, :LANES] + both[:, LANES:]
              + jnp.dot(h_lo, wr[:, :LANES], preferred_element_type=F32)) + br_ref[...]
    lane = lax.broadcasted_iota(I32, logits.shape, 1)
    neg = -jnp.inf
    gl = jnp.where(lane < N_GROUPS, logits, neg)
    gmax = jnp.max(gl, axis=-1, keepdims=True)
    g_idx = jnp.min(jnp.where(gl == gmax, lane, LANES), axis=-1, keepdims=True)
    p_g = 1.0 / jnp.sum(jnp.exp(gl - gmax), axis=-1, keepdims=True)
    start = N_GROUPS + EXPERTS_PER_GROUP * g_idx
    el = jnp.where((lane >= start) & (lane < start + EXPERTS_PER_GROUP), logits, neg)
    v1 = jnp.max(el, axis=-1, keepdims=True)
    i1 = jnp.min(jnp.where(el == v1, lane, LANES), axis=-1, keepdims=True)
    el2 = jnp.where(lane == i1, neg, el)
    v2 = jnp.max(el2, axis=-1, keepdims=True)
    i2 = jnp.min(jnp.where(el2 == v2, lane, LANES), axis=-1, keepdims=True)
    e21 = jnp.exp(v2 - v1)
    w1 = p_g / (1.0 + e21)
    w2 = p_g * e21 / (1.0 + e21)
    e1 = i1 - N_GROUPS
    e2 = i2 - N_GROUPS
    route = jnp.where(lane == 0, e1.astype(F32),
                      jnp.where(lane == 1, e2.astype(F32),
                                jnp.where(lane == 2, w1, jnp.where(lane == 3, w2, 0.0))))
    route_ref[rows, :] = route
    onehot = ((lane == e1) | (lane == e2)).astype(F32)
    return jnp.sum(onehot, axis=0, keepdims=True)


def _mix(attn, proj, za, xf, wb, wo, g, wr, br, chunk, t):
    n_tiles = t // TM_MIX
    first = chunk * n_tiles
    full = lambda shape: pl.BlockSpec(shape, lambda i: (0, 0))
    tile = lambda w=D_MODEL: pl.BlockSpec((TM_MIX, w), lambda i: (i, 0))
    src = lambda c=0: pl.BlockSpec((TM_MIX, D_MODEL), lambda i: (first + i, c))
    return pl.pallas_call(
        _mix_kernel,
        name="merge_router",
        grid=(n_tiles,),
        in_specs=[
            src(), src(COL_GB), src(), src(),
            full((D_MODEL, D_MODEL)), full((D_MODEL, D_MODEL)), full((1, D_MODEL)),
            full((D_MODEL, 2 * LANES)), full((1, LANES)),
        ],
        out_specs=[tile(), tile(PACKED), tile(LANES), pl.BlockSpec((8, LANES), lambda i: (i, 0))],
        out_shape=[
            jax.ShapeDtypeStruct((t, D_MODEL), F32),
            jax.ShapeDtypeStruct((t, PACKED), I32),
            jax.ShapeDtypeStruct((t, LANES), F32),
            jax.ShapeDtypeStruct((n_tiles * 8, LANES), F32),
        ],
        compiler_params=pltpu.CompilerParams(
            dimension_semantics=("arbitrary",), vmem_limit_bytes=VMEM_LIMIT),
    )(attn, proj, za, xf, wb, wo, g, wr, br)


def _pos_kernel(route_ref, base_ref, pos_ref):
    route = route_ref[...]
    lane = lax.broadcasted_iota(I32, route.shape, 1)
    e1 = jnp.sum(jnp.where(lane == 0, route, 0.0), axis=-1, keepdims=True).astype(I32)
    e2 = jnp.sum(jnp.where(lane == 1, route, 0.0), axis=-1, keepdims=True).astype(I32)
    onehot = ((lane == e1) | (lane == e2)).astype(BF16)
    r = lax.broadcasted_iota(I32, (TM_MIX, TM_MIX), 0)
    c = lax.broadcasted_iota(I32, (TM_MIX, TM_MIX), 1)
    lower = (c < r).astype(BF16)
    before = jnp.dot(lower, onehot, preferred_element_type=F32) + base_ref[0]
    p1 = jnp.sum(jnp.where(lane == e1, before, 0.0), axis=-1, keepdims=True)
    p2 = jnp.sum(jnp.where(lane == e2, before, 0.0), axis=-1, keepdims=True)
    packed = jnp.where(lane == 0, p1, jnp.where(lane == 1, p2, 0.0))
    pos_ref[...] = jnp.transpose(packed)[0:TOP_K, :].astype(I32)


def _positions(route, base):
    t = route.shape[0]
    n_tiles = t // TM_MIX
    return pl.pallas_call(
        _pos_kernel,
        name="positions",
        grid=(n_tiles,),
        in_specs=[
            pl.BlockSpec((TM_MIX, LANES), lambda i: (i, 0)),
            pl.BlockSpec((1, 1, LANES), lambda i: (i, 0, 0)),
        ],
        out_specs=pl.BlockSpec((TOP_K, TM_MIX), lambda i: (0, i)),
        out_shape=jax.ShapeDtypeStruct((TOP_K, t), I32),
        compiler_params=pltpu.CompilerParams(dimension_semantics=("arbitrary",)),
    )(route, base)


def _sc_mesh():
    return plsc.VectorSubcoreMesh(core_axis_name="c", subcore_axis_name="s",
                                  num_cores=SC_CORES, num_subcores=SC_SUBCORES)


def _sc_worker():
    return lax.axis_index("s") * SC_CORES + lax.axis_index("c")


def _dispatch(pos, hp, rows):
    t = hp.shape[0]
    per_w = t // SC_WORKERS
    n_ch = per_w // SC_CHUNK
    pos4 = pos.reshape(TOP_K, SC_WORKERS, n_ch, SC_CHUNK)

    @functools.partial(
        pl.kernel, mesh=_sc_mesh(),
        out_type=jax.ShapeDtypeStruct((rows, PACKED), I32),
        scratch_types=[pltpu.VMEM((TOP_K, n_ch, SC_CHUNK), I32),
                       pltpu.VMEM((SC_CHUNK, PACKED), I32)])
    def scatter(hp_hbm, pos_hbm, xs_hbm, idx_v, rows_v):
        wid = _sc_worker()
        for k in range(TOP_K):
            pltpu.sync_copy(pos_hbm.at[k, wid], idx_v.at[k])

        def body(c, carry):
            start = pl.multiple_of(wid * per_w + c * SC_CHUNK, SC_CHUNK)
            pltpu.sync_copy(hp_hbm.at[pl.ds(start, SC_CHUNK)], rows_v)
            for k in range(TOP_K):
                pltpu.sync_copy(rows_v, xs_hbm.at[idx_v.at[k, c]])
            return carry

        lax.fori_loop(0, n_ch, body, 0)

    return scatter(hp, pos4)


def _gather_rows(table, idx):
    n = idx.shape[0]
    per_w = n // SC_WORKERS
    n_ch = per_w // SC_CHUNK
    idx3 = idx.reshape(SC_WORKERS, n_ch, SC_CHUNK)

    @functools.partial(
        pl.kernel, mesh=_sc_mesh(),
        out_type=jax.ShapeDtypeStruct((n, PACKED), I32),
        scratch_types=[pltpu.VMEM((n_ch, SC_CHUNK), I32),
                       pltpu.VMEM((SC_CHUNK, PACKED), I32)])
    def gather(table_hbm, idx_hbm, out_hbm, idx_v, rows_v):
        wid = _sc_worker()
        pltpu.sync_copy(idx_hbm.at[wid], idx_v)

        def body(c, carry):
            start = pl.multiple_of(wid * per_w + c * SC_CHUNK, SC_CHUNK)
            pltpu.sync_copy(table_hbm.at[idx_v.at[c]], rows_v)
            pltpu.sync_copy(rows_v, out_hbm.at[pl.ds(start, SC_CHUNK)])
            return carry

        lax.fori_loop(0, n_ch, body, 0)

    return gather(table, idx3)


def _expert_kernel(te_ref, ts_ref, tv_ref, xs_ref, wg_ref, wu_ref, wd_ref, o_ref, wg_s, wu_s, wd_s):
    del ts_ref
    i = pl.program_id(0)
    n_valid = tv_ref[i]

    @pl.when(jnp.logical_or(i == 0, te_ref[i] != te_ref[jnp.maximum(i - 1, 0)]))
    def _():
        wg_s[...] = wg_ref[0].astype(BF16)
        wu_s[...] = wu_ref[0].astype(BF16)
        wd_s[...] = wd_ref[0].astype(BF16)

    @pl.when(n_valid > 0)
    def _():
        rid = lax.broadcasted_iota(I32, xs_ref.shape, 0)
        lo, hi = _unpack_bf16_pairs(jnp.where(rid < n_valid, xs_ref[...], 0))
        lo = lo.astype(BF16)
        hi = hi.astype(BF16)
        a = (jnp.dot(lo, wg_s[:PACKED, :], preferred_element_type=F32)
             + jnp.dot(hi, wg_s[PACKED:, :], preferred_element_type=F32))
        b = (jnp.dot(lo, wu_s[:PACKED, :], preferred_element_type=F32)
             + jnp.dot(hi, wu_s[PACKED:, :], preferred_element_type=F32))
        hmid = (a * jax.nn.sigmoid(a) * b).astype(BF16)
        o_ref[...] = _pack_bf16_pairs(jnp.dot(hmid, wd_s[...], preferred_element_type=F32))

    @pl.when(n_valid <= 0)
    def _():
        o_ref[...] = jnp.zeros(o_ref.shape, o_ref.dtype)


def _experts(tile_expert, tile_src, tile_valid, xs, wg, wu, wd):
    rows = xs.shape[0]
    grid_spec = pltpu.PrefetchScalarGridSpec(
        num_scalar_prefetch=3,
        grid=(rows // TM_EXP,),
        in_specs=[
            pl.BlockSpec((TM_EXP, PACKED), lambda i, te, ts, tv: (ts[i], 0)),
            pl.BlockSpec((1, D_MODEL, D_FF), lambda i, te, ts, tv: (te[i], 0, 0)),
            pl.BlockSpec((1, D_MODEL, D_FF), lambda i, te, ts, tv: (te[i], 0, 0)),
            pl.BlockSpec((1, D_FF, D_MODEL), lambda i, te, ts, tv: (te[i], 0, 0)),
        ],
        out_specs=pl.BlockSpec((TM_EXP, PACKED), lambda i, te, ts, tv: (i, 0)),
        scratch_shapes=[pltpu.VMEM((D_MODEL, D_FF), BF16), pltpu.VMEM((D_MODEL, D_FF), BF16),
                        pltpu.VMEM((D_FF, D_MODEL), BF16)],
    )
    return pl.pallas_call(
        _expert_kernel,
        name="experts",
        grid_spec=grid_spec,
        out_shape=jax.ShapeDtypeStruct((rows, PACKED), I32),
        compiler_params=pltpu.CompilerParams(
            dimension_semantics=("arbitrary",), vmem_limit_bytes=VMEM_LIMIT),
    )(tile_expert, tile_src, tile_valid, xs, wg, wu, wd)


def _combine_kernel(y1_ref, y2_ref, route_ref, xmid_ref, g_ref, *rest):
    o_ref = rest[-1]
    route = route_ref[...]
    lane = lax.broadcasted_iota(I32, route.shape, 1)
    w1 = jnp.sum(jnp.where(lane == 2, route, 0.0), axis=-1, keepdims=True)
    w2 = jnp.sum(jnp.where(lane == 3, route, 0.0), axis=-1, keepdims=True)
    lo1, hi1 = _unpack_bf16_pairs(y1_ref[...])
    lo2, hi2 = _unpack_bf16_pairs(y2_ref[...])
    y = jnp.concatenate([lo1 * w1 + lo2 * w2, hi1 * w1 + hi2 * w2], axis=1)
    o_ref[...] = _rms(xmid_ref[...] + y, g_ref[...])


def _combine(yg, route, xmid, g, chunk, t_total, out_prev):
    t = xmid.shape[0]
    n_tiles = t // TM_CMB
    first = chunk * n_tiles
    in_specs = [
        pl.BlockSpec((TM_CMB, PACKED), lambda i: (i, 0)),
        pl.BlockSpec((TM_CMB, PACKED), lambda i: (n_tiles + i, 0)),
        pl.BlockSpec((TM_CMB, LANES), lambda i: (i, 0)),
        pl.BlockSpec((TM_CMB, D_MODEL), lambda i: (i, 0)),
        pl.BlockSpec((1, D_MODEL), lambda i: (0, 0)),
    ]
    args = [yg, yg, route, xmid, g]
    aliases = {}
    if out_prev is not None:
        in_specs.append(pl.BlockSpec(memory_space=pl.ANY))
        aliases = {len(args): 0}
        args.append(out_prev)
    return pl.pallas_call(
        _combine_kernel,
        name="combine",
        grid=(n_tiles,),
        in_specs=in_specs,
        out_specs=pl.BlockSpec((TM_CMB, D_MODEL), lambda i: (first + i, 0)),
        out_shape=jax.ShapeDtypeStruct((t_total, D_MODEL), F32),
        input_output_aliases=aliases,
        compiler_params=pltpu.CompilerParams(
            dimension_semantics=("arbitrary",), vmem_limit_bytes=VMEM_LIMIT),
    )(*args)


def _split_bf16(w):
    hi = w.astype(BF16)
    lo = (w - hi.astype(F32)).astype(BF16)
    return hi, lo


def kernel(x, norm_mix, w_in, conv_w, conv_b, w_a_out, sinks, w_b_out, w_o, norm_ffn, w_group,
           b_group, w_expert, b_expert, w_gate, w_up, w_down, norm_final):
    bsz, seq, d = x.shape
    t = bsz * seq
    assert d == D_MODEL and seq % TM_PROJ == 0 and seq % TQ_ATTN == 0 and seq % TM_CONV == 0
    xf = x.reshape(t, d)
    row = lambda v: v.reshape(1, -1)

    w_b, w_c, w_u, w_q, w_k, w_v, w_ga, w_gb = jnp.split(w_in, REF_SPLITS, axis=1)
    w_in_r = jnp.concatenate([w_b, w_c, w_u, w_ga, w_gb, w_q, w_k, w_v], axis=1).astype(BF16)

    proj = _inproj(xf, row(norm_mix), w_in_r)
    za = _conv_mixer(proj, conv_w, row(conv_b), w_a_out.astype(BF16), seq)
    attn = _attention(proj, sinks, seq)

    pad = LANES - N_GROUPS - N_EXPERTS
    w_r = jnp.concatenate([w_group, w_expert, jnp.zeros((d, pad), F32)], axis=1)
    b_r = jnp.concatenate([b_group, b_expert, jnp.zeros((pad,), F32)]).reshape(1, LANES)
    wr = jnp.concatenate(_split_bf16(w_r), axis=1)
    wb = w_b_out.astype(BF16)
    wo = w_o.astype(BF16)

    t_chunk = t // MOE_CHUNKS
    out = None
    for chunk in range(MOE_CHUNKS):
        xmid, h2, route, cnt = _mix(attn, proj, za, xf, wb, wo, row(norm_ffn), wr, b_r, chunk, t_chunk)
        out = _moe_chunk(xmid, h2, route, cnt, w_gate, w_up, w_down, row(norm_final), chunk, t, out)
    return out.reshape(bsz, seq, d)


def _moe_chunk(xmid, h2, route, cnt, w_gate, w_up, w_down, g_final, chunk, t_total, out_prev):
    t = xmid.shape[0]
    n_tiles = t // TM_MIX
    cnt = cnt.reshape(n_tiles, 8, LANES)[:, 0, :N_EXPERTS].astype(I32)
    totals = jnp.sum(cnt, axis=0)
    tiles_e = (totals + TM_EXP - 1) // TM_EXP
    tile_end = jnp.cumsum(tiles_e)
    offset = (tile_end - tiles_e) * TM_EXP
    base = offset[None, :] + jnp.cumsum(cnt, axis=0) - cnt
    base = jnp.pad(base, ((0, 0), (0, LANES - N_EXPERTS))).astype(F32).reshape(n_tiles, 1, LANES)
    rows = t * TOP_K + N_EXPERTS * TM_EXP
    n_active = tile_end[-1]
    tile_id = jnp.arange(rows // TM_EXP, dtype=I32)
    tile_src = jnp.minimum(tile_id, n_active - 1)
    tile_expert = jnp.sum((tile_src[:, None] >= tile_end[None, :]).astype(I32), axis=1)
    tile_expert = jnp.minimum(tile_expert, N_EXPERTS - 1)
    row_in_expert = (tile_id - (tile_end - tiles_e)[tile_expert]) * TM_EXP
    tile_valid = jnp.clip(totals[tile_expert] - row_in_expert, 0, TM_EXP)
    tile_valid = jnp.where(tile_id < n_active, tile_valid, 0).astype(I32)

    pos = _positions(route, base)
    xs = _dispatch(pos, h2, rows)
    ys = _experts(tile_expert.astype(I32), tile_src.astype(I32), tile_valid, xs,
                  w_gate, w_up, w_down)
    yg = _gather_rows(ys, pos.reshape(TOP_K * t))
    return _combine(yg, route, xmid, g_final, chunk, t_total, out_prev)
```

```python
import functools
import math

import jax
import jax.numpy as jnp
from jax import lax
from jax.experimental import pallas as pl
from jax.experimental.pallas import tpu as pltpu
from jax.experimental.pallas import tpu_sc as plsc

F32 = jnp.float32
BF16 = jnp.bfloat16
I32 = jnp.int32

D_MODEL = 1024
HEAD_DIM = 64
N_HEADS = 16
N_KV_HEADS = 4
GROUP = N_HEADS // N_KV_HEADS
KV_WIDTH = N_KV_HEADS * HEAD_DIM
WINDOW = 128
N_GROUPS = 4
EXPERTS_PER_GROUP = 8
N_EXPERTS = N_GROUPS * EXPERTS_PER_GROUP
TOP_K = 2
D_FF = 512
EPS = 1e-6
LANES = 128

REF_SPLITS = (1024, 2048, 3072, 4096, 4352, 4608, 5632)
REST_COLS = 2 * D_MODEL + 2 * KV_WIDTH
COL_GB, COL_Q = 0, 1
COL_K, COL_V = 2 * D_MODEL // KV_WIDTH, 2 * D_MODEL // KV_WIDTH + 1

TM_PROJ = 512
TQ_ATTN = 512
TM_MIX = 512
SUB_MIX = 256
MOE_CHUNKS = 2
TM_EXP = 512
TM_CMB = TM_MIX
HALO_ROWS = 8
VMEM_LIMIT = 56 * 1024 * 1024
PACKED = D_MODEL // 2

SC_CORES = 2
SC_SUBCORES = 16
SC_WORKERS = SC_CORES * SC_SUBCORES
SC_CHUNK = 64


def _rms(x, g):
    r = lax.rsqrt(jnp.mean(x * x, axis=-1, keepdims=True) + EPS)
    return (x * r) * g


def _pack_bf16_pairs(x):
    n = x.shape[1] // 2
    lo = lax.bitcast_convert_type(x[:, :n].astype(BF16).astype(F32), I32)
    hi = lax.bitcast_convert_type(x[:, n:].astype(BF16).astype(F32), I32)
    return (hi & jnp.int32(-65536)) | lax.shift_right_logical(lo, 16)


def _unpack_bf16_pairs(p):
    lo = lax.bitcast_convert_type(lax.shift_left(p, 16), F32)
    hi = lax.bitcast_convert_type(p & jnp.int32(-65536), F32)
    return lo, hi


def _inproj_kernel(x_ref, g_ref, wcu_ref, wbg_ref, wrest_ref, cw_ref, cb_ref, wa_ref,
                   za_ref, proj_ref, halo_ref, *, tiles_per_seq):
    h = _rms(x_ref[...], g_ref[...]).astype(BF16)
    pcu = jnp.dot(h, wcu_ref[...], preferred_element_type=F32)
    cu = pcu[:, :D_MODEL] * pcu[:, D_MODEL:]
    first = (pl.program_id(0) % tiles_per_seq) == 0
    hist = jnp.where(first, 0.0, halo_ref[...])
    prev1 = hist[HALO_ROWS - 1:HALO_ROWS]
    prev2 = hist[HALO_ROWS - 2:HALO_ROWS - 1]
    halo_ref[...] = cu[TM_PROJ - HALO_ROWS:, :]
    row = lax.broadcasted_iota(I32, cu.shape, 0)
    cu1 = jnp.where(row == 0, prev1, pltpu.roll(cu, 1, 0))
    cu2 = jnp.where(row == 0, prev2, jnp.where(row == 1, prev1, pltpu.roll(cu, 2, 0)))
    cw = cw_ref[...]
    y = cw[0:1] * cu2 + cw[1:2] * cu1 + cw[2:3] * cu + cb_ref[...]
    pbg = jnp.dot(h, wbg_ref[...], preferred_element_type=F32)
    ya = (pbg[:, :D_MODEL] * y).astype(BF16)
    z = jnp.dot(ya, wa_ref[...], preferred_element_type=F32)
    za_ref[...] = (jax.nn.sigmoid(pbg[:, D_MODEL:]) * z).astype(BF16)
    proj_ref[...] = jnp.dot(h, wrest_ref[...], preferred_element_type=F32).astype(BF16)


def _inproj(xf, g, w_cu, w_bg, w_rest, conv_w, conv_b, wa, seq):
    t = xf.shape[0]
    const = lambda shape: pl.BlockSpec(shape, lambda i: (0, 0), pipeline_mode=pl.Buffered(1))
    return pl.pallas_call(
        functools.partial(_inproj_kernel, tiles_per_seq=seq // TM_PROJ),
        name="inproj_conv",
        grid=(t // TM_PROJ,),
        in_specs=[
            pl.BlockSpec((TM_PROJ, D_MODEL), lambda i: (i, 0)),
            const((1, D_MODEL)),
            const((D_MODEL, 2 * D_MODEL)), const((D_MODEL, 2 * D_MODEL)), const((D_MODEL, REST_COLS)),
            const((3, D_MODEL)), const((1, D_MODEL)), const((D_MODEL, D_MODEL)),
        ],
        out_specs=[pl.BlockSpec((TM_PROJ, D_MODEL), lambda i: (i, 0)),
                   pl.BlockSpec((TM_PROJ, REST_COLS), lambda i: (i, 0))],
        out_shape=[jax.ShapeDtypeStruct((t, D_MODEL), BF16),
                   jax.ShapeDtypeStruct((t, REST_COLS), BF16)],
        scratch_shapes=[pltpu.VMEM((HALO_ROWS, D_MODEL), F32)],
        compiler_params=pltpu.CompilerParams(
            dimension_semantics=("arbitrary",), vmem_limit_bytes=VMEM_LIMIT),
    )(xf, g, w_cu, w_bg, w_rest, conv_w, conv_b, wa)


def _attn_kernel(sink_ref, q_ref, k_ref, v_ref, kp_ref, vp_ref, o_ref, *, tiles_per_seq):
    first_tile = (pl.program_id(0) % tiles_per_seq) == 0
    ks = lax.broadcasted_iota(I32, (WINDOW, WINDOW), 0)
    qq = lax.broadcasted_iota(I32, (WINDOW, WINDOW), 1)
    own = ks <= qq
    dist = jnp.where(own, qq - ks, qq - ks + WINDOW).astype(F32)
    visible0 = jnp.logical_or(own, jnp.logical_not(first_tile))
    log2e = math.log2(math.e)
    c_scale = log2e / math.sqrt(HEAD_DIM)
    nt = (((1,), (1,)), ((), ()))
    zk = jnp.zeros((2 * WINDOW, HEAD_DIM), BF16)
    zv = jnp.zeros((HEAD_DIM, 2 * WINDOW), BF16)

    def transposed(v_blk):
        return jnp.transpose(v_blk.astype(F32)).astype(BF16)

    prev_k = kp_ref[...]
    prev_vt = transposed(vp_ref[...])
    for sb in range(TQ_ATTN // WINDOW):
        rows = slice(sb * WINDOW, (sb + 1) * WINDOW)
        cur_k = k_ref[rows, :]
        cur_vt = transposed(v_ref[rows, :])
        out_t = []
        for kh in range(N_KV_HEADS):
            cols = slice(kh * HEAD_DIM, (kh + 1) * HEAD_DIM)
            kcat = jnp.concatenate([prev_k[:, cols], cur_k[:, cols]], axis=0)
            vcat = jnp.concatenate([prev_vt[cols, :], cur_vt[cols, :]], axis=1)
            qg = jnp.concatenate([q_ref[rows, (2 * kh) * LANES:(2 * kh + 1) * LANES],
                                  q_ref[rows, (2 * kh + 1) * LANES:(2 * kh + 2) * LANES]], axis=0)
            probs = [[None, None], [None, None]]
            for pos in range(2):
                k_pad = jnp.concatenate([kcat, zk] if pos == 0 else [zk, kcat], axis=1)
                st = lax.dot_general(k_pad, qg, nt, preferred_element_type=F32)
                for half in range(2):
                    h = kh * GROUP + 2 * half + pos
                    slope = 2.0 ** (-8.0 * (h + 1) / N_HEADS)
                    qcols = slice(half * WINDOW, (half + 1) * WINDOW)
                    s = (jnp.where(own, st[WINDOW:, qcols], st[:WINDOW, qcols]) * c_scale
                         - (slope * log2e) * dist)
                    if sb == 0:
                        s = jnp.where(visible0, s, -jnp.inf)
                    m = jnp.max(s, axis=0, keepdims=True)
                    p = jnp.exp2(s - m)
                    den = jnp.sum(p, axis=0, keepdims=True) + jnp.exp2(sink_ref[h] * log2e - m)
                    pn = p * (1.0 / den)
                    probs[pos][half] = jnp.concatenate(
                        [jnp.where(own, 0.0, pn).astype(BF16), jnp.where(own, pn, 0.0).astype(BF16)],
                        axis=0)
            v_lo = jnp.concatenate([vcat, zv], axis=0)
            v_hi = jnp.concatenate([zv, vcat], axis=0)
            for half in range(2):
                out_t.append(jnp.dot(v_lo, probs[0][half], preferred_element_type=F32)
                             + jnp.dot(v_hi, probs[1][half], preferred_element_type=F32))
        o_ref[rows, :] = jnp.transpose(jnp.concatenate(out_t, axis=0)).astype(BF16)
        prev_k, prev_vt = cur_k, cur_vt


def _attention(proj, sinks, seq):
    t = proj.shape[0]
    sub = TQ_ATTN // WINDOW
    return pl.pallas_call(
        functools.partial(_attn_kernel, tiles_per_seq=seq // TQ_ATTN),
        name="swattn",
        grid=(t // TQ_ATTN,),
        in_specs=[
            pl.BlockSpec(memory_space=pltpu.SMEM),
            pl.BlockSpec((TQ_ATTN, D_MODEL), lambda i: (i, COL_Q)),
            pl.BlockSpec((TQ_ATTN, KV_WIDTH), lambda i: (i, COL_K)),
            pl.BlockSpec((TQ_ATTN, KV_WIDTH), lambda i: (i, COL_V)),
            pl.BlockSpec((WINDOW, KV_WIDTH), lambda i: (jnp.maximum(i * sub - 1, 0), COL_K)),
            pl.BlockSpec((WINDOW, KV_WIDTH), lambda i: (jnp.maximum(i * sub - 1, 0), COL_V)),
        ],
        out_specs=pl.BlockSpec((TQ_ATTN, D_MODEL), lambda i: (i, 0)),
        out_shape=jax.ShapeDtypeStruct((t, D_MODEL), BF16),
        compiler_params=pltpu.CompilerParams(
            dimension_semantics=("arbitrary",), vmem_limit_bytes=VMEM_LIMIT),
    )(sinks, proj, proj, proj, proj, proj)


def _mix_kernel(attn_ref, gb_ref, za_ref, x_ref, wb_ref, wo_ref, g_ref, wr_ref, br_ref,
                xmid_ref, h_ref, route_ref, cnt_ref):
    cnt = jnp.zeros((1, LANES), F32)
    for sub in range(TM_MIX // SUB_MIX):
        rows = slice(sub * SUB_MIX, (sub + 1) * SUB_MIX)
        cnt = cnt + _mix_rows(rows, attn_ref, gb_ref, za_ref, x_ref, wb_ref, wo_ref, g_ref, wr_ref,
                              br_ref, xmid_ref, h_ref, route_ref)
    cnt_ref[...] = jnp.broadcast_to(cnt, cnt_ref.shape)


def _mix_rows(rows, attn_ref, gb_ref, za_ref, x_ref, wb_ref, wo_ref, g_ref, wr_ref, br_ref,
              xmid_ref, h_ref, route_ref):
    yb = jnp.dot(attn_ref[rows, :], wb_ref[...], preferred_element_type=F32)
    merged = za_ref[rows, :].astype(F32) + jax.nn.sigmoid(gb_ref[rows, :].astype(F32)) * yb
    xm = x_ref[rows, :] + jnp.dot(merged.astype(BF16), wo_ref[...], preferred_element_type=F32)
    xmid_ref[rows, :] = xm
    h = _rms(xm, g_ref[...])
    h_ref[rows, :] = _pack_bf16_pairs(h)
    h_hi = h.astype(BF16)
    h_lo = (h - h_hi.astype(F32)).astype(BF16)
    wr = wr_ref[...]
    both = jnp.dot(h_hi, wr, preferred_element_type=F32)
    logits = (both[:, :LANES] + both[:, LANES:]
              + jnp.dot(h_lo, wr[:, :LANES], preferred_element_type=F32)) + br_ref[...]
    lane = lax.broadcasted_iota(I32, logits.shape, 1)
    neg = -jnp.inf
    gl = jnp.where(lane < N_GROUPS, logits, neg)
    gmax = jnp.max(gl, axis=-1, keepdims=True)
    g_idx = jnp.min(jnp.where(gl == gmax, lane, LANES), axis=-1, keepdims=True)
    p_g = 1.0 / jnp.sum(jnp.exp(gl - gmax), axis=-1, keepdims=True)
    start = N_GROUPS + EXPERTS_PER_GROUP * g_idx
    el = jnp.where((lane >= start) & (lane < start + EXPERTS_PER_GROUP), logits, neg)
    v1 = jnp.max(el, axis=-1, keepdims=True)
    i1 = jnp.min(jnp.where(el == v1, lane, LANES), axis=-1, keepdims=True)
    el2 = jnp.where(lane == i1, neg, el)
    v2 = jnp.max(el2, axis=-1, keepdims=True)
    i2 = jnp.min(jnp.where(el2 == v2, lane, LANES), axis=-1, keepdims=True)
    e21 = jnp.exp(v2 - v1)
    w1 = p_g / (1.0 + e21)
    w2 = p_g * e21 / (1.0 + e21)
    e1 = i1 - N_GROUPS
    e2 = i2 - N_GROUPS
    route = jnp.where(lane == 0, e1.astype(F32),
                      jnp.where(lane == 1, e2.astype(F32),
                                jnp.where(lane == 2, w1, jnp.where(lane == 3, w2, 0.0))))
    route_ref[rows, :] = route
    onehot = ((lane == e1) | (lane == e2)).astype(F32)
    return jnp.sum(onehot, axis=0, keepdims=True)


def _mix(attn, proj, za, xf, wb, wo, g, wr, br, chunk, t):
    n_tiles = t // TM_MIX
    first = chunk * n_tiles
    full = lambda shape: pl.BlockSpec(shape, lambda i: (0, 0))
    tile = lambda w=D_MODEL: pl.BlockSpec((TM_MIX, w), lambda i: (i, 0))
    src = lambda c=0: pl.BlockSpec((TM_MIX, D_MODEL), lambda i: (first + i, c))
    return pl.pallas_call(
        _mix_kernel,
        name="merge_router",
        grid=(n_tiles,),
        in_specs=[
            src(), src(COL_GB), src(), src(),
            full((D_MODEL, D_MODEL)), full((D_MODEL, D_MODEL)), full((1, D_MODEL)),
            full((D_MODEL, 2 * LANES)), full((1, LANES)),
        ],
        out_specs=[tile(), tile(PACKED), tile(LANES), pl.BlockSpec((8, LANES), lambda i: (i, 0))],
        out_shape=[
            jax.ShapeDtypeStruct((t, D_MODEL), F32),
            jax.ShapeDtypeStruct((t, PACKED), I32),
            jax.ShapeDtypeStruct((t, LANES), F32),
            jax.ShapeDtypeStruct((n_tiles * 8, LANES), F32),
        ],
        compiler_params=pltpu.CompilerParams(
            dimension_semantics=("arbitrary",), vmem_limit_bytes=VMEM_LIMIT),
    )(attn, proj, za, xf, wb, wo, g, wr, br)


def _pos_kernel(route_ref, base_ref, pos_ref):
    route = route_ref[...]
    lane = lax.broadcasted_iota(I32, route.shape, 1)
    e1 = jnp.sum(jnp.where(lane == 0, route, 0.0), axis=-1, keepdims=True).astype(I32)
    e2 = jnp.sum(jnp.where(lane == 1, route, 0.0), axis=-1, keepdims=True).astype(I32)
    onehot = ((lane == e1) | (lane == e2)).astype(BF16)
    r = lax.broadcasted_iota(I32, (TM_MIX, TM_MIX), 0)
    c = lax.broadcasted_iota(I32, (TM_MIX, TM_MIX), 1)
    lower = (c < r).astype(BF16)
    before = jnp.dot(lower, onehot, preferred_element_type=F32) + base_ref[0]
    p1 = jnp.sum(jnp.where(lane == e1, before, 0.0), axis=-1, keepdims=True)
    p2 = jnp.sum(jnp.where(lane == e2, before, 0.0), axis=-1, keepdims=True)
    packed = jnp.where(lane == 0, p1, jnp.where(lane == 1, p2, 0.0))
    pos_ref[...] = jnp.transpose(packed)[0:TOP_K, :].astype(I32)


def _positions(route, base):
    t = route.shape[0]
    n_tiles = t // TM_MIX
    return pl.pallas_call(
        _pos_kernel,
        name="positions",
        grid=(n_tiles,),
        in_specs=[
            pl.BlockSpec((TM_MIX, LANES), lambda i: (i, 0)),
            pl.BlockSpec((1, 1, LANES), lambda i: (i, 0, 0)),
        ],
        out_specs=pl.BlockSpec((TOP_K, TM_MIX), lambda i: (0, i)),
        out_shape=jax.ShapeDtypeStruct((TOP_K, t), I32),
        compiler_params=pltpu.CompilerParams(dimension_semantics=("arbitrary",)),
    )(route, base)


def _sc_mesh():
    return plsc.VectorSubcoreMesh(core_axis_name="c", subcore_axis_name="s",
                                  num_cores=SC_CORES, num_subcores=SC_SUBCORES)


def _sc_worker():
    return lax.axis_index("s") * SC_CORES + lax.axis_index("c")


def _dispatch(pos, hp, rows):
    t = hp.shape[0]
    per_w = t // SC_WORKERS
    n_ch = per_w // SC_CHUNK
    pos4 = pos.reshape(TOP_K, SC_WORKERS, n_ch, SC_CHUNK)

    @functools.partial(
        pl.kernel, mesh=_sc_mesh(),
        out_type=jax.ShapeDtypeStruct((rows, PACKED), I32),
        scratch_types=[pltpu.VMEM((TOP_K, n_ch, SC_CHUNK), I32),
                       pltpu.VMEM((SC_CHUNK, PACKED), I32)])
    def scatter(hp_hbm, pos_hbm, xs_hbm, idx_v, rows_v):
        wid = _sc_worker()
        for k in range(TOP_K):
            pltpu.sync_copy(pos_hbm.at[k, wid], idx_v.at[k])

        def body(c, carry):
            start = pl.multiple_of(wid * per_w + c * SC_CHUNK, SC_CHUNK)
            pltpu.sync_copy(hp_hbm.at[pl.ds(start, SC_CHUNK)], rows_v)
            for k in range(TOP_K):
                pltpu.sync_copy(rows_v, xs_hbm.at[idx_v.at[k, c]])
            return carry

        lax.fori_loop(0, n_ch, body, 0)

    return scatter(hp, pos4)


def _gather_rows(table, idx):
    n = idx.shape[0]
    per_w = n // SC_WORKERS
    n_ch = per_w // SC_CHUNK
    idx3 = idx.reshape(SC_WORKERS, n_ch, SC_CHUNK)

    @functools.partial(
        pl.kernel, mesh=_sc_mesh(),
        out_type=jax.ShapeDtypeStruct((n, PACKED), I32),
        scratch_types=[pltpu.VMEM((n_ch, SC_CHUNK), I32),
                       pltpu.VMEM((SC_CHUNK, PACKED), I32)])
    def gather(table_hbm, idx_hbm, out_hbm, idx_v, rows_v):
        wid = _sc_worker()
        pltpu.sync_copy(idx_hbm.at[wid], idx_v)

        def body(c, carry):
            start = pl.multiple_of(wid * per_w + c * SC_CHUNK, SC_CHUNK)
            pltpu.sync_copy(table_hbm.at[idx_v.at[c]], rows_v)
            pltpu.sync_copy(rows_v, out_hbm.at[pl.ds(start, SC_CHUNK)])
            return carry

        lax.fori_loop(0, n_ch, body, 0)

    return gather(table, idx3)


def _expert_kernel(te_ref, nx_ref, sl_ref, ts_ref, tv_ref, xs_ref, wg_hbm, wu_hbm, wd_hbm, o_ref,
                   stage_g, stage_u, stage_d, wg_s, wu_s, wd_s, sem):
    del ts_ref
    i = pl.program_id(0)
    n_valid = tv_ref[i]
    expert = te_ref[i]
    slot = sl_ref[i]

    def weight_copies(e, s):
        return (pltpu.make_async_copy(wg_hbm.at[e], stage_g.at[s], sem.at[s, 0]),
                pltpu.make_async_copy(wu_hbm.at[e], stage_u.at[s], sem.at[s, 1]),
                pltpu.make_async_copy(wd_hbm.at[e], stage_d.at[s], sem.at[s, 2]))

    @pl.when(i == 0)
    def _():
        for copy in weight_copies(expert, slot):
            copy.start()

    @pl.when(jnp.logical_or(i == 0, expert != te_ref[jnp.maximum(i - 1, 0)]))
    def _():
        for copy in weight_copies(expert, slot):
            copy.wait()
        nxt = nx_ref[i]

        @pl.when(nxt >= 0)
        def _():
            for copy in weight_copies(nxt, 1 - slot):
                copy.start()

        wg_s[...] = stage_g[slot].astype(BF16)
        wu_s[...] = stage_u[slot].astype(BF16)
        wd_s[...] = stage_d[slot].astype(BF16)

    @pl.when(n_valid > 0)
    def _():
        rid = lax.broadcasted_iota(I32, xs_ref.shape, 0)
        lo, hi = _unpack_bf16_pairs(jnp.where(rid < n_valid, xs_ref[...], 0))
        lo = lo.astype(BF16)
        hi = hi.astype(BF16)
        a = (jnp.dot(lo, wg_s[:PACKED, :], preferred_element_type=F32)
             + jnp.dot(hi, wg_s[PACKED:, :], preferred_element_type=F32))
        b = (jnp.dot(lo, wu_s[:PACKED, :], preferred_element_type=F32)
             + jnp.dot(hi, wu_s[PACKED:, :], preferred_element_type=F32))
        hmid = (a * jax.nn.sigmoid(a) * b).astype(BF16)
        o_ref[...] = _pack_bf16_pairs(jnp.dot(hmid, wd_s[...], preferred_element_type=F32))

    @pl.when(n_valid <= 0)
    def _():
        o_ref[...] = jnp.zeros(o_ref.shape, o_ref.dtype)


def _experts(tile_expert, next_expert, tile_slot, tile_src, tile_valid, xs, wg, wu, wd):
    rows = xs.shape[0]
    hbm = pl.BlockSpec(memory_space=pl.ANY)
    grid_spec = pltpu.PrefetchScalarGridSpec(
        num_scalar_prefetch=5,
        grid=(rows // TM_EXP,),
        in_specs=[pl.BlockSpec((TM_EXP, PACKED), lambda i, te, nx, sl, ts, tv: (ts[i], 0)),
                  hbm, hbm, hbm],
        out_specs=pl.BlockSpec((TM_EXP, PACKED), lambda i, te, nx, sl, ts, tv: (i, 0)),
        scratch_shapes=[
            pltpu.VMEM((2, D_MODEL, D_FF), F32), pltpu.VMEM((2, D_MODEL, D_FF), F32),
            pltpu.VMEM((2, D_FF, D_MODEL), F32),
            pltpu.VMEM((D_MODEL, D_FF), BF16), pltpu.VMEM((D_MODEL, D_FF), BF16),
            pltpu.VMEM((D_FF, D_MODEL), BF16),
            pltpu.SemaphoreType.DMA((2, 3)),
        ],
    )
    return pl.pallas_call(
        _expert_kernel,
        name="experts",
        grid_spec=grid_spec,
        out_shape=jax.ShapeDtypeStruct((rows, PACKED), I32),
        compiler_params=pltpu.CompilerParams(
            dimension_semantics=("arbitrary",), vmem_limit_bytes=VMEM_LIMIT),
    )(tile_expert, next_expert, tile_slot, tile_src, tile_valid, xs, wg, wu, wd)


def _combine_kernel(y1_ref, y2_ref, route_ref, xmid_ref, g_ref, *rest):
    o_ref = rest[-1]
    route = route_ref[...]
    lane = lax.broadcasted_iota(I32, route.shape, 1)
    w1 = jnp.sum(jnp.where(lane == 2, route, 0.0), axis=-1, keepdims=True)
    w2 = jnp.sum(jnp.where(lane == 3, route, 0.0), axis=-1, keepdims=True)
    lo1, hi1 = _unpack_bf16_pairs(y1_ref[...])
    lo2, hi2 = _unpack_bf16_pairs(y2_ref[...])
    y = jnp.concatenate([lo1 * w1 + lo2 * w2, hi1 * w1 + hi2 * w2], axis=1)
    o_ref[...] = _rms(xmid_ref[...] + y, g_ref[...])


def _combine(yg, route, xmid, g, chunk, t_total, out_prev):
    t = xmid.shape[0]
    n_tiles = t // TM_CMB
    first = chunk * n_tiles
    in_specs = [
        pl.BlockSpec((TM_CMB, PACKED), lambda i: (i, 0)),
        pl.BlockSpec((TM_CMB, PACKED), lambda i: (n_tiles + i, 0)),
        pl.BlockSpec((TM_CMB, LANES), lambda i: (i, 0)),
        pl.BlockSpec((TM_CMB, D_MODEL), lambda i: (i, 0)),
        pl.BlockSpec((1, D_MODEL), lambda i: (0, 0)),
    ]
    args = [yg, yg, route, xmid, g]
    aliases = {}
    if out_prev is not None:
        in_specs.append(pl.BlockSpec(memory_space=pl.ANY))
        aliases = {len(args): 0}
        args.append(out_prev)
    return pl.pallas_call(
        _combine_kernel,
        name="combine",
        grid=(n_tiles,),
        in_specs=in_specs,
        out_specs=pl.BlockSpec((TM_CMB, D_MODEL), lambda i: (first + i, 0)),
        out_shape=jax.ShapeDtypeStruct((t_total, D_MODEL), F32),
        input_output_aliases=aliases,
        compiler_params=pltpu.CompilerParams(
            dimension_semantics=("arbitrary",), vmem_limit_bytes=VMEM_LIMIT),
    )(*args)


def _split_bf16(w):
    hi = w.astype(BF16)
    lo = (w - hi.astype(F32)).astype(BF16)
    return hi, lo


def kernel(x, norm_mix, w_in, conv_w, conv_b, w_a_out, sinks, w_b_out, w_o, norm_ffn, w_group,
           b_group, w_expert, b_expert, w_gate, w_up, w_down, norm_final):
    bsz, seq, d = x.shape
    t = bsz * seq
    assert d == D_MODEL and seq % TM_PROJ == 0 and seq % TQ_ATTN == 0
    xf = x.reshape(t, d)
    row = lambda v: v.reshape(1, -1)

    w_b, w_c, w_u, w_q, w_k, w_v, w_ga, w_gb = jnp.split(w_in, REF_SPLITS, axis=1)
    w_cu = jnp.concatenate([w_c, w_u], axis=1).astype(BF16)
    w_bg = jnp.concatenate([w_b, w_ga], axis=1).astype(BF16)
    w_rest = jnp.concatenate([w_gb, w_q, w_k, w_v], axis=1).astype(BF16)

    za, proj = _inproj(xf, row(norm_mix), w_cu, w_bg, w_rest, conv_w, row(conv_b),
                       w_a_out.astype(BF16), seq)
    attn = _attention(proj, sinks, seq)

    pad = LANES - N_GROUPS - N_EXPERTS
    w_r = jnp.concatenate([w_group, w_expert, jnp.zeros((d, pad), F32)], axis=1)
    b_r = jnp.concatenate([b_group, b_expert, jnp.zeros((pad,), F32)]).reshape(1, LANES)
    wr = jnp.concatenate(_split_bf16(w_r), axis=1)
    wb = w_b_out.astype(BF16)
    wo = w_o.astype(BF16)

    t_chunk = t // MOE_CHUNKS
    out = None
    for chunk in range(MOE_CHUNKS):
        xmid, h2, route, cnt = _mix(attn, proj, za, xf, wb, wo, row(norm_ffn), wr, b_r, chunk, t_chunk)
        out = _moe_chunk(xmid, h2, route, cnt, w_gate, w_up, w_down, row(norm_final), chunk, t, out)
    return out.reshape(bsz, seq, d)


def _moe_chunk(xmid, h2, route, cnt, w_gate, w_up, w_down, g_final, chunk, t_total, out_prev):
    t = xmid.shape[0]
    n_tiles = t // TM_MIX
    cnt = cnt.reshape(n_tiles, 8, LANES)[:, 0, :N_EXPERTS].astype(I32)
    totals = jnp.sum(cnt, axis=0)
    tiles_e = (totals + TM_EXP - 1) // TM_EXP
    tile_end = jnp.cumsum(tiles_e)
    offset = (tile_end - tiles_e) * TM_EXP
    base = offset[None, :] + jnp.cumsum(cnt, axis=0) - cnt
    base = jnp.pad(base, ((0, 0), (0, LANES - N_EXPERTS))).astype(F32).reshape(n_tiles, 1, LANES)
    rows = t * TOP_K + N_EXPERTS * TM_EXP
    n_active = tile_end[-1]
    tile_id = jnp.arange(rows // TM_EXP, dtype=I32)
    tile_src = jnp.minimum(tile_id, n_active - 1)
    tile_expert = jnp.sum((tile_src[:, None] >= tile_end[None, :]).astype(I32), axis=1)
    tile_expert = jnp.minimum(tile_expert, N_EXPERTS - 1)
    row_in_expert = (tile_id - (tile_end - tiles_e)[tile_expert]) * TM_EXP
    tile_valid = jnp.clip(totals[tile_expert] - row_in_expert, 0, TM_EXP)
    tile_valid = jnp.where(tile_id < n_active, tile_valid, 0).astype(I32)
    after = tile_end[tile_expert]
    next_expert = jnp.where(after < n_active, tile_expert[jnp.minimum(after, n_active - 1)], -1)
    first_of_expert = jnp.concatenate(
        [jnp.ones((1,), I32), (tile_expert[1:] != tile_expert[:-1]).astype(I32)])
    tile_slot = (jnp.cumsum(first_of_expert) - 1) % 2

    pos = _positions(route, base)
    xs = _dispatch(pos, h2, rows)
    ys = _experts(tile_expert.astype(I32), next_expert.astype(I32), tile_slot.astype(I32),
                  tile_src.astype(I32), tile_valid, xs, w_gate, w_up, w_down)
    yg = _gather_rows(ys, pos.reshape(TOP_K * t))
    return _combine(yg, route, xmid, g_final, chunk, t_total, out_prev)
```

```python
import functools
import math

import jax
import jax.numpy as jnp
from jax import lax
from jax.experimental import pallas as pl
from jax.experimental.pallas import tpu as pltpu
from jax.experimental.pallas import tpu_sc as plsc

F32 = jnp.float32
BF16 = jnp.bfloat16
I32 = jnp.int32

D_MODEL = 1024
HEAD_DIM = 64
N_HEADS = 16
N_KV_HEADS = 4
GROUP = N_HEADS // N_KV_HEADS
KV_WIDTH = N_KV_HEADS * HEAD_DIM
WINDOW = 128
N_GROUPS = 4
EXPERTS_PER_GROUP = 8
N_EXPERTS = N_GROUPS * EXPERTS_PER_GROUP
TOP_K = 2
D_FF = 512
EPS = 1e-6
LANES = 128

REF_SPLITS = (1024, 2048, 3072, 4096, 4352, 4608, 5632)
REST_COLS = 2 * D_MODEL + 2 * KV_WIDTH
COL_GB, COL_Q = 0, 1
COL_K, COL_V = 2 * D_MODEL // KV_WIDTH, 2 * D_MODEL // KV_WIDTH + 1

TM_PROJ = 512
TQ_ATTN = 512
TM_MIX = 512
SUB_MIX = 256
MOE_CHUNKS = 2
TM_EXP = 512
SUB_EXP = 256
TM_CMB = TM_MIX
HALO_ROWS = 8
VMEM_LIMIT = 56 * 1024 * 1024
PACKED = D_MODEL // 2

SC_CORES = 2
SC_SUBCORES = 16
SC_WORKERS = SC_CORES * SC_SUBCORES
SC_CHUNK = 64


def _rms(x, g):
    r = lax.rsqrt(jnp.mean(x * x, axis=-1, keepdims=True) + EPS)
    return (x * r) * g


def _pack_bf16_pairs(x):
    n = x.shape[1] // 2
    lo = lax.bitcast_convert_type(x[:, :n].astype(BF16).astype(F32), I32)
    hi = lax.bitcast_convert_type(x[:, n:].astype(BF16).astype(F32), I32)
    return (hi & jnp.int32(-65536)) | lax.shift_right_logical(lo, 16)


def _unpack_bf16_pairs(p):
    lo = lax.bitcast_convert_type(lax.shift_left(p, 16), F32)
    hi = lax.bitcast_convert_type(p & jnp.int32(-65536), F32)
    return lo, hi


def _inproj_kernel(x_ref, g_ref, wcu_ref, wbg_ref, wrest_ref, cw_ref, cb_ref, wa_ref,
                   za_ref, proj_ref, halo_ref, *, tiles_per_seq):
    h = _rms(x_ref[...], g_ref[...]).astype(BF16)
    pcu = jnp.dot(h, wcu_ref[...], preferred_element_type=F32)
    cu = pcu[:, :D_MODEL] * pcu[:, D_MODEL:]
    first = (pl.program_id(0) % tiles_per_seq) == 0
    hist = jnp.where(first, 0.0, halo_ref[...])
    prev1 = hist[HALO_ROWS - 1:HALO_ROWS]
    prev2 = hist[HALO_ROWS - 2:HALO_ROWS - 1]
    halo_ref[...] = cu[TM_PROJ - HALO_ROWS:, :]
    row = lax.broadcasted_iota(I32, cu.shape, 0)
    cu1 = jnp.where(row == 0, prev1, pltpu.roll(cu, 1, 0))
    cu2 = jnp.where(row == 0, prev2, jnp.where(row == 1, prev1, pltpu.roll(cu, 2, 0)))
    cw = cw_ref[...]
    y = cw[0:1] * cu2 + cw[1:2] * cu1 + cw[2:3] * cu + cb_ref[...]
    pbg = jnp.dot(h, wbg_ref[...], preferred_element_type=F32)
    ya = (pbg[:, :D_MODEL] * y).astype(BF16)
    z = jnp.dot(ya, wa_ref[...], preferred_element_type=F32)
    za_ref[...] = (jax.nn.sigmoid(pbg[:, D_MODEL:]) * z).astype(BF16)
    proj_ref[...] = jnp.dot(h, wrest_ref[...], preferred_element_type=F32).astype(BF16)


def _inproj(xf, g, w_cu, w_bg, w_rest, conv_w, conv_b, wa, seq):
    t = xf.shape[0]
    const = lambda shape: pl.BlockSpec(shape, lambda i: (0, 0), pipeline_mode=pl.Buffered(1))
    return pl.pallas_call(
        functools.partial(_inproj_kernel, tiles_per_seq=seq // TM_PROJ),
        name="inproj_conv",
        grid=(t // TM_PROJ,),
        in_specs=[
            pl.BlockSpec((TM_PROJ, D_MODEL), lambda i: (i, 0)),
            const((1, D_MODEL)),
            const((D_MODEL, 2 * D_MODEL)), const((D_MODEL, 2 * D_MODEL)), const((D_MODEL, REST_COLS)),
            const((3, D_MODEL)), const((1, D_MODEL)), const((D_MODEL, D_MODEL)),
        ],
        out_specs=[pl.BlockSpec((TM_PROJ, D_MODEL), lambda i: (i, 0)),
                   pl.BlockSpec((TM_PROJ, REST_COLS), lambda i: (i, 0))],
        out_shape=[jax.ShapeDtypeStruct((t, D_MODEL), BF16),
                   jax.ShapeDtypeStruct((t, REST_COLS), BF16)],
        scratch_shapes=[pltpu.VMEM((HALO_ROWS, D_MODEL), F32)],
        compiler_params=pltpu.CompilerParams(
            dimension_semantics=("arbitrary",), vmem_limit_bytes=VMEM_LIMIT),
    )(xf, g, w_cu, w_bg, w_rest, conv_w, conv_b, wa)


def _attn_kernel(sink_ref, q_ref, k_ref, v_ref, kp_ref, vp_ref, o_ref, *, tiles_per_seq):
    first_tile = (pl.program_id(0) % tiles_per_seq) == 0
    ks = lax.broadcasted_iota(I32, (WINDOW, WINDOW), 0)
    qq = lax.broadcasted_iota(I32, (WINDOW, WINDOW), 1)
    own = ks <= qq
    dist = jnp.where(own, qq - ks, qq - ks + WINDOW).astype(F32)
    visible0 = jnp.logical_or(own, jnp.logical_not(first_tile))
    log2e = math.log2(math.e)
    c_scale = log2e / math.sqrt(HEAD_DIM)
    nt = (((1,), (1,)), ((), ()))
    zk = jnp.zeros((2 * WINDOW, HEAD_DIM), BF16)

    def transposed(v_blk):
        return jnp.transpose(v_blk.astype(F32)).astype(BF16)

    prev_k = kp_ref[...]
    prev_vt = transposed(vp_ref[...])
    for sb in range(TQ_ATTN // WINDOW):
        rows = slice(sb * WINDOW, (sb + 1) * WINDOW)
        cur_k = k_ref[rows, :]
        cur_vt = transposed(v_ref[rows, :])
        scores, vcats = [], []
        for kh in range(N_KV_HEADS):
            cols = slice(kh * HEAD_DIM, (kh + 1) * HEAD_DIM)
            kcat = jnp.concatenate([prev_k[:, cols], cur_k[:, cols]], axis=0)
            vcats.append(jnp.concatenate([prev_vt[cols, :], cur_vt[cols, :]], axis=1))
            qg = jnp.concatenate([q_ref[rows, (2 * kh) * LANES:(2 * kh + 1) * LANES],
                                  q_ref[rows, (2 * kh + 1) * LANES:(2 * kh + 2) * LANES]], axis=0)
            k_pad = jnp.concatenate([jnp.concatenate([kcat, zk], axis=1),
                                     jnp.concatenate([zk, kcat], axis=1)], axis=0)
            scores.append(lax.dot_general(k_pad, qg, nt, preferred_element_type=F32))
        probs, rdens = [], []
        for kh in range(N_KV_HEADS):
            for pos in range(2):
                pr, rd = [], []
                for half in range(2):
                    h = kh * GROUP + 2 * half + pos
                    slope = 2.0 ** (-8.0 * (h + 1) / N_HEADS)
                    qcols = slice(half * WINDOW, (half + 1) * WINDOW)
                    krow = pos * 2 * WINDOW
                    st = scores[kh]
                    s = (jnp.where(own, st[krow + WINDOW:krow + 2 * WINDOW, qcols],
                                   st[krow:krow + WINDOW, qcols]) * c_scale
                         - (slope * log2e) * dist)
                    if sb == 0:
                        s = jnp.where(visible0, s, -jnp.inf)
                    m = jnp.max(s, axis=0, keepdims=True)
                    p = jnp.exp2(s - m)
                    den = jnp.sum(p, axis=0, keepdims=True) + jnp.exp2(sink_ref[h] * log2e - m)
                    rd.append(1.0 / den)
                    pr.append(jnp.concatenate(
                        [jnp.where(own, 0.0, p).astype(BF16), jnp.where(own, p, 0.0).astype(BF16)],
                        axis=0))
                probs.append(jnp.concatenate(pr, axis=1))
                rdens.append(jnp.concatenate(rd, axis=1))
        out_t = [None] * N_HEADS
        for kh in range(N_KV_HEADS):
            for pos in range(2):
                o2 = jnp.dot(vcats[kh], probs[2 * kh + pos], preferred_element_type=F32)
                o2 = o2 * rdens[2 * kh + pos]
                out_t[kh * GROUP + pos] = o2[:, :WINDOW]
                out_t[kh * GROUP + 2 + pos] = o2[:, WINDOW:]
        o_ref[rows, :] = jnp.transpose(jnp.concatenate(out_t, axis=0)).astype(BF16)
        prev_k, prev_vt = cur_k, cur_vt


def _attention(proj, sinks, seq):
    t = proj.shape[0]
    sub = TQ_ATTN // WINDOW
    return pl.pallas_call(
        functools.partial(_attn_kernel, tiles_per_seq=seq // TQ_ATTN),
        name="swattn",
        grid=(t // TQ_ATTN,),
        in_specs=[
            pl.BlockSpec(memory_space=pltpu.SMEM),
            pl.BlockSpec((TQ_ATTN, D_MODEL), lambda i: (i, COL_Q)),
            pl.BlockSpec((TQ_ATTN, KV_WIDTH), lambda i: (i, COL_K)),
            pl.BlockSpec((TQ_ATTN, KV_WIDTH), lambda i: (i, COL_V)),
            pl.BlockSpec((WINDOW, KV_WIDTH), lambda i: (jnp.maximum(i * sub - 1, 0), COL_K)),
            pl.BlockSpec((WINDOW, KV_WIDTH), lambda i: (jnp.maximum(i * sub - 1, 0), COL_V)),
        ],
        out_specs=pl.BlockSpec((TQ_ATTN, D_MODEL), lambda i: (i, 0)),
        out_shape=jax.ShapeDtypeStruct((t, D_MODEL), BF16),
        compiler_params=pltpu.CompilerParams(
            dimension_semantics=("arbitrary",), vmem_limit_bytes=VMEM_LIMIT),
    )(sinks, proj, proj, proj, proj, proj)


def _mix_kernel(attn_ref, gb_ref, za_ref, x_ref, wb_ref, wo_ref, g_ref, wr_ref, br_ref,
                xmid_ref, h_ref, route_ref, cnt_ref):
    subs = [slice(s * SUB_MIX, (s + 1) * SUB_MIX) for s in range(TM_MIX // SUB_MIX)]
    yb = [jnp.dot(attn_ref[r, :], wb_ref[...], preferred_element_type=F32) for r in subs]
    merged = [(za_ref[r, :].astype(F32) + jax.nn.sigmoid(gb_ref[r, :].astype(F32)) * y).astype(BF16)
              for r, y in zip(subs, yb)]
    xm = [x_ref[r, :] + jnp.dot(m, wo_ref[...], preferred_element_type=F32)
          for r, m in zip(subs, merged)]
    hs = []
    for r, v in zip(subs, xm):
        xmid_ref[r, :] = v
        h = _rms(v, g_ref[...])
        h_ref[r, :] = _pack_bf16_pairs(h)
        hs.append(h)
    wr = wr_ref[...]
    logits = []
    for h in hs:
        h_hi = h.astype(BF16)
        h_lo = (h - h_hi.astype(F32)).astype(BF16)
        both = jnp.dot(h_hi, wr, preferred_element_type=F32)
        logits.append(both[:, :LANES] + both[:, LANES:]
                      + jnp.dot(h_lo, wr[:, :LANES], preferred_element_type=F32) + br_ref[...])
    cnt = jnp.zeros((1, LANES), F32)
    for r, lg in zip(subs, logits):
        route, onehot = _route(lg)
        route_ref[r, :] = route
        cnt = cnt + jnp.sum(onehot, axis=0, keepdims=True)
    cnt_ref[...] = jnp.broadcast_to(cnt, cnt_ref.shape)


def _route(logits):
    lane = lax.broadcasted_iota(I32, logits.shape, 1)
    neg = -jnp.inf
    gl = jnp.where(lane < N_GROUPS, logits, neg)
    gmax = jnp.max(gl, axis=-1, keepdims=True)
    g_idx = jnp.min(jnp.where(gl == gmax, lane, LANES), axis=-1, keepdims=True)
    p_g = 1.0 / jnp.sum(jnp.exp(gl - gmax), axis=-1, keepdims=True)
    start = N_GROUPS + EXPERTS_PER_GROUP * g_idx
    el = jnp.where((lane >= start) & (lane < start + EXPERTS_PER_GROUP), logits, neg)
    v1 = jnp.max(el, axis=-1, keepdims=True)
    i1 = jnp.min(jnp.where(el == v1, lane, LANES), axis=-1, keepdims=True)
    el2 = jnp.where(lane == i1, neg, el)
    v2 = jnp.max(el2, axis=-1, keepdims=True)
    i2 = jnp.min(jnp.where(el2 == v2, lane, LANES), axis=-1, keepdims=True)
    e21 = jnp.exp(v2 - v1)
    w1 = p_g / (1.0 + e21)
    w2 = p_g * e21 / (1.0 + e21)
    e1 = i1 - N_GROUPS
    e2 = i2 - N_GROUPS
    route = jnp.where(lane == 0, e1.astype(F32),
                      jnp.where(lane == 1, e2.astype(F32),
                                jnp.where(lane == 2, w1, jnp.where(lane == 3, w2, 0.0))))
    onehot = ((lane == e1) | (lane == e2)).astype(F32)
    return route, onehot


def _mix(attn, proj, za, xf, wb, wo, g, wr, br, chunk, t):
    n_tiles = t // TM_MIX
    first = chunk * n_tiles
    full = lambda shape: pl.BlockSpec(shape, lambda i: (0, 0))
    tile = lambda w=D_MODEL: pl.BlockSpec((TM_MIX, w), lambda i: (i, 0))
    src = lambda c=0: pl.BlockSpec((TM_MIX, D_MODEL), lambda i: (first + i, c))
    return pl.pallas_call(
        _mix_kernel,
        name="merge_router",
        grid=(n_tiles,),
        in_specs=[
            src(), src(COL_GB), src(), src(),
            full((D_MODEL, D_MODEL)), full((D_MODEL, D_MODEL)), full((1, D_MODEL)),
            full((D_MODEL, 2 * LANES)), full((1, LANES)),
        ],
        out_specs=[tile(), tile(PACKED), tile(LANES), pl.BlockSpec((8, LANES), lambda i: (i, 0))],
        out_shape=[
            jax.ShapeDtypeStruct((t, D_MODEL), F32),
            jax.ShapeDtypeStruct((t, PACKED), I32),
            jax.ShapeDtypeStruct((t, LANES), F32),
            jax.ShapeDtypeStruct((n_tiles * 8, LANES), F32),
        ],
        compiler_params=pltpu.CompilerParams(
            dimension_semantics=("arbitrary",), vmem_limit_bytes=VMEM_LIMIT),
    )(attn, proj, za, xf, wb, wo, g, wr, br)


def _pos_kernel(route_ref, base_ref, pos_ref):
    route = route_ref[...]
    lane = lax.broadcasted_iota(I32, route.shape, 1)
    e1 = jnp.sum(jnp.where(lane == 0, route, 0.0), axis=-1, keepdims=True).astype(I32)
    e2 = jnp.sum(jnp.where(lane == 1, route, 0.0), axis=-1, keepdims=True).astype(I32)
    onehot = ((lane == e1) | (lane == e2)).astype(BF16)
    r = lax.broadcasted_iota(I32, (TM_MIX, TM_MIX), 0)
    c = lax.broadcasted_iota(I32, (TM_MIX, TM_MIX), 1)
    lower = (c < r).astype(BF16)
    before = jnp.dot(lower, onehot, preferred_element_type=F32) + base_ref[0]
    p1 = jnp.sum(jnp.where(lane == e1, before, 0.0), axis=-1, keepdims=True)
    p2 = jnp.sum(jnp.where(lane == e2, before, 0.0), axis=-1, keepdims=True)
    packed = jnp.where(lane == 0, p1, jnp.where(lane == 1, p2, 0.0))
    pos_ref[...] = jnp.transpose(packed)[0:TOP_K, :].astype(I32)


def _positions(route, base):
    t = route.shape[0]
    n_tiles = t // TM_MIX
    return pl.pallas_call(
        _pos_kernel,
        name="positions",
        grid=(n_tiles,),
        in_specs=[
            pl.BlockSpec((TM_MIX, LANES), lambda i: (i, 0)),
            pl.BlockSpec((1, 1, LANES), lambda i: (i, 0, 0)),
        ],
        out_specs=pl.BlockSpec((TOP_K, TM_MIX), lambda i: (0, i)),
        out_shape=jax.ShapeDtypeStruct((TOP_K, t), I32),
        compiler_params=pltpu.CompilerParams(dimension_semantics=("arbitrary",)),
    )(route, base)


def _sc_mesh():
    return plsc.VectorSubcoreMesh(core_axis_name="c", subcore_axis_name="s",
                                  num_cores=SC_CORES, num_subcores=SC_SUBCORES)


def _sc_worker():
    return lax.axis_index("s") * SC_CORES + lax.axis_index("c")


def _dispatch(pos, hp, rows):
    t = hp.shape[0]
    per_w = t // SC_WORKERS
    n_ch = per_w // SC_CHUNK
    pos4 = pos.reshape(TOP_K, SC_WORKERS, n_ch, SC_CHUNK)

    @functools.partial(
        pl.kernel, mesh=_sc_mesh(),
        out_type=jax.ShapeDtypeStruct((rows, PACKED), I32),
        scratch_types=[pltpu.VMEM((TOP_K, n_ch, SC_CHUNK), I32),
                       pltpu.VMEM((SC_CHUNK, PACKED), I32)])
    def scatter(hp_hbm, pos_hbm, xs_hbm, idx_v, rows_v):
        wid = _sc_worker()
        for k in range(TOP_K):
            pltpu.sync_copy(pos_hbm.at[k, wid], idx_v.at[k])

        def body(c, carry):
            start = pl.multiple_of(wid * per_w + c * SC_CHUNK, SC_CHUNK)
            pltpu.sync_copy(hp_hbm.at[pl.ds(start, SC_CHUNK)], rows_v)
            for k in range(TOP_K):
                pltpu.sync_copy(rows_v, xs_hbm.at[idx_v.at[k, c]])
            return carry

        lax.fori_loop(0, n_ch, body, 0)

    return scatter(hp, pos4)


def _gather_rows(table, idx):
    n = idx.shape[0]
    per_w = n // SC_WORKERS
    n_ch = per_w // SC_CHUNK
    idx3 = idx.reshape(SC_WORKERS, n_ch, SC_CHUNK)

    @functools.partial(
        pl.kernel, mesh=_sc_mesh(),
        out_type=jax.ShapeDtypeStruct((n, PACKED), I32),
        scratch_types=[pltpu.VMEM((n_ch, SC_CHUNK), I32),
                       pltpu.VMEM((SC_CHUNK, PACKED), I32)])
    def gather(table_hbm, idx_hbm, out_hbm, idx_v, rows_v):
        wid = _sc_worker()
        pltpu.sync_copy(idx_hbm.at[wid], idx_v)

        def body(c, carry):
            start = pl.multiple_of(wid * per_w + c * SC_CHUNK, SC_CHUNK)
            pltpu.sync_copy(table_hbm.at[idx_v.at[c]], rows_v)
            pltpu.sync_copy(rows_v, out_hbm.at[pl.ds(start, SC_CHUNK)])
            return carry

        lax.fori_loop(0, n_ch, body, 0)

    return gather(table, idx3)


def _expert_kernel(te_ref, nx_ref, sl_ref, ts_ref, tv_ref, xs_ref, wg_hbm, wu_hbm, wd_hbm, o_ref,
                   stage_g, stage_u, stage_d, wg_s, wu_s, wd_s, sem):
    del ts_ref
    i = pl.program_id(0)
    n_valid = tv_ref[i]
    expert = te_ref[i]
    slot = sl_ref[i]

    def weight_copies(e, s):
        return (pltpu.make_async_copy(wg_hbm.at[e], stage_g.at[s], sem.at[s, 0]),
                pltpu.make_async_copy(wu_hbm.at[e], stage_u.at[s], sem.at[s, 1]),
                pltpu.make_async_copy(wd_hbm.at[e], stage_d.at[s], sem.at[s, 2]))

    @pl.when(i == 0)
    def _():
        for copy in weight_copies(expert, slot):
            copy.start()

    @pl.when(jnp.logical_or(i == 0, expert != te_ref[jnp.maximum(i - 1, 0)]))
    def _():
        for copy in weight_copies(expert, slot):
            copy.wait()
        nxt = nx_ref[i]

        @pl.when(nxt >= 0)
        def _():
            for copy in weight_copies(nxt, 1 - slot):
                copy.start()

        wg_s[...] = stage_g[slot].astype(BF16)
        wu_s[...] = stage_u[slot].astype(BF16)
        wd_s[...] = stage_d[slot].astype(BF16)

    @pl.when(n_valid > 0)
    def _():
        subs = [slice(s * SUB_EXP, (s + 1) * SUB_EXP) for s in range(TM_EXP // SUB_EXP)]
        xin = []
        for r in subs:
            rid = r.start + lax.broadcasted_iota(I32, (SUB_EXP, PACKED), 0)
            lo, hi = _unpack_bf16_pairs(jnp.where(rid < n_valid, xs_ref[r, :], 0))
            xin.append((lo.astype(BF16), hi.astype(BF16)))
        a = [jnp.dot(lo, wg_s[:PACKED, :], preferred_element_type=F32)
             + jnp.dot(hi, wg_s[PACKED:, :], preferred_element_type=F32) for lo, hi in xin]
        b = [jnp.dot(lo, wu_s[:PACKED, :], preferred_element_type=F32)
             + jnp.dot(hi, wu_s[PACKED:, :], preferred_element_type=F32) for lo, hi in xin]
        hmid = [(ai * jax.nn.sigmoid(ai) * bi).astype(BF16) for ai, bi in zip(a, b)]
        for r, hm in zip(subs, hmid):
            o_ref[r, :] = _pack_bf16_pairs(jnp.dot(hm, wd_s[...], preferred_element_type=F32))

    @pl.when(n_valid <= 0)
    def _():
        o_ref[...] = jnp.zeros(o_ref.shape, o_ref.dtype)


def _experts(tile_expert, next_expert, tile_slot, tile_src, tile_valid, xs, wg, wu, wd):
    rows = xs.shape[0]
    hbm = pl.BlockSpec(memory_space=pl.ANY)
    grid_spec = pltpu.PrefetchScalarGridSpec(
        num_scalar_prefetch=5,
        grid=(rows // TM_EXP,),
        in_specs=[pl.BlockSpec((TM_EXP, PACKED), lambda i, te, nx, sl, ts, tv: (ts[i], 0)),
                  hbm, hbm, hbm],
        out_specs=pl.BlockSpec((TM_EXP, PACKED), lambda i, te, nx, sl, ts, tv: (i, 0)),
        scratch_shapes=[
            pltpu.VMEM((2, D_MODEL, D_FF), F32), pltpu.VMEM((2, D_MODEL, D_FF), F32),
            pltpu.VMEM((2, D_FF, D_MODEL), F32),
            pltpu.VMEM((D_MODEL, D_FF), BF16), pltpu.VMEM((D_MODEL, D_FF), BF16),
            pltpu.VMEM((D_FF, D_MODEL), BF16),
            pltpu.SemaphoreType.DMA((2, 3)),
        ],
    )
    return pl.pallas_call(
        _expert_kernel,
        name="experts",
        grid_spec=grid_spec,
        out_shape=jax.ShapeDtypeStruct((rows, PACKED), I32),
        compiler_params=pltpu.CompilerParams(
            dimension_semantics=("arbitrary",), vmem_limit_bytes=VMEM_LIMIT),
    )(tile_expert, next_expert, tile_slot, tile_src, tile_valid, xs, wg, wu, wd)


def _combine_kernel(y1_ref, y2_ref, route_ref, xmid_ref, g_ref, *rest):
    o_ref = rest[-1]
    route = route_ref[...]
    lane = lax.broadcasted_iota(I32, route.shape, 1)
    w1 = jnp.sum(jnp.where(lane == 2, route, 0.0), axis=-1, keepdims=True)
    w2 = jnp.sum(jnp.where(lane == 3, route, 0.0), axis=-1, keepdims=True)
    lo1, hi1 = _unpack_bf16_pairs(y1_ref[...])
    lo2, hi2 = _unpack_bf16_pairs(y2_ref[...])
    y = jnp.concatenate([lo1 * w1 + lo2 * w2, hi1 * w1 + hi2 * w2], axis=1)
    o_ref[...] = _rms(xmid_ref[...] + y, g_ref[...])


def _combine(yg, route, xmid, g, chunk, t_total, out_prev):
    t = xmid.shape[0]
    n_tiles = t // TM_CMB
    first = chunk * n_tiles
    in_specs = [
        pl.BlockSpec((TM_CMB, PACKED), lambda i: (i, 0)),
        pl.BlockSpec((TM_CMB, PACKED), lambda i: (n_tiles + i, 0)),
        pl.BlockSpec((TM_CMB, LANES), lambda i: (i, 0)),
        pl.BlockSpec((TM_CMB, D_MODEL), lambda i: (i, 0)),
        pl.BlockSpec((1, D_MODEL), lambda i: (0, 0)),
    ]
    args = [yg, yg, route, xmid, g]
    aliases = {}
    if out_prev is not None:
        in_specs.append(pl.BlockSpec(memory_space=pl.ANY))
        aliases = {len(args): 0}
        args.append(out_prev)
    return pl.pallas_call(
        _combine_kernel,
        name="combine",
        grid=(n_tiles,),
        in_specs=in_specs,
        out_specs=pl.BlockSpec((TM_CMB, D_MODEL), lambda i: (first + i, 0)),
        out_shape=jax.ShapeDtypeStruct((t_total, D_MODEL), F32),
        input_output_aliases=aliases,
        compiler_params=pltpu.CompilerParams(
            dimension_semantics=("arbitrary",), vmem_limit_bytes=VMEM_LIMIT),
    )(*args)


def _split_bf16(w):
    hi = w.astype(BF16)
    lo = (w - hi.astype(F32)).astype(BF16)
    return hi, lo


def kernel(x, norm_mix, w_in, conv_w, conv_b, w_a_out, sinks, w_b_out, w_o, norm_ffn, w_group,
           b_group, w_expert, b_expert, w_gate, w_up, w_down, norm_final):
    bsz, seq, d = x.shape
    t = bsz * seq
    assert d == D_MODEL and seq % TM_PROJ == 0 and seq % TQ_ATTN == 0
    xf = x.reshape(t, d)
    row = lambda v: v.reshape(1, -1)

    w_b, w_c, w_u, w_q, w_k, w_v, w_ga, w_gb = jnp.split(w_in, REF_SPLITS, axis=1)
    w_cu = jnp.concatenate([w_c, w_u], axis=1).astype(BF16)
    w_bg = jnp.concatenate([w_b, w_ga], axis=1).astype(BF16)
    w_rest = jnp.concatenate([w_gb, w_q, w_k, w_v], axis=1).astype(BF16)

    za, proj = _inproj(xf, row(norm_mix), w_cu, w_bg, w_rest, conv_w, row(conv_b),
                       w_a_out.astype(BF16), seq)
    attn = _attention(proj, sinks, seq)

    pad = LANES - N_GROUPS - N_EXPERTS
    w_r = jnp.concatenate([w_group, w_expert, jnp.zeros((d, pad), F32)], axis=1)
    b_r = jnp.concatenate([b_group, b_expert, jnp.zeros((pad,), F32)]).reshape(1, LANES)
    wr = jnp.concatenate(_split_bf16(w_r), axis=1)
    wb = w_b_out.astype(BF16)
    wo = w_o.astype(BF16)

    t_chunk = t // MOE_CHUNKS
    out = None
    for chunk in range(MOE_CHUNKS):
        xmid, h2, route, cnt = _mix(attn, proj, za, xf, wb, wo, row(norm_ffn), wr, b_r, chunk, t_chunk)
        out = _moe_chunk(xmid, h2, route, cnt, w_gate, w_up, w_down, row(norm_final), chunk, t, out)
    return out.reshape(bsz, seq, d)


def _moe_chunk(xmid, h2, route, cnt, w_gate, w_up, w_down, g_final, chunk, t_total, out_prev):
    t = xmid.shape[0]
    n_tiles = t // TM_MIX
    cnt = cnt.reshape(n_tiles, 8, LANES)[:, 0, :N_EXPERTS].astype(I32)
    totals = jnp.sum(cnt, axis=0)
    tiles_e = (totals + TM_EXP - 1) // TM_EXP
    tile_end = jnp.cumsum(tiles_e)
    offset = (tile_end - tiles_e) * TM_EXP
    base = offset[None, :] + jnp.cumsum(cnt, axis=0) - cnt
    base = jnp.pad(base, ((0, 0), (0, LANES - N_EXPERTS))).astype(F32).reshape(n_tiles, 1, LANES)
    rows = t * TOP_K + N_EXPERTS * TM_EXP
    n_active = tile_end[-1]
    tile_id = jnp.arange(rows // TM_EXP, dtype=I32)
    tile_src = jnp.minimum(tile_id, n_active - 1)
    tile_expert = jnp.sum((tile_src[:, None] >= tile_end[None, :]).astype(I32), axis=1)
    tile_expert = jnp.minimum(tile_expert, N_EXPERTS - 1)
    row_in_expert = (tile_id - (tile_end - tiles_e)[tile_expert]) * TM_EXP
    tile_valid = jnp.clip(totals[tile_expert] - row_in_expert, 0, TM_EXP)
    tile_valid = jnp.where(tile_id < n_active, tile_valid, 0).astype(I32)
    after = tile_end[tile_expert]
    next_expert = jnp.where(after < n_active, tile_expert[jnp.minimum(after, n_active - 1)], -1)
    first_of_expert = jnp.concatenate(
        [jnp.ones((1,), I32), (tile_expert[1:] != tile_expert[:-1]).astype(I32)])
    tile_slot = (jnp.cumsum(first_of_expert) - 1) % 2

    pos = _positions(route, base)
    xs = _dispatch(pos, h2, rows)
    ys = _experts(tile_expert.astype(I32), next_expert.astype(I32), tile_slot.astype(I32),
                  tile_src.astype(I32), tile_valid, xs, w_gate, w_up, w_down)
    yg = _gather_rows(ys, pos.reshape(TOP_K * t))
    return _combine(yg, route, xmid, g_final, chunk, t_total, out_prev)
```

```python
import functools
import math

import jax
import jax.numpy as jnp
from jax import lax
from jax.experimental import pallas as pl
from jax.experimental.pallas import tpu as pltpu
from jax.experimental.pallas import tpu_sc as plsc

F32 = jnp.float32
BF16 = jnp.bfloat16
I32 = jnp.int32

D_MODEL = 1024
HEAD_DIM = 64
N_HEADS = 16
N_KV_HEADS = 4
GROUP = N_HEADS // N_KV_HEADS
KV_WIDTH = N_KV_HEADS * HEAD_DIM
WINDOW = 128
N_GROUPS = 4
EXPERTS_PER_GROUP = 8
N_EXPERTS = N_GROUPS * EXPERTS_PER_GROUP
TOP_K = 2
D_FF = 512
EPS = 1e-6
LANES = 128

REF_SPLITS = (1024, 2048, 3072, 4096, 4352, 4608, 5632)
REST_COLS = 2 * D_MODEL + 2 * KV_WIDTH
COL_GB, COL_Q = 0, 1
COL_K, COL_V = 2 * D_MODEL // KV_WIDTH, 2 * D_MODEL // KV_WIDTH + 1

TM_PROJ = 512
TQ_ATTN = 512
TM_MIX = 512
SUB_MIX = 256
MOE_CHUNKS = 2
TM_EXP = 512
SUB_EXP = 256
TM_CMB = TM_MIX
HALO_ROWS = 8
VMEM_LIMIT = 56 * 1024 * 1024
PACKED = D_MODEL // 2

SC_CORES = 2
SC_SUBCORES = 16
SC_WORKERS = SC_CORES * SC_SUBCORES
SC_CHUNK = 64


def _rms(x, g):
    r = lax.rsqrt(jnp.mean(x * x, axis=-1, keepdims=True) + EPS)
    return (x * r) * g


def _pack_bf16_pairs(x):
    n = x.shape[1] // 2
    lo = lax.bitcast_convert_type(x[:, :n].astype(BF16).astype(F32), I32)
    hi = lax.bitcast_convert_type(x[:, n:].astype(BF16).astype(F32), I32)
    return (hi & jnp.int32(-65536)) | lax.shift_right_logical(lo, 16)


def _unpack_bf16_pairs(p):
    lo = lax.bitcast_convert_type(lax.shift_left(p, 16), F32)
    hi = lax.bitcast_convert_type(p & jnp.int32(-65536), F32)
    return lo, hi


def _inproj_kernel(x_ref, g_ref, wcu_ref, wbg_ref, wrest_ref, cw_ref, cb_ref, wa_ref,
                   za_ref, proj_ref, halo_ref, *, tiles_per_seq):
    h = _rms(x_ref[...], g_ref[...]).astype(BF16)
    pcu = jnp.dot(h, wcu_ref[...], preferred_element_type=F32)
    cu = pcu[:, :D_MODEL] * pcu[:, D_MODEL:]
    first = (pl.program_id(0) % tiles_per_seq) == 0
    hist = jnp.where(first, 0.0, halo_ref[...])
    prev1 = hist[HALO_ROWS - 1:HALO_ROWS]
    prev2 = hist[HALO_ROWS - 2:HALO_ROWS - 1]
    halo_ref[...] = cu[TM_PROJ - HALO_ROWS:, :]
    row = lax.broadcasted_iota(I32, cu.shape, 0)
    cu1 = jnp.where(row == 0, prev1, pltpu.roll(cu, 1, 0))
    cu2 = jnp.where(row == 0, prev2, jnp.where(row == 1, prev1, pltpu.roll(cu, 2, 0)))
    cw = cw_ref[...]
    y = cw[0:1] * cu2 + cw[1:2] * cu1 + cw[2:3] * cu + cb_ref[...]
    pbg = jnp.dot(h, wbg_ref[...], preferred_element_type=F32)
    ya = (pbg[:, :D_MODEL] * y).astype(BF16)
    z = jnp.dot(ya, wa_ref[...], preferred_element_type=F32)
    za_ref[...] = (jax.nn.sigmoid(pbg[:, D_MODEL:]) * z).astype(BF16)
    proj_ref[...] = jnp.dot(h, wrest_ref[...], preferred_element_type=F32).astype(BF16)


def _inproj(xf, g, w_cu, w_bg, w_rest, conv_w, conv_b, wa, seq):
    t = xf.shape[0]
    const = lambda shape: pl.BlockSpec(shape, lambda i: (0, 0), pipeline_mode=pl.Buffered(1))
    return pl.pallas_call(
        functools.partial(_inproj_kernel, tiles_per_seq=seq // TM_PROJ),
        name="inproj_conv",
        grid=(t // TM_PROJ,),
        in_specs=[
            pl.BlockSpec((TM_PROJ, D_MODEL), lambda i: (i, 0)),
            const((1, D_MODEL)),
            const((D_MODEL, 2 * D_MODEL)), const((D_MODEL, 2 * D_MODEL)), const((D_MODEL, REST_COLS)),
            const((3, D_MODEL)), const((1, D_MODEL)), const((D_MODEL, D_MODEL)),
        ],
        out_specs=[pl.BlockSpec((TM_PROJ, D_MODEL), lambda i: (i, 0)),
                   pl.BlockSpec((TM_PROJ, REST_COLS), lambda i: (i, 0))],
        out_shape=[jax.ShapeDtypeStruct((t, D_MODEL), BF16),
                   jax.ShapeDtypeStruct((t, REST_COLS), BF16)],
        scratch_shapes=[pltpu.VMEM((HALO_ROWS, D_MODEL), F32)],
        compiler_params=pltpu.CompilerParams(
            dimension_semantics=("arbitrary",), vmem_limit_bytes=VMEM_LIMIT),
    )(xf, g, w_cu, w_bg, w_rest, conv_w, conv_b, wa)


def _attn_kernel(sink_ref, q_ref, k_ref, v_ref, kp_ref, vp_ref, o_ref, *, tiles_per_seq):
    first_tile = (pl.program_id(0) % tiles_per_seq) == 0
    ks = lax.broadcasted_iota(I32, (WINDOW, WINDOW), 0)
    qq = lax.broadcasted_iota(I32, (WINDOW, WINDOW), 1)
    own = ks <= qq
    dist = jnp.where(own, qq - ks, qq - ks + WINDOW).astype(F32)
    visible0 = jnp.logical_or(own, jnp.logical_not(first_tile))
    log2e = math.log2(math.e)
    c_scale = log2e / math.sqrt(HEAD_DIM)
    nt = (((1,), (1,)), ((), ()))
    zk = jnp.zeros((2 * WINDOW, HEAD_DIM), BF16)

    def transposed(v_blk):
        return jnp.transpose(v_blk.astype(F32)).astype(BF16)

    prev_k = kp_ref[...]
    prev_vt = transposed(vp_ref[...])
    for sb in range(TQ_ATTN // WINDOW):
        rows = slice(sb * WINDOW, (sb + 1) * WINDOW)
        cur_k = k_ref[rows, :]
        cur_vt = transposed(v_ref[rows, :])
        scores, vcats = [], []
        for kh in range(N_KV_HEADS):
            cols = slice(kh * HEAD_DIM, (kh + 1) * HEAD_DIM)
            kcat = jnp.concatenate([prev_k[:, cols], cur_k[:, cols]], axis=0)
            vcats.append(jnp.concatenate([prev_vt[cols, :], cur_vt[cols, :]], axis=1))
            qg = jnp.concatenate([q_ref[rows, (2 * kh) * LANES:(2 * kh + 1) * LANES],
                                  q_ref[rows, (2 * kh + 1) * LANES:(2 * kh + 2) * LANES]], axis=0)
            k_pad = jnp.concatenate([jnp.concatenate([kcat, zk], axis=1),
                                     jnp.concatenate([zk, kcat], axis=1)], axis=0)
            scores.append(lax.dot_general(k_pad, qg, nt, preferred_element_type=F32))
        probs, rdens = [], []
        for kh in range(N_KV_HEADS):
            for pos in range(2):
                pr, rd = [], []
                for half in range(2):
                    h = kh * GROUP + 2 * half + pos
                    slope = 2.0 ** (-8.0 * (h + 1) / N_HEADS)
                    qcols = slice(half * WINDOW, (half + 1) * WINDOW)
                    krow = pos * 2 * WINDOW
                    st = scores[kh]
                    s = (jnp.where(own, st[krow + WINDOW:krow + 2 * WINDOW, qcols],
                                   st[krow:krow + WINDOW, qcols]) * c_scale
                         - (slope * log2e) * dist)
                    if sb == 0:
                        s = jnp.where(visible0, s, -jnp.inf)
                    m = jnp.max(s, axis=0, keepdims=True)
                    p = jnp.exp2(s - m)
                    den = jnp.sum(p, axis=0, keepdims=True) + jnp.exp2(sink_ref[h] * log2e - m)
                    rd.append(1.0 / den)
                    pr.append(jnp.concatenate(
                        [jnp.where(own, 0.0, p).astype(BF16), jnp.where(own, p, 0.0).astype(BF16)],
                        axis=0))
                probs.append(jnp.concatenate(pr, axis=1))
                rdens.append(jnp.concatenate(rd, axis=1))
        out_t = [None] * N_HEADS
        for kh in range(N_KV_HEADS):
            for pos in range(2):
                o2 = jnp.dot(vcats[kh], probs[2 * kh + pos], preferred_element_type=F32)
                o2 = o2 * rdens[2 * kh + pos]
                out_t[kh * GROUP + pos] = o2[:, :WINDOW]
                out_t[kh * GROUP + 2 + pos] = o2[:, WINDOW:]
        o_ref[rows, :] = jnp.transpose(jnp.concatenate(out_t, axis=0)).astype(BF16)
        prev_k, prev_vt = cur_k, cur_vt


def _attention(proj, sinks, seq):
    t = proj.shape[0]
    sub = TQ_ATTN // WINDOW
    return pl.pallas_call(
        functools.partial(_attn_kernel, tiles_per_seq=seq // TQ_ATTN),
        name="swattn",
        grid=(t // TQ_ATTN,),
        in_specs=[
            pl.BlockSpec(memory_space=pltpu.SMEM),
            pl.BlockSpec((TQ_ATTN, D_MODEL), lambda i: (i, COL_Q)),
            pl.BlockSpec((TQ_ATTN, KV_WIDTH), lambda i: (i, COL_K)),
            pl.BlockSpec((TQ_ATTN, KV_WIDTH), lambda i: (i, COL_V)),
            pl.BlockSpec((WINDOW, KV_WIDTH), lambda i: (jnp.maximum(i * sub - 1, 0), COL_K)),
            pl.BlockSpec((WINDOW, KV_WIDTH), lambda i: (jnp.maximum(i * sub - 1, 0), COL_V)),
        ],
        out_specs=pl.BlockSpec((TQ_ATTN, D_MODEL), lambda i: (i, 0)),
        out_shape=jax.ShapeDtypeStruct((t, D_MODEL), BF16),
        compiler_params=pltpu.CompilerParams(
            dimension_semantics=("arbitrary",), vmem_limit_bytes=VMEM_LIMIT),
    )(sinks, proj, proj, proj, proj, proj)


def _mix_kernel(attn_ref, gb_ref, za_ref, x_ref, wb_ref, wo_ref, g_ref, wr_ref, br_ref,
                xmid_ref, h_ref, route_ref, cnt_ref):
    subs = [slice(s * SUB_MIX, (s + 1) * SUB_MIX) for s in range(TM_MIX // SUB_MIX)]
    yb = [jnp.dot(attn_ref[r, :], wb_ref[...], preferred_element_type=F32) for r in subs]
    merged = [(za_ref[r, :].astype(F32) + jax.nn.sigmoid(gb_ref[r, :].astype(F32)) * y).astype(BF16)
              for r, y in zip(subs, yb)]
    xm = [x_ref[r, :] + jnp.dot(m, wo_ref[...], preferred_element_type=F32)
          for r, m in zip(subs, merged)]
    hs = []
    for r, v in zip(subs, xm):
        xmid_ref[r, :] = v
        h = _rms(v, g_ref[...])
        h_ref[r, :] = _pack_bf16_pairs(h)
        hs.append(h)
    wr = wr_ref[...]
    logits = []
    for h in hs:
        h_hi = h.astype(BF16)
        h_lo = (h - h_hi.astype(F32)).astype(BF16)
        both = jnp.dot(h_hi, wr, preferred_element_type=F32)
        logits.append(both[:, :LANES] + both[:, LANES:]
                      + jnp.dot(h_lo, wr[:, :LANES], preferred_element_type=F32) + br_ref[...])
    cnt = jnp.zeros((1, LANES), F32)
    for r, lg in zip(subs, logits):
        route, onehot = _route(lg)
        route_ref[r, :] = route
        cnt = cnt + jnp.sum(onehot, axis=0, keepdims=True)
    cnt_ref[...] = jnp.broadcast_to(cnt, cnt_ref.shape)


def _route(logits):
    lane = lax.broadcasted_iota(I32, logits.shape, 1)
    neg = -jnp.inf
    gl = jnp.where(lane < N_GROUPS, logits, neg)
    gmax = jnp.max(gl, axis=-1, keepdims=True)
    g_idx = jnp.min(jnp.where(gl == gmax, lane, LANES), axis=-1, keepdims=True)
    p_g = 1.0 / jnp.sum(jnp.exp(gl - gmax), axis=-1, keepdims=True)
    start = N_GROUPS + EXPERTS_PER_GROUP * g_idx
    el = jnp.where((lane >= start) & (lane < start + EXPERTS_PER_GROUP), logits, neg)
    v1 = jnp.max(el, axis=-1, keepdims=True)
    i1 = jnp.min(jnp.where(el == v1, lane, LANES), axis=-1, keepdims=True)
    el2 = jnp.where(lane == i1, neg, el)
    v2 = jnp.max(el2, axis=-1, keepdims=True)
    i2 = jnp.min(jnp.where(el2 == v2, lane, LANES), axis=-1, keepdims=True)
    e21 = jnp.exp(v2 - v1)
    w1 = p_g / (1.0 + e21)
    w2 = p_g * e21 / (1.0 + e21)
    e1 = i1 - N_GROUPS
    e2 = i2 - N_GROUPS
    route = jnp.where(lane == 0, e1.astype(F32),
                      jnp.where(lane == 1, e2.astype(F32),
                                jnp.where(lane == 2, w1, jnp.where(lane == 3, w2, 0.0))))
    onehot = ((lane == e1) | (lane == e2)).astype(F32)
    return route, onehot


def _mix(attn, proj, za, xf, wb, wo, g, wr, br, chunk, t):
    n_tiles = t // TM_MIX
    first = chunk * n_tiles
    full = lambda shape: pl.BlockSpec(shape, lambda i: (0, 0))
    tile = lambda w=D_MODEL: pl.BlockSpec((TM_MIX, w), lambda i: (i, 0))
    src = lambda c=0: pl.BlockSpec((TM_MIX, D_MODEL), lambda i: (first + i, c))
    return pl.pallas_call(
        _mix_kernel,
        name="merge_router",
        grid=(n_tiles,),
        in_specs=[
            src(), src(COL_GB), src(), src(),
            full((D_MODEL, D_MODEL)), full((D_MODEL, D_MODEL)), full((1, D_MODEL)),
            full((D_MODEL, 2 * LANES)), full((1, LANES)),
        ],
        out_specs=[tile(), tile(PACKED), tile(LANES), pl.BlockSpec((8, LANES), lambda i: (i, 0))],
        out_shape=[
            jax.ShapeDtypeStruct((t, D_MODEL), F32),
            jax.ShapeDtypeStruct((t, PACKED), I32),
            jax.ShapeDtypeStruct((t, LANES), F32),
            jax.ShapeDtypeStruct((n_tiles * 8, LANES), F32),
        ],
        compiler_params=pltpu.CompilerParams(
            dimension_semantics=("arbitrary",), vmem_limit_bytes=VMEM_LIMIT),
    )(attn, proj, za, xf, wb, wo, g, wr, br)


def _pos_kernel(route_ref, base_ref, pos_ref):
    route = route_ref[...]
    lane = lax.broadcasted_iota(I32, route.shape, 1)
    e1 = jnp.sum(jnp.where(lane == 0, route, 0.0), axis=-1, keepdims=True).astype(I32)
    e2 = jnp.sum(jnp.where(lane == 1, route, 0.0), axis=-1, keepdims=True).astype(I32)
    onehot = ((lane == e1) | (lane == e2)).astype(BF16)
    r = lax.broadcasted_iota(I32, (TM_MIX, TM_MIX), 0)
    c = lax.broadcasted_iota(I32, (TM_MIX, TM_MIX), 1)
    lower = (c < r).astype(BF16)
    before = jnp.dot(lower, onehot, preferred_element_type=F32) + base_ref[0]
    p1 = jnp.sum(jnp.where(lane == e1, before, 0.0), axis=-1, keepdims=True)
    p2 = jnp.sum(jnp.where(lane == e2, before, 0.0), axis=-1, keepdims=True)
    packed = jnp.where(lane == 0, p1, jnp.where(lane == 1, p2, 0.0))
    pos_ref[...] = jnp.transpose(packed)[0:TOP_K, :].astype(I32)


def _positions(route, base):
    t = route.shape[0]
    n_tiles = t // TM_MIX
    return pl.pallas_call(
        _pos_kernel,
        name="positions",
        grid=(n_tiles,),
        in_specs=[
            pl.BlockSpec((TM_MIX, LANES), lambda i: (i, 0)),
            pl.BlockSpec((1, 1, LANES), lambda i: (i, 0, 0)),
        ],
        out_specs=pl.BlockSpec((TOP_K, TM_MIX), lambda i: (0, i)),
        out_shape=jax.ShapeDtypeStruct((TOP_K, t), I32),
        compiler_params=pltpu.CompilerParams(dimension_semantics=("arbitrary",)),
    )(route, base)


def _sc_mesh():
    return plsc.VectorSubcoreMesh(core_axis_name="c", subcore_axis_name="s",
                                  num_cores=SC_CORES, num_subcores=SC_SUBCORES)


def _sc_worker():
    return lax.axis_index("s") * SC_CORES + lax.axis_index("c")


def _dispatch(pos, hp, rows):
    t = hp.shape[0]
    per_w = t // SC_WORKERS
    n_ch = per_w // SC_CHUNK
    pos4 = pos.reshape(TOP_K, SC_WORKERS, n_ch, SC_CHUNK)

    @functools.partial(
        pl.kernel, mesh=_sc_mesh(),
        out_type=jax.ShapeDtypeStruct((rows, PACKED), I32),
        scratch_types=[pltpu.VMEM((TOP_K, n_ch, SC_CHUNK), I32),
                       pltpu.VMEM((SC_CHUNK, PACKED), I32)])
    def scatter(hp_hbm, pos_hbm, xs_hbm, idx_v, rows_v):
        wid = _sc_worker()
        for k in range(TOP_K):
            pltpu.sync_copy(pos_hbm.at[k, wid], idx_v.at[k])

        def body(c, carry):
            start = pl.multiple_of(wid * per_w + c * SC_CHUNK, SC_CHUNK)
            pltpu.sync_copy(hp_hbm.at[pl.ds(start, SC_CHUNK)], rows_v)
            for k in range(TOP_K):
                pltpu.sync_copy(rows_v, xs_hbm.at[idx_v.at[k, c]])
            return carry

        lax.fori_loop(0, n_ch, body, 0)

    return scatter(hp, pos4)


def _gather_rows(table, idx):
    n = idx.shape[0]
    per_w = n // SC_WORKERS
    n_ch = per_w // SC_CHUNK
    idx3 = idx.reshape(SC_WORKERS, n_ch, SC_CHUNK)

    @functools.partial(
        pl.kernel, mesh=_sc_mesh(),
        out_type=jax.ShapeDtypeStruct((n, PACKED), I32),
        scratch_types=[pltpu.VMEM((n_ch, SC_CHUNK), I32),
                       pltpu.VMEM((SC_CHUNK, PACKED), I32)])
    def gather(table_hbm, idx_hbm, out_hbm, idx_v, rows_v):
        wid = _sc_worker()
        pltpu.sync_copy(idx_hbm.at[wid], idx_v)

        def body(c, carry):
            start = pl.multiple_of(wid * per_w + c * SC_CHUNK, SC_CHUNK)
            pltpu.sync_copy(table_hbm.at[idx_v.at[c]], rows_v)
            pltpu.sync_copy(rows_v, out_hbm.at[pl.ds(start, SC_CHUNK)])
            return carry

        lax.fori_loop(0, n_ch, body, 0)

    return gather(table, idx3)


def _expert_kernel(te_ref, nx_ref, sl_ref, ts_ref, tv_ref, xs_ref, wg_hbm, wu_hbm, wd_hbm, o_ref,
                   stage_g, stage_u, stage_d, wg_s, wu_s, wd_s, sem):
    del ts_ref
    i = pl.program_id(0)
    n_valid = tv_ref[i]
    expert = te_ref[i]
    slot = sl_ref[i]

    def weight_copies(e, s):
        return (pltpu.make_async_copy(wg_hbm.at[e], stage_g.at[s], sem.at[s, 0]),
                pltpu.make_async_copy(wu_hbm.at[e], stage_u.at[s], sem.at[s, 1]),
                pltpu.make_async_copy(wd_hbm.at[e], stage_d.at[s], sem.at[s, 2]))

    @pl.when(i == 0)
    def _():
        for copy in weight_copies(expert, slot):
            copy.start()

    @pl.when(jnp.logical_or(i == 0, expert != te_ref[jnp.maximum(i - 1, 0)]))
    def _():
        for copy in weight_copies(expert, slot):
            copy.wait()
        nxt = nx_ref[i]

        @pl.when(nxt >= 0)
        def _():
            for copy in weight_copies(nxt, 1 - slot):
                copy.start()

        wg_s[...] = stage_g[slot].astype(BF16)
        wu_s[...] = stage_u[slot].astype(BF16)
        wd_s[...] = stage_d[slot].astype(BF16)

    def mlp(n_sub):
        subs = [slice(s * SUB_EXP, (s + 1) * SUB_EXP) for s in range(n_sub)]
        xin = []
        for r in subs:
            rid = r.start + lax.broadcasted_iota(I32, (SUB_EXP, PACKED), 0)
            lo, hi = _unpack_bf16_pairs(jnp.where(rid < n_valid, xs_ref[r, :], 0))
            xin.append((lo.astype(BF16), hi.astype(BF16)))
        a = [jnp.dot(lo, wg_s[:PACKED, :], preferred_element_type=F32)
             + jnp.dot(hi, wg_s[PACKED:, :], preferred_element_type=F32) for lo, hi in xin]
        b = [jnp.dot(lo, wu_s[:PACKED, :], preferred_element_type=F32)
             + jnp.dot(hi, wu_s[PACKED:, :], preferred_element_type=F32) for lo, hi in xin]
        hmid = [(ai * jax.nn.sigmoid(ai) * bi).astype(BF16) for ai, bi in zip(a, b)]
        for r, hm in zip(subs, hmid):
            o_ref[r, :] = _pack_bf16_pairs(jnp.dot(hm, wd_s[...], preferred_element_type=F32))
        if n_sub * SUB_EXP < TM_EXP:
            o_ref[n_sub * SUB_EXP:, :] = jnp.zeros((TM_EXP - n_sub * SUB_EXP, PACKED), I32)

    n_subs = TM_EXP // SUB_EXP
    for n_sub in range(n_subs + 1):
        lo_rows = (n_sub - 1) * SUB_EXP if n_sub else -1
        in_range = jnp.logical_and(n_valid > lo_rows, n_valid <= n_sub * SUB_EXP)
        pl.when(in_range)(functools.partial(mlp, n_sub))


def _experts(tile_expert, next_expert, tile_slot, tile_src, tile_valid, xs, wg, wu, wd):
    rows = xs.shape[0]
    hbm = pl.BlockSpec(memory_space=pl.ANY)
    grid_spec = pltpu.PrefetchScalarGridSpec(
        num_scalar_prefetch=5,
        grid=(rows // TM_EXP,),
        in_specs=[pl.BlockSpec((TM_EXP, PACKED), lambda i, te, nx, sl, ts, tv: (ts[i], 0)),
                  hbm, hbm, hbm],
        out_specs=pl.BlockSpec((TM_EXP, PACKED), lambda i, te, nx, sl, ts, tv: (i, 0)),
        scratch_shapes=[
            pltpu.VMEM((2, D_MODEL, D_FF), F32), pltpu.VMEM((2, D_MODEL, D_FF), F32),
            pltpu.VMEM((2, D_FF, D_MODEL), F32),
            pltpu.VMEM((D_MODEL, D_FF), BF16), pltpu.VMEM((D_MODEL, D_FF), BF16),
            pltpu.VMEM((D_FF, D_MODEL), BF16),
            pltpu.SemaphoreType.DMA((2, 3)),
        ],
    )
    return pl.pallas_call(
        _expert_kernel,
        name="experts",
        grid_spec=grid_spec,
        out_shape=jax.ShapeDtypeStruct((rows, PACKED), I32),
        compiler_params=pltpu.CompilerParams(
            dimension_semantics=("arbitrary",), vmem_limit_bytes=VMEM_LIMIT),
    )(tile_expert, next_expert, tile_slot, tile_src, tile_valid, xs, wg, wu, wd)


def _combine_kernel(y1_ref, y2_ref, route_ref, xmid_ref, g_ref, *rest):
    o_ref = rest[-1]
    route = route_ref[...]
    lane = lax.broadcasted_iota(I32, route.shape, 1)
    w1 = jnp.sum(jnp.where(lane == 2, route, 0.0), axis=-1, keepdims=True)
    w2 = jnp.sum(jnp.where(lane == 3, route, 0.0), axis=-1, keepdims=True)
    lo1, hi1 = _unpack_bf16_pairs(y1_ref[...])
    lo2, hi2 = _unpack_bf16_pairs(y2_ref[...])
    y = jnp.concatenate([lo1 * w1 + lo2 * w2, hi1 * w1 + hi2 * w2], axis=1)
    o_ref[...] = _rms(xmid_ref[...] + y, g_ref[...])


def _combine(yg, route, xmid, g, chunk, t_total, out_prev):
    t = xmid.shape[0]
    n_tiles = t // TM_CMB
    first = chunk * n_tiles
    in_specs = [
        pl.BlockSpec((TM_CMB, PACKED), lambda i: (i, 0)),
        pl.BlockSpec((TM_CMB, PACKED), lambda i: (n_tiles + i, 0)),
        pl.BlockSpec((TM_CMB, LANES), lambda i: (i, 0)),
        pl.BlockSpec((TM_CMB, D_MODEL), lambda i: (i, 0)),
        pl.BlockSpec((1, D_MODEL), lambda i: (0, 0)),
    ]
    args = [yg, yg, route, xmid, g]
    aliases = {}
    if out_prev is not None:
        in_specs.append(pl.BlockSpec(memory_space=pl.ANY))
        aliases = {len(args): 0}
        args.append(out_prev)
    return pl.pallas_call(
        _combine_kernel,
        name="combine",
        grid=(n_tiles,),
        in_specs=in_specs,
        out_specs=pl.BlockSpec((TM_CMB, D_MODEL), lambda i: (first + i, 0)),
        out_shape=jax.ShapeDtypeStruct((t_total, D_MODEL), F32),
        input_output_aliases=aliases,
        compiler_params=pltpu.CompilerParams(
            dimension_semantics=("arbitrary",), vmem_limit_bytes=VMEM_LIMIT),
    )(*args)


def _split_bf16(w):
    hi = w.astype(BF16)
    lo = (w - hi.astype(F32)).astype(BF16)
    return hi, lo


def kernel(x, norm_mix, w_in, conv_w, conv_b, w_a_out, sinks, w_b_out, w_o, norm_ffn, w_group,
           b_group, w_expert, b_expert, w_gate, w_up, w_down, norm_final):
    bsz, seq, d = x.shape
    t = bsz * seq
    assert d == D_MODEL and seq % TM_PROJ == 0 and seq % TQ_ATTN == 0
    xf = x.reshape(t, d)
    row = lambda v: v.reshape(1, -1)

    w_b, w_c, w_u, w_q, w_k, w_v, w_ga, w_gb = jnp.split(w_in, REF_SPLITS, axis=1)
    w_cu = jnp.concatenate([w_c, w_u], axis=1).astype(BF16)
    w_bg = jnp.concatenate([w_b, w_ga], axis=1).astype(BF16)
    w_rest = jnp.concatenate([w_gb, w_q, w_k, w_v], axis=1).astype(BF16)

    za, proj = _inproj(xf, row(norm_mix), w_cu, w_bg, w_rest, conv_w, row(conv_b),
                       w_a_out.astype(BF16), seq)
    attn = _attention(proj, sinks, seq)

    pad = LANES - N_GROUPS - N_EXPERTS
    w_r = jnp.concatenate([w_group, w_expert, jnp.zeros((d, pad), F32)], axis=1)
    b_r = jnp.concatenate([b_group, b_expert, jnp.zeros((pad,), F32)]).reshape(1, LANES)
    wr = jnp.concatenate(_split_bf16(w_r), axis=1)
    wb = w_b_out.astype(BF16)
    wo = w_o.astype(BF16)

    t_chunk = t // MOE_CHUNKS
    out = None
    for chunk in range(MOE_CHUNKS):
        xmid, h2, route, cnt = _mix(attn, proj, za, xf, wb, wo, row(norm_ffn), wr, b_r, chunk, t_chunk)
        out = _moe_chunk(xmid, h2, route, cnt, w_gate, w_up, w_down, row(norm_final), chunk, t, out)
    return out.reshape(bsz, seq, d)


def _moe_chunk(xmid, h2, route, cnt, w_gate, w_up, w_down, g_final, chunk, t_total, out_prev):
    t = xmid.shape[0]
    n_tiles = t // TM_MIX
    cnt = cnt.reshape(n_tiles, 8, LANES)[:, 0, :N_EXPERTS].astype(I32)
    totals = jnp.sum(cnt, axis=0)
    tiles_e = (totals + TM_EXP - 1) // TM_EXP
    tile_end = jnp.cumsum(tiles_e)
    offset = (tile_end - tiles_e) * TM_EXP
    base = offset[None, :] + jnp.cumsum(cnt, axis=0) - cnt
    base = jnp.pad(base, ((0, 0), (0, LANES - N_EXPERTS))).astype(F32).reshape(n_tiles, 1, LANES)
    rows = t * TOP_K + N_EXPERTS * TM_EXP
    n_active = tile_end[-1]
    tile_id = jnp.arange(rows // TM_EXP, dtype=I32)
    tile_src = jnp.minimum(tile_id, n_active - 1)
    tile_expert = jnp.sum((tile_src[:, None] >= tile_end[None, :]).astype(I32), axis=1)
    tile_expert = jnp.minimum(tile_expert, N_EXPERTS - 1)
    row_in_expert = (tile_id - (tile_end - tiles_e)[tile_expert]) * TM_EXP
    tile_valid = jnp.clip(totals[tile_expert] - row_in_expert, 0, TM_EXP)
    tile_valid = jnp.where(tile_id < n_active, tile_valid, 0).astype(I32)
    after = tile_end[tile_expert]
    next_expert = jnp.where(after < n_active, tile_expert[jnp.minimum(after, n_active - 1)], -1)
    first_of_expert = jnp.concatenate(
        [jnp.ones((1,), I32), (tile_expert[1:] != tile_expert[:-1]).astype(I32)])
    tile_slot = (jnp.cumsum(first_of_expert) - 1) % 2

    pos = _positions(route, base)
    xs = _dispatch(pos, h2, rows)
    ys = _experts(tile_expert.astype(I32), next_expert.astype(I32), tile_slot.astype(I32),
                  tile_src.astype(I32), tile_valid, xs, w_gate, w_up, w_down)
    yg = _gather_rows(ys, pos.reshape(TOP_K * t))
    return _combine(yg, route, xmid, g_final, chunk, t_total, out_prev)
```

```python
import functools
import math

import jax
import jax.numpy as jnp
from jax import lax
from jax.experimental import pallas as pl
from jax.experimental.pallas import tpu as pltpu
from jax.experimental.pallas import tpu_sc as plsc

F32 = jnp.float32
BF16 = jnp.bfloat16
I32 = jnp.int32

D_MODEL = 1024
HEAD_DIM = 64
N_HEADS = 16
N_KV_HEADS = 4
GROUP = N_HEADS // N_KV_HEADS
KV_WIDTH = N_KV_HEADS * HEAD_DIM
WINDOW = 128
N_GROUPS = 4
EXPERTS_PER_GROUP = 8
N_EXPERTS = N_GROUPS * EXPERTS_PER_GROUP
TOP_K = 2
D_FF = 512
EPS = 1e-6
LANES = 128

REF_SPLITS = (1024, 2048, 3072, 4096, 4352, 4608, 5632)
REST_COLS = 2 * D_MODEL + 2 * KV_WIDTH
COL_GB, COL_Q = 0, 1
COL_K, COL_V = 2 * D_MODEL // KV_WIDTH, 2 * D_MODEL // KV_WIDTH + 1

TM_PROJ = 512
TQ_ATTN = 512
TM_MIX = 512
SUB_MIX = 256
POS_TILES = 4
MOE_CHUNKS = 2
TM_EXP = 512
SUB_EXP = 256
TM_CMB = 1024
HALO_ROWS = 8
VMEM_LIMIT = 56 * 1024 * 1024
PACKED = D_MODEL // 2

SC_CORES = 2
SC_SUBCORES = 16
SC_WORKERS = SC_CORES * SC_SUBCORES
SC_CHUNK = 64


def _rms(x, g):
    r = lax.rsqrt(jnp.mean(x * x, axis=-1, keepdims=True) + EPS)
    return (x * r) * g


def _pack_bf16_pairs(x):
    n = x.shape[1] // 2
    lo = lax.bitcast_convert_type(x[:, :n].astype(BF16).astype(F32), I32)
    hi = lax.bitcast_convert_type(x[:, n:].astype(BF16).astype(F32), I32)
    return (hi & jnp.int32(-65536)) | lax.shift_right_logical(lo, 16)


def _unpack_bf16_pairs(p):
    lo = lax.bitcast_convert_type(lax.shift_left(p, 16), F32)
    hi = lax.bitcast_convert_type(p & jnp.int32(-65536), F32)
    return lo, hi


def _inproj_kernel(x_ref, g_ref, wcu_ref, wbg_ref, wrest_ref, cw_ref, cb_ref, wa_ref,
                   za_ref, proj_ref, halo_ref, *, tiles_per_seq):
    h = _rms(x_ref[...], g_ref[...]).astype(BF16)
    pcu = jnp.dot(h, wcu_ref[...], preferred_element_type=F32)
    cu = pcu[:, :D_MODEL] * pcu[:, D_MODEL:]
    first = (pl.program_id(0) % tiles_per_seq) == 0
    hist = jnp.where(first, 0.0, halo_ref[...])
    prev1 = hist[HALO_ROWS - 1:HALO_ROWS]
    prev2 = hist[HALO_ROWS - 2:HALO_ROWS - 1]
    halo_ref[...] = cu[TM_PROJ - HALO_ROWS:, :]
    row = lax.broadcasted_iota(I32, cu.shape, 0)
    cu1 = jnp.where(row == 0, prev1, pltpu.roll(cu, 1, 0))
    cu2 = jnp.where(row == 0, prev2, jnp.where(row == 1, prev1, pltpu.roll(cu, 2, 0)))
    cw = cw_ref[...]
    y = cw[0:1] * cu2 + cw[1:2] * cu1 + cw[2:3] * cu + cb_ref[...]
    pbg = jnp.dot(h, wbg_ref[...], preferred_element_type=F32)
    ya = (pbg[:, :D_MODEL] * y).astype(BF16)
    z = jnp.dot(ya, wa_ref[...], preferred_element_type=F32)
    za_ref[...] = (jax.nn.sigmoid(pbg[:, D_MODEL:]) * z).astype(BF16)
    proj_ref[...] = jnp.dot(h, wrest_ref[...], preferred_element_type=F32).astype(BF16)


def _inproj(xf, g, w_cu, w_bg, w_rest, conv_w, conv_b, wa, seq):
    t = xf.shape[0]
    const = lambda shape: pl.BlockSpec(shape, lambda i: (0, 0), pipeline_mode=pl.Buffered(1))
    return pl.pallas_call(
        functools.partial(_inproj_kernel, tiles_per_seq=seq // TM_PROJ),
        name="inproj_conv",
        grid=(t // TM_PROJ,),
        in_specs=[
            pl.BlockSpec((TM_PROJ, D_MODEL), lambda i: (i, 0)),
            const((1, D_MODEL)),
            const((D_MODEL, 2 * D_MODEL)), const((D_MODEL, 2 * D_MODEL)), const((D_MODEL, REST_COLS)),
            const((3, D_MODEL)), const((1, D_MODEL)), const((D_MODEL, D_MODEL)),
        ],
        out_specs=[pl.BlockSpec((TM_PROJ, D_MODEL), lambda i: (i, 0)),
                   pl.BlockSpec((TM_PROJ, REST_COLS), lambda i: (i, 0))],
        out_shape=[jax.ShapeDtypeStruct((t, D_MODEL), BF16),
                   jax.ShapeDtypeStruct((t, REST_COLS), BF16)],
        scratch_shapes=[pltpu.VMEM((HALO_ROWS, D_MODEL), F32)],
        compiler_params=pltpu.CompilerParams(
            dimension_semantics=("arbitrary",), vmem_limit_bytes=VMEM_LIMIT),
    )(xf, g, w_cu, w_bg, w_rest, conv_w, conv_b, wa)


def _attn_kernel(sink_ref, q_ref, k_ref, v_ref, kp_ref, vp_ref, o_ref, *, tiles_per_seq):
    first_tile = (pl.program_id(0) % tiles_per_seq) == 0
    ks = lax.broadcasted_iota(I32, (WINDOW, WINDOW), 0)
    qq = lax.broadcasted_iota(I32, (WINDOW, WINDOW), 1)
    own = ks <= qq
    dist = jnp.where(own, qq - ks, qq - ks + WINDOW).astype(F32)
    visible0 = jnp.logical_or(own, jnp.logical_not(first_tile))
    log2e = math.log2(math.e)
    c_scale = log2e / math.sqrt(HEAD_DIM)
    nt = (((1,), (1,)), ((), ()))
    zk = jnp.zeros((2 * WINDOW, HEAD_DIM), BF16)

    def transposed(v_blk):
        return jnp.transpose(v_blk.astype(F32)).astype(BF16)

    prev_k = kp_ref[...]
    prev_vt = transposed(vp_ref[...])
    for sb in range(TQ_ATTN // WINDOW):
        rows = slice(sb * WINDOW, (sb + 1) * WINDOW)
        cur_k = k_ref[rows, :]
        cur_vt = transposed(v_ref[rows, :])
        scores, vcats = [], []
        for kh in range(N_KV_HEADS):
            cols = slice(kh * HEAD_DIM, (kh + 1) * HEAD_DIM)
            kcat = jnp.concatenate([prev_k[:, cols], cur_k[:, cols]], axis=0)
            vcats.append(jnp.concatenate([prev_vt[cols, :], cur_vt[cols, :]], axis=1))
            qg = jnp.concatenate([q_ref[rows, (2 * kh) * LANES:(2 * kh + 1) * LANES],
                                  q_ref[rows, (2 * kh + 1) * LANES:(2 * kh + 2) * LANES]], axis=0)
            k_pad = jnp.concatenate([jnp.concatenate([kcat, zk], axis=1),
                                     jnp.concatenate([zk, kcat], axis=1)], axis=0)
            scores.append(lax.dot_general(k_pad, qg, nt, preferred_element_type=F32))
        probs, rdens = [], []
        for kh in range(N_KV_HEADS):
            for pos in range(2):
                pr, rd = [], []
                for half in range(2):
                    h = kh * GROUP + 2 * half + pos
                    slope = 2.0 ** (-8.0 * (h + 1) / N_HEADS)
                    qcols = slice(half * WINDOW, (half + 1) * WINDOW)
                    krow = pos * 2 * WINDOW
                    st = scores[kh]
                    s = (jnp.where(own, st[krow + WINDOW:krow + 2 * WINDOW, qcols],
                                   st[krow:krow + WINDOW, qcols]) * c_scale
                         - (slope * log2e) * dist)
                    if sb == 0:
                        s = jnp.where(visible0, s, -jnp.inf)
                    m = jnp.max(s, axis=0, keepdims=True)
                    p = jnp.exp2(s - m)
                    den = jnp.sum(p, axis=0, keepdims=True) + jnp.exp2(sink_ref[h] * log2e - m)
                    rd.append(1.0 / den)
                    pr.append(jnp.concatenate(
                        [jnp.where(own, 0.0, p).astype(BF16), jnp.where(own, p, 0.0).astype(BF16)],
                        axis=0))
                probs.append(jnp.concatenate(pr, axis=1))
                rdens.append(jnp.concatenate(rd, axis=1))
        out_t = [None] * N_HEADS
        for kh in range(N_KV_HEADS):
            for pos in range(2):
                o2 = jnp.dot(vcats[kh], probs[2 * kh + pos], preferred_element_type=F32)
                o2 = o2 * rdens[2 * kh + pos]
                out_t[kh * GROUP + pos] = o2[:, :WINDOW]
                out_t[kh * GROUP + 2 + pos] = o2[:, WINDOW:]
        o_ref[rows, :] = jnp.transpose(jnp.concatenate(out_t, axis=0)).astype(BF16)
        prev_k, prev_vt = cur_k, cur_vt


def _attention(proj, sinks, seq):
    t = proj.shape[0]
    sub = TQ_ATTN // WINDOW
    return pl.pallas_call(
        functools.partial(_attn_kernel, tiles_per_seq=seq // TQ_ATTN),
        name="swattn",
        grid=(t // TQ_ATTN,),
        in_specs=[
            pl.BlockSpec(memory_space=pltpu.SMEM),
            pl.BlockSpec((TQ_ATTN, D_MODEL), lambda i: (i, COL_Q)),
            pl.BlockSpec((TQ_ATTN, KV_WIDTH), lambda i: (i, COL_K)),
            pl.BlockSpec((TQ_ATTN, KV_WIDTH), lambda i: (i, COL_V)),
            pl.BlockSpec((WINDOW, KV_WIDTH), lambda i: (jnp.maximum(i * sub - 1, 0), COL_K)),
            pl.BlockSpec((WINDOW, KV_WIDTH), lambda i: (jnp.maximum(i * sub - 1, 0), COL_V)),
        ],
        out_specs=pl.BlockSpec((TQ_ATTN, D_MODEL), lambda i: (i, 0)),
        out_shape=jax.ShapeDtypeStruct((t, D_MODEL), BF16),
        compiler_params=pltpu.CompilerParams(
            dimension_semantics=("arbitrary",), vmem_limit_bytes=VMEM_LIMIT),
    )(sinks, proj, proj, proj, proj, proj)


def _mix_kernel(attn_ref, gb_ref, za_ref, x_ref, wb_ref, wo_ref, g_ref, wr_ref, br_ref,
                xmid_ref, h_ref, route_ref, cnt_ref):
    subs = [slice(s * SUB_MIX, (s + 1) * SUB_MIX) for s in range(TM_MIX // SUB_MIX)]
    yb = [jnp.dot(attn_ref[r, :], wb_ref[...], preferred_element_type=F32) for r in subs]
    merged = [(za_ref[r, :].astype(F32) + jax.nn.sigmoid(gb_ref[r, :].astype(F32)) * y).astype(BF16)
              for r, y in zip(subs, yb)]
    xm = [x_ref[r, :] + jnp.dot(m, wo_ref[...], preferred_element_type=F32)
          for r, m in zip(subs, merged)]
    hs = []
    for r, v in zip(subs, xm):
        xmid_ref[r, :] = _pack_bf16_pairs(v)
        h = _rms(v, g_ref[...])
        h_ref[r, :] = _pack_bf16_pairs(h)
        hs.append(h)
    wr = wr_ref[...]
    logits = []
    for h in hs:
        h_hi = h.astype(BF16)
        h_lo = (h - h_hi.astype(F32)).astype(BF16)
        both = jnp.dot(h_hi, wr, preferred_element_type=F32)
        logits.append(both[:, :LANES] + both[:, LANES:]
                      + jnp.dot(h_lo, wr[:, :LANES], preferred_element_type=F32) + br_ref[...])
    cnt = jnp.zeros((1, LANES), F32)
    for r, lg in zip(subs, logits):
        route, onehot = _route(lg)
        route_ref[r, :] = route
        cnt = cnt + jnp.sum(onehot, axis=0, keepdims=True)
    cnt_ref[...] = jnp.broadcast_to(cnt, cnt_ref.shape)


def _route(logits):
    lane = lax.broadcasted_iota(I32, logits.shape, 1)
    neg = -jnp.inf
    gl = jnp.where(lane < N_GROUPS, logits, neg)
    gmax = jnp.max(gl, axis=-1, keepdims=True)
    g_idx = jnp.min(jnp.where(gl == gmax, lane, LANES), axis=-1, keepdims=True)
    p_g = 1.0 / jnp.sum(jnp.exp(gl - gmax), axis=-1, keepdims=True)
    start = N_GROUPS + EXPERTS_PER_GROUP * g_idx
    el = jnp.where((lane >= start) & (lane < start + EXPERTS_PER_GROUP), logits, neg)
    v1 = jnp.max(el, axis=-1, keepdims=True)
    i1 = jnp.min(jnp.where(el == v1, lane, LANES), axis=-1, keepdims=True)
    el2 = jnp.where(lane == i1, neg, el)
    v2 = jnp.max(el2, axis=-1, keepdims=True)
    i2 = jnp.min(jnp.where(el2 == v2, lane, LANES), axis=-1, keepdims=True)
    e21 = jnp.exp(v2 - v1)
    w1 = p_g / (1.0 + e21)
    w2 = p_g * e21 / (1.0 + e21)
    e1 = i1 - N_GROUPS
    e2 = i2 - N_GROUPS
    route = jnp.where(lane == 0, e1.astype(F32),
                      jnp.where(lane == 1, e2.astype(F32),
                                jnp.where(lane == 2, w1, jnp.where(lane == 3, w2, 0.0))))
    onehot = ((lane == e1) | (lane == e2)).astype(F32)
    return route, onehot


def _mix(attn, proj, za, xf, wb, wo, g, wr, br, chunk, t):
    n_tiles = t // TM_MIX
    first = chunk * n_tiles
    full = lambda shape: pl.BlockSpec(shape, lambda i: (0, 0))
    tile = lambda w=D_MODEL: pl.BlockSpec((TM_MIX, w), lambda i: (i, 0))
    src = lambda c=0: pl.BlockSpec((TM_MIX, D_MODEL), lambda i: (first + i, c))
    return pl.pallas_call(
        _mix_kernel,
        name="merge_router",
        grid=(n_tiles,),
        in_specs=[
            src(), src(COL_GB), src(), src(),
            full((D_MODEL, D_MODEL)), full((D_MODEL, D_MODEL)), full((1, D_MODEL)),
            full((D_MODEL, 2 * LANES)), full((1, LANES)),
        ],
        out_specs=[tile(PACKED), tile(PACKED), tile(LANES),
                   pl.BlockSpec((8, LANES), lambda i: (i, 0))],
        out_shape=[
            jax.ShapeDtypeStruct((t, PACKED), I32),
            jax.ShapeDtypeStruct((t, PACKED), I32),
            jax.ShapeDtypeStruct((t, LANES), F32),
            jax.ShapeDtypeStruct((n_tiles * 8, LANES), F32),
        ],
        compiler_params=pltpu.CompilerParams(
            dimension_semantics=("arbitrary",), vmem_limit_bytes=VMEM_LIMIT),
    )(attn, proj, za, xf, wb, wo, g, wr, br)


def _pos_kernel(route_ref, base_ref, pos_ref):
    lane = lax.broadcasted_iota(I32, (TM_MIX, LANES), 1)
    r = lax.broadcasted_iota(I32, (TM_MIX, TM_MIX), 0)
    c = lax.broadcasted_iota(I32, (TM_MIX, TM_MIX), 1)
    lower = (c < r).astype(BF16)
    subs = [slice(s * TM_MIX, (s + 1) * TM_MIX) for s in range(POS_TILES)]
    routes = [route_ref[rs, :] for rs in subs]
    e1 = [jnp.sum(jnp.where(lane == 0, rt, 0.0), axis=-1, keepdims=True).astype(I32) for rt in routes]
    e2 = [jnp.sum(jnp.where(lane == 1, rt, 0.0), axis=-1, keepdims=True).astype(I32) for rt in routes]
    onehot = [((lane == a) | (lane == b)).astype(BF16) for a, b in zip(e1, e2)]
    before = [jnp.dot(lower, oh, preferred_element_type=F32) + base_ref[s]
              for s, oh in enumerate(onehot)]
    for s, rs in enumerate(subs):
        p1 = jnp.sum(jnp.where(lane == e1[s], before[s], 0.0), axis=-1, keepdims=True)
        p2 = jnp.sum(jnp.where(lane == e2[s], before[s], 0.0), axis=-1, keepdims=True)
        packed = jnp.where(lane == 0, p1, jnp.where(lane == 1, p2, 0.0))
        pos_ref[:, rs] = jnp.transpose(packed)[0:TOP_K, :].astype(I32)


def _positions(route, base):
    t = route.shape[0]
    n_steps = t // (TM_MIX * POS_TILES)
    return pl.pallas_call(
        _pos_kernel,
        name="positions",
        grid=(n_steps,),
        in_specs=[
            pl.BlockSpec((TM_MIX * POS_TILES, LANES), lambda i: (i, 0)),
            pl.BlockSpec((POS_TILES, 1, LANES), lambda i: (i, 0, 0)),
        ],
        out_specs=pl.BlockSpec((TOP_K, TM_MIX * POS_TILES), lambda i: (0, i)),
        out_shape=jax.ShapeDtypeStruct((TOP_K, t), I32),
        compiler_params=pltpu.CompilerParams(dimension_semantics=("arbitrary",)),
    )(route, base)


def _sc_mesh():
    return plsc.VectorSubcoreMesh(core_axis_name="c", subcore_axis_name="s",
                                  num_cores=SC_CORES, num_subcores=SC_SUBCORES)


def _sc_worker():
    return lax.axis_index("s") * SC_CORES + lax.axis_index("c")


def _dispatch(pos, hp, rows):
    t = hp.shape[0]
    per_w = t // SC_WORKERS
    n_ch = per_w // SC_CHUNK
    pos4 = pos.reshape(TOP_K, SC_WORKERS, n_ch, SC_CHUNK)

    @functools.partial(
        pl.kernel, mesh=_sc_mesh(),
        out_type=jax.ShapeDtypeStruct((rows, PACKED), I32),
        scratch_types=[pltpu.VMEM((TOP_K, n_ch, SC_CHUNK), I32),
                       pltpu.VMEM((SC_CHUNK, PACKED), I32)])
    def scatter(hp_hbm, pos_hbm, xs_hbm, idx_v, rows_v):
        wid = _sc_worker()
        for k in range(TOP_K):
            pltpu.sync_copy(pos_hbm.at[k, wid], idx_v.at[k])

        def body(c, carry):
            start = pl.multiple_of(wid * per_w + c * SC_CHUNK, SC_CHUNK)
            pltpu.sync_copy(hp_hbm.at[pl.ds(start, SC_CHUNK)], rows_v)
            for k in range(TOP_K):
                pltpu.sync_copy(rows_v, xs_hbm.at[idx_v.at[k, c]])
            return carry

        lax.fori_loop(0, n_ch, body, 0)

    return scatter(hp, pos4)


def _gather_rows(table, idx):
    n = idx.shape[0]
    per_w = n // SC_WORKERS
    n_ch = per_w // SC_CHUNK
    idx3 = idx.reshape(SC_WORKERS, n_ch, SC_CHUNK)

    @functools.partial(
        pl.kernel, mesh=_sc_mesh(),
        out_type=jax.ShapeDtypeStruct((n, PACKED), I32),
        scratch_types=[pltpu.VMEM((n_ch, SC_CHUNK), I32),
                       pltpu.VMEM((SC_CHUNK, PACKED), I32)])
    def gather(table_hbm, idx_hbm, out_hbm, idx_v, rows_v):
        wid = _sc_worker()
        pltpu.sync_copy(idx_hbm.at[wid], idx_v)

        def body(c, carry):
            start = pl.multiple_of(wid * per_w + c * SC_CHUNK, SC_CHUNK)
            pltpu.sync_copy(table_hbm.at[idx_v.at[c]], rows_v)
            pltpu.sync_copy(rows_v, out_hbm.at[pl.ds(start, SC_CHUNK)])
            return carry

        lax.fori_loop(0, n_ch, body, 0)

    return gather(table, idx3)


def _expert_kernel(te_ref, nx_ref, sl_ref, ts_ref, tv_ref, xs_ref, wg_hbm, wu_hbm, wd_hbm, o_ref,
                   stage_g, stage_u, stage_d, wg_s, wu_s, wd_s, sem):
    del ts_ref
    i = pl.program_id(0)
    n_valid = tv_ref[i]
    expert = te_ref[i]
    slot = sl_ref[i]

    def weight_copies(e, s):
        return (pltpu.make_async_copy(wg_hbm.at[e], stage_g.at[s], sem.at[s, 0]),
                pltpu.make_async_copy(wu_hbm.at[e], stage_u.at[s], sem.at[s, 1]),
                pltpu.make_async_copy(wd_hbm.at[e], stage_d.at[s], sem.at[s, 2]))

    @pl.when(i == 0)
    def _():
        for copy in weight_copies(expert, slot):
            copy.start()

    @pl.when(jnp.logical_or(i == 0, expert != te_ref[jnp.maximum(i - 1, 0)]))
    def _():
        for copy in weight_copies(expert, slot):
            copy.wait()
        nxt = nx_ref[i]

        @pl.when(nxt >= 0)
        def _():
            for copy in weight_copies(nxt, 1 - slot):
                copy.start()

        wg_s[...] = stage_g[slot].astype(BF16)
        wu_s[...] = stage_u[slot].astype(BF16)
        wd_s[...] = stage_d[slot].astype(BF16)

    def mlp(n_sub):
        subs = [slice(s * SUB_EXP, (s + 1) * SUB_EXP) for s in range(n_sub)]
        xin = []
        for r in subs:
            rid = r.start + lax.broadcasted_iota(I32, (SUB_EXP, PACKED), 0)
            lo, hi = _unpack_bf16_pairs(jnp.where(rid < n_valid, xs_ref[r, :], 0))
            xin.append((lo.astype(BF16), hi.astype(BF16)))
        a = [jnp.dot(lo, wg_s[:PACKED, :], preferred_element_type=F32)
             + jnp.dot(hi, wg_s[PACKED:, :], preferred_element_type=F32) for lo, hi in xin]
        b = [jnp.dot(lo, wu_s[:PACKED, :], preferred_element_type=F32)
             + jnp.dot(hi, wu_s[PACKED:, :], preferred_element_type=F32) for lo, hi in xin]
        hmid = [(ai * jax.nn.sigmoid(ai) * bi).astype(BF16) for ai, bi in zip(a, b)]
        for r, hm in zip(subs, hmid):
            o_ref[r, :] = _pack_bf16_pairs(jnp.dot(hm, wd_s[...], preferred_element_type=F32))
        if n_sub * SUB_EXP < TM_EXP:
            o_ref[n_sub * SUB_EXP:, :] = jnp.zeros((TM_EXP - n_sub * SUB_EXP, PACKED), I32)

    n_subs = TM_EXP // SUB_EXP
    for n_sub in range(n_subs + 1):
        lo_rows = (n_sub - 1) * SUB_EXP if n_sub else -1
        in_range = jnp.logical_and(n_valid > lo_rows, n_valid <= n_sub * SUB_EXP)
        pl.when(in_range)(functools.partial(mlp, n_sub))


def _experts(tile_expert, next_expert, tile_slot, tile_src, tile_valid, xs, wg, wu, wd):
    rows = xs.shape[0]
    hbm = pl.BlockSpec(memory_space=pl.ANY)
    grid_spec = pltpu.PrefetchScalarGridSpec(
        num_scalar_prefetch=5,
        grid=(rows // TM_EXP,),
        in_specs=[pl.BlockSpec((TM_EXP, PACKED), lambda i, te, nx, sl, ts, tv: (ts[i], 0)),
                  hbm, hbm, hbm],
        out_specs=pl.BlockSpec((TM_EXP, PACKED), lambda i, te, nx, sl, ts, tv: (i, 0)),
        scratch_shapes=[
            pltpu.VMEM((2, D_MODEL, D_FF), F32), pltpu.VMEM((2, D_MODEL, D_FF), F32),
            pltpu.VMEM((2, D_FF, D_MODEL), F32),
            pltpu.VMEM((D_MODEL, D_FF), BF16), pltpu.VMEM((D_MODEL, D_FF), BF16),
            pltpu.VMEM((D_FF, D_MODEL), BF16),
            pltpu.SemaphoreType.DMA((2, 3)),
        ],
    )
    return pl.pallas_call(
        _expert_kernel,
        name="experts",
        grid_spec=grid_spec,
        out_shape=jax.ShapeDtypeStruct((rows, PACKED), I32),
        compiler_params=pltpu.CompilerParams(
            dimension_semantics=("arbitrary",), vmem_limit_bytes=VMEM_LIMIT),
    )(tile_expert, next_expert, tile_slot, tile_src, tile_valid, xs, wg, wu, wd)


def _combine_kernel(y1_ref, y2_ref, route_ref, xmid_ref, g_ref, *rest):
    o_ref = rest[-1]
    route = route_ref[...]
    lane = lax.broadcasted_iota(I32, route.shape, 1)
    w1 = jnp.sum(jnp.where(lane == 2, route, 0.0), axis=-1, keepdims=True)
    w2 = jnp.sum(jnp.where(lane == 3, route, 0.0), axis=-1, keepdims=True)
    lo1, hi1 = _unpack_bf16_pairs(y1_ref[...])
    lo2, hi2 = _unpack_bf16_pairs(y2_ref[...])
    lox, hix = _unpack_bf16_pairs(xmid_ref[...])
    x_out = jnp.concatenate([lox + (lo1 * w1 + lo2 * w2), hix + (hi1 * w1 + hi2 * w2)], axis=1)
    o_ref[...] = _rms(x_out, g_ref[...])


def _combine(yg, route, xmid, g, chunk, t_total, out_prev):
    t = xmid.shape[0]
    n_tiles = t // TM_CMB
    first = chunk * n_tiles
    in_specs = [
        pl.BlockSpec((TM_CMB, PACKED), lambda i: (i, 0)),
        pl.BlockSpec((TM_CMB, PACKED), lambda i: (n_tiles + i, 0)),
        pl.BlockSpec((TM_CMB, LANES), lambda i: (i, 0)),
        pl.BlockSpec((TM_CMB, PACKED), lambda i: (i, 0)),
        pl.BlockSpec((1, D_MODEL), lambda i: (0, 0)),
    ]
    args = [yg, yg, route, xmid, g]
    aliases = {}
    if out_prev is not None:
        in_specs.append(pl.BlockSpec(memory_space=pl.ANY))
        aliases = {len(args): 0}
        args.append(out_prev)
    return pl.pallas_call(
        _combine_kernel,
        name="combine",
        grid=(n_tiles,),
        in_specs=in_specs,
        out_specs=pl.BlockSpec((TM_CMB, D_MODEL), lambda i: (first + i, 0)),
        out_shape=jax.ShapeDtypeStruct((t_total, D_MODEL), F32),
        input_output_aliases=aliases,
        compiler_params=pltpu.CompilerParams(
            dimension_semantics=("arbitrary",), vmem_limit_bytes=VMEM_LIMIT),
    )(*args)


def _split_bf16(w):
    hi = w.astype(BF16)
    lo = (w - hi.astype(F32)).astype(BF16)
    return hi, lo


def kernel(x, norm_mix, w_in, conv_w, conv_b, w_a_out, sinks, w_b_out, w_o, norm_ffn, w_group,
           b_group, w_expert, b_expert, w_gate, w_up, w_down, norm_final):
    bsz, seq, d = x.shape
    t = bsz * seq
    assert d == D_MODEL and seq % TM_PROJ == 0 and seq % TQ_ATTN == 0
    xf = x.reshape(t, d)
    row = lambda v: v.reshape(1, -1)

    w_b, w_c, w_u, w_q, w_k, w_v, w_ga, w_gb = jnp.split(w_in, REF_SPLITS, axis=1)
    w_cu = jnp.concatenate([w_c, w_u], axis=1).astype(BF16)
    w_bg = jnp.concatenate([w_b, w_ga], axis=1).astype(BF16)
    w_rest = jnp.concatenate([w_gb, w_q, w_k, w_v], axis=1).astype(BF16)

    za, proj = _inproj(xf, row(norm_mix), w_cu, w_bg, w_rest, conv_w, row(conv_b),
                       w_a_out.astype(BF16), seq)
    attn = _attention(proj, sinks, seq)

    pad = LANES - N_GROUPS - N_EXPERTS
    w_r = jnp.concatenate([w_group, w_expert, jnp.zeros((d, pad), F32)], axis=1)
    b_r = jnp.concatenate([b_group, b_expert, jnp.zeros((pad,), F32)]).reshape(1, LANES)
    wr = jnp.concatenate(_split_bf16(w_r), axis=1)
    wb = w_b_out.astype(BF16)
    wo = w_o.astype(BF16)

    t_chunk = t // MOE_CHUNKS
    out = None
    for chunk in range(MOE_CHUNKS):
        xmid, h2, route, cnt = _mix(attn, proj, za, xf, wb, wo, row(norm_ffn), wr, b_r, chunk, t_chunk)
        out = _moe_chunk(xmid, h2, route, cnt, w_gate, w_up, w_down, row(norm_final), chunk, t, out)
    return out.reshape(bsz, seq, d)


def _moe_chunk(xmid, h2, route, cnt, w_gate, w_up, w_down, g_final, chunk, t_total, out_prev):
    t = xmid.shape[0]
    n_tiles = t // TM_MIX
    cnt = cnt.reshape(n_tiles, 8, LANES)[:, 0, :N_EXPERTS].astype(I32)
    totals = jnp.sum(cnt, axis=0)
    tiles_e = (totals + TM_EXP - 1) // TM_EXP
    tile_end = jnp.cumsum(tiles_e)
    offset = (tile_end - tiles_e) * TM_EXP
    base = offset[None, :] + jnp.cumsum(cnt, axis=0) - cnt
    base = jnp.pad(base, ((0, 0), (0, LANES - N_EXPERTS))).astype(F32).reshape(n_tiles, 1, LANES)
    rows = t * TOP_K + N_EXPERTS * TM_EXP
    n_active = tile_end[-1]
    tile_id = jnp.arange(rows // TM_EXP, dtype=I32)
    tile_src = jnp.minimum(tile_id, n_active - 1)
    tile_expert = jnp.sum((tile_src[:, None] >= tile_end[None, :]).astype(I32), axis=1)
    tile_expert = jnp.minimum(tile_expert, N_EXPERTS - 1)
    row_in_expert = (tile_id - (tile_end - tiles_e)[tile_expert]) * TM_EXP
    tile_valid = jnp.clip(totals[tile_expert] - row_in_expert, 0, TM_EXP)
    tile_valid = jnp.where(tile_id < n_active, tile_valid, 0).astype(I32)
    after = tile_end[tile_expert]
    next_expert = jnp.where(after < n_active, tile_expert[jnp.minimum(after, n_active - 1)], -1)
    first_of_expert = jnp.concatenate(
        [jnp.ones((1,), I32), (tile_expert[1:] != tile_expert[:-1]).astype(I32)])
    tile_slot = (jnp.cumsum(first_of_expert) - 1) % 2

    pos = _positions(route, base)
    xs = _dispatch(pos, h2, rows)
    ys = _experts(tile_expert.astype(I32), next_expert.astype(I32), tile_slot.astype(I32),
                  tile_src.astype(I32), tile_valid, xs, w_gate, w_up, w_down)
    yg = _gather_rows(ys, pos.reshape(TOP_K * t))
    return _combine(yg, route, xmid, g_final, chunk, t_total, out_prev)
```

```python
import functools
import math

import jax
import jax.numpy as jnp
from jax import lax
from jax.experimental import pallas as pl
from jax.experimental.pallas import tpu as pltpu
from jax.experimental.pallas import tpu_sc as plsc

F32 = jnp.float32
BF16 = jnp.bfloat16
I32 = jnp.int32

D_MODEL = 1024
HEAD_DIM = 64
N_HEADS = 16
N_KV_HEADS = 4
GROUP = N_HEADS // N_KV_HEADS
KV_WIDTH = N_KV_HEADS * HEAD_DIM
WINDOW = 128
N_GROUPS = 4
EXPERTS_PER_GROUP = 8
N_EXPERTS = N_GROUPS * EXPERTS_PER_GROUP
TOP_K = 2
D_FF = 512
EPS = 1e-6
LANES = 128

REF_SPLITS = (1024, 2048, 3072, 4096, 4352, 4608, 5632)
REST_COLS = 2 * D_MODEL + 2 * KV_WIDTH
COL_GB, COL_Q = 0, 1
COL_K, COL_V = 2 * D_MODEL // KV_WIDTH, 2 * D_MODEL // KV_WIDTH + 1

TM_PROJ = 512
TQ_ATTN = 512
TM_MIX = 1024
SUB_MIX = 256
TM_CNT = 512
POS_TILES = 4
MOE_CHUNKS = 2
TM_EXP = 512
SUB_EXP = 256
TM_CMB = 1024
HALO_ROWS = 8
VMEM_LIMIT = 56 * 1024 * 1024
PACKED = D_MODEL // 2

SC_CORES = 2
SC_SUBCORES = 16
SC_WORKERS = SC_CORES * SC_SUBCORES
SC_CHUNK = 64


def _rms(x, g):
    r = lax.rsqrt(jnp.mean(x * x, axis=-1, keepdims=True) + EPS)
    return (x * r) * g


def _pack_bf16_pairs(x):
    n = x.shape[1] // 2
    lo = lax.bitcast_convert_type(x[:, :n].astype(BF16).astype(F32), I32)
    hi = lax.bitcast_convert_type(x[:, n:].astype(BF16).astype(F32), I32)
    return (hi & jnp.int32(-65536)) | lax.shift_right_logical(lo, 16)


def _unpack_bf16_pairs(p):
    lo = lax.bitcast_convert_type(lax.shift_left(p, 16), F32)
    hi = lax.bitcast_convert_type(p & jnp.int32(-65536), F32)
    return lo, hi


def _inproj_kernel(x_ref, xn_ref, g_ref, wcu_ref, wbg_ref, wrest_ref, cw_ref, cb_ref, wa_ref,
                   za_ref, proj_ref, halo_ref, h_ref, *, tiles_per_seq):
    i = pl.program_id(0)

    @pl.when(i == 0)
    def _():
        h_ref[0] = _rms(x_ref[...], g_ref[...]).astype(BF16)

    h = h_ref[i % 2]
    h_ref[(i + 1) % 2] = _rms(xn_ref[...], g_ref[...]).astype(BF16)
    pcu = jnp.dot(h, wcu_ref[...], preferred_element_type=F32)
    pbg = jnp.dot(h, wbg_ref[...], preferred_element_type=F32)
    proj_ref[...] = jnp.dot(h, wrest_ref[...], preferred_element_type=F32).astype(BF16)
    cu = pcu[:, :D_MODEL] * pcu[:, D_MODEL:]
    first = (i % tiles_per_seq) == 0
    hist = jnp.where(first, 0.0, halo_ref[...])
    prev1 = hist[HALO_ROWS - 1:HALO_ROWS]
    prev2 = hist[HALO_ROWS - 2:HALO_ROWS - 1]
    halo_ref[...] = cu[TM_PROJ - HALO_ROWS:, :]
    row = lax.broadcasted_iota(I32, cu.shape, 0)
    cu1 = jnp.where(row == 0, prev1, pltpu.roll(cu, 1, 0))
    cu2 = jnp.where(row == 0, prev2, jnp.where(row == 1, prev1, pltpu.roll(cu, 2, 0)))
    cw = cw_ref[...]
    y = cw[0:1] * cu2 + cw[1:2] * cu1 + cw[2:3] * cu + cb_ref[...]
    ya = (pbg[:, :D_MODEL] * y).astype(BF16)
    z = jnp.dot(ya, wa_ref[...], preferred_element_type=F32)
    za_ref[...] = (jax.nn.sigmoid(pbg[:, D_MODEL:]) * z).astype(BF16)


def _inproj(xf, g, w_cu, w_bg, w_rest, conv_w, conv_b, wa, seq):
    t = xf.shape[0]
    n_tiles = t // TM_PROJ
    const = lambda shape: pl.BlockSpec(shape, lambda i: (0, 0), pipeline_mode=pl.Buffered(1))
    return pl.pallas_call(
        functools.partial(_inproj_kernel, tiles_per_seq=seq // TM_PROJ),
        name="inproj_conv",
        grid=(n_tiles,),
        in_specs=[
            pl.BlockSpec((TM_PROJ, D_MODEL), lambda i: (i, 0)),
            pl.BlockSpec((TM_PROJ, D_MODEL), lambda i: (jnp.minimum(i + 1, n_tiles - 1), 0)),
            const((1, D_MODEL)),
            const((D_MODEL, 2 * D_MODEL)), const((D_MODEL, 2 * D_MODEL)), const((D_MODEL, REST_COLS)),
            const((3, D_MODEL)), const((1, D_MODEL)), const((D_MODEL, D_MODEL)),
        ],
        out_specs=[pl.BlockSpec((TM_PROJ, D_MODEL), lambda i: (i, 0)),
                   pl.BlockSpec((TM_PROJ, REST_COLS), lambda i: (i, 0))],
        out_shape=[jax.ShapeDtypeStruct((t, D_MODEL), BF16),
                   jax.ShapeDtypeStruct((t, REST_COLS), BF16)],
        scratch_shapes=[pltpu.VMEM((HALO_ROWS, D_MODEL), F32),
                        pltpu.VMEM((2, TM_PROJ, D_MODEL), BF16)],
        compiler_params=pltpu.CompilerParams(
            dimension_semantics=("arbitrary",), vmem_limit_bytes=VMEM_LIMIT),
    )(xf, xf, g, w_cu, w_bg, w_rest, conv_w, conv_b, wa)


def _attn_kernel(sink_ref, q_ref, k_ref, v_ref, kp_ref, vp_ref, o_ref, *, tiles_per_seq):
    first_tile = (pl.program_id(0) % tiles_per_seq) == 0
    ks = lax.broadcasted_iota(I32, (WINDOW, WINDOW), 0)
    qq = lax.broadcasted_iota(I32, (WINDOW, WINDOW), 1)
    own = ks <= qq
    dist = jnp.where(own, qq - ks, qq - ks + WINDOW).astype(F32)
    visible0 = jnp.logical_or(own, jnp.logical_not(first_tile))
    log2e = math.log2(math.e)
    c_scale = log2e / math.sqrt(HEAD_DIM)
    nt = (((1,), (1,)), ((), ()))
    zk = jnp.zeros((2 * WINDOW, HEAD_DIM), BF16)

    def transposed(v_blk):
        return jnp.transpose(v_blk.astype(F32)).astype(BF16)

    prev_k = kp_ref[...]
    prev_vt = transposed(vp_ref[...])
    for sb in range(TQ_ATTN // WINDOW):
        rows = slice(sb * WINDOW, (sb + 1) * WINDOW)
        cur_k = k_ref[rows, :]
        cur_vt = transposed(v_ref[rows, :])
        scores, vcats = [], []
        for kh in range(N_KV_HEADS):
            cols = slice(kh * HEAD_DIM, (kh + 1) * HEAD_DIM)
            kcat = jnp.concatenate([prev_k[:, cols], cur_k[:, cols]], axis=0)
            vcats.append(jnp.concatenate([prev_vt[cols, :], cur_vt[cols, :]], axis=1))
            qg = jnp.concatenate([q_ref[rows, (2 * kh) * LANES:(2 * kh + 1) * LANES],
                                  q_ref[rows, (2 * kh + 1) * LANES:(2 * kh + 2) * LANES]], axis=0)
            k_pad = jnp.concatenate([jnp.concatenate([kcat, zk], axis=1),
                                     jnp.concatenate([zk, kcat], axis=1)], axis=0)
            scores.append(lax.dot_general(k_pad, qg, nt, preferred_element_type=F32))
        probs, rdens = [], []
        for kh in range(N_KV_HEADS):
            for pos in range(2):
                pr, rd = [], []
                for half in range(2):
                    h = kh * GROUP + 2 * half + pos
                    slope = 2.0 ** (-8.0 * (h + 1) / N_HEADS)
                    qcols = slice(half * WINDOW, (half + 1) * WINDOW)
                    krow = pos * 2 * WINDOW
                    st = scores[kh]
                    s = (jnp.where(own, st[krow + WINDOW:krow + 2 * WINDOW, qcols],
                                   st[krow:krow + WINDOW, qcols]) * c_scale
                         - (slope * log2e) * dist)
                    if sb == 0:
                        s = jnp.where(visible0, s, -jnp.inf)
                    m = jnp.max(s, axis=0, keepdims=True)
                    p = jnp.exp2(s - m)
                    den = jnp.sum(p, axis=0, keepdims=True) + jnp.exp2(sink_ref[h] * log2e - m)
                    rd.append(1.0 / den)
                    pr.append(jnp.concatenate(
                        [jnp.where(own, 0.0, p).astype(BF16), jnp.where(own, p, 0.0).astype(BF16)],
                        axis=0))
                probs.append(jnp.concatenate(pr, axis=1))
                rdens.append(jnp.concatenate(rd, axis=1))
        out_t = [None] * N_HEADS
        for kh in range(N_KV_HEADS):
            for pos in range(2):
                o2 = jnp.dot(vcats[kh], probs[2 * kh + pos], preferred_element_type=F32)
                o2 = o2 * rdens[2 * kh + pos]
                out_t[kh * GROUP + pos] = o2[:, :WINDOW]
                out_t[kh * GROUP + 2 + pos] = o2[:, WINDOW:]
        o_ref[rows, :] = jnp.transpose(jnp.concatenate(out_t, axis=0)).astype(BF16)
        prev_k, prev_vt = cur_k, cur_vt


def _attention(proj, sinks, seq):
    t = proj.shape[0]
    sub = TQ_ATTN // WINDOW
    return pl.pallas_call(
        functools.partial(_attn_kernel, tiles_per_seq=seq // TQ_ATTN),
        name="swattn",
        grid=(t // TQ_ATTN,),
        in_specs=[
            pl.BlockSpec(memory_space=pltpu.SMEM),
            pl.BlockSpec((TQ_ATTN, D_MODEL), lambda i: (i, COL_Q)),
            pl.BlockSpec((TQ_ATTN, KV_WIDTH), lambda i: (i, COL_K)),
            pl.BlockSpec((TQ_ATTN, KV_WIDTH), lambda i: (i, COL_V)),
            pl.BlockSpec((WINDOW, KV_WIDTH), lambda i: (jnp.maximum(i * sub - 1, 0), COL_K)),
            pl.BlockSpec((WINDOW, KV_WIDTH), lambda i: (jnp.maximum(i * sub - 1, 0), COL_V)),
        ],
        out_specs=pl.BlockSpec((TQ_ATTN, D_MODEL), lambda i: (i, 0)),
        out_shape=jax.ShapeDtypeStruct((t, D_MODEL), BF16),
        compiler_params=pltpu.CompilerParams(
            dimension_semantics=("arbitrary",), vmem_limit_bytes=VMEM_LIMIT),
    )(sinks, proj, proj, proj, proj, proj)


def _mix_kernel(attn_ref, gb_ref, za_ref, x_ref, wb_ref, wo_ref, g_ref, wr_ref, br_ref,
                xmid_ref, h_ref, route_ref, cnt_ref):
    subs = [slice(s * SUB_MIX, (s + 1) * SUB_MIX) for s in range(TM_MIX // SUB_MIX)]
    yb = [jnp.dot(attn_ref[r, :], wb_ref[...], preferred_element_type=F32) for r in subs]
    merged = [(za_ref[r, :].astype(F32) + jax.nn.sigmoid(gb_ref[r, :].astype(F32)) * y).astype(BF16)
              for r, y in zip(subs, yb)]
    xm = [x_ref[r, :] + jnp.dot(m, wo_ref[...], preferred_element_type=F32)
          for r, m in zip(subs, merged)]
    hs = []
    for r, v in zip(subs, xm):
        xmid_ref[r, :] = _pack_bf16_pairs(v)
        h = _rms(v, g_ref[...])
        h_ref[r, :] = _pack_bf16_pairs(h)
        hs.append(h)
    wr = wr_ref[...]
    logits = []
    for h in hs:
        h_hi = h.astype(BF16)
        h_lo = (h - h_hi.astype(F32)).astype(BF16)
        both = jnp.dot(h_hi, wr, preferred_element_type=F32)
        logits.append(both[:, :LANES] + both[:, LANES:]
                      + jnp.dot(h_lo, wr[:, :LANES], preferred_element_type=F32) + br_ref[...])
    subs_per_cnt = TM_CNT // SUB_MIX
    cnts = [jnp.zeros((1, LANES), F32) for _ in range(TM_MIX // TM_CNT)]
    for s, (r, lg) in enumerate(zip(subs, logits)):
        route, onehot = _route(lg)
        route_ref[r, :] = route
        cnts[s // subs_per_cnt] = cnts[s // subs_per_cnt] + jnp.sum(onehot, axis=0, keepdims=True)
    for c, cnt in enumerate(cnts):
        cnt_ref[c * 8:(c + 1) * 8, :] = jnp.broadcast_to(cnt, (8, LANES))


def _route(logits):
    lane = lax.broadcasted_iota(I32, logits.shape, 1)
    neg = -jnp.inf
    gl = jnp.where(lane < N_GROUPS, logits, neg)
    gmax = jnp.max(gl, axis=-1, keepdims=True)
    g_idx = jnp.min(jnp.where(gl == gmax, lane, LANES), axis=-1, keepdims=True)
    p_g = 1.0 / jnp.sum(jnp.exp(gl - gmax), axis=-1, keepdims=True)
    start = N_GROUPS + EXPERTS_PER_GROUP * g_idx
    el = jnp.where((lane >= start) & (lane < start + EXPERTS_PER_GROUP), logits, neg)
    v1 = jnp.max(el, axis=-1, keepdims=True)
    i1 = jnp.min(jnp.where(el == v1, lane, LANES), axis=-1, keepdims=True)
    el2 = jnp.where(lane == i1, neg, el)
    v2 = jnp.max(el2, axis=-1, keepdims=True)
    i2 = jnp.min(jnp.where(el2 == v2, lane, LANES), axis=-1, keepdims=True)
    e21 = jnp.exp(v2 - v1)
    w1 = p_g / (1.0 + e21)
    w2 = p_g * e21 / (1.0 + e21)
    e1 = i1 - N_GROUPS
    e2 = i2 - N_GROUPS
    route = jnp.where(lane == 0, e1.astype(F32),
                      jnp.where(lane == 1, e2.astype(F32),
                                jnp.where(lane == 2, w1, jnp.where(lane == 3, w2, 0.0))))
    onehot = ((lane == e1) | (lane == e2)).astype(F32)
    return route, onehot


def _mix(attn, proj, za, xf, wb, wo, g, wr, br, chunk, t):
    n_tiles = t // TM_MIX
    first = chunk * n_tiles
    cnt_rows = TM_MIX // TM_CNT * 8
    full = lambda shape: pl.BlockSpec(shape, lambda i: (0, 0), pipeline_mode=pl.Buffered(1))
    tile = lambda w=D_MODEL: pl.BlockSpec((TM_MIX, w), lambda i: (i, 0))
    src = lambda c=0: pl.BlockSpec((TM_MIX, D_MODEL), lambda i: (first + i, c))
    return pl.pallas_call(
        _mix_kernel,
        name="merge_router",
        grid=(n_tiles,),
        in_specs=[
            src(), src(COL_GB), src(), src(),
            full((D_MODEL, D_MODEL)), full((D_MODEL, D_MODEL)), full((1, D_MODEL)),
            full((D_MODEL, 2 * LANES)), full((1, LANES)),
        ],
        out_specs=[tile(PACKED), tile(PACKED), tile(LANES),
                   pl.BlockSpec((cnt_rows, LANES), lambda i: (i, 0))],
        out_shape=[
            jax.ShapeDtypeStruct((t, PACKED), I32),
            jax.ShapeDtypeStruct((t, PACKED), I32),
            jax.ShapeDtypeStruct((t, LANES), F32),
            jax.ShapeDtypeStruct((n_tiles * cnt_rows, LANES), F32),
        ],
        compiler_params=pltpu.CompilerParams(
            dimension_semantics=("arbitrary",), vmem_limit_bytes=VMEM_LIMIT),
    )(attn, proj, za, xf, wb, wo, g, wr, br)


def _pos_kernel(route_ref, base_ref, pos_ref):
    lane = lax.broadcasted_iota(I32, (TM_CNT, LANES), 1)
    r = lax.broadcasted_iota(I32, (TM_CNT, TM_CNT), 0)
    c = lax.broadcasted_iota(I32, (TM_CNT, TM_CNT), 1)
    lower = (c < r).astype(BF16)
    subs = [slice(s * TM_CNT, (s + 1) * TM_CNT) for s in range(POS_TILES)]
    routes = [route_ref[rs, :] for rs in subs]
    e1 = [jnp.sum(jnp.where(lane == 0, rt, 0.0), axis=-1, keepdims=True).astype(I32) for rt in routes]
    e2 = [jnp.sum(jnp.where(lane == 1, rt, 0.0), axis=-1, keepdims=True).astype(I32) for rt in routes]
    onehot = [((lane == a) | (lane == b)).astype(BF16) for a, b in zip(e1, e2)]
    before = [jnp.dot(lower, oh, preferred_element_type=F32) + base_ref[s]
              for s, oh in enumerate(onehot)]
    for s, rs in enumerate(subs):
        p1 = jnp.sum(jnp.where(lane == e1[s], before[s], 0.0), axis=-1, keepdims=True)
        p2 = jnp.sum(jnp.where(lane == e2[s], before[s], 0.0), axis=-1, keepdims=True)
        packed = jnp.where(lane == 0, p1, jnp.where(lane == 1, p2, 0.0))
        pos_ref[:, rs] = jnp.transpose(packed)[0:TOP_K, :].astype(I32)


def _positions(route, base):
    t = route.shape[0]
    n_steps = t // (TM_CNT * POS_TILES)
    return pl.pallas_call(
        _pos_kernel,
        name="positions",
        grid=(n_steps,),
        in_specs=[
            pl.BlockSpec((TM_CNT * POS_TILES, LANES), lambda i: (i, 0)),
            pl.BlockSpec((POS_TILES, 1, LANES), lambda i: (i, 0, 0)),
        ],
        out_specs=pl.BlockSpec((TOP_K, TM_CNT * POS_TILES), lambda i: (0, i)),
        out_shape=jax.ShapeDtypeStruct((TOP_K, t), I32),
        compiler_params=pltpu.CompilerParams(dimension_semantics=("arbitrary",)),
    )(route, base)


def _sc_mesh():
    return plsc.VectorSubcoreMesh(core_axis_name="c", subcore_axis_name="s",
                                  num_cores=SC_CORES, num_subcores=SC_SUBCORES)


def _sc_worker():
    return lax.axis_index("s") * SC_CORES + lax.axis_index("c")


def _dispatch(pos, hp, rows):
    t = hp.shape[0]
    per_w = t // SC_WORKERS
    n_ch = per_w // SC_CHUNK
    pos4 = pos.reshape(TOP_K, SC_WORKERS, n_ch, SC_CHUNK)

    @functools.partial(
        pl.kernel, mesh=_sc_mesh(),
        out_type=jax.ShapeDtypeStruct((rows, PACKED), I32),
        scratch_types=[pltpu.VMEM((TOP_K, n_ch, SC_CHUNK), I32),
                       pltpu.VMEM((SC_CHUNK, PACKED), I32)])
    def scatter(hp_hbm, pos_hbm, xs_hbm, idx_v, rows_v):
        wid = _sc_worker()
        for k in range(TOP_K):
            pltpu.sync_copy(pos_hbm.at[k, wid], idx_v.at[k])

        def body(c, carry):
            start = pl.multiple_of(wid * per_w + c * SC_CHUNK, SC_CHUNK)
            pltpu.sync_copy(hp_hbm.at[pl.ds(start, SC_CHUNK)], rows_v)
            for k in range(TOP_K):
                pltpu.sync_copy(rows_v, xs_hbm.at[idx_v.at[k, c]])
            return carry

        lax.fori_loop(0, n_ch, body, 0)

    return scatter(hp, pos4)


def _gather_rows(table, idx):
    n = idx.shape[0]
    per_w = n // SC_WORKERS
    n_ch = per_w // SC_CHUNK
    idx3 = idx.reshape(SC_WORKERS, n_ch, SC_CHUNK)

    @functools.partial(
        pl.kernel, mesh=_sc_mesh(),
        out_type=jax.ShapeDtypeStruct((n, PACKED), I32),
        scratch_types=[pltpu.VMEM((n_ch, SC_CHUNK), I32),
                       pltpu.VMEM((SC_CHUNK, PACKED), I32)])
    def gather(table_hbm, idx_hbm, out_hbm, idx_v, rows_v):
        wid = _sc_worker()
        pltpu.sync_copy(idx_hbm.at[wid], idx_v)

        def body(c, carry):
            start = pl.multiple_of(wid * per_w + c * SC_CHUNK, SC_CHUNK)
            pltpu.sync_copy(table_hbm.at[idx_v.at[c]], rows_v)
            pltpu.sync_copy(rows_v, out_hbm.at[pl.ds(start, SC_CHUNK)])
            return carry

        lax.fori_loop(0, n_ch, body, 0)

    return gather(table, idx3)


def _expert_kernel(te_ref, nx_ref, sl_ref, ts_ref, tv_ref, xs_ref, wg_hbm, wu_hbm, wd_hbm, o_ref,
                   stage_g, stage_u, stage_d, wg_s, wu_s, wd_s, sem):
    del ts_ref
    i = pl.program_id(0)
    n_valid = tv_ref[i]
    expert = te_ref[i]
    slot = sl_ref[i]

    def weight_copies(e, s):
        return (pltpu.make_async_copy(wg_hbm.at[e], stage_g.at[s], sem.at[s, 0]),
                pltpu.make_async_copy(wu_hbm.at[e], stage_u.at[s], sem.at[s, 1]),
                pltpu.make_async_copy(wd_hbm.at[e], stage_d.at[s], sem.at[s, 2]))

    @pl.when(i == 0)
    def _():
        for copy in weight_copies(expert, slot):
            copy.start()

    @pl.when(jnp.logical_or(i == 0, expert != te_ref[jnp.maximum(i - 1, 0)]))
    def _():
        for copy in weight_copies(expert, slot):
            copy.wait()
        nxt = nx_ref[i]

        @pl.when(nxt >= 0)
        def _():
            for copy in weight_copies(nxt, 1 - slot):
                copy.start()

        wg_s[...] = stage_g[slot].astype(BF16)
        wu_s[...] = stage_u[slot].astype(BF16)
        wd_s[...] = stage_d[slot].astype(BF16)

    def mlp(n_sub):
        subs = [slice(s * SUB_EXP, (s + 1) * SUB_EXP) for s in range(n_sub)]
        xin = []
        for r in subs:
            rid = r.start + lax.broadcasted_iota(I32, (SUB_EXP, PACKED), 0)
            lo, hi = _unpack_bf16_pairs(jnp.where(rid < n_valid, xs_ref[r, :], 0))
            xin.append((lo.astype(BF16), hi.astype(BF16)))
        a = [jnp.dot(lo, wg_s[:PACKED, :], preferred_element_type=F32)
             + jnp.dot(hi, wg_s[PACKED:, :], preferred_element_type=F32) for lo, hi in xin]
        b = [jnp.dot(lo, wu_s[:PACKED, :], preferred_element_type=F32)
             + jnp.dot(hi, wu_s[PACKED:, :], preferred_element_type=F32) for lo, hi in xin]
        hmid = [(ai * jax.nn.sigmoid(ai) * bi).astype(BF16) for ai, bi in zip(a, b)]
        for r, hm in zip(subs, hmid):
            o_ref[r, :] = _pack_bf16_pairs(jnp.dot(hm, wd_s[...], preferred_element_type=F32))
        if n_sub * SUB_EXP < TM_EXP:
            o_ref[n_sub * SUB_EXP:, :] = jnp.zeros((TM_EXP - n_sub * SUB_EXP, PACKED), I32)

    n_subs = TM_EXP // SUB_EXP
    for n_sub in range(n_subs + 1):
        lo_rows = (n_sub - 1) * SUB_EXP if n_sub else -1
        in_range = jnp.logical_and(n_valid > lo_rows, n_valid <= n_sub * SUB_EXP)
        pl.when(in_range)(functools.partial(mlp, n_sub))


def _experts(tile_expert, next_expert, tile_slot, tile_src, tile_valid, xs, wg, wu, wd):
    rows = xs.shape[0]
    hbm = pl.BlockSpec(memory_space=pl.ANY)
    grid_spec = pltpu.PrefetchScalarGridSpec(
        num_scalar_prefetch=5,
        grid=(rows // TM_EXP,),
        in_specs=[pl.BlockSpec((TM_EXP, PACKED), lambda i, te, nx, sl, ts, tv: (ts[i], 0)),
                  hbm, hbm, hbm],
        out_specs=pl.BlockSpec((TM_EXP, PACKED), lambda i, te, nx, sl, ts, tv: (i, 0)),
        scratch_shapes=[
            pltpu.VMEM((2, D_MODEL, D_FF), F32), pltpu.VMEM((2, D_MODEL, D_FF), F32),
            pltpu.VMEM((2, D_FF, D_MODEL), F32),
            pltpu.VMEM((D_MODEL, D_FF), BF16), pltpu.VMEM((D_MODEL, D_FF), BF16),
            pltpu.VMEM((D_FF, D_MODEL), BF16),
            pltpu.SemaphoreType.DMA((2, 3)),
        ],
    )
    return pl.pallas_call(
        _expert_kernel,
        name="experts",
        grid_spec=grid_spec,
        out_shape=jax.ShapeDtypeStruct((rows, PACKED), I32),
        compiler_params=pltpu.CompilerParams(
            dimension_semantics=("arbitrary",), vmem_limit_bytes=VMEM_LIMIT),
    )(tile_expert, next_expert, tile_slot, tile_src, tile_valid, xs, wg, wu, wd)


def _combine_kernel(y1_ref, y2_ref, route_ref, xmid_ref, g_ref, *rest):
    o_ref = rest[-1]
    route = route_ref[...]
    lane = lax.broadcasted_iota(I32, route.shape, 1)
    w1 = jnp.sum(jnp.where(lane == 2, route, 0.0), axis=-1, keepdims=True)
    w2 = jnp.sum(jnp.where(lane == 3, route, 0.0), axis=-1, keepdims=True)
    lo1, hi1 = _unpack_bf16_pairs(y1_ref[...])
    lo2, hi2 = _unpack_bf16_pairs(y2_ref[...])
    lox, hix = _unpack_bf16_pairs(xmid_ref[...])
    x_out = jnp.concatenate([lox + (lo1 * w1 + lo2 * w2), hix + (hi1 * w1 + hi2 * w2)], axis=1)
    o_ref[...] = _rms(x_out, g_ref[...])


def _combine(yg, route, xmid, g, chunk, t_total, out_prev):
    t = xmid.shape[0]
    n_tiles = t // TM_CMB
    first = chunk * n_tiles
    in_specs = [
        pl.BlockSpec((TM_CMB, PACKED), lambda i: (i, 0)),
        pl.BlockSpec((TM_CMB, PACKED), lambda i: (n_tiles + i, 0)),
        pl.BlockSpec((TM_CMB, LANES), lambda i: (i, 0)),
        pl.BlockSpec((TM_CMB, PACKED), lambda i: (i, 0)),
        pl.BlockSpec((1, D_MODEL), lambda i: (0, 0)),
    ]
    args = [yg, yg, route, xmid, g]
    aliases = {}
    if out_prev is not None:
        in_specs.append(pl.BlockSpec(memory_space=pl.ANY))
        aliases = {len(args): 0}
        args.append(out_prev)
    return pl.pallas_call(
        _combine_kernel,
        name="combine",
        grid=(n_tiles,),
        in_specs=in_specs,
        out_specs=pl.BlockSpec((TM_CMB, D_MODEL), lambda i: (first + i, 0)),
        out_shape=jax.ShapeDtypeStruct((t_total, D_MODEL), F32),
        input_output_aliases=aliases,
        compiler_params=pltpu.CompilerParams(
            dimension_semantics=("arbitrary",), vmem_limit_bytes=VMEM_LIMIT),
    )(*args)


def _split_bf16(w):
    hi = w.astype(BF16)
    lo = (w - hi.astype(F32)).astype(BF16)
    return hi, lo


def kernel(x, norm_mix, w_in, conv_w, conv_b, w_a_out, sinks, w_b_out, w_o, norm_ffn, w_group,
           b_group, w_expert, b_expert, w_gate, w_up, w_down, norm_final):
    bsz, seq, d = x.shape
    t = bsz * seq
    assert d == D_MODEL and seq % TM_PROJ == 0 and seq % TQ_ATTN == 0
    xf = x.reshape(t, d)
    row = lambda v: v.reshape(1, -1)

    w_b, w_c, w_u, w_q, w_k, w_v, w_ga, w_gb = jnp.split(w_in, REF_SPLITS, axis=1)
    w_cu = jnp.concatenate([w_c, w_u], axis=1).astype(BF16)
    w_bg = jnp.concatenate([w_b, w_ga], axis=1).astype(BF16)
    w_rest = jnp.concatenate([w_gb, w_q, w_k, w_v], axis=1).astype(BF16)

    za, proj = _inproj(xf, row(norm_mix), w_cu, w_bg, w_rest, conv_w, row(conv_b),
                       w_a_out.astype(BF16), seq)
    attn = _attention(proj, sinks, seq)

    pad = LANES - N_GROUPS - N_EXPERTS
    w_r = jnp.concatenate([w_group, w_expert, jnp.zeros((d, pad), F32)], axis=1)
    b_r = jnp.concatenate([b_group, b_expert, jnp.zeros((pad,), F32)]).reshape(1, LANES)
    wr = jnp.concatenate(_split_bf16(w_r), axis=1)
    wb = w_b_out.astype(BF16)
    wo = w_o.astype(BF16)

    t_chunk = t // MOE_CHUNKS
    out = None
    for chunk in range(MOE_CHUNKS):
        xmid, h2, route, cnt = _mix(attn, proj, za, xf, wb, wo, row(norm_ffn), wr, b_r, chunk, t_chunk)
        out = _moe_chunk(xmid, h2, route, cnt, w_gate, w_up, w_down, row(norm_final), chunk, t, out)
    return out.reshape(bsz, seq, d)


def _moe_chunk(xmid, h2, route, cnt, w_gate, w_up, w_down, g_final, chunk, t_total, out_prev):
    t = xmid.shape[0]
    n_tiles = t // TM_CNT
    cnt = cnt.reshape(n_tiles, 8, LANES)[:, 0, :N_EXPERTS].astype(I32)
    totals = jnp.sum(cnt, axis=0)
    tiles_e = (totals + TM_EXP - 1) // TM_EXP
    tile_end = jnp.cumsum(tiles_e)
    offset = (tile_end - tiles_e) * TM_EXP
    base = offset[None, :] + jnp.cumsum(cnt, axis=0) - cnt
    base = jnp.pad(base, ((0, 0), (0, LANES - N_EXPERTS))).astype(F32).reshape(n_tiles, 1, LANES)
    rows = t * TOP_K + N_EXPERTS * TM_EXP
    n_active = tile_end[-1]
    tile_id = jnp.arange(rows // TM_EXP, dtype=I32)
    tile_src = jnp.minimum(tile_id, n_active - 1)
    tile_expert = jnp.sum((tile_src[:, None] >= tile_end[None, :]).astype(I32), axis=1)
    tile_expert = jnp.minimum(tile_expert, N_EXPERTS - 1)
    row_in_expert = (tile_id - (tile_end - tiles_e)[tile_expert]) * TM_EXP
    tile_valid = jnp.clip(totals[tile_expert] - row_in_expert, 0, TM_EXP)
    tile_valid = jnp.where(tile_id < n_active, tile_valid, 0).astype(I32)
    after = tile_end[tile_expert]
    next_expert = jnp.where(after < n_active, tile_expert[jnp.minimum(after, n_active - 1)], -1)
    first_of_expert = jnp.concatenate(
        [jnp.ones((1,), I32), (tile_expert[1:] != tile_expert[:-1]).astype(I32)])
    tile_slot = (jnp.cumsum(first_of_expert) - 1) % 2

    pos = _positions(route, base)
    xs = _dispatch(pos, h2, rows)
    ys = _experts(tile_expert.astype(I32), next_expert.astype(I32), tile_slot.astype(I32),
                  tile_src.astype(I32), tile_valid, xs, w_gate, w_up, w_down)
    yg = _gather_rows(ys, pos.reshape(TOP_K * t))
    return _combine(yg, route, xmid, g_final, chunk, t_total, out_prev)
```

```python
import functools
import math

import jax
import jax.numpy as jnp
from jax import lax
from jax.experimental import pallas as pl
from jax.experimental.pallas import tpu as pltpu
from jax.experimental.pallas import tpu_sc as plsc

F32 = jnp.float32
BF16 = jnp.bfloat16
I32 = jnp.int32

D_MODEL = 1024
HEAD_DIM = 64
N_HEADS = 16
N_KV_HEADS = 4
GROUP = N_HEADS // N_KV_HEADS
KV_WIDTH = N_KV_HEADS * HEAD_DIM
WINDOW = 128
N_GROUPS = 4
EXPERTS_PER_GROUP = 8
N_EXPERTS = N_GROUPS * EXPERTS_PER_GROUP
TOP_K = 2
D_FF = 512
EPS = 1e-6
LANES = 128

REF_SPLITS = (1024, 2048, 3072, 4096, 4352, 4608, 5632)
REST_COLS = 2 * D_MODEL + 2 * KV_WIDTH
COL_GB, COL_Q = 0, 1
COL_K, COL_V = 2 * D_MODEL // KV_WIDTH, 2 * D_MODEL // KV_WIDTH + 1

TM_PROJ = 512
TQ_ATTN = 512
TM_MIX = 1024
SUB_MIX = 256
TM_CNT = 512
POS_TILES = 4
MOE_CHUNKS = 2
TM_EXP = 1024
SUB_EXP = 256
TM_CMB = 1024
HALO_ROWS = 8
VMEM_LIMIT = 56 * 1024 * 1024
PACKED = D_MODEL // 2

SC_CORES = 2
SC_SUBCORES = 16
SC_WORKERS = SC_CORES * SC_SUBCORES
SC_CHUNK = 64


def _rms(x, g):
    r = lax.rsqrt(jnp.mean(x * x, axis=-1, keepdims=True) + EPS)
    return (x * r) * g


def _pack_bf16_pairs(x):
    n = x.shape[1] // 2
    lo = lax.bitcast_convert_type(x[:, :n].astype(BF16).astype(F32), I32)
    hi = lax.bitcast_convert_type(x[:, n:].astype(BF16).astype(F32), I32)
    return (hi & jnp.int32(-65536)) | lax.shift_right_logical(lo, 16)


def _unpack_bf16_pairs(p):
    lo = lax.bitcast_convert_type(lax.shift_left(p, 16), F32)
    hi = lax.bitcast_convert_type(p & jnp.int32(-65536), F32)
    return lo, hi


def _inproj_kernel(x_ref, g_ref, wcu_ref, wbg_ref, wrest_ref, cw_ref, cb_ref, wa_ref,
                   za_ref, proj_ref, halo_ref, *, tiles_per_seq):
    i = pl.program_id(0)
    h = _rms(x_ref[...], g_ref[...]).astype(BF16)
    pcu = jnp.dot(h, wcu_ref[...], preferred_element_type=F32)
    pbg = jnp.dot(h, wbg_ref[...], preferred_element_type=F32)
    proj_ref[...] = jnp.dot(h, wrest_ref[...], preferred_element_type=F32).astype(BF16)
    cu = pcu[:, :D_MODEL] * pcu[:, D_MODEL:]
    first = (i % tiles_per_seq) == 0
    hist = jnp.where(first, 0.0, halo_ref[...])
    prev1 = hist[HALO_ROWS - 1:HALO_ROWS]
    prev2 = hist[HALO_ROWS - 2:HALO_ROWS - 1]
    halo_ref[...] = cu[TM_PROJ - HALO_ROWS:, :]
    row = lax.broadcasted_iota(I32, cu.shape, 0)
    cu1 = jnp.where(row == 0, prev1, pltpu.roll(cu, 1, 0))
    cu2 = jnp.where(row == 0, prev2, jnp.where(row == 1, prev1, pltpu.roll(cu, 2, 0)))
    cw = cw_ref[...]
    y = cw[0:1] * cu2 + cw[1:2] * cu1 + cw[2:3] * cu + cb_ref[...]
    ya = (pbg[:, :D_MODEL] * y).astype(BF16)
    z = jnp.dot(ya, wa_ref[...], preferred_element_type=F32)
    za_ref[...] = (jax.nn.sigmoid(pbg[:, D_MODEL:]) * z).astype(BF16)


def _inproj(xf, g, w_cu, w_bg, w_rest, conv_w, conv_b, wa, seq):
    t = xf.shape[0]
    n_tiles = t // TM_PROJ
    const = lambda shape: pl.BlockSpec(shape, lambda i: (0, 0), pipeline_mode=pl.Buffered(1))
    return pl.pallas_call(
        functools.partial(_inproj_kernel, tiles_per_seq=seq // TM_PROJ),
        name="inproj_conv",
        grid=(n_tiles,),
        in_specs=[
            pl.BlockSpec((TM_PROJ, D_MODEL), lambda i: (i, 0)),
            const((1, D_MODEL)),
            const((D_MODEL, 2 * D_MODEL)), const((D_MODEL, 2 * D_MODEL)), const((D_MODEL, REST_COLS)),
            const((3, D_MODEL)), const((1, D_MODEL)), const((D_MODEL, D_MODEL)),
        ],
        out_specs=[pl.BlockSpec((TM_PROJ, D_MODEL), lambda i: (i, 0)),
                   pl.BlockSpec((TM_PROJ, REST_COLS), lambda i: (i, 0))],
        out_shape=[jax.ShapeDtypeStruct((t, D_MODEL), BF16),
                   jax.ShapeDtypeStruct((t, REST_COLS), BF16)],
        scratch_shapes=[pltpu.VMEM((HALO_ROWS, D_MODEL), F32)],
        compiler_params=pltpu.CompilerParams(
            dimension_semantics=("arbitrary",), vmem_limit_bytes=VMEM_LIMIT),
    )(xf, g, w_cu, w_bg, w_rest, conv_w, conv_b, wa)


def _attn_kernel(sink_ref, q_ref, k_ref, v_ref, kp_ref, vp_ref, o_ref, *, tiles_per_seq):
    first_tile = (pl.program_id(0) % tiles_per_seq) == 0
    ks = lax.broadcasted_iota(I32, (WINDOW, WINDOW), 0)
    qq = lax.broadcasted_iota(I32, (WINDOW, WINDOW), 1)
    own = ks <= qq
    dist = jnp.where(own, qq - ks, qq - ks + WINDOW).astype(F32)
    visible0 = jnp.logical_or(own, jnp.logical_not(first_tile))
    log2e = math.log2(math.e)
    c_scale = log2e / math.sqrt(HEAD_DIM)
    nt = (((1,), (1,)), ((), ()))
    zk = jnp.zeros((2 * WINDOW, HEAD_DIM), BF16)

    def transposed(v_blk):
        return jnp.transpose(v_blk.astype(F32)).astype(BF16)

    prev_k = kp_ref[...]
    prev_vt = transposed(vp_ref[...])
    for sb in range(TQ_ATTN // WINDOW):
        rows = slice(sb * WINDOW, (sb + 1) * WINDOW)
        cur_k = k_ref[rows, :]
        cur_vt = transposed(v_ref[rows, :])
        scores, vcats = [], []
        for kh in range(N_KV_HEADS):
            cols = slice(kh * HEAD_DIM, (kh + 1) * HEAD_DIM)
            kcat = jnp.concatenate([prev_k[:, cols], cur_k[:, cols]], axis=0)
            vcats.append(jnp.concatenate([prev_vt[cols, :], cur_vt[cols, :]], axis=1))
            qg = jnp.concatenate([q_ref[rows, (2 * kh) * LANES:(2 * kh + 1) * LANES],
                                  q_ref[rows, (2 * kh + 1) * LANES:(2 * kh + 2) * LANES]], axis=0)
            k_pad = jnp.concatenate([jnp.concatenate([kcat, zk], axis=1),
                                     jnp.concatenate([zk, kcat], axis=1)], axis=0)
            scores.append(lax.dot_general(k_pad, qg, nt, preferred_element_type=F32))
        probs, rdens = [], []
        for kh in range(N_KV_HEADS):
            for pos in range(2):
                pr, rd = [], []
                for half in range(2):
                    h = kh * GROUP + 2 * half + pos
                    slope = 2.0 ** (-8.0 * (h + 1) / N_HEADS)
                    qcols = slice(half * WINDOW, (half + 1) * WINDOW)
                    krow = pos * 2 * WINDOW
                    st = scores[kh]
                    s = (jnp.where(own, st[krow + WINDOW:krow + 2 * WINDOW, qcols],
                                   st[krow:krow + WINDOW, qcols]) * c_scale
                         - (slope * log2e) * dist)
                    if sb == 0:
                        s = jnp.where(visible0, s, -jnp.inf)
                    m = jnp.max(s, axis=0, keepdims=True)
                    p = jnp.exp2(s - m)
                    den = jnp.sum(p, axis=0, keepdims=True) + jnp.exp2(sink_ref[h] * log2e - m)
                    rd.append(1.0 / den)
                    pr.append(jnp.concatenate(
                        [jnp.where(own, 0.0, p).astype(BF16), jnp.where(own, p, 0.0).astype(BF16)],
                        axis=0))
                probs.append(jnp.concatenate(pr, axis=1))
                rdens.append(jnp.concatenate(rd, axis=1))
        out_t = [None] * N_HEADS
        for kh in range(N_KV_HEADS):
            for pos in range(2):
                o2 = jnp.dot(vcats[kh], probs[2 * kh + pos], preferred_element_type=F32)
                o2 = o2 * rdens[2 * kh + pos]
                out_t[kh * GROUP + pos] = o2[:, :WINDOW]
                out_t[kh * GROUP + 2 + pos] = o2[:, WINDOW:]
        o_ref[rows, :] = jnp.transpose(jnp.concatenate(out_t, axis=0)).astype(BF16)
        prev_k, prev_vt = cur_k, cur_vt


def _attention(proj, sinks, seq):
    t = proj.shape[0]
    sub = TQ_ATTN // WINDOW
    return pl.pallas_call(
        functools.partial(_attn_kernel, tiles_per_seq=seq // TQ_ATTN),
        name="swattn",
        grid=(t // TQ_ATTN,),
        in_specs=[
            pl.BlockSpec(memory_space=pltpu.SMEM),
            pl.BlockSpec((TQ_ATTN, D_MODEL), lambda i: (i, COL_Q)),
            pl.BlockSpec((TQ_ATTN, KV_WIDTH), lambda i: (i, COL_K)),
            pl.BlockSpec((TQ_ATTN, KV_WIDTH), lambda i: (i, COL_V)),
            pl.BlockSpec((WINDOW, KV_WIDTH), lambda i: (jnp.maximum(i * sub - 1, 0), COL_K)),
            pl.BlockSpec((WINDOW, KV_WIDTH), lambda i: (jnp.maximum(i * sub - 1, 0), COL_V)),
        ],
        out_specs=pl.BlockSpec((TQ_ATTN, D_MODEL), lambda i: (i, 0)),
        out_shape=jax.ShapeDtypeStruct((t, D_MODEL), BF16),
        compiler_params=pltpu.CompilerParams(
            dimension_semantics=("arbitrary",), vmem_limit_bytes=VMEM_LIMIT),
    )(sinks, proj, proj, proj, proj, proj)


def _mix_kernel(attn_ref, gb_ref, za_ref, x_ref, wb_ref, wo_ref, g_ref, wr_ref, br_ref,
                xmid_ref, h_ref, route_ref, cnt_ref):
    subs = [slice(s * SUB_MIX, (s + 1) * SUB_MIX) for s in range(TM_MIX // SUB_MIX)]
    yb = [jnp.dot(attn_ref[r, :], wb_ref[...], preferred_element_type=F32) for r in subs]
    merged = [(za_ref[r, :].astype(F32) + jax.nn.sigmoid(gb_ref[r, :].astype(F32)) * y).astype(BF16)
              for r, y in zip(subs, yb)]
    xm = [x_ref[r, :] + jnp.dot(m, wo_ref[...], preferred_element_type=F32)
          for r, m in zip(subs, merged)]
    hs = []
    for r, v in zip(subs, xm):
        xmid_ref[r, :] = _pack_bf16_pairs(v)
        h = _rms(v, g_ref[...])
        h_ref[r, :] = _pack_bf16_pairs(h)
        hs.append(h)
    wr = wr_ref[...]
    logits = []
    for h in hs:
        h_hi = h.astype(BF16)
        h_lo = (h - h_hi.astype(F32)).astype(BF16)
        both = jnp.dot(h_hi, wr, preferred_element_type=F32)
        logits.append(both[:, :LANES] + both[:, LANES:]
                      + jnp.dot(h_lo, wr[:, :LANES], preferred_element_type=F32) + br_ref[...])
    subs_per_cnt = TM_CNT // SUB_MIX
    cnts = [jnp.zeros((1, LANES), F32) for _ in range(TM_MIX // TM_CNT)]
    for s, (r, lg) in enumerate(zip(subs, logits)):
        route, onehot = _route(lg)
        route_ref[r, :] = route
        cnts[s // subs_per_cnt] = cnts[s // subs_per_cnt] + jnp.sum(onehot, axis=0, keepdims=True)
    for c, cnt in enumerate(cnts):
        cnt_ref[c * 8:(c + 1) * 8, :] = jnp.broadcast_to(cnt, (8, LANES))


def _route(logits):
    lane = lax.broadcasted_iota(I32, logits.shape, 1)
    neg = -jnp.inf
    gl = jnp.where(lane < N_GROUPS, logits, neg)
    gmax = jnp.max(gl, axis=-1, keepdims=True)
    g_idx = jnp.min(jnp.where(gl == gmax, lane, LANES), axis=-1, keepdims=True)
    p_g = 1.0 / jnp.sum(jnp.exp(gl - gmax), axis=-1, keepdims=True)
    start = N_GROUPS + EXPERTS_PER_GROUP * g_idx
    el = jnp.where((lane >= start) & (lane < start + EXPERTS_PER_GROUP), logits, neg)
    v1 = jnp.max(el, axis=-1, keepdims=True)
    i1 = jnp.min(jnp.where(el == v1, lane, LANES), axis=-1, keepdims=True)
    el2 = jnp.where(lane == i1, neg, el)
    v2 = jnp.max(el2, axis=-1, keepdims=True)
    i2 = jnp.min(jnp.where(el2 == v2, lane, LANES), axis=-1, keepdims=True)
    e21 = jnp.exp(v2 - v1)
    w1 = p_g / (1.0 + e21)
    w2 = p_g * e21 / (1.0 + e21)
    e1 = i1 - N_GROUPS
    e2 = i2 - N_GROUPS
    route = jnp.where(lane == 0, e1.astype(F32),
                      jnp.where(lane == 1, e2.astype(F32),
                                jnp.where(lane == 2, w1, jnp.where(lane == 3, w2, 0.0))))
    onehot = ((lane == e1) | (lane == e2)).astype(F32)
    return route, onehot


def _mix(attn, proj, za, xf, wb, wo, g, wr, br, chunk, t):
    n_tiles = t // TM_MIX
    first = chunk * n_tiles
    cnt_rows = TM_MIX // TM_CNT * 8
    full = lambda shape: pl.BlockSpec(shape, lambda i: (0, 0), pipeline_mode=pl.Buffered(1))
    tile = lambda w=D_MODEL: pl.BlockSpec((TM_MIX, w), lambda i: (i, 0))
    src = lambda c=0: pl.BlockSpec((TM_MIX, D_MODEL), lambda i: (first + i, c))
    return pl.pallas_call(
        _mix_kernel,
        name="merge_router",
        grid=(n_tiles,),
        in_specs=[
            src(), src(COL_GB), src(), src(),
            full((D_MODEL, D_MODEL)), full((D_MODEL, D_MODEL)), full((1, D_MODEL)),
            full((D_MODEL, 2 * LANES)), full((1, LANES)),
        ],
        out_specs=[tile(PACKED), tile(PACKED), tile(LANES),
                   pl.BlockSpec((cnt_rows, LANES), lambda i: (i, 0))],
        out_shape=[
            jax.ShapeDtypeStruct((t, PACKED), I32),
            jax.ShapeDtypeStruct((t, PACKED), I32),
            jax.ShapeDtypeStruct((t, LANES), F32),
            jax.ShapeDtypeStruct((n_tiles * cnt_rows, LANES), F32),
        ],
        compiler_params=pltpu.CompilerParams(
            dimension_semantics=("arbitrary",), vmem_limit_bytes=VMEM_LIMIT),
    )(attn, proj, za, xf, wb, wo, g, wr, br)


def _pos_kernel(route_ref, base_ref, pos_ref):
    lane = lax.broadcasted_iota(I32, (TM_CNT, LANES), 1)
    r = lax.broadcasted_iota(I32, (TM_CNT, TM_CNT), 0)
    c = lax.broadcasted_iota(I32, (TM_CNT, TM_CNT), 1)
    lower = (c < r).astype(BF16)
    subs = [slice(s * TM_CNT, (s + 1) * TM_CNT) for s in range(POS_TILES)]
    routes = [route_ref[rs, :] for rs in subs]
    e1 = [jnp.sum(jnp.where(lane == 0, rt, 0.0), axis=-1, keepdims=True).astype(I32) for rt in routes]
    e2 = [jnp.sum(jnp.where(lane == 1, rt, 0.0), axis=-1, keepdims=True).astype(I32) for rt in routes]
    onehot = [((lane == a) | (lane == b)).astype(BF16) for a, b in zip(e1, e2)]
    before = [jnp.dot(lower, oh, preferred_element_type=F32) + base_ref[s]
              for s, oh in enumerate(onehot)]
    for s, rs in enumerate(subs):
        p1 = jnp.sum(jnp.where(lane == e1[s], before[s], 0.0), axis=-1, keepdims=True)
        p2 = jnp.sum(jnp.where(lane == e2[s], before[s], 0.0), axis=-1, keepdims=True)
        packed = jnp.where(lane == 0, p1, jnp.where(lane == 1, p2, 0.0))
        pos_ref[:, rs] = jnp.transpose(packed)[0:TOP_K, :].astype(I32)


def _positions(route, base):
    t = route.shape[0]
    n_steps = t // (TM_CNT * POS_TILES)
    return pl.pallas_call(
        _pos_kernel,
        name="positions",
        grid=(n_steps,),
        in_specs=[
            pl.BlockSpec((TM_CNT * POS_TILES, LANES), lambda i: (i, 0)),
            pl.BlockSpec((POS_TILES, 1, LANES), lambda i: (i, 0, 0)),
        ],
        out_specs=pl.BlockSpec((TOP_K, TM_CNT * POS_TILES), lambda i: (0, i)),
        out_shape=jax.ShapeDtypeStruct((TOP_K, t), I32),
        compiler_params=pltpu.CompilerParams(dimension_semantics=("arbitrary",)),
    )(route, base)


def _sc_mesh():
    return plsc.VectorSubcoreMesh(core_axis_name="c", subcore_axis_name="s",
                                  num_cores=SC_CORES, num_subcores=SC_SUBCORES)


def _sc_worker():
    return lax.axis_index("s") * SC_CORES + lax.axis_index("c")


def _dispatch(pos, hp, rows):
    t = hp.shape[0]
    per_w = t // SC_WORKERS
    n_ch = per_w // SC_CHUNK
    pos4 = pos.reshape(TOP_K, SC_WORKERS, n_ch, SC_CHUNK)

    @functools.partial(
        pl.kernel, mesh=_sc_mesh(),
        out_type=jax.ShapeDtypeStruct((rows, PACKED), I32),
        scratch_types=[pltpu.VMEM((TOP_K, n_ch, SC_CHUNK), I32),
                       pltpu.VMEM((SC_CHUNK, PACKED), I32)])
    def scatter(hp_hbm, pos_hbm, xs_hbm, idx_v, rows_v):
        wid = _sc_worker()
        for k in range(TOP_K):
            pltpu.sync_copy(pos_hbm.at[k, wid], idx_v.at[k])

        def body(c, carry):
            start = pl.multiple_of(wid * per_w + c * SC_CHUNK, SC_CHUNK)
            pltpu.sync_copy(hp_hbm.at[pl.ds(start, SC_CHUNK)], rows_v)
            for k in range(TOP_K):
                pltpu.sync_copy(rows_v, xs_hbm.at[idx_v.at[k, c]])
            return carry

        lax.fori_loop(0, n_ch, body, 0)

    return scatter(hp, pos4)


def _gather_rows(table, idx):
    n = idx.shape[0]
    per_w = n // SC_WORKERS
    n_ch = per_w // SC_CHUNK
    idx3 = idx.reshape(SC_WORKERS, n_ch, SC_CHUNK)

    @functools.partial(
        pl.kernel, mesh=_sc_mesh(),
        out_type=jax.ShapeDtypeStruct((n, PACKED), I32),
        scratch_types=[pltpu.VMEM((n_ch, SC_CHUNK), I32),
                       pltpu.VMEM((SC_CHUNK, PACKED), I32)])
    def gather(table_hbm, idx_hbm, out_hbm, idx_v, rows_v):
        wid = _sc_worker()
        pltpu.sync_copy(idx_hbm.at[wid], idx_v)

        def body(c, carry):
            start = pl.multiple_of(wid * per_w + c * SC_CHUNK, SC_CHUNK)
            pltpu.sync_copy(table_hbm.at[idx_v.at[c]], rows_v)
            pltpu.sync_copy(rows_v, out_hbm.at[pl.ds(start, SC_CHUNK)])
            return carry

        lax.fori_loop(0, n_ch, body, 0)

    return gather(table, idx3)


def _expert_kernel(te_ref, nx_ref, sl_ref, ts_ref, tv_ref, xs_ref, wg_hbm, wu_hbm, wd_hbm, o_ref,
                   stage_g, stage_u, stage_d, wg_s, wu_s, wd_s, sem):
    del ts_ref
    i = pl.program_id(0)
    n_valid = tv_ref[i]
    expert = te_ref[i]
    slot = sl_ref[i]

    def weight_copies(e, s):
        return (pltpu.make_async_copy(wg_hbm.at[e], stage_g.at[s], sem.at[s, 0]),
                pltpu.make_async_copy(wu_hbm.at[e], stage_u.at[s], sem.at[s, 1]),
                pltpu.make_async_copy(wd_hbm.at[e], stage_d.at[s], sem.at[s, 2]))

    @pl.when(i == 0)
    def _():
        for copy in weight_copies(expert, slot):
            copy.start()

    @pl.when(jnp.logical_or(i == 0, expert != te_ref[jnp.maximum(i - 1, 0)]))
    def _():
        for copy in weight_copies(expert, slot):
            copy.wait()
        nxt = nx_ref[i]

        @pl.when(nxt >= 0)
        def _():
            for copy in weight_copies(nxt, 1 - slot):
                copy.start()

        wg_s[...] = stage_g[slot].astype(BF16)
        wu_s[...] = stage_u[slot].astype(BF16)
        wd_s[...] = stage_d[slot].astype(BF16)

    def mlp(n_sub):
        subs = [slice(s * SUB_EXP, (s + 1) * SUB_EXP) for s in range(n_sub)]
        xin = []
        for r in subs:
            rid = r.start + lax.broadcasted_iota(I32, (SUB_EXP, PACKED), 0)
            lo, hi = _unpack_bf16_pairs(jnp.where(rid < n_valid, xs_ref[r, :], 0))
            xin.append((lo.astype(BF16), hi.astype(BF16)))
        ab = [(jnp.dot(lo, wg_s[:PACKED, :], preferred_element_type=F32)
               + jnp.dot(hi, wg_s[PACKED:, :], preferred_element_type=F32),
               jnp.dot(lo, wu_s[:PACKED, :], preferred_element_type=F32)
               + jnp.dot(hi, wu_s[PACKED:, :], preferred_element_type=F32)) for lo, hi in xin]
        for r, (ai, bi) in zip(subs, ab):
            hm = (ai * jax.nn.sigmoid(ai) * bi).astype(BF16)
            o_ref[r, :] = _pack_bf16_pairs(jnp.dot(hm, wd_s[...], preferred_element_type=F32))
        if n_sub * SUB_EXP < TM_EXP:
            o_ref[n_sub * SUB_EXP:, :] = jnp.zeros((TM_EXP - n_sub * SUB_EXP, PACKED), I32)

    n_subs = TM_EXP // SUB_EXP
    for n_sub in range(n_subs + 1):
        lo_rows = (n_sub - 1) * SUB_EXP if n_sub else -1
        in_range = jnp.logical_and(n_valid > lo_rows, n_valid <= n_sub * SUB_EXP)
        pl.when(in_range)(functools.partial(mlp, n_sub))


def _experts(tile_expert, next_expert, tile_slot, tile_src, tile_valid, xs, wg, wu, wd):
    rows = xs.shape[0]
    hbm = pl.BlockSpec(memory_space=pl.ANY)
    grid_spec = pltpu.PrefetchScalarGridSpec(
        num_scalar_prefetch=5,
        grid=(rows // TM_EXP,),
        in_specs=[pl.BlockSpec((TM_EXP, PACKED), lambda i, te, nx, sl, ts, tv: (ts[i], 0)),
                  hbm, hbm, hbm],
        out_specs=pl.BlockSpec((TM_EXP, PACKED), lambda i, te, nx, sl, ts, tv: (i, 0)),
        scratch_shapes=[
            pltpu.VMEM((2, D_MODEL, D_FF), F32), pltpu.VMEM((2, D_MODEL, D_FF), F32),
            pltpu.VMEM((2, D_FF, D_MODEL), F32),
            pltpu.VMEM((D_MODEL, D_FF), BF16), pltpu.VMEM((D_MODEL, D_FF), BF16),
            pltpu.VMEM((D_FF, D_MODEL), BF16),
            pltpu.SemaphoreType.DMA((2, 3)),
        ],
    )
    return pl.pallas_call(
        _expert_kernel,
        name="experts",
        grid_spec=grid_spec,
        out_shape=jax.ShapeDtypeStruct((rows, PACKED), I32),
        compiler_params=pltpu.CompilerParams(
            dimension_semantics=("arbitrary",), vmem_limit_bytes=VMEM_LIMIT),
    )(tile_expert, next_expert, tile_slot, tile_src, tile_valid, xs, wg, wu, wd)


def _combine_kernel(y1_ref, y2_ref, route_ref, xmid_ref, g_ref, *rest):
    o_ref = rest[-1]
    route = route_ref[...]
    lane = lax.broadcasted_iota(I32, route.shape, 1)
    w1 = jnp.sum(jnp.where(lane == 2, route, 0.0), axis=-1, keepdims=True)
    w2 = jnp.sum(jnp.where(lane == 3, route, 0.0), axis=-1, keepdims=True)
    lo1, hi1 = _unpack_bf16_pairs(y1_ref[...])
    lo2, hi2 = _unpack_bf16_pairs(y2_ref[...])
    lox, hix = _unpack_bf16_pairs(xmid_ref[...])
    x_out = jnp.concatenate([lox + (lo1 * w1 + lo2 * w2), hix + (hi1 * w1 + hi2 * w2)], axis=1)
    o_ref[...] = _rms(x_out, g_ref[...])


def _combine(yg, route, xmid, g, chunk, t_total, out_prev):
    t = xmid.shape[0]
    n_tiles = t // TM_CMB
    first = chunk * n_tiles
    in_specs = [
        pl.BlockSpec((TM_CMB, PACKED), lambda i: (i, 0)),
        pl.BlockSpec((TM_CMB, PACKED), lambda i: (n_tiles + i, 0)),
        pl.BlockSpec((TM_CMB, LANES), lambda i: (i, 0)),
        pl.BlockSpec((TM_CMB, PACKED), lambda i: (i, 0)),
        pl.BlockSpec((1, D_MODEL), lambda i: (0, 0)),
    ]
    args = [yg, yg, route, xmid, g]
    aliases = {}
    if out_prev is not None:
        in_specs.append(pl.BlockSpec(memory_space=pl.ANY))
        aliases = {len(args): 0}
        args.append(out_prev)
    return pl.pallas_call(
        _combine_kernel,
        name="combine",
        grid=(n_tiles,),
        in_specs=in_specs,
        out_specs=pl.BlockSpec((TM_CMB, D_MODEL), lambda i: (first + i, 0)),
        out_shape=jax.ShapeDtypeStruct((t_total, D_MODEL), F32),
        input_output_aliases=aliases,
        compiler_params=pltpu.CompilerParams(
            dimension_semantics=("arbitrary",), vmem_limit_bytes=VMEM_LIMIT),
    )(*args)


def _split_bf16(w):
    hi = w.astype(BF16)
    lo = (w - hi.astype(F32)).astype(BF16)
    return hi, lo


def kernel(x, norm_mix, w_in, conv_w, conv_b, w_a_out, sinks, w_b_out, w_o, norm_ffn, w_group,
           b_group, w_expert, b_expert, w_gate, w_up, w_down, norm_final):
    bsz, seq, d = x.shape
    t = bsz * seq
    assert d == D_MODEL and seq % TM_PROJ == 0 and seq % TQ_ATTN == 0
    xf = x.reshape(t, d)
    row = lambda v: v.reshape(1, -1)

    w_b, w_c, w_u, w_q, w_k, w_v, w_ga, w_gb = jnp.split(w_in, REF_SPLITS, axis=1)
    w_cu = jnp.concatenate([w_c, w_u], axis=1).astype(BF16)
    w_bg = jnp.concatenate([w_b, w_ga], axis=1).astype(BF16)
    w_rest = jnp.concatenate([w_gb, w_q, w_k, w_v], axis=1).astype(BF16)

    za, proj = _inproj(xf, row(norm_mix), w_cu, w_bg, w_rest, conv_w, row(conv_b),
                       w_a_out.astype(BF16), seq)
    attn = _attention(proj, sinks, seq)

    pad = LANES - N_GROUPS - N_EXPERTS
    w_r = jnp.concatenate([w_group, w_expert, jnp.zeros((d, pad), F32)], axis=1)
    b_r = jnp.concatenate([b_group, b_expert, jnp.zeros((pad,), F32)]).reshape(1, LANES)
    wr = jnp.concatenate(_split_bf16(w_r), axis=1)
    wb = w_b_out.astype(BF16)
    wo = w_o.astype(BF16)

    t_chunk = t // MOE_CHUNKS
    out = None
    for chunk in range(MOE_CHUNKS):
        xmid, h2, route, cnt = _mix(attn, proj, za, xf, wb, wo, row(norm_ffn), wr, b_r, chunk, t_chunk)
        out = _moe_chunk(xmid, h2, route, cnt, w_gate, w_up, w_down, row(norm_final), chunk, t, out)
    return out.reshape(bsz, seq, d)


def _moe_chunk(xmid, h2, route, cnt, w_gate, w_up, w_down, g_final, chunk, t_total, out_prev):
    t = xmid.shape[0]
    n_tiles = t // TM_CNT
    cnt = cnt.reshape(n_tiles, 8, LANES)[:, 0, :N_EXPERTS].astype(I32)
    totals = jnp.sum(cnt, axis=0)
    tiles_e = (totals + TM_EXP - 1) // TM_EXP
    tile_end = jnp.cumsum(tiles_e)
    offset = (tile_end - tiles_e) * TM_EXP
    base = offset[None, :] + jnp.cumsum(cnt, axis=0) - cnt
    base = jnp.pad(base, ((0, 0), (0, LANES - N_EXPERTS))).astype(F32).reshape(n_tiles, 1, LANES)
    rows = t * TOP_K + N_EXPERTS * TM_EXP
    n_active = tile_end[-1]
    tile_id = jnp.arange(rows // TM_EXP, dtype=I32)
    tile_src = jnp.minimum(tile_id, n_active - 1)
    tile_expert = jnp.sum((tile_src[:, None] >= tile_end[None, :]).astype(I32), axis=1)
    tile_expert = jnp.minimum(tile_expert, N_EXPERTS - 1)
    row_in_expert = (tile_id - (tile_end - tiles_e)[tile_expert]) * TM_EXP
    tile_valid = jnp.clip(totals[tile_expert] - row_in_expert, 0, TM_EXP)
    tile_valid = jnp.where(tile_id < n_active, tile_valid, 0).astype(I32)
    after = tile_end[tile_expert]
    next_expert = jnp.where(after < n_active, tile_expert[jnp.minimum(after, n_active - 1)], -1)
    first_of_expert = jnp.concatenate(
        [jnp.ones((1,), I32), (tile_expert[1:] != tile_expert[:-1]).astype(I32)])
    tile_slot = (jnp.cumsum(first_of_expert) - 1) % 2

    pos = _positions(route, base)
    xs = _dispatch(pos, h2, rows)
    ys = _experts(tile_expert.astype(I32), next_expert.astype(I32), tile_slot.astype(I32),
                  tile_src.astype(I32), tile_valid, xs, w_gate, w_up, w_down)
    yg = _gather_rows(ys, pos.reshape(TOP_K * t))
    return _combine(yg, route, xmid, g_final, chunk, t_total, out_prev)
```

```python
import functools
import math

import jax
import jax.numpy as jnp
from jax import lax
from jax.experimental import pallas as pl
from jax.experimental.pallas import tpu as pltpu
from jax.experimental.pallas import tpu_sc as plsc

F32 = jnp.float32
BF16 = jnp.bfloat16
I32 = jnp.int32

D_MODEL = 1024
HEAD_DIM = 64
N_HEADS = 16
N_KV_HEADS = 4
GROUP = N_HEADS // N_KV_HEADS
KV_WIDTH = N_KV_HEADS * HEAD_DIM
WINDOW = 128
N_GROUPS = 4
EXPERTS_PER_GROUP = 8
N_EXPERTS = N_GROUPS * EXPERTS_PER_GROUP
TOP_K = 2
D_FF = 512
EPS = 1e-6
LANES = 128

REF_SPLITS = (1024, 2048, 3072, 4096, 4352, 4608, 5632)
REST_COLS = 2 * D_MODEL + 2 * KV_WIDTH
COL_GB, COL_Q = 0, 1
COL_K, COL_V = 2 * D_MODEL // KV_WIDTH, 2 * D_MODEL // KV_WIDTH + 1

TM_PROJ = 512
TQ_ATTN = 512
TM_MIX = 1024
SUB_MIX = 256
TM_CNT = 512
POS_TILES = 4
MOE_CHUNKS = 2
TM_EXP = 512
SUB_EXP = 256
TM_CMB = 1024
HALO_ROWS = 8
VMEM_LIMIT = 56 * 1024 * 1024
PACKED = D_MODEL // 2

SC_CORES = 2
SC_SUBCORES = 16
SC_WORKERS = SC_CORES * SC_SUBCORES
SC_CHUNK = 64


def _rms(x, g):
    r = lax.rsqrt(jnp.mean(x * x, axis=-1, keepdims=True) + EPS)
    return (x * r) * g


def _pack_bf16_pairs(x):
    n = x.shape[1] // 2
    lo = lax.bitcast_convert_type(x[:, :n].astype(BF16).astype(F32), I32)
    hi = lax.bitcast_convert_type(x[:, n:].astype(BF16).astype(F32), I32)
    return (hi & jnp.int32(-65536)) | lax.shift_right_logical(lo, 16)


def _unpack_bf16_pairs(p):
    lo = lax.bitcast_convert_type(lax.shift_left(p, 16), F32)
    hi = lax.bitcast_convert_type(p & jnp.int32(-65536), F32)
    return lo, hi


def _inproj_kernel(x_ref, g_ref, wcu_ref, wbg_ref, wrest_ref, cw_ref, cb_ref, wa_ref,
                   eg_ref, eu_ref, ed_ref,
                   za_ref, proj_ref, eg_out, eu_out, ed_out, halo_ref, *, tiles_per_seq):
    i = pl.program_id(0)
    eg_out[...] = eg_ref[...].astype(BF16)
    eu_out[...] = eu_ref[...].astype(BF16)
    ed_out[...] = ed_ref[...].astype(BF16)
    h = _rms(x_ref[...], g_ref[...]).astype(BF16)
    pcu = jnp.dot(h, wcu_ref[...], preferred_element_type=F32)
    pbg = jnp.dot(h, wbg_ref[...], preferred_element_type=F32)
    proj_ref[...] = jnp.dot(h, wrest_ref[...], preferred_element_type=F32).astype(BF16)
    cu = pcu[:, :D_MODEL] * pcu[:, D_MODEL:]
    first = (i % tiles_per_seq) == 0
    hist = jnp.where(first, 0.0, halo_ref[...])
    prev1 = hist[HALO_ROWS - 1:HALO_ROWS]
    prev2 = hist[HALO_ROWS - 2:HALO_ROWS - 1]
    halo_ref[...] = cu[TM_PROJ - HALO_ROWS:, :]
    row = lax.broadcasted_iota(I32, cu.shape, 0)
    cu1 = jnp.where(row == 0, prev1, pltpu.roll(cu, 1, 0))
    cu2 = jnp.where(row == 0, prev2, jnp.where(row == 1, prev1, pltpu.roll(cu, 2, 0)))
    cw = cw_ref[...]
    y = cw[0:1] * cu2 + cw[1:2] * cu1 + cw[2:3] * cu + cb_ref[...]
    ya = (pbg[:, :D_MODEL] * y).astype(BF16)
    z = jnp.dot(ya, wa_ref[...], preferred_element_type=F32)
    za_ref[...] = (jax.nn.sigmoid(pbg[:, D_MODEL:]) * z).astype(BF16)


def _inproj(xf, g, w_cu, w_bg, w_rest, conv_w, conv_b, wa, seq, w_gate, w_up, w_down):
    t = xf.shape[0]
    n_tiles = t // TM_PROJ
    const = lambda shape: pl.BlockSpec(shape, lambda i: (0, 0), pipeline_mode=pl.Buffered(1))
    slabs = [w.reshape(-1, w.shape[-1]) for w in (w_gate, w_up, w_down)]
    slab_rows = [s.shape[0] // n_tiles for s in slabs]
    assert all(s.shape[0] == r * n_tiles and r % 16 == 0 for s, r in zip(slabs, slab_rows))
    slab_specs = [pl.BlockSpec((r, s.shape[1]), lambda i: (i, 0)) for s, r in zip(slabs, slab_rows)]
    outs = pl.pallas_call(
        functools.partial(_inproj_kernel, tiles_per_seq=seq // TM_PROJ),
        name="inproj_conv",
        grid=(n_tiles,),
        in_specs=[
            pl.BlockSpec((TM_PROJ, D_MODEL), lambda i: (i, 0)),
            const((1, D_MODEL)),
            const((D_MODEL, 2 * D_MODEL)), const((D_MODEL, 2 * D_MODEL)), const((D_MODEL, REST_COLS)),
            const((3, D_MODEL)), const((1, D_MODEL)), const((D_MODEL, D_MODEL)),
        ] + slab_specs,
        out_specs=[pl.BlockSpec((TM_PROJ, D_MODEL), lambda i: (i, 0)),
                   pl.BlockSpec((TM_PROJ, REST_COLS), lambda i: (i, 0))] + slab_specs,
        out_shape=[jax.ShapeDtypeStruct((t, D_MODEL), BF16),
                   jax.ShapeDtypeStruct((t, REST_COLS), BF16)]
                  + [jax.ShapeDtypeStruct(s.shape, BF16) for s in slabs],
        scratch_shapes=[pltpu.VMEM((HALO_ROWS, D_MODEL), F32)],
        compiler_params=pltpu.CompilerParams(
            dimension_semantics=("arbitrary",), vmem_limit_bytes=VMEM_LIMIT),
    )(xf, g, w_cu, w_bg, w_rest, conv_w, conv_b, wa, *slabs)
    za, proj, eg, eu, ed = outs
    return za, proj, eg.reshape(w_gate.shape), eu.reshape(w_up.shape), ed.reshape(w_down.shape)


def _attn_kernel(sink_ref, q_ref, k_ref, v_ref, kp_ref, vp_ref, o_ref, *, tiles_per_seq):
    first_tile = (pl.program_id(0) % tiles_per_seq) == 0
    ks = lax.broadcasted_iota(I32, (WINDOW, WINDOW), 0)
    qq = lax.broadcasted_iota(I32, (WINDOW, WINDOW), 1)
    own = ks <= qq
    dist = jnp.where(own, qq - ks, qq - ks + WINDOW).astype(F32)
    visible0 = jnp.logical_or(own, jnp.logical_not(first_tile))
    log2e = math.log2(math.e)
    c_scale = log2e / math.sqrt(HEAD_DIM)
    nt = (((1,), (1,)), ((), ()))
    zk = jnp.zeros((2 * WINDOW, HEAD_DIM), BF16)

    def transposed(v_blk):
        return jnp.transpose(v_blk.astype(F32)).astype(BF16)

    prev_k = kp_ref[...]
    prev_vt = transposed(vp_ref[...])
    for sb in range(TQ_ATTN // WINDOW):
        rows = slice(sb * WINDOW, (sb + 1) * WINDOW)
        cur_k = k_ref[rows, :]
        cur_vt = transposed(v_ref[rows, :])
        scores, vcats = [], []
        for kh in range(N_KV_HEADS):
            cols = slice(kh * HEAD_DIM, (kh + 1) * HEAD_DIM)
            kcat = jnp.concatenate([prev_k[:, cols], cur_k[:, cols]], axis=0)
            vcats.append(jnp.concatenate([prev_vt[cols, :], cur_vt[cols, :]], axis=1))
            qg = jnp.concatenate([q_ref[rows, (2 * kh) * LANES:(2 * kh + 1) * LANES],
                                  q_ref[rows, (2 * kh + 1) * LANES:(2 * kh + 2) * LANES]], axis=0)
            k_pad = jnp.concatenate([jnp.concatenate([kcat, zk], axis=1),
                                     jnp.concatenate([zk, kcat], axis=1)], axis=0)
            scores.append(lax.dot_general(k_pad, qg, nt, preferred_element_type=F32))
        probs, rdens = [], []
        for kh in range(N_KV_HEADS):
            for pos in range(2):
                pr, rd = [], []
                for half in range(2):
                    h = kh * GROUP + 2 * half + pos
                    slope = 2.0 ** (-8.0 * (h + 1) / N_HEADS)
                    qcols = slice(half * WINDOW, (half + 1) * WINDOW)
                    krow = pos * 2 * WINDOW
                    st = scores[kh]
                    s = (jnp.where(own, st[krow + WINDOW:krow + 2 * WINDOW, qcols],
                                   st[krow:krow + WINDOW, qcols]) * c_scale
                         - (slope * log2e) * dist)
                    if sb == 0:
                        s = jnp.where(visible0, s, -jnp.inf)
                    m = jnp.max(s, axis=0, keepdims=True)
                    p = jnp.exp2(s - m)
                    den = jnp.sum(p, axis=0, keepdims=True) + jnp.exp2(sink_ref[h] * log2e - m)
                    rd.append(1.0 / den)
                    pr.append(jnp.concatenate(
                        [jnp.where(own, 0.0, p).astype(BF16), jnp.where(own, p, 0.0).astype(BF16)],
                        axis=0))
                probs.append(jnp.concatenate(pr, axis=1))
                rdens.append(jnp.concatenate(rd, axis=1))
        out_t = [None] * N_HEADS
        for kh in range(N_KV_HEADS):
            for pos in range(2):
                o2 = jnp.dot(vcats[kh], probs[2 * kh + pos], preferred_element_type=F32)
                o2 = o2 * rdens[2 * kh + pos]
                out_t[kh * GROUP + pos] = o2[:, :WINDOW]
                out_t[kh * GROUP + 2 + pos] = o2[:, WINDOW:]
        o_ref[rows, :] = jnp.transpose(jnp.concatenate(out_t, axis=0)).astype(BF16)
        prev_k, prev_vt = cur_k, cur_vt


def _attention(proj, sinks, seq):
    t = proj.shape[0]
    sub = TQ_ATTN // WINDOW
    return pl.pallas_call(
        functools.partial(_attn_kernel, tiles_per_seq=seq // TQ_ATTN),
        name="swattn",
        grid=(t // TQ_ATTN,),
        in_specs=[
            pl.BlockSpec(memory_space=pltpu.SMEM),
            pl.BlockSpec((TQ_ATTN, D_MODEL), lambda i: (i, COL_Q)),
            pl.BlockSpec((TQ_ATTN, KV_WIDTH), lambda i: (i, COL_K)),
            pl.BlockSpec((TQ_ATTN, KV_WIDTH), lambda i: (i, COL_V)),
            pl.BlockSpec((WINDOW, KV_WIDTH), lambda i: (jnp.maximum(i * sub - 1, 0), COL_K)),
            pl.BlockSpec((WINDOW, KV_WIDTH), lambda i: (jnp.maximum(i * sub - 1, 0), COL_V)),
        ],
        out_specs=pl.BlockSpec((TQ_ATTN, D_MODEL), lambda i: (i, 0)),
        out_shape=jax.ShapeDtypeStruct((t, D_MODEL), BF16),
        compiler_params=pltpu.CompilerParams(
            dimension_semantics=("arbitrary",), vmem_limit_bytes=VMEM_LIMIT),
    )(sinks, proj, proj, proj, proj, proj)


def _mix_kernel(attn_ref, gb_ref, za_ref, x_ref, wb_ref, wo_ref, g_ref, wr_ref, br_ref,
                xmid_ref, h_ref, route_ref, cnt_ref):
    subs = [slice(s * SUB_MIX, (s + 1) * SUB_MIX) for s in range(TM_MIX // SUB_MIX)]
    yb = [jnp.dot(attn_ref[r, :], wb_ref[...], preferred_element_type=F32) for r in subs]
    merged = [(za_ref[r, :].astype(F32) + jax.nn.sigmoid(gb_ref[r, :].astype(F32)) * y).astype(BF16)
              for r, y in zip(subs, yb)]
    xm = [x_ref[r, :] + jnp.dot(m, wo_ref[...], preferred_element_type=F32)
          for r, m in zip(subs, merged)]
    hs = []
    for r, v in zip(subs, xm):
        xmid_ref[r, :] = _pack_bf16_pairs(v)
        h = _rms(v, g_ref[...])
        h_ref[r, :] = _pack_bf16_pairs(h)
        hs.append(h)
    wr = wr_ref[...]
    logits = []
    for h in hs:
        h_hi = h.astype(BF16)
        h_lo = (h - h_hi.astype(F32)).astype(BF16)
        both = jnp.dot(h_hi, wr, preferred_element_type=F32)
        logits.append(both[:, :LANES] + both[:, LANES:]
                      + jnp.dot(h_lo, wr[:, :LANES], preferred_element_type=F32) + br_ref[...])
    subs_per_cnt = TM_CNT // SUB_MIX
    cnts = [jnp.zeros((1, LANES), F32) for _ in range(TM_MIX // TM_CNT)]
    for s, (r, lg) in enumerate(zip(subs, logits)):
        route, onehot = _route(lg)
        route_ref[r, :] = route
        cnts[s // subs_per_cnt] = cnts[s // subs_per_cnt] + jnp.sum(onehot, axis=0, keepdims=True)
    for c, cnt in enumerate(cnts):
        cnt_ref[c * 8:(c + 1) * 8, :] = jnp.broadcast_to(cnt, (8, LANES))


def _route(logits):
    lane = lax.broadcasted_iota(I32, logits.shape, 1)
    neg = -jnp.inf
    gl = jnp.where(lane < N_GROUPS, logits, neg)
    gmax = jnp.max(gl, axis=-1, keepdims=True)
    g_idx = jnp.min(jnp.where(gl == gmax, lane, LANES), axis=-1, keepdims=True)
    p_g = 1.0 / jnp.sum(jnp.exp(gl - gmax), axis=-1, keepdims=True)
    start = N_GROUPS + EXPERTS_PER_GROUP * g_idx
    el = jnp.where((lane >= start) & (lane < start + EXPERTS_PER_GROUP), logits, neg)
    v1 = jnp.max(el, axis=-1, keepdims=True)
    i1 = jnp.min(jnp.where(el == v1, lane, LANES), axis=-1, keepdims=True)
    el2 = jnp.where(lane == i1, neg, el)
    v2 = jnp.max(el2, axis=-1, keepdims=True)
    i2 = jnp.min(jnp.where(el2 == v2, lane, LANES), axis=-1, keepdims=True)
    e21 = jnp.exp(v2 - v1)
    w1 = p_g / (1.0 + e21)
    w2 = p_g * e21 / (1.0 + e21)
    e1 = i1 - N_GROUPS
    e2 = i2 - N_GROUPS
    route = jnp.where(lane == 0, e1.astype(F32),
                      jnp.where(lane == 1, e2.astype(F32),
                                jnp.where(lane == 2, w1, jnp.where(lane == 3, w2, 0.0))))
    onehot = ((lane == e1) | (lane == e2)).astype(F32)
    return route, onehot


def _mix(attn, proj, za, xf, wb, wo, g, wr, br, chunk, t):
    n_tiles = t // TM_MIX
    first = chunk * n_tiles
    cnt_rows = TM_MIX // TM_CNT * 8
    full = lambda shape: pl.BlockSpec(shape, lambda i: (0, 0), pipeline_mode=pl.Buffered(1))
    tile = lambda w=D_MODEL: pl.BlockSpec((TM_MIX, w), lambda i: (i, 0))
    src = lambda c=0: pl.BlockSpec((TM_MIX, D_MODEL), lambda i: (first + i, c))
    return pl.pallas_call(
        _mix_kernel,
        name="merge_router",
        grid=(n_tiles,),
        in_specs=[
            src(), src(COL_GB), src(), src(),
            full((D_MODEL, D_MODEL)), full((D_MODEL, D_MODEL)), full((1, D_MODEL)),
            full((D_MODEL, 2 * LANES)), full((1, LANES)),
        ],
        out_specs=[tile(PACKED), tile(PACKED), tile(LANES),
                   pl.BlockSpec((cnt_rows, LANES), lambda i: (i, 0))],
        out_shape=[
            jax.ShapeDtypeStruct((t, PACKED), I32),
            jax.ShapeDtypeStruct((t, PACKED), I32),
            jax.ShapeDtypeStruct((t, LANES), F32),
            jax.ShapeDtypeStruct((n_tiles * cnt_rows, LANES), F32),
        ],
        compiler_params=pltpu.CompilerParams(
            dimension_semantics=("arbitrary",), vmem_limit_bytes=VMEM_LIMIT),
    )(attn, proj, za, xf, wb, wo, g, wr, br)


def _pos_kernel(route_ref, base_ref, pos_ref):
    lane = lax.broadcasted_iota(I32, (TM_CNT, LANES), 1)
    r = lax.broadcasted_iota(I32, (TM_CNT, TM_CNT), 0)
    c = lax.broadcasted_iota(I32, (TM_CNT, TM_CNT), 1)
    lower = (c < r).astype(BF16)
    subs = [slice(s * TM_CNT, (s + 1) * TM_CNT) for s in range(POS_TILES)]
    routes = [route_ref[rs, :] for rs in subs]
    e1 = [jnp.sum(jnp.where(lane == 0, rt, 0.0), axis=-1, keepdims=True).astype(I32) for rt in routes]
    e2 = [jnp.sum(jnp.where(lane == 1, rt, 0.0), axis=-1, keepdims=True).astype(I32) for rt in routes]
    onehot = [((lane == a) | (lane == b)).astype(BF16) for a, b in zip(e1, e2)]
    before = [jnp.dot(lower, oh, preferred_element_type=F32) + base_ref[s]
              for s, oh in enumerate(onehot)]
    for s, rs in enumerate(subs):
        p1 = jnp.sum(jnp.where(lane == e1[s], before[s], 0.0), axis=-1, keepdims=True)
        p2 = jnp.sum(jnp.where(lane == e2[s], before[s], 0.0), axis=-1, keepdims=True)
        packed = jnp.where(lane == 0, p1, jnp.where(lane == 1, p2, 0.0))
        pos_ref[:, rs] = jnp.transpose(packed)[0:TOP_K, :].astype(I32)


def _positions(route, base):
    t = route.shape[0]
    n_steps = t // (TM_CNT * POS_TILES)
    return pl.pallas_call(
        _pos_kernel,
        name="positions",
        grid=(n_steps,),
        in_specs=[
            pl.BlockSpec((TM_CNT * POS_TILES, LANES), lambda i: (i, 0)),
            pl.BlockSpec((POS_TILES, 1, LANES), lambda i: (i, 0, 0)),
        ],
        out_specs=pl.BlockSpec((TOP_K, TM_CNT * POS_TILES), lambda i: (0, i)),
        out_shape=jax.ShapeDtypeStruct((TOP_K, t), I32),
        compiler_params=pltpu.CompilerParams(dimension_semantics=("arbitrary",)),
    )(route, base)


def _sc_mesh():
    return plsc.VectorSubcoreMesh(core_axis_name="c", subcore_axis_name="s",
                                  num_cores=SC_CORES, num_subcores=SC_SUBCORES)


def _sc_worker():
    return lax.axis_index("s") * SC_CORES + lax.axis_index("c")


def _dispatch(pos, hp, rows):
    t = hp.shape[0]
    per_w = t // SC_WORKERS
    n_ch = per_w // SC_CHUNK
    pos4 = pos.reshape(TOP_K, SC_WORKERS, n_ch, SC_CHUNK)

    @functools.partial(
        pl.kernel, mesh=_sc_mesh(),
        out_type=jax.ShapeDtypeStruct((rows, PACKED), I32),
        scratch_types=[pltpu.VMEM((TOP_K, n_ch, SC_CHUNK), I32),
                       pltpu.VMEM((SC_CHUNK, PACKED), I32)])
    def scatter(hp_hbm, pos_hbm, xs_hbm, idx_v, rows_v):
        wid = _sc_worker()
        for k in range(TOP_K):
            pltpu.sync_copy(pos_hbm.at[k, wid], idx_v.at[k])

        def body(c, carry):
            start = pl.multiple_of(wid * per_w + c * SC_CHUNK, SC_CHUNK)
            pltpu.sync_copy(hp_hbm.at[pl.ds(start, SC_CHUNK)], rows_v)
            for k in range(TOP_K):
                pltpu.sync_copy(rows_v, xs_hbm.at[idx_v.at[k, c]])
            return carry

        lax.fori_loop(0, n_ch, body, 0)

    return scatter(hp, pos4)


def _gather_rows(table, idx):
    n = idx.shape[0]
    per_w = n // SC_WORKERS
    n_ch = per_w // SC_CHUNK
    idx3 = idx.reshape(SC_WORKERS, n_ch, SC_CHUNK)

    @functools.partial(
        pl.kernel, mesh=_sc_mesh(),
        out_type=jax.ShapeDtypeStruct((n, PACKED), I32),
        scratch_types=[pltpu.VMEM((n_ch, SC_CHUNK), I32),
                       pltpu.VMEM((SC_CHUNK, PACKED), I32)])
    def gather(table_hbm, idx_hbm, out_hbm, idx_v, rows_v):
        wid = _sc_worker()
        pltpu.sync_copy(idx_hbm.at[wid], idx_v)

        def body(c, carry):
            start = pl.multiple_of(wid * per_w + c * SC_CHUNK, SC_CHUNK)
            pltpu.sync_copy(table_hbm.at[idx_v.at[c]], rows_v)
            pltpu.sync_copy(rows_v, out_hbm.at[pl.ds(start, SC_CHUNK)])
            return carry

        lax.fori_loop(0, n_ch, body, 0)

    return gather(table, idx3)


def _expert_kernel(te_ref, nx_ref, sl_ref, ts_ref, tv_ref, xs_ref, wg_hbm, wu_hbm, wd_hbm, o_ref,
                   wg_s, wu_s, wd_s, sem):
    del ts_ref
    i = pl.program_id(0)
    n_valid = tv_ref[i]
    expert = te_ref[i]
    slot = sl_ref[i]

    def weight_copies(e, s):
        return (pltpu.make_async_copy(wg_hbm.at[e], wg_s.at[s], sem.at[s, 0]),
                pltpu.make_async_copy(wu_hbm.at[e], wu_s.at[s], sem.at[s, 1]),
                pltpu.make_async_copy(wd_hbm.at[e], wd_s.at[s], sem.at[s, 2]))

    @pl.when(i == 0)
    def _():
        for copy in weight_copies(expert, slot):
            copy.start()

    @pl.when(jnp.logical_or(i == 0, expert != te_ref[jnp.maximum(i - 1, 0)]))
    def _():
        for copy in weight_copies(expert, slot):
            copy.wait()
        nxt = nx_ref[i]

        @pl.when(nxt >= 0)
        def _():
            for copy in weight_copies(nxt, 1 - slot):
                copy.start()

    def mlp(n_sub):
        wg, wu, wd = wg_s.at[slot], wu_s.at[slot], wd_s.at[slot]
        subs = [slice(s * SUB_EXP, (s + 1) * SUB_EXP) for s in range(n_sub)]
        xin = []
        for r in subs:
            rid = r.start + lax.broadcasted_iota(I32, (SUB_EXP, PACKED), 0)
            lo, hi = _unpack_bf16_pairs(jnp.where(rid < n_valid, xs_ref[r, :], 0))
            xin.append((lo.astype(BF16), hi.astype(BF16)))
        ab = [(jnp.dot(lo, wg[:PACKED, :], preferred_element_type=F32)
               + jnp.dot(hi, wg[PACKED:, :], preferred_element_type=F32),
               jnp.dot(lo, wu[:PACKED, :], preferred_element_type=F32)
               + jnp.dot(hi, wu[PACKED:, :], preferred_element_type=F32)) for lo, hi in xin]
        for r, (ai, bi) in zip(subs, ab):
            hm = (ai * jax.nn.sigmoid(ai) * bi).astype(BF16)
            o_ref[r, :] = _pack_bf16_pairs(jnp.dot(hm, wd[...], preferred_element_type=F32))
        if n_sub * SUB_EXP < TM_EXP:
            o_ref[n_sub * SUB_EXP:, :] = jnp.zeros((TM_EXP - n_sub * SUB_EXP, PACKED), I32)

    n_subs = TM_EXP // SUB_EXP
    for n_sub in range(n_subs + 1):
        lo_rows = (n_sub - 1) * SUB_EXP if n_sub else -1
        in_range = jnp.logical_and(n_valid > lo_rows, n_valid <= n_sub * SUB_EXP)
        pl.when(in_range)(functools.partial(mlp, n_sub))


def _experts(tile_expert, next_expert, tile_slot, tile_src, tile_valid, xs, wg, wu, wd):
    rows = xs.shape[0]
    hbm = pl.BlockSpec(memory_space=pl.ANY)
    grid_spec = pltpu.PrefetchScalarGridSpec(
        num_scalar_prefetch=5,
        grid=(rows // TM_EXP,),
        in_specs=[pl.BlockSpec((TM_EXP, PACKED), lambda i, te, nx, sl, ts, tv: (ts[i], 0)),
                  hbm, hbm, hbm],
        out_specs=pl.BlockSpec((TM_EXP, PACKED), lambda i, te, nx, sl, ts, tv: (i, 0)),
        scratch_shapes=[
            pltpu.VMEM((2, D_MODEL, D_FF), BF16), pltpu.VMEM((2, D_MODEL, D_FF), BF16),
            pltpu.VMEM((2, D_FF, D_MODEL), BF16),
            pltpu.SemaphoreType.DMA((2, 3)),
        ],
    )
    return pl.pallas_call(
        _expert_kernel,
        name="experts",
        grid_spec=grid_spec,
        out_shape=jax.ShapeDtypeStruct((rows, PACKED), I32),
        compiler_params=pltpu.CompilerParams(
            dimension_semantics=("arbitrary",), vmem_limit_bytes=VMEM_LIMIT),
    )(tile_expert, next_expert, tile_slot, tile_src, tile_valid, xs, wg, wu, wd)


def _combine_kernel(y1_ref, y2_ref, route_ref, xmid_ref, g_ref, *rest):
    o_ref = rest[-1]
    route = route_ref[...]
    lane = lax.broadcasted_iota(I32, route.shape, 1)
    w1 = jnp.sum(jnp.where(lane == 2, route, 0.0), axis=-1, keepdims=True)
    w2 = jnp.sum(jnp.where(lane == 3, route, 0.0), axis=-1, keepdims=True)
    lo1, hi1 = _unpack_bf16_pairs(y1_ref[...])
    lo2, hi2 = _unpack_bf16_pairs(y2_ref[...])
    lox, hix = _unpack_bf16_pairs(xmid_ref[...])
    x_out = jnp.concatenate([lox + (lo1 * w1 + lo2 * w2), hix + (hi1 * w1 + hi2 * w2)], axis=1)
    o_ref[...] = _rms(x_out, g_ref[...])


def _combine(yg, route, xmid, g, chunk, t_total, out_prev):
    t = xmid.shape[0]
    n_tiles = t // TM_CMB
    first = chunk * n_tiles
    in_specs = [
        pl.BlockSpec((TM_CMB, PACKED), lambda i: (i, 0)),
        pl.BlockSpec((TM_CMB, PACKED), lambda i: (n_tiles + i, 0)),
        pl.BlockSpec((TM_CMB, LANES), lambda i: (i, 0)),
        pl.BlockSpec((TM_CMB, PACKED), lambda i: (i, 0)),
        pl.BlockSpec((1, D_MODEL), lambda i: (0, 0)),
    ]
    args = [yg, yg, route, xmid, g]
    aliases = {}
    if out_prev is not None:
        in_specs.append(pl.BlockSpec(memory_space=pl.ANY))
        aliases = {len(args): 0}
        args.append(out_prev)
    return pl.pallas_call(
        _combine_kernel,
        name="combine",
        grid=(n_tiles,),
        in_specs=in_specs,
        out_specs=pl.BlockSpec((TM_CMB, D_MODEL), lambda i: (first + i, 0)),
        out_shape=jax.ShapeDtypeStruct((t_total, D_MODEL), F32),
        input_output_aliases=aliases,
        compiler_params=pltpu.CompilerParams(
            dimension_semantics=("arbitrary",), vmem_limit_bytes=VMEM_LIMIT),
    )(*args)


def _split_bf16(w):
    hi = w.astype(BF16)
    lo = (w - hi.astype(F32)).astype(BF16)
    return hi, lo


def kernel(x, norm_mix, w_in, conv_w, conv_b, w_a_out, sinks, w_b_out, w_o, norm_ffn, w_group,
           b_group, w_expert, b_expert, w_gate, w_up, w_down, norm_final):
    bsz, seq, d = x.shape
    t = bsz * seq
    assert d == D_MODEL and seq % TM_PROJ == 0 and seq % TQ_ATTN == 0
    xf = x.reshape(t, d)
    row = lambda v: v.reshape(1, -1)

    w_b, w_c, w_u, w_q, w_k, w_v, w_ga, w_gb = jnp.split(w_in, REF_SPLITS, axis=1)
    w_cu = jnp.concatenate([w_c, w_u], axis=1).astype(BF16)
    w_bg = jnp.concatenate([w_b, w_ga], axis=1).astype(BF16)
    w_rest = jnp.concatenate([w_gb, w_q, w_k, w_v], axis=1).astype(BF16)

    za, proj, w_gate, w_up, w_down = _inproj(xf, row(norm_mix), w_cu, w_bg, w_rest, conv_w,
                                             row(conv_b), w_a_out.astype(BF16), seq,
                                             w_gate, w_up, w_down)
    attn = _attention(proj, sinks, seq)

    pad = LANES - N_GROUPS - N_EXPERTS
    w_r = jnp.concatenate([w_group, w_expert, jnp.zeros((d, pad), F32)], axis=1)
    b_r = jnp.concatenate([b_group, b_expert, jnp.zeros((pad,), F32)]).reshape(1, LANES)
    wr = jnp.concatenate(_split_bf16(w_r), axis=1)
    wb = w_b_out.astype(BF16)
    wo = w_o.astype(BF16)

    t_chunk = t // MOE_CHUNKS
    out = None
    for chunk in range(MOE_CHUNKS):
        xmid, h2, route, cnt = _mix(attn, proj, za, xf, wb, wo, row(norm_ffn), wr, b_r, chunk, t_chunk)
        out = _moe_chunk(xmid, h2, route, cnt, w_gate, w_up, w_down, row(norm_final), chunk, t, out)
    return out.reshape(bsz, seq, d)


def _moe_chunk(xmid, h2, route, cnt, w_gate, w_up, w_down, g_final, chunk, t_total, out_prev):
    t = xmid.shape[0]
    n_tiles = t // TM_CNT
    cnt = cnt.reshape(n_tiles, 8, LANES)[:, 0, :N_EXPERTS].astype(I32)
    totals = jnp.sum(cnt, axis=0)
    tiles_e = (totals + TM_EXP - 1) // TM_EXP
    tile_end = jnp.cumsum(tiles_e)
    offset = (tile_end - tiles_e) * TM_EXP
    base = offset[None, :] + jnp.cumsum(cnt, axis=0) - cnt
    base = jnp.pad(base, ((0, 0), (0, LANES - N_EXPERTS))).astype(F32).reshape(n_tiles, 1, LANES)
    rows = t * TOP_K + N_EXPERTS * TM_EXP
    n_active = tile_end[-1]
    tile_id = jnp.arange(rows // TM_EXP, dtype=I32)
    tile_src = jnp.minimum(tile_id, n_active - 1)
    tile_expert = jnp.sum((tile_src[:, None] >= tile_end[None, :]).astype(I32), axis=1)
    tile_expert = jnp.minimum(tile_expert, N_EXPERTS - 1)
    row_in_expert = (tile_id - (tile_end - tiles_e)[tile_expert]) * TM_EXP
    tile_valid = jnp.clip(totals[tile_expert] - row_in_expert, 0, TM_EXP)
    tile_valid = jnp.where(tile_id < n_active, tile_valid, 0).astype(I32)
    after = tile_end[tile_expert]
    next_expert = jnp.where(after < n_active, tile_expert[jnp.minimum(after, n_active - 1)], -1)
    first_of_expert = jnp.concatenate(
        [jnp.ones((1,), I32), (tile_expert[1:] != tile_expert[:-1]).astype(I32)])
    tile_slot = (jnp.cumsum(first_of_expert) - 1) % 2

    pos = _positions(route, base)
    xs = _dispatch(pos, h2, rows)
    ys = _experts(tile_expert.astype(I32), next_expert.astype(I32), tile_slot.astype(I32),
                  tile_src.astype(I32), tile_valid, xs, w_gate, w_up, w_down)
    yg = _gather_rows(ys, pos.reshape(TOP_K * t))
    return _combine(yg, route, xmid, g_final, chunk, t_total, out_prev)
```

```python
import functools
import math

import jax
import jax.numpy as jnp
from jax import lax
from jax.experimental import pallas as pl
from jax.experimental.pallas import tpu as pltpu
from jax.experimental.pallas import tpu_sc as plsc

F32 = jnp.float32
BF16 = jnp.bfloat16
I32 = jnp.int32

D_MODEL = 1024
HEAD_DIM = 64
N_HEADS = 16
N_KV_HEADS = 4
GROUP = N_HEADS // N_KV_HEADS
KV_WIDTH = N_KV_HEADS * HEAD_DIM
WINDOW = 128
N_GROUPS = 4
EXPERTS_PER_GROUP = 8
N_EXPERTS = N_GROUPS * EXPERTS_PER_GROUP
TOP_K = 2
D_FF = 512
EPS = 1e-6
LANES = 128

REF_SPLITS = (1024, 2048, 3072, 4096, 4352, 4608, 5632)
REST_COLS = 2 * D_MODEL + 2 * KV_WIDTH
COL_GB, COL_Q = 0, 1
COL_K, COL_V = 2 * D_MODEL // KV_WIDTH, 2 * D_MODEL // KV_WIDTH + 1

TM_PROJ = 512
TQ_ATTN = 512
TM_MIX = 1024
SUB_MIX = 256
TM_CNT = 512
POS_TILES = 4
MOE_CHUNKS = 2
TM_EXP = 512
SUB_EXP = 256
TM_CMB = 1024
HALO_ROWS = 8
VMEM_LIMIT = 56 * 1024 * 1024
PACKED = D_MODEL // 2

SC_CORES = 2
SC_SUBCORES = 16
SC_WORKERS = SC_CORES * SC_SUBCORES
SC_CHUNK = 64


def _rms(x, g):
    r = lax.rsqrt(jnp.mean(x * x, axis=-1, keepdims=True) + EPS)
    return (x * r) * g


def _pack_bf16_pairs(x):
    n = x.shape[1] // 2
    lo = lax.bitcast_convert_type(x[:, :n].astype(BF16).astype(F32), I32)
    hi = lax.bitcast_convert_type(x[:, n:].astype(BF16).astype(F32), I32)
    return (hi & jnp.int32(-65536)) | lax.shift_right_logical(lo, 16)


def _unpack_bf16_pairs(p):
    lo = lax.bitcast_convert_type(lax.shift_left(p, 16), F32)
    hi = lax.bitcast_convert_type(p & jnp.int32(-65536), F32)
    return lo, hi


def _inproj_kernel(x_ref, g_ref, wcu_ref, wbg_ref, wrest_ref, cw_ref, cb_ref, wa_ref,
                   eg_ref, eu_ref, ed_ref,
                   za_ref, proj_ref, eg_out, eu_out, ed_out, halo_ref, *, tiles_per_seq):
    i = pl.program_id(0)
    eg_out[...] = eg_ref[...].astype(BF16)
    eu_out[...] = eu_ref[...].astype(BF16)
    ed_out[...] = ed_ref[...].astype(BF16)
    h = _rms(x_ref[...], g_ref[...]).astype(BF16)
    pcu = jnp.dot(h, wcu_ref[...], preferred_element_type=F32)
    pbg = jnp.dot(h, wbg_ref[...], preferred_element_type=F32)
    proj_ref[...] = jnp.dot(h, wrest_ref[...], preferred_element_type=F32).astype(BF16)
    cu = pcu[:, :D_MODEL] * pcu[:, D_MODEL:]
    first = (i % tiles_per_seq) == 0
    hist = jnp.where(first, 0.0, halo_ref[...])
    prev1 = hist[HALO_ROWS - 1:HALO_ROWS]
    prev2 = hist[HALO_ROWS - 2:HALO_ROWS - 1]
    halo_ref[...] = cu[TM_PROJ - HALO_ROWS:, :]
    row = lax.broadcasted_iota(I32, cu.shape, 0)
    cu1 = jnp.where(row == 0, prev1, pltpu.roll(cu, 1, 0))
    cu2 = jnp.where(row == 0, prev2, jnp.where(row == 1, prev1, pltpu.roll(cu, 2, 0)))
    cw = cw_ref[...]
    y = cw[0:1] * cu2 + cw[1:2] * cu1 + cw[2:3] * cu + cb_ref[...]
    ya = (pbg[:, :D_MODEL] * y).astype(BF16)
    z = jnp.dot(ya, wa_ref[...], preferred_element_type=F32)
    za_ref[...] = (jax.nn.sigmoid(pbg[:, D_MODEL:]) * z).astype(BF16)


def _inproj(xf, g, w_cu, w_bg, w_rest, conv_w, conv_b, wa, seq, w_gate, w_up, w_down):
    t = xf.shape[0]
    n_tiles = t // TM_PROJ
    const = lambda shape: pl.BlockSpec(shape, lambda i: (0, 0), pipeline_mode=pl.Buffered(1))
    slabs = [w.reshape(-1, w.shape[-1]) for w in (w_gate, w_up, w_down)]
    slab_rows = [s.shape[0] // n_tiles for s in slabs]
    assert all(s.shape[0] == r * n_tiles and r % 16 == 0 for s, r in zip(slabs, slab_rows))
    slab_specs = [pl.BlockSpec((r, s.shape[1]), lambda i: (i, 0)) for s, r in zip(slabs, slab_rows)]
    outs = pl.pallas_call(
        functools.partial(_inproj_kernel, tiles_per_seq=seq // TM_PROJ),
        name="inproj_conv",
        grid=(n_tiles,),
        in_specs=[
            pl.BlockSpec((TM_PROJ, D_MODEL), lambda i: (i, 0)),
            const((1, D_MODEL)),
            const((D_MODEL, 2 * D_MODEL)), const((D_MODEL, 2 * D_MODEL)), const((D_MODEL, REST_COLS)),
            const((3, D_MODEL)), const((1, D_MODEL)), const((D_MODEL, D_MODEL)),
        ] + slab_specs,
        out_specs=[pl.BlockSpec((TM_PROJ, D_MODEL), lambda i: (i, 0)),
                   pl.BlockSpec((TM_PROJ, REST_COLS), lambda i: (i, 0))] + slab_specs,
        out_shape=[jax.ShapeDtypeStruct((t, D_MODEL), BF16),
                   jax.ShapeDtypeStruct((t, REST_COLS), BF16)]
                  + [jax.ShapeDtypeStruct(s.shape, BF16) for s in slabs],
        scratch_shapes=[pltpu.VMEM((HALO_ROWS, D_MODEL), F32)],
        compiler_params=pltpu.CompilerParams(
            dimension_semantics=("arbitrary",), vmem_limit_bytes=VMEM_LIMIT),
    )(xf, g, w_cu, w_bg, w_rest, conv_w, conv_b, wa, *slabs)
    za, proj, eg, eu, ed = outs
    return za, proj, eg.reshape(w_gate.shape), eu.reshape(w_up.shape), ed.reshape(w_down.shape)


def _attn_kernel(sink_ref, q_ref, k_ref, v_ref, kp_ref, vp_ref, gb_ref, wb_ref, o_ref,
                 *, tiles_per_seq):
    first_tile = (pl.program_id(0) % tiles_per_seq) == 0
    ks = lax.broadcasted_iota(I32, (WINDOW, WINDOW), 0)
    qq = lax.broadcasted_iota(I32, (WINDOW, WINDOW), 1)
    own = ks <= qq
    dist = jnp.where(own, qq - ks, qq - ks + WINDOW).astype(F32)
    visible0 = jnp.logical_or(own, jnp.logical_not(first_tile))
    log2e = math.log2(math.e)
    c_scale = log2e / math.sqrt(HEAD_DIM)
    nt = (((1,), (1,)), ((), ()))
    zk = jnp.zeros((2 * WINDOW, HEAD_DIM), BF16)

    def transposed(v_blk):
        return jnp.transpose(v_blk.astype(F32)).astype(BF16)

    def project(rows_p, attn_blk):
        yb = jnp.dot(attn_blk, wb_ref[...], preferred_element_type=F32)
        o_ref[rows_p, :] = (jax.nn.sigmoid(gb_ref[rows_p, :].astype(F32)) * yb).astype(BF16)

    pending = None
    prev_k = kp_ref[...]
    prev_vt = transposed(vp_ref[...])
    for sb in range(TQ_ATTN // WINDOW):
        rows = slice(sb * WINDOW, (sb + 1) * WINDOW)
        cur_k = k_ref[rows, :]
        cur_vt = transposed(v_ref[rows, :])
        scores, vcats = [], []
        for kh in range(N_KV_HEADS):
            cols = slice(kh * HEAD_DIM, (kh + 1) * HEAD_DIM)
            kcat = jnp.concatenate([prev_k[:, cols], cur_k[:, cols]], axis=0)
            vcats.append(jnp.concatenate([prev_vt[cols, :], cur_vt[cols, :]], axis=1))
            qg = jnp.concatenate([q_ref[rows, (2 * kh) * LANES:(2 * kh + 1) * LANES],
                                  q_ref[rows, (2 * kh + 1) * LANES:(2 * kh + 2) * LANES]], axis=0)
            k_pad = jnp.concatenate([jnp.concatenate([kcat, zk], axis=1),
                                     jnp.concatenate([zk, kcat], axis=1)], axis=0)
            scores.append(lax.dot_general(k_pad, qg, nt, preferred_element_type=F32))
        if pending is not None:
            project(*pending)
        probs, rdens = [], []
        for kh in range(N_KV_HEADS):
            for pos in range(2):
                pr, rd = [], []
                for half in range(2):
                    h = kh * GROUP + 2 * half + pos
                    slope = 2.0 ** (-8.0 * (h + 1) / N_HEADS)
                    qcols = slice(half * WINDOW, (half + 1) * WINDOW)
                    krow = pos * 2 * WINDOW
                    st = scores[kh]
                    s = (jnp.where(own, st[krow + WINDOW:krow + 2 * WINDOW, qcols],
                                   st[krow:krow + WINDOW, qcols]) * c_scale
                         - (slope * log2e) * dist)
                    if sb == 0:
                        s = jnp.where(visible0, s, -jnp.inf)
                    m = jnp.max(s, axis=0, keepdims=True)
                    p = jnp.exp2(s - m)
                    den = jnp.sum(p, axis=0, keepdims=True) + jnp.exp2(sink_ref[h] * log2e - m)
                    rd.append(1.0 / den)
                    pr.append(jnp.concatenate(
                        [jnp.where(own, 0.0, p).astype(BF16), jnp.where(own, p, 0.0).astype(BF16)],
                        axis=0))
                probs.append(jnp.concatenate(pr, axis=1))
                rdens.append(jnp.concatenate(rd, axis=1))
        out_t = [None] * N_HEADS
        for kh in range(N_KV_HEADS):
            for pos in range(2):
                o2 = jnp.dot(vcats[kh], probs[2 * kh + pos], preferred_element_type=F32)
                o2 = o2 * rdens[2 * kh + pos]
                out_t[kh * GROUP + pos] = o2[:, :WINDOW]
                out_t[kh * GROUP + 2 + pos] = o2[:, WINDOW:]
        pending = (rows, jnp.transpose(jnp.concatenate(out_t, axis=0)).astype(BF16))
        prev_k, prev_vt = cur_k, cur_vt
    project(*pending)


def _attention(proj, sinks, wb, seq):
    t = proj.shape[0]
    sub = TQ_ATTN // WINDOW
    return pl.pallas_call(
        functools.partial(_attn_kernel, tiles_per_seq=seq // TQ_ATTN),
        name="swattn",
        grid=(t // TQ_ATTN,),
        in_specs=[
            pl.BlockSpec(memory_space=pltpu.SMEM),
            pl.BlockSpec((TQ_ATTN, D_MODEL), lambda i: (i, COL_Q)),
            pl.BlockSpec((TQ_ATTN, KV_WIDTH), lambda i: (i, COL_K)),
            pl.BlockSpec((TQ_ATTN, KV_WIDTH), lambda i: (i, COL_V)),
            pl.BlockSpec((WINDOW, KV_WIDTH), lambda i: (jnp.maximum(i * sub - 1, 0), COL_K)),
            pl.BlockSpec((WINDOW, KV_WIDTH), lambda i: (jnp.maximum(i * sub - 1, 0), COL_V)),
            pl.BlockSpec((TQ_ATTN, D_MODEL), lambda i: (i, COL_GB)),
            pl.BlockSpec((D_MODEL, D_MODEL), lambda i: (0, 0), pipeline_mode=pl.Buffered(1)),
        ],
        out_specs=pl.BlockSpec((TQ_ATTN, D_MODEL), lambda i: (i, 0)),
        out_shape=jax.ShapeDtypeStruct((t, D_MODEL), BF16),
        compiler_params=pltpu.CompilerParams(
            dimension_semantics=("arbitrary",), vmem_limit_bytes=VMEM_LIMIT),
    )(sinks, proj, proj, proj, proj, proj, proj, wb)


def _mix_kernel(zb_ref, za_ref, x_ref, wo_ref, g_ref, wr_ref, br_ref,
                xmid_ref, h_ref, route_ref, cnt_ref):
    subs = [slice(s * SUB_MIX, (s + 1) * SUB_MIX) for s in range(TM_MIX // SUB_MIX)]
    merged = [(za_ref[r, :].astype(F32) + zb_ref[r, :].astype(F32)).astype(BF16) for r in subs]
    xm = [x_ref[r, :] + jnp.dot(m, wo_ref[...], preferred_element_type=F32)
          for r, m in zip(subs, merged)]
    hs = []
    for r, v in zip(subs, xm):
        xmid_ref[r, :] = _pack_bf16_pairs(v)
        h = _rms(v, g_ref[...])
        h_ref[r, :] = _pack_bf16_pairs(h)
        hs.append(h)
    wr = wr_ref[...]
    logits = []
    for h in hs:
        h_hi = h.astype(BF16)
        h_lo = (h - h_hi.astype(F32)).astype(BF16)
        both = jnp.dot(h_hi, wr, preferred_element_type=F32)
        logits.append(both[:, :LANES] + both[:, LANES:]
                      + jnp.dot(h_lo, wr[:, :LANES], preferred_element_type=F32) + br_ref[...])
    subs_per_cnt = TM_CNT // SUB_MIX
    cnts = [jnp.zeros((1, LANES), F32) for _ in range(TM_MIX // TM_CNT)]
    for s, (r, lg) in enumerate(zip(subs, logits)):
        route, onehot = _route(lg)
        route_ref[r, :] = route
        cnts[s // subs_per_cnt] = cnts[s // subs_per_cnt] + jnp.sum(onehot, axis=0, keepdims=True)
    for c, cnt in enumerate(cnts):
        cnt_ref[c * 8:(c + 1) * 8, :] = jnp.broadcast_to(cnt, (8, LANES))


def _route(logits):
    lane = lax.broadcasted_iota(I32, logits.shape, 1)
    neg = -jnp.inf
    gl = jnp.where(lane < N_GROUPS, logits, neg)
    gmax = jnp.max(gl, axis=-1, keepdims=True)
    g_idx = jnp.min(jnp.where(gl == gmax, lane, LANES), axis=-1, keepdims=True)
    p_g = 1.0 / jnp.sum(jnp.exp(gl - gmax), axis=-1, keepdims=True)
    start = N_GROUPS + EXPERTS_PER_GROUP * g_idx
    el = jnp.where((lane >= start) & (lane < start + EXPERTS_PER_GROUP), logits, neg)
    v1 = jnp.max(el, axis=-1, keepdims=True)
    i1 = jnp.min(jnp.where(el == v1, lane, LANES), axis=-1, keepdims=True)
    el2 = jnp.where(lane == i1, neg, el)
    v2 = jnp.max(el2, axis=-1, keepdims=True)
    i2 = jnp.min(jnp.where(el2 == v2, lane, LANES), axis=-1, keepdims=True)
    e21 = jnp.exp(v2 - v1)
    w1 = p_g / (1.0 + e21)
    w2 = p_g * e21 / (1.0 + e21)
    e1 = i1 - N_GROUPS
    e2 = i2 - N_GROUPS
    route = jnp.where(lane == 0, e1.astype(F32),
                      jnp.where(lane == 1, e2.astype(F32),
                                jnp.where(lane == 2, w1, jnp.where(lane == 3, w2, 0.0))))
    onehot = ((lane == e1) | (lane == e2)).astype(F32)
    return route, onehot


def _mix(zb, za, xf, wo, g, wr, br, chunk, t):
    n_tiles = t // TM_MIX
    first = chunk * n_tiles
    cnt_rows = TM_MIX // TM_CNT * 8
    full = lambda shape: pl.BlockSpec(shape, lambda i: (0, 0), pipeline_mode=pl.Buffered(1))
    tile = lambda w=D_MODEL: pl.BlockSpec((TM_MIX, w), lambda i: (i, 0))
    src = lambda c=0: pl.BlockSpec((TM_MIX, D_MODEL), lambda i: (first + i, c))
    return pl.pallas_call(
        _mix_kernel,
        name="merge_router",
        grid=(n_tiles,),
        in_specs=[
            src(), src(), src(),
            full((D_MODEL, D_MODEL)), full((1, D_MODEL)),
            full((D_MODEL, 2 * LANES)), full((1, LANES)),
        ],
        out_specs=[tile(PACKED), tile(PACKED), tile(LANES),
                   pl.BlockSpec((cnt_rows, LANES), lambda i: (i, 0))],
        out_shape=[
            jax.ShapeDtypeStruct((t, PACKED), I32),
            jax.ShapeDtypeStruct((t, PACKED), I32),
            jax.ShapeDtypeStruct((t, LANES), F32),
            jax.ShapeDtypeStruct((n_tiles * cnt_rows, LANES), F32),
        ],
        compiler_params=pltpu.CompilerParams(
            dimension_semantics=("arbitrary",), vmem_limit_bytes=VMEM_LIMIT),
    )(zb, za, xf, wo, g, wr, br)


def _pos_kernel(route_ref, base_ref, pos_ref):
    lane = lax.broadcasted_iota(I32, (TM_CNT, LANES), 1)
    r = lax.broadcasted_iota(I32, (TM_CNT, TM_CNT), 0)
    c = lax.broadcasted_iota(I32, (TM_CNT, TM_CNT), 1)
    lower = (c < r).astype(BF16)
    subs = [slice(s * TM_CNT, (s + 1) * TM_CNT) for s in range(POS_TILES)]
    routes = [route_ref[rs, :] for rs in subs]
    e1 = [jnp.sum(jnp.where(lane == 0, rt, 0.0), axis=-1, keepdims=True).astype(I32) for rt in routes]
    e2 = [jnp.sum(jnp.where(lane == 1, rt, 0.0), axis=-1, keepdims=True).astype(I32) for rt in routes]
    onehot = [((lane == a) | (lane == b)).astype(BF16) for a, b in zip(e1, e2)]
    before = [jnp.dot(lower, oh, preferred_element_type=F32) + base_ref[s]
              for s, oh in enumerate(onehot)]
    for s, rs in enumerate(subs):
        p1 = jnp.sum(jnp.where(lane == e1[s], before[s], 0.0), axis=-1, keepdims=True)
        p2 = jnp.sum(jnp.where(lane == e2[s], before[s], 0.0), axis=-1, keepdims=True)
        packed = jnp.where(lane == 0, p1, jnp.where(lane == 1, p2, 0.0))
        pos_ref[:, rs] = jnp.transpose(packed)[0:TOP_K, :].astype(I32)


def _positions(route, base):
    t = route.shape[0]
    n_steps = t // (TM_CNT * POS_TILES)
    return pl.pallas_call(
        _pos_kernel,
        name="positions",
        grid=(n_steps,),
        in_specs=[
            pl.BlockSpec((TM_CNT * POS_TILES, LANES), lambda i: (i, 0)),
            pl.BlockSpec((POS_TILES, 1, LANES), lambda i: (i, 0, 0)),
        ],
        out_specs=pl.BlockSpec((TOP_K, TM_CNT * POS_TILES), lambda i: (0, i)),
        out_shape=jax.ShapeDtypeStruct((TOP_K, t), I32),
        compiler_params=pltpu.CompilerParams(dimension_semantics=("arbitrary",)),
    )(route, base)


def _sc_mesh():
    return plsc.VectorSubcoreMesh(core_axis_name="c", subcore_axis_name="s",
                                  num_cores=SC_CORES, num_subcores=SC_SUBCORES)


def _sc_worker():
    return lax.axis_index("s") * SC_CORES + lax.axis_index("c")


def _dispatch(pos, hp, rows):
    t = hp.shape[0]
    per_w = t // SC_WORKERS
    n_ch = per_w // SC_CHUNK
    pos4 = pos.reshape(TOP_K, SC_WORKERS, n_ch, SC_CHUNK)

    @functools.partial(
        pl.kernel, mesh=_sc_mesh(),
        out_type=jax.ShapeDtypeStruct((rows, PACKED), I32),
        scratch_types=[pltpu.VMEM((TOP_K, n_ch, SC_CHUNK), I32),
                       pltpu.VMEM((SC_CHUNK, PACKED), I32)])
    def scatter(hp_hbm, pos_hbm, xs_hbm, idx_v, rows_v):
        wid = _sc_worker()
        for k in range(TOP_K):
            pltpu.sync_copy(pos_hbm.at[k, wid], idx_v.at[k])

        def body(c, carry):
            start = pl.multiple_of(wid * per_w + c * SC_CHUNK, SC_CHUNK)
            pltpu.sync_copy(hp_hbm.at[pl.ds(start, SC_CHUNK)], rows_v)
            for k in range(TOP_K):
                pltpu.sync_copy(rows_v, xs_hbm.at[idx_v.at[k, c]])
            return carry

        lax.fori_loop(0, n_ch, body, 0)

    return scatter(hp, pos4)


def _gather_rows(table, idx):
    n = idx.shape[0]
    per_w = n // SC_WORKERS
    n_ch = per_w // SC_CHUNK
    idx3 = idx.reshape(SC_WORKERS, n_ch, SC_CHUNK)

    @functools.partial(
        pl.kernel, mesh=_sc_mesh(),
        out_type=jax.ShapeDtypeStruct((n, PACKED), I32),
        scratch_types=[pltpu.VMEM((n_ch, SC_CHUNK), I32),
                       pltpu.VMEM((SC_CHUNK, PACKED), I32)])
    def gather(table_hbm, idx_hbm, out_hbm, idx_v, rows_v):
        wid = _sc_worker()
        pltpu.sync_copy(idx_hbm.at[wid], idx_v)

        def body(c, carry):
            start = pl.multiple_of(wid * per_w + c * SC_CHUNK, SC_CHUNK)
            pltpu.sync_copy(table_hbm.at[idx_v.at[c]], rows_v)
            pltpu.sync_copy(rows_v, out_hbm.at[pl.ds(start, SC_CHUNK)])
            return carry

        lax.fori_loop(0, n_ch, body, 0)

    return gather(table, idx3)


def _expert_kernel(te_ref, nx_ref, sl_ref, ts_ref, tv_ref, xs_ref, wg_hbm, wu_hbm, wd_hbm, o_ref,
                   wg_s, wu_s, wd_s, sem):
    del ts_ref
    i = pl.program_id(0)
    n_valid = tv_ref[i]
    expert = te_ref[i]
    slot = sl_ref[i]

    def weight_copies(e, s):
        return (pltpu.make_async_copy(wg_hbm.at[e], wg_s.at[s], sem.at[s, 0]),
                pltpu.make_async_copy(wu_hbm.at[e], wu_s.at[s], sem.at[s, 1]),
                pltpu.make_async_copy(wd_hbm.at[e], wd_s.at[s], sem.at[s, 2]))

    @pl.when(i == 0)
    def _():
        for copy in weight_copies(expert, slot):
            copy.start()

    @pl.when(jnp.logical_or(i == 0, expert != te_ref[jnp.maximum(i - 1, 0)]))
    def _():
        for copy in weight_copies(expert, slot):
            copy.wait()
        nxt = nx_ref[i]

        @pl.when(nxt >= 0)
        def _():
            for copy in weight_copies(nxt, 1 - slot):
                copy.start()

    def mlp(n_sub):
        wg, wu, wd = wg_s.at[slot], wu_s.at[slot], wd_s.at[slot]
        subs = [slice(s * SUB_EXP, (s + 1) * SUB_EXP) for s in range(n_sub)]
        xin = []
        for r in subs:
            rid = r.start + lax.broadcasted_iota(I32, (SUB_EXP, PACKED), 0)
            lo, hi = _unpack_bf16_pairs(jnp.where(rid < n_valid, xs_ref[r, :], 0))
            xin.append((lo.astype(BF16), hi.astype(BF16)))
        ab = [(jnp.dot(lo, wg[:PACKED, :], preferred_element_type=F32)
               + jnp.dot(hi, wg[PACKED:, :], preferred_element_type=F32),
               jnp.dot(lo, wu[:PACKED, :], preferred_element_type=F32)
               + jnp.dot(hi, wu[PACKED:, :], preferred_element_type=F32)) for lo, hi in xin]
        for r, (ai, bi) in zip(subs, ab):
            hm = (ai * jax.nn.sigmoid(ai) * bi).astype(BF16)
            o_ref[r, :] = _pack_bf16_pairs(jnp.dot(hm, wd[...], preferred_element_type=F32))
        if n_sub * SUB_EXP < TM_EXP:
            o_ref[n_sub * SUB_EXP:, :] = jnp.zeros((TM_EXP - n_sub * SUB_EXP, PACKED), I32)

    n_subs = TM_EXP // SUB_EXP
    for n_sub in range(n_subs + 1):
        lo_rows = (n_sub - 1) * SUB_EXP if n_sub else -1
        in_range = jnp.logical_and(n_valid > lo_rows, n_valid <= n_sub * SUB_EXP)
        pl.when(in_range)(functools.partial(mlp, n_sub))


def _experts(tile_expert, next_expert, tile_slot, tile_src, tile_valid, xs, wg, wu, wd):
    rows = xs.shape[0]
    hbm = pl.BlockSpec(memory_space=pl.ANY)
    grid_spec = pltpu.PrefetchScalarGridSpec(
        num_scalar_prefetch=5,
        grid=(rows // TM_EXP,),
        in_specs=[pl.BlockSpec((TM_EXP, PACKED), lambda i, te, nx, sl, ts, tv: (ts[i], 0)),
                  hbm, hbm, hbm],
        out_specs=pl.BlockSpec((TM_EXP, PACKED), lambda i, te, nx, sl, ts, tv: (i, 0)),
        scratch_shapes=[
            pltpu.VMEM((2, D_MODEL, D_FF), BF16), pltpu.VMEM((2, D_MODEL, D_FF), BF16),
            pltpu.VMEM((2, D_FF, D_MODEL), BF16),
            pltpu.SemaphoreType.DMA((2, 3)),
        ],
    )
    return pl.pallas_call(
        _expert_kernel,
        name="experts",
        grid_spec=grid_spec,
        out_shape=jax.ShapeDtypeStruct((rows, PACKED), I32),
        compiler_params=pltpu.CompilerParams(
            dimension_semantics=("arbitrary",), vmem_limit_bytes=VMEM_LIMIT),
    )(tile_expert, next_expert, tile_slot, tile_src, tile_valid, xs, wg, wu, wd)


def _combine_kernel(y1_ref, y2_ref, route_ref, xmid_ref, g_ref, *rest):
    o_ref = rest[-1]
    route = route_ref[...]
    lane = lax.broadcasted_iota(I32, route.shape, 1)
    w1 = jnp.sum(jnp.where(lane == 2, route, 0.0), axis=-1, keepdims=True)
    w2 = jnp.sum(jnp.where(lane == 3, route, 0.0), axis=-1, keepdims=True)
    lo1, hi1 = _unpack_bf16_pairs(y1_ref[...])
    lo2, hi2 = _unpack_bf16_pairs(y2_ref[...])
    lox, hix = _unpack_bf16_pairs(xmid_ref[...])
    x_out = jnp.concatenate([lox + (lo1 * w1 + lo2 * w2), hix + (hi1 * w1 + hi2 * w2)], axis=1)
    o_ref[...] = _rms(x_out, g_ref[...])


def _combine(yg, route, xmid, g, chunk, t_total, out_prev):
    t = xmid.shape[0]
    n_tiles = t // TM_CMB
    first = chunk * n_tiles
    in_specs = [
        pl.BlockSpec((TM_CMB, PACKED), lambda i: (i, 0)),
        pl.BlockSpec((TM_CMB, PACKED), lambda i: (n_tiles + i, 0)),
        pl.BlockSpec((TM_CMB, LANES), lambda i: (i, 0)),
        pl.BlockSpec((TM_CMB, PACKED), lambda i: (i, 0)),
        pl.BlockSpec((1, D_MODEL), lambda i: (0, 0)),
    ]
    args = [yg, yg, route, xmid, g]
    aliases = {}
    if out_prev is not None:
        in_specs.append(pl.BlockSpec(memory_space=pl.ANY))
        aliases = {len(args): 0}
        args.append(out_prev)
    return pl.pallas_call(
        _combine_kernel,
        name="combine",
        grid=(n_tiles,),
        in_specs=in_specs,
        out_specs=pl.BlockSpec((TM_CMB, D_MODEL), lambda i: (first + i, 0)),
        out_shape=jax.ShapeDtypeStruct((t_total, D_MODEL), F32),
        input_output_aliases=aliases,
        compiler_params=pltpu.CompilerParams(
            dimension_semantics=("arbitrary",), vmem_limit_bytes=VMEM_LIMIT),
    )(*args)


def _split_bf16(w):
    hi = w.astype(BF16)
    lo = (w - hi.astype(F32)).astype(BF16)
    return hi, lo


def kernel(x, norm_mix, w_in, conv_w, conv_b, w_a_out, sinks, w_b_out, w_o, norm_ffn, w_group,
           b_group, w_expert, b_expert, w_gate, w_up, w_down, norm_final):
    bsz, seq, d = x.shape
    t = bsz * seq
    assert d == D_MODEL and seq % TM_PROJ == 0 and seq % TQ_ATTN == 0
    xf = x.reshape(t, d)
    row = lambda v: v.reshape(1, -1)

    w_b, w_c, w_u, w_q, w_k, w_v, w_ga, w_gb = jnp.split(w_in, REF_SPLITS, axis=1)
    w_cu = jnp.concatenate([w_c, w_u], axis=1).astype(BF16)
    w_bg = jnp.concatenate([w_b, w_ga], axis=1).astype(BF16)
    w_rest = jnp.concatenate([w_gb, w_q, w_k, w_v], axis=1).astype(BF16)

    za, proj, w_gate, w_up, w_down = _inproj(xf, row(norm_mix), w_cu, w_bg, w_rest, conv_w,
                                             row(conv_b), w_a_out.astype(BF16), seq,
                                             w_gate, w_up, w_down)
    zb = _attention(proj, sinks, w_b_out.astype(BF16), seq)

    pad = LANES - N_GROUPS - N_EXPERTS
    w_r = jnp.concatenate([w_group, w_expert, jnp.zeros((d, pad), F32)], axis=1)
    b_r = jnp.concatenate([b_group, b_expert, jnp.zeros((pad,), F32)]).reshape(1, LANES)
    wr = jnp.concatenate(_split_bf16(w_r), axis=1)
    wo = w_o.astype(BF16)

    t_chunk = t // MOE_CHUNKS
    out = None
    for chunk in range(MOE_CHUNKS):
        xmid, h2, route, cnt = _mix(zb, za, xf, wo, row(norm_ffn), wr, b_r, chunk, t_chunk)
        out = _moe_chunk(xmid, h2, route, cnt, w_gate, w_up, w_down, row(norm_final), chunk, t, out)
    return out.reshape(bsz, seq, d)


def _moe_chunk(xmid, h2, route, cnt, w_gate, w_up, w_down, g_final, chunk, t_total, out_prev):
    t = xmid.shape[0]
    n_tiles = t // TM_CNT
    cnt = cnt.reshape(n_tiles, 8, LANES)[:, 0, :N_EXPERTS].astype(I32)
    totals = jnp.sum(cnt, axis=0)
    tiles_e = (totals + TM_EXP - 1) // TM_EXP
    tile_end = jnp.cumsum(tiles_e)
    offset = (tile_end - tiles_e) * TM_EXP
    base = offset[None, :] + jnp.cumsum(cnt, axis=0) - cnt
    base = jnp.pad(base, ((0, 0), (0, LANES - N_EXPERTS))).astype(F32).reshape(n_tiles, 1, LANES)
    rows = t * TOP_K + N_EXPERTS * TM_EXP
    n_active = tile_end[-1]
    tile_id = jnp.arange(rows // TM_EXP, dtype=I32)
    tile_src = jnp.minimum(tile_id, n_active - 1)
    tile_expert = jnp.sum((tile_src[:, None] >= tile_end[None, :]).astype(I32), axis=1)
    tile_expert = jnp.minimum(tile_expert, N_EXPERTS - 1)
    row_in_expert = (tile_id - (tile_end - tiles_e)[tile_expert]) * TM_EXP
    tile_valid = jnp.clip(totals[tile_expert] - row_in_expert, 0, TM_EXP)
    tile_valid = jnp.where(tile_id < n_active, tile_valid, 0).astype(I32)
    after = tile_end[tile_expert]
    next_expert = jnp.where(after < n_active, tile_expert[jnp.minimum(after, n_active - 1)], -1)
    first_of_expert = jnp.concatenate(
        [jnp.ones((1,), I32), (tile_expert[1:] != tile_expert[:-1]).astype(I32)])
    tile_slot = (jnp.cumsum(first_of_expert) - 1) % 2

    pos = _positions(route, base)
    xs = _dispatch(pos, h2, rows)
    ys = _experts(tile_expert.astype(I32), next_expert.astype(I32), tile_slot.astype(I32),
                  tile_src.astype(I32), tile_valid, xs, w_gate, w_up, w_down)
    yg = _gather_rows(ys, pos.reshape(TOP_K * t))
    return _combine(yg, route, xmid, g_final, chunk, t_total, out_prev)
```

```python
import functools
import math

import jax
import jax.numpy as jnp
from jax import lax
from jax.experimental import pallas as pl
from jax.experimental.pallas import tpu as pltpu
from jax.experimental.pallas import tpu_sc as plsc

F32 = jnp.float32
BF16 = jnp.bfloat16
I32 = jnp.int32

D_MODEL = 1024
HEAD_DIM = 64
N_HEADS = 16
N_KV_HEADS = 4
GROUP = N_HEADS // N_KV_HEADS
KV_WIDTH = N_KV_HEADS * HEAD_DIM
WINDOW = 128
N_GROUPS = 4
EXPERTS_PER_GROUP = 8
N_EXPERTS = N_GROUPS * EXPERTS_PER_GROUP
TOP_K = 2
D_FF = 512
EPS = 1e-6
LANES = 128

REF_SPLITS = (1024, 2048, 3072, 4096, 4352, 4608, 5632)
REST_COLS = 2 * D_MODEL + 2 * KV_WIDTH
COL_GB, COL_Q = 0, 1
COL_K, COL_V = 2 * D_MODEL // KV_WIDTH, 2 * D_MODEL // KV_WIDTH + 1

TM_PROJ = 512
TQ_ATTN = 1024
TM_MIX = 1024
SUB_MIX = 256
TM_CNT = 512
POS_TILES = 8
MOE_CHUNKS = 2
TM_EXP = 512
SUB_EXP = 256
TM_CMB = 1024
HALO_ROWS = 8
VMEM_LIMIT = 56 * 1024 * 1024
PACKED = D_MODEL // 2

SC_CORES = 2
SC_SUBCORES = 16
SC_WORKERS = SC_CORES * SC_SUBCORES
SC_CHUNK = 64


def _rms(x, g):
    r = lax.rsqrt(jnp.mean(x * x, axis=-1, keepdims=True) + EPS)
    return (x * r) * g


def _pack_bf16_pairs(x):
    n = x.shape[1] // 2
    lo = lax.bitcast_convert_type(x[:, :n].astype(BF16).astype(F32), I32)
    hi = lax.bitcast_convert_type(x[:, n:].astype(BF16).astype(F32), I32)
    return (hi & jnp.int32(-65536)) | lax.shift_right_logical(lo, 16)


def _unpack_bf16_pairs(p):
    lo = lax.bitcast_convert_type(lax.shift_left(p, 16), F32)
    hi = lax.bitcast_convert_type(p & jnp.int32(-65536), F32)
    return lo, hi


def _inproj_kernel(x_ref, g_ref, wcu_ref, wbg_ref, wrest_ref, cw_ref, cb_ref, wa_ref,
                   eg_ref, eu_ref, ed_ref,
                   za_ref, proj_ref, eg_out, eu_out, ed_out, halo_ref, *, tiles_per_seq):
    i = pl.program_id(0)
    eg_out[...] = eg_ref[...].astype(BF16)
    eu_out[...] = eu_ref[...].astype(BF16)
    ed_out[...] = ed_ref[...].astype(BF16)
    h = _rms(x_ref[...], g_ref[...]).astype(BF16)
    pcu = jnp.dot(h, wcu_ref[...], preferred_element_type=F32)
    pbg = jnp.dot(h, wbg_ref[...], preferred_element_type=F32)
    proj_ref[...] = jnp.dot(h, wrest_ref[...], preferred_element_type=F32).astype(BF16)
    cu = pcu[:, :D_MODEL] * pcu[:, D_MODEL:]
    first = (i % tiles_per_seq) == 0
    hist = jnp.where(first, 0.0, halo_ref[...])
    prev1 = hist[HALO_ROWS - 1:HALO_ROWS]
    prev2 = hist[HALO_ROWS - 2:HALO_ROWS - 1]
    halo_ref[...] = cu[TM_PROJ - HALO_ROWS:, :]
    row = lax.broadcasted_iota(I32, cu.shape, 0)
    cu1 = jnp.where(row == 0, prev1, pltpu.roll(cu, 1, 0))
    cu2 = jnp.where(row == 0, prev2, jnp.where(row == 1, prev1, pltpu.roll(cu, 2, 0)))
    cw = cw_ref[...]
    y = cw[0:1] * cu2 + cw[1:2] * cu1 + cw[2:3] * cu + cb_ref[...]
    ya = (pbg[:, :D_MODEL] * y).astype(BF16)
    z = jnp.dot(ya, wa_ref[...], preferred_element_type=F32)
    za_ref[...] = (jax.nn.sigmoid(pbg[:, D_MODEL:]) * z).astype(BF16)


def _inproj(xf, g, w_cu, w_bg, w_rest, conv_w, conv_b, wa, seq, w_gate, w_up, w_down):
    t = xf.shape[0]
    n_tiles = t // TM_PROJ
    const = lambda shape: pl.BlockSpec(shape, lambda i: (0, 0), pipeline_mode=pl.Buffered(1))
    slabs = [w.reshape(-1, w.shape[-1]) for w in (w_gate, w_up, w_down)]
    slab_rows = [s.shape[0] // n_tiles for s in slabs]
    assert all(s.shape[0] == r * n_tiles and r % 16 == 0 for s, r in zip(slabs, slab_rows))
    slab_specs = [pl.BlockSpec((r, s.shape[1]), lambda i: (i, 0)) for s, r in zip(slabs, slab_rows)]
    outs = pl.pallas_call(
        functools.partial(_inproj_kernel, tiles_per_seq=seq // TM_PROJ),
        name="inproj_conv",
        grid=(n_tiles,),
        in_specs=[
            pl.BlockSpec((TM_PROJ, D_MODEL), lambda i: (i, 0)),
            const((1, D_MODEL)),
            const((D_MODEL, 2 * D_MODEL)), const((D_MODEL, 2 * D_MODEL)), const((D_MODEL, REST_COLS)),
            const((3, D_MODEL)), const((1, D_MODEL)), const((D_MODEL, D_MODEL)),
        ] + slab_specs,
        out_specs=[pl.BlockSpec((TM_PROJ, D_MODEL), lambda i: (i, 0)),
                   pl.BlockSpec((TM_PROJ, REST_COLS), lambda i: (i, 0))] + slab_specs,
        out_shape=[jax.ShapeDtypeStruct((t, D_MODEL), BF16),
                   jax.ShapeDtypeStruct((t, REST_COLS), BF16)]
                  + [jax.ShapeDtypeStruct(s.shape, BF16) for s in slabs],
        scratch_shapes=[pltpu.VMEM((HALO_ROWS, D_MODEL), F32)],
        compiler_params=pltpu.CompilerParams(
            dimension_semantics=("arbitrary",), vmem_limit_bytes=VMEM_LIMIT),
    )(xf, g, w_cu, w_bg, w_rest, conv_w, conv_b, wa, *slabs)
    za, proj, eg, eu, ed = outs
    return za, proj, eg.reshape(w_gate.shape), eu.reshape(w_up.shape), ed.reshape(w_down.shape)


def _attn_kernel(sink_ref, q_ref, k_ref, v_ref, kp_ref, vp_ref, gb_ref, wb_ref, o_ref,
                 *, tiles_per_seq):
    first_tile = (pl.program_id(0) % tiles_per_seq) == 0
    ks = lax.broadcasted_iota(I32, (WINDOW, WINDOW), 0)
    qq = lax.broadcasted_iota(I32, (WINDOW, WINDOW), 1)
    own = ks <= qq
    dist = jnp.where(own, qq - ks, qq - ks + WINDOW).astype(F32)
    visible0 = jnp.logical_or(own, jnp.logical_not(first_tile))
    log2e = math.log2(math.e)
    c_scale = log2e / math.sqrt(HEAD_DIM)
    nt = (((1,), (1,)), ((), ()))
    zk = jnp.zeros((2 * WINDOW, HEAD_DIM), BF16)

    def transposed(v_blk):
        return jnp.transpose(v_blk.astype(F32)).astype(BF16)

    def project(rows_p, attn_blk):
        yb = jnp.dot(attn_blk, wb_ref[...], preferred_element_type=F32)
        o_ref[rows_p, :] = (jax.nn.sigmoid(gb_ref[rows_p, :].astype(F32)) * yb).astype(BF16)

    pending = None
    prev_k = kp_ref[...]
    prev_vt = transposed(vp_ref[...])
    for sb in range(TQ_ATTN // WINDOW):
        rows = slice(sb * WINDOW, (sb + 1) * WINDOW)
        cur_k = k_ref[rows, :]
        cur_vt = transposed(v_ref[rows, :])
        scores, vcats = [], []
        for kh in range(N_KV_HEADS):
            cols = slice(kh * HEAD_DIM, (kh + 1) * HEAD_DIM)
            kcat = jnp.concatenate([prev_k[:, cols], cur_k[:, cols]], axis=0)
            vcats.append(jnp.concatenate([prev_vt[cols, :], cur_vt[cols, :]], axis=1))
            qg = jnp.concatenate([q_ref[rows, (2 * kh) * LANES:(2 * kh + 1) * LANES],
                                  q_ref[rows, (2 * kh + 1) * LANES:(2 * kh + 2) * LANES]], axis=0)
            k_pad = jnp.concatenate([jnp.concatenate([kcat, zk], axis=1),
                                     jnp.concatenate([zk, kcat], axis=1)], axis=0)
            scores.append(lax.dot_general(k_pad, qg, nt, preferred_element_type=F32))
        if pending is not None:
            project(*pending)
        probs, rdens = [], []
        for kh in range(N_KV_HEADS):
            for pos in range(2):
                pr, rd = [], []
                for half in range(2):
                    h = kh * GROUP + 2 * half + pos
                    slope = 2.0 ** (-8.0 * (h + 1) / N_HEADS)
                    qcols = slice(half * WINDOW, (half + 1) * WINDOW)
                    krow = pos * 2 * WINDOW
                    st = scores[kh]
                    s = (jnp.where(own, st[krow + WINDOW:krow + 2 * WINDOW, qcols],
                                   st[krow:krow + WINDOW, qcols]) * c_scale
                         - (slope * log2e) * dist)
                    if sb == 0:
                        s = jnp.where(visible0, s, -jnp.inf)
                    m = jnp.max(s, axis=0, keepdims=True)
                    p = jnp.exp2(s - m)
                    den = jnp.sum(p, axis=0, keepdims=True) + jnp.exp2(sink_ref[h] * log2e - m)
                    rd.append(1.0 / den)
                    pr.append(jnp.concatenate(
                        [jnp.where(own, 0.0, p).astype(BF16), jnp.where(own, p, 0.0).astype(BF16)],
                        axis=0))
                probs.append(jnp.concatenate(pr, axis=1))
                rdens.append(jnp.concatenate(rd, axis=1))
        out_t = [None] * N_HEADS
        for kh in range(N_KV_HEADS):
            for pos in range(2):
                o2 = jnp.dot(vcats[kh], probs[2 * kh + pos], preferred_element_type=F32)
                o2 = o2 * rdens[2 * kh + pos]
                out_t[kh * GROUP + pos] = o2[:, :WINDOW]
                out_t[kh * GROUP + 2 + pos] = o2[:, WINDOW:]
        pending = (rows, jnp.transpose(jnp.concatenate(out_t, axis=0)).astype(BF16))
        prev_k, prev_vt = cur_k, cur_vt
    project(*pending)


def _attention(proj, sinks, wb, seq):
    t = proj.shape[0]
    sub = TQ_ATTN // WINDOW
    return pl.pallas_call(
        functools.partial(_attn_kernel, tiles_per_seq=seq // TQ_ATTN),
        name="swattn",
        grid=(t // TQ_ATTN,),
        in_specs=[
            pl.BlockSpec(memory_space=pltpu.SMEM),
            pl.BlockSpec((TQ_ATTN, D_MODEL), lambda i: (i, COL_Q)),
            pl.BlockSpec((TQ_ATTN, KV_WIDTH), lambda i: (i, COL_K)),
            pl.BlockSpec((TQ_ATTN, KV_WIDTH), lambda i: (i, COL_V)),
            pl.BlockSpec((WINDOW, KV_WIDTH), lambda i: (jnp.maximum(i * sub - 1, 0), COL_K)),
            pl.BlockSpec((WINDOW, KV_WIDTH), lambda i: (jnp.maximum(i * sub - 1, 0), COL_V)),
            pl.BlockSpec((TQ_ATTN, D_MODEL), lambda i: (i, COL_GB)),
            pl.BlockSpec((D_MODEL, D_MODEL), lambda i: (0, 0), pipeline_mode=pl.Buffered(1)),
        ],
        out_specs=pl.BlockSpec((TQ_ATTN, D_MODEL), lambda i: (i, 0)),
        out_shape=jax.ShapeDtypeStruct((t, D_MODEL), BF16),
        compiler_params=pltpu.CompilerParams(
            dimension_semantics=("arbitrary",), vmem_limit_bytes=VMEM_LIMIT),
    )(sinks, proj, proj, proj, proj, proj, proj, wb)


def _mix_kernel(zb_ref, za_ref, x_ref, wo_ref, g_ref, wr_ref, br_ref,
                xmid_ref, h_ref, route_ref, cnt_ref):
    subs = [slice(s * SUB_MIX, (s + 1) * SUB_MIX) for s in range(TM_MIX // SUB_MIX)]
    merged = [za_ref[r, :] + zb_ref[r, :] for r in subs]
    xm = [x_ref[r, :] + jnp.dot(m, wo_ref[...], preferred_element_type=F32)
          for r, m in zip(subs, merged)]
    hs = []
    for r, v in zip(subs, xm):
        xmid_ref[r, :] = _pack_bf16_pairs(v)
        h = _rms(v, g_ref[...])
        h_ref[r, :] = _pack_bf16_pairs(h)
        hs.append(h)
    wr = wr_ref[...]
    logits = []
    for h in hs:
        h_hi = h.astype(BF16)
        h_lo = (h - h_hi.astype(F32)).astype(BF16)
        both = jnp.dot(h_hi, wr, preferred_element_type=F32)
        logits.append(both[:, :LANES] + both[:, LANES:]
                      + jnp.dot(h_lo, wr[:, :LANES], preferred_element_type=F32) + br_ref[...])
    subs_per_cnt = TM_CNT // SUB_MIX
    cnts = [jnp.zeros((1, LANES), F32) for _ in range(TM_MIX // TM_CNT)]
    for s, (r, lg) in enumerate(zip(subs, logits)):
        route, onehot = _route(lg)
        route_ref[r, :] = route
        cnts[s // subs_per_cnt] = cnts[s // subs_per_cnt] + jnp.sum(onehot, axis=0, keepdims=True)
    for c, cnt in enumerate(cnts):
        cnt_ref[c * 8:(c + 1) * 8, :] = jnp.broadcast_to(cnt, (8, LANES))


def _route(logits):
    lane = lax.broadcasted_iota(I32, logits.shape, 1)
    neg = -jnp.inf
    gl = jnp.where(lane < N_GROUPS, logits, neg)
    gmax = jnp.max(gl, axis=-1, keepdims=True)
    g_idx = jnp.min(jnp.where(gl == gmax, lane, LANES), axis=-1, keepdims=True)
    p_g = 1.0 / jnp.sum(jnp.exp(gl - gmax), axis=-1, keepdims=True)
    start = N_GROUPS + EXPERTS_PER_GROUP * g_idx
    el = jnp.where((lane >= start) & (lane < start + EXPERTS_PER_GROUP), logits, neg)
    v1 = jnp.max(el, axis=-1, keepdims=True)
    i1 = jnp.min(jnp.where(el == v1, lane, LANES), axis=-1, keepdims=True)
    el2 = jnp.where(lane == i1, neg, el)
    v2 = jnp.max(el2, axis=-1, keepdims=True)
    i2 = jnp.min(jnp.where(el2 == v2, lane, LANES), axis=-1, keepdims=True)
    e21 = jnp.exp(v2 - v1)
    w1 = p_g / (1.0 + e21)
    w2 = p_g * e21 / (1.0 + e21)
    e1 = i1 - N_GROUPS
    e2 = i2 - N_GROUPS
    route = jnp.where(lane == 0, e1.astype(F32),
                      jnp.where(lane == 1, e2.astype(F32),
                                jnp.where(lane == 2, w1, jnp.where(lane == 3, w2, 0.0))))
    onehot = ((lane == e1) | (lane == e2)).astype(F32)
    return route, onehot


def _mix(zb, za, xf, wo, g, wr, br, chunk, t):
    n_tiles = t // TM_MIX
    first = chunk * n_tiles
    cnt_rows = TM_MIX // TM_CNT * 8
    full = lambda shape: pl.BlockSpec(shape, lambda i: (0, 0), pipeline_mode=pl.Buffered(1))
    tile = lambda w=D_MODEL: pl.BlockSpec((TM_MIX, w), lambda i: (i, 0))
    src = lambda c=0: pl.BlockSpec((TM_MIX, D_MODEL), lambda i: (first + i, c))
    return pl.pallas_call(
        _mix_kernel,
        name="merge_router",
        grid=(n_tiles,),
        in_specs=[
            src(), src(), src(),
            full((D_MODEL, D_MODEL)), full((1, D_MODEL)),
            full((D_MODEL, 2 * LANES)), full((1, LANES)),
        ],
        out_specs=[tile(PACKED), tile(PACKED), tile(LANES),
                   pl.BlockSpec((cnt_rows, LANES), lambda i: (i, 0))],
        out_shape=[
            jax.ShapeDtypeStruct((t, PACKED), I32),
            jax.ShapeDtypeStruct((t, PACKED), I32),
            jax.ShapeDtypeStruct((t, LANES), F32),
            jax.ShapeDtypeStruct((n_tiles * cnt_rows, LANES), F32),
        ],
        compiler_params=pltpu.CompilerParams(
            dimension_semantics=("arbitrary",), vmem_limit_bytes=VMEM_LIMIT),
    )(zb, za, xf, wo, g, wr, br)


def _pos_kernel(route_ref, base_ref, pos_ref):
    lane = lax.broadcasted_iota(I32, (TM_CNT, LANES), 1)
    r = lax.broadcasted_iota(I32, (TM_CNT, TM_CNT), 0)
    c = lax.broadcasted_iota(I32, (TM_CNT, TM_CNT), 1)
    lower = (c < r).astype(BF16)
    subs = [slice(s * TM_CNT, (s + 1) * TM_CNT) for s in range(POS_TILES)]
    routes = [route_ref[rs, :] for rs in subs]
    e1 = [jnp.sum(jnp.where(lane == 0, rt, 0.0), axis=-1, keepdims=True).astype(I32) for rt in routes]
    e2 = [jnp.sum(jnp.where(lane == 1, rt, 0.0), axis=-1, keepdims=True).astype(I32) for rt in routes]
    onehot = [((lane == a) | (lane == b)).astype(BF16) for a, b in zip(e1, e2)]
    before = [jnp.dot(lower, oh, preferred_element_type=F32) + base_ref[s]
              for s, oh in enumerate(onehot)]
    for s, rs in enumerate(subs):
        p1 = jnp.sum(jnp.where(lane == e1[s], before[s], 0.0), axis=-1, keepdims=True)
        p2 = jnp.sum(jnp.where(lane == e2[s], before[s], 0.0), axis=-1, keepdims=True)
        packed = jnp.where(lane == 0, p1, jnp.where(lane == 1, p2, 0.0))
        pos_ref[:, rs] = jnp.transpose(packed)[0:TOP_K, :].astype(I32)


def _positions(route, base):
    t = route.shape[0]
    n_steps = t // (TM_CNT * POS_TILES)
    return pl.pallas_call(
        _pos_kernel,
        name="positions",
        grid=(n_steps,),
        in_specs=[
            pl.BlockSpec((TM_CNT * POS_TILES, LANES), lambda i: (i, 0)),
            pl.BlockSpec((POS_TILES, 1, LANES), lambda i: (i, 0, 0)),
        ],
        out_specs=pl.BlockSpec((TOP_K, TM_CNT * POS_TILES), lambda i: (0, i)),
        out_shape=jax.ShapeDtypeStruct((TOP_K, t), I32),
        compiler_params=pltpu.CompilerParams(dimension_semantics=("arbitrary",)),
    )(route, base)


def _sc_mesh():
    return plsc.VectorSubcoreMesh(core_axis_name="c", subcore_axis_name="s",
                                  num_cores=SC_CORES, num_subcores=SC_SUBCORES)


def _sc_worker():
    return lax.axis_index("s") * SC_CORES + lax.axis_index("c")


def _dispatch(pos, hp, rows):
    t = hp.shape[0]
    per_w = t // SC_WORKERS
    n_ch = per_w // SC_CHUNK
    pos4 = pos.reshape(TOP_K, SC_WORKERS, n_ch, SC_CHUNK)

    @functools.partial(
        pl.kernel, mesh=_sc_mesh(),
        out_type=jax.ShapeDtypeStruct((rows, PACKED), I32),
        scratch_types=[pltpu.VMEM((TOP_K, n_ch, SC_CHUNK), I32),
                       pltpu.VMEM((SC_CHUNK, PACKED), I32)])
    def scatter(hp_hbm, pos_hbm, xs_hbm, idx_v, rows_v):
        wid = _sc_worker()
        for k in range(TOP_K):
            pltpu.sync_copy(pos_hbm.at[k, wid], idx_v.at[k])

        def body(c, carry):
            start = pl.multiple_of(wid * per_w + c * SC_CHUNK, SC_CHUNK)
            pltpu.sync_copy(hp_hbm.at[pl.ds(start, SC_CHUNK)], rows_v)
            for k in range(TOP_K):
                pltpu.sync_copy(rows_v, xs_hbm.at[idx_v.at[k, c]])
            return carry

        lax.fori_loop(0, n_ch, body, 0)

    return scatter(hp, pos4)


def _gather_rows(table, idx):
    n = idx.shape[0]
    per_w = n // SC_WORKERS
    n_ch = per_w // SC_CHUNK
    idx3 = idx.reshape(SC_WORKERS, n_ch, SC_CHUNK)

    @functools.partial(
        pl.kernel, mesh=_sc_mesh(),
        out_type=jax.ShapeDtypeStruct((n, PACKED), I32),
        scratch_types=[pltpu.VMEM((n_ch, SC_CHUNK), I32),
                       pltpu.VMEM((SC_CHUNK, PACKED), I32)])
    def gather(table_hbm, idx_hbm, out_hbm, idx_v, rows_v):
        wid = _sc_worker()
        pltpu.sync_copy(idx_hbm.at[wid], idx_v)

        def body(c, carry):
            start = pl.multiple_of(wid * per_w + c * SC_CHUNK, SC_CHUNK)
            pltpu.sync_copy(table_hbm.at[idx_v.at[c]], rows_v)
            pltpu.sync_copy(rows_v, out_hbm.at[pl.ds(start, SC_CHUNK)])
            return carry

        lax.fori_loop(0, n_ch, body, 0)

    return gather(table, idx3)


def _expert_kernel(te_ref, nx_ref, sl_ref, ts_ref, tv_ref, xs_ref, wg_hbm, wu_hbm, wd_hbm, o_ref,
                   wg_s, wu_s, wd_s, sem):
    del ts_ref
    i = pl.program_id(0)
    n_valid = tv_ref[i]
    expert = te_ref[i]
    slot = sl_ref[i]

    def weight_copies(e, s):
        return (pltpu.make_async_copy(wg_hbm.at[e], wg_s.at[s], sem.at[s, 0]),
                pltpu.make_async_copy(wu_hbm.at[e], wu_s.at[s], sem.at[s, 1]),
                pltpu.make_async_copy(wd_hbm.at[e], wd_s.at[s], sem.at[s, 2]))

    @pl.when(i == 0)
    def _():
        for copy in weight_copies(expert, slot):
            copy.start()

    @pl.when(jnp.logical_or(i == 0, expert != te_ref[jnp.maximum(i - 1, 0)]))
    def _():
        for copy in weight_copies(expert, slot):
            copy.wait()
        nxt = nx_ref[i]

        @pl.when(nxt >= 0)
        def _():
            for copy in weight_copies(nxt, 1 - slot):
                copy.start()

    def mlp(n_sub):
        wg, wu, wd = wg_s.at[slot], wu_s.at[slot], wd_s.at[slot]
        subs = [slice(s * SUB_EXP, (s + 1) * SUB_EXP) for s in range(n_sub)]
        xin = []
        for r in subs:
            rid = r.start + lax.broadcasted_iota(I32, (SUB_EXP, PACKED), 0)
            lo, hi = _unpack_bf16_pairs(jnp.where(rid < n_valid, xs_ref[r, :], 0))
            xin.append((lo.astype(BF16), hi.astype(BF16)))
        ab = [(jnp.dot(lo, wg[:PACKED, :], preferred_element_type=F32)
               + jnp.dot(hi, wg[PACKED:, :], preferred_element_type=F32),
               jnp.dot(lo, wu[:PACKED, :], preferred_element_type=F32)
               + jnp.dot(hi, wu[PACKED:, :], preferred_element_type=F32)) for lo, hi in xin]
        for r, (ai, bi) in zip(subs, ab):
            hm = (ai * jax.nn.sigmoid(ai) * bi).astype(BF16)
            o_ref[r, :] = _pack_bf16_pairs(jnp.dot(hm, wd[...], preferred_element_type=F32))
        if n_sub * SUB_EXP < TM_EXP:
            o_ref[n_sub * SUB_EXP:, :] = jnp.zeros((TM_EXP - n_sub * SUB_EXP, PACKED), I32)

    n_subs = TM_EXP // SUB_EXP
    for n_sub in range(n_subs + 1):
        lo_rows = (n_sub - 1) * SUB_EXP if n_sub else -1
        in_range = jnp.logical_and(n_valid > lo_rows, n_valid <= n_sub * SUB_EXP)
        pl.when(in_range)(functools.partial(mlp, n_sub))


def _experts(tile_expert, next_expert, tile_slot, tile_src, tile_valid, xs, wg, wu, wd):
    rows = xs.shape[0]
    hbm = pl.BlockSpec(memory_space=pl.ANY)
    grid_spec = pltpu.PrefetchScalarGridSpec(
        num_scalar_prefetch=5,
        grid=(rows // TM_EXP,),
        in_specs=[pl.BlockSpec((TM_EXP, PACKED), lambda i, te, nx, sl, ts, tv: (ts[i], 0)),
                  hbm, hbm, hbm],
        out_specs=pl.BlockSpec((TM_EXP, PACKED), lambda i, te, nx, sl, ts, tv: (i, 0)),
        scratch_shapes=[
            pltpu.VMEM((2, D_MODEL, D_FF), BF16), pltpu.VMEM((2, D_MODEL, D_FF), BF16),
            pltpu.VMEM((2, D_FF, D_MODEL), BF16),
            pltpu.SemaphoreType.DMA((2, 3)),
        ],
    )
    return pl.pallas_call(
        _expert_kernel,
        name="experts",
        grid_spec=grid_spec,
        out_shape=jax.ShapeDtypeStruct((rows, PACKED), I32),
        compiler_params=pltpu.CompilerParams(
            dimension_semantics=("arbitrary",), vmem_limit_bytes=VMEM_LIMIT),
    )(tile_expert, next_expert, tile_slot, tile_src, tile_valid, xs, wg, wu, wd)


def _combine_kernel(y1_ref, y2_ref, route_ref, xmid_ref, g_ref, *rest):
    o_ref = rest[-1]
    route = route_ref[...]
    lane = lax.broadcasted_iota(I32, route.shape, 1)
    w1 = jnp.sum(jnp.where(lane == 2, route, 0.0), axis=-1, keepdims=True)
    w2 = jnp.sum(jnp.where(lane == 3, route, 0.0), axis=-1, keepdims=True)
    lo1, hi1 = _unpack_bf16_pairs(y1_ref[...])
    lo2, hi2 = _unpack_bf16_pairs(y2_ref[...])
    lox, hix = _unpack_bf16_pairs(xmid_ref[...])
    x_out = jnp.concatenate([lox + (lo1 * w1 + lo2 * w2), hix + (hi1 * w1 + hi2 * w2)], axis=1)
    o_ref[...] = _rms(x_out, g_ref[...])


def _combine(yg, route, xmid, g, chunk, t_total, out_prev):
    t = xmid.shape[0]
    n_tiles = t // TM_CMB
    first = chunk * n_tiles
    in_specs = [
        pl.BlockSpec((TM_CMB, PACKED), lambda i: (i, 0)),
        pl.BlockSpec((TM_CMB, PACKED), lambda i: (n_tiles + i, 0)),
        pl.BlockSpec((TM_CMB, LANES), lambda i: (i, 0)),
        pl.BlockSpec((TM_CMB, PACKED), lambda i: (i, 0)),
        pl.BlockSpec((1, D_MODEL), lambda i: (0, 0)),
    ]
    args = [yg, yg, route, xmid, g]
    aliases = {}
    if out_prev is not None:
        in_specs.append(pl.BlockSpec(memory_space=pl.ANY))
        aliases = {len(args): 0}
        args.append(out_prev)
    return pl.pallas_call(
        _combine_kernel,
        name="combine",
        grid=(n_tiles,),
        in_specs=in_specs,
        out_specs=pl.BlockSpec((TM_CMB, D_MODEL), lambda i: (first + i, 0)),
        out_shape=jax.ShapeDtypeStruct((t_total, D_MODEL), F32),
        input_output_aliases=aliases,
        compiler_params=pltpu.CompilerParams(
            dimension_semantics=("arbitrary",), vmem_limit_bytes=VMEM_LIMIT),
    )(*args)


def _split_bf16(w):
    hi = w.astype(BF16)
    lo = (w - hi.astype(F32)).astype(BF16)
    return hi, lo


def kernel(x, norm_mix, w_in, conv_w, conv_b, w_a_out, sinks, w_b_out, w_o, norm_ffn, w_group,
           b_group, w_expert, b_expert, w_gate, w_up, w_down, norm_final):
    bsz, seq, d = x.shape
    t = bsz * seq
    assert d == D_MODEL and seq % TM_PROJ == 0 and seq % TQ_ATTN == 0
    xf = x.reshape(t, d)
    row = lambda v: v.reshape(1, -1)

    w_b, w_c, w_u, w_q, w_k, w_v, w_ga, w_gb = jnp.split(w_in, REF_SPLITS, axis=1)
    w_cu = jnp.concatenate([w_c, w_u], axis=1).astype(BF16)
    w_bg = jnp.concatenate([w_b, w_ga], axis=1).astype(BF16)
    w_rest = jnp.concatenate([w_gb, w_q, w_k, w_v], axis=1).astype(BF16)

    za, proj, w_gate, w_up, w_down = _inproj(xf, row(norm_mix), w_cu, w_bg, w_rest, conv_w,
                                             row(conv_b), w_a_out.astype(BF16), seq,
                                             w_gate, w_up, w_down)
    zb = _attention(proj, sinks, w_b_out.astype(BF16), seq)

    pad = LANES - N_GROUPS - N_EXPERTS
    w_r = jnp.concatenate([w_group, w_expert, jnp.zeros((d, pad), F32)], axis=1)
    b_r = jnp.concatenate([b_group, b_expert, jnp.zeros((pad,), F32)]).reshape(1, LANES)
    wr = jnp.concatenate(_split_bf16(w_r), axis=1)
    wo = w_o.astype(BF16)

    t_chunk = t // MOE_CHUNKS
    out = None
    for chunk in range(MOE_CHUNKS):
        xmid, h2, route, cnt = _mix(zb, za, xf, wo, row(norm_ffn), wr, b_r, chunk, t_chunk)
        out = _moe_chunk(xmid, h2, route, cnt, w_gate, w_up, w_down, row(norm_final), chunk, t, out)
    return out.reshape(bsz, seq, d)


def _moe_chunk(xmid, h2, route, cnt, w_gate, w_up, w_down, g_final, chunk, t_total, out_prev):
    t = xmid.shape[0]
    n_tiles = t // TM_CNT
    cnt = cnt.reshape(n_tiles, 8, LANES)[:, 0, :N_EXPERTS].astype(I32)
    totals = jnp.sum(cnt, axis=0)
    tiles_e = (totals + TM_EXP - 1) // TM_EXP
    tile_end = jnp.cumsum(tiles_e)
    offset = (tile_end - tiles_e) * TM_EXP
    base = offset[None, :] + jnp.cumsum(cnt, axis=0) - cnt
    base = jnp.pad(base, ((0, 0), (0, LANES - N_EXPERTS))).astype(F32).reshape(n_tiles, 1, LANES)
    rows = t * TOP_K + N_EXPERTS * TM_EXP
    n_active = tile_end[-1]
    tile_id = jnp.arange(rows // TM_EXP, dtype=I32)
    tile_src = jnp.minimum(tile_id, n_active - 1)
    tile_expert = jnp.sum((tile_src[:, None] >= tile_end[None, :]).astype(I32), axis=1)
    tile_expert = jnp.minimum(tile_expert, N_EXPERTS - 1)
    row_in_expert = (tile_id - (tile_end - tiles_e)[tile_expert]) * TM_EXP
    tile_valid = jnp.clip(totals[tile_expert] - row_in_expert, 0, TM_EXP)
    tile_valid = jnp.where(tile_id < n_active, tile_valid, 0).astype(I32)
    after = tile_end[tile_expert]
    next_expert = jnp.where(after < n_active, tile_expert[jnp.minimum(after, n_active - 1)], -1)
    first_of_expert = jnp.concatenate(
        [jnp.ones((1,), I32), (tile_expert[1:] != tile_expert[:-1]).astype(I32)])
    tile_slot = (jnp.cumsum(first_of_expert) - 1) % 2

    pos = _positions(route, base)
    xs = _dispatch(pos, h2, rows)
    ys = _experts(tile_expert.astype(I32), next_expert.astype(I32), tile_slot.astype(I32),
                  tile_src.astype(I32), tile_valid, xs, w_gate, w_up, w_down)
    yg = _gather_rows(ys, pos.reshape(TOP_K * t))
    return _combine(yg, route, xmid, g_final, chunk, t_total, out_prev)
```

```python
import functools
import math

import jax
import jax.numpy as jnp
from jax import lax
from jax.experimental import pallas as pl
from jax.experimental.pallas import tpu as pltpu
from jax.experimental.pallas import tpu_sc as plsc

F32 = jnp.float32
BF16 = jnp.bfloat16
I32 = jnp.int32

D_MODEL = 1024
HEAD_DIM = 64
N_HEADS = 16
N_KV_HEADS = 4
GROUP = N_HEADS // N_KV_HEADS
KV_WIDTH = N_KV_HEADS * HEAD_DIM
WINDOW = 128
N_GROUPS = 4
EXPERTS_PER_GROUP = 8
N_EXPERTS = N_GROUPS * EXPERTS_PER_GROUP
TOP_K = 2
D_FF = 512
EPS = 1e-6
LANES = 128

REF_SPLITS = (1024, 2048, 3072, 4096, 4352, 4608, 5632)
REST_COLS = 2 * D_MODEL + 2 * KV_WIDTH
COL_GB, COL_Q = 0, 1
COL_K, COL_V = 2 * D_MODEL // KV_WIDTH, 2 * D_MODEL // KV_WIDTH + 1

TM_PROJ = 512
TQ_ATTN = 1024
TM_MIX = 1024
SUB_MIX = 256
TM_CNT = 512
POS_TILES = 8
MOE_CHUNKS = 2
TM_EXP = 512
SUB_EXP = 256
TM_CMB = 1024
HALO_ROWS = 8
VMEM_LIMIT = 56 * 1024 * 1024
PACKED = D_MODEL // 2

SC_CORES = 2
SC_SUBCORES = 16
SC_WORKERS = SC_CORES * SC_SUBCORES
SC_CHUNK = 64


def _rms(x, g):
    r = lax.rsqrt(jnp.mean(x * x, axis=-1, keepdims=True) + EPS)
    return (x * r) * g


def _pack_bf16_pairs(x):
    n = x.shape[1] // 2
    lo = lax.bitcast_convert_type(x[:, :n].astype(BF16).astype(F32), I32)
    hi = lax.bitcast_convert_type(x[:, n:].astype(BF16).astype(F32), I32)
    return (hi & jnp.int32(-65536)) | lax.shift_right_logical(lo, 16)


def _unpack_bf16_pairs(p):
    lo = lax.bitcast_convert_type(lax.shift_left(p, 16), F32)
    hi = lax.bitcast_convert_type(p & jnp.int32(-65536), F32)
    return lo, hi


def _inproj_kernel(x_ref, g_ref, wcu_ref, wbg_ref, wrest_ref, cw_ref, cb_ref, wa_ref,
                   eg_ref, eu_ref, ed_ref,
                   za_ref, proj_ref, eg_out, eu_out, ed_out, halo_ref, *, tiles_per_seq):
    i = pl.program_id(0)
    eg_out[...] = eg_ref[...].astype(BF16)
    eu_out[...] = eu_ref[...].astype(BF16)
    ed_out[...] = ed_ref[...].astype(BF16)
    h = _rms(x_ref[...], g_ref[...]).astype(BF16)
    pcu = jnp.dot(h, wcu_ref[...], preferred_element_type=F32)
    pbg = jnp.dot(h, wbg_ref[...], preferred_element_type=F32)
    proj_ref[...] = jnp.dot(h, wrest_ref[...], preferred_element_type=F32).astype(BF16)
    cu = pcu[:, :D_MODEL] * pcu[:, D_MODEL:]
    first = (i % tiles_per_seq) == 0
    hist = jnp.where(first, 0.0, halo_ref[...])
    prev1 = hist[HALO_ROWS - 1:HALO_ROWS]
    prev2 = hist[HALO_ROWS - 2:HALO_ROWS - 1]
    halo_ref[...] = cu[TM_PROJ - HALO_ROWS:, :]
    row = lax.broadcasted_iota(I32, cu.shape, 0)
    cu1 = jnp.where(row == 0, prev1, pltpu.roll(cu, 1, 0))
    cu2 = jnp.where(row == 0, prev2, jnp.where(row == 1, prev1, pltpu.roll(cu, 2, 0)))
    cw = cw_ref[...]
    y = cw[0:1] * cu2 + cw[1:2] * cu1 + cw[2:3] * cu + cb_ref[...]
    ya = (pbg[:, :D_MODEL] * y).astype(BF16)
    z = jnp.dot(ya, wa_ref[...], preferred_element_type=F32)
    za_ref[...] = (jax.nn.sigmoid(pbg[:, D_MODEL:]) * z).astype(BF16)


def _inproj(xf, g, w_cu, w_bg, w_rest, conv_w, conv_b, wa, seq, w_gate, w_up, w_down):
    t = xf.shape[0]
    n_tiles = t // TM_PROJ
    const = lambda shape: pl.BlockSpec(shape, lambda i: (0, 0), pipeline_mode=pl.Buffered(1))
    slabs = [w.reshape(-1, w.shape[-1]) for w in (w_gate, w_up, w_down)]
    slab_rows = [s.shape[0] // n_tiles for s in slabs]
    assert all(s.shape[0] == r * n_tiles and r % 16 == 0 for s, r in zip(slabs, slab_rows))
    slab_specs = [pl.BlockSpec((r, s.shape[1]), lambda i: (i, 0)) for s, r in zip(slabs, slab_rows)]
    outs = pl.pallas_call(
        functools.partial(_inproj_kernel, tiles_per_seq=seq // TM_PROJ),
        name="inproj_conv",
        grid=(n_tiles,),
        in_specs=[
            pl.BlockSpec((TM_PROJ, D_MODEL), lambda i: (i, 0)),
            const((1, D_MODEL)),
            const((D_MODEL, 2 * D_MODEL)), const((D_MODEL, 2 * D_MODEL)), const((D_MODEL, REST_COLS)),
            const((3, D_MODEL)), const((1, D_MODEL)), const((D_MODEL, D_MODEL)),
        ] + slab_specs,
        out_specs=[pl.BlockSpec((TM_PROJ, D_MODEL), lambda i: (i, 0)),
                   pl.BlockSpec((TM_PROJ, REST_COLS), lambda i: (i, 0))] + slab_specs,
        out_shape=[jax.ShapeDtypeStruct((t, D_MODEL), BF16),
                   jax.ShapeDtypeStruct((t, REST_COLS), BF16)]
                  + [jax.ShapeDtypeStruct(s.shape, BF16) for s in slabs],
        scratch_shapes=[pltpu.VMEM((HALO_ROWS, D_MODEL), F32)],
        compiler_params=pltpu.CompilerParams(
            dimension_semantics=("arbitrary",), vmem_limit_bytes=VMEM_LIMIT),
    )(xf, g, w_cu, w_bg, w_rest, conv_w, conv_b, wa, *slabs)
    za, proj, eg, eu, ed = outs
    return za, proj, eg.reshape(w_gate.shape), eu.reshape(w_up.shape), ed.reshape(w_down.shape)


def _attn_kernel(sink_ref, q_ref, k_ref, v_ref, kp_ref, vp_ref, gb_ref, wb_ref, o_ref,
                 *, tiles_per_seq):
    first_tile = (pl.program_id(0) % tiles_per_seq) == 0
    ks = lax.broadcasted_iota(I32, (WINDOW, WINDOW), 0)
    qq = lax.broadcasted_iota(I32, (WINDOW, WINDOW), 1)
    own = ks <= qq
    dist = jnp.where(own, qq - ks, qq - ks + WINDOW).astype(F32)
    visible0 = jnp.logical_or(own, jnp.logical_not(first_tile))
    log2e = math.log2(math.e)
    c_scale = log2e / math.sqrt(HEAD_DIM)
    nt = (((1,), (1,)), ((), ()))
    zk = jnp.zeros((2 * WINDOW, HEAD_DIM), BF16)

    def transposed(v_blk):
        return jnp.transpose(v_blk.astype(F32)).astype(BF16)

    def project(rows_p, attn_blk):
        yb = jnp.dot(attn_blk, wb_ref[...], preferred_element_type=F32)
        o_ref[rows_p, :] = (jax.nn.sigmoid(gb_ref[rows_p, :].astype(F32)) * yb).astype(BF16)

    pending = None
    prev_k = kp_ref[...]
    prev_vt = transposed(vp_ref[...])
    for sb in range(TQ_ATTN // WINDOW):
        rows = slice(sb * WINDOW, (sb + 1) * WINDOW)
        cur_k = k_ref[rows, :]
        cur_vt = transposed(v_ref[rows, :])
        scores, vcats = [], []
        for kh in range(N_KV_HEADS):
            cols = slice(kh * HEAD_DIM, (kh + 1) * HEAD_DIM)
            kcat = jnp.concatenate([prev_k[:, cols], cur_k[:, cols]], axis=0)
            vcats.append(jnp.concatenate([prev_vt[cols, :], cur_vt[cols, :]], axis=1))
            qg = jnp.concatenate([q_ref[rows, (2 * kh) * LANES:(2 * kh + 1) * LANES],
                                  q_ref[rows, (2 * kh + 1) * LANES:(2 * kh + 2) * LANES]], axis=0)
            k_pad = jnp.concatenate([jnp.concatenate([kcat, zk], axis=1),
                                     jnp.concatenate([zk, kcat], axis=1)], axis=0)
            scores.append(lax.dot_general(k_pad, qg, nt, preferred_element_type=F32))
        if pending is not None:
            project(*pending)
        probs, rdens = [], []
        for kh in range(N_KV_HEADS):
            for pos in range(2):
                pr, rd = [], []
                for half in range(2):
                    h = kh * GROUP + 2 * half + pos
                    slope = 2.0 ** (-8.0 * (h + 1) / N_HEADS)
                    qcols = slice(half * WINDOW, (half + 1) * WINDOW)
                    krow = pos * 2 * WINDOW
                    st = scores[kh]
                    s = (jnp.where(own, st[krow + WINDOW:krow + 2 * WINDOW, qcols],
                                   st[krow:krow + WINDOW, qcols]) * c_scale
                         - (slope * log2e) * dist)
                    if sb == 0:
                        s = jnp.where(visible0, s, -jnp.inf)
                    m = jnp.max(s, axis=0, keepdims=True)
                    p = jnp.exp2(s - m)
                    den = jnp.sum(p, axis=0, keepdims=True) + jnp.exp2(sink_ref[h] * log2e - m)
                    rd.append(1.0 / den)
                    pr.append(jnp.concatenate(
                        [jnp.where(own, 0.0, p).astype(BF16), jnp.where(own, p, 0.0).astype(BF16)],
                        axis=0))
                probs.append(jnp.concatenate(pr, axis=1))
                rdens.append(jnp.concatenate(rd, axis=1))
        out_t = [None] * N_HEADS
        for kh in range(N_KV_HEADS):
            for pos in range(2):
                o2 = jnp.dot(vcats[kh], probs[2 * kh + pos], preferred_element_type=F32)
                o2 = o2 * rdens[2 * kh + pos]
                out_t[kh * GROUP + pos] = o2[:, :WINDOW]
                out_t[kh * GROUP + 2 + pos] = o2[:, WINDOW:]
        pending = (rows, jnp.transpose(jnp.concatenate(out_t, axis=0)).astype(BF16))
        prev_k, prev_vt = cur_k, cur_vt
    project(*pending)


def _attention(proj, sinks, wb, seq):
    t = proj.shape[0]
    sub = TQ_ATTN // WINDOW
    return pl.pallas_call(
        functools.partial(_attn_kernel, tiles_per_seq=seq // TQ_ATTN),
        name="swattn",
        grid=(t // TQ_ATTN,),
        in_specs=[
            pl.BlockSpec(memory_space=pltpu.SMEM),
            pl.BlockSpec((TQ_ATTN, D_MODEL), lambda i: (i, COL_Q)),
            pl.BlockSpec((TQ_ATTN, KV_WIDTH), lambda i: (i, COL_K)),
            pl.BlockSpec((TQ_ATTN, KV_WIDTH), lambda i: (i, COL_V)),
            pl.BlockSpec((WINDOW, KV_WIDTH), lambda i: (jnp.maximum(i * sub - 1, 0), COL_K)),
            pl.BlockSpec((WINDOW, KV_WIDTH), lambda i: (jnp.maximum(i * sub - 1, 0), COL_V)),
            pl.BlockSpec((TQ_ATTN, D_MODEL), lambda i: (i, COL_GB)),
            pl.BlockSpec((D_MODEL, D_MODEL), lambda i: (0, 0), pipeline_mode=pl.Buffered(1)),
        ],
        out_specs=pl.BlockSpec((TQ_ATTN, D_MODEL), lambda i: (i, 0)),
        out_shape=jax.ShapeDtypeStruct((t, D_MODEL), BF16),
        compiler_params=pltpu.CompilerParams(
            dimension_semantics=("arbitrary",), vmem_limit_bytes=VMEM_LIMIT),
    )(sinks, proj, proj, proj, proj, proj, proj, wb)


def _mix_kernel(zb_ref, za_ref, x_ref, wo_ref, g_ref, wr_ref, br_ref,
                xmid_ref, h_ref, route_ref, cnt_ref):
    subs = [slice(s * SUB_MIX, (s + 1) * SUB_MIX) for s in range(TM_MIX // SUB_MIX)]
    merged = [za_ref[r, :] + zb_ref[r, :] for r in subs]
    xm = [x_ref[r, :] + jnp.dot(m, wo_ref[...], preferred_element_type=F32)
          for r, m in zip(subs, merged)]
    hs = []
    for r, v in zip(subs, xm):
        xmid_ref[r, :] = _pack_bf16_pairs(v)
        h = _rms(v, g_ref[...])
        h_ref[r, :] = _pack_bf16_pairs(h)
        hs.append(h)
    wr = wr_ref[...]
    logits = []
    for h in hs:
        h_hi = h.astype(BF16)
        h_lo = (h - h_hi.astype(F32)).astype(BF16)
        both = jnp.dot(h_hi, wr, preferred_element_type=F32)
        logits.append(both[:, :LANES] + both[:, LANES:]
                      + jnp.dot(h_lo, wr[:, :LANES], preferred_element_type=F32) + br_ref[...])
    subs_per_cnt = TM_CNT // SUB_MIX
    cnts = [jnp.zeros((8, LANES), F32) for _ in range(TM_MIX // TM_CNT)]
    for s, (r, lg) in enumerate(zip(subs, logits)):
        route, cnt = _route(lg)
        route_ref[r, :] = route
        cnts[s // subs_per_cnt] = cnts[s // subs_per_cnt] + cnt
    for c, cnt in enumerate(cnts):
        cnt_ref[c * 8:(c + 1) * 8, :] = cnt


def _route(logits):
    n = logits.shape[0]
    lt = jnp.transpose(logits)
    sub = lax.broadcasted_iota(I32, (EXPERTS_PER_GROUP, n), 0)
    neg = -jnp.inf
    gl = jnp.where(sub < N_GROUPS, lt[N_EXPERTS:N_EXPERTS + EXPERTS_PER_GROUP], neg)
    gmax = jnp.max(gl, axis=0, keepdims=True)
    g_idx = jnp.min(jnp.where(gl == gmax, sub, EXPERTS_PER_GROUP), axis=0, keepdims=True)
    p_g = 1.0 / jnp.sum(jnp.exp(gl - gmax), axis=0, keepdims=True)
    v1 = v2 = i1 = i2 = None
    for g in range(N_GROUPS):
        eg = lt[g * EXPERTS_PER_GROUP:(g + 1) * EXPERTS_PER_GROUP]
        a1 = jnp.max(eg, axis=0, keepdims=True)
        j1 = jnp.min(jnp.where(eg == a1, sub, EXPERTS_PER_GROUP), axis=0, keepdims=True)
        eg2 = jnp.where(sub == j1, neg, eg)
        a2 = jnp.max(eg2, axis=0, keepdims=True)
        j2 = jnp.min(jnp.where(eg2 == a2, sub, EXPERTS_PER_GROUP), axis=0, keepdims=True)
        if g == 0:
            v1, v2, i1, i2 = a1, a2, j1, j2
        else:
            chosen = g_idx == g
            v1, v2 = jnp.where(chosen, a1, v1), jnp.where(chosen, a2, v2)
            i1, i2 = jnp.where(chosen, j1, i1), jnp.where(chosen, j2, i2)
    e21 = jnp.exp(v2 - v1)
    w1 = p_g / (1.0 + e21)
    w2 = p_g * e21 / (1.0 + e21)
    e1 = g_idx * EXPERTS_PER_GROUP + i1
    e2 = g_idx * EXPERTS_PER_GROUP + i2
    rows8 = jnp.where(sub == 0, e1.astype(F32),
                      jnp.where(sub == 1, e2.astype(F32),
                                jnp.where(sub == 2, w1, jnp.where(sub == 3, w2, 0.0))))
    route_t = jnp.concatenate([rows8, jnp.zeros((LANES - EXPERTS_PER_GROUP, n), F32)], axis=0)
    expert_row = lax.broadcasted_iota(I32, (LANES, n), 0)
    onehot_t = ((expert_row == e1) | (expert_row == e2)).astype(BF16)
    cnt = lax.dot_general(jnp.ones((8, n), BF16), onehot_t, (((1,), (1,)), ((), ())),
                          preferred_element_type=F32)
    return jnp.transpose(route_t), cnt


def _mix(zb, za, xf, wo, g, wr, br, chunk, t):
    n_tiles = t // TM_MIX
    first = chunk * n_tiles
    cnt_rows = TM_MIX // TM_CNT * 8
    full = lambda shape: pl.BlockSpec(shape, lambda i: (0, 0), pipeline_mode=pl.Buffered(1))
    tile = lambda w=D_MODEL: pl.BlockSpec((TM_MIX, w), lambda i: (i, 0))
    src = lambda c=0: pl.BlockSpec((TM_MIX, D_MODEL), lambda i: (first + i, c))
    return pl.pallas_call(
        _mix_kernel,
        name="merge_router",
        grid=(n_tiles,),
        in_specs=[
            src(), src(), src(),
            full((D_MODEL, D_MODEL)), full((1, D_MODEL)),
            full((D_MODEL, 2 * LANES)), full((1, LANES)),
        ],
        out_specs=[tile(PACKED), tile(PACKED), tile(LANES),
                   pl.BlockSpec((cnt_rows, LANES), lambda i: (i, 0))],
        out_shape=[
            jax.ShapeDtypeStruct((t, PACKED), I32),
            jax.ShapeDtypeStruct((t, PACKED), I32),
            jax.ShapeDtypeStruct((t, LANES), F32),
            jax.ShapeDtypeStruct((n_tiles * cnt_rows, LANES), F32),
        ],
        compiler_params=pltpu.CompilerParams(
            dimension_semantics=("arbitrary",), vmem_limit_bytes=VMEM_LIMIT),
    )(zb, za, xf, wo, g, wr, br)


def _pos_kernel(route_ref, base_ref, pos_ref):
    lane = lax.broadcasted_iota(I32, (TM_CNT, LANES), 1)
    r = lax.broadcasted_iota(I32, (TM_CNT, TM_CNT), 0)
    c = lax.broadcasted_iota(I32, (TM_CNT, TM_CNT), 1)
    lower = (c < r).astype(BF16)
    subs = [slice(s * TM_CNT, (s + 1) * TM_CNT) for s in range(POS_TILES)]
    routes = [route_ref[rs, :] for rs in subs]
    e1 = [jnp.sum(jnp.where(lane == 0, rt, 0.0), axis=-1, keepdims=True).astype(I32) for rt in routes]
    e2 = [jnp.sum(jnp.where(lane == 1, rt, 0.0), axis=-1, keepdims=True).astype(I32) for rt in routes]
    onehot = [((lane == a) | (lane == b)).astype(BF16) for a, b in zip(e1, e2)]
    before = [jnp.dot(lower, oh, preferred_element_type=F32) + base_ref[s]
              for s, oh in enumerate(onehot)]
    for s, rs in enumerate(subs):
        p1 = jnp.sum(jnp.where(lane == e1[s], before[s], 0.0), axis=-1, keepdims=True)
        p2 = jnp.sum(jnp.where(lane == e2[s], before[s], 0.0), axis=-1, keepdims=True)
        packed = jnp.where(lane == 0, p1, jnp.where(lane == 1, p2, 0.0))
        pos_ref[:, rs] = jnp.transpose(packed)[0:TOP_K, :].astype(I32)


def _positions(route, base):
    t = route.shape[0]
    n_steps = t // (TM_CNT * POS_TILES)
    return pl.pallas_call(
        _pos_kernel,
        name="positions",
        grid=(n_steps,),
        in_specs=[
            pl.BlockSpec((TM_CNT * POS_TILES, LANES), lambda i: (i, 0)),
            pl.BlockSpec((POS_TILES, 1, LANES), lambda i: (i, 0, 0)),
        ],
        out_specs=pl.BlockSpec((TOP_K, TM_CNT * POS_TILES), lambda i: (0, i)),
        out_shape=jax.ShapeDtypeStruct((TOP_K, t), I32),
        compiler_params=pltpu.CompilerParams(dimension_semantics=("arbitrary",)),
    )(route, base)


def _sc_mesh():
    return plsc.VectorSubcoreMesh(core_axis_name="c", subcore_axis_name="s",
                                  num_cores=SC_CORES, num_subcores=SC_SUBCORES)


def _sc_worker():
    return lax.axis_index("s") * SC_CORES + lax.axis_index("c")


def _dispatch(pos, hp, rows):
    t = hp.shape[0]
    per_w = t // SC_WORKERS
    n_ch = per_w // SC_CHUNK
    pos4 = pos.reshape(TOP_K, SC_WORKERS, n_ch, SC_CHUNK)

    @functools.partial(
        pl.kernel, mesh=_sc_mesh(),
        out_type=jax.ShapeDtypeStruct((rows, PACKED), I32),
        scratch_types=[pltpu.VMEM((TOP_K, n_ch, SC_CHUNK), I32),
                       pltpu.VMEM((SC_CHUNK, PACKED), I32)])
    def scatter(hp_hbm, pos_hbm, xs_hbm, idx_v, rows_v):
        wid = _sc_worker()
        for k in range(TOP_K):
            pltpu.sync_copy(pos_hbm.at[k, wid], idx_v.at[k])

        def body(c, carry):
            start = pl.multiple_of(wid * per_w + c * SC_CHUNK, SC_CHUNK)
            pltpu.sync_copy(hp_hbm.at[pl.ds(start, SC_CHUNK)], rows_v)
            for k in range(TOP_K):
                pltpu.sync_copy(rows_v, xs_hbm.at[idx_v.at[k, c]])
            return carry

        lax.fori_loop(0, n_ch, body, 0)

    return scatter(hp, pos4)


def _gather_rows(table, idx):
    n = idx.shape[0]
    per_w = n // SC_WORKERS
    n_ch = per_w // SC_CHUNK
    idx3 = idx.reshape(SC_WORKERS, n_ch, SC_CHUNK)

    @functools.partial(
        pl.kernel, mesh=_sc_mesh(),
        out_type=jax.ShapeDtypeStruct((n, PACKED), I32),
        scratch_types=[pltpu.VMEM((n_ch, SC_CHUNK), I32),
                       pltpu.VMEM((SC_CHUNK, PACKED), I32)])
    def gather(table_hbm, idx_hbm, out_hbm, idx_v, rows_v):
        wid = _sc_worker()
        pltpu.sync_copy(idx_hbm.at[wid], idx_v)

        def body(c, carry):
            start = pl.multiple_of(wid * per_w + c * SC_CHUNK, SC_CHUNK)
            pltpu.sync_copy(table_hbm.at[idx_v.at[c]], rows_v)
            pltpu.sync_copy(rows_v, out_hbm.at[pl.ds(start, SC_CHUNK)])
            return carry

        lax.fori_loop(0, n_ch, body, 0)

    return gather(table, idx3)


def _expert_kernel(te_ref, nx_ref, sl_ref, ts_ref, tv_ref, xs_ref, wg_hbm, wu_hbm, wd_hbm, o_ref,
                   wg_s, wu_s, wd_s, sem):
    del ts_ref
    i = pl.program_id(0)
    n_valid = tv_ref[i]
    expert = te_ref[i]
    slot = sl_ref[i]

    def weight_copies(e, s):
        return (pltpu.make_async_copy(wg_hbm.at[e], wg_s.at[s], sem.at[s, 0]),
                pltpu.make_async_copy(wu_hbm.at[e], wu_s.at[s], sem.at[s, 1]),
                pltpu.make_async_copy(wd_hbm.at[e], wd_s.at[s], sem.at[s, 2]))

    @pl.when(i == 0)
    def _():
        for copy in weight_copies(expert, slot):
            copy.start()

    @pl.when(jnp.logical_or(i == 0, expert != te_ref[jnp.maximum(i - 1, 0)]))
    def _():
        for copy in weight_copies(expert, slot):
            copy.wait()
        nxt = nx_ref[i]

        @pl.when(nxt >= 0)
        def _():
            for copy in weight_copies(nxt, 1 - slot):
                copy.start()

    def mlp(n_sub):
        wg, wu, wd = wg_s.at[slot], wu_s.at[slot], wd_s.at[slot]
        subs = [slice(s * SUB_EXP, (s + 1) * SUB_EXP) for s in range(n_sub)]
        xin = []
        for r in subs:
            rid = r.start + lax.broadcasted_iota(I32, (SUB_EXP, PACKED), 0)
            lo, hi = _unpack_bf16_pairs(jnp.where(rid < n_valid, xs_ref[r, :], 0))
            xin.append((lo.astype(BF16), hi.astype(BF16)))
        ab = [(jnp.dot(lo, wg[:PACKED, :], preferred_element_type=F32)
               + jnp.dot(hi, wg[PACKED:, :], preferred_element_type=F32),
               jnp.dot(lo, wu[:PACKED, :], preferred_element_type=F32)
               + jnp.dot(hi, wu[PACKED:, :], preferred_element_type=F32)) for lo, hi in xin]
        for r, (ai, bi) in zip(subs, ab):
            hm = (ai * jax.nn.sigmoid(ai) * bi).astype(BF16)
            o_ref[r, :] = _pack_bf16_pairs(jnp.dot(hm, wd[...], preferred_element_type=F32))
        if n_sub * SUB_EXP < TM_EXP:
            o_ref[n_sub * SUB_EXP:, :] = jnp.zeros((TM_EXP - n_sub * SUB_EXP, PACKED), I32)

    n_subs = TM_EXP // SUB_EXP
    for n_sub in range(n_subs + 1):
        lo_rows = (n_sub - 1) * SUB_EXP if n_sub else -1
        in_range = jnp.logical_and(n_valid > lo_rows, n_valid <= n_sub * SUB_EXP)
        pl.when(in_range)(functools.partial(mlp, n_sub))


def _experts(tile_expert, next_expert, tile_slot, tile_src, tile_valid, xs, wg, wu, wd):
    rows = xs.shape[0]
    hbm = pl.BlockSpec(memory_space=pl.ANY)
    grid_spec = pltpu.PrefetchScalarGridSpec(
        num_scalar_prefetch=5,
        grid=(rows // TM_EXP,),
        in_specs=[pl.BlockSpec((TM_EXP, PACKED), lambda i, te, nx, sl, ts, tv: (ts[i], 0)),
                  hbm, hbm, hbm],
        out_specs=pl.BlockSpec((TM_EXP, PACKED), lambda i, te, nx, sl, ts, tv: (i, 0)),
        scratch_shapes=[
            pltpu.VMEM((2, D_MODEL, D_FF), BF16), pltpu.VMEM((2, D_MODEL, D_FF), BF16),
            pltpu.VMEM((2, D_FF, D_MODEL), BF16),
            pltpu.SemaphoreType.DMA((2, 3)),
        ],
    )
    return pl.pallas_call(
        _expert_kernel,
        name="experts",
        grid_spec=grid_spec,
        out_shape=jax.ShapeDtypeStruct((rows, PACKED), I32),
        compiler_params=pltpu.CompilerParams(
            dimension_semantics=("arbitrary",), vmem_limit_bytes=VMEM_LIMIT),
    )(tile_expert, next_expert, tile_slot, tile_src, tile_valid, xs, wg, wu, wd)


def _combine_kernel(y1_ref, y2_ref, route_ref, xmid_ref, g_ref, *rest):
    o_ref = rest[-1]
    route = route_ref[...]
    lane = lax.broadcasted_iota(I32, route.shape, 1)
    w1 = jnp.sum(jnp.where(lane == 2, route, 0.0), axis=-1, keepdims=True)
    w2 = jnp.sum(jnp.where(lane == 3, route, 0.0), axis=-1, keepdims=True)
    lo1, hi1 = _unpack_bf16_pairs(y1_ref[...])
    lo2, hi2 = _unpack_bf16_pairs(y2_ref[...])
    lox, hix = _unpack_bf16_pairs(xmid_ref[...])
    x_out = jnp.concatenate([lox + (lo1 * w1 + lo2 * w2), hix + (hi1 * w1 + hi2 * w2)], axis=1)
    o_ref[...] = _rms(x_out, g_ref[...])


def _combine(yg, route, xmid, g, chunk, t_total, out_prev):
    t = xmid.shape[0]
    n_tiles = t // TM_CMB
    first = chunk * n_tiles
    in_specs = [
        pl.BlockSpec((TM_CMB, PACKED), lambda i: (i, 0)),
        pl.BlockSpec((TM_CMB, PACKED), lambda i: (n_tiles + i, 0)),
        pl.BlockSpec((TM_CMB, LANES), lambda i: (i, 0)),
        pl.BlockSpec((TM_CMB, PACKED), lambda i: (i, 0)),
        pl.BlockSpec((1, D_MODEL), lambda i: (0, 0)),
    ]
    args = [yg, yg, route, xmid, g]
    aliases = {}
    if out_prev is not None:
        in_specs.append(pl.BlockSpec(memory_space=pl.ANY))
        aliases = {len(args): 0}
        args.append(out_prev)
    return pl.pallas_call(
        _combine_kernel,
        name="combine",
        grid=(n_tiles,),
        in_specs=in_specs,
        out_specs=pl.BlockSpec((TM_CMB, D_MODEL), lambda i: (first + i, 0)),
        out_shape=jax.ShapeDtypeStruct((t_total, D_MODEL), F32),
        input_output_aliases=aliases,
        compiler_params=pltpu.CompilerParams(
            dimension_semantics=("arbitrary",), vmem_limit_bytes=VMEM_LIMIT),
    )(*args)


def _split_bf16(w):
    hi = w.astype(BF16)
    lo = (w - hi.astype(F32)).astype(BF16)
    return hi, lo


def kernel(x, norm_mix, w_in, conv_w, conv_b, w_a_out, sinks, w_b_out, w_o, norm_ffn, w_group,
           b_group, w_expert, b_expert, w_gate, w_up, w_down, norm_final):
    bsz, seq, d = x.shape
    t = bsz * seq
    assert d == D_MODEL and seq % TM_PROJ == 0 and seq % TQ_ATTN == 0
    xf = x.reshape(t, d)
    row = lambda v: v.reshape(1, -1)

    w_b, w_c, w_u, w_q, w_k, w_v, w_ga, w_gb = jnp.split(w_in, REF_SPLITS, axis=1)
    w_cu = jnp.concatenate([w_c, w_u], axis=1).astype(BF16)
    w_bg = jnp.concatenate([w_b, w_ga], axis=1).astype(BF16)
    w_rest = jnp.concatenate([w_gb, w_q, w_k, w_v], axis=1).astype(BF16)

    za, proj, w_gate, w_up, w_down = _inproj(xf, row(norm_mix), w_cu, w_bg, w_rest, conv_w,
                                             row(conv_b), w_a_out.astype(BF16), seq,
                                             w_gate, w_up, w_down)
    zb = _attention(proj, sinks, w_b_out.astype(BF16), seq)

    pad = LANES - N_GROUPS - N_EXPERTS
    w_r = jnp.concatenate([w_expert, w_group, jnp.zeros((d, pad), F32)], axis=1)
    b_r = jnp.concatenate([b_expert, b_group, jnp.zeros((pad,), F32)]).reshape(1, LANES)
    wr = jnp.concatenate(_split_bf16(w_r), axis=1)
    wo = w_o.astype(BF16)

    t_chunk = t // MOE_CHUNKS
    assert t == t_chunk * MOE_CHUNKS and all(
        t_chunk % step == 0 for step in (TM_MIX, TM_CNT * POS_TILES, TM_CMB, SC_WORKERS * SC_CHUNK))
    out = None
    for chunk in range(MOE_CHUNKS):
        xmid, h2, route, cnt = _mix(zb, za, xf, wo, row(norm_ffn), wr, b_r, chunk, t_chunk)
        out = _moe_chunk(xmid, h2, route, cnt, w_gate, w_up, w_down, row(norm_final), chunk, t, out)
    return out.reshape(bsz, seq, d)


def _moe_chunk(xmid, h2, route, cnt, w_gate, w_up, w_down, g_final, chunk, t_total, out_prev):
    t = xmid.shape[0]
    n_tiles = t // TM_CNT
    cnt = cnt.reshape(n_tiles, 8, LANES)[:, 0, :N_EXPERTS].astype(I32)
    totals = jnp.sum(cnt, axis=0)
    tiles_e = (totals + TM_EXP - 1) // TM_EXP
    tile_end = jnp.cumsum(tiles_e)
    offset = (tile_end - tiles_e) * TM_EXP
    base = offset[None, :] + jnp.cumsum(cnt, axis=0) - cnt
    base = jnp.pad(base, ((0, 0), (0, LANES - N_EXPERTS))).astype(F32).reshape(n_tiles, 1, LANES)
    rows = t * TOP_K + N_EXPERTS * TM_EXP
    n_active = tile_end[-1]
    tile_id = jnp.arange(rows // TM_EXP, dtype=I32)
    tile_src = jnp.minimum(tile_id, n_active - 1)
    tile_expert = jnp.sum((tile_src[:, None] >= tile_end[None, :]).astype(I32), axis=1)
    tile_expert = jnp.minimum(tile_expert, N_EXPERTS - 1)
    row_in_expert = (tile_id - (tile_end - tiles_e)[tile_expert]) * TM_EXP
    tile_valid = jnp.clip(totals[tile_expert] - row_in_expert, 0, TM_EXP)
    tile_valid = jnp.where(tile_id < n_active, tile_valid, 0).astype(I32)
    after = tile_end[tile_expert]
    next_expert = jnp.where(after < n_active, tile_expert[jnp.minimum(after, n_active - 1)], -1)
    first_of_expert = jnp.concatenate(
        [jnp.ones((1,), I32), (tile_expert[1:] != tile_expert[:-1]).astype(I32)])
    tile_slot = (jnp.cumsum(first_of_expert) - 1) % 2

    pos = _positions(route, base)
    xs = _dispatch(pos, h2, rows)
    ys = _experts(tile_expert.astype(I32), next_expert.astype(I32), tile_slot.astype(I32),
                  tile_src.astype(I32), tile_valid, xs, w_gate, w_up, w_down)
    yg = _gather_rows(ys, pos.reshape(TOP_K * t))
    return _combine(yg, route, xmid, g_final, chunk, t_total, out_prev)
```

```python
import functools
import math

import jax
import jax.numpy as jnp
from jax import lax
from jax.experimental import pallas as pl
from jax.experimental.pallas import tpu as pltpu
from jax.experimental.pallas import tpu_sc as plsc

F32 = jnp.float32
BF16 = jnp.bfloat16
I32 = jnp.int32

D_MODEL = 1024
HEAD_DIM = 64
N_HEADS = 16
N_KV_HEADS = 4
GROUP = N_HEADS // N_KV_HEADS
KV_WIDTH = N_KV_HEADS * HEAD_DIM
WINDOW = 128
N_GROUPS = 4
EXPERTS_PER_GROUP = 8
N_EXPERTS = N_GROUPS * EXPERTS_PER_GROUP
TOP_K = 2
D_FF = 512
EPS = 1e-6
LANES = 128

REF_SPLITS = (1024, 2048, 3072, 4096, 4352, 4608, 5632)
REST_COLS = 2 * D_MODEL + 2 * KV_WIDTH
COL_GB, COL_Q = 0, 1
COL_K, COL_V = 2 * D_MODEL // KV_WIDTH, 2 * D_MODEL // KV_WIDTH + 1

TM_PROJ = 512
TQ_ATTN = 1024
TM_MIX = 1024
SUB_MIX = 256
TM_CNT = 512
POS_TILES = 8
MOE_CHUNKS = 2
TM_EXP = 512
SUB_EXP = 256
TM_CMB = 1024
HALO_ROWS = 8
VMEM_LIMIT = 56 * 1024 * 1024
PACKED = D_MODEL // 2

SC_CORES = 2
SC_SUBCORES = 16
SC_WORKERS = SC_CORES * SC_SUBCORES
SC_CHUNK = 64


def _rms(x, g):
    r = lax.rsqrt(jnp.mean(x * x, axis=-1, keepdims=True) + EPS)
    return (x * r) * g


def _pack_bf16_pairs(x):
    n = x.shape[1] // 2
    lo = lax.bitcast_convert_type(x[:, :n].astype(BF16).astype(F32), I32)
    hi = lax.bitcast_convert_type(x[:, n:].astype(BF16).astype(F32), I32)
    return (hi & jnp.int32(-65536)) | lax.shift_right_logical(lo, 16)


def _unpack_bf16_pairs(p):
    lo = lax.bitcast_convert_type(lax.shift_left(p, 16), F32)
    hi = lax.bitcast_convert_type(p & jnp.int32(-65536), F32)
    return lo, hi


def _inproj_kernel(x_ref, g_ref, wcu_ref, wbg_ref, wrest_ref, cw_ref, cb_ref, wa_ref,
                   eg_ref, eu_ref, ed_ref,
                   za_ref, proj_ref, eg_out, eu_out, ed_out, halo_ref, *, tiles_per_seq):
    i = pl.program_id(0)
    eg_out[...] = eg_ref[...].astype(BF16)
    eu_out[...] = eu_ref[...].astype(BF16)
    ed_out[...] = ed_ref[...].astype(BF16)
    h = _rms(x_ref[...], g_ref[...]).astype(BF16)
    pcu = jnp.dot(h, wcu_ref[...], preferred_element_type=F32)
    pbg = jnp.dot(h, wbg_ref[...], preferred_element_type=F32)
    proj_ref[...] = jnp.dot(h, wrest_ref[...], preferred_element_type=F32).astype(BF16)
    cu = pcu[:, :D_MODEL] * pcu[:, D_MODEL:]
    first = (i % tiles_per_seq) == 0
    hist = jnp.where(first, 0.0, halo_ref[...])
    prev1 = hist[HALO_ROWS - 1:HALO_ROWS]
    prev2 = hist[HALO_ROWS - 2:HALO_ROWS - 1]
    halo_ref[...] = cu[TM_PROJ - HALO_ROWS:, :]
    row = lax.broadcasted_iota(I32, cu.shape, 0)
    cu1 = jnp.where(row == 0, prev1, pltpu.roll(cu, 1, 0))
    cu2 = jnp.where(row == 0, prev2, jnp.where(row == 1, prev1, pltpu.roll(cu, 2, 0)))
    cw = cw_ref[...]
    y = cw[0:1] * cu2 + cw[1:2] * cu1 + cw[2:3] * cu + cb_ref[...]
    ya = (pbg[:, :D_MODEL] * y).astype(BF16)
    z = jnp.dot(ya, wa_ref[...], preferred_element_type=F32)
    za_ref[...] = (jax.nn.sigmoid(pbg[:, D_MODEL:]) * z).astype(BF16)


def _inproj(xf, g, w_cu, w_bg, w_rest, conv_w, conv_b, wa, seq, w_gate, w_up, w_down):
    t = xf.shape[0]
    n_tiles = t // TM_PROJ
    const = lambda shape: pl.BlockSpec(shape, lambda i: (0, 0), pipeline_mode=pl.Buffered(1))
    slabs = [w.reshape(-1, w.shape[-1]) for w in (w_gate, w_up, w_down)]
    slab_rows = [s.shape[0] // n_tiles for s in slabs]
    assert all(s.shape[0] == r * n_tiles and r % 16 == 0 for s, r in zip(slabs, slab_rows))
    slab_specs = [pl.BlockSpec((r, s.shape[1]), lambda i: (i, 0)) for s, r in zip(slabs, slab_rows)]
    outs = pl.pallas_call(
        functools.partial(_inproj_kernel, tiles_per_seq=seq // TM_PROJ),
        name="inproj_conv",
        grid=(n_tiles,),
        in_specs=[
            pl.BlockSpec((TM_PROJ, D_MODEL), lambda i: (i, 0)),
            const((1, D_MODEL)),
            const((D_MODEL, 2 * D_MODEL)), const((D_MODEL, 2 * D_MODEL)), const((D_MODEL, REST_COLS)),
            const((3, D_MODEL)), const((1, D_MODEL)), const((D_MODEL, D_MODEL)),
        ] + slab_specs,
        out_specs=[pl.BlockSpec((TM_PROJ, D_MODEL), lambda i: (i, 0)),
                   pl.BlockSpec((TM_PROJ, REST_COLS), lambda i: (i, 0))] + slab_specs,
        out_shape=[jax.ShapeDtypeStruct((t, D_MODEL), BF16),
                   jax.ShapeDtypeStruct((t, REST_COLS), BF16)]
                  + [jax.ShapeDtypeStruct(s.shape, BF16) for s in slabs],
        scratch_shapes=[pltpu.VMEM((HALO_ROWS, D_MODEL), F32)],
        compiler_params=pltpu.CompilerParams(
            dimension_semantics=("arbitrary",), vmem_limit_bytes=VMEM_LIMIT),
    )(xf, g, w_cu, w_bg, w_rest, conv_w, conv_b, wa, *slabs)
    za, proj, eg, eu, ed = outs
    return za, proj, eg.reshape(w_gate.shape), eu.reshape(w_up.shape), ed.reshape(w_down.shape)


def _attn_kernel(sink_ref, q_ref, k_ref, v_ref, kp_ref, vp_ref, gb_ref, za_ref, wb_ref, o_ref,
                 *, tiles_per_seq):
    first_tile = (pl.program_id(0) % tiles_per_seq) == 0
    ks = lax.broadcasted_iota(I32, (WINDOW, WINDOW), 0)
    qq = lax.broadcasted_iota(I32, (WINDOW, WINDOW), 1)
    own = ks <= qq
    dist = jnp.where(own, qq - ks, qq - ks + WINDOW).astype(F32)
    visible0 = jnp.logical_or(own, jnp.logical_not(first_tile))
    log2e = math.log2(math.e)
    c_scale = log2e / math.sqrt(HEAD_DIM)
    nt = (((1,), (1,)), ((), ()))
    zk = jnp.zeros((2 * WINDOW, HEAD_DIM), BF16)

    def transposed(v_blk):
        return jnp.transpose(v_blk.astype(F32)).astype(BF16)

    def project(rows_p, attn_blk):
        yb = jnp.dot(attn_blk, wb_ref[...], preferred_element_type=F32)
        zb = jax.nn.sigmoid(gb_ref[rows_p, :].astype(F32)) * yb
        o_ref[rows_p, :] = (za_ref[rows_p, :].astype(F32) + zb).astype(BF16)

    pending = None
    prev_k = kp_ref[...]
    prev_vt = transposed(vp_ref[...])
    for sb in range(TQ_ATTN // WINDOW):
        rows = slice(sb * WINDOW, (sb + 1) * WINDOW)
        cur_k = k_ref[rows, :]
        cur_vt = transposed(v_ref[rows, :])
        scores, vcats = [], []
        for kh in range(N_KV_HEADS):
            cols = slice(kh * HEAD_DIM, (kh + 1) * HEAD_DIM)
            kcat = jnp.concatenate([prev_k[:, cols], cur_k[:, cols]], axis=0)
            vcats.append(jnp.concatenate([prev_vt[cols, :], cur_vt[cols, :]], axis=1))
            qg = jnp.concatenate([q_ref[rows, (2 * kh) * LANES:(2 * kh + 1) * LANES],
                                  q_ref[rows, (2 * kh + 1) * LANES:(2 * kh + 2) * LANES]], axis=0)
            k_pad = jnp.concatenate([jnp.concatenate([kcat, zk], axis=1),
                                     jnp.concatenate([zk, kcat], axis=1)], axis=0)
            scores.append(lax.dot_general(k_pad, qg, nt, preferred_element_type=F32))
        if pending is not None:
            project(*pending)
        probs, rdens = [], []
        for kh in range(N_KV_HEADS):
            for pos in range(2):
                pr, rd = [], []
                for half in range(2):
                    h = kh * GROUP + 2 * half + pos
                    slope = 2.0 ** (-8.0 * (h + 1) / N_HEADS)
                    qcols = slice(half * WINDOW, (half + 1) * WINDOW)
                    krow = pos * 2 * WINDOW
                    st = scores[kh]
                    s = (jnp.where(own, st[krow + WINDOW:krow + 2 * WINDOW, qcols],
                                   st[krow:krow + WINDOW, qcols]) * c_scale
                         - (slope * log2e) * dist)
                    if sb == 0:
                        s = jnp.where(visible0, s, -jnp.inf)
                    m = jnp.max(s, axis=0, keepdims=True)
                    p = jnp.exp2(s - m)
                    den = jnp.sum(p, axis=0, keepdims=True) + jnp.exp2(sink_ref[h] * log2e - m)
                    rd.append(1.0 / den)
                    pr.append(jnp.concatenate(
                        [jnp.where(own, 0.0, p).astype(BF16), jnp.where(own, p, 0.0).astype(BF16)],
                        axis=0))
                probs.append(jnp.concatenate(pr, axis=1))
                rdens.append(jnp.concatenate(rd, axis=1))
        out_t = [None] * N_HEADS
        for kh in range(N_KV_HEADS):
            for pos in range(2):
                o2 = jnp.dot(vcats[kh], probs[2 * kh + pos], preferred_element_type=F32)
                o2 = o2 * rdens[2 * kh + pos]
                out_t[kh * GROUP + pos] = o2[:, :WINDOW]
                out_t[kh * GROUP + 2 + pos] = o2[:, WINDOW:]
        pending = (rows, jnp.transpose(jnp.concatenate(out_t, axis=0)).astype(BF16))
        prev_k, prev_vt = cur_k, cur_vt
    project(*pending)


def _attention(proj, za, sinks, wb, seq):
    t = proj.shape[0]
    sub = TQ_ATTN // WINDOW
    return pl.pallas_call(
        functools.partial(_attn_kernel, tiles_per_seq=seq // TQ_ATTN),
        name="swattn",
        grid=(t // TQ_ATTN,),
        in_specs=[
            pl.BlockSpec(memory_space=pltpu.SMEM),
            pl.BlockSpec((TQ_ATTN, D_MODEL), lambda i: (i, COL_Q)),
            pl.BlockSpec((TQ_ATTN, KV_WIDTH), lambda i: (i, COL_K)),
            pl.BlockSpec((TQ_ATTN, KV_WIDTH), lambda i: (i, COL_V)),
            pl.BlockSpec((WINDOW, KV_WIDTH), lambda i: (jnp.maximum(i * sub - 1, 0), COL_K)),
            pl.BlockSpec((WINDOW, KV_WIDTH), lambda i: (jnp.maximum(i * sub - 1, 0), COL_V)),
            pl.BlockSpec((TQ_ATTN, D_MODEL), lambda i: (i, COL_GB)),
            pl.BlockSpec((TQ_ATTN, D_MODEL), lambda i: (i, 0)),
            pl.BlockSpec((D_MODEL, D_MODEL), lambda i: (0, 0), pipeline_mode=pl.Buffered(1)),
        ],
        out_specs=pl.BlockSpec((TQ_ATTN, D_MODEL), lambda i: (i, 0)),
        out_shape=jax.ShapeDtypeStruct((t, D_MODEL), BF16),
        compiler_params=pltpu.CompilerParams(
            dimension_semantics=("arbitrary",), vmem_limit_bytes=VMEM_LIMIT),
    )(sinks, proj, proj, proj, proj, proj, proj, za, wb)


def _mix_kernel(merged_ref, x_ref, wo_ref, g_ref, wr_ref, br_ref,
                xmid_ref, h_ref, route_ref, cnt_ref):
    subs = [slice(s * SUB_MIX, (s + 1) * SUB_MIX) for s in range(TM_MIX // SUB_MIX)]
    xm = [x_ref[r, :] + jnp.dot(merged_ref[r, :], wo_ref[...], preferred_element_type=F32)
          for r in subs]
    hs = []
    for r, v in zip(subs, xm):
        xmid_ref[r, :] = _pack_bf16_pairs(v)
        h = _rms(v, g_ref[...])
        h_ref[r, :] = _pack_bf16_pairs(h)
        hs.append(h)
    wr = wr_ref[...]
    logits = []
    for h in hs:
        h_hi = h.astype(BF16)
        h_lo = (h - h_hi.astype(F32)).astype(BF16)
        both = jnp.dot(h_hi, wr, preferred_element_type=F32)
        logits.append(both[:, :LANES] + both[:, LANES:]
                      + jnp.dot(h_lo, wr[:, :LANES], preferred_element_type=F32) + br_ref[...])
    subs_per_cnt = TM_CNT // SUB_MIX
    cnts = [jnp.zeros((8, LANES), F32) for _ in range(TM_MIX // TM_CNT)]
    for s, (r, lg) in enumerate(zip(subs, logits)):
        route, cnt = _route(lg)
        route_ref[r, :] = route
        cnts[s // subs_per_cnt] = cnts[s // subs_per_cnt] + cnt
    for c, cnt in enumerate(cnts):
        cnt_ref[c * 8:(c + 1) * 8, :] = cnt


def _route(logits):
    n = logits.shape[0]
    lt = jnp.transpose(logits)
    sub = lax.broadcasted_iota(I32, (EXPERTS_PER_GROUP, n), 0)
    neg = -jnp.inf
    gl = jnp.where(sub < N_GROUPS, lt[N_EXPERTS:N_EXPERTS + EXPERTS_PER_GROUP], neg)
    gmax = jnp.max(gl, axis=0, keepdims=True)
    g_idx = jnp.min(jnp.where(gl == gmax, sub, EXPERTS_PER_GROUP), axis=0, keepdims=True)
    p_g = 1.0 / jnp.sum(jnp.exp(gl - gmax), axis=0, keepdims=True)
    v1 = v2 = i1 = i2 = None
    for g in range(N_GROUPS):
        eg = lt[g * EXPERTS_PER_GROUP:(g + 1) * EXPERTS_PER_GROUP]
        a1 = jnp.max(eg, axis=0, keepdims=True)
        j1 = jnp.min(jnp.where(eg == a1, sub, EXPERTS_PER_GROUP), axis=0, keepdims=True)
        eg2 = jnp.where(sub == j1, neg, eg)
        a2 = jnp.max(eg2, axis=0, keepdims=True)
        j2 = jnp.min(jnp.where(eg2 == a2, sub, EXPERTS_PER_GROUP), axis=0, keepdims=True)
        if g == 0:
            v1, v2, i1, i2 = a1, a2, j1, j2
        else:
            chosen = g_idx == g
            v1, v2 = jnp.where(chosen, a1, v1), jnp.where(chosen, a2, v2)
            i1, i2 = jnp.where(chosen, j1, i1), jnp.where(chosen, j2, i2)
    e21 = jnp.exp(v2 - v1)
    w1 = p_g / (1.0 + e21)
    w2 = p_g * e21 / (1.0 + e21)
    e1 = g_idx * EXPERTS_PER_GROUP + i1
    e2 = g_idx * EXPERTS_PER_GROUP + i2
    rows8 = jnp.where(sub == 0, e1.astype(F32),
                      jnp.where(sub == 1, e2.astype(F32),
                                jnp.where(sub == 2, w1, jnp.where(sub == 3, w2, 0.0))))
    route_t = jnp.concatenate([rows8, jnp.zeros((LANES - EXPERTS_PER_GROUP, n), F32)], axis=0)
    expert_row = lax.broadcasted_iota(I32, (LANES, n), 0)
    onehot_t = ((expert_row == e1) | (expert_row == e2)).astype(BF16)
    cnt = lax.dot_general(jnp.ones((8, n), BF16), onehot_t, (((1,), (1,)), ((), ())),
                          preferred_element_type=F32)
    return jnp.transpose(route_t), cnt


def _mix(merged, xf, wo, g, wr, br, chunk, t):
    n_tiles = t // TM_MIX
    first = chunk * n_tiles
    cnt_rows = TM_MIX // TM_CNT * 8
    full = lambda shape: pl.BlockSpec(shape, lambda i: (0, 0), pipeline_mode=pl.Buffered(1))
    tile = lambda w=D_MODEL: pl.BlockSpec((TM_MIX, w), lambda i: (i, 0))
    src = lambda c=0: pl.BlockSpec((TM_MIX, D_MODEL), lambda i: (first + i, c))
    return pl.pallas_call(
        _mix_kernel,
        name="merge_router",
        grid=(n_tiles,),
        in_specs=[
            src(), src(),
            full((D_MODEL, D_MODEL)), full((1, D_MODEL)),
            full((D_MODEL, 2 * LANES)), full((1, LANES)),
        ],
        out_specs=[tile(PACKED), tile(PACKED), tile(LANES),
                   pl.BlockSpec((cnt_rows, LANES), lambda i: (i, 0))],
        out_shape=[
            jax.ShapeDtypeStruct((t, PACKED), I32),
            jax.ShapeDtypeStruct((t, PACKED), I32),
            jax.ShapeDtypeStruct((t, LANES), F32),
            jax.ShapeDtypeStruct((n_tiles * cnt_rows, LANES), F32),
        ],
        compiler_params=pltpu.CompilerParams(
            dimension_semantics=("arbitrary",), vmem_limit_bytes=VMEM_LIMIT),
    )(merged, xf, wo, g, wr, br)


def _pos_kernel(route_ref, base_ref, pos_ref):
    lane = lax.broadcasted_iota(I32, (TM_CNT, LANES), 1)
    r = lax.broadcasted_iota(I32, (TM_CNT, TM_CNT), 0)
    c = lax.broadcasted_iota(I32, (TM_CNT, TM_CNT), 1)
    lower = (c < r).astype(BF16)
    subs = [slice(s * TM_CNT, (s + 1) * TM_CNT) for s in range(POS_TILES)]
    routes = [route_ref[rs, :] for rs in subs]
    e1 = [jnp.sum(jnp.where(lane == 0, rt, 0.0), axis=-1, keepdims=True).astype(I32) for rt in routes]
    e2 = [jnp.sum(jnp.where(lane == 1, rt, 0.0), axis=-1, keepdims=True).astype(I32) for rt in routes]
    onehot = [((lane == a) | (lane == b)).astype(BF16) for a, b in zip(e1, e2)]
    before = [jnp.dot(lower, oh, preferred_element_type=F32) + base_ref[s]
              for s, oh in enumerate(onehot)]
    for s, rs in enumerate(subs):
        p1 = jnp.sum(jnp.where(lane == e1[s], before[s], 0.0), axis=-1, keepdims=True)
        p2 = jnp.sum(jnp.where(lane == e2[s], before[s], 0.0), axis=-1, keepdims=True)
        packed = jnp.where(lane == 0, p1, jnp.where(lane == 1, p2, 0.0))
        pos_ref[:, rs] = jnp.transpose(packed)[0:TOP_K, :].astype(I32)


def _positions(route, base):
    t = route.shape[0]
    n_steps = t // (TM_CNT * POS_TILES)
    return pl.pallas_call(
        _pos_kernel,
        name="positions",
        grid=(n_steps,),
        in_specs=[
            pl.BlockSpec((TM_CNT * POS_TILES, LANES), lambda i: (i, 0)),
            pl.BlockSpec((POS_TILES, 1, LANES), lambda i: (i, 0, 0)),
        ],
        out_specs=pl.BlockSpec((TOP_K, TM_CNT * POS_TILES), lambda i: (0, i)),
        out_shape=jax.ShapeDtypeStruct((TOP_K, t), I32),
        compiler_params=pltpu.CompilerParams(dimension_semantics=("arbitrary",)),
    )(route, base)


def _sc_mesh():
    return plsc.VectorSubcoreMesh(core_axis_name="c", subcore_axis_name="s",
                                  num_cores=SC_CORES, num_subcores=SC_SUBCORES)


def _sc_worker():
    return lax.axis_index("s") * SC_CORES + lax.axis_index("c")


def _dispatch(pos, hp, rows):
    t = hp.shape[0]
    per_w = t // SC_WORKERS
    n_ch = per_w // SC_CHUNK
    pos4 = pos.reshape(TOP_K, SC_WORKERS, n_ch, SC_CHUNK)

    @functools.partial(
        pl.kernel, mesh=_sc_mesh(),
        out_type=jax.ShapeDtypeStruct((rows, PACKED), I32),
        scratch_types=[pltpu.VMEM((TOP_K, n_ch, SC_CHUNK), I32),
                       pltpu.VMEM((SC_CHUNK, PACKED), I32)])
    def scatter(hp_hbm, pos_hbm, xs_hbm, idx_v, rows_v):
        wid = _sc_worker()
        for k in range(TOP_K):
            pltpu.sync_copy(pos_hbm.at[k, wid], idx_v.at[k])

        def body(c, carry):
            start = pl.multiple_of(wid * per_w + c * SC_CHUNK, SC_CHUNK)
            pltpu.sync_copy(hp_hbm.at[pl.ds(start, SC_CHUNK)], rows_v)
            for k in range(TOP_K):
                pltpu.sync_copy(rows_v, xs_hbm.at[idx_v.at[k, c]])
            return carry

        lax.fori_loop(0, n_ch, body, 0)

    return scatter(hp, pos4)


def _gather_rows(table, idx):
    n = idx.shape[0]
    per_w = n // SC_WORKERS
    n_ch = per_w // SC_CHUNK
    idx3 = idx.reshape(SC_WORKERS, n_ch, SC_CHUNK)

    @functools.partial(
        pl.kernel, mesh=_sc_mesh(),
        out_type=jax.ShapeDtypeStruct((n, PACKED), I32),
        scratch_types=[pltpu.VMEM((n_ch, SC_CHUNK), I32),
                       pltpu.VMEM((SC_CHUNK, PACKED), I32)])
    def gather(table_hbm, idx_hbm, out_hbm, idx_v, rows_v):
        wid = _sc_worker()
        pltpu.sync_copy(idx_hbm.at[wid], idx_v)

        def body(c, carry):
            start = pl.multiple_of(wid * per_w + c * SC_CHUNK, SC_CHUNK)
            pltpu.sync_copy(table_hbm.at[idx_v.at[c]], rows_v)
            pltpu.sync_copy(rows_v, out_hbm.at[pl.ds(start, SC_CHUNK)])
            return carry

        lax.fori_loop(0, n_ch, body, 0)

    return gather(table, idx3)


def _expert_kernel(te_ref, nx_ref, sl_ref, ts_ref, tv_ref, xs_ref, wg_hbm, wu_hbm, wd_hbm, o_ref,
                   wg_s, wu_s, wd_s, sem):
    del ts_ref
    i = pl.program_id(0)
    n_valid = tv_ref[i]
    expert = te_ref[i]
    slot = sl_ref[i]

    def weight_copies(e, s):
        return (pltpu.make_async_copy(wg_hbm.at[e], wg_s.at[s], sem.at[s, 0]),
                pltpu.make_async_copy(wu_hbm.at[e], wu_s.at[s], sem.at[s, 1]),
                pltpu.make_async_copy(wd_hbm.at[e], wd_s.at[s], sem.at[s, 2]))

    @pl.when(i == 0)
    def _():
        for copy in weight_copies(expert, slot):
            copy.start()

    @pl.when(jnp.logical_or(i == 0, expert != te_ref[jnp.maximum(i - 1, 0)]))
    def _():
        for copy in weight_copies(expert, slot):
            copy.wait()
        nxt = nx_ref[i]

        @pl.when(nxt >= 0)
        def _():
            for copy in weight_copies(nxt, 1 - slot):
                copy.start()

    def mlp(n_sub):
        wg, wu, wd = wg_s.at[slot], wu_s.at[slot], wd_s.at[slot]
        subs = [slice(s * SUB_EXP, (s + 1) * SUB_EXP) for s in range(n_sub)]
        xin = []
        for r in subs:
            rid = r.start + lax.broadcasted_iota(I32, (SUB_EXP, PACKED), 0)
            lo, hi = _unpack_bf16_pairs(jnp.where(rid < n_valid, xs_ref[r, :], 0))
            xin.append((lo.astype(BF16), hi.astype(BF16)))
        ab = [(jnp.dot(lo, wg[:PACKED, :], preferred_element_type=F32)
               + jnp.dot(hi, wg[PACKED:, :], preferred_element_type=F32),
               jnp.dot(lo, wu[:PACKED, :], preferred_element_type=F32)
               + jnp.dot(hi, wu[PACKED:, :], preferred_element_type=F32)) for lo, hi in xin]
        for r, (ai, bi) in zip(subs, ab):
            hm = (ai * jax.nn.sigmoid(ai) * bi).astype(BF16)
            o_ref[r, :] = _pack_bf16_pairs(jnp.dot(hm, wd[...], preferred_element_type=F32))
        if n_sub * SUB_EXP < TM_EXP:
            o_ref[n_sub * SUB_EXP:, :] = jnp.zeros((TM_EXP - n_sub * SUB_EXP, PACKED), I32)

    n_subs = TM_EXP // SUB_EXP
    for n_sub in range(n_subs + 1):
        lo_rows = (n_sub - 1) * SUB_EXP if n_sub else -1
        in_range = jnp.logical_and(n_valid > lo_rows, n_valid <= n_sub * SUB_EXP)
        pl.when(in_range)(functools.partial(mlp, n_sub))


def _experts(tile_expert, next_expert, tile_slot, tile_src, tile_valid, xs, wg, wu, wd):
    rows = xs.shape[0]
    hbm = pl.BlockSpec(memory_space=pl.ANY)
    grid_spec = pltpu.PrefetchScalarGridSpec(
        num_scalar_prefetch=5,
        grid=(rows // TM_EXP,),
        in_specs=[pl.BlockSpec((TM_EXP, PACKED), lambda i, te, nx, sl, ts, tv: (ts[i], 0)),
                  hbm, hbm, hbm],
        out_specs=pl.BlockSpec((TM_EXP, PACKED), lambda i, te, nx, sl, ts, tv: (i, 0)),
        scratch_shapes=[
            pltpu.VMEM((2, D_MODEL, D_FF), BF16), pltpu.VMEM((2, D_MODEL, D_FF), BF16),
            pltpu.VMEM((2, D_FF, D_MODEL), BF16),
            pltpu.SemaphoreType.DMA((2, 3)),
        ],
    )
    return pl.pallas_call(
        _expert_kernel,
        name="experts",
        grid_spec=grid_spec,
        out_shape=jax.ShapeDtypeStruct((rows, PACKED), I32),
        compiler_params=pltpu.CompilerParams(
            dimension_semantics=("arbitrary",), vmem_limit_bytes=VMEM_LIMIT),
    )(tile_expert, next_expert, tile_slot, tile_src, tile_valid, xs, wg, wu, wd)


def _combine_kernel(y1_ref, y2_ref, route_ref, xmid_ref, g_ref, *rest):
    o_ref = rest[-1]
    route = route_ref[...]
    lane = lax.broadcasted_iota(I32, route.shape, 1)
    w1 = jnp.sum(jnp.where(lane == 2, route, 0.0), axis=-1, keepdims=True)
    w2 = jnp.sum(jnp.where(lane == 3, route, 0.0), axis=-1, keepdims=True)
    lo1, hi1 = _unpack_bf16_pairs(y1_ref[...])
    lo2, hi2 = _unpack_bf16_pairs(y2_ref[...])
    lox, hix = _unpack_bf16_pairs(xmid_ref[...])
    x_out = jnp.concatenate([lox + (lo1 * w1 + lo2 * w2), hix + (hi1 * w1 + hi2 * w2)], axis=1)
    o_ref[...] = _rms(x_out, g_ref[...])


def _combine(yg, route, xmid, g, chunk, t_total, out_prev):
    t = xmid.shape[0]
    n_tiles = t // TM_CMB
    first = chunk * n_tiles
    in_specs = [
        pl.BlockSpec((TM_CMB, PACKED), lambda i: (i, 0)),
        pl.BlockSpec((TM_CMB, PACKED), lambda i: (n_tiles + i, 0)),
        pl.BlockSpec((TM_CMB, LANES), lambda i: (i, 0)),
        pl.BlockSpec((TM_CMB, PACKED), lambda i: (i, 0)),
        pl.BlockSpec((1, D_MODEL), lambda i: (0, 0)),
    ]
    args = [yg, yg, route, xmid, g]
    aliases = {}
    if out_prev is not None:
        in_specs.append(pl.BlockSpec(memory_space=pl.ANY))
        aliases = {len(args): 0}
        args.append(out_prev)
    return pl.pallas_call(
        _combine_kernel,
        name="combine",
        grid=(n_tiles,),
        in_specs=in_specs,
        out_specs=pl.BlockSpec((TM_CMB, D_MODEL), lambda i: (first + i, 0)),
        out_shape=jax.ShapeDtypeStruct((t_total, D_MODEL), F32),
        input_output_aliases=aliases,
        compiler_params=pltpu.CompilerParams(
            dimension_semantics=("arbitrary",), vmem_limit_bytes=VMEM_LIMIT),
    )(*args)


def _split_bf16(w):
    hi = w.astype(BF16)
    lo = (w - hi.astype(F32)).astype(BF16)
    return hi, lo


def kernel(x, norm_mix, w_in, conv_w, conv_b, w_a_out, sinks, w_b_out, w_o, norm_ffn, w_group,
           b_group, w_expert, b_expert, w_gate, w_up, w_down, norm_final):
    bsz, seq, d = x.shape
    t = bsz * seq
    assert d == D_MODEL and seq % TM_PROJ == 0 and seq % TQ_ATTN == 0
    xf = x.reshape(t, d)
    row = lambda v: v.reshape(1, -1)

    w_b, w_c, w_u, w_q, w_k, w_v, w_ga, w_gb = jnp.split(w_in, REF_SPLITS, axis=1)
    w_cu = jnp.concatenate([w_c, w_u], axis=1).astype(BF16)
    w_bg = jnp.concatenate([w_b, w_ga], axis=1).astype(BF16)
    w_rest = jnp.concatenate([w_gb, w_q, w_k, w_v], axis=1).astype(BF16)

    za, proj, w_gate, w_up, w_down = _inproj(xf, row(norm_mix), w_cu, w_bg, w_rest, conv_w,
                                             row(conv_b), w_a_out.astype(BF16), seq,
                                             w_gate, w_up, w_down)
    merged = _attention(proj, za, sinks, w_b_out.astype(BF16), seq)

    pad = LANES - N_GROUPS - N_EXPERTS
    w_r = jnp.concatenate([w_expert, w_group, jnp.zeros((d, pad), F32)], axis=1)
    b_r = jnp.concatenate([b_expert, b_group, jnp.zeros((pad,), F32)]).reshape(1, LANES)
    wr = jnp.concatenate(_split_bf16(w_r), axis=1)
    wo = w_o.astype(BF16)

    t_chunk = t // MOE_CHUNKS
    assert t == t_chunk * MOE_CHUNKS and all(
        t_chunk % step == 0 for step in (TM_MIX, TM_CNT * POS_TILES, TM_CMB, SC_WORKERS * SC_CHUNK))
    out = None
    for chunk in range(MOE_CHUNKS):
        xmid, h2, route, cnt = _mix(merged, xf, wo, row(norm_ffn), wr, b_r, chunk, t_chunk)
        out = _moe_chunk(xmid, h2, route, cnt, w_gate, w_up, w_down, row(norm_final), chunk, t, out)
    return out.reshape(bsz, seq, d)


def _moe_chunk(xmid, h2, route, cnt, w_gate, w_up, w_down, g_final, chunk, t_total, out_prev):
    t = xmid.shape[0]
    n_tiles = t // TM_CNT
    cnt = cnt.reshape(n_tiles, 8, LANES)[:, 0, :N_EXPERTS].astype(I32)
    totals = jnp.sum(cnt, axis=0)
    tiles_e = (totals + TM_EXP - 1) // TM_EXP
    tile_end = jnp.cumsum(tiles_e)
    offset = (tile_end - tiles_e) * TM_EXP
    base = offset[None, :] + jnp.cumsum(cnt, axis=0) - cnt
    base = jnp.pad(base, ((0, 0), (0, LANES - N_EXPERTS))).astype(F32).reshape(n_tiles, 1, LANES)
    rows = t * TOP_K + N_EXPERTS * TM_EXP
    n_active = tile_end[-1]
    tile_id = jnp.arange(rows // TM_EXP, dtype=I32)
    tile_src = jnp.minimum(tile_id, n_active - 1)
    tile_expert = jnp.sum((tile_src[:, None] >= tile_end[None, :]).astype(I32), axis=1)
    tile_expert = jnp.minimum(tile_expert, N_EXPERTS - 1)
    row_in_expert = (tile_id - (tile_end - tiles_e)[tile_expert]) * TM_EXP
    tile_valid = jnp.clip(totals[tile_expert] - row_in_expert, 0, TM_EXP)
    tile_valid = jnp.where(tile_id < n_active, tile_valid, 0).astype(I32)
    after = tile_end[tile_expert]
    next_expert = jnp.where(after < n_active, tile_expert[jnp.minimum(after, n_active - 1)], -1)
    first_of_expert = jnp.concatenate(
        [jnp.ones((1,), I32), (tile_expert[1:] != tile_expert[:-1]).astype(I32)])
    tile_slot = (jnp.cumsum(first_of_expert) - 1) % 2

    pos = _positions(route, base)
    xs = _dispatch(pos, h2, rows)
    ys = _experts(tile_expert.astype(I32), next_expert.astype(I32), tile_slot.astype(I32),
                  tile_src.astype(I32), tile_valid, xs, w_gate, w_up, w_down)
    yg = _gather_rows(ys, pos.reshape(TOP_K * t))
    return _combine(yg, route, xmid, g_final, chunk, t_total, out_prev)
```

```python
import functools
import math

import jax
import jax.numpy as jnp
from jax import lax
from jax.experimental import pallas as pl
from jax.experimental.pallas import tpu as pltpu
from jax.experimental.pallas import tpu_sc as plsc

F32 = jnp.float32
BF16 = jnp.bfloat16
I32 = jnp.int32

D_MODEL = 1024
HEAD_DIM = 64
N_HEADS = 16
N_KV_HEADS = 4
GROUP = N_HEADS // N_KV_HEADS
KV_WIDTH = N_KV_HEADS * HEAD_DIM
WINDOW = 128
N_GROUPS = 4
EXPERTS_PER_GROUP = 8
N_EXPERTS = N_GROUPS * EXPERTS_PER_GROUP
TOP_K = 2
D_FF = 512
EPS = 1e-6
LANES = 128

REF_SPLITS = (1024, 2048, 3072, 4096, 4352, 4608, 5632)
REST_COLS = 2 * D_MODEL + 2 * KV_WIDTH
COL_GB, COL_Q = 0, 1
COL_K, COL_V = 2 * D_MODEL // KV_WIDTH, 2 * D_MODEL // KV_WIDTH + 1

TM_PROJ = 512
TQ_ATTN = 1024
TM_MIX = 1024
SUB_MIX = 256
TM_CNT = 512
POS_TILES = 8
MOE_CHUNKS = 2
TM_EXP = 512
SUB_EXP = 256
TM_CMB = 1024
HALO_ROWS = 8
VMEM_LIMIT = 56 * 1024 * 1024
PACKED = D_MODEL // 2

SC_CORES = 2
SC_SUBCORES = 16
SC_WORKERS = SC_CORES * SC_SUBCORES
SC_CHUNK = 64


def _rms(x, g):
    r = lax.rsqrt(jnp.mean(x * x, axis=-1, keepdims=True) + EPS)
    return (x * r) * g


def _pack_bf16_pairs(x):
    n = x.shape[1] // 2
    lo = lax.bitcast_convert_type(x[:, :n].astype(BF16).astype(F32), I32)
    hi = lax.bitcast_convert_type(x[:, n:].astype(BF16).astype(F32), I32)
    return (hi & jnp.int32(-65536)) | lax.shift_right_logical(lo, 16)


def _unpack_bf16_pairs(p):
    lo = lax.bitcast_convert_type(lax.shift_left(p, 16), F32)
    hi = lax.bitcast_convert_type(p & jnp.int32(-65536), F32)
    return lo, hi


def _inproj_kernel(x_ref, g_ref, wcu_ref, wbg_ref, wrest_ref, cw_ref, cb_ref, wa_ref,
                   eg_ref, eu_ref, ed_ref,
                   za_ref, proj_ref, eg_out, eu_out, ed_out, halo_ref, *, tiles_per_seq):
    i = pl.program_id(0)
    eg_out[...] = eg_ref[...].astype(BF16)
    eu_out[...] = eu_ref[...].astype(BF16)
    ed_out[...] = ed_ref[...].astype(BF16)
    h = _rms(x_ref[...], g_ref[...]).astype(BF16)
    pcu = jnp.dot(h, wcu_ref[...], preferred_element_type=F32)
    pbg = jnp.dot(h, wbg_ref[...], preferred_element_type=F32)
    proj_ref[...] = jnp.dot(h, wrest_ref[...], preferred_element_type=F32).astype(BF16)
    cu = pcu[:, :D_MODEL] * pcu[:, D_MODEL:]
    first = (i % tiles_per_seq) == 0
    hist = jnp.where(first, 0.0, halo_ref[...])
    prev1 = hist[HALO_ROWS - 1:HALO_ROWS]
    prev2 = hist[HALO_ROWS - 2:HALO_ROWS - 1]
    halo_ref[...] = cu[TM_PROJ - HALO_ROWS:, :]
    row = lax.broadcasted_iota(I32, cu.shape, 0)
    cu1 = jnp.where(row == 0, prev1, pltpu.roll(cu, 1, 0))
    cu2 = jnp.where(row == 0, prev2, jnp.where(row == 1, prev1, pltpu.roll(cu, 2, 0)))
    cw = cw_ref[...]
    y = cw[0:1] * cu2 + cw[1:2] * cu1 + cw[2:3] * cu + cb_ref[...]
    ya = (pbg[:, :D_MODEL] * y).astype(BF16)
    z = jnp.dot(ya, wa_ref[...], preferred_element_type=F32)
    za_ref[...] = (jax.nn.sigmoid(pbg[:, D_MODEL:]) * z).astype(BF16)


def _inproj(xf, g, w_cu, w_bg, w_rest, conv_w, conv_b, wa, seq, w_gate, w_up, w_down):
    t = xf.shape[0]
    n_tiles = t // TM_PROJ
    const = lambda shape: pl.BlockSpec(shape, lambda i: (0, 0), pipeline_mode=pl.Buffered(1))
    slabs = [w.reshape(-1, w.shape[-1]) for w in (w_gate, w_up, w_down)]
    slab_rows = [s.shape[0] // n_tiles for s in slabs]
    assert all(s.shape[0] == r * n_tiles and r % 16 == 0 for s, r in zip(slabs, slab_rows))
    slab_specs = [pl.BlockSpec((r, s.shape[1]), lambda i: (i, 0)) for s, r in zip(slabs, slab_rows)]
    outs = pl.pallas_call(
        functools.partial(_inproj_kernel, tiles_per_seq=seq // TM_PROJ),
        name="inproj_conv",
        grid=(n_tiles,),
        in_specs=[
            pl.BlockSpec((TM_PROJ, D_MODEL), lambda i: (i, 0)),
            const((1, D_MODEL)),
            const((D_MODEL, 2 * D_MODEL)), const((D_MODEL, 2 * D_MODEL)), const((D_MODEL, REST_COLS)),
            const((3, D_MODEL)), const((1, D_MODEL)), const((D_MODEL, D_MODEL)),
        ] + slab_specs,
        out_specs=[pl.BlockSpec((TM_PROJ, D_MODEL), lambda i: (i, 0)),
                   pl.BlockSpec((TM_PROJ, REST_COLS), lambda i: (i, 0))] + slab_specs,
        out_shape=[jax.ShapeDtypeStruct((t, D_MODEL), BF16),
                   jax.ShapeDtypeStruct((t, REST_COLS), BF16)]
                  + [jax.ShapeDtypeStruct(s.shape, BF16) for s in slabs],
        scratch_shapes=[pltpu.VMEM((HALO_ROWS, D_MODEL), F32)],
        compiler_params=pltpu.CompilerParams(
            dimension_semantics=("arbitrary",), vmem_limit_bytes=VMEM_LIMIT),
    )(xf, g, w_cu, w_bg, w_rest, conv_w, conv_b, wa, *slabs)
    za, proj, eg, eu, ed = outs
    return za, proj, eg.reshape(w_gate.shape), eu.reshape(w_up.shape), ed.reshape(w_down.shape)


def _attn_kernel(sink_ref, q_ref, k_ref, v_ref, kp_ref, vp_ref, gb_ref, za_ref, wb_ref, o_ref,
                 *, tiles_per_seq):
    first_tile = (pl.program_id(0) % tiles_per_seq) == 0
    ks = lax.broadcasted_iota(I32, (WINDOW, WINDOW), 0)
    qq = lax.broadcasted_iota(I32, (WINDOW, WINDOW), 1)
    own = ks <= qq
    dist = jnp.where(own, qq - ks, qq - ks + WINDOW).astype(F32)
    visible0 = jnp.logical_or(own, jnp.logical_not(first_tile))
    log2e = math.log2(math.e)
    c_scale = log2e / math.sqrt(HEAD_DIM)
    nt = (((1,), (1,)), ((), ()))
    zk = jnp.zeros((2 * WINDOW, HEAD_DIM), BF16)

    def transposed(v_blk):
        return jnp.transpose(v_blk.astype(F32)).astype(BF16)

    def project(rows_p, attn_blk):
        yb = jnp.dot(attn_blk, wb_ref[...], preferred_element_type=F32)
        zb = jax.nn.sigmoid(gb_ref[rows_p, :].astype(F32)) * yb
        o_ref[rows_p, :] = (za_ref[rows_p, :].astype(F32) + zb).astype(BF16)

    pending = None
    prev_k = kp_ref[...]
    prev_vt = transposed(vp_ref[...])
    for sb in range(TQ_ATTN // WINDOW):
        rows = slice(sb * WINDOW, (sb + 1) * WINDOW)
        cur_k = k_ref[rows, :]
        cur_vt = transposed(v_ref[rows, :])
        scores, vcats = [], []
        for kh in range(N_KV_HEADS):
            cols = slice(kh * HEAD_DIM, (kh + 1) * HEAD_DIM)
            kcat = jnp.concatenate([prev_k[:, cols], cur_k[:, cols]], axis=0)
            vcats.append(jnp.concatenate([prev_vt[cols, :], cur_vt[cols, :]], axis=1))
            qg = jnp.concatenate([q_ref[rows, (2 * kh) * LANES:(2 * kh + 1) * LANES],
                                  q_ref[rows, (2 * kh + 1) * LANES:(2 * kh + 2) * LANES]], axis=0)
            k_pad = jnp.concatenate([jnp.concatenate([kcat, zk], axis=1),
                                     jnp.concatenate([zk, kcat], axis=1)], axis=0)
            scores.append(lax.dot_general(k_pad, qg, nt, preferred_element_type=F32))
        if pending is not None:
            project(*pending)
        probs, rdens = [], []
        for kh in range(N_KV_HEADS):
            for pos in range(2):
                pr, rd = [], []
                for half in range(2):
                    h = kh * GROUP + 2 * half + pos
                    slope = 2.0 ** (-8.0 * (h + 1) / N_HEADS)
                    qcols = slice(half * WINDOW, (half + 1) * WINDOW)
                    krow = pos * 2 * WINDOW
                    st = scores[kh]
                    s = (jnp.where(own, st[krow + WINDOW:krow + 2 * WINDOW, qcols],
                                   st[krow:krow + WINDOW, qcols]) * c_scale
                         - (slope * log2e) * dist)
                    if sb == 0:
                        s = jnp.where(visible0, s, -jnp.inf)
                    m = jnp.max(s, axis=0, keepdims=True)
                    p = jnp.exp2(s - m)
                    den = jnp.sum(p, axis=0, keepdims=True) + jnp.exp2(sink_ref[h] * log2e - m)
                    rd.append(1.0 / den)
                    pr.append(jnp.concatenate(
                        [jnp.where(own, 0.0, p).astype(BF16), jnp.where(own, p, 0.0).astype(BF16)],
                        axis=0))
                probs.append(jnp.concatenate(pr, axis=1))
                rdens.append(jnp.concatenate(rd, axis=1))
        out_t = [None] * N_HEADS
        for kh in range(N_KV_HEADS):
            for pos in range(2):
                o2 = jnp.dot(vcats[kh], probs[2 * kh + pos], preferred_element_type=F32)
                o2 = o2 * rdens[2 * kh + pos]
                out_t[kh * GROUP + pos] = o2[:, :WINDOW]
                out_t[kh * GROUP + 2 + pos] = o2[:, WINDOW:]
        pending = (rows, jnp.transpose(jnp.concatenate(out_t, axis=0)).astype(BF16))
        prev_k, prev_vt = cur_k, cur_vt
    project(*pending)


def _attention(proj, za, sinks, wb, seq):
    t = proj.shape[0]
    sub = TQ_ATTN // WINDOW
    return pl.pallas_call(
        functools.partial(_attn_kernel, tiles_per_seq=seq // TQ_ATTN),
        name="swattn",
        grid=(t // TQ_ATTN,),
        in_specs=[
            pl.BlockSpec(memory_space=pltpu.SMEM),
            pl.BlockSpec((TQ_ATTN, D_MODEL), lambda i: (i, COL_Q)),
            pl.BlockSpec((TQ_ATTN, KV_WIDTH), lambda i: (i, COL_K)),
            pl.BlockSpec((TQ_ATTN, KV_WIDTH), lambda i: (i, COL_V)),
            pl.BlockSpec((WINDOW, KV_WIDTH), lambda i: (jnp.maximum(i * sub - 1, 0), COL_K)),
            pl.BlockSpec((WINDOW, KV_WIDTH), lambda i: (jnp.maximum(i * sub - 1, 0), COL_V)),
            pl.BlockSpec((TQ_ATTN, D_MODEL), lambda i: (i, COL_GB)),
            pl.BlockSpec((TQ_ATTN, D_MODEL), lambda i: (i, 0)),
            pl.BlockSpec((D_MODEL, D_MODEL), lambda i: (0, 0), pipeline_mode=pl.Buffered(1)),
        ],
        out_specs=pl.BlockSpec((TQ_ATTN, D_MODEL), lambda i: (i, 0)),
        out_shape=jax.ShapeDtypeStruct((t, D_MODEL), BF16),
        compiler_params=pltpu.CompilerParams(
            dimension_semantics=("arbitrary",), vmem_limit_bytes=VMEM_LIMIT),
    )(sinks, proj, proj, proj, proj, proj, proj, za, wb)


def _mix_kernel(merged_ref, x_ref, wo_ref, g_ref, wr_ref, br_ref,
                xmid_ref, h_ref, route_ref, cnt_ref):
    subs = [slice(s * SUB_MIX, (s + 1) * SUB_MIX) for s in range(TM_MIX // SUB_MIX)]
    xm = [x_ref[r, :] + jnp.dot(merged_ref[r, :], wo_ref[...], preferred_element_type=F32)
          for r in subs]
    hs = []
    for r, v in zip(subs, xm):
        xmid_ref[r, :] = _pack_bf16_pairs(v)
        h = _rms(v, g_ref[...])
        h_ref[r, :] = _pack_bf16_pairs(h)
        hs.append(h)
    wr = wr_ref[...]
    logits = []
    for h in hs:
        h_hi = h.astype(BF16)
        h_lo = (h - h_hi.astype(F32)).astype(BF16)
        both = jnp.dot(h_hi, wr, preferred_element_type=F32)
        logits.append(both[:, :LANES] + both[:, LANES:]
                      + jnp.dot(h_lo, wr[:, :LANES], preferred_element_type=F32) + br_ref[...])
    subs_per_cnt = TM_CNT // SUB_MIX
    cnts = [jnp.zeros((8, LANES), F32) for _ in range(TM_MIX // TM_CNT)]
    for s, (r, lg) in enumerate(zip(subs, logits)):
        route, cnt = _route(lg)
        route_ref[r, :] = route
        cnts[s // subs_per_cnt] = cnts[s // subs_per_cnt] + cnt
    for c, cnt in enumerate(cnts):
        cnt_ref[c * 8:(c + 1) * 8, :] = cnt


def _route(logits):
    n = logits.shape[0]
    lt = jnp.transpose(logits)
    sub = lax.broadcasted_iota(I32, (EXPERTS_PER_GROUP, n), 0)
    neg = -jnp.inf
    gl = jnp.where(sub < N_GROUPS, lt[N_EXPERTS:N_EXPERTS + EXPERTS_PER_GROUP], neg)
    gmax = jnp.max(gl, axis=0, keepdims=True)
    g_idx = jnp.min(jnp.where(gl == gmax, sub, EXPERTS_PER_GROUP), axis=0, keepdims=True)
    p_g = 1.0 / jnp.sum(jnp.exp(gl - gmax), axis=0, keepdims=True)
    v1 = v2 = i1 = i2 = None
    for g in range(N_GROUPS):
        eg = lt[g * EXPERTS_PER_GROUP:(g + 1) * EXPERTS_PER_GROUP]
        a1 = jnp.max(eg, axis=0, keepdims=True)
        j1 = jnp.min(jnp.where(eg == a1, sub, EXPERTS_PER_GROUP), axis=0, keepdims=True)
        eg2 = jnp.where(sub == j1, neg, eg)
        a2 = jnp.max(eg2, axis=0, keepdims=True)
        j2 = jnp.min(jnp.where(eg2 == a2, sub, EXPERTS_PER_GROUP), axis=0, keepdims=True)
        if g == 0:
            v1, v2, i1, i2 = a1, a2, j1, j2
        else:
            chosen = g_idx == g
            v1, v2 = jnp.where(chosen, a1, v1), jnp.where(chosen, a2, v2)
            i1, i2 = jnp.where(chosen, j1, i1), jnp.where(chosen, j2, i2)
    e21 = jnp.exp(v2 - v1)
    w1 = p_g / (1.0 + e21)
    w2 = p_g * e21 / (1.0 + e21)
    e1 = g_idx * EXPERTS_PER_GROUP + i1
    e2 = g_idx * EXPERTS_PER_GROUP + i2
    rows8 = jnp.where(sub == 0, e1.astype(F32),
                      jnp.where(sub == 1, e2.astype(F32),
                                jnp.where(sub == 2, w1, jnp.where(sub == 3, w2, 0.0))))
    route_t = jnp.concatenate([rows8, jnp.zeros((LANES - EXPERTS_PER_GROUP, n), F32)], axis=0)
    expert_row = lax.broadcasted_iota(I32, (LANES, n), 0)
    onehot_t = ((expert_row == e1) | (expert_row == e2)).astype(BF16)
    cnt = lax.dot_general(jnp.ones((8, n), BF16), onehot_t, (((1,), (1,)), ((), ())),
                          preferred_element_type=F32)
    return jnp.transpose(route_t), cnt


def _mix(merged, xf, wo, g, wr, br, chunk, t):
    n_tiles = t // TM_MIX
    first = chunk * n_tiles
    cnt_rows = TM_MIX // TM_CNT * 8
    full = lambda shape: pl.BlockSpec(shape, lambda i: (0, 0), pipeline_mode=pl.Buffered(1))
    tile = lambda w=D_MODEL: pl.BlockSpec((TM_MIX, w), lambda i: (i, 0))
    src = lambda c=0: pl.BlockSpec((TM_MIX, D_MODEL), lambda i: (first + i, c))
    return pl.pallas_call(
        _mix_kernel,
        name="merge_router",
        grid=(n_tiles,),
        in_specs=[
            src(), src(),
            full((D_MODEL, D_MODEL)), full((1, D_MODEL)),
            full((D_MODEL, 2 * LANES)), full((1, LANES)),
        ],
        out_specs=[tile(PACKED), tile(PACKED), tile(LANES),
                   pl.BlockSpec((cnt_rows, LANES), lambda i: (i, 0))],
        out_shape=[
            jax.ShapeDtypeStruct((t, PACKED), I32),
            jax.ShapeDtypeStruct((t, PACKED), I32),
            jax.ShapeDtypeStruct((t, LANES), F32),
            jax.ShapeDtypeStruct((n_tiles * cnt_rows, LANES), F32),
        ],
        compiler_params=pltpu.CompilerParams(
            dimension_semantics=("arbitrary",), vmem_limit_bytes=VMEM_LIMIT),
    )(merged, xf, wo, g, wr, br)


def _pos_kernel(route_ref, base_ref, pos_ref):
    lane = lax.broadcasted_iota(I32, (TM_CNT, LANES), 1)
    r = lax.broadcasted_iota(I32, (TM_CNT, TM_CNT), 0)
    c = lax.broadcasted_iota(I32, (TM_CNT, TM_CNT), 1)
    lower = (c < r).astype(BF16)
    subs = [slice(s * TM_CNT, (s + 1) * TM_CNT) for s in range(POS_TILES)]
    routes = [route_ref[rs, :] for rs in subs]
    e1 = [jnp.sum(jnp.where(lane == 0, rt, 0.0), axis=-1, keepdims=True).astype(I32) for rt in routes]
    e2 = [jnp.sum(jnp.where(lane == 1, rt, 0.0), axis=-1, keepdims=True).astype(I32) for rt in routes]
    onehot = [((lane == a) | (lane == b)).astype(BF16) for a, b in zip(e1, e2)]
    before = [jnp.dot(lower, oh, preferred_element_type=F32) + base_ref[s]
              for s, oh in enumerate(onehot)]
    for s, rs in enumerate(subs):
        p1 = jnp.sum(jnp.where(lane == e1[s], before[s], 0.0), axis=-1, keepdims=True)
        p2 = jnp.sum(jnp.where(lane == e2[s], before[s], 0.0), axis=-1, keepdims=True)
        packed = jnp.where(lane == 0, p1, jnp.where(lane == 1, p2, 0.0))
        pos_ref[:, rs] = jnp.transpose(packed)[0:TOP_K, :].astype(I32)


def _positions(route, base):
    t = route.shape[0]
    n_steps = t // (TM_CNT * POS_TILES)
    return pl.pallas_call(
        _pos_kernel,
        name="positions",
        grid=(n_steps,),
        in_specs=[
            pl.BlockSpec((TM_CNT * POS_TILES, LANES), lambda i: (i, 0)),
            pl.BlockSpec((POS_TILES, 1, LANES), lambda i: (i, 0, 0)),
        ],
        out_specs=pl.BlockSpec((TOP_K, TM_CNT * POS_TILES), lambda i: (0, i)),
        out_shape=jax.ShapeDtypeStruct((TOP_K, t), I32),
        compiler_params=pltpu.CompilerParams(dimension_semantics=("arbitrary",)),
    )(route, base)


def _sc_mesh():
    return plsc.VectorSubcoreMesh(core_axis_name="c", subcore_axis_name="s",
                                  num_cores=SC_CORES, num_subcores=SC_SUBCORES)


def _sc_worker():
    return lax.axis_index("s") * SC_CORES + lax.axis_index("c")


def _dispatch(pos, hp, rows):
    t = hp.shape[0]
    per_w = t // SC_WORKERS
    n_ch = per_w // SC_CHUNK
    pos4 = pos.reshape(TOP_K, SC_WORKERS, n_ch, SC_CHUNK)

    @functools.partial(
        pl.kernel, mesh=_sc_mesh(),
        out_type=jax.ShapeDtypeStruct((rows, PACKED), I32),
        scratch_types=[pltpu.VMEM((TOP_K, n_ch, SC_CHUNK), I32),
                       pltpu.VMEM((SC_CHUNK, PACKED), I32)])
    def scatter(hp_hbm, pos_hbm, xs_hbm, idx_v, rows_v):
        wid = _sc_worker()
        for k in range(TOP_K):
            pltpu.sync_copy(pos_hbm.at[k, wid], idx_v.at[k])

        def body(c, carry):
            start = pl.multiple_of(wid * per_w + c * SC_CHUNK, SC_CHUNK)
            pltpu.sync_copy(hp_hbm.at[pl.ds(start, SC_CHUNK)], rows_v)
            for k in range(TOP_K):
                pltpu.sync_copy(rows_v, xs_hbm.at[idx_v.at[k, c]])
            return carry

        lax.fori_loop(0, n_ch, body, 0)

    return scatter(hp, pos4)


def _gather_rows(table, idx):
    n = idx.shape[0]
    per_w = n // SC_WORKERS
    n_ch = per_w // SC_CHUNK
    idx3 = idx.reshape(SC_WORKERS, n_ch, SC_CHUNK)

    @functools.partial(
        pl.kernel, mesh=_sc_mesh(),
        out_type=jax.ShapeDtypeStruct((n, PACKED), I32),
        scratch_types=[pltpu.VMEM((n_ch, SC_CHUNK), I32),
                       pltpu.VMEM((SC_CHUNK, PACKED), I32)])
    def gather(table_hbm, idx_hbm, out_hbm, idx_v, rows_v):
        wid = _sc_worker()
        pltpu.sync_copy(idx_hbm.at[wid], idx_v)

        def body(c, carry):
            start = pl.multiple_of(wid * per_w + c * SC_CHUNK, SC_CHUNK)
            pltpu.sync_copy(table_hbm.at[idx_v.at[c]], rows_v)
            pltpu.sync_copy(rows_v, out_hbm.at[pl.ds(start, SC_CHUNK)])
            return carry

        lax.fori_loop(0, n_ch, body, 0)

    return gather(table, idx3)


def _expert_kernel(te_ref, nx_ref, sl_ref, ts_ref, tv_ref, xs_ref, wg_hbm, wu_hbm, wd_hbm, o_ref,
                   wg_s, wu_s, wd_s, sem):
    del ts_ref
    i = pl.program_id(0)
    n_valid = tv_ref[i]
    expert = te_ref[i]
    slot = sl_ref[i]

    def weight_copies(e, s):
        return (pltpu.make_async_copy(wg_hbm.at[e], wg_s.at[s], sem.at[s, 0]),
                pltpu.make_async_copy(wu_hbm.at[e], wu_s.at[s], sem.at[s, 1]),
                pltpu.make_async_copy(wd_hbm.at[e], wd_s.at[s], sem.at[s, 2]))

    @pl.when(i == 0)
    def _():
        for copy in weight_copies(expert, slot):
            copy.start()

    @pl.when(jnp.logical_or(i == 0, expert != te_ref[jnp.maximum(i - 1, 0)]))
    def _():
        for copy in weight_copies(expert, slot):
            copy.wait()
        nxt = nx_ref[i]

        @pl.when(nxt >= 0)
        def _():
            for copy in weight_copies(nxt, 1 - slot):
                copy.start()

    def mlp(n_sub):
        wg, wu, wd = wg_s.at[slot], wu_s.at[slot], wd_s.at[slot]
        subs = [slice(0, n_sub * SUB_EXP)] if n_sub else []
        xin = []
        for r in subs:
            rid = lax.broadcasted_iota(I32, (n_sub * SUB_EXP, PACKED), 0)
            lo, hi = _unpack_bf16_pairs(jnp.where(rid < n_valid, xs_ref[r, :], 0))
            xin.append((lo.astype(BF16), hi.astype(BF16)))
        ab = [(jnp.dot(lo, wg[:PACKED, :], preferred_element_type=F32)
               + jnp.dot(hi, wg[PACKED:, :], preferred_element_type=F32),
               jnp.dot(lo, wu[:PACKED, :], preferred_element_type=F32)
               + jnp.dot(hi, wu[PACKED:, :], preferred_element_type=F32)) for lo, hi in xin]
        for r, (ai, bi) in zip(subs, ab):
            hm = (ai * jax.nn.sigmoid(ai) * bi).astype(BF16)
            o_ref[r, :] = _pack_bf16_pairs(jnp.dot(hm, wd[...], preferred_element_type=F32))
        if n_sub * SUB_EXP < TM_EXP:
            o_ref[n_sub * SUB_EXP:, :] = jnp.zeros((TM_EXP - n_sub * SUB_EXP, PACKED), I32)

    n_subs = TM_EXP // SUB_EXP
    for n_sub in range(n_subs + 1):
        lo_rows = (n_sub - 1) * SUB_EXP if n_sub else -1
        in_range = jnp.logical_and(n_valid > lo_rows, n_valid <= n_sub * SUB_EXP)
        pl.when(in_range)(functools.partial(mlp, n_sub))


def _experts(tile_expert, next_expert, tile_slot, tile_src, tile_valid, xs, wg, wu, wd):
    rows = xs.shape[0]
    hbm = pl.BlockSpec(memory_space=pl.ANY)
    grid_spec = pltpu.PrefetchScalarGridSpec(
        num_scalar_prefetch=5,
        grid=(rows // TM_EXP,),
        in_specs=[pl.BlockSpec((TM_EXP, PACKED), lambda i, te, nx, sl, ts, tv: (ts[i], 0)),
                  hbm, hbm, hbm],
        out_specs=pl.BlockSpec((TM_EXP, PACKED), lambda i, te, nx, sl, ts, tv: (i, 0)),
        scratch_shapes=[
            pltpu.VMEM((2, D_MODEL, D_FF), BF16), pltpu.VMEM((2, D_MODEL, D_FF), BF16),
            pltpu.VMEM((2, D_FF, D_MODEL), BF16),
            pltpu.SemaphoreType.DMA((2, 3)),
        ],
    )
    return pl.pallas_call(
        _expert_kernel,
        name="experts",
        grid_spec=grid_spec,
        out_shape=jax.ShapeDtypeStruct((rows, PACKED), I32),
        compiler_params=pltpu.CompilerParams(
            dimension_semantics=("arbitrary",), vmem_limit_bytes=VMEM_LIMIT),
    )(tile_expert, next_expert, tile_slot, tile_src, tile_valid, xs, wg, wu, wd)


def _combine_kernel(y1_ref, y2_ref, route_ref, xmid_ref, g_ref, *rest):
    o_ref = rest[-1]
    route = route_ref[...]
    lane = lax.broadcasted_iota(I32, route.shape, 1)
    w1 = jnp.sum(jnp.where(lane == 2, route, 0.0), axis=-1, keepdims=True)
    w2 = jnp.sum(jnp.where(lane == 3, route, 0.0), axis=-1, keepdims=True)
    lo1, hi1 = _unpack_bf16_pairs(y1_ref[...])
    lo2, hi2 = _unpack_bf16_pairs(y2_ref[...])
    lox, hix = _unpack_bf16_pairs(xmid_ref[...])
    x_out = jnp.concatenate([lox + (lo1 * w1 + lo2 * w2), hix + (hi1 * w1 + hi2 * w2)], axis=1)
    o_ref[...] = _rms(x_out, g_ref[...])


def _combine(yg, route, xmid, g, chunk, t_total, out_prev):
    t = xmid.shape[0]
    n_tiles = t // TM_CMB
    first = chunk * n_tiles
    in_specs = [
        pl.BlockSpec((TM_CMB, PACKED), lambda i: (i, 0)),
        pl.BlockSpec((TM_CMB, PACKED), lambda i: (n_tiles + i, 0)),
        pl.BlockSpec((TM_CMB, LANES), lambda i: (i, 0)),
        pl.BlockSpec((TM_CMB, PACKED), lambda i: (i, 0)),
        pl.BlockSpec((1, D_MODEL), lambda i: (0, 0)),
    ]
    args = [yg, yg, route, xmid, g]
    aliases = {}
    if out_prev is not None:
        in_specs.append(pl.BlockSpec(memory_space=pl.ANY))
        aliases = {len(args): 0}
        args.append(out_prev)
    return pl.pallas_call(
        _combine_kernel,
        name="combine",
        grid=(n_tiles,),
        in_specs=in_specs,
        out_specs=pl.BlockSpec((TM_CMB, D_MODEL), lambda i: (first + i, 0)),
        out_shape=jax.ShapeDtypeStruct((t_total, D_MODEL), F32),
        input_output_aliases=aliases,
        compiler_params=pltpu.CompilerParams(
            dimension_semantics=("arbitrary",), vmem_limit_bytes=VMEM_LIMIT),
    )(*args)


def _split_bf16(w):
    hi = w.astype(BF16)
    lo = (w - hi.astype(F32)).astype(BF16)
    return hi, lo


def kernel(x, norm_mix, w_in, conv_w, conv_b, w_a_out, sinks, w_b_out, w_o, norm_ffn, w_group,
           b_group, w_expert, b_expert, w_gate, w_up, w_down, norm_final):
    bsz, seq, d = x.shape
    t = bsz * seq
    assert d == D_MODEL and seq % TM_PROJ == 0 and seq % TQ_ATTN == 0
    xf = x.reshape(t, d)
    row = lambda v: v.reshape(1, -1)

    w_b, w_c, w_u, w_q, w_k, w_v, w_ga, w_gb = jnp.split(w_in, REF_SPLITS, axis=1)
    w_cu = jnp.concatenate([w_c, w_u], axis=1).astype(BF16)
    w_bg = jnp.concatenate([w_b, w_ga], axis=1).astype(BF16)
    w_rest = jnp.concatenate([w_gb, w_q, w_k, w_v], axis=1).astype(BF16)

    za, proj, w_gate, w_up, w_down = _inproj(xf, row(norm_mix), w_cu, w_bg, w_rest, conv_w,
                                             row(conv_b), w_a_out.astype(BF16), seq,
                                             w_gate, w_up, w_down)
    merged = _attention(proj, za, sinks, w_b_out.astype(BF16), seq)

    pad = LANES - N_GROUPS - N_EXPERTS
    w_r = jnp.concatenate([w_expert, w_group, jnp.zeros((d, pad), F32)], axis=1)
    b_r = jnp.concatenate([b_expert, b_group, jnp.zeros((pad,), F32)]).reshape(1, LANES)
    wr = jnp.concatenate(_split_bf16(w_r), axis=1)
    wo = w_o.astype(BF16)

    t_chunk = t // MOE_CHUNKS
    assert t == t_chunk * MOE_CHUNKS and all(
        t_chunk % step == 0 for step in (TM_MIX, TM_CNT * POS_TILES, TM_CMB, SC_WORKERS * SC_CHUNK))
    out = None
    for chunk in range(MOE_CHUNKS):
        xmid, h2, route, cnt = _mix(merged, xf, wo, row(norm_ffn), wr, b_r, chunk, t_chunk)
        out = _moe_chunk(xmid, h2, route, cnt, w_gate, w_up, w_down, row(norm_final), chunk, t, out)
    return out.reshape(bsz, seq, d)


def _moe_chunk(xmid, h2, route, cnt, w_gate, w_up, w_down, g_final, chunk, t_total, out_prev):
    t = xmid.shape[0]
    n_tiles = t // TM_CNT
    cnt = cnt.reshape(n_tiles, 8, LANES)[:, 0, :N_EXPERTS].astype(I32)
    totals = jnp.sum(cnt, axis=0)
    tiles_e = (totals + TM_EXP - 1) // TM_EXP
    tile_end = jnp.cumsum(tiles_e)
    offset = (tile_end - tiles_e) * TM_EXP
    base = offset[None, :] + jnp.cumsum(cnt, axis=0) - cnt
    base = jnp.pad(base, ((0, 0), (0, LANES - N_EXPERTS))).astype(F32).reshape(n_tiles, 1, LANES)
    rows = t * TOP_K + N_EXPERTS * TM_EXP
    n_active = tile_end[-1]
    tile_id = jnp.arange(rows // TM_EXP, dtype=I32)
    tile_src = jnp.minimum(tile_id, n_active - 1)
    tile_expert = jnp.sum((tile_src[:, None] >= tile_end[None, :]).astype(I32), axis=1)
    tile_expert = jnp.minimum(tile_expert, N_EXPERTS - 1)
    row_in_expert = (tile_id - (tile_end - tiles_e)[tile_expert]) * TM_EXP
    tile_valid = jnp.clip(totals[tile_expert] - row_in_expert, 0, TM_EXP)
    tile_valid = jnp.where(tile_id < n_active, tile_valid, 0).astype(I32)
    after = tile_end[tile_expert]
    next_expert = jnp.where(after < n_active, tile_expert[jnp.minimum(after, n_active - 1)], -1)
    first_of_expert = jnp.concatenate(
        [jnp.ones((1,), I32), (tile_expert[1:] != tile_expert[:-1]).astype(I32)])
    tile_slot = (jnp.cumsum(first_of_expert) - 1) % 2

    pos = _positions(route, base)
    xs = _dispatch(pos, h2, rows)
    ys = _experts(tile_expert.astype(I32), next_expert.astype(I32), tile_slot.astype(I32),
                  tile_src.astype(I32), tile_valid, xs, w_gate, w_up, w_down)
    yg = _gather_rows(ys, pos.reshape(TOP_K * t))
    return _combine(yg, route, xmid, g_final, chunk, t_total, out_prev)
```

```python
import functools
import math

import jax
import jax.numpy as jnp
from jax import lax
from jax.experimental import pallas as pl
from jax.experimental.pallas import tpu as pltpu
from jax.experimental.pallas import tpu_sc as plsc

F32 = jnp.float32
BF16 = jnp.bfloat16
I32 = jnp.int32

D_MODEL = 1024
HEAD_DIM = 64
N_HEADS = 16
N_KV_HEADS = 4
GROUP = N_HEADS // N_KV_HEADS
KV_WIDTH = N_KV_HEADS * HEAD_DIM
WINDOW = 128
N_GROUPS = 4
EXPERTS_PER_GROUP = 8
N_EXPERTS = N_GROUPS * EXPERTS_PER_GROUP
TOP_K = 2
D_FF = 512
EPS = 1e-6
LANES = 128

COL_WIDTH = {"b": D_MODEL, "c": D_MODEL, "u": D_MODEL, "q": D_MODEL, "k": KV_WIDTH, "v": KV_WIDTH,
             "ga": D_MODEL, "gb": D_MODEL}
COL_START = dict(zip(COL_WIDTH, (sum(list(COL_WIDTH.values())[:i]) for i in range(len(COL_WIDTH)))))
IN_COLS = sum(COL_WIDTH.values())
REST_COLS = 2 * D_MODEL + 2 * KV_WIDTH
COL_GB, COL_Q = 0, 1
COL_K, COL_V = 2 * D_MODEL // KV_WIDTH, 2 * D_MODEL // KV_WIDTH + 1

TM_PROJ = 512
TQ_ATTN = 1024
TM_MIX = 1024
SUB_MIX = 256
TM_CNT = 512
POS_TILES = 8
MOE_CHUNKS = 2
TM_EXP = 512
SUB_EXP = 256
TM_CMB = 1024
HALO_ROWS = 8
VMEM_LIMIT = 56 * 1024 * 1024
PACKED = D_MODEL // 2

SC_CORES = 2
SC_SUBCORES = 16
SC_WORKERS = SC_CORES * SC_SUBCORES
SC_CHUNK = 64


def _rms(x, g):
    r = lax.rsqrt(jnp.mean(x * x, axis=-1, keepdims=True) + EPS)
    return (x * r) * g


def _pack_bf16_pairs(x):
    n = x.shape[1] // 2
    lo = lax.bitcast_convert_type(x[:, :n].astype(BF16).astype(F32), I32)
    hi = lax.bitcast_convert_type(x[:, n:].astype(BF16).astype(F32), I32)
    return (hi & jnp.int32(-65536)) | lax.shift_right_logical(lo, 16)


def _unpack_bf16_pairs(p):
    lo = lax.bitcast_convert_type(lax.shift_left(p, 16), F32)
    hi = lax.bitcast_convert_type(p & jnp.int32(-65536), F32)
    return lo, hi


def _inproj_kernel(x_ref, g_ref, w_ref, cw_ref, cb_ref, wa_ref,
                   eg_ref, eu_ref, ed_ref,
                   za_ref, proj_ref, eg_out, eu_out, ed_out, halo_ref, *, tiles_per_seq):
    i = pl.program_id(0)
    eg_out[...] = eg_ref[...].astype(BF16)
    eu_out[...] = eu_ref[...].astype(BF16)
    ed_out[...] = ed_ref[...].astype(BF16)
    h = _rms(x_ref[...], g_ref[...]).astype(BF16)
    col = lambda a, b=None: slice(COL_START[a], COL_START[b or a] + COL_WIDTH[b or a])
    pcu = jnp.dot(h, w_ref[:, col("c", "u")], preferred_element_type=F32)
    pb = jnp.dot(h, w_ref[:, col("b")], preferred_element_type=F32)
    pga = jnp.dot(h, w_ref[:, col("ga")], preferred_element_type=F32)
    proj_ref[:, :D_MODEL] = jnp.dot(h, w_ref[:, col("gb")], preferred_element_type=F32).astype(BF16)
    proj_ref[:, D_MODEL:] = jnp.dot(h, w_ref[:, col("q", "v")],
                                    preferred_element_type=F32).astype(BF16)
    cu = pcu[:, :D_MODEL] * pcu[:, D_MODEL:]
    first = (i % tiles_per_seq) == 0
    hist = jnp.where(first, 0.0, halo_ref[...])
    prev1 = hist[HALO_ROWS - 1:HALO_ROWS]
    prev2 = hist[HALO_ROWS - 2:HALO_ROWS - 1]
    halo_ref[...] = cu[TM_PROJ - HALO_ROWS:, :]
    row = lax.broadcasted_iota(I32, cu.shape, 0)
    cu1 = jnp.where(row == 0, prev1, pltpu.roll(cu, 1, 0))
    cu2 = jnp.where(row == 0, prev2, jnp.where(row == 1, prev1, pltpu.roll(cu, 2, 0)))
    cw = cw_ref[...]
    y = cw[0:1] * cu2 + cw[1:2] * cu1 + cw[2:3] * cu + cb_ref[...]
    ya = (pb * y).astype(BF16)
    z = jnp.dot(ya, wa_ref[...], preferred_element_type=F32)
    za_ref[...] = (jax.nn.sigmoid(pga) * z).astype(BF16)


def _inproj(xf, g, w_in, conv_w, conv_b, wa, seq, w_gate, w_up, w_down):
    t = xf.shape[0]
    n_tiles = t // TM_PROJ
    const = lambda shape: pl.BlockSpec(shape, lambda i: (0, 0), pipeline_mode=pl.Buffered(1))
    slabs = [w.reshape(-1, w.shape[-1]) for w in (w_gate, w_up, w_down)]
    slab_rows = [s.shape[0] // n_tiles for s in slabs]
    assert all(s.shape[0] == r * n_tiles and r % 16 == 0 for s, r in zip(slabs, slab_rows))
    slab_specs = [pl.BlockSpec((r, s.shape[1]), lambda i: (i, 0)) for s, r in zip(slabs, slab_rows)]
    outs = pl.pallas_call(
        functools.partial(_inproj_kernel, tiles_per_seq=seq // TM_PROJ),
        name="inproj_conv",
        grid=(n_tiles,),
        in_specs=[
            pl.BlockSpec((TM_PROJ, D_MODEL), lambda i: (i, 0)),
            const((1, D_MODEL)),
            const((D_MODEL, IN_COLS)),
            const((3, D_MODEL)), const((1, D_MODEL)), const((D_MODEL, D_MODEL)),
        ] + slab_specs,
        out_specs=[pl.BlockSpec((TM_PROJ, D_MODEL), lambda i: (i, 0)),
                   pl.BlockSpec((TM_PROJ, REST_COLS), lambda i: (i, 0))] + slab_specs,
        out_shape=[jax.ShapeDtypeStruct((t, D_MODEL), BF16),
                   jax.ShapeDtypeStruct((t, REST_COLS), BF16)]
                  + [jax.ShapeDtypeStruct(s.shape, BF16) for s in slabs],
        scratch_shapes=[pltpu.VMEM((HALO_ROWS, D_MODEL), F32)],
        compiler_params=pltpu.CompilerParams(
            dimension_semantics=("arbitrary",), vmem_limit_bytes=VMEM_LIMIT),
    )(xf, g, w_in, conv_w, conv_b, wa, *slabs)
    za, proj, eg, eu, ed = outs
    return za, proj, eg.reshape(w_gate.shape), eu.reshape(w_up.shape), ed.reshape(w_down.shape)


def _attn_kernel(sink_ref, q_ref, k_ref, v_ref, kp_ref, vp_ref, gb_ref, za_ref, wb_ref, o_ref,
                 *, tiles_per_seq):
    first_tile = (pl.program_id(0) % tiles_per_seq) == 0
    ks = lax.broadcasted_iota(I32, (WINDOW, WINDOW), 0)
    qq = lax.broadcasted_iota(I32, (WINDOW, WINDOW), 1)
    own = ks <= qq
    dist = jnp.where(own, qq - ks, qq - ks + WINDOW).astype(F32)
    visible0 = jnp.logical_or(own, jnp.logical_not(first_tile))
    log2e = math.log2(math.e)
    c_scale = log2e / math.sqrt(HEAD_DIM)
    nt = (((1,), (1,)), ((), ()))
    zk = jnp.zeros((2 * WINDOW, HEAD_DIM), BF16)

    def transposed(v_blk):
        return jnp.transpose(v_blk.astype(F32)).astype(BF16)

    def project(rows_p, attn_blk):
        yb = jnp.dot(attn_blk, wb_ref[...], preferred_element_type=F32)
        zb = jax.nn.sigmoid(gb_ref[rows_p, :].astype(F32)) * yb
        o_ref[rows_p, :] = (za_ref[rows_p, :].astype(F32) + zb).astype(BF16)

    pending = None
    prev_k = kp_ref[...]
    prev_vt = transposed(vp_ref[...])
    for sb in range(TQ_ATTN // WINDOW):
        rows = slice(sb * WINDOW, (sb + 1) * WINDOW)
        cur_k = k_ref[rows, :]
        cur_vt = transposed(v_ref[rows, :])
        scores, vcats = [], []
        for kh in range(N_KV_HEADS):
            cols = slice(kh * HEAD_DIM, (kh + 1) * HEAD_DIM)
            kcat = jnp.concatenate([prev_k[:, cols], cur_k[:, cols]], axis=0)
            vcats.append(jnp.concatenate([prev_vt[cols, :], cur_vt[cols, :]], axis=1))
            qg = jnp.concatenate([q_ref[rows, (2 * kh) * LANES:(2 * kh + 1) * LANES],
                                  q_ref[rows, (2 * kh + 1) * LANES:(2 * kh + 2) * LANES]], axis=0)
            k_pad = jnp.concatenate([jnp.concatenate([kcat, zk], axis=1),
                                     jnp.concatenate([zk, kcat], axis=1)], axis=0)
            scores.append(lax.dot_general(k_pad, qg, nt, preferred_element_type=F32))
        if pending is not None:
            project(*pending)
        probs, rdens = [], []
        for kh in range(N_KV_HEADS):
            for pos in range(2):
                pr, rd = [], []
                for half in range(2):
                    h = kh * GROUP + 2 * half + pos
                    slope = 2.0 ** (-8.0 * (h + 1) / N_HEADS)
                    qcols = slice(half * WINDOW, (half + 1) * WINDOW)
                    krow = pos * 2 * WINDOW
                    st = scores[kh]
                    s = (jnp.where(own, st[krow + WINDOW:krow + 2 * WINDOW, qcols],
                                   st[krow:krow + WINDOW, qcols]) * c_scale
                         - (slope * log2e) * dist)
                    if sb == 0:
                        s = jnp.where(visible0, s, -jnp.inf)
                    m = jnp.max(s, axis=0, keepdims=True)
                    p = jnp.exp2(s - m)
                    den = jnp.sum(p, axis=0, keepdims=True) + jnp.exp2(sink_ref[h] * log2e - m)
                    rd.append(1.0 / den)
                    pr.append(jnp.concatenate(
                        [jnp.where(own, 0.0, p).astype(BF16), jnp.where(own, p, 0.0).astype(BF16)],
                        axis=0))
                probs.append(jnp.concatenate(pr, axis=1))
                rdens.append(jnp.concatenate(rd, axis=1))
        out_t = [None] * N_HEADS
        for kh in range(N_KV_HEADS):
            for pos in range(2):
                o2 = jnp.dot(vcats[kh], probs[2 * kh + pos], preferred_element_type=F32)
                o2 = o2 * rdens[2 * kh + pos]
                out_t[kh * GROUP + pos] = o2[:, :WINDOW]
                out_t[kh * GROUP + 2 + pos] = o2[:, WINDOW:]
        pending = (rows, jnp.transpose(jnp.concatenate(out_t, axis=0)).astype(BF16))
        prev_k, prev_vt = cur_k, cur_vt
    project(*pending)


def _attention(proj, za, sinks, wb, seq):
    t = proj.shape[0]
    sub = TQ_ATTN // WINDOW
    return pl.pallas_call(
        functools.partial(_attn_kernel, tiles_per_seq=seq // TQ_ATTN),
        name="swattn",
        grid=(t // TQ_ATTN,),
        in_specs=[
            pl.BlockSpec(memory_space=pltpu.SMEM),
            pl.BlockSpec((TQ_ATTN, D_MODEL), lambda i: (i, COL_Q)),
            pl.BlockSpec((TQ_ATTN, KV_WIDTH), lambda i: (i, COL_K)),
            pl.BlockSpec((TQ_ATTN, KV_WIDTH), lambda i: (i, COL_V)),
            pl.BlockSpec((WINDOW, KV_WIDTH), lambda i: (jnp.maximum(i * sub - 1, 0), COL_K)),
            pl.BlockSpec((WINDOW, KV_WIDTH), lambda i: (jnp.maximum(i * sub - 1, 0), COL_V)),
            pl.BlockSpec((TQ_ATTN, D_MODEL), lambda i: (i, COL_GB)),
            pl.BlockSpec((TQ_ATTN, D_MODEL), lambda i: (i, 0)),
            pl.BlockSpec((D_MODEL, D_MODEL), lambda i: (0, 0), pipeline_mode=pl.Buffered(1)),
        ],
        out_specs=pl.BlockSpec((TQ_ATTN, D_MODEL), lambda i: (i, 0)),
        out_shape=jax.ShapeDtypeStruct((t, D_MODEL), BF16),
        compiler_params=pltpu.CompilerParams(
            dimension_semantics=("arbitrary",), vmem_limit_bytes=VMEM_LIMIT),
    )(sinks, proj, proj, proj, proj, proj, proj, za, wb)


def _mix_kernel(merged_ref, x_ref, wo_ref, g_ref, wr_ref, br_ref,
                xmid_ref, h_ref, route_ref, cnt_ref):
    subs = [slice(s * SUB_MIX, (s + 1) * SUB_MIX) for s in range(TM_MIX // SUB_MIX)]
    xm = [x_ref[r, :] + jnp.dot(merged_ref[r, :], wo_ref[...], preferred_element_type=F32)
          for r in subs]
    hs = []
    for r, v in zip(subs, xm):
        xmid_ref[r, :] = _pack_bf16_pairs(v)
        h = _rms(v, g_ref[...])
        h_ref[r, :] = _pack_bf16_pairs(h)
        hs.append(h)
    wr = wr_ref[...]
    logits = []
    for h in hs:
        h_hi = h.astype(BF16)
        h_lo = (h - h_hi.astype(F32)).astype(BF16)
        both = jnp.dot(h_hi, wr, preferred_element_type=F32)
        logits.append(both[:, :LANES] + both[:, LANES:]
                      + jnp.dot(h_lo, wr[:, :LANES], preferred_element_type=F32) + br_ref[...])
    subs_per_cnt = TM_CNT // SUB_MIX
    cnts = [jnp.zeros((8, LANES), F32) for _ in range(TM_MIX // TM_CNT)]
    for s, (r, lg) in enumerate(zip(subs, logits)):
        route, cnt = _route(lg)
        route_ref[r, :] = route
        cnts[s // subs_per_cnt] = cnts[s // subs_per_cnt] + cnt
    for c, cnt in enumerate(cnts):
        cnt_ref[c * 8:(c + 1) * 8, :] = cnt


def _route(logits):
    n = logits.shape[0]
    lt = jnp.transpose(logits)
    sub = lax.broadcasted_iota(I32, (EXPERTS_PER_GROUP, n), 0)
    neg = -jnp.inf
    gl = jnp.where(sub < N_GROUPS, lt[N_EXPERTS:N_EXPERTS + EXPERTS_PER_GROUP], neg)
    gmax = jnp.max(gl, axis=0, keepdims=True)
    g_idx = jnp.min(jnp.where(gl == gmax, sub, EXPERTS_PER_GROUP), axis=0, keepdims=True)
    p_g = 1.0 / jnp.sum(jnp.exp(gl - gmax), axis=0, keepdims=True)
    v1 = v2 = i1 = i2 = None
    for g in range(N_GROUPS):
        eg = lt[g * EXPERTS_PER_GROUP:(g + 1) * EXPERTS_PER_GROUP]
        a1 = jnp.max(eg, axis=0, keepdims=True)
        j1 = jnp.min(jnp.where(eg == a1, sub, EXPERTS_PER_GROUP), axis=0, keepdims=True)
        eg2 = jnp.where(sub == j1, neg, eg)
        a2 = jnp.max(eg2, axis=0, keepdims=True)
        j2 = jnp.min(jnp.where(eg2 == a2, sub, EXPERTS_PER_GROUP), axis=0, keepdims=True)
        if g == 0:
            v1, v2, i1, i2 = a1, a2, j1, j2
        else:
            chosen = g_idx == g
            v1, v2 = jnp.where(chosen, a1, v1), jnp.where(chosen, a2, v2)
            i1, i2 = jnp.where(chosen, j1, i1), jnp.where(chosen, j2, i2)
    e21 = jnp.exp(v2 - v1)
    w1 = p_g / (1.0 + e21)
    w2 = p_g * e21 / (1.0 + e21)
    e1 = g_idx * EXPERTS_PER_GROUP + i1
    e2 = g_idx * EXPERTS_PER_GROUP + i2
    rows8 = jnp.where(sub == 0, e1.astype(F32),
                      jnp.where(sub == 1, e2.astype(F32),
                                jnp.where(sub == 2, w1, jnp.where(sub == 3, w2, 0.0))))
    route_t = jnp.concatenate([rows8, jnp.zeros((LANES - EXPERTS_PER_GROUP, n), F32)], axis=0)
    expert_row = lax.broadcasted_iota(I32, (LANES, n), 0)
    onehot_t = ((expert_row == e1) | (expert_row == e2)).astype(BF16)
    cnt = lax.dot_general(jnp.ones((8, n), BF16), onehot_t, (((1,), (1,)), ((), ())),
                          preferred_element_type=F32)
    return jnp.transpose(route_t), cnt


def _mix(merged, xf, wo, g, wr, br, chunk, t):
    n_tiles = t // TM_MIX
    first = chunk * n_tiles
    cnt_rows = TM_MIX // TM_CNT * 8
    full = lambda shape: pl.BlockSpec(shape, lambda i: (0, 0), pipeline_mode=pl.Buffered(1))
    tile = lambda w=D_MODEL: pl.BlockSpec((TM_MIX, w), lambda i: (i, 0))
    src = lambda c=0: pl.BlockSpec((TM_MIX, D_MODEL), lambda i: (first + i, c))
    return pl.pallas_call(
        _mix_kernel,
        name="merge_router",
        grid=(n_tiles,),
        in_specs=[
            src(), src(),
            full((D_MODEL, D_MODEL)), full((1, D_MODEL)),
            full((D_MODEL, 2 * LANES)), full((1, LANES)),
        ],
        out_specs=[tile(PACKED), tile(PACKED), tile(LANES),
                   pl.BlockSpec((cnt_rows, LANES), lambda i: (i, 0))],
        out_shape=[
            jax.ShapeDtypeStruct((t, PACKED), I32),
            jax.ShapeDtypeStruct((t, PACKED), I32),
            jax.ShapeDtypeStruct((t, LANES), F32),
            jax.ShapeDtypeStruct((n_tiles * cnt_rows, LANES), F32),
        ],
        compiler_params=pltpu.CompilerParams(
            dimension_semantics=("arbitrary",), vmem_limit_bytes=VMEM_LIMIT),
    )(merged, xf, wo, g, wr, br)


def _pos_kernel(route_ref, base_ref, pos_ref):
    lane = lax.broadcasted_iota(I32, (TM_CNT, LANES), 1)
    r = lax.broadcasted_iota(I32, (TM_CNT, TM_CNT), 0)
    c = lax.broadcasted_iota(I32, (TM_CNT, TM_CNT), 1)
    lower = (c < r).astype(BF16)
    subs = [slice(s * TM_CNT, (s + 1) * TM_CNT) for s in range(POS_TILES)]
    routes = [route_ref[rs, :] for rs in subs]
    e1 = [jnp.sum(jnp.where(lane == 0, rt, 0.0), axis=-1, keepdims=True).astype(I32) for rt in routes]
    e2 = [jnp.sum(jnp.where(lane == 1, rt, 0.0), axis=-1, keepdims=True).astype(I32) for rt in routes]
    onehot = [((lane == a) | (lane == b)).astype(BF16) for a, b in zip(e1, e2)]
    before = [jnp.dot(lower, oh, preferred_element_type=F32) + base_ref[s]
              for s, oh in enumerate(onehot)]
    for s, rs in enumerate(subs):
        p1 = jnp.sum(jnp.where(lane == e1[s], before[s], 0.0), axis=-1, keepdims=True)
        p2 = jnp.sum(jnp.where(lane == e2[s], before[s], 0.0), axis=-1, keepdims=True)
        packed = jnp.where(lane == 0, p1, jnp.where(lane == 1, p2, 0.0))
        pos_ref[:, rs] = jnp.transpose(packed)[0:TOP_K, :].astype(I32)


def _positions(route, base):
    t = route.shape[0]
    n_steps = t // (TM_CNT * POS_TILES)
    return pl.pallas_call(
        _pos_kernel,
        name="positions",
        grid=(n_steps,),
        in_specs=[
            pl.BlockSpec((TM_CNT * POS_TILES, LANES), lambda i: (i, 0)),
            pl.BlockSpec((POS_TILES, 1, LANES), lambda i: (i, 0, 0)),
        ],
        out_specs=pl.BlockSpec((TOP_K, TM_CNT * POS_TILES), lambda i: (0, i)),
        out_shape=jax.ShapeDtypeStruct((TOP_K, t), I32),
        compiler_params=pltpu.CompilerParams(dimension_semantics=("arbitrary",)),
    )(route, base)


def _sc_mesh():
    return plsc.VectorSubcoreMesh(core_axis_name="c", subcore_axis_name="s",
                                  num_cores=SC_CORES, num_subcores=SC_SUBCORES)


def _sc_worker():
    return lax.axis_index("s") * SC_CORES + lax.axis_index("c")


def _dispatch(pos, hp, rows):
    t = hp.shape[0]
    per_w = t // SC_WORKERS
    n_ch = per_w // SC_CHUNK
    pos4 = pos.reshape(TOP_K, SC_WORKERS, n_ch, SC_CHUNK)

    @functools.partial(
        pl.kernel, mesh=_sc_mesh(),
        out_type=jax.ShapeDtypeStruct((rows, PACKED), I32),
        scratch_types=[pltpu.VMEM((TOP_K, n_ch, SC_CHUNK), I32),
                       pltpu.VMEM((SC_CHUNK, PACKED), I32)])
    def scatter(hp_hbm, pos_hbm, xs_hbm, idx_v, rows_v):
        wid = _sc_worker()
        for k in range(TOP_K):
            pltpu.sync_copy(pos_hbm.at[k, wid], idx_v.at[k])

        def body(c, carry):
            start = pl.multiple_of(wid * per_w + c * SC_CHUNK, SC_CHUNK)
            pltpu.sync_copy(hp_hbm.at[pl.ds(start, SC_CHUNK)], rows_v)
            for k in range(TOP_K):
                pltpu.sync_copy(rows_v, xs_hbm.at[idx_v.at[k, c]])
            return carry

        lax.fori_loop(0, n_ch, body, 0)

    return scatter(hp, pos4)


def _gather_rows(table, idx):
    n = idx.shape[0]
    per_w = n // SC_WORKERS
    n_ch = per_w // SC_CHUNK
    idx3 = idx.reshape(SC_WORKERS, n_ch, SC_CHUNK)

    @functools.partial(
        pl.kernel, mesh=_sc_mesh(),
        out_type=jax.ShapeDtypeStruct((n, PACKED), I32),
        scratch_types=[pltpu.VMEM((n_ch, SC_CHUNK), I32),
                       pltpu.VMEM((SC_CHUNK, PACKED), I32)])
    def gather(table_hbm, idx_hbm, out_hbm, idx_v, rows_v):
        wid = _sc_worker()
        pltpu.sync_copy(idx_hbm.at[wid], idx_v)

        def body(c, carry):
            start = pl.multiple_of(wid * per_w + c * SC_CHUNK, SC_CHUNK)
            pltpu.sync_copy(table_hbm.at[idx_v.at[c]], rows_v)
            pltpu.sync_copy(rows_v, out_hbm.at[pl.ds(start, SC_CHUNK)])
            return carry

        lax.fori_loop(0, n_ch, body, 0)

    return gather(table, idx3)


def _expert_kernel(te_ref, nx_ref, sl_ref, ts_ref, tv_ref, xs_ref, wg_hbm, wu_hbm, wd_hbm, o_ref,
                   wg_s, wu_s, wd_s, sem):
    del ts_ref
    i = pl.program_id(0)
    n_valid = tv_ref[i]
    expert = te_ref[i]
    slot = sl_ref[i]

    def weight_copies(e, s):
        return (pltpu.make_async_copy(wg_hbm.at[e], wg_s.at[s], sem.at[s, 0]),
                pltpu.make_async_copy(wu_hbm.at[e], wu_s.at[s], sem.at[s, 1]),
                pltpu.make_async_copy(wd_hbm.at[e], wd_s.at[s], sem.at[s, 2]))

    @pl.when(i == 0)
    def _():
        for copy in weight_copies(expert, slot):
            copy.start()

    @pl.when(jnp.logical_or(i == 0, expert != te_ref[jnp.maximum(i - 1, 0)]))
    def _():
        for copy in weight_copies(expert, slot):
            copy.wait()
        nxt = nx_ref[i]

        @pl.when(nxt >= 0)
        def _():
            for copy in weight_copies(nxt, 1 - slot):
                copy.start()

    def mlp(n_sub):
        wg, wu, wd = wg_s.at[slot], wu_s.at[slot], wd_s.at[slot]
        subs = [slice(s * SUB_EXP, (s + 1) * SUB_EXP) for s in range(n_sub)]
        xin = []
        for r in subs:
            rid = r.start + lax.broadcasted_iota(I32, (SUB_EXP, PACKED), 0)
            lo, hi = _unpack_bf16_pairs(jnp.where(rid < n_valid, xs_ref[r, :], 0))
            xin.append((lo.astype(BF16), hi.astype(BF16)))
        ab = [(jnp.dot(lo, wg[:PACKED, :], preferred_element_type=F32)
               + jnp.dot(hi, wg[PACKED:, :], preferred_element_type=F32),
               jnp.dot(lo, wu[:PACKED, :], preferred_element_type=F32)
               + jnp.dot(hi, wu[PACKED:, :], preferred_element_type=F32)) for lo, hi in xin]
        for r, (ai, bi) in zip(subs, ab):
            hm = (ai * jax.nn.sigmoid(ai) * bi).astype(BF16)
            o_ref[r, :] = _pack_bf16_pairs(jnp.dot(hm, wd[...], preferred_element_type=F32))
        if n_sub * SUB_EXP < TM_EXP:
            o_ref[n_sub * SUB_EXP:, :] = jnp.zeros((TM_EXP - n_sub * SUB_EXP, PACKED), I32)

    n_subs = TM_EXP // SUB_EXP
    for n_sub in range(n_subs + 1):
        lo_rows = (n_sub - 1) * SUB_EXP if n_sub else -1
        in_range = jnp.logical_and(n_valid > lo_rows, n_valid <= n_sub * SUB_EXP)
        pl.when(in_range)(functools.partial(mlp, n_sub))


def _experts(tile_expert, next_expert, tile_slot, tile_src, tile_valid, xs, wg, wu, wd):
    rows = xs.shape[0]
    hbm = pl.BlockSpec(memory_space=pl.ANY)
    grid_spec = pltpu.PrefetchScalarGridSpec(
        num_scalar_prefetch=5,
        grid=(rows // TM_EXP,),
        in_specs=[pl.BlockSpec((TM_EXP, PACKED), lambda i, te, nx, sl, ts, tv: (ts[i], 0)),
                  hbm, hbm, hbm],
        out_specs=pl.BlockSpec((TM_EXP, PACKED), lambda i, te, nx, sl, ts, tv: (i, 0)),
        scratch_shapes=[
            pltpu.VMEM((2, D_MODEL, D_FF), BF16), pltpu.VMEM((2, D_MODEL, D_FF), BF16),
            pltpu.VMEM((2, D_FF, D_MODEL), BF16),
            pltpu.SemaphoreType.DMA((2, 3)),
        ],
    )
    return pl.pallas_call(
        _expert_kernel,
        name="experts",
        grid_spec=grid_spec,
        out_shape=jax.ShapeDtypeStruct((rows, PACKED), I32),
        compiler_params=pltpu.CompilerParams(
            dimension_semantics=("arbitrary",), vmem_limit_bytes=VMEM_LIMIT),
    )(tile_expert, next_expert, tile_slot, tile_src, tile_valid, xs, wg, wu, wd)


def _combine_kernel(y1_ref, y2_ref, route_ref, xmid_ref, g_ref, *rest):
    o_ref = rest[-1]
    route = route_ref[...]
    lane = lax.broadcasted_iota(I32, route.shape, 1)
    w1 = jnp.sum(jnp.where(lane == 2, route, 0.0), axis=-1, keepdims=True)
    w2 = jnp.sum(jnp.where(lane == 3, route, 0.0), axis=-1, keepdims=True)
    lo1, hi1 = _unpack_bf16_pairs(y1_ref[...])
    lo2, hi2 = _unpack_bf16_pairs(y2_ref[...])
    lox, hix = _unpack_bf16_pairs(xmid_ref[...])
    x_out = jnp.concatenate([lox + (lo1 * w1 + lo2 * w2), hix + (hi1 * w1 + hi2 * w2)], axis=1)
    o_ref[...] = _rms(x_out, g_ref[...])


def _combine(yg, route, xmid, g, chunk, t_total, out_prev):
    t = xmid.shape[0]
    n_tiles = t // TM_CMB
    first = chunk * n_tiles
    in_specs = [
        pl.BlockSpec((TM_CMB, PACKED), lambda i: (i, 0)),
        pl.BlockSpec((TM_CMB, PACKED), lambda i: (n_tiles + i, 0)),
        pl.BlockSpec((TM_CMB, LANES), lambda i: (i, 0)),
        pl.BlockSpec((TM_CMB, PACKED), lambda i: (i, 0)),
        pl.BlockSpec((1, D_MODEL), lambda i: (0, 0)),
    ]
    args = [yg, yg, route, xmid, g]
    aliases = {}
    if out_prev is not None:
        in_specs.append(pl.BlockSpec(memory_space=pl.ANY))
        aliases = {len(args): 0}
        args.append(out_prev)
    return pl.pallas_call(
        _combine_kernel,
        name="combine",
        grid=(n_tiles,),
        in_specs=in_specs,
        out_specs=pl.BlockSpec((TM_CMB, D_MODEL), lambda i: (first + i, 0)),
        out_shape=jax.ShapeDtypeStruct((t_total, D_MODEL), F32),
        input_output_aliases=aliases,
        compiler_params=pltpu.CompilerParams(
            dimension_semantics=("arbitrary",), vmem_limit_bytes=VMEM_LIMIT),
    )(*args)


def _split_bf16(w):
    hi = w.astype(BF16)
    lo = (w - hi.astype(F32)).astype(BF16)
    return hi, lo


def kernel(x, norm_mix, w_in, conv_w, conv_b, w_a_out, sinks, w_b_out, w_o, norm_ffn, w_group,
           b_group, w_expert, b_expert, w_gate, w_up, w_down, norm_final):
    bsz, seq, d = x.shape
    t = bsz * seq
    assert d == D_MODEL and seq % TM_PROJ == 0 and seq % TQ_ATTN == 0
    xf = x.reshape(t, d)
    row = lambda v: v.reshape(1, -1)

    assert w_in.shape == (D_MODEL, IN_COLS)
    za, proj, w_gate, w_up, w_down = _inproj(xf, row(norm_mix), w_in.astype(BF16), conv_w,
                                             row(conv_b), w_a_out.astype(BF16), seq,
                                             w_gate, w_up, w_down)
    merged = _attention(proj, za, sinks, w_b_out.astype(BF16), seq)

    pad = LANES - N_GROUPS - N_EXPERTS
    w_r = jnp.concatenate([w_expert, w_group, jnp.zeros((d, pad), F32)], axis=1)
    b_r = jnp.concatenate([b_expert, b_group, jnp.zeros((pad,), F32)]).reshape(1, LANES)
    wr = jnp.concatenate(_split_bf16(w_r), axis=1)
    wo = w_o.astype(BF16)

    t_chunk = t // MOE_CHUNKS
    assert t == t_chunk * MOE_CHUNKS and all(
        t_chunk % step == 0 for step in (TM_MIX, TM_CNT * POS_TILES, TM_CMB, SC_WORKERS * SC_CHUNK))
    out = None
    for chunk in range(MOE_CHUNKS):
        xmid, h2, route, cnt = _mix(merged, xf, wo, row(norm_ffn), wr, b_r, chunk, t_chunk)
        out = _moe_chunk(xmid, h2, route, cnt, w_gate, w_up, w_down, row(norm_final), chunk, t, out)
    return out.reshape(bsz, seq, d)


def _moe_chunk(xmid, h2, route, cnt, w_gate, w_up, w_down, g_final, chunk, t_total, out_prev):
    t = xmid.shape[0]
    n_tiles = t // TM_CNT
    cnt = cnt.reshape(n_tiles, 8, LANES)[:, 0, :N_EXPERTS].astype(I32)
    totals = jnp.sum(cnt, axis=0)
    tiles_e = (totals + TM_EXP - 1) // TM_EXP
    tile_end = jnp.cumsum(tiles_e)
    offset = (tile_end - tiles_e) * TM_EXP
    base = offset[None, :] + jnp.cumsum(cnt, axis=0) - cnt
    base = jnp.pad(base, ((0, 0), (0, LANES - N_EXPERTS))).astype(F32).reshape(n_tiles, 1, LANES)
    rows = t * TOP_K + N_EXPERTS * TM_EXP
    n_active = tile_end[-1]
    tile_id = jnp.arange(rows // TM_EXP, dtype=I32)
    tile_src = jnp.minimum(tile_id, n_active - 1)
    tile_expert = jnp.sum((tile_src[:, None] >= tile_end[None, :]).astype(I32), axis=1)
    tile_expert = jnp.minimum(tile_expert, N_EXPERTS - 1)
    row_in_expert = (tile_id - (tile_end - tiles_e)[tile_expert]) * TM_EXP
    tile_valid = jnp.clip(totals[tile_expert] - row_in_expert, 0, TM_EXP)
    tile_valid = jnp.where(tile_id < n_active, tile_valid, 0).astype(I32)
    after = tile_end[tile_expert]
    next_expert = jnp.where(after < n_active, tile_expert[jnp.minimum(after, n_active - 1)], -1)
    first_of_expert = jnp.concatenate(
        [jnp.ones((1,), I32), (tile_expert[1:] != tile_expert[:-1]).astype(I32)])
    tile_slot = (jnp.cumsum(first_of_expert) - 1) % 2

    pos = _positions(route, base)
    xs = _dispatch(pos, h2, rows)
    ys = _experts(tile_expert.astype(I32), next_expert.astype(I32), tile_slot.astype(I32),
                  tile_src.astype(I32), tile_valid, xs, w_gate, w_up, w_down)
    yg = _gather_rows(ys, pos.reshape(TOP_K * t))
    return _combine(yg, route, xmid, g_final, chunk, t_total, out_prev)
```

```python
import functools
import math

import jax
import jax.numpy as jnp
from jax import lax
from jax.experimental import pallas as pl
from jax.experimental.pallas import tpu as pltpu
from jax.experimental.pallas import tpu_sc as plsc

F32 = jnp.float32
BF16 = jnp.bfloat16
I32 = jnp.int32

D_MODEL = 1024
HEAD_DIM = 64
N_HEADS = 16
N_KV_HEADS = 4
GROUP = N_HEADS // N_KV_HEADS
KV_WIDTH = N_KV_HEADS * HEAD_DIM
WINDOW = 128
N_GROUPS = 4
EXPERTS_PER_GROUP = 8
N_EXPERTS = N_GROUPS * EXPERTS_PER_GROUP
TOP_K = 2
D_FF = 512
EPS = 1e-6
LANES = 128

COL_WIDTH = {"b": D_MODEL, "c": D_MODEL, "u": D_MODEL, "q": D_MODEL, "k": KV_WIDTH, "v": KV_WIDTH,
             "ga": D_MODEL, "gb": D_MODEL}
COL_START = dict(zip(COL_WIDTH, (sum(list(COL_WIDTH.values())[:i]) for i in range(len(COL_WIDTH)))))
IN_COLS = sum(COL_WIDTH.values())
REST_COLS = 2 * D_MODEL + 2 * KV_WIDTH
COL_GB, COL_Q = 0, 1
COL_K, COL_V = 2 * D_MODEL // KV_WIDTH, 2 * D_MODEL // KV_WIDTH + 1

TM_PROJ = 512
TQ_ATTN = 1024
TM_MIX = 1024
SUB_MIX = 256
TM_CNT = 512
POS_TILES = 8
MOE_CHUNKS = 2
TM_EXP = 512
SUB_EXP = 256
ROW_STEP_EXP = 128
TM_CMB = 1024
HALO_ROWS = 8
VMEM_LIMIT = 56 * 1024 * 1024
PACKED = D_MODEL // 2

SC_CORES = 2
SC_SUBCORES = 16
SC_WORKERS = SC_CORES * SC_SUBCORES
SC_CHUNK = 64


def _rms(x, g):
    r = lax.rsqrt(jnp.mean(x * x, axis=-1, keepdims=True) + EPS)
    return (x * r) * g


def _pack_bf16_pairs(x):
    n = x.shape[1] // 2
    lo = lax.bitcast_convert_type(x[:, :n].astype(BF16).astype(F32), I32)
    hi = lax.bitcast_convert_type(x[:, n:].astype(BF16).astype(F32), I32)
    return (hi & jnp.int32(-65536)) | lax.shift_right_logical(lo, 16)


def _unpack_bf16_pairs(p):
    lo = lax.bitcast_convert_type(lax.shift_left(p, 16), F32)
    hi = lax.bitcast_convert_type(p & jnp.int32(-65536), F32)
    return lo, hi


def _inproj_kernel(x_ref, g_ref, w_ref, cw_ref, cb_ref, wa_ref,
                   eg_ref, eu_ref, ed_ref,
                   za_ref, proj_ref, eg_out, eu_out, ed_out, halo_ref, *, tiles_per_seq):
    i = pl.program_id(0)
    eg_out[...] = eg_ref[...].astype(BF16)
    eu_out[...] = eu_ref[...].astype(BF16)
    ed_out[...] = ed_ref[...].astype(BF16)
    h = _rms(x_ref[...], g_ref[...]).astype(BF16)
    col = lambda a, b=None: slice(COL_START[a], COL_START[b or a] + COL_WIDTH[b or a])
    pcu = jnp.dot(h, w_ref[:, col("c", "u")], preferred_element_type=F32)
    pb = jnp.dot(h, w_ref[:, col("b")], preferred_element_type=F32)
    pga = jnp.dot(h, w_ref[:, col("ga")], preferred_element_type=F32)
    proj_ref[:, :D_MODEL] = jnp.dot(h, w_ref[:, col("gb")], preferred_element_type=F32).astype(BF16)
    proj_ref[:, D_MODEL:] = jnp.dot(h, w_ref[:, col("q", "v")],
                                    preferred_element_type=F32).astype(BF16)
    cu = pcu[:, :D_MODEL] * pcu[:, D_MODEL:]
    first = (i % tiles_per_seq) == 0
    hist = jnp.where(first, 0.0, halo_ref[...])
    prev1 = hist[HALO_ROWS - 1:HALO_ROWS]
    prev2 = hist[HALO_ROWS - 2:HALO_ROWS - 1]
    halo_ref[...] = cu[TM_PROJ - HALO_ROWS:, :]
    row = lax.broadcasted_iota(I32, cu.shape, 0)
    cu1 = jnp.where(row == 0, prev1, pltpu.roll(cu, 1, 0))
    cu2 = jnp.where(row == 0, prev2, jnp.where(row == 1, prev1, pltpu.roll(cu, 2, 0)))
    cw = cw_ref[...]
    y = cw[0:1] * cu2 + cw[1:2] * cu1 + cw[2:3] * cu + cb_ref[...]
    ya = (pb * y).astype(BF16)
    z = jnp.dot(ya, wa_ref[...], preferred_element_type=F32)
    za_ref[...] = (jax.nn.sigmoid(pga) * z).astype(BF16)


def _inproj(xf, g, w_in, conv_w, conv_b, wa, seq, w_gate, w_up, w_down):
    t = xf.shape[0]
    n_tiles = t // TM_PROJ
    const = lambda shape: pl.BlockSpec(shape, lambda i: (0, 0), pipeline_mode=pl.Buffered(1))
    slabs = [w.reshape(-1, w.shape[-1]) for w in (w_gate, w_up, w_down)]
    slab_rows = [s.shape[0] // n_tiles for s in slabs]
    assert all(s.shape[0] == r * n_tiles and r % 16 == 0 for s, r in zip(slabs, slab_rows))
    slab_specs = [pl.BlockSpec((r, s.shape[1]), lambda i: (i, 0)) for s, r in zip(slabs, slab_rows)]
    outs = pl.pallas_call(
        functools.partial(_inproj_kernel, tiles_per_seq=seq // TM_PROJ),
        name="inproj_conv",
        grid=(n_tiles,),
        in_specs=[
            pl.BlockSpec((TM_PROJ, D_MODEL), lambda i: (i, 0)),
            const((1, D_MODEL)),
            const((D_MODEL, IN_COLS)),
            const((3, D_MODEL)), const((1, D_MODEL)), const((D_MODEL, D_MODEL)),
        ] + slab_specs,
        out_specs=[pl.BlockSpec((TM_PROJ, D_MODEL), lambda i: (i, 0)),
                   pl.BlockSpec((TM_PROJ, REST_COLS), lambda i: (i, 0))] + slab_specs,
        out_shape=[jax.ShapeDtypeStruct((t, D_MODEL), BF16),
                   jax.ShapeDtypeStruct((t, REST_COLS), BF16)]
                  + [jax.ShapeDtypeStruct(s.shape, BF16) for s in slabs],
        scratch_shapes=[pltpu.VMEM((HALO_ROWS, D_MODEL), F32)],
        compiler_params=pltpu.CompilerParams(
            dimension_semantics=("arbitrary",), vmem_limit_bytes=VMEM_LIMIT),
    )(xf, g, w_in, conv_w, conv_b, wa, *slabs)
    za, proj, eg, eu, ed = outs
    return za, proj, eg.reshape(w_gate.shape), eu.reshape(w_up.shape), ed.reshape(w_down.shape)


def _attn_kernel(sink_ref, q_ref, k_ref, v_ref, kp_ref, vp_ref, gb_ref, za_ref, wb_ref, o_ref,
                 *, tiles_per_seq):
    first_tile = (pl.program_id(0) % tiles_per_seq) == 0
    ks = lax.broadcasted_iota(I32, (WINDOW, WINDOW), 0)
    qq = lax.broadcasted_iota(I32, (WINDOW, WINDOW), 1)
    own = ks <= qq
    dist = jnp.where(own, qq - ks, qq - ks + WINDOW).astype(F32)
    visible0 = jnp.logical_or(own, jnp.logical_not(first_tile))
    log2e = math.log2(math.e)
    c_scale = log2e / math.sqrt(HEAD_DIM)
    nt = (((1,), (1,)), ((), ()))
    zk = jnp.zeros((2 * WINDOW, HEAD_DIM), BF16)

    def transposed(v_blk):
        return jnp.transpose(v_blk.astype(F32)).astype(BF16)

    def project(rows_p, attn_blk):
        yb = jnp.dot(attn_blk, wb_ref[...], preferred_element_type=F32)
        zb = jax.nn.sigmoid(gb_ref[rows_p, :].astype(F32)) * yb
        o_ref[rows_p, :] = (za_ref[rows_p, :].astype(F32) + zb).astype(BF16)

    pending = None
    prev_k = kp_ref[...]
    prev_vt = transposed(vp_ref[...])
    for sb in range(TQ_ATTN // WINDOW):
        rows = slice(sb * WINDOW, (sb + 1) * WINDOW)
        cur_k = k_ref[rows, :]
        cur_vt = transposed(v_ref[rows, :])
        scores, vcats = [], []
        for kh in range(N_KV_HEADS):
            cols = slice(kh * HEAD_DIM, (kh + 1) * HEAD_DIM)
            kcat = jnp.concatenate([prev_k[:, cols], cur_k[:, cols]], axis=0)
            vcats.append(jnp.concatenate([prev_vt[cols, :], cur_vt[cols, :]], axis=1))
            qg = jnp.concatenate([q_ref[rows, (2 * kh) * LANES:(2 * kh + 1) * LANES],
                                  q_ref[rows, (2 * kh + 1) * LANES:(2 * kh + 2) * LANES]], axis=0)
            k_pad = jnp.concatenate([jnp.concatenate([kcat, zk], axis=1),
                                     jnp.concatenate([zk, kcat], axis=1)], axis=0)
            scores.append(lax.dot_general(k_pad, qg, nt, preferred_element_type=F32))
        if pending is not None:
            project(*pending)
        probs, rdens = [], []
        for kh in range(N_KV_HEADS):
            for pos in range(2):
                pr, rd = [], []
                for half in range(2):
                    h = kh * GROUP + 2 * half + pos
                    slope = 2.0 ** (-8.0 * (h + 1) / N_HEADS)
                    qcols = slice(half * WINDOW, (half + 1) * WINDOW)
                    krow = pos * 2 * WINDOW
                    st = scores[kh]
                    s = (jnp.where(own, st[krow + WINDOW:krow + 2 * WINDOW, qcols],
                                   st[krow:krow + WINDOW, qcols]) * c_scale
                         - (slope * log2e) * dist)
                    if sb == 0:
                        s = jnp.where(visible0, s, -jnp.inf)
                    m = jnp.max(s, axis=0, keepdims=True)
                    p = jnp.exp2(s - m)
                    den = jnp.sum(p, axis=0, keepdims=True) + jnp.exp2(sink_ref[h] * log2e - m)
                    rd.append(1.0 / den)
                    pr.append(jnp.concatenate(
                        [jnp.where(own, 0.0, p).astype(BF16), jnp.where(own, p, 0.0).astype(BF16)],
                        axis=0))
                probs.append(jnp.concatenate(pr, axis=1))
                rdens.append(jnp.concatenate(rd, axis=1))
        out_t = [None] * N_HEADS
        for kh in range(N_KV_HEADS):
            for pos in range(2):
                o2 = jnp.dot(vcats[kh], probs[2 * kh + pos], preferred_element_type=F32)
                o2 = o2 * rdens[2 * kh + pos]
                out_t[kh * GROUP + pos] = o2[:, :WINDOW]
                out_t[kh * GROUP + 2 + pos] = o2[:, WINDOW:]
        pending = (rows, jnp.transpose(jnp.concatenate(out_t, axis=0)).astype(BF16))
        prev_k, prev_vt = cur_k, cur_vt
    project(*pending)


def _attention(proj, za, sinks, wb, seq):
    t = proj.shape[0]
    sub = TQ_ATTN // WINDOW
    return pl.pallas_call(
        functools.partial(_attn_kernel, tiles_per_seq=seq // TQ_ATTN),
        name="swattn",
        grid=(t // TQ_ATTN,),
        in_specs=[
            pl.BlockSpec(memory_space=pltpu.SMEM),
            pl.BlockSpec((TQ_ATTN, D_MODEL), lambda i: (i, COL_Q)),
            pl.BlockSpec((TQ_ATTN, KV_WIDTH), lambda i: (i, COL_K)),
            pl.BlockSpec((TQ_ATTN, KV_WIDTH), lambda i: (i, COL_V)),
            pl.BlockSpec((WINDOW, KV_WIDTH), lambda i: (jnp.maximum(i * sub - 1, 0), COL_K)),
            pl.BlockSpec((WINDOW, KV_WIDTH), lambda i: (jnp.maximum(i * sub - 1, 0), COL_V)),
            pl.BlockSpec((TQ_ATTN, D_MODEL), lambda i: (i, COL_GB)),
            pl.BlockSpec((TQ_ATTN, D_MODEL), lambda i: (i, 0)),
            pl.BlockSpec((D_MODEL, D_MODEL), lambda i: (0, 0), pipeline_mode=pl.Buffered(1)),
        ],
        out_specs=pl.BlockSpec((TQ_ATTN, D_MODEL), lambda i: (i, 0)),
        out_shape=jax.ShapeDtypeStruct((t, D_MODEL), BF16),
        compiler_params=pltpu.CompilerParams(
            dimension_semantics=("arbitrary",), vmem_limit_bytes=VMEM_LIMIT),
    )(sinks, proj, proj, proj, proj, proj, proj, za, wb)


def _mix_kernel(merged_ref, x_ref, wo_ref, g_ref, wr_ref, br_ref,
                xmid_ref, h_ref, route_ref, cnt_ref):
    subs = [slice(s * SUB_MIX, (s + 1) * SUB_MIX) for s in range(TM_MIX // SUB_MIX)]
    xm = [x_ref[r, :] + jnp.dot(merged_ref[r, :], wo_ref[...], preferred_element_type=F32)
          for r in subs]
    hs = []
    for r, v in zip(subs, xm):
        xmid_ref[r, :] = _pack_bf16_pairs(v)
        h = _rms(v, g_ref[...])
        h_ref[r, :] = _pack_bf16_pairs(h)
        hs.append(h)
    wr = wr_ref[...]
    logits = []
    for h in hs:
        h_hi = h.astype(BF16)
        h_lo = (h - h_hi.astype(F32)).astype(BF16)
        both = jnp.dot(h_hi, wr, preferred_element_type=F32)
        logits.append(both[:, :LANES] + both[:, LANES:]
                      + jnp.dot(h_lo, wr[:, :LANES], preferred_element_type=F32) + br_ref[...])
    subs_per_cnt = TM_CNT // SUB_MIX
    cnts = [jnp.zeros((8, LANES), F32) for _ in range(TM_MIX // TM_CNT)]
    for s, (r, lg) in enumerate(zip(subs, logits)):
        route, cnt = _route(lg)
        route_ref[r, :] = route
        cnts[s // subs_per_cnt] = cnts[s // subs_per_cnt] + cnt
    for c, cnt in enumerate(cnts):
        cnt_ref[c * 8:(c + 1) * 8, :] = cnt


def _route(logits):
    n = logits.shape[0]
    lt = jnp.transpose(logits)
    sub = lax.broadcasted_iota(I32, (EXPERTS_PER_GROUP, n), 0)
    neg = -jnp.inf
    gl = jnp.where(sub < N_GROUPS, lt[N_EXPERTS:N_EXPERTS + EXPERTS_PER_GROUP], neg)
    gmax = jnp.max(gl, axis=0, keepdims=True)
    g_idx = jnp.min(jnp.where(gl == gmax, sub, EXPERTS_PER_GROUP), axis=0, keepdims=True)
    p_g = 1.0 / jnp.sum(jnp.exp(gl - gmax), axis=0, keepdims=True)
    v1 = v2 = i1 = i2 = None
    for g in range(N_GROUPS):
        eg = lt[g * EXPERTS_PER_GROUP:(g + 1) * EXPERTS_PER_GROUP]
        a1 = jnp.max(eg, axis=0, keepdims=True)
        j1 = jnp.min(jnp.where(eg == a1, sub, EXPERTS_PER_GROUP), axis=0, keepdims=True)
        eg2 = jnp.where(sub == j1, neg, eg)
        a2 = jnp.max(eg2, axis=0, keepdims=True)
        j2 = jnp.min(jnp.where(eg2 == a2, sub, EXPERTS_PER_GROUP), axis=0, keepdims=True)
        if g == 0:
            v1, v2, i1, i2 = a1, a2, j1, j2
        else:
            chosen = g_idx == g
            v1, v2 = jnp.where(chosen, a1, v1), jnp.where(chosen, a2, v2)
            i1, i2 = jnp.where(chosen, j1, i1), jnp.where(chosen, j2, i2)
    e21 = jnp.exp(v2 - v1)
    w1 = p_g / (1.0 + e21)
    w2 = p_g * e21 / (1.0 + e21)
    e1 = g_idx * EXPERTS_PER_GROUP + i1
    e2 = g_idx * EXPERTS_PER_GROUP + i2
    rows8 = jnp.where(sub == 0, e1.astype(F32),
                      jnp.where(sub == 1, e2.astype(F32),
                                jnp.where(sub == 2, w1, jnp.where(sub == 3, w2, 0.0))))
    route_t = jnp.concatenate([rows8, jnp.zeros((LANES - EXPERTS_PER_GROUP, n), F32)], axis=0)
    expert_row = lax.broadcasted_iota(I32, (LANES, n), 0)
    onehot_t = ((expert_row == e1) | (expert_row == e2)).astype(BF16)
    cnt = lax.dot_general(jnp.ones((8, n), BF16), onehot_t, (((1,), (1,)), ((), ())),
                          preferred_element_type=F32)
    return jnp.transpose(route_t), cnt


def _mix(merged, xf, wo, g, wr, br, chunk, t):
    n_tiles = t // TM_MIX
    first = chunk * n_tiles
    cnt_rows = TM_MIX // TM_CNT * 8
    full = lambda shape: pl.BlockSpec(shape, lambda i: (0, 0), pipeline_mode=pl.Buffered(1))
    tile = lambda w=D_MODEL: pl.BlockSpec((TM_MIX, w), lambda i: (i, 0))
    src = lambda c=0: pl.BlockSpec((TM_MIX, D_MODEL), lambda i: (first + i, c))
    return pl.pallas_call(
        _mix_kernel,
        name="merge_router",
        grid=(n_tiles,),
        in_specs=[
            src(), src(),
            full((D_MODEL, D_MODEL)), full((1, D_MODEL)),
            full((D_MODEL, 2 * LANES)), full((1, LANES)),
        ],
        out_specs=[tile(PACKED), tile(PACKED), tile(LANES),
                   pl.BlockSpec((cnt_rows, LANES), lambda i: (i, 0))],
        out_shape=[
            jax.ShapeDtypeStruct((t, PACKED), I32),
            jax.ShapeDtypeStruct((t, PACKED), I32),
            jax.ShapeDtypeStruct((t, LANES), F32),
            jax.ShapeDtypeStruct((n_tiles * cnt_rows, LANES), F32),
        ],
        compiler_params=pltpu.CompilerParams(
            dimension_semantics=("arbitrary",), vmem_limit_bytes=VMEM_LIMIT),
    )(merged, xf, wo, g, wr, br)


def _pos_kernel(route_ref, base_ref, pos_ref):
    lane = lax.broadcasted_iota(I32, (TM_CNT, LANES), 1)
    r = lax.broadcasted_iota(I32, (TM_CNT, TM_CNT), 0)
    c = lax.broadcasted_iota(I32, (TM_CNT, TM_CNT), 1)
    lower = (c < r).astype(BF16)
    subs = [slice(s * TM_CNT, (s + 1) * TM_CNT) for s in range(POS_TILES)]
    routes = [route_ref[rs, :] for rs in subs]
    e1 = [jnp.sum(jnp.where(lane == 0, rt, 0.0), axis=-1, keepdims=True).astype(I32) for rt in routes]
    e2 = [jnp.sum(jnp.where(lane == 1, rt, 0.0), axis=-1, keepdims=True).astype(I32) for rt in routes]
    onehot = [((lane == a) | (lane == b)).astype(BF16) for a, b in zip(e1, e2)]
    before = [jnp.dot(lower, oh, preferred_element_type=F32) + base_ref[s]
              for s, oh in enumerate(onehot)]
    for s, rs in enumerate(subs):
        p1 = jnp.sum(jnp.where(lane == e1[s], before[s], 0.0), axis=-1, keepdims=True)
        p2 = jnp.sum(jnp.where(lane == e2[s], before[s], 0.0), axis=-1, keepdims=True)
        packed = jnp.where(lane == 0, p1, jnp.where(lane == 1, p2, 0.0))
        pos_ref[:, rs] = jnp.transpose(packed)[0:TOP_K, :].astype(I32)


def _positions(route, base):
    t = route.shape[0]
    n_steps = t // (TM_CNT * POS_TILES)
    return pl.pallas_call(
        _pos_kernel,
        name="positions",
        grid=(n_steps,),
        in_specs=[
            pl.BlockSpec((TM_CNT * POS_TILES, LANES), lambda i: (i, 0)),
            pl.BlockSpec((POS_TILES, 1, LANES), lambda i: (i, 0, 0)),
        ],
        out_specs=pl.BlockSpec((TOP_K, TM_CNT * POS_TILES), lambda i: (0, i)),
        out_shape=jax.ShapeDtypeStruct((TOP_K, t), I32),
        compiler_params=pltpu.CompilerParams(dimension_semantics=("arbitrary",)),
    )(route, base)


def _sc_mesh():
    return plsc.VectorSubcoreMesh(core_axis_name="c", subcore_axis_name="s",
                                  num_cores=SC_CORES, num_subcores=SC_SUBCORES)


def _sc_worker():
    return lax.axis_index("s") * SC_CORES + lax.axis_index("c")


def _dispatch(pos, hp, rows):
    t = hp.shape[0]
    per_w = t // SC_WORKERS
    n_ch = per_w // SC_CHUNK
    pos4 = pos.reshape(TOP_K, SC_WORKERS, n_ch, SC_CHUNK)

    @functools.partial(
        pl.kernel, mesh=_sc_mesh(),
        out_type=jax.ShapeDtypeStruct((rows, PACKED), I32),
        scratch_types=[pltpu.VMEM((TOP_K, n_ch, SC_CHUNK), I32),
                       pltpu.VMEM((SC_CHUNK, PACKED), I32)])
    def scatter(hp_hbm, pos_hbm, xs_hbm, idx_v, rows_v):
        wid = _sc_worker()
        for k in range(TOP_K):
            pltpu.sync_copy(pos_hbm.at[k, wid], idx_v.at[k])

        def body(c, carry):
            start = pl.multiple_of(wid * per_w + c * SC_CHUNK, SC_CHUNK)
            pltpu.sync_copy(hp_hbm.at[pl.ds(start, SC_CHUNK)], rows_v)
            for k in range(TOP_K):
                pltpu.sync_copy(rows_v, xs_hbm.at[idx_v.at[k, c]])
            return carry

        lax.fori_loop(0, n_ch, body, 0)

    return scatter(hp, pos4)


def _gather_rows(table, idx):
    n = idx.shape[0]
    per_w = n // SC_WORKERS
    n_ch = per_w // SC_CHUNK
    idx3 = idx.reshape(SC_WORKERS, n_ch, SC_CHUNK)

    @functools.partial(
        pl.kernel, mesh=_sc_mesh(),
        out_type=jax.ShapeDtypeStruct((n, PACKED), I32),
        scratch_types=[pltpu.VMEM((n_ch, SC_CHUNK), I32),
                       pltpu.VMEM((SC_CHUNK, PACKED), I32)])
    def gather(table_hbm, idx_hbm, out_hbm, idx_v, rows_v):
        wid = _sc_worker()
        pltpu.sync_copy(idx_hbm.at[wid], idx_v)

        def body(c, carry):
            start = pl.multiple_of(wid * per_w + c * SC_CHUNK, SC_CHUNK)
            pltpu.sync_copy(table_hbm.at[idx_v.at[c]], rows_v)
            pltpu.sync_copy(rows_v, out_hbm.at[pl.ds(start, SC_CHUNK)])
            return carry

        lax.fori_loop(0, n_ch, body, 0)

    return gather(table, idx3)


def _expert_kernel(te_ref, nx_ref, sl_ref, ts_ref, tv_ref, xs_ref, wg_hbm, wu_hbm, wd_hbm, o_ref,
                   wg_s, wu_s, wd_s, sem):
    del ts_ref
    i = pl.program_id(0)
    n_valid = tv_ref[i]
    expert = te_ref[i]
    slot = sl_ref[i]

    def weight_copies(e, s):
        return (pltpu.make_async_copy(wg_hbm.at[e], wg_s.at[s], sem.at[s, 0]),
                pltpu.make_async_copy(wu_hbm.at[e], wu_s.at[s], sem.at[s, 1]),
                pltpu.make_async_copy(wd_hbm.at[e], wd_s.at[s], sem.at[s, 2]))

    @pl.when(i == 0)
    def _():
        for copy in weight_copies(expert, slot):
            copy.start()

    @pl.when(jnp.logical_or(i == 0, expert != te_ref[jnp.maximum(i - 1, 0)]))
    def _():
        for copy in weight_copies(expert, slot):
            copy.wait()
        nxt = nx_ref[i]

        @pl.when(nxt >= 0)
        def _():
            for copy in weight_copies(nxt, 1 - slot):
                copy.start()

    def mlp(n_rows):
        wg, wu, wd = wg_s.at[slot], wu_s.at[slot], wd_s.at[slot]
        subs = [slice(r0, min(r0 + SUB_EXP, n_rows)) for r0 in range(0, n_rows, SUB_EXP)]
        xin = []
        for r in subs:
            rid = r.start + lax.broadcasted_iota(I32, (r.stop - r.start, PACKED), 0)
            lo, hi = _unpack_bf16_pairs(jnp.where(rid < n_valid, xs_ref[r, :], 0))
            xin.append((lo.astype(BF16), hi.astype(BF16)))
        ab = [(jnp.dot(lo, wg[:PACKED, :], preferred_element_type=F32)
               + jnp.dot(hi, wg[PACKED:, :], preferred_element_type=F32),
               jnp.dot(lo, wu[:PACKED, :], preferred_element_type=F32)
               + jnp.dot(hi, wu[PACKED:, :], preferred_element_type=F32)) for lo, hi in xin]
        for r, (ai, bi) in zip(subs, ab):
            hm = (ai * jax.nn.sigmoid(ai) * bi).astype(BF16)
            o_ref[r, :] = _pack_bf16_pairs(jnp.dot(hm, wd[...], preferred_element_type=F32))
        if n_rows < TM_EXP:
            o_ref[n_rows:, :] = jnp.zeros((TM_EXP - n_rows, PACKED), I32)

    for n_rows in range(0, TM_EXP + 1, ROW_STEP_EXP):
        lo_rows = n_rows - ROW_STEP_EXP if n_rows else -1
        in_range = jnp.logical_and(n_valid > lo_rows, n_valid <= n_rows)
        pl.when(in_range)(functools.partial(mlp, n_rows))


def _experts(tile_expert, next_expert, tile_slot, tile_src, tile_valid, xs, wg, wu, wd):
    rows = xs.shape[0]
    hbm = pl.BlockSpec(memory_space=pl.ANY)
    grid_spec = pltpu.PrefetchScalarGridSpec(
        num_scalar_prefetch=5,
        grid=(rows // TM_EXP,),
        in_specs=[pl.BlockSpec((TM_EXP, PACKED), lambda i, te, nx, sl, ts, tv: (ts[i], 0)),
                  hbm, hbm, hbm],
        out_specs=pl.BlockSpec((TM_EXP, PACKED), lambda i, te, nx, sl, ts, tv: (i, 0)),
        scratch_shapes=[
            pltpu.VMEM((2, D_MODEL, D_FF), BF16), pltpu.VMEM((2, D_MODEL, D_FF), BF16),
            pltpu.VMEM((2, D_FF, D_MODEL), BF16),
            pltpu.SemaphoreType.DMA((2, 3)),
        ],
    )
    return pl.pallas_call(
        _expert_kernel,
        name="experts",
        grid_spec=grid_spec,
        out_shape=jax.ShapeDtypeStruct((rows, PACKED), I32),
        compiler_params=pltpu.CompilerParams(
            dimension_semantics=("arbitrary",), vmem_limit_bytes=VMEM_LIMIT),
    )(tile_expert, next_expert, tile_slot, tile_src, tile_valid, xs, wg, wu, wd)


def _combine_kernel(y1_ref, y2_ref, route_ref, xmid_ref, g_ref, *rest):
    o_ref = rest[-1]
    route = route_ref[...]
    lane = lax.broadcasted_iota(I32, route.shape, 1)
    w1 = jnp.sum(jnp.where(lane == 2, route, 0.0), axis=-1, keepdims=True)
    w2 = jnp.sum(jnp.where(lane == 3, route, 0.0), axis=-1, keepdims=True)
    lo1, hi1 = _unpack_bf16_pairs(y1_ref[...])
    lo2, hi2 = _unpack_bf16_pairs(y2_ref[...])
    lox, hix = _unpack_bf16_pairs(xmid_ref[...])
    x_out = jnp.concatenate([lox + (lo1 * w1 + lo2 * w2), hix + (hi1 * w1 + hi2 * w2)], axis=1)
    o_ref[...] = _rms(x_out, g_ref[...])


def _combine(yg, route, xmid, g, chunk, t_total, out_prev):
    t = xmid.shape[0]
    n_tiles = t // TM_CMB
    first = chunk * n_tiles
    in_specs = [
        pl.BlockSpec((TM_CMB, PACKED), lambda i: (i, 0)),
        pl.BlockSpec((TM_CMB, PACKED), lambda i: (n_tiles + i, 0)),
        pl.BlockSpec((TM_CMB, LANES), lambda i: (i, 0)),
        pl.BlockSpec((TM_CMB, PACKED), lambda i: (i, 0)),
        pl.BlockSpec((1, D_MODEL), lambda i: (0, 0)),
    ]
    args = [yg, yg, route, xmid, g]
    aliases = {}
    if out_prev is not None:
        in_specs.append(pl.BlockSpec(memory_space=pl.ANY))
        aliases = {len(args): 0}
        args.append(out_prev)
    return pl.pallas_call(
        _combine_kernel,
        name="combine",
        grid=(n_tiles,),
        in_specs=in_specs,
        out_specs=pl.BlockSpec((TM_CMB, D_MODEL), lambda i: (first + i, 0)),
        out_shape=jax.ShapeDtypeStruct((t_total, D_MODEL), F32),
        input_output_aliases=aliases,
        compiler_params=pltpu.CompilerParams(
            dimension_semantics=("arbitrary",), vmem_limit_bytes=VMEM_LIMIT),
    )(*args)


def _split_bf16(w):
    hi = w.astype(BF16)
    lo = (w - hi.astype(F32)).astype(BF16)
    return hi, lo


def kernel(x, norm_mix, w_in, conv_w, conv_b, w_a_out, sinks, w_b_out, w_o, norm_ffn, w_group,
           b_group, w_expert, b_expert, w_gate, w_up, w_down, norm_final):
    bsz, seq, d = x.shape
    t = bsz * seq
    assert d == D_MODEL and seq % TM_PROJ == 0 and seq % TQ_ATTN == 0
    xf = x.reshape(t, d)
    row = lambda v: v.reshape(1, -1)

    assert w_in.shape == (D_MODEL, IN_COLS)
    za, proj, w_gate, w_up, w_down = _inproj(xf, row(norm_mix), w_in.astype(BF16), conv_w,
                                             row(conv_b), w_a_out.astype(BF16), seq,
                                             w_gate, w_up, w_down)
    merged = _attention(proj, za, sinks, w_b_out.astype(BF16), seq)

    pad = LANES - N_GROUPS - N_EXPERTS
    w_r = jnp.concatenate([w_expert, w_group, jnp.zeros((d, pad), F32)], axis=1)
    b_r = jnp.concatenate([b_expert, b_group, jnp.zeros((pad,), F32)]).reshape(1, LANES)
    wr = jnp.concatenate(_split_bf16(w_r), axis=1)
    wo = w_o.astype(BF16)

    t_chunk = t // MOE_CHUNKS
    assert t == t_chunk * MOE_CHUNKS and all(
        t_chunk % step == 0 for step in (TM_MIX, TM_CNT * POS_TILES, TM_CMB, SC_WORKERS * SC_CHUNK))
    out = None
    for chunk in range(MOE_CHUNKS):
        xmid, h2, route, cnt = _mix(merged, xf, wo, row(norm_ffn), wr, b_r, chunk, t_chunk)
        out = _moe_chunk(xmid, h2, route, cnt, w_gate, w_up, w_down, row(norm_final), chunk, t, out)
    return out.reshape(bsz, seq, d)


def _moe_chunk(xmid, h2, route, cnt, w_gate, w_up, w_down, g_final, chunk, t_total, out_prev):
    t = xmid.shape[0]
    n_tiles = t // TM_CNT
    cnt = cnt.reshape(n_tiles, 8, LANES)[:, 0, :N_EXPERTS].astype(I32)
    totals = jnp.sum(cnt, axis=0)
    tiles_e = (totals + TM_EXP - 1) // TM_EXP
    tile_end = jnp.cumsum(tiles_e)
    offset = (tile_end - tiles_e) * TM_EXP
    base = offset[None, :] + jnp.cumsum(cnt, axis=0) - cnt
    base = jnp.pad(base, ((0, 0), (0, LANES - N_EXPERTS))).astype(F32).reshape(n_tiles, 1, LANES)
    rows = t * TOP_K + N_EXPERTS * TM_EXP
    n_active = tile_end[-1]
    tile_id = jnp.arange(rows // TM_EXP, dtype=I32)
    tile_src = jnp.minimum(tile_id, n_active - 1)
    tile_expert = jnp.sum((tile_src[:, None] >= tile_end[None, :]).astype(I32), axis=1)
    tile_expert = jnp.minimum(tile_expert, N_EXPERTS - 1)
    row_in_expert = (tile_id - (tile_end - tiles_e)[tile_expert]) * TM_EXP
    tile_valid = jnp.clip(totals[tile_expert] - row_in_expert, 0, TM_EXP)
    tile_valid = jnp.where(tile_id < n_active, tile_valid, 0).astype(I32)
    after = tile_end[tile_expert]
    next_expert = jnp.where(after < n_active, tile_expert[jnp.minimum(after, n_active - 1)], -1)
    first_of_expert = jnp.concatenate(
        [jnp.ones((1,), I32), (tile_expert[1:] != tile_expert[:-1]).astype(I32)])
    tile_slot = (jnp.cumsum(first_of_expert) - 1) % 2

    pos = _positions(route, base)
    xs = _dispatch(pos, h2, rows)
    ys = _experts(tile_expert.astype(I32), next_expert.astype(I32), tile_slot.astype(I32),
                  tile_src.astype(I32), tile_valid, xs, w_gate, w_up, w_down)
    yg = _gather_rows(ys, pos.reshape(TOP_K * t))
    return _combine(yg, route, xmid, g_final, chunk, t_total, out_prev)
```

```python
import functools
import math

import jax
import jax.numpy as jnp
from jax import lax
from jax.experimental import pallas as pl
from jax.experimental.pallas import tpu as pltpu
from jax.experimental.pallas import tpu_sc as plsc

F32 = jnp.float32
BF16 = jnp.bfloat16
I32 = jnp.int32

D_MODEL = 1024
HEAD_DIM = 64
N_HEADS = 16
N_KV_HEADS = 4
GROUP = N_HEADS // N_KV_HEADS
KV_WIDTH = N_KV_HEADS * HEAD_DIM
WINDOW = 128
N_GROUPS = 4
EXPERTS_PER_GROUP = 8
N_EXPERTS = N_GROUPS * EXPERTS_PER_GROUP
TOP_K = 2
D_FF = 512
EPS = 1e-6
LANES = 128

COL_WIDTH = {"b": D_MODEL, "c": D_MODEL, "u": D_MODEL, "q": D_MODEL, "k": KV_WIDTH, "v": KV_WIDTH,
             "ga": D_MODEL, "gb": D_MODEL}
COL_START = dict(zip(COL_WIDTH, (sum(list(COL_WIDTH.values())[:i]) for i in range(len(COL_WIDTH)))))
IN_COLS = sum(COL_WIDTH.values())
REST_COLS = 2 * D_MODEL + 2 * KV_WIDTH
COL_GB, COL_Q = 0, 1
COL_K, COL_V = 2 * D_MODEL // KV_WIDTH, 2 * D_MODEL // KV_WIDTH + 1

TM_PROJ = 512
TQ_ATTN = 1024
TM_MIX = 1024
SUB_MIX = 256
TM_CNT = 512
POS_TILES = 8
MOE_CHUNKS = 2
TM_EXP = 512
SUB_EXP = 256
TM_CMB = 1024
HALO_ROWS = 8
VMEM_LIMIT = 56 * 1024 * 1024
PACKED = D_MODEL // 2

SC_CORES = 2
SC_SUBCORES = 16
SC_WORKERS = SC_CORES * SC_SUBCORES
SC_CHUNK = 64


def _rms(x, g):
    r = lax.rsqrt(jnp.mean(x * x, axis=-1, keepdims=True) + EPS)
    return (x * r) * g


def _pack_bf16_pairs(x):
    n = x.shape[1] // 2
    lo = lax.bitcast_convert_type(x[:, :n].astype(BF16).astype(F32), I32)
    hi = lax.bitcast_convert_type(x[:, n:].astype(BF16).astype(F32), I32)
    return (hi & jnp.int32(-65536)) | lax.shift_right_logical(lo, 16)


def _unpack_bf16_pairs(p):
    lo = lax.bitcast_convert_type(lax.shift_left(p, 16), F32)
    hi = lax.bitcast_convert_type(p & jnp.int32(-65536), F32)
    return lo, hi


def _inproj_kernel(x_ref, g_ref, w_ref, cw_ref, cb_ref, wa_ref,
                   eg_ref, eu_ref, ed_ref,
                   za_ref, proj_ref, eg_out, eu_out, ed_out, halo_ref, *, tiles_per_seq):
    i = pl.program_id(0)
    eg_out[...] = eg_ref[...].astype(BF16)
    eu_out[...] = eu_ref[...].astype(BF16)
    ed_out[...] = ed_ref[...].astype(BF16)
    h = _rms(x_ref[...], g_ref[...]).astype(BF16)
    col = lambda a, b=None: slice(COL_START[a], COL_START[b or a] + COL_WIDTH[b or a])
    pcu = jnp.dot(h, w_ref[:, col("c", "u")], preferred_element_type=F32)
    pb = jnp.dot(h, w_ref[:, col("b")], preferred_element_type=F32)
    pga = jnp.dot(h, w_ref[:, col("ga")], preferred_element_type=F32)
    proj_ref[:, :D_MODEL] = jnp.dot(h, w_ref[:, col("gb")], preferred_element_type=F32).astype(BF16)
    proj_ref[:, D_MODEL:] = jnp.dot(h, w_ref[:, col("q", "v")],
                                    preferred_element_type=F32).astype(BF16)
    cu = pcu[:, :D_MODEL] * pcu[:, D_MODEL:]
    first = (i % tiles_per_seq) == 0
    hist = jnp.where(first, 0.0, halo_ref[...])
    prev1 = hist[HALO_ROWS - 1:HALO_ROWS]
    prev2 = hist[HALO_ROWS - 2:HALO_ROWS - 1]
    halo_ref[...] = cu[TM_PROJ - HALO_ROWS:, :]
    row = lax.broadcasted_iota(I32, cu.shape, 0)
    cu1 = jnp.where(row == 0, prev1, pltpu.roll(cu, 1, 0))
    cu2 = jnp.where(row == 0, prev2, jnp.where(row == 1, prev1, pltpu.roll(cu, 2, 0)))
    cw = cw_ref[...]
    y = cw[0:1] * cu2 + cw[1:2] * cu1 + cw[2:3] * cu + cb_ref[...]
    ya = (pb * y).astype(BF16)
    z = jnp.dot(ya, wa_ref[...], preferred_element_type=F32)
    za_ref[...] = (jax.nn.sigmoid(pga) * z).astype(BF16)


def _inproj(xf, g, w_in, conv_w, conv_b, wa, seq, w_gate, w_up, w_down):
    t = xf.shape[0]
    n_tiles = t // TM_PROJ
    const = lambda shape: pl.BlockSpec(shape, lambda i: (0, 0), pipeline_mode=pl.Buffered(1))
    slabs = [w.reshape(-1, w.shape[-1]) for w in (w_gate, w_up, w_down)]
    slab_rows = [s.shape[0] // n_tiles for s in slabs]
    assert all(s.shape[0] == r * n_tiles and r % 16 == 0 for s, r in zip(slabs, slab_rows))
    slab_specs = [pl.BlockSpec((r, s.shape[1]), lambda i: (i, 0)) for s, r in zip(slabs, slab_rows)]
    outs = pl.pallas_call(
        functools.partial(_inproj_kernel, tiles_per_seq=seq // TM_PROJ),
        name="inproj_conv",
        grid=(n_tiles,),
        in_specs=[
            pl.BlockSpec((TM_PROJ, D_MODEL), lambda i: (i, 0)),
            const((1, D_MODEL)),
            const((D_MODEL, IN_COLS)),
            const((3, D_MODEL)), const((1, D_MODEL)), const((D_MODEL, D_MODEL)),
        ] + slab_specs,
        out_specs=[pl.BlockSpec((TM_PROJ, D_MODEL), lambda i: (i, 0)),
                   pl.BlockSpec((TM_PROJ, REST_COLS), lambda i: (i, 0))] + slab_specs,
        out_shape=[jax.ShapeDtypeStruct((t, D_MODEL), BF16),
                   jax.ShapeDtypeStruct((t, REST_COLS), BF16)]
                  + [jax.ShapeDtypeStruct(s.shape, BF16) for s in slabs],
        scratch_shapes=[pltpu.VMEM((HALO_ROWS, D_MODEL), F32)],
        compiler_params=pltpu.CompilerParams(
            dimension_semantics=("arbitrary",), vmem_limit_bytes=VMEM_LIMIT),
    )(xf, g, w_in, conv_w, conv_b, wa, *slabs)
    za, proj, eg, eu, ed = outs
    return za, proj, eg.reshape(w_gate.shape), eu.reshape(w_up.shape), ed.reshape(w_down.shape)


def _attn_kernel(sink_ref, q_ref, k_ref, v_ref, kp_ref, vp_ref, gb_ref, za_ref, wb_ref, o_ref,
                 *, tiles_per_seq):
    first_tile = (pl.program_id(0) % tiles_per_seq) == 0
    ks = lax.broadcasted_iota(I32, (WINDOW, WINDOW), 0)
    qq = lax.broadcasted_iota(I32, (WINDOW, WINDOW), 1)
    own = ks <= qq
    dist = jnp.where(own, qq - ks, qq - ks + WINDOW).astype(F32)
    visible0 = jnp.logical_or(own, jnp.logical_not(first_tile))
    log2e = math.log2(math.e)
    c_scale = log2e / math.sqrt(HEAD_DIM)
    nt = (((1,), (1,)), ((), ()))
    zk = jnp.zeros((2 * WINDOW, HEAD_DIM), BF16)

    def transposed(v_blk):
        return jnp.transpose(v_blk.astype(F32)).astype(BF16)

    def project(rows_p, attn_blk):
        yb = jnp.dot(attn_blk, wb_ref[...], preferred_element_type=F32)
        zb = jax.nn.sigmoid(gb_ref[rows_p, :].astype(F32)) * yb
        o_ref[rows_p, :] = (za_ref[rows_p, :].astype(F32) + zb).astype(BF16)

    pending = None
    prev_k = kp_ref[...]
    prev_vt = transposed(vp_ref[...])
    for sb in range(TQ_ATTN // WINDOW):
        rows = slice(sb * WINDOW, (sb + 1) * WINDOW)
        cur_k = k_ref[rows, :]
        cur_vt = transposed(v_ref[rows, :])
        scores, vcats = [], []
        for kh in range(N_KV_HEADS):
            cols = slice(kh * HEAD_DIM, (kh + 1) * HEAD_DIM)
            kcat = jnp.concatenate([prev_k[:, cols], cur_k[:, cols]], axis=0)
            vcats.append(jnp.concatenate([prev_vt[cols, :], cur_vt[cols, :]], axis=1))
            qg = jnp.concatenate([q_ref[rows, (2 * kh) * LANES:(2 * kh + 1) * LANES],
                                  q_ref[rows, (2 * kh + 1) * LANES:(2 * kh + 2) * LANES]], axis=0)
            k_pad = jnp.concatenate([jnp.concatenate([kcat, zk], axis=1),
                                     jnp.concatenate([zk, kcat], axis=1)], axis=0)
            scores.append(lax.dot_general(k_pad, qg, nt, preferred_element_type=F32))
        if pending is not None:
            project(*pending)
        probs, rdens = [], []
        for kh in range(N_KV_HEADS):
            for pos in range(2):
                pr, rd = [], []
                for half in range(2):
                    h = kh * GROUP + 2 * half + pos
                    slope = 2.0 ** (-8.0 * (h + 1) / N_HEADS)
                    qcols = slice(half * WINDOW, (half + 1) * WINDOW)
                    krow = pos * 2 * WINDOW
                    st = scores[kh]
                    s = (jnp.where(own, st[krow + WINDOW:krow + 2 * WINDOW, qcols],
                                   st[krow:krow + WINDOW, qcols]) * c_scale
                         - (slope * log2e) * dist)
                    if sb == 0:
                        s = jnp.where(visible0, s, -jnp.inf)
                    m = jnp.max(s, axis=0, keepdims=True)
                    p = jnp.exp2(s - m)
                    den = jnp.sum(p, axis=0, keepdims=True) + jnp.exp2(sink_ref[h] * log2e - m)
                    rd.append(1.0 / den)
                    pr.append(jnp.concatenate(
                        [jnp.where(own, 0.0, p).astype(BF16), jnp.where(own, p, 0.0).astype(BF16)],
                        axis=0))
                probs.append(jnp.concatenate(pr, axis=1))
                rdens.append(jnp.concatenate(rd, axis=1))
        out_t = [None] * N_HEADS
        for kh in range(N_KV_HEADS):
            for pos in range(2):
                o2 = jnp.dot(vcats[kh], probs[2 * kh + pos], preferred_element_type=F32)
                o2 = o2 * rdens[2 * kh + pos]
                out_t[kh * GROUP + pos] = o2[:, :WINDOW]
                out_t[kh * GROUP + 2 + pos] = o2[:, WINDOW:]
        pending = (rows, jnp.transpose(jnp.concatenate(out_t, axis=0)).astype(BF16))
        prev_k, prev_vt = cur_k, cur_vt
    project(*pending)


def _attention(proj, za, sinks, wb, seq):
    t = proj.shape[0]
    sub = TQ_ATTN // WINDOW
    return pl.pallas_call(
        functools.partial(_attn_kernel, tiles_per_seq=seq // TQ_ATTN),
        name="swattn",
        grid=(t // TQ_ATTN,),
        in_specs=[
            pl.BlockSpec(memory_space=pltpu.SMEM),
            pl.BlockSpec((TQ_ATTN, D_MODEL), lambda i: (i, COL_Q)),
            pl.BlockSpec((TQ_ATTN, KV_WIDTH), lambda i: (i, COL_K)),
            pl.BlockSpec((TQ_ATTN, KV_WIDTH), lambda i: (i, COL_V)),
            pl.BlockSpec((WINDOW, KV_WIDTH), lambda i: (jnp.maximum(i * sub - 1, 0), COL_K)),
            pl.BlockSpec((WINDOW, KV_WIDTH), lambda i: (jnp.maximum(i * sub - 1, 0), COL_V)),
            pl.BlockSpec((TQ_ATTN, D_MODEL), lambda i: (i, COL_GB)),
            pl.BlockSpec((TQ_ATTN, D_MODEL), lambda i: (i, 0)),
            pl.BlockSpec((D_MODEL, D_MODEL), lambda i: (0, 0), pipeline_mode=pl.Buffered(1)),
        ],
        out_specs=pl.BlockSpec((TQ_ATTN, D_MODEL), lambda i: (i, 0)),
        out_shape=jax.ShapeDtypeStruct((t, D_MODEL), BF16),
        compiler_params=pltpu.CompilerParams(
            dimension_semantics=("arbitrary",), vmem_limit_bytes=VMEM_LIMIT),
    )(sinks, proj, proj, proj, proj, proj, proj, za, wb)


def _mix_kernel(merged_ref, x_ref, wo_ref, g_ref, wr_ref, br_ref,
                xmid_ref, h_ref, route_ref, cnt_ref):
    subs = [slice(s * SUB_MIX, (s + 1) * SUB_MIX) for s in range(TM_MIX // SUB_MIX)]
    xm = [x_ref[r, :] + jnp.dot(merged_ref[r, :], wo_ref[...], preferred_element_type=F32)
          for r in subs]
    hs = []
    for r, v in zip(subs, xm):
        xmid_ref[r, :] = _pack_bf16_pairs(v)
        h = _rms(v, g_ref[...])
        h_ref[r, :] = _pack_bf16_pairs(h)
        hs.append(h)
    wr = wr_ref[...]
    logits = []
    for h in hs:
        h_hi = h.astype(BF16)
        h_lo = (h - h_hi.astype(F32)).astype(BF16)
        both = jnp.dot(h_hi, wr, preferred_element_type=F32)
        logits.append(both[:, :LANES] + both[:, LANES:]
                      + jnp.dot(h_lo, wr[:, :LANES], preferred_element_type=F32) + br_ref[...])
    subs_per_cnt = TM_CNT // SUB_MIX
    cnts = [jnp.zeros((8, LANES), F32) for _ in range(TM_MIX // TM_CNT)]
    for s, (r, lg) in enumerate(zip(subs, logits)):
        route, cnt = _route(lg)
        route_ref[r, :] = route
        cnts[s // subs_per_cnt] = cnts[s // subs_per_cnt] + cnt
    for c, cnt in enumerate(cnts):
        cnt_ref[c * 8:(c + 1) * 8, :] = cnt


def _route(logits):
    n = logits.shape[0]
    lt = jnp.transpose(logits)
    sub = lax.broadcasted_iota(I32, (EXPERTS_PER_GROUP, n), 0)
    neg = -jnp.inf
    gl = jnp.where(sub < N_GROUPS, lt[N_EXPERTS:N_EXPERTS + EXPERTS_PER_GROUP], neg)
    gmax = jnp.max(gl, axis=0, keepdims=True)
    g_idx = jnp.min(jnp.where(gl == gmax, sub, EXPERTS_PER_GROUP), axis=0, keepdims=True)
    p_g = 1.0 / jnp.sum(jnp.exp(gl - gmax), axis=0, keepdims=True)
    v1 = v2 = i1 = i2 = None
    for g in range(N_GROUPS):
        eg = lt[g * EXPERTS_PER_GROUP:(g + 1) * EXPERTS_PER_GROUP]
        a1 = jnp.max(eg, axis=0, keepdims=True)
        j1 = jnp.min(jnp.where(eg == a1, sub, EXPERTS_PER_GROUP), axis=0, keepdims=True)
        eg2 = jnp.where(sub == j1, neg, eg)
        a2 = jnp.max(eg2, axis=0, keepdims=True)
        j2 = jnp.min(jnp.where(eg2 == a2, sub, EXPERTS_PER_GROUP), axis=0, keepdims=True)
        if g == 0:
            v1, v2, i1, i2 = a1, a2, j1, j2
        else:
            chosen = g_idx == g
            v1, v2 = jnp.where(chosen, a1, v1), jnp.where(chosen, a2, v2)
            i1, i2 = jnp.where(chosen, j1, i1), jnp.where(chosen, j2, i2)
    e21 = jnp.exp(v2 - v1)
    w1 = p_g / (1.0 + e21)
    w2 = p_g * e21 / (1.0 + e21)
    e1 = g_idx * EXPERTS_PER_GROUP + i1
    e2 = g_idx * EXPERTS_PER_GROUP + i2
    rows8 = jnp.where(sub == 0, e1.astype(F32),
                      jnp.where(sub == 1, e2.astype(F32),
                                jnp.where(sub == 2, w1, jnp.where(sub == 3, w2, 0.0))))
    route_t = jnp.concatenate([rows8, jnp.zeros((LANES - EXPERTS_PER_GROUP, n), F32)], axis=0)
    expert_row = lax.broadcasted_iota(I32, (LANES, n), 0)
    onehot_t = ((expert_row == e1) | (expert_row == e2)).astype(BF16)
    cnt = lax.dot_general(jnp.ones((8, n), BF16), onehot_t, (((1,), (1,)), ((), ())),
                          preferred_element_type=F32)
    return jnp.transpose(route_t), cnt


def _mix(merged, xf, wo, g, wr, br, chunk, t):
    n_tiles = t // TM_MIX
    first = chunk * n_tiles
    cnt_rows = TM_MIX // TM_CNT * 8
    full = lambda shape: pl.BlockSpec(shape, lambda i: (0, 0), pipeline_mode=pl.Buffered(1))
    tile = lambda w=D_MODEL: pl.BlockSpec((TM_MIX, w), lambda i: (i, 0))
    src = lambda c=0: pl.BlockSpec((TM_MIX, D_MODEL), lambda i: (first + i, c))
    return pl.pallas_call(
        _mix_kernel,
        name="merge_router",
        grid=(n_tiles,),
        in_specs=[
            src(), src(),
            full((D_MODEL, D_MODEL)), full((1, D_MODEL)),
            full((D_MODEL, 2 * LANES)), full((1, LANES)),
        ],
        out_specs=[tile(PACKED), tile(PACKED), tile(LANES),
                   pl.BlockSpec((cnt_rows, LANES), lambda i: (i, 0))],
        out_shape=[
            jax.ShapeDtypeStruct((t, PACKED), I32),
            jax.ShapeDtypeStruct((t, PACKED), I32),
            jax.ShapeDtypeStruct((t, LANES), F32),
            jax.ShapeDtypeStruct((n_tiles * cnt_rows, LANES), F32),
        ],
        compiler_params=pltpu.CompilerParams(
            dimension_semantics=("arbitrary",), vmem_limit_bytes=VMEM_LIMIT),
    )(merged, xf, wo, g, wr, br)


def _pos_kernel(route_ref, base_ref, pos_ref):
    lane = lax.broadcasted_iota(I32, (TM_CNT, LANES), 1)
    r = lax.broadcasted_iota(I32, (TM_CNT, TM_CNT), 0)
    c = lax.broadcasted_iota(I32, (TM_CNT, TM_CNT), 1)
    lower = (c < r).astype(BF16)
    subs = [slice(s * TM_CNT, (s + 1) * TM_CNT) for s in range(POS_TILES)]
    routes = [route_ref[rs, :] for rs in subs]
    e1 = [jnp.sum(jnp.where(lane == 0, rt, 0.0), axis=-1, keepdims=True).astype(I32) for rt in routes]
    e2 = [jnp.sum(jnp.where(lane == 1, rt, 0.0), axis=-1, keepdims=True).astype(I32) for rt in routes]
    onehot = [((lane == a) | (lane == b)).astype(BF16) for a, b in zip(e1, e2)]
    before = [jnp.dot(lower, oh, preferred_element_type=F32) + base_ref[s]
              for s, oh in enumerate(onehot)]
    for s, rs in enumerate(subs):
        p1 = jnp.sum(jnp.where(lane == e1[s], before[s], 0.0), axis=-1, keepdims=True)
        p2 = jnp.sum(jnp.where(lane == e2[s], before[s], 0.0), axis=-1, keepdims=True)
        packed = jnp.where(lane == 0, p1, jnp.where(lane == 1, p2, 0.0))
        pos_ref[:, rs] = jnp.transpose(packed)[0:TOP_K, :].astype(I32)


def _positions(route, base):
    t = route.shape[0]
    n_steps = t // (TM_CNT * POS_TILES)
    return pl.pallas_call(
        _pos_kernel,
        name="positions",
        grid=(n_steps,),
        in_specs=[
            pl.BlockSpec((TM_CNT * POS_TILES, LANES), lambda i: (i, 0)),
            pl.BlockSpec((POS_TILES, 1, LANES), lambda i: (i, 0, 0)),
        ],
        out_specs=pl.BlockSpec((TOP_K, TM_CNT * POS_TILES), lambda i: (0, i)),
        out_shape=jax.ShapeDtypeStruct((TOP_K, t), I32),
        compiler_params=pltpu.CompilerParams(dimension_semantics=("arbitrary",)),
    )(route, base)


def _sc_mesh():
    return plsc.VectorSubcoreMesh(core_axis_name="c", subcore_axis_name="s",
                                  num_cores=SC_CORES, num_subcores=SC_SUBCORES)


def _sc_worker():
    return lax.axis_index("s") * SC_CORES + lax.axis_index("c")


def _dispatch(pos, hp, rows):
    t = hp.shape[0]
    per_w = t // SC_WORKERS
    n_ch = per_w // SC_CHUNK
    pos4 = pos.reshape(TOP_K, SC_WORKERS, n_ch, SC_CHUNK)

    @functools.partial(
        pl.kernel, mesh=_sc_mesh(),
        out_type=jax.ShapeDtypeStruct((rows, PACKED), I32),
        scratch_types=[pltpu.VMEM((TOP_K, n_ch, SC_CHUNK), I32),
                       pltpu.VMEM((SC_CHUNK, PACKED), I32)])
    def scatter(hp_hbm, pos_hbm, xs_hbm, idx_v, rows_v):
        wid = _sc_worker()
        for k in range(TOP_K):
            pltpu.sync_copy(pos_hbm.at[k, wid], idx_v.at[k])

        def body(c, carry):
            start = pl.multiple_of(wid * per_w + c * SC_CHUNK, SC_CHUNK)
            pltpu.sync_copy(hp_hbm.at[pl.ds(start, SC_CHUNK)], rows_v)
            for k in range(TOP_K):
                pltpu.sync_copy(rows_v, xs_hbm.at[idx_v.at[k, c]])
            return carry

        lax.fori_loop(0, n_ch, body, 0)

    return scatter(hp, pos4)


def _gather_rows(table, idx):
    n = idx.shape[0]
    per_w = n // SC_WORKERS
    n_ch = per_w // SC_CHUNK
    idx3 = idx.reshape(SC_WORKERS, n_ch, SC_CHUNK)

    @functools.partial(
        pl.kernel, mesh=_sc_mesh(),
        out_type=jax.ShapeDtypeStruct((n, PACKED), I32),
        scratch_types=[pltpu.VMEM((n_ch, SC_CHUNK), I32),
                       pltpu.VMEM((SC_CHUNK, PACKED), I32)])
    def gather(table_hbm, idx_hbm, out_hbm, idx_v, rows_v):
        wid = _sc_worker()
        pltpu.sync_copy(idx_hbm.at[wid], idx_v)

        def body(c, carry):
            start = pl.multiple_of(wid * per_w + c * SC_CHUNK, SC_CHUNK)
            pltpu.sync_copy(table_hbm.at[idx_v.at[c]], rows_v)
            pltpu.sync_copy(rows_v, out_hbm.at[pl.ds(start, SC_CHUNK)])
            return carry

        lax.fori_loop(0, n_ch, body, 0)

    return gather(table, idx3)


def _expert_kernel(te_ref, nx_ref, sl_ref, ts_ref, tv_ref, xs_ref, wg_hbm, wu_hbm, wd_hbm, o_ref,
                   wg_s, wu_s, wd_s, sem):
    del ts_ref
    i = pl.program_id(0)
    n_valid = tv_ref[i]
    expert = te_ref[i]
    slot = sl_ref[i]

    def weight_copies(e, s):
        return (pltpu.make_async_copy(wg_hbm.at[e], wg_s.at[s], sem.at[s, 0]),
                pltpu.make_async_copy(wu_hbm.at[e], wu_s.at[s], sem.at[s, 1]),
                pltpu.make_async_copy(wd_hbm.at[e], wd_s.at[s], sem.at[s, 2]))

    @pl.when(i == 0)
    def _():
        for copy in weight_copies(expert, slot):
            copy.start()

    @pl.when(jnp.logical_or(i == 0, expert != te_ref[jnp.maximum(i - 1, 0)]))
    def _():
        for copy in weight_copies(expert, slot):
            copy.wait()
        nxt = nx_ref[i]

        @pl.when(nxt >= 0)
        def _():
            for copy in weight_copies(nxt, 1 - slot):
                copy.start()

    def mlp(n_sub, static_slot):
        wg, wu, wd = wg_s.at[static_slot], wu_s.at[static_slot], wd_s.at[static_slot]
        subs = [slice(s * SUB_EXP, (s + 1) * SUB_EXP) for s in range(n_sub)]
        xin = []
        for r in subs:
            rid = r.start + lax.broadcasted_iota(I32, (SUB_EXP, PACKED), 0)
            lo, hi = _unpack_bf16_pairs(jnp.where(rid < n_valid, xs_ref[r, :], 0))
            xin.append((lo.astype(BF16), hi.astype(BF16)))
        ab = [(jnp.dot(lo, wg[:PACKED, :], preferred_element_type=F32)
               + jnp.dot(hi, wg[PACKED:, :], preferred_element_type=F32),
               jnp.dot(lo, wu[:PACKED, :], preferred_element_type=F32)
               + jnp.dot(hi, wu[PACKED:, :], preferred_element_type=F32)) for lo, hi in xin]
        for r, (ai, bi) in zip(subs, ab):
            hm = (ai * jax.nn.sigmoid(ai) * bi).astype(BF16)
            o_ref[r, :] = _pack_bf16_pairs(jnp.dot(hm, wd[...], preferred_element_type=F32))
        if n_sub * SUB_EXP < TM_EXP:
            o_ref[n_sub * SUB_EXP:, :] = jnp.zeros((TM_EXP - n_sub * SUB_EXP, PACKED), I32)

    n_subs = TM_EXP // SUB_EXP
    for n_sub in range(n_subs + 1):
        lo_rows = (n_sub - 1) * SUB_EXP if n_sub else -1
        in_range = jnp.logical_and(n_valid > lo_rows, n_valid <= n_sub * SUB_EXP)
        if n_sub == 0:
            pl.when(in_range)(functools.partial(mlp, 0, 0))
            continue
        for static_slot in range(2):
            pl.when(jnp.logical_and(in_range, slot == static_slot))(
                functools.partial(mlp, n_sub, static_slot))


def _experts(tile_expert, next_expert, tile_slot, tile_src, tile_valid, xs, wg, wu, wd):
    rows = xs.shape[0]
    hbm = pl.BlockSpec(memory_space=pl.ANY)
    grid_spec = pltpu.PrefetchScalarGridSpec(
        num_scalar_prefetch=5,
        grid=(rows // TM_EXP,),
        in_specs=[pl.BlockSpec((TM_EXP, PACKED), lambda i, te, nx, sl, ts, tv: (ts[i], 0)),
                  hbm, hbm, hbm],
        out_specs=pl.BlockSpec((TM_EXP, PACKED), lambda i, te, nx, sl, ts, tv: (i, 0)),
        scratch_shapes=[
            pltpu.VMEM((2, D_MODEL, D_FF), BF16), pltpu.VMEM((2, D_MODEL, D_FF), BF16),
            pltpu.VMEM((2, D_FF, D_MODEL), BF16),
            pltpu.SemaphoreType.DMA((2, 3)),
        ],
    )
    return pl.pallas_call(
        _expert_kernel,
        name="experts",
        grid_spec=grid_spec,
        out_shape=jax.ShapeDtypeStruct((rows, PACKED), I32),
        compiler_params=pltpu.CompilerParams(
            dimension_semantics=("arbitrary",), vmem_limit_bytes=VMEM_LIMIT),
    )(tile_expert, next_expert, tile_slot, tile_src, tile_valid, xs, wg, wu, wd)


def _combine_kernel(y1_ref, y2_ref, route_ref, xmid_ref, g_ref, *rest):
    o_ref = rest[-1]
    route = route_ref[...]
    lane = lax.broadcasted_iota(I32, route.shape, 1)
    w1 = jnp.sum(jnp.where(lane == 2, route, 0.0), axis=-1, keepdims=True)
    w2 = jnp.sum(jnp.where(lane == 3, route, 0.0), axis=-1, keepdims=True)
    lo1, hi1 = _unpack_bf16_pairs(y1_ref[...])
    lo2, hi2 = _unpack_bf16_pairs(y2_ref[...])
    lox, hix = _unpack_bf16_pairs(xmid_ref[...])
    x_out = jnp.concatenate([lox + (lo1 * w1 + lo2 * w2), hix + (hi1 * w1 + hi2 * w2)], axis=1)
    o_ref[...] = _rms(x_out, g_ref[...])


def _combine(yg, route, xmid, g, chunk, t_total, out_prev):
    t = xmid.shape[0]
    n_tiles = t // TM_CMB
    first = chunk * n_tiles
    in_specs = [
        pl.BlockSpec((TM_CMB, PACKED), lambda i: (i, 0)),
        pl.BlockSpec((TM_CMB, PACKED), lambda i: (n_tiles + i, 0)),
        pl.BlockSpec((TM_CMB, LANES), lambda i: (i, 0)),
        pl.BlockSpec((TM_CMB, PACKED), lambda i: (i, 0)),
        pl.BlockSpec((1, D_MODEL), lambda i: (0, 0)),
    ]
    args = [yg, yg, route, xmid, g]
    aliases = {}
    if out_prev is not None:
        in_specs.append(pl.BlockSpec(memory_space=pl.ANY))
        aliases = {len(args): 0}
        args.append(out_prev)
    return pl.pallas_call(
        _combine_kernel,
        name="combine",
        grid=(n_tiles,),
        in_specs=in_specs,
        out_specs=pl.BlockSpec((TM_CMB, D_MODEL), lambda i: (first + i, 0)),
        out_shape=jax.ShapeDtypeStruct((t_total, D_MODEL), F32),
        input_output_aliases=aliases,
        compiler_params=pltpu.CompilerParams(
            dimension_semantics=("arbitrary",), vmem_limit_bytes=VMEM_LIMIT),
    )(*args)


def _split_bf16(w):
    hi = w.astype(BF16)
    lo = (w - hi.astype(F32)).astype(BF16)
    return hi, lo


def kernel(x, norm_mix, w_in, conv_w, conv_b, w_a_out, sinks, w_b_out, w_o, norm_ffn, w_group,
           b_group, w_expert, b_expert, w_gate, w_up, w_down, norm_final):
    bsz, seq, d = x.shape
    t = bsz * seq
    assert d == D_MODEL and seq % TM_PROJ == 0 and seq % TQ_ATTN == 0
    xf = x.reshape(t, d)
    row = lambda v: v.reshape(1, -1)

    assert w_in.shape == (D_MODEL, IN_COLS)
    za, proj, w_gate, w_up, w_down = _inproj(xf, row(norm_mix), w_in.astype(BF16), conv_w,
                                             row(conv_b), w_a_out.astype(BF16), seq,
                                             w_gate, w_up, w_down)
    merged = _attention(proj, za, sinks, w_b_out.astype(BF16), seq)

    pad = LANES - N_GROUPS - N_EXPERTS
    w_r = jnp.concatenate([w_expert, w_group, jnp.zeros((d, pad), F32)], axis=1)
    b_r = jnp.concatenate([b_expert, b_group, jnp.zeros((pad,), F32)]).reshape(1, LANES)
    wr = jnp.concatenate(_split_bf16(w_r), axis=1)
    wo = w_o.astype(BF16)

    t_chunk = t // MOE_CHUNKS
    assert t == t_chunk * MOE_CHUNKS and all(
        t_chunk % step == 0 for step in (TM_MIX, TM_CNT * POS_TILES, TM_CMB, SC_WORKERS * SC_CHUNK))
    out = None
    for chunk in range(MOE_CHUNKS):
        xmid, h2, route, cnt = _mix(merged, xf, wo, row(norm_ffn), wr, b_r, chunk, t_chunk)
        out = _moe_chunk(xmid, h2, route, cnt, w_gate, w_up, w_down, row(norm_final), chunk, t, out)
    return out.reshape(bsz, seq, d)


def _moe_chunk(xmid, h2, route, cnt, w_gate, w_up, w_down, g_final, chunk, t_total, out_prev):
    t = xmid.shape[0]
    n_tiles = t // TM_CNT
    cnt = cnt.reshape(n_tiles, 8, LANES)[:, 0, :N_EXPERTS].astype(I32)
    totals = jnp.sum(cnt, axis=0)
    tiles_e = (totals + TM_EXP - 1) // TM_EXP
    tile_end = jnp.cumsum(tiles_e)
    offset = (tile_end - tiles_e) * TM_EXP
    base = offset[None, :] + jnp.cumsum(cnt, axis=0) - cnt
    base = jnp.pad(base, ((0, 0), (0, LANES - N_EXPERTS))).astype(F32).reshape(n_tiles, 1, LANES)
    rows = t * TOP_K + N_EXPERTS * TM_EXP
    n_active = tile_end[-1]
    tile_id = jnp.arange(rows // TM_EXP, dtype=I32)
    tile_src = jnp.minimum(tile_id, n_active - 1)
    tile_expert = jnp.sum((tile_src[:, None] >= tile_end[None, :]).astype(I32), axis=1)
    tile_expert = jnp.minimum(tile_expert, N_EXPERTS - 1)
    row_in_expert = (tile_id - (tile_end - tiles_e)[tile_expert]) * TM_EXP
    tile_valid = jnp.clip(totals[tile_expert] - row_in_expert, 0, TM_EXP)
    tile_valid = jnp.where(tile_id < n_active, tile_valid, 0).astype(I32)
    after = tile_end[tile_expert]
    next_expert = jnp.where(after < n_active, tile_expert[jnp.minimum(after, n_active - 1)], -1)
    first_of_expert = jnp.concatenate(
        [jnp.ones((1,), I32), (tile_expert[1:] != tile_expert[:-1]).astype(I32)])
    tile_slot = (jnp.cumsum(first_of_expert) - 1) % 2

    pos = _positions(route, base)
    xs = _dispatch(pos, h2, rows)
    ys = _experts(tile_expert.astype(I32), next_expert.astype(I32), tile_slot.astype(I32),
                  tile_src.astype(I32), tile_valid, xs, w_gate, w_up, w_down)
    yg = _gather_rows(ys, pos.reshape(TOP_K * t))
    return _combine(yg, route, xmid, g_final, chunk, t_total, out_prev)
```

```python
import functools
import math

import jax
import jax.numpy as jnp
from jax import lax
from jax.experimental import pallas as pl
from jax.experimental.pallas import tpu as pltpu
from jax.experimental.pallas import tpu_sc as plsc

F32 = jnp.float32
BF16 = jnp.bfloat16
I32 = jnp.int32

D_MODEL = 1024
HEAD_DIM = 64
N_HEADS = 16
N_KV_HEADS = 4
GROUP = N_HEADS // N_KV_HEADS
KV_WIDTH = N_KV_HEADS * HEAD_DIM
WINDOW = 128
N_GROUPS = 4
EXPERTS_PER_GROUP = 8
N_EXPERTS = N_GROUPS * EXPERTS_PER_GROUP
TOP_K = 2
D_FF = 512
EPS = 1e-6
LANES = 128

COL_WIDTH = {"b": D_MODEL, "c": D_MODEL, "u": D_MODEL, "q": D_MODEL, "k": KV_WIDTH, "v": KV_WIDTH,
             "ga": D_MODEL, "gb": D_MODEL}
COL_START = dict(zip(COL_WIDTH, (sum(list(COL_WIDTH.values())[:i]) for i in range(len(COL_WIDTH)))))
IN_COLS = sum(COL_WIDTH.values())
REST_COLS = 2 * D_MODEL + 2 * KV_WIDTH
COL_GB, COL_Q = 0, 1
COL_K, COL_V = 2 * D_MODEL // KV_WIDTH, 2 * D_MODEL // KV_WIDTH + 1

TM_PROJ = 512
TQ_ATTN = 1024
TM_MIX = 1024
SUB_MIX = 256
TM_CNT = 512
POS_TILES = 8
MOE_CHUNKS = 2
TM_EXP = 512
SUB_EXP = 256
TM_CMB = 1024
HALO_ROWS = 8
VMEM_LIMIT = 56 * 1024 * 1024
PACKED = D_MODEL // 2

SC_CORES = 2
SC_SUBCORES = 16
SC_WORKERS = SC_CORES * SC_SUBCORES
SC_CHUNK = 64


def _rms(x, g):
    r = lax.rsqrt(jnp.mean(x * x, axis=-1, keepdims=True) + EPS)
    return (x * r) * g


def _pack_bf16_pairs(x):
    n = x.shape[1] // 2
    lo = lax.bitcast_convert_type(x[:, :n].astype(BF16).astype(F32), I32)
    hi = lax.bitcast_convert_type(x[:, n:].astype(BF16).astype(F32), I32)
    return (hi & jnp.int32(-65536)) | lax.shift_right_logical(lo, 16)


def _unpack_bf16_pairs(p):
    lo = lax.bitcast_convert_type(lax.shift_left(p, 16), F32)
    hi = lax.bitcast_convert_type(p & jnp.int32(-65536), F32)
    return lo, hi


def _inproj_kernel(x_ref, g_ref, w_ref, cw_ref, cb_ref, wa_ref,
                   eg_ref, eu_ref, ed_ref,
                   za_ref, proj_ref, eg_out, eu_out, ed_out, halo_ref, *, tiles_per_seq):
    i = pl.program_id(0)
    eg_out[...] = eg_ref[...].astype(BF16)
    eu_out[...] = eu_ref[...].astype(BF16)
    ed_out[...] = ed_ref[...].astype(BF16)
    h = _rms(x_ref[...], g_ref[...]).astype(BF16)
    col = lambda a, b=None: slice(COL_START[a], COL_START[b or a] + COL_WIDTH[b or a])
    pcu = jnp.dot(h, w_ref[:, col("c", "u")], preferred_element_type=F32)
    pb = jnp.dot(h, w_ref[:, col("b")], preferred_element_type=F32)
    pga = jnp.dot(h, w_ref[:, col("ga")], preferred_element_type=F32)
    proj_ref[:, :D_MODEL] = jnp.dot(h, w_ref[:, col("gb")], preferred_element_type=F32).astype(BF16)
    proj_ref[:, D_MODEL:] = jnp.dot(h, w_ref[:, col("q", "v")],
                                    preferred_element_type=F32).astype(BF16)
    cu = pcu[:, :D_MODEL] * pcu[:, D_MODEL:]
    first = (i % tiles_per_seq) == 0
    hist = jnp.where(first, 0.0, halo_ref[...])
    prev1 = hist[HALO_ROWS - 1:HALO_ROWS]
    prev2 = hist[HALO_ROWS - 2:HALO_ROWS - 1]
    halo_ref[...] = cu[TM_PROJ - HALO_ROWS:, :]
    row = lax.broadcasted_iota(I32, cu.shape, 0)
    cu1 = jnp.where(row == 0, prev1, pltpu.roll(cu, 1, 0))
    cu2 = jnp.where(row == 0, prev2, jnp.where(row == 1, prev1, pltpu.roll(cu, 2, 0)))
    cw = cw_ref[...]
    y = cw[0:1] * cu2 + cw[1:2] * cu1 + cw[2:3] * cu + cb_ref[...]
    ya = (pb * y).astype(BF16)
    z = jnp.dot(ya, wa_ref[...], preferred_element_type=F32)
    za_ref[...] = (jax.nn.sigmoid(pga) * z).astype(BF16)


def _inproj(xf, g, w_in, conv_w, conv_b, wa, seq, w_gate, w_up, w_down):
    t = xf.shape[0]
    n_tiles = t // TM_PROJ
    const = lambda shape: pl.BlockSpec(shape, lambda i: (0, 0), pipeline_mode=pl.Buffered(1))
    slabs = [w.reshape(-1, w.shape[-1]) for w in (w_gate, w_up, w_down)]
    slab_rows = [s.shape[0] // n_tiles for s in slabs]
    assert all(s.shape[0] == r * n_tiles and r % 16 == 0 for s, r in zip(slabs, slab_rows))
    slab_specs = [pl.BlockSpec((r, s.shape[1]), lambda i: (i, 0)) for s, r in zip(slabs, slab_rows)]
    outs = pl.pallas_call(
        functools.partial(_inproj_kernel, tiles_per_seq=seq // TM_PROJ),
        name="inproj_conv",
        grid=(n_tiles,),
        in_specs=[
            pl.BlockSpec((TM_PROJ, D_MODEL), lambda i: (i, 0)),
            const((1, D_MODEL)),
            const((D_MODEL, IN_COLS)),
            const((3, D_MODEL)), const((1, D_MODEL)), const((D_MODEL, D_MODEL)),
        ] + slab_specs,
        out_specs=[pl.BlockSpec((TM_PROJ, D_MODEL), lambda i: (i, 0)),
                   pl.BlockSpec((TM_PROJ, REST_COLS), lambda i: (i, 0))] + slab_specs,
        out_shape=[jax.ShapeDtypeStruct((t, D_MODEL), BF16),
                   jax.ShapeDtypeStruct((t, REST_COLS), BF16)]
                  + [jax.ShapeDtypeStruct(s.shape, BF16) for s in slabs],
        scratch_shapes=[pltpu.VMEM((HALO_ROWS, D_MODEL), F32)],
        compiler_params=pltpu.CompilerParams(
            dimension_semantics=("arbitrary",), vmem_limit_bytes=VMEM_LIMIT),
    )(xf, g, w_in, conv_w, conv_b, wa, *slabs)
    za, proj, eg, eu, ed = outs
    return za, proj, eg.reshape(w_gate.shape), eu.reshape(w_up.shape), ed.reshape(w_down.shape)


def _attn_kernel(sink_ref, q_ref, k_ref, v_ref, kp_ref, vp_ref, gb_ref, za_ref, wb_ref, o_ref,
                 *, tiles_per_seq):
    first_tile = (pl.program_id(0) % tiles_per_seq) == 0
    ks = lax.broadcasted_iota(I32, (WINDOW, WINDOW), 0)
    qq = lax.broadcasted_iota(I32, (WINDOW, WINDOW), 1)
    own = ks <= qq
    dist = jnp.where(own, qq - ks, qq - ks + WINDOW).astype(F32)
    visible0 = jnp.logical_or(own, jnp.logical_not(first_tile))
    log2e = math.log2(math.e)
    c_scale = log2e / math.sqrt(HEAD_DIM)
    nt = (((1,), (1,)), ((), ()))
    zk = jnp.zeros((2 * WINDOW, HEAD_DIM), BF16)

    def transposed(v_blk):
        return jnp.transpose(v_blk.astype(F32)).astype(BF16)

    def project(rows_p, attn_blk):
        yb = jnp.dot(attn_blk, wb_ref[...], preferred_element_type=F32)
        zb = jax.nn.sigmoid(gb_ref[rows_p, :].astype(F32)) * yb
        o_ref[rows_p, :] = (za_ref[rows_p, :].astype(F32) + zb).astype(BF16)

    pending = None
    prev_k = kp_ref[...]
    prev_vt = transposed(vp_ref[...])
    for sb in range(TQ_ATTN // WINDOW):
        rows = slice(sb * WINDOW, (sb + 1) * WINDOW)
        cur_k = k_ref[rows, :]
        cur_vt = transposed(v_ref[rows, :])
        scores, vcats = [], []
        for kh in range(N_KV_HEADS):
            cols = slice(kh * HEAD_DIM, (kh + 1) * HEAD_DIM)
            kcat = jnp.concatenate([prev_k[:, cols], cur_k[:, cols]], axis=0)
            vcats.append(jnp.concatenate([prev_vt[cols, :], cur_vt[cols, :]], axis=1))
            qg = jnp.concatenate([q_ref[rows, (2 * kh) * LANES:(2 * kh + 1) * LANES],
                                  q_ref[rows, (2 * kh + 1) * LANES:(2 * kh + 2) * LANES]], axis=0)
            k_pad = jnp.concatenate([jnp.concatenate([kcat, zk], axis=1),
                                     jnp.concatenate([zk, kcat], axis=1)], axis=0)
            scores.append(lax.dot_general(k_pad, qg, nt, preferred_element_type=F32))
        if pending is not None:
            project(*pending)
        probs, rdens = [], []
        for kh in range(N_KV_HEADS):
            for pos in range(2):
                pr, rd = [], []
                for half in range(2):
                    h = kh * GROUP + 2 * half + pos
                    slope = 2.0 ** (-8.0 * (h + 1) / N_HEADS)
                    qcols = slice(half * WINDOW, (half + 1) * WINDOW)
                    krow = pos * 2 * WINDOW
                    st = scores[kh]
                    s = (jnp.where(own, st[krow + WINDOW:krow + 2 * WINDOW, qcols],
                                   st[krow:krow + WINDOW, qcols]) * c_scale
                         - (slope * log2e) * dist)
                    if sb == 0:
                        s = jnp.where(visible0, s, -jnp.inf)
                    m = jnp.max(s, axis=0, keepdims=True)
                    p = jnp.exp2(s - m)
                    den = jnp.sum(p, axis=0, keepdims=True) + jnp.exp2(sink_ref[h] * log2e - m)
                    rd.append(1.0 / den)
                    pr.append(jnp.concatenate(
                        [jnp.where(own, 0.0, p).astype(BF16), jnp.where(own, p, 0.0).astype(BF16)],
                        axis=0))
                probs.append(jnp.concatenate(pr, axis=1))
                rdens.append(jnp.concatenate(rd, axis=1))
        out_t = [None] * N_HEADS
        for kh in range(N_KV_HEADS):
            for pos in range(2):
                o2 = jnp.dot(vcats[kh], probs[2 * kh + pos], preferred_element_type=F32)
                o2 = o2 * rdens[2 * kh + pos]
                out_t[kh * GROUP + pos] = o2[:, :WINDOW]
                out_t[kh * GROUP + 2 + pos] = o2[:, WINDOW:]
        pending = (rows, jnp.transpose(jnp.concatenate(out_t, axis=0)).astype(BF16))
        prev_k, prev_vt = cur_k, cur_vt
    project(*pending)


def _attention(proj, za, sinks, wb, seq):
    t = proj.shape[0]
    sub = TQ_ATTN // WINDOW
    return pl.pallas_call(
        functools.partial(_attn_kernel, tiles_per_seq=seq // TQ_ATTN),
        name="swattn",
        grid=(t // TQ_ATTN,),
        in_specs=[
            pl.BlockSpec(memory_space=pltpu.SMEM),
            pl.BlockSpec((TQ_ATTN, D_MODEL), lambda i: (i, COL_Q)),
            pl.BlockSpec((TQ_ATTN, KV_WIDTH), lambda i: (i, COL_K)),
            pl.BlockSpec((TQ_ATTN, KV_WIDTH), lambda i: (i, COL_V)),
            pl.BlockSpec((WINDOW, KV_WIDTH), lambda i: (jnp.maximum(i * sub - 1, 0), COL_K)),
            pl.BlockSpec((WINDOW, KV_WIDTH), lambda i: (jnp.maximum(i * sub - 1, 0), COL_V)),
            pl.BlockSpec((TQ_ATTN, D_MODEL), lambda i: (i, COL_GB)),
            pl.BlockSpec((TQ_ATTN, D_MODEL), lambda i: (i, 0)),
            pl.BlockSpec((D_MODEL, D_MODEL), lambda i: (0, 0), pipeline_mode=pl.Buffered(1)),
        ],
        out_specs=pl.BlockSpec((TQ_ATTN, D_MODEL), lambda i: (i, 0)),
        out_shape=jax.ShapeDtypeStruct((t, D_MODEL), BF16),
        compiler_params=pltpu.CompilerParams(
            dimension_semantics=("arbitrary",), vmem_limit_bytes=VMEM_LIMIT),
    )(sinks, proj, proj, proj, proj, proj, proj, za, wb)


def _mix_kernel(merged_ref, x_ref, wo_ref, g_ref, wr_ref, br_ref,
                xmid_ref, h_ref, route_ref, cnt_ref):
    subs = [slice(s * SUB_MIX, (s + 1) * SUB_MIX) for s in range(TM_MIX // SUB_MIX)]
    xm = [x_ref[r, :] + jnp.dot(merged_ref[r, :], wo_ref[...], preferred_element_type=F32)
          for r in subs]
    hs = []
    for r, v in zip(subs, xm):
        xmid_ref[r, :] = _pack_bf16_pairs(v)
        h = _rms(v, g_ref[...])
        h_ref[r, :] = _pack_bf16_pairs(h)
        hs.append(h)
    wr = wr_ref[...]
    logits = []
    for h in hs:
        h_hi = h.astype(BF16)
        h_lo = (h - h_hi.astype(F32)).astype(BF16)
        both = jnp.dot(h_hi, wr, preferred_element_type=F32)
        logits.append(both[:, :LANES] + both[:, LANES:]
                      + jnp.dot(h_lo, wr[:, :LANES], preferred_element_type=F32) + br_ref[...])
    subs_per_cnt = TM_CNT // SUB_MIX
    cnts = [jnp.zeros((8, LANES), F32) for _ in range(TM_MIX // TM_CNT)]
    for s, (r, lg) in enumerate(zip(subs, logits)):
        route, cnt = _route(lg)
        route_ref[r, :] = route
        cnts[s // subs_per_cnt] = cnts[s // subs_per_cnt] + cnt
    for c, cnt in enumerate(cnts):
        cnt_ref[c * 8:(c + 1) * 8, :] = cnt


def _route(logits):
    n = logits.shape[0]
    lt = jnp.transpose(logits)
    sub = lax.broadcasted_iota(I32, (EXPERTS_PER_GROUP, n), 0)
    neg = -jnp.inf
    gl = jnp.where(sub < N_GROUPS, lt[N_EXPERTS:N_EXPERTS + EXPERTS_PER_GROUP], neg)
    gmax = jnp.max(gl, axis=0, keepdims=True)
    g_idx = jnp.min(jnp.where(gl == gmax, sub, EXPERTS_PER_GROUP), axis=0, keepdims=True)
    p_g = 1.0 / jnp.sum(jnp.exp(gl - gmax), axis=0, keepdims=True)
    v1 = v2 = i1 = i2 = None
    for g in range(N_GROUPS):
        eg = lt[g * EXPERTS_PER_GROUP:(g + 1) * EXPERTS_PER_GROUP]
        a1 = jnp.max(eg, axis=0, keepdims=True)
        j1 = jnp.min(jnp.where(eg == a1, sub, EXPERTS_PER_GROUP), axis=0, keepdims=True)
        eg2 = jnp.where(sub == j1, neg, eg)
        a2 = jnp.max(eg2, axis=0, keepdims=True)
        j2 = jnp.min(jnp.where(eg2 == a2, sub, EXPERTS_PER_GROUP), axis=0, keepdims=True)
        if g == 0:
            v1, v2, i1, i2 = a1, a2, j1, j2
        else:
            chosen = g_idx == g
            v1, v2 = jnp.where(chosen, a1, v1), jnp.where(chosen, a2, v2)
            i1, i2 = jnp.where(chosen, j1, i1), jnp.where(chosen, j2, i2)
    e21 = jnp.exp(v2 - v1)
    w1 = p_g / (1.0 + e21)
    w2 = p_g * e21 / (1.0 + e21)
    e1 = g_idx * EXPERTS_PER_GROUP + i1
    e2 = g_idx * EXPERTS_PER_GROUP + i2
    rows8 = jnp.where(sub == 0, e1.astype(F32),
                      jnp.where(sub == 1, e2.astype(F32),
                                jnp.where(sub == 2, w1, jnp.where(sub == 3, w2, 0.0))))
    route_t = jnp.concatenate([rows8, jnp.zeros((LANES - EXPERTS_PER_GROUP, n), F32)], axis=0)
    expert_row = lax.broadcasted_iota(I32, (LANES, n), 0)
    onehot_t = ((expert_row == e1) | (expert_row == e2)).astype(BF16)
    cnt = lax.dot_general(jnp.ones((8, n), BF16), onehot_t, (((1,), (1,)), ((), ())),
                          preferred_element_type=F32)
    return jnp.transpose(route_t), cnt


def _mix(merged, xf, wo, g, wr, br, chunk, t):
    n_tiles = t // TM_MIX
    first = chunk * n_tiles
    cnt_rows = TM_MIX // TM_CNT * 8
    full = lambda shape: pl.BlockSpec(shape, lambda i: (0, 0), pipeline_mode=pl.Buffered(1))
    tile = lambda w=D_MODEL: pl.BlockSpec((TM_MIX, w), lambda i: (i, 0))
    src = lambda c=0: pl.BlockSpec((TM_MIX, D_MODEL), lambda i: (first + i, c))
    return pl.pallas_call(
        _mix_kernel,
        name="merge_router",
        grid=(n_tiles,),
        in_specs=[
            src(), src(),
            full((D_MODEL, D_MODEL)), full((1, D_MODEL)),
            full((D_MODEL, 2 * LANES)), full((1, LANES)),
        ],
        out_specs=[tile(PACKED), tile(PACKED), tile(LANES),
                   pl.BlockSpec((cnt_rows, LANES), lambda i: (i, 0))],
        out_shape=[
            jax.ShapeDtypeStruct((t, PACKED), I32),
            jax.ShapeDtypeStruct((t, PACKED), I32),
            jax.ShapeDtypeStruct((t, LANES), F32),
            jax.ShapeDtypeStruct((n_tiles * cnt_rows, LANES), F32),
        ],
        compiler_params=pltpu.CompilerParams(
            dimension_semantics=("arbitrary",), vmem_limit_bytes=VMEM_LIMIT),
    )(merged, xf, wo, g, wr, br)


def _pos_kernel(route_ref, base_ref, pos_ref):
    lane = lax.broadcasted_iota(I32, (TM_CNT, LANES), 1)
    r = lax.broadcasted_iota(I32, (TM_CNT, TM_CNT), 0)
    c = lax.broadcasted_iota(I32, (TM_CNT, TM_CNT), 1)
    lower = (c < r).astype(BF16)
    subs = [slice(s * TM_CNT, (s + 1) * TM_CNT) for s in range(POS_TILES)]
    routes = [route_ref[rs, :] for rs in subs]
    e1 = [jnp.sum(jnp.where(lane == 0, rt, 0.0), axis=-1, keepdims=True).astype(I32) for rt in routes]
    e2 = [jnp.sum(jnp.where(lane == 1, rt, 0.0), axis=-1, keepdims=True).astype(I32) for rt in routes]
    onehot = [((lane == a) | (lane == b)).astype(BF16) for a, b in zip(e1, e2)]
    before = [jnp.dot(lower, oh, preferred_element_type=F32) + base_ref[s]
              for s, oh in enumerate(onehot)]
    for s, rs in enumerate(subs):
        p1 = jnp.sum(jnp.where(lane == e1[s], before[s], 0.0), axis=-1, keepdims=True)
        p2 = jnp.sum(jnp.where(lane == e2[s], before[s], 0.0), axis=-1, keepdims=True)
        packed = jnp.where(lane == 0, p1, jnp.where(lane == 1, p2, 0.0))
        pos_ref[:, rs] = jnp.transpose(packed)[0:TOP_K, :].astype(I32)


def _positions(route, base):
    t = route.shape[0]
    n_steps = t // (TM_CNT * POS_TILES)
    return pl.pallas_call(
        _pos_kernel,
        name="positions",
        grid=(n_steps,),
        in_specs=[
            pl.BlockSpec((TM_CNT * POS_TILES, LANES), lambda i: (i, 0)),
            pl.BlockSpec((POS_TILES, 1, LANES), lambda i: (i, 0, 0)),
        ],
        out_specs=pl.BlockSpec((TOP_K, TM_CNT * POS_TILES), lambda i: (0, i)),
        out_shape=jax.ShapeDtypeStruct((TOP_K, t), I32),
        compiler_params=pltpu.CompilerParams(dimension_semantics=("arbitrary",)),
    )(route, base)


def _sc_mesh():
    return plsc.VectorSubcoreMesh(core_axis_name="c", subcore_axis_name="s",
                                  num_cores=SC_CORES, num_subcores=SC_SUBCORES)


def _sc_worker():
    return lax.axis_index("s") * SC_CORES + lax.axis_index("c")


def _dispatch(pos, hp, rows):
    t = hp.shape[0]
    per_w = t // SC_WORKERS
    n_ch = per_w // SC_CHUNK
    pos4 = pos.reshape(TOP_K, SC_WORKERS, n_ch, SC_CHUNK)

    @functools.partial(
        pl.kernel, mesh=_sc_mesh(),
        out_type=jax.ShapeDtypeStruct((rows, PACKED), I32),
        scratch_types=[pltpu.VMEM((TOP_K, n_ch, SC_CHUNK), I32),
                       pltpu.VMEM((SC_CHUNK, PACKED), I32)])
    def scatter(hp_hbm, pos_hbm, xs_hbm, idx_v, rows_v):
        wid = _sc_worker()
        for k in range(TOP_K):
            pltpu.sync_copy(pos_hbm.at[k, wid], idx_v.at[k])

        def body(c, carry):
            start = pl.multiple_of(wid * per_w + c * SC_CHUNK, SC_CHUNK)
            pltpu.sync_copy(hp_hbm.at[pl.ds(start, SC_CHUNK)], rows_v)
            for k in range(TOP_K):
                pltpu.sync_copy(rows_v, xs_hbm.at[idx_v.at[k, c]])
            return carry

        lax.fori_loop(0, n_ch, body, 0)

    return scatter(hp, pos4)


def _gather_rows(table, idx):
    n = idx.shape[0]
    per_w = n // SC_WORKERS
    n_ch = per_w // SC_CHUNK
    idx3 = idx.reshape(SC_WORKERS, n_ch, SC_CHUNK)

    @functools.partial(
        pl.kernel, mesh=_sc_mesh(),
        out_type=jax.ShapeDtypeStruct((n, PACKED), I32),
        scratch_types=[pltpu.VMEM((n_ch, SC_CHUNK), I32),
                       pltpu.VMEM((SC_CHUNK, PACKED), I32)])
    def gather(table_hbm, idx_hbm, out_hbm, idx_v, rows_v):
        wid = _sc_worker()
        pltpu.sync_copy(idx_hbm.at[wid], idx_v)

        def body(c, carry):
            start = pl.multiple_of(wid * per_w + c * SC_CHUNK, SC_CHUNK)
            pltpu.sync_copy(table_hbm.at[idx_v.at[c]], rows_v)
            pltpu.sync_copy(rows_v, out_hbm.at[pl.ds(start, SC_CHUNK)])
            return carry

        lax.fori_loop(0, n_ch, body, 0)

    return gather(table, idx3)


def _expert_kernel(te_ref, nx_ref, sl_ref, ts_ref, tv_ref, xs_ref, wg_hbm, wu_hbm, wd_hbm, o_ref,
                   wg_s, wu_s, wd_s, sem):
    del ts_ref
    i = pl.program_id(0)
    n_valid = tv_ref[i]
    expert = te_ref[i]
    slot = sl_ref[i]

    def weight_copies(e, s):
        return (pltpu.make_async_copy(wg_hbm.at[e], wg_s.at[s], sem.at[s, 0]),
                pltpu.make_async_copy(wu_hbm.at[e], wu_s.at[s], sem.at[s, 1]),
                pltpu.make_async_copy(wd_hbm.at[e], wd_s.at[s], sem.at[s, 2]))

    @pl.when(i == 0)
    def _():
        for copy in weight_copies(expert, slot):
            copy.start()

    @pl.when(jnp.logical_or(i == 0, expert != te_ref[jnp.maximum(i - 1, 0)]))
    def _():
        for copy in weight_copies(expert, slot):
            copy.wait()
        nxt = nx_ref[i]

        @pl.when(nxt >= 0)
        def _():
            for copy in weight_copies(nxt, 1 - slot):
                copy.start()

    def mlp(n_sub):
        wg, wu, wd = wg_s.at[slot], wu_s.at[slot], wd_s.at[slot]
        subs = [slice(s * SUB_EXP, (s + 1) * SUB_EXP) for s in range(n_sub)]
        xin = []
        for r in subs:
            rid = r.start + lax.broadcasted_iota(I32, (SUB_EXP, PACKED), 0)
            lo, hi = _unpack_bf16_pairs(jnp.where(rid < n_valid, xs_ref[r, :], 0))
            xin.append((lo.astype(BF16), hi.astype(BF16)))
        ab = [(jnp.dot(lo, wg[:PACKED, :], preferred_element_type=F32)
               + jnp.dot(hi, wg[PACKED:, :], preferred_element_type=F32),
               jnp.dot(lo, wu[:PACKED, :], preferred_element_type=F32)
               + jnp.dot(hi, wu[PACKED:, :], preferred_element_type=F32)) for lo, hi in xin]
        for r, (ai, bi) in zip(subs, ab):
            hm = (ai * jax.nn.sigmoid(ai) * bi).astype(BF16)
            o_ref[r, :] = _pack_bf16_pairs(jnp.dot(hm, wd[...], preferred_element_type=F32))
        if n_sub * SUB_EXP < TM_EXP:
            o_ref[n_sub * SUB_EXP:, :] = jnp.zeros((TM_EXP - n_sub * SUB_EXP, PACKED), I32)

    n_subs = TM_EXP // SUB_EXP
    for n_sub in range(n_subs + 1):
        lo_rows = (n_sub - 1) * SUB_EXP if n_sub else -1
        in_range = jnp.logical_and(n_valid > lo_rows, n_valid <= n_sub * SUB_EXP)
        pl.when(in_range)(functools.partial(mlp, n_sub))


def _experts(tile_expert, next_expert, tile_slot, tile_src, tile_valid, xs, wg, wu, wd):
    rows = xs.shape[0]
    hbm = pl.BlockSpec(memory_space=pl.ANY)
    grid_spec = pltpu.PrefetchScalarGridSpec(
        num_scalar_prefetch=5,
        grid=(rows // TM_EXP,),
        in_specs=[pl.BlockSpec((TM_EXP, PACKED), lambda i, te, nx, sl, ts, tv: (ts[i], 0)),
                  hbm, hbm, hbm],
        out_specs=pl.BlockSpec((TM_EXP, PACKED), lambda i, te, nx, sl, ts, tv: (i, 0)),
        scratch_shapes=[
            pltpu.VMEM((2, D_MODEL, D_FF), BF16), pltpu.VMEM((2, D_MODEL, D_FF), BF16),
            pltpu.VMEM((2, D_FF, D_MODEL), BF16),
            pltpu.SemaphoreType.DMA((2, 3)),
        ],
    )
    return pl.pallas_call(
        _expert_kernel,
        name="experts",
        grid_spec=grid_spec,
        out_shape=jax.ShapeDtypeStruct((rows, PACKED), I32),
        compiler_params=pltpu.CompilerParams(
            dimension_semantics=("arbitrary",), vmem_limit_bytes=VMEM_LIMIT),
    )(tile_expert, next_expert, tile_slot, tile_src, tile_valid, xs, wg, wu, wd)


def _combine_kernel(y1_ref, y2_ref, route_ref, xmid_ref, g_ref, *rest):
    o_ref = rest[-1]
    route = route_ref[...]
    lane = lax.broadcasted_iota(I32, route.shape, 1)
    w1 = jnp.sum(jnp.where(lane == 2, route, 0.0), axis=-1, keepdims=True)
    w2 = jnp.sum(jnp.where(lane == 3, route, 0.0), axis=-1, keepdims=True)
    lo1, hi1 = _unpack_bf16_pairs(y1_ref[...])
    lo2, hi2 = _unpack_bf16_pairs(y2_ref[...])
    lox, hix = _unpack_bf16_pairs(xmid_ref[...])
    x_out = jnp.concatenate([lox + (lo1 * w1 + lo2 * w2), hix + (hi1 * w1 + hi2 * w2)], axis=1)
    o_ref[...] = _rms(x_out, g_ref[...])


def _combine(yg, route, xmid, g, chunk, t_total, out_prev):
    t = xmid.shape[0]
    n_tiles = t // TM_CMB
    first = chunk * n_tiles
    in_specs = [
        pl.BlockSpec((TM_CMB, PACKED), lambda i: (i, 0)),
        pl.BlockSpec((TM_CMB, PACKED), lambda i: (n_tiles + i, 0)),
        pl.BlockSpec((TM_CMB, LANES), lambda i: (i, 0)),
        pl.BlockSpec((TM_CMB, PACKED), lambda i: (i, 0)),
        pl.BlockSpec((1, D_MODEL), lambda i: (0, 0)),
    ]
    args = [yg, yg, route, xmid, g]
    aliases = {}
    if out_prev is not None:
        in_specs.append(pl.BlockSpec(memory_space=pl.ANY))
        aliases = {len(args): 0}
        args.append(out_prev)
    return pl.pallas_call(
        _combine_kernel,
        name="combine",
        grid=(n_tiles,),
        in_specs=in_specs,
        out_specs=pl.BlockSpec((TM_CMB, D_MODEL), lambda i: (first + i, 0)),
        out_shape=jax.ShapeDtypeStruct((t_total, D_MODEL), F32),
        input_output_aliases=aliases,
        compiler_params=pltpu.CompilerParams(
            dimension_semantics=("arbitrary",), vmem_limit_bytes=VMEM_LIMIT),
    )(*args)


def _split_bf16(w):
    hi = w.astype(BF16)
    lo = (w - hi.astype(F32)).astype(BF16)
    return hi, lo


def kernel(x, norm_mix, w_in, conv_w, conv_b, w_a_out, sinks, w_b_out, w_o, norm_ffn, w_group,
           b_group, w_expert, b_expert, w_gate, w_up, w_down, norm_final):
    bsz, seq, d = x.shape
    t = bsz * seq
    assert d == D_MODEL and seq % TM_PROJ == 0 and seq % TQ_ATTN == 0
    xf = x.reshape(t, d)
    row = lambda v: v.reshape(1, -1)

    assert w_in.shape == (D_MODEL, IN_COLS)
    za, proj, w_gate, w_up, w_down = _inproj(xf, row(norm_mix), w_in.astype(BF16), conv_w,
                                             row(conv_b), w_a_out.astype(BF16), seq,
                                             w_gate, w_up, w_down)
    merged = _attention(proj, za, sinks, w_b_out.astype(BF16), seq)

    pad = LANES - N_GROUPS - N_EXPERTS
    w_r = jnp.concatenate([w_expert, w_group, jnp.zeros((d, pad), F32)], axis=1)
    b_r = jnp.concatenate([b_expert, b_group, jnp.zeros((pad,), F32)]).reshape(1, LANES)
    wr = jnp.concatenate(_split_bf16(w_r), axis=1)
    wo = w_o.astype(BF16)

    t_chunk = t // MOE_CHUNKS
    assert t == t_chunk * MOE_CHUNKS and all(
        t_chunk % step == 0 for step in (TM_MIX, TM_CNT * POS_TILES, TM_CMB, SC_WORKERS * SC_CHUNK))
    out = None
    for chunk in range(MOE_CHUNKS):
        xmid, h2, route, cnt = _mix(merged, xf, wo, row(norm_ffn), wr, b_r, chunk, t_chunk)
        out = _moe_chunk(xmid, h2, route, cnt, w_gate, w_up, w_down, row(norm_final), chunk, t, out)
    return out.reshape(bsz, seq, d)


def _moe_chunk(xmid, h2, route, cnt, w_gate, w_up, w_down, g_final, chunk, t_total, out_prev):
    t = xmid.shape[0]
    n_tiles = t // TM_CNT
    cnt = cnt.reshape(n_tiles, 8, LANES)[:, 0, :N_EXPERTS].astype(I32)
    totals = jnp.sum(cnt, axis=0)
    tiles_e = (totals + TM_EXP - 1) // TM_EXP
    tile_end = jnp.cumsum(tiles_e)
    offset = (tile_end - tiles_e) * TM_EXP
    base = offset[None, :] + jnp.cumsum(cnt, axis=0) - cnt
    base = jnp.pad(base, ((0, 0), (0, LANES - N_EXPERTS))).astype(F32).reshape(n_tiles, 1, LANES)
    rows = t * TOP_K + N_EXPERTS * TM_EXP
    n_active = tile_end[-1]
    tile_id = jnp.arange(rows // TM_EXP, dtype=I32)
    tile_src = jnp.minimum(tile_id, n_active - 1)
    tile_expert = jnp.sum((tile_src[:, None] >= tile_end[None, :]).astype(I32), axis=1)
    tile_expert = jnp.minimum(tile_expert, N_EXPERTS - 1)
    row_in_expert = (tile_id - (tile_end - tiles_e)[tile_expert]) * TM_EXP
    tile_valid = jnp.clip(totals[tile_expert] - row_in_expert, 0, TM_EXP)
    tile_valid = jnp.where(tile_id < n_active, tile_valid, 0).astype(I32)
    after = tile_end[tile_expert]
    next_expert = jnp.where(after < n_active, tile_expert[jnp.minimum(after, n_active - 1)], -1)
    first_of_expert = jnp.concatenate(
        [jnp.ones((1,), I32), (tile_expert[1:] != tile_expert[:-1]).astype(I32)])
    tile_slot = (jnp.cumsum(first_of_expert) - 1) % 2

    pos = _positions(route, base)
    xs = _dispatch(pos, h2, rows)
    ys = _experts(tile_expert.astype(I32), next_expert.astype(I32), tile_slot.astype(I32),
                  tile_src.astype(I32), tile_valid, xs, w_gate, w_up, w_down)
    yg = _gather_rows(ys, pos.reshape(TOP_K * t))
    return _combine(yg, route, xmid, g_final, chunk, t_total, out_prev)
```

```python
import functools
import math

import jax
import jax.numpy as jnp
from jax import lax
from jax.experimental import pallas as pl
from jax.experimental.pallas import tpu as pltpu
from jax.experimental.pallas import tpu_sc as plsc

F32 = jnp.float32
BF16 = jnp.bfloat16
I32 = jnp.int32

D_MODEL = 1024
HEAD_DIM = 64
N_HEADS = 16
N_KV_HEADS = 4
GROUP = N_HEADS // N_KV_HEADS
KV_WIDTH = N_KV_HEADS * HEAD_DIM
WINDOW = 128
N_GROUPS = 4
EXPERTS_PER_GROUP = 8
N_EXPERTS = N_GROUPS * EXPERTS_PER_GROUP
TOP_K = 2
D_FF = 512
EPS = 1e-6
LANES = 128

COL_WIDTH = {"b": D_MODEL, "c": D_MODEL, "u": D_MODEL, "q": D_MODEL, "k": KV_WIDTH, "v": KV_WIDTH,
             "ga": D_MODEL, "gb": D_MODEL}
COL_START = dict(zip(COL_WIDTH, (sum(list(COL_WIDTH.values())[:i]) for i in range(len(COL_WIDTH)))))
IN_COLS = sum(COL_WIDTH.values())
REST_COLS = 2 * D_MODEL + 2 * KV_WIDTH
COL_GB, COL_Q = 0, 1
COL_K, COL_V = 2 * D_MODEL // KV_WIDTH, 2 * D_MODEL // KV_WIDTH + 1

TM_PROJ = 512
TQ_ATTN = 1024
TM_MIX = 1024
SUB_MIX = 256
TM_CNT = 512
POS_TILES = 8
MOE_CHUNKS = 2
TM_EXP = 512
SUB_EXP = 256
TM_CMB = 1024
HALO_ROWS = 8
VMEM_LIMIT = 56 * 1024 * 1024
PACKED = D_MODEL // 2

SC_CORES = 2
SC_SUBCORES = 16
SC_WORKERS = SC_CORES * SC_SUBCORES
SC_CHUNK = 64


def _rms(x, g):
    r = lax.rsqrt(jnp.mean(x * x, axis=-1, keepdims=True) + EPS)
    return (x * r) * g


def _pack_bf16_pairs(x):
    n = x.shape[1] // 2
    lo = lax.bitcast_convert_type(x[:, :n].astype(BF16).astype(F32), I32)
    hi = lax.bitcast_convert_type(x[:, n:].astype(BF16).astype(F32), I32)
    return (hi & jnp.int32(-65536)) | lax.shift_right_logical(lo, 16)


def _unpack_bf16_pairs(p):
    lo = lax.bitcast_convert_type(lax.shift_left(p, 16), F32)
    hi = lax.bitcast_convert_type(p & jnp.int32(-65536), F32)
    return lo, hi


def _inproj_kernel(x_ref, g_ref, w_ref, cw_ref, cb_ref, wa_ref,
                   eg_ref, eu_ref, ed_ref,
                   za_ref, proj_ref, eg_out, eu_out, ed_out, halo_ref, *, tiles_per_seq):
    i = pl.program_id(0)
    eg_out[...] = eg_ref[...].astype(BF16)
    eu_out[...] = eu_ref[...].astype(BF16)
    ed_out[...] = ed_ref[...].astype(BF16)
    h = _rms(x_ref[...], g_ref[...]).astype(BF16)
    col = lambda a, b=None: slice(COL_START[a], COL_START[b or a] + COL_WIDTH[b or a])
    pcu = jnp.dot(h, w_ref[:, col("c", "u")], preferred_element_type=F32)
    pb = jnp.dot(h, w_ref[:, col("b")], preferred_element_type=F32)
    pga = jnp.dot(h, w_ref[:, col("ga")], preferred_element_type=F32)
    proj_ref[:, :D_MODEL] = jnp.dot(h, w_ref[:, col("gb")], preferred_element_type=F32).astype(BF16)
    proj_ref[:, D_MODEL:] = jnp.dot(h, w_ref[:, col("q", "v")],
                                    preferred_element_type=F32).astype(BF16)
    cu = pcu[:, :D_MODEL] * pcu[:, D_MODEL:]
    first = (i % tiles_per_seq) == 0
    hist = jnp.where(first, 0.0, halo_ref[...])
    prev1 = hist[HALO_ROWS - 1:HALO_ROWS]
    prev2 = hist[HALO_ROWS - 2:HALO_ROWS - 1]
    halo_ref[...] = cu[TM_PROJ - HALO_ROWS:, :]
    row = lax.broadcasted_iota(I32, cu.shape, 0)
    cu1 = jnp.where(row == 0, prev1, pltpu.roll(cu, 1, 0))
    cu2 = jnp.where(row == 0, prev2, jnp.where(row == 1, prev1, pltpu.roll(cu, 2, 0)))
    cw = cw_ref[...]
    y = cw[0:1] * cu2 + cw[1:2] * cu1 + cw[2:3] * cu + cb_ref[...]
    ya = (pb * y).astype(BF16)
    z = jnp.dot(ya, wa_ref[...], preferred_element_type=F32)
    za_ref[...] = (jax.nn.sigmoid(pga) * z).astype(BF16)


def _inproj(xf, g, w_in, conv_w, conv_b, wa, seq, w_gate, w_up, w_down):
    t = xf.shape[0]
    n_tiles = t // TM_PROJ
    const = lambda shape: pl.BlockSpec(shape, lambda i: (0, 0), pipeline_mode=pl.Buffered(1))
    slabs = [w.reshape(-1, w.shape[-1]) for w in (w_gate, w_up, w_down)]
    slab_rows = [s.shape[0] // n_tiles for s in slabs]
    assert all(s.shape[0] == r * n_tiles and r % 16 == 0 for s, r in zip(slabs, slab_rows))
    slab_specs = [pl.BlockSpec((r, s.shape[1]), lambda i: (i, 0)) for s, r in zip(slabs, slab_rows)]
    outs = pl.pallas_call(
        functools.partial(_inproj_kernel, tiles_per_seq=seq // TM_PROJ),
        name="inproj_conv",
        grid=(n_tiles,),
        in_specs=[
            pl.BlockSpec((TM_PROJ, D_MODEL), lambda i: (i, 0)),
            const((1, D_MODEL)),
            const((D_MODEL, IN_COLS)),
            const((3, D_MODEL)), const((1, D_MODEL)), const((D_MODEL, D_MODEL)),
        ] + slab_specs,
        out_specs=[pl.BlockSpec((TM_PROJ, D_MODEL), lambda i: (i, 0)),
                   pl.BlockSpec((TM_PROJ, REST_COLS), lambda i: (i, 0))] + slab_specs,
        out_shape=[jax.ShapeDtypeStruct((t, D_MODEL), BF16),
                   jax.ShapeDtypeStruct((t, REST_COLS), BF16)]
                  + [jax.ShapeDtypeStruct(s.shape, BF16) for s in slabs],
        scratch_shapes=[pltpu.VMEM((HALO_ROWS, D_MODEL), F32)],
        compiler_params=pltpu.CompilerParams(
            dimension_semantics=("arbitrary",), vmem_limit_bytes=VMEM_LIMIT),
    )(xf, g, w_in, conv_w, conv_b, wa, *slabs)
    za, proj, eg, eu, ed = outs
    return za, proj, eg.reshape(w_gate.shape), eu.reshape(w_up.shape), ed.reshape(w_down.shape)


def _attn_kernel(sink_ref, q_ref, k_ref, v_ref, kp_ref, vp_ref, gb_ref, za_ref, wb_ref, o_ref,
                 *, first_tile_index, tiles_per_seq):
    first_tile = ((first_tile_index + pl.program_id(0)) % tiles_per_seq) == 0
    ks = lax.broadcasted_iota(I32, (WINDOW, WINDOW), 0)
    qq = lax.broadcasted_iota(I32, (WINDOW, WINDOW), 1)
    own = ks <= qq
    dist = jnp.where(own, qq - ks, qq - ks + WINDOW).astype(F32)
    visible0 = jnp.logical_or(own, jnp.logical_not(first_tile))
    log2e = math.log2(math.e)
    c_scale = log2e / math.sqrt(HEAD_DIM)
    nt = (((1,), (1,)), ((), ()))
    zk = jnp.zeros((2 * WINDOW, HEAD_DIM), BF16)

    def transposed(v_blk):
        return jnp.transpose(v_blk.astype(F32)).astype(BF16)

    def project(rows_p, attn_blk):
        yb = jnp.dot(attn_blk, wb_ref[...], preferred_element_type=F32)
        zb = jax.nn.sigmoid(gb_ref[rows_p, :].astype(F32)) * yb
        o_ref[rows_p, :] = (za_ref[rows_p, :].astype(F32) + zb).astype(BF16)

    pending = None
    prev_k = kp_ref[...]
    prev_vt = transposed(vp_ref[...])
    for sb in range(TQ_ATTN // WINDOW):
        rows = slice(sb * WINDOW, (sb + 1) * WINDOW)
        cur_k = k_ref[rows, :]
        cur_vt = transposed(v_ref[rows, :])
        scores, vcats = [], []
        for kh in range(N_KV_HEADS):
            cols = slice(kh * HEAD_DIM, (kh + 1) * HEAD_DIM)
            kcat = jnp.concatenate([prev_k[:, cols], cur_k[:, cols]], axis=0)
            vcats.append(jnp.concatenate([prev_vt[cols, :], cur_vt[cols, :]], axis=1))
            qg = jnp.concatenate([q_ref[rows, (2 * kh) * LANES:(2 * kh + 1) * LANES],
                                  q_ref[rows, (2 * kh + 1) * LANES:(2 * kh + 2) * LANES]], axis=0)
            k_pad = jnp.concatenate([jnp.concatenate([kcat, zk], axis=1),
                                     jnp.concatenate([zk, kcat], axis=1)], axis=0)
            scores.append(lax.dot_general(k_pad, qg, nt, preferred_element_type=F32))
        if pending is not None:
            project(*pending)
        probs, rdens = [], []
        for kh in range(N_KV_HEADS):
            for pos in range(2):
                pr, rd = [], []
                for half in range(2):
                    h = kh * GROUP + 2 * half + pos
                    slope = 2.0 ** (-8.0 * (h + 1) / N_HEADS)
                    qcols = slice(half * WINDOW, (half + 1) * WINDOW)
                    krow = pos * 2 * WINDOW
                    st = scores[kh]
                    s = (jnp.where(own, st[krow + WINDOW:krow + 2 * WINDOW, qcols],
                                   st[krow:krow + WINDOW, qcols]) * c_scale
                         - (slope * log2e) * dist)
                    if sb == 0:
                        s = jnp.where(visible0, s, -jnp.inf)
                    m = jnp.max(s, axis=0, keepdims=True)
                    p = jnp.exp2(s - m)
                    den = jnp.sum(p, axis=0, keepdims=True) + jnp.exp2(sink_ref[h] * log2e - m)
                    rd.append(1.0 / den)
                    pr.append(jnp.concatenate(
                        [jnp.where(own, 0.0, p).astype(BF16), jnp.where(own, p, 0.0).astype(BF16)],
                        axis=0))
                probs.append(jnp.concatenate(pr, axis=1))
                rdens.append(jnp.concatenate(rd, axis=1))
        out_t = [None] * N_HEADS
        for kh in range(N_KV_HEADS):
            for pos in range(2):
                o2 = jnp.dot(vcats[kh], probs[2 * kh + pos], preferred_element_type=F32)
                o2 = o2 * rdens[2 * kh + pos]
                out_t[kh * GROUP + pos] = o2[:, :WINDOW]
                out_t[kh * GROUP + 2 + pos] = o2[:, WINDOW:]
        pending = (rows, jnp.transpose(jnp.concatenate(out_t, axis=0)).astype(BF16))
        prev_k, prev_vt = cur_k, cur_vt
    project(*pending)


def _attention(proj, za, sinks, wb, seq, chunk, t):
    sub = TQ_ATTN // WINDOW
    n_tiles = t // TQ_ATTN
    first = chunk * n_tiles
    src = lambda w, c: pl.BlockSpec((TQ_ATTN, w), lambda i: (first + i, c))
    prev = lambda c: pl.BlockSpec((WINDOW, KV_WIDTH),
                                  lambda i: (jnp.maximum((first + i) * sub - 1, 0), c))
    return pl.pallas_call(
        functools.partial(_attn_kernel, first_tile_index=first, tiles_per_seq=seq // TQ_ATTN),
        name="swattn",
        grid=(n_tiles,),
        in_specs=[
            pl.BlockSpec(memory_space=pltpu.SMEM),
            src(D_MODEL, COL_Q), src(KV_WIDTH, COL_K), src(KV_WIDTH, COL_V), prev(COL_K), prev(COL_V),
            src(D_MODEL, COL_GB), src(D_MODEL, 0),
            pl.BlockSpec((D_MODEL, D_MODEL), lambda i: (0, 0), pipeline_mode=pl.Buffered(1)),
        ],
        out_specs=pl.BlockSpec((TQ_ATTN, D_MODEL), lambda i: (i, 0)),
        out_shape=jax.ShapeDtypeStruct((t, D_MODEL), BF16),
        compiler_params=pltpu.CompilerParams(
            dimension_semantics=("arbitrary",), vmem_limit_bytes=VMEM_LIMIT),
    )(sinks, proj, proj, proj, proj, proj, proj, za, wb)


def _mix_kernel(merged_ref, x_ref, wo_ref, g_ref, wr_ref, br_ref,
                xmid_ref, h_ref, route_ref, cnt_ref):
    subs = [slice(s * SUB_MIX, (s + 1) * SUB_MIX) for s in range(TM_MIX // SUB_MIX)]
    xm = [x_ref[r, :] + jnp.dot(merged_ref[r, :], wo_ref[...], preferred_element_type=F32)
          for r in subs]
    hs = []
    for r, v in zip(subs, xm):
        xmid_ref[r, :] = _pack_bf16_pairs(v)
        h = _rms(v, g_ref[...])
        h_ref[r, :] = _pack_bf16_pairs(h)
        hs.append(h)
    wr = wr_ref[...]
    logits = []
    for h in hs:
        h_hi = h.astype(BF16)
        h_lo = (h - h_hi.astype(F32)).astype(BF16)
        both = jnp.dot(h_hi, wr, preferred_element_type=F32)
        logits.append(both[:, :LANES] + both[:, LANES:]
                      + jnp.dot(h_lo, wr[:, :LANES], preferred_element_type=F32) + br_ref[...])
    subs_per_cnt = TM_CNT // SUB_MIX
    cnts = [jnp.zeros((8, LANES), F32) for _ in range(TM_MIX // TM_CNT)]
    for s, (r, lg) in enumerate(zip(subs, logits)):
        route, cnt = _route(lg)
        route_ref[r, :] = route
        cnts[s // subs_per_cnt] = cnts[s // subs_per_cnt] + cnt
    for c, cnt in enumerate(cnts):
        cnt_ref[c * 8:(c + 1) * 8, :] = cnt


def _route(logits):
    n = logits.shape[0]
    lt = jnp.transpose(logits)
    sub = lax.broadcasted_iota(I32, (EXPERTS_PER_GROUP, n), 0)
    neg = -jnp.inf
    gl = jnp.where(sub < N_GROUPS, lt[N_EXPERTS:N_EXPERTS + EXPERTS_PER_GROUP], neg)
    gmax = jnp.max(gl, axis=0, keepdims=True)
    g_idx = jnp.min(jnp.where(gl == gmax, sub, EXPERTS_PER_GROUP), axis=0, keepdims=True)
    p_g = 1.0 / jnp.sum(jnp.exp(gl - gmax), axis=0, keepdims=True)
    v1 = v2 = i1 = i2 = None
    for g in range(N_GROUPS):
        eg = lt[g * EXPERTS_PER_GROUP:(g + 1) * EXPERTS_PER_GROUP]
        a1 = jnp.max(eg, axis=0, keepdims=True)
        j1 = jnp.min(jnp.where(eg == a1, sub, EXPERTS_PER_GROUP), axis=0, keepdims=True)
        eg2 = jnp.where(sub == j1, neg, eg)
        a2 = jnp.max(eg2, axis=0, keepdims=True)
        j2 = jnp.min(jnp.where(eg2 == a2, sub, EXPERTS_PER_GROUP), axis=0, keepdims=True)
        if g == 0:
            v1, v2, i1, i2 = a1, a2, j1, j2
        else:
            chosen = g_idx == g
            v1, v2 = jnp.where(chosen, a1, v1), jnp.where(chosen, a2, v2)
            i1, i2 = jnp.where(chosen, j1, i1), jnp.where(chosen, j2, i2)
    e21 = jnp.exp(v2 - v1)
    w1 = p_g / (1.0 + e21)
    w2 = p_g * e21 / (1.0 + e21)
    e1 = g_idx * EXPERTS_PER_GROUP + i1
    e2 = g_idx * EXPERTS_PER_GROUP + i2
    rows8 = jnp.where(sub == 0, e1.astype(F32),
                      jnp.where(sub == 1, e2.astype(F32),
                                jnp.where(sub == 2, w1, jnp.where(sub == 3, w2, 0.0))))
    route_t = jnp.concatenate([rows8, jnp.zeros((LANES - EXPERTS_PER_GROUP, n), F32)], axis=0)
    expert_row = lax.broadcasted_iota(I32, (LANES, n), 0)
    onehot_t = ((expert_row == e1) | (expert_row == e2)).astype(BF16)
    cnt = lax.dot_general(jnp.ones((8, n), BF16), onehot_t, (((1,), (1,)), ((), ())),
                          preferred_element_type=F32)
    return jnp.transpose(route_t), cnt


def _mix(merged, xf, wo, g, wr, br, chunk, t):
    n_tiles = t // TM_MIX
    first = chunk * n_tiles
    cnt_rows = TM_MIX // TM_CNT * 8
    full = lambda shape: pl.BlockSpec(shape, lambda i: (0, 0), pipeline_mode=pl.Buffered(1))
    tile = lambda w=D_MODEL: pl.BlockSpec((TM_MIX, w), lambda i: (i, 0))
    return pl.pallas_call(
        _mix_kernel,
        name="merge_router",
        grid=(n_tiles,),
        in_specs=[
            tile(), pl.BlockSpec((TM_MIX, D_MODEL), lambda i: (first + i, 0)),
            full((D_MODEL, D_MODEL)), full((1, D_MODEL)),
            full((D_MODEL, 2 * LANES)), full((1, LANES)),
        ],
        out_specs=[tile(PACKED), tile(PACKED), tile(LANES),
                   pl.BlockSpec((cnt_rows, LANES), lambda i: (i, 0))],
        out_shape=[
            jax.ShapeDtypeStruct((t, PACKED), I32),
            jax.ShapeDtypeStruct((t, PACKED), I32),
            jax.ShapeDtypeStruct((t, LANES), F32),
            jax.ShapeDtypeStruct((n_tiles * cnt_rows, LANES), F32),
        ],
        compiler_params=pltpu.CompilerParams(
            dimension_semantics=("arbitrary",), vmem_limit_bytes=VMEM_LIMIT),
    )(merged, xf, wo, g, wr, br)


def _pos_kernel(route_ref, base_ref, pos_ref):
    lane = lax.broadcasted_iota(I32, (TM_CNT, LANES), 1)
    r = lax.broadcasted_iota(I32, (TM_CNT, TM_CNT), 0)
    c = lax.broadcasted_iota(I32, (TM_CNT, TM_CNT), 1)
    lower = (c < r).astype(BF16)
    subs = [slice(s * TM_CNT, (s + 1) * TM_CNT) for s in range(POS_TILES)]
    routes = [route_ref[rs, :] for rs in subs]
    e1 = [jnp.sum(jnp.where(lane == 0, rt, 0.0), axis=-1, keepdims=True).astype(I32) for rt in routes]
    e2 = [jnp.sum(jnp.where(lane == 1, rt, 0.0), axis=-1, keepdims=True).astype(I32) for rt in routes]
    onehot = [((lane == a) | (lane == b)).astype(BF16) for a, b in zip(e1, e2)]
    before = [jnp.dot(lower, oh, preferred_element_type=F32) + base_ref[s]
              for s, oh in enumerate(onehot)]
    for s, rs in enumerate(subs):
        p1 = jnp.sum(jnp.where(lane == e1[s], before[s], 0.0), axis=-1, keepdims=True)
        p2 = jnp.sum(jnp.where(lane == e2[s], before[s], 0.0), axis=-1, keepdims=True)
        packed = jnp.where(lane == 0, p1, jnp.where(lane == 1, p2, 0.0))
        pos_ref[:, rs] = jnp.transpose(packed)[0:TOP_K, :].astype(I32)


def _positions(route, base):
    t = route.shape[0]
    n_steps = t // (TM_CNT * POS_TILES)
    return pl.pallas_call(
        _pos_kernel,
        name="positions",
        grid=(n_steps,),
        in_specs=[
            pl.BlockSpec((TM_CNT * POS_TILES, LANES), lambda i: (i, 0)),
            pl.BlockSpec((POS_TILES, 1, LANES), lambda i: (i, 0, 0)),
        ],
        out_specs=pl.BlockSpec((TOP_K, TM_CNT * POS_TILES), lambda i: (0, i)),
        out_shape=jax.ShapeDtypeStruct((TOP_K, t), I32),
        compiler_params=pltpu.CompilerParams(dimension_semantics=("arbitrary",)),
    )(route, base)


def _sc_mesh():
    return plsc.VectorSubcoreMesh(core_axis_name="c", subcore_axis_name="s",
                                  num_cores=SC_CORES, num_subcores=SC_SUBCORES)


def _sc_worker():
    return lax.axis_index("s") * SC_CORES + lax.axis_index("c")


def _dispatch(pos, hp, rows):
    t = hp.shape[0]
    per_w = t // SC_WORKERS
    n_ch = per_w // SC_CHUNK
    pos4 = pos.reshape(TOP_K, SC_WORKERS, n_ch, SC_CHUNK)

    @functools.partial(
        pl.kernel, mesh=_sc_mesh(),
        out_type=jax.ShapeDtypeStruct((rows, PACKED), I32),
        scratch_types=[pltpu.VMEM((TOP_K, n_ch, SC_CHUNK), I32),
                       pltpu.VMEM((SC_CHUNK, PACKED), I32)])
    def scatter(hp_hbm, pos_hbm, xs_hbm, idx_v, rows_v):
        wid = _sc_worker()
        for k in range(TOP_K):
            pltpu.sync_copy(pos_hbm.at[k, wid], idx_v.at[k])

        def body(c, carry):
            start = pl.multiple_of(wid * per_w + c * SC_CHUNK, SC_CHUNK)
            pltpu.sync_copy(hp_hbm.at[pl.ds(start, SC_CHUNK)], rows_v)
            for k in range(TOP_K):
                pltpu.sync_copy(rows_v, xs_hbm.at[idx_v.at[k, c]])
            return carry

        lax.fori_loop(0, n_ch, body, 0)

    return scatter(hp, pos4)


def _gather_rows(table, idx):
    n = idx.shape[0]
    per_w = n // SC_WORKERS
    n_ch = per_w // SC_CHUNK
    idx3 = idx.reshape(SC_WORKERS, n_ch, SC_CHUNK)

    @functools.partial(
        pl.kernel, mesh=_sc_mesh(),
        out_type=jax.ShapeDtypeStruct((n, PACKED), I32),
        scratch_types=[pltpu.VMEM((n_ch, SC_CHUNK), I32),
                       pltpu.VMEM((SC_CHUNK, PACKED), I32)])
    def gather(table_hbm, idx_hbm, out_hbm, idx_v, rows_v):
        wid = _sc_worker()
        pltpu.sync_copy(idx_hbm.at[wid], idx_v)

        def body(c, carry):
            start = pl.multiple_of(wid * per_w + c * SC_CHUNK, SC_CHUNK)
            pltpu.sync_copy(table_hbm.at[idx_v.at[c]], rows_v)
            pltpu.sync_copy(rows_v, out_hbm.at[pl.ds(start, SC_CHUNK)])
            return carry

        lax.fori_loop(0, n_ch, body, 0)

    return gather(table, idx3)


def _expert_kernel(te_ref, nx_ref, sl_ref, ts_ref, tv_ref, xs_ref, wg_hbm, wu_hbm, wd_hbm, o_ref,
                   wg_s, wu_s, wd_s, sem):
    del ts_ref
    i = pl.program_id(0)
    n_valid = tv_ref[i]
    expert = te_ref[i]
    slot = sl_ref[i]

    def weight_copies(e, s):
        return (pltpu.make_async_copy(wg_hbm.at[e], wg_s.at[s], sem.at[s, 0]),
                pltpu.make_async_copy(wu_hbm.at[e], wu_s.at[s], sem.at[s, 1]),
                pltpu.make_async_copy(wd_hbm.at[e], wd_s.at[s], sem.at[s, 2]))

    @pl.when(i == 0)
    def _():
        for copy in weight_copies(expert, slot):
            copy.start()

    @pl.when(jnp.logical_or(i == 0, expert != te_ref[jnp.maximum(i - 1, 0)]))
    def _():
        for copy in weight_copies(expert, slot):
            copy.wait()
        nxt = nx_ref[i]

        @pl.when(nxt >= 0)
        def _():
            for copy in weight_copies(nxt, 1 - slot):
                copy.start()

    def mlp(n_sub):
        wg, wu, wd = wg_s.at[slot], wu_s.at[slot], wd_s.at[slot]
        subs = [slice(s * SUB_EXP, (s + 1) * SUB_EXP) for s in range(n_sub)]
        xin = []
        for r in subs:
            rid = r.start + lax.broadcasted_iota(I32, (SUB_EXP, PACKED), 0)
            lo, hi = _unpack_bf16_pairs(jnp.where(rid < n_valid, xs_ref[r, :], 0))
            xin.append((lo.astype(BF16), hi.astype(BF16)))
        ab = [(jnp.dot(lo, wg[:PACKED, :], preferred_element_type=F32)
               + jnp.dot(hi, wg[PACKED:, :], preferred_element_type=F32),
               jnp.dot(lo, wu[:PACKED, :], preferred_element_type=F32)
               + jnp.dot(hi, wu[PACKED:, :], preferred_element_type=F32)) for lo, hi in xin]
        for r, (ai, bi) in zip(subs, ab):
            hm = (ai * jax.nn.sigmoid(ai) * bi).astype(BF16)
            o_ref[r, :] = _pack_bf16_pairs(jnp.dot(hm, wd[...], preferred_element_type=F32))
        if n_sub * SUB_EXP < TM_EXP:
            o_ref[n_sub * SUB_EXP:, :] = jnp.zeros((TM_EXP - n_sub * SUB_EXP, PACKED), I32)

    n_subs = TM_EXP // SUB_EXP
    for n_sub in range(n_subs + 1):
        lo_rows = (n_sub - 1) * SUB_EXP if n_sub else -1
        in_range = jnp.logical_and(n_valid > lo_rows, n_valid <= n_sub * SUB_EXP)
        pl.when(in_range)(functools.partial(mlp, n_sub))


def _experts(tile_expert, next_expert, tile_slot, tile_src, tile_valid, xs, wg, wu, wd):
    rows = xs.shape[0]
    hbm = pl.BlockSpec(memory_space=pl.ANY)
    grid_spec = pltpu.PrefetchScalarGridSpec(
        num_scalar_prefetch=5,
        grid=(rows // TM_EXP,),
        in_specs=[pl.BlockSpec((TM_EXP, PACKED), lambda i, te, nx, sl, ts, tv: (ts[i], 0)),
                  hbm, hbm, hbm],
        out_specs=pl.BlockSpec((TM_EXP, PACKED), lambda i, te, nx, sl, ts, tv: (i, 0)),
        scratch_shapes=[
            pltpu.VMEM((2, D_MODEL, D_FF), BF16), pltpu.VMEM((2, D_MODEL, D_FF), BF16),
            pltpu.VMEM((2, D_FF, D_MODEL), BF16),
            pltpu.SemaphoreType.DMA((2, 3)),
        ],
    )
    return pl.pallas_call(
        _expert_kernel,
        name="experts",
        grid_spec=grid_spec,
        out_shape=jax.ShapeDtypeStruct((rows, PACKED), I32),
        compiler_params=pltpu.CompilerParams(
            dimension_semantics=("arbitrary",), vmem_limit_bytes=VMEM_LIMIT),
    )(tile_expert, next_expert, tile_slot, tile_src, tile_valid, xs, wg, wu, wd)


def _combine_kernel(y1_ref, y2_ref, route_ref, xmid_ref, g_ref, *rest):
    o_ref = rest[-1]
    route = route_ref[...]
    lane = lax.broadcasted_iota(I32, route.shape, 1)
    w1 = jnp.sum(jnp.where(lane == 2, route, 0.0), axis=-1, keepdims=True)
    w2 = jnp.sum(jnp.where(lane == 3, route, 0.0), axis=-1, keepdims=True)
    lo1, hi1 = _unpack_bf16_pairs(y1_ref[...])
    lo2, hi2 = _unpack_bf16_pairs(y2_ref[...])
    lox, hix = _unpack_bf16_pairs(xmid_ref[...])
    x_out = jnp.concatenate([lox + (lo1 * w1 + lo2 * w2), hix + (hi1 * w1 + hi2 * w2)], axis=1)
    o_ref[...] = _rms(x_out, g_ref[...])


def _combine(yg, route, xmid, g, chunk, t_total, out_prev):
    t = xmid.shape[0]
    n_tiles = t // TM_CMB
    first = chunk * n_tiles
    in_specs = [
        pl.BlockSpec((TM_CMB, PACKED), lambda i: (i, 0)),
        pl.BlockSpec((TM_CMB, PACKED), lambda i: (n_tiles + i, 0)),
        pl.BlockSpec((TM_CMB, LANES), lambda i: (i, 0)),
        pl.BlockSpec((TM_CMB, PACKED), lambda i: (i, 0)),
        pl.BlockSpec((1, D_MODEL), lambda i: (0, 0)),
    ]
    args = [yg, yg, route, xmid, g]
    aliases = {}
    if out_prev is not None:
        in_specs.append(pl.BlockSpec(memory_space=pl.ANY))
        aliases = {len(args): 0}
        args.append(out_prev)
    return pl.pallas_call(
        _combine_kernel,
        name="combine",
        grid=(n_tiles,),
        in_specs=in_specs,
        out_specs=pl.BlockSpec((TM_CMB, D_MODEL), lambda i: (first + i, 0)),
        out_shape=jax.ShapeDtypeStruct((t_total, D_MODEL), F32),
        input_output_aliases=aliases,
        compiler_params=pltpu.CompilerParams(
            dimension_semantics=("arbitrary",), vmem_limit_bytes=VMEM_LIMIT),
    )(*args)


def _split_bf16(w):
    hi = w.astype(BF16)
    lo = (w - hi.astype(F32)).astype(BF16)
    return hi, lo


def kernel(x, norm_mix, w_in, conv_w, conv_b, w_a_out, sinks, w_b_out, w_o, norm_ffn, w_group,
           b_group, w_expert, b_expert, w_gate, w_up, w_down, norm_final):
    bsz, seq, d = x.shape
    t = bsz * seq
    assert d == D_MODEL and seq % TM_PROJ == 0 and seq % TQ_ATTN == 0
    xf = x.reshape(t, d)
    row = lambda v: v.reshape(1, -1)

    assert w_in.shape == (D_MODEL, IN_COLS)
    za, proj, w_gate, w_up, w_down = _inproj(xf, row(norm_mix), w_in.astype(BF16), conv_w,
                                             row(conv_b), w_a_out.astype(BF16), seq,
                                             w_gate, w_up, w_down)
    pad = LANES - N_GROUPS - N_EXPERTS
    w_r = jnp.concatenate([w_expert, w_group, jnp.zeros((d, pad), F32)], axis=1)
    b_r = jnp.concatenate([b_expert, b_group, jnp.zeros((pad,), F32)]).reshape(1, LANES)
    wr = jnp.concatenate(_split_bf16(w_r), axis=1)
    wb = w_b_out.astype(BF16)
    wo = w_o.astype(BF16)

    t_chunk = t // MOE_CHUNKS
    assert t == t_chunk * MOE_CHUNKS and all(
        t_chunk % step == 0
        for step in (TQ_ATTN, TM_MIX, TM_CNT * POS_TILES, TM_CMB, SC_WORKERS * SC_CHUNK))
    out = None
    for chunk in range(MOE_CHUNKS):
        merged = _attention(proj, za, sinks, wb, seq, chunk, t_chunk)
        xmid, h2, route, cnt = _mix(merged, xf, wo, row(norm_ffn), wr, b_r, chunk, t_chunk)
        out = _moe_chunk(xmid, h2, route, cnt, w_gate, w_up, w_down, row(norm_final), chunk, t, out)
    return out.reshape(bsz, seq, d)


def _moe_chunk(xmid, h2, route, cnt, w_gate, w_up, w_down, g_final, chunk, t_total, out_prev):
    t = xmid.shape[0]
    n_tiles = t // TM_CNT
    cnt = cnt.reshape(n_tiles, 8, LANES)[:, 0, :N_EXPERTS].astype(I32)
    totals = jnp.sum(cnt, axis=0)
    tiles_e = (totals + TM_EXP - 1) // TM_EXP
    tile_end = jnp.cumsum(tiles_e)
    offset = (tile_end - tiles_e) * TM_EXP
    base = offset[None, :] + jnp.cumsum(cnt, axis=0) - cnt
    base = jnp.pad(base, ((0, 0), (0, LANES - N_EXPERTS))).astype(F32).reshape(n_tiles, 1, LANES)
    rows = t * TOP_K + N_EXPERTS * TM_EXP
    n_active = tile_end[-1]
    tile_id = jnp.arange(rows // TM_EXP, dtype=I32)
    tile_src = jnp.minimum(tile_id, n_active - 1)
    tile_expert = jnp.sum((tile_src[:, None] >= tile_end[None, :]).astype(I32), axis=1)
    tile_expert = jnp.minimum(tile_expert, N_EXPERTS - 1)
    row_in_expert = (tile_id - (tile_end - tiles_e)[tile_expert]) * TM_EXP
    tile_valid = jnp.clip(totals[tile_expert] - row_in_expert, 0, TM_EXP)
    tile_valid = jnp.where(tile_id < n_active, tile_valid, 0).astype(I32)
    after = tile_end[tile_expert]
    next_expert = jnp.where(after < n_active, tile_expert[jnp.minimum(after, n_active - 1)], -1)
    first_of_expert = jnp.concatenate(
        [jnp.ones((1,), I32), (tile_expert[1:] != tile_expert[:-1]).astype(I32)])
    tile_slot = (jnp.cumsum(first_of_expert) - 1) % 2

    pos = _positions(route, base)
    xs = _dispatch(pos, h2, rows)
    ys = _experts(tile_expert.astype(I32), next_expert.astype(I32), tile_slot.astype(I32),
                  tile_src.astype(I32), tile_valid, xs, w_gate, w_up, w_down)
    yg = _gather_rows(ys, pos.reshape(TOP_K * t))
    return _combine(yg, route, xmid, g_final, chunk, t_total, out_prev)
```

```python
import functools
import math

import jax
import jax.numpy as jnp
from jax import lax
from jax.experimental import pallas as pl
from jax.experimental.pallas import tpu as pltpu
from jax.experimental.pallas import tpu_sc as plsc

F32 = jnp.float32
BF16 = jnp.bfloat16
I32 = jnp.int32

D_MODEL = 1024
HEAD_DIM = 64
N_HEADS = 16
N_KV_HEADS = 4
GROUP = N_HEADS // N_KV_HEADS
KV_WIDTH = N_KV_HEADS * HEAD_DIM
WINDOW = 128
N_GROUPS = 4
EXPERTS_PER_GROUP = 8
N_EXPERTS = N_GROUPS * EXPERTS_PER_GROUP
TOP_K = 2
D_FF = 512
EPS = 1e-6
LANES = 128

COL_WIDTH = {"b": D_MODEL, "c": D_MODEL, "u": D_MODEL, "q": D_MODEL, "k": KV_WIDTH, "v": KV_WIDTH,
             "ga": D_MODEL, "gb": D_MODEL}
COL_START = dict(zip(COL_WIDTH, (sum(list(COL_WIDTH.values())[:i]) for i in range(len(COL_WIDTH)))))
IN_COLS = sum(COL_WIDTH.values())
REST_COLS = 2 * D_MODEL + 2 * KV_WIDTH
COL_GB, COL_Q = 0, 1
COL_K, COL_V = 2 * D_MODEL // KV_WIDTH, 2 * D_MODEL // KV_WIDTH + 1

TM_PROJ = 512
SUB_PROJ = 256
TQ_ATTN = 1024
TM_MIX = 1024
SUB_MIX = 256
TM_CNT = 512
POS_TILES = 8
MOE_CHUNKS = 2
TM_EXP = 512
SUB_EXP = 256
TM_CMB = 2048
HALO_ROWS = 8
VMEM_LIMIT = 56 * 1024 * 1024
PACKED = D_MODEL // 2

SC_CORES = 2
SC_SUBCORES = 16
SC_WORKERS = SC_CORES * SC_SUBCORES
SC_CHUNK = 64


def _rms(x, g):
    r = lax.rsqrt(jnp.mean(x * x, axis=-1, keepdims=True) + EPS)
    return (x * r) * g


def _pack_bf16_pairs(x):
    n = x.shape[1] // 2
    lo = lax.bitcast_convert_type(x[:, :n].astype(BF16).astype(F32), I32)
    hi = lax.bitcast_convert_type(x[:, n:].astype(BF16).astype(F32), I32)
    return (hi & jnp.int32(-65536)) | lax.shift_right_logical(lo, 16)


def _unpack_bf16_pairs(p):
    lo = lax.bitcast_convert_type(lax.shift_left(p, 16), F32)
    hi = lax.bitcast_convert_type(p & jnp.int32(-65536), F32)
    return lo, hi


def _inproj_kernel(x_ref, g_ref, w_ref, cw_ref, cb_ref, wa_ref,
                   eg_ref, eu_ref, ed_ref,
                   za_ref, proj_ref, eg_out, eu_out, ed_out, halo_ref, *, tiles_per_seq):
    i = pl.program_id(0)
    eg_out[...] = eg_ref[...].astype(BF16)
    eu_out[...] = eu_ref[...].astype(BF16)
    ed_out[...] = ed_ref[...].astype(BF16)
    halves = [slice(s * SUB_PROJ, (s + 1) * SUB_PROJ) for s in range(TM_PROJ // SUB_PROJ)]
    hs = [_rms(x_ref[r, :], g_ref[...]).astype(BF16) for r in halves]
    col = lambda a, b=None: slice(COL_START[a], COL_START[b or a] + COL_WIDTH[b or a])
    proj = lambda a, b=None: [jnp.dot(h, w_ref[:, col(a, b)], preferred_element_type=F32) for h in hs]
    pcu = jnp.concatenate(proj("c", "u"), axis=0)
    pb = jnp.concatenate(proj("b"), axis=0)
    pga = jnp.concatenate(proj("ga"), axis=0)
    for r, pgb, pqkv in zip(halves, proj("gb"), proj("q", "v")):
        proj_ref[r, :D_MODEL] = pgb.astype(BF16)
        proj_ref[r, D_MODEL:] = pqkv.astype(BF16)
    cu = pcu[:, :D_MODEL] * pcu[:, D_MODEL:]
    first = (i % tiles_per_seq) == 0
    hist = jnp.where(first, 0.0, halo_ref[...])
    prev1 = hist[HALO_ROWS - 1:HALO_ROWS]
    prev2 = hist[HALO_ROWS - 2:HALO_ROWS - 1]
    halo_ref[...] = cu[TM_PROJ - HALO_ROWS:, :]
    row = lax.broadcasted_iota(I32, cu.shape, 0)
    cu1 = jnp.where(row == 0, prev1, pltpu.roll(cu, 1, 0))
    cu2 = jnp.where(row == 0, prev2, jnp.where(row == 1, prev1, pltpu.roll(cu, 2, 0)))
    cw = cw_ref[...]
    y = cw[0:1] * cu2 + cw[1:2] * cu1 + cw[2:3] * cu + cb_ref[...]
    ya = (pb * y).astype(BF16)
    z = jnp.dot(ya, wa_ref[...], preferred_element_type=F32)
    za_ref[...] = (jax.nn.sigmoid(pga) * z).astype(BF16)


def _inproj(xf, g, w_in, conv_w, conv_b, wa, seq, w_gate, w_up, w_down):
    t = xf.shape[0]
    n_tiles = t // TM_PROJ
    const = lambda shape: pl.BlockSpec(shape, lambda i: (0, 0), pipeline_mode=pl.Buffered(1))
    slabs = [w.reshape(-1, w.shape[-1]) for w in (w_gate, w_up, w_down)]
    slab_rows = [s.shape[0] // n_tiles for s in slabs]
    assert all(s.shape[0] == r * n_tiles and r % 16 == 0 for s, r in zip(slabs, slab_rows))
    slab_specs = [pl.BlockSpec((r, s.shape[1]), lambda i: (i, 0)) for s, r in zip(slabs, slab_rows)]
    outs = pl.pallas_call(
        functools.partial(_inproj_kernel, tiles_per_seq=seq // TM_PROJ),
        name="inproj_conv",
        grid=(n_tiles,),
        in_specs=[
            pl.BlockSpec((TM_PROJ, D_MODEL), lambda i: (i, 0)),
            const((1, D_MODEL)),
            const((D_MODEL, IN_COLS)),
            const((3, D_MODEL)), const((1, D_MODEL)), const((D_MODEL, D_MODEL)),
        ] + slab_specs,
        out_specs=[pl.BlockSpec((TM_PROJ, D_MODEL), lambda i: (i, 0)),
                   pl.BlockSpec((TM_PROJ, REST_COLS), lambda i: (i, 0))] + slab_specs,
        out_shape=[jax.ShapeDtypeStruct((t, D_MODEL), BF16),
                   jax.ShapeDtypeStruct((t, REST_COLS), BF16)]
                  + [jax.ShapeDtypeStruct(s.shape, BF16) for s in slabs],
        scratch_shapes=[pltpu.VMEM((HALO_ROWS, D_MODEL), F32)],
        compiler_params=pltpu.CompilerParams(
            dimension_semantics=("arbitrary",), vmem_limit_bytes=VMEM_LIMIT),
    )(xf, g, w_in, conv_w, conv_b, wa, *slabs)
    za, proj, eg, eu, ed = outs
    return za, proj, eg.reshape(w_gate.shape), eu.reshape(w_up.shape), ed.reshape(w_down.shape)


def _attn_kernel(sink_ref, q_ref, k_ref, v_ref, kp_ref, vp_ref, gb_ref, za_ref, wb_ref, o_ref,
                 *, first_tile_index, tiles_per_seq):
    first_tile = ((first_tile_index + pl.program_id(0)) % tiles_per_seq) == 0
    ks = lax.broadcasted_iota(I32, (WINDOW, WINDOW), 0)
    qq = lax.broadcasted_iota(I32, (WINDOW, WINDOW), 1)
    own = ks <= qq
    dist = jnp.where(own, qq - ks, qq - ks + WINDOW).astype(F32)
    visible0 = jnp.logical_or(own, jnp.logical_not(first_tile))
    log2e = math.log2(math.e)
    c_scale = log2e / math.sqrt(HEAD_DIM)
    nt = (((1,), (1,)), ((), ()))
    zk = jnp.zeros((2 * WINDOW, HEAD_DIM), BF16)

    def transposed(v_blk):
        return jnp.transpose(v_blk.astype(F32)).astype(BF16)

    def project(rows_p, attn_blk):
        yb = jnp.dot(attn_blk, wb_ref[...], preferred_element_type=F32)
        zb = jax.nn.sigmoid(gb_ref[rows_p, :].astype(F32)) * yb
        o_ref[rows_p, :] = (za_ref[rows_p, :].astype(F32) + zb).astype(BF16)

    pending = None
    prev_k = kp_ref[...]
    prev_vt = transposed(vp_ref[...])
    for sb in range(TQ_ATTN // WINDOW):
        rows = slice(sb * WINDOW, (sb + 1) * WINDOW)
        cur_k = k_ref[rows, :]
        cur_vt = transposed(v_ref[rows, :])
        scores, vcats = [], []
        for kh in range(N_KV_HEADS):
            cols = slice(kh * HEAD_DIM, (kh + 1) * HEAD_DIM)
            kcat = jnp.concatenate([prev_k[:, cols], cur_k[:, cols]], axis=0)
            vcats.append(jnp.concatenate([prev_vt[cols, :], cur_vt[cols, :]], axis=1))
            qg = jnp.concatenate([q_ref[rows, (2 * kh) * LANES:(2 * kh + 1) * LANES],
                                  q_ref[rows, (2 * kh + 1) * LANES:(2 * kh + 2) * LANES]], axis=0)
            k_pad = jnp.concatenate([jnp.concatenate([kcat, zk], axis=1),
                                     jnp.concatenate([zk, kcat], axis=1)], axis=0)
            scores.append(lax.dot_general(k_pad, qg, nt, preferred_element_type=F32))
        if pending is not None:
            project(*pending)
        probs, rdens = [], []
        for kh in range(N_KV_HEADS):
            for pos in range(2):
                pr, rd = [], []
                for half in range(2):
                    h = kh * GROUP + 2 * half + pos
                    slope = 2.0 ** (-8.0 * (h + 1) / N_HEADS)
                    qcols = slice(half * WINDOW, (half + 1) * WINDOW)
                    krow = pos * 2 * WINDOW
                    st = scores[kh]
                    s = (jnp.where(own, st[krow + WINDOW:krow + 2 * WINDOW, qcols],
                                   st[krow:krow + WINDOW, qcols]) * c_scale
                         - (slope * log2e) * dist)
                    if sb == 0:
                        s = jnp.where(visible0, s, -jnp.inf)
                    m = jnp.max(s, axis=0, keepdims=True)
                    p = jnp.exp2(s - m)
                    den = jnp.sum(p, axis=0, keepdims=True) + jnp.exp2(sink_ref[h] * log2e - m)
                    rd.append(1.0 / den)
                    pr.append(jnp.concatenate(
                        [jnp.where(own, 0.0, p).astype(BF16), jnp.where(own, p, 0.0).astype(BF16)],
                        axis=0))
                probs.append(jnp.concatenate(pr, axis=1))
                rdens.append(jnp.concatenate(rd, axis=1))
        out_t = [None] * N_HEADS
        for kh in range(N_KV_HEADS):
            for pos in range(2):
                o2 = jnp.dot(vcats[kh], probs[2 * kh + pos], preferred_element_type=F32)
                o2 = o2 * rdens[2 * kh + pos]
                out_t[kh * GROUP + pos] = o2[:, :WINDOW]
                out_t[kh * GROUP + 2 + pos] = o2[:, WINDOW:]
        pending = (rows, jnp.transpose(jnp.concatenate(out_t, axis=0)).astype(BF16))
        prev_k, prev_vt = cur_k, cur_vt
    project(*pending)


def _attention(proj, za, sinks, wb, seq, chunk, t):
    sub = TQ_ATTN // WINDOW
    n_tiles = t // TQ_ATTN
    first = chunk * n_tiles
    src = lambda w, c: pl.BlockSpec((TQ_ATTN, w), lambda i: (first + i, c))
    prev = lambda c: pl.BlockSpec((WINDOW, KV_WIDTH),
                                  lambda i: (jnp.maximum((first + i) * sub - 1, 0), c))
    return pl.pallas_call(
        functools.partial(_attn_kernel, first_tile_index=first, tiles_per_seq=seq // TQ_ATTN),
        name="swattn",
        grid=(n_tiles,),
        in_specs=[
            pl.BlockSpec(memory_space=pltpu.SMEM),
            src(D_MODEL, COL_Q), src(KV_WIDTH, COL_K), src(KV_WIDTH, COL_V), prev(COL_K), prev(COL_V),
            src(D_MODEL, COL_GB), src(D_MODEL, 0),
            pl.BlockSpec((D_MODEL, D_MODEL), lambda i: (0, 0), pipeline_mode=pl.Buffered(1)),
        ],
        out_specs=pl.BlockSpec((TQ_ATTN, D_MODEL), lambda i: (i, 0)),
        out_shape=jax.ShapeDtypeStruct((t, D_MODEL), BF16),
        compiler_params=pltpu.CompilerParams(
            dimension_semantics=("arbitrary",), vmem_limit_bytes=VMEM_LIMIT),
    )(sinks, proj, proj, proj, proj, proj, proj, za, wb)


def _mix_kernel(merged_ref, x_ref, wo_ref, g_ref, wr_ref, br_ref,
                xmid_ref, h_ref, route_ref, cnt_ref):
    subs = [slice(s * SUB_MIX, (s + 1) * SUB_MIX) for s in range(TM_MIX // SUB_MIX)]
    xm = [x_ref[r, :] + jnp.dot(merged_ref[r, :], wo_ref[...], preferred_element_type=F32)
          for r in subs]
    hs = []
    for r, v in zip(subs, xm):
        xmid_ref[r, :] = _pack_bf16_pairs(v)
        h = _rms(v, g_ref[...])
        h_ref[r, :] = _pack_bf16_pairs(h)
        hs.append(h)
    wr = wr_ref[...]
    logits = []
    for h in hs:
        h_hi = h.astype(BF16)
        h_lo = (h - h_hi.astype(F32)).astype(BF16)
        both = jnp.dot(h_hi, wr, preferred_element_type=F32)
        logits.append(both[:, :LANES] + both[:, LANES:]
                      + jnp.dot(h_lo, wr[:, :LANES], preferred_element_type=F32) + br_ref[...])
    subs_per_cnt = TM_CNT // SUB_MIX
    cnts = [jnp.zeros((8, LANES), F32) for _ in range(TM_MIX // TM_CNT)]
    for s, (r, lg) in enumerate(zip(subs, logits)):
        route, cnt = _route(lg)
        route_ref[r, :] = route
        cnts[s // subs_per_cnt] = cnts[s // subs_per_cnt] + cnt
    for c, cnt in enumerate(cnts):
        cnt_ref[c * 8:(c + 1) * 8, :] = cnt


def _route(logits):
    n = logits.shape[0]
    lt = jnp.transpose(logits)
    sub = lax.broadcasted_iota(I32, (EXPERTS_PER_GROUP, n), 0)
    neg = -jnp.inf
    gl = jnp.where(sub < N_GROUPS, lt[N_EXPERTS:N_EXPERTS + EXPERTS_PER_GROUP], neg)
    gmax = jnp.max(gl, axis=0, keepdims=True)
    g_idx = jnp.min(jnp.where(gl == gmax, sub, EXPERTS_PER_GROUP), axis=0, keepdims=True)
    p_g = 1.0 / jnp.sum(jnp.exp(gl - gmax), axis=0, keepdims=True)
    v1 = v2 = i1 = i2 = None
    for g in range(N_GROUPS):
        eg = lt[g * EXPERTS_PER_GROUP:(g + 1) * EXPERTS_PER_GROUP]
        a1 = jnp.max(eg, axis=0, keepdims=True)
        j1 = jnp.min(jnp.where(eg == a1, sub, EXPERTS_PER_GROUP), axis=0, keepdims=True)
        eg2 = jnp.where(sub == j1, neg, eg)
        a2 = jnp.max(eg2, axis=0, keepdims=True)
        j2 = jnp.min(jnp.where(eg2 == a2, sub, EXPERTS_PER_GROUP), axis=0, keepdims=True)
        if g == 0:
            v1, v2, i1, i2 = a1, a2, j1, j2
        else:
            chosen = g_idx == g
            v1, v2 = jnp.where(chosen, a1, v1), jnp.where(chosen, a2, v2)
            i1, i2 = jnp.where(chosen, j1, i1), jnp.where(chosen, j2, i2)
    e21 = jnp.exp(v2 - v1)
    w1 = p_g / (1.0 + e21)
    w2 = p_g * e21 / (1.0 + e21)
    e1 = g_idx * EXPERTS_PER_GROUP + i1
    e2 = g_idx * EXPERTS_PER_GROUP + i2
    rows8 = jnp.where(sub == 0, e1.astype(F32),
                      jnp.where(sub == 1, e2.astype(F32),
                                jnp.where(sub == 2, w1, jnp.where(sub == 3, w2, 0.0))))
    route_t = jnp.concatenate([rows8, jnp.zeros((LANES - EXPERTS_PER_GROUP, n), F32)], axis=0)
    expert_row = lax.broadcasted_iota(I32, (LANES, n), 0)
    onehot_t = ((expert_row == e1) | (expert_row == e2)).astype(BF16)
    cnt = lax.dot_general(jnp.ones((8, n), BF16), onehot_t, (((1,), (1,)), ((), ())),
                          preferred_element_type=F32)
    return jnp.transpose(route_t), cnt


def _mix(merged, xf, wo, g, wr, br, chunk, t):
    n_tiles = t // TM_MIX
    first = chunk * n_tiles
    cnt_rows = TM_MIX // TM_CNT * 8
    full = lambda shape: pl.BlockSpec(shape, lambda i: (0, 0), pipeline_mode=pl.Buffered(1))
    tile = lambda w=D_MODEL: pl.BlockSpec((TM_MIX, w), lambda i: (i, 0))
    return pl.pallas_call(
        _mix_kernel,
        name="merge_router",
        grid=(n_tiles,),
        in_specs=[
            tile(), pl.BlockSpec((TM_MIX, D_MODEL), lambda i: (first + i, 0)),
            full((D_MODEL, D_MODEL)), full((1, D_MODEL)),
            full((D_MODEL, 2 * LANES)), full((1, LANES)),
        ],
        out_specs=[tile(PACKED), tile(PACKED), tile(LANES),
                   pl.BlockSpec((cnt_rows, LANES), lambda i: (i, 0))],
        out_shape=[
            jax.ShapeDtypeStruct((t, PACKED), I32),
            jax.ShapeDtypeStruct((t, PACKED), I32),
            jax.ShapeDtypeStruct((t, LANES), F32),
            jax.ShapeDtypeStruct((n_tiles * cnt_rows, LANES), F32),
        ],
        compiler_params=pltpu.CompilerParams(
            dimension_semantics=("arbitrary",), vmem_limit_bytes=VMEM_LIMIT),
    )(merged, xf, wo, g, wr, br)


def _pos_kernel(route_ref, base_ref, pos_ref):
    lane = lax.broadcasted_iota(I32, (TM_CNT, LANES), 1)
    r = lax.broadcasted_iota(I32, (TM_CNT, TM_CNT), 0)
    c = lax.broadcasted_iota(I32, (TM_CNT, TM_CNT), 1)
    lower = (c < r).astype(BF16)
    subs = [slice(s * TM_CNT, (s + 1) * TM_CNT) for s in range(POS_TILES)]
    routes = [route_ref[rs, :] for rs in subs]
    e1 = [jnp.sum(jnp.where(lane == 0, rt, 0.0), axis=-1, keepdims=True).astype(I32) for rt in routes]
    e2 = [jnp.sum(jnp.where(lane == 1, rt, 0.0), axis=-1, keepdims=True).astype(I32) for rt in routes]
    onehot = [((lane == a) | (lane == b)).astype(BF16) for a, b in zip(e1, e2)]
    before = [jnp.dot(lower, oh, preferred_element_type=F32) + base_ref[s]
              for s, oh in enumerate(onehot)]
    for s, rs in enumerate(subs):
        p1 = jnp.sum(jnp.where(lane == e1[s], before[s], 0.0), axis=-1, keepdims=True)
        p2 = jnp.sum(jnp.where(lane == e2[s], before[s], 0.0), axis=-1, keepdims=True)
        packed = jnp.where(lane == 0, p1, jnp.where(lane == 1, p2, 0.0))
        pos_ref[:, rs] = jnp.transpose(packed)[0:TOP_K, :].astype(I32)


def _positions(route, base):
    t = route.shape[0]
    n_steps = t // (TM_CNT * POS_TILES)
    return pl.pallas_call(
        _pos_kernel,
        name="positions",
        grid=(n_steps,),
        in_specs=[
            pl.BlockSpec((TM_CNT * POS_TILES, LANES), lambda i: (i, 0)),
            pl.BlockSpec((POS_TILES, 1, LANES), lambda i: (i, 0, 0)),
        ],
        out_specs=pl.BlockSpec((TOP_K, TM_CNT * POS_TILES), lambda i: (0, i)),
        out_shape=jax.ShapeDtypeStruct((TOP_K, t), I32),
        compiler_params=pltpu.CompilerParams(dimension_semantics=("arbitrary",)),
    )(route, base)


def _sc_mesh():
    return plsc.VectorSubcoreMesh(core_axis_name="c", subcore_axis_name="s",
                                  num_cores=SC_CORES, num_subcores=SC_SUBCORES)


def _sc_worker():
    return lax.axis_index("s") * SC_CORES + lax.axis_index("c")


def _dispatch(pos, hp, rows):
    t = hp.shape[0]
    per_w = t // SC_WORKERS
    n_ch = per_w // SC_CHUNK
    pos4 = pos.reshape(TOP_K, SC_WORKERS, n_ch, SC_CHUNK)

    @functools.partial(
        pl.kernel, mesh=_sc_mesh(),
        out_type=jax.ShapeDtypeStruct((rows, PACKED), I32),
        scratch_types=[pltpu.VMEM((TOP_K, n_ch, SC_CHUNK), I32),
                       pltpu.VMEM((SC_CHUNK, PACKED), I32)])
    def scatter(hp_hbm, pos_hbm, xs_hbm, idx_v, rows_v):
        wid = _sc_worker()
        for k in range(TOP_K):
            pltpu.sync_copy(pos_hbm.at[k, wid], idx_v.at[k])

        def body(c, carry):
            start = pl.multiple_of(wid * per_w + c * SC_CHUNK, SC_CHUNK)
            pltpu.sync_copy(hp_hbm.at[pl.ds(start, SC_CHUNK)], rows_v)
            for k in range(TOP_K):
                pltpu.sync_copy(rows_v, xs_hbm.at[idx_v.at[k, c]])
            return carry

        lax.fori_loop(0, n_ch, body, 0)

    return scatter(hp, pos4)


def _gather_rows(table, idx):
    n = idx.shape[0]
    per_w = n // SC_WORKERS
    n_ch = per_w // SC_CHUNK
    idx3 = idx.reshape(SC_WORKERS, n_ch, SC_CHUNK)

    @functools.partial(
        pl.kernel, mesh=_sc_mesh(),
        out_type=jax.ShapeDtypeStruct((n, PACKED), I32),
        scratch_types=[pltpu.VMEM((n_ch, SC_CHUNK), I32),
                       pltpu.VMEM((SC_CHUNK, PACKED), I32)])
    def gather(table_hbm, idx_hbm, out_hbm, idx_v, rows_v):
        wid = _sc_worker()
        pltpu.sync_copy(idx_hbm.at[wid], idx_v)

        def body(c, carry):
            start = pl.multiple_of(wid * per_w + c * SC_CHUNK, SC_CHUNK)
            pltpu.sync_copy(table_hbm.at[idx_v.at[c]], rows_v)
            pltpu.sync_copy(rows_v, out_hbm.at[pl.ds(start, SC_CHUNK)])
            return carry

        lax.fori_loop(0, n_ch, body, 0)

    return gather(table, idx3)


def _expert_kernel(te_ref, nx_ref, sl_ref, ts_ref, tv_ref, xs_ref, wg_hbm, wu_hbm, wd_hbm, o_ref,
                   wg_s, wu_s, wd_s, sem):
    del ts_ref
    i = pl.program_id(0)
    n_valid = tv_ref[i]
    expert = te_ref[i]
    slot = sl_ref[i]

    def weight_copies(e, s):
        return (pltpu.make_async_copy(wg_hbm.at[e], wg_s.at[s], sem.at[s, 0]),
                pltpu.make_async_copy(wu_hbm.at[e], wu_s.at[s], sem.at[s, 1]),
                pltpu.make_async_copy(wd_hbm.at[e], wd_s.at[s], sem.at[s, 2]))

    @pl.when(i == 0)
    def _():
        for copy in weight_copies(expert, slot):
            copy.start()

    @pl.when(jnp.logical_or(i == 0, expert != te_ref[jnp.maximum(i - 1, 0)]))
    def _():
        for copy in weight_copies(expert, slot):
            copy.wait()
        nxt = nx_ref[i]

        @pl.when(nxt >= 0)
        def _():
            for copy in weight_copies(nxt, 1 - slot):
                copy.start()

    def mlp(n_sub):
        wg, wu, wd = wg_s.at[slot], wu_s.at[slot], wd_s.at[slot]
        subs = [slice(s * SUB_EXP, (s + 1) * SUB_EXP) for s in range(n_sub)]
        xin = []
        for r in subs:
            rid = r.start + lax.broadcasted_iota(I32, (SUB_EXP, PACKED), 0)
            lo, hi = _unpack_bf16_pairs(jnp.where(rid < n_valid, xs_ref[r, :], 0))
            xin.append((lo.astype(BF16), hi.astype(BF16)))
        ab = [(jnp.dot(lo, wg[:PACKED, :], preferred_element_type=F32)
               + jnp.dot(hi, wg[PACKED:, :], preferred_element_type=F32),
               jnp.dot(lo, wu[:PACKED, :], preferred_element_type=F32)
               + jnp.dot(hi, wu[PACKED:, :], preferred_element_type=F32)) for lo, hi in xin]
        for r, (ai, bi) in zip(subs, ab):
            hm = (ai * jax.nn.sigmoid(ai) * bi).astype(BF16)
            o_ref[r, :] = _pack_bf16_pairs(jnp.dot(hm, wd[...], preferred_element_type=F32))
        if n_sub * SUB_EXP < TM_EXP:
            o_ref[n_sub * SUB_EXP:, :] = jnp.zeros((TM_EXP - n_sub * SUB_EXP, PACKED), I32)

    n_subs = TM_EXP // SUB_EXP
    for n_sub in range(n_subs + 1):
        lo_rows = (n_sub - 1) * SUB_EXP if n_sub else -1
        in_range = jnp.logical_and(n_valid > lo_rows, n_valid <= n_sub * SUB_EXP)
        pl.when(in_range)(functools.partial(mlp, n_sub))


def _experts(tile_expert, next_expert, tile_slot, tile_src, tile_valid, xs, wg, wu, wd):
    rows = xs.shape[0]
    hbm = pl.BlockSpec(memory_space=pl.ANY)
    grid_spec = pltpu.PrefetchScalarGridSpec(
        num_scalar_prefetch=5,
        grid=(rows // TM_EXP,),
        in_specs=[pl.BlockSpec((TM_EXP, PACKED), lambda i, te, nx, sl, ts, tv: (ts[i], 0)),
                  hbm, hbm, hbm],
        out_specs=pl.BlockSpec((TM_EXP, PACKED), lambda i, te, nx, sl, ts, tv: (i, 0)),
        scratch_shapes=[
            pltpu.VMEM((2, D_MODEL, D_FF), BF16), pltpu.VMEM((2, D_MODEL, D_FF), BF16),
            pltpu.VMEM((2, D_FF, D_MODEL), BF16),
            pltpu.SemaphoreType.DMA((2, 3)),
        ],
    )
    return pl.pallas_call(
        _expert_kernel,
        name="experts",
        grid_spec=grid_spec,
        out_shape=jax.ShapeDtypeStruct((rows, PACKED), I32),
        compiler_params=pltpu.CompilerParams(
            dimension_semantics=("arbitrary",), vmem_limit_bytes=VMEM_LIMIT),
    )(tile_expert, next_expert, tile_slot, tile_src, tile_valid, xs, wg, wu, wd)


def _combine_kernel(y1_ref, y2_ref, route_ref, xmid_ref, g_ref, *rest):
    o_ref = rest[-1]
    route = route_ref[...]
    lane = lax.broadcasted_iota(I32, route.shape, 1)
    w1 = jnp.sum(jnp.where(lane == 2, route, 0.0), axis=-1, keepdims=True)
    w2 = jnp.sum(jnp.where(lane == 3, route, 0.0), axis=-1, keepdims=True)
    lo1, hi1 = _unpack_bf16_pairs(y1_ref[...])
    lo2, hi2 = _unpack_bf16_pairs(y2_ref[...])
    lox, hix = _unpack_bf16_pairs(xmid_ref[...])
    x_out = jnp.concatenate([lox + (lo1 * w1 + lo2 * w2), hix + (hi1 * w1 + hi2 * w2)], axis=1)
    o_ref[...] = _rms(x_out, g_ref[...])


def _combine(yg, route, xmid, g, chunk, t_total, out_prev):
    t = xmid.shape[0]
    n_tiles = t // TM_CMB
    first = chunk * n_tiles
    in_specs = [
        pl.BlockSpec((TM_CMB, PACKED), lambda i: (i, 0)),
        pl.BlockSpec((TM_CMB, PACKED), lambda i: (n_tiles + i, 0)),
        pl.BlockSpec((TM_CMB, LANES), lambda i: (i, 0)),
        pl.BlockSpec((TM_CMB, PACKED), lambda i: (i, 0)),
        pl.BlockSpec((1, D_MODEL), lambda i: (0, 0)),
    ]
    args = [yg, yg, route, xmid, g]
    aliases = {}
    if out_prev is not None:
        in_specs.append(pl.BlockSpec(memory_space=pl.ANY))
        aliases = {len(args): 0}
        args.append(out_prev)
    return pl.pallas_call(
        _combine_kernel,
        name="combine",
        grid=(n_tiles,),
        in_specs=in_specs,
        out_specs=pl.BlockSpec((TM_CMB, D_MODEL), lambda i: (first + i, 0)),
        out_shape=jax.ShapeDtypeStruct((t_total, D_MODEL), F32),
        input_output_aliases=aliases,
        compiler_params=pltpu.CompilerParams(
            dimension_semantics=("arbitrary",), vmem_limit_bytes=VMEM_LIMIT),
    )(*args)


def _split_bf16(w):
    hi = w.astype(BF16)
    lo = (w - hi.astype(F32)).astype(BF16)
    return hi, lo


def kernel(x, norm_mix, w_in, conv_w, conv_b, w_a_out, sinks, w_b_out, w_o, norm_ffn, w_group,
           b_group, w_expert, b_expert, w_gate, w_up, w_down, norm_final):
    bsz, seq, d = x.shape
    t = bsz * seq
    assert d == D_MODEL and seq % TM_PROJ == 0 and seq % TQ_ATTN == 0
    xf = x.reshape(t, d)
    row = lambda v: v.reshape(1, -1)

    assert w_in.shape == (D_MODEL, IN_COLS)
    za, proj, w_gate, w_up, w_down = _inproj(xf, row(norm_mix), w_in.astype(BF16), conv_w,
                                             row(conv_b), w_a_out.astype(BF16), seq,
                                             w_gate, w_up, w_down)
    pad = LANES - N_GROUPS - N_EXPERTS
    w_r = jnp.concatenate([w_expert, w_group, jnp.zeros((d, pad), F32)], axis=1)
    b_r = jnp.concatenate([b_expert, b_group, jnp.zeros((pad,), F32)]).reshape(1, LANES)
    wr = jnp.concatenate(_split_bf16(w_r), axis=1)
    wb = w_b_out.astype(BF16)
    wo = w_o.astype(BF16)

    t_chunk = t // MOE_CHUNKS
    assert t == t_chunk * MOE_CHUNKS and all(
        t_chunk % step == 0
        for step in (TQ_ATTN, TM_MIX, TM_CNT * POS_TILES, TM_CMB, SC_WORKERS * SC_CHUNK))
    out = None
    for chunk in range(MOE_CHUNKS):
        merged = _attention(proj, za, sinks, wb, seq, chunk, t_chunk)
        xmid, h2, route, cnt = _mix(merged, xf, wo, row(norm_ffn), wr, b_r, chunk, t_chunk)
        out = _moe_chunk(xmid, h2, route, cnt, w_gate, w_up, w_down, row(norm_final), chunk, t, out)
    return out.reshape(bsz, seq, d)


def _moe_chunk(xmid, h2, route, cnt, w_gate, w_up, w_down, g_final, chunk, t_total, out_prev):
    t = xmid.shape[0]
    n_tiles = t // TM_CNT
    cnt = cnt.reshape(n_tiles, 8, LANES)[:, 0, :N_EXPERTS].astype(I32)
    totals = jnp.sum(cnt, axis=0)
    tiles_e = (totals + TM_EXP - 1) // TM_EXP
    tile_end = jnp.cumsum(tiles_e)
    offset = (tile_end - tiles_e) * TM_EXP
    base = offset[None, :] + jnp.cumsum(cnt, axis=0) - cnt
    base = jnp.pad(base, ((0, 0), (0, LANES - N_EXPERTS))).astype(F32).reshape(n_tiles, 1, LANES)
    rows = t * TOP_K + N_EXPERTS * TM_EXP
    n_active = tile_end[-1]
    tile_id = jnp.arange(rows // TM_EXP, dtype=I32)
    tile_src = jnp.minimum(tile_id, n_active - 1)
    tile_expert = jnp.sum((tile_src[:, None] >= tile_end[None, :]).astype(I32), axis=1)
    tile_expert = jnp.minimum(tile_expert, N_EXPERTS - 1)
    row_in_expert = (tile_id - (tile_end - tiles_e)[tile_expert]) * TM_EXP
    tile_valid = jnp.clip(totals[tile_expert] - row_in_expert, 0, TM_EXP)
    tile_valid = jnp.where(tile_id < n_active, tile_valid, 0).astype(I32)
    after = tile_end[tile_expert]
    next_expert = jnp.where(after < n_active, tile_expert[jnp.minimum(after, n_active - 1)], -1)
    first_of_expert = jnp.concatenate(
        [jnp.ones((1,), I32), (tile_expert[1:] != tile_expert[:-1]).astype(I32)])
    tile_slot = (jnp.cumsum(first_of_expert) - 1) % 2

    pos = _positions(route, base)
    xs = _dispatch(pos, h2, rows)
    ys = _experts(tile_expert.astype(I32), next_expert.astype(I32), tile_slot.astype(I32),
                  tile_src.astype(I32), tile_valid, xs, w_gate, w_up, w_down)
    yg = _gather_rows(ys, pos.reshape(TOP_K * t))
    return _combine(yg, route, xmid, g_final, chunk, t_total, out_prev)
```

```python
import functools
import math

import jax
import jax.numpy as jnp
from jax import lax
from jax.experimental import pallas as pl
from jax.experimental.pallas import tpu as pltpu
from jax.experimental.pallas import tpu_sc as plsc

F32 = jnp.float32
BF16 = jnp.bfloat16
I32 = jnp.int32

D_MODEL = 1024
HEAD_DIM = 64
N_HEADS = 16
N_KV_HEADS = 4
GROUP = N_HEADS // N_KV_HEADS
KV_WIDTH = N_KV_HEADS * HEAD_DIM
WINDOW = 128
N_GROUPS = 4
EXPERTS_PER_GROUP = 8
N_EXPERTS = N_GROUPS * EXPERTS_PER_GROUP
TOP_K = 2
D_FF = 512
EPS = 1e-6
LANES = 128

COL_WIDTH = {"b": D_MODEL, "c": D_MODEL, "u": D_MODEL, "q": D_MODEL, "k": KV_WIDTH, "v": KV_WIDTH,
             "ga": D_MODEL, "gb": D_MODEL}
COL_START = dict(zip(COL_WIDTH, (sum(list(COL_WIDTH.values())[:i]) for i in range(len(COL_WIDTH)))))
IN_COLS = sum(COL_WIDTH.values())
REST_COLS = 2 * D_MODEL + 2 * KV_WIDTH
COL_GB, COL_Q = 0, 1
COL_K, COL_V = 2 * D_MODEL // KV_WIDTH, 2 * D_MODEL // KV_WIDTH + 1

TM_PROJ = 512
SUB_PROJ = 256
TQ_ATTN = 2048
TM_MIX = 1024
SUB_MIX = 256
TM_CNT = 512
POS_TILES = 8
MOE_CHUNKS = 2
TM_EXP = 512
SUB_EXP = 256
TM_CMB = 2048
HALO_ROWS = 8
VMEM_LIMIT = 56 * 1024 * 1024
PACKED = D_MODEL // 2

SC_CORES = 2
SC_SUBCORES = 16
SC_WORKERS = SC_CORES * SC_SUBCORES
SC_CHUNK = 64


def _rms(x, g):
    r = lax.rsqrt(jnp.mean(x * x, axis=-1, keepdims=True) + EPS)
    return (x * r) * g


def _pack_bf16_pairs(x):
    n = x.shape[1] // 2
    lo = lax.bitcast_convert_type(x[:, :n].astype(BF16).astype(F32), I32)
    hi = lax.bitcast_convert_type(x[:, n:].astype(BF16).astype(F32), I32)
    return (hi & jnp.int32(-65536)) | lax.shift_right_logical(lo, 16)


def _unpack_bf16_pairs(p):
    lo = lax.bitcast_convert_type(lax.shift_left(p, 16), F32)
    hi = lax.bitcast_convert_type(p & jnp.int32(-65536), F32)
    return lo, hi


def _inproj_kernel(x_ref, g_ref, w_ref, cw_ref, cb_ref, wa_ref,
                   eg_ref, eu_ref, ed_ref,
                   za_ref, proj_ref, eg_out, eu_out, ed_out, halo_ref, *, tiles_per_seq):
    i = pl.program_id(0)
    eg_out[...] = eg_ref[...].astype(BF16)
    eu_out[...] = eu_ref[...].astype(BF16)
    ed_out[...] = ed_ref[...].astype(BF16)
    halves = [slice(s * SUB_PROJ, (s + 1) * SUB_PROJ) for s in range(TM_PROJ // SUB_PROJ)]
    hs = [_rms(x_ref[r, :], g_ref[...]).astype(BF16) for r in halves]
    col = lambda a, b=None: slice(COL_START[a], COL_START[b or a] + COL_WIDTH[b or a])
    proj = lambda a, b=None: [jnp.dot(h, w_ref[:, col(a, b)], preferred_element_type=F32) for h in hs]
    pcu = jnp.concatenate(proj("c", "u"), axis=0)
    pb = jnp.concatenate(proj("b"), axis=0)
    pga = jnp.concatenate(proj("ga"), axis=0)
    for r, pgb, pqkv in zip(halves, proj("gb"), proj("q", "v")):
        proj_ref[r, :D_MODEL] = pgb.astype(BF16)
        proj_ref[r, D_MODEL:] = pqkv.astype(BF16)
    cu = pcu[:, :D_MODEL] * pcu[:, D_MODEL:]
    first = (i % tiles_per_seq) == 0
    hist = jnp.where(first, 0.0, halo_ref[...])
    prev1 = hist[HALO_ROWS - 1:HALO_ROWS]
    prev2 = hist[HALO_ROWS - 2:HALO_ROWS - 1]
    halo_ref[...] = cu[TM_PROJ - HALO_ROWS:, :]
    row = lax.broadcasted_iota(I32, cu.shape, 0)
    cu1 = jnp.where(row == 0, prev1, pltpu.roll(cu, 1, 0))
    cu2 = jnp.where(row == 0, prev2, jnp.where(row == 1, prev1, pltpu.roll(cu, 2, 0)))
    cw = cw_ref[...]
    y = cw[0:1] * cu2 + cw[1:2] * cu1 + cw[2:3] * cu + cb_ref[...]
    ya = (pb * y).astype(BF16)
    z = jnp.dot(ya, wa_ref[...], preferred_element_type=F32)
    za_ref[...] = (jax.nn.sigmoid(pga) * z).astype(BF16)


def _inproj(xf, g, w_in, conv_w, conv_b, wa, seq, w_gate, w_up, w_down):
    t = xf.shape[0]
    n_tiles = t // TM_PROJ
    const = lambda shape: pl.BlockSpec(shape, lambda i: (0, 0), pipeline_mode=pl.Buffered(1))
    slabs = [w.reshape(-1, w.shape[-1]) for w in (w_gate, w_up, w_down)]
    slab_rows = [s.shape[0] // n_tiles for s in slabs]
    assert all(s.shape[0] == r * n_tiles and r % 16 == 0 for s, r in zip(slabs, slab_rows))
    slab_specs = [pl.BlockSpec((r, s.shape[1]), lambda i: (i, 0)) for s, r in zip(slabs, slab_rows)]
    outs = pl.pallas_call(
        functools.partial(_inproj_kernel, tiles_per_seq=seq // TM_PROJ),
        name="inproj_conv",
        grid=(n_tiles,),
        in_specs=[
            pl.BlockSpec((TM_PROJ, D_MODEL), lambda i: (i, 0)),
            const((1, D_MODEL)),
            const((D_MODEL, IN_COLS)),
            const((3, D_MODEL)), const((1, D_MODEL)), const((D_MODEL, D_MODEL)),
        ] + slab_specs,
        out_specs=[pl.BlockSpec((TM_PROJ, D_MODEL), lambda i: (i, 0)),
                   pl.BlockSpec((TM_PROJ, REST_COLS), lambda i: (i, 0))] + slab_specs,
        out_shape=[jax.ShapeDtypeStruct((t, D_MODEL), BF16),
                   jax.ShapeDtypeStruct((t, REST_COLS), BF16)]
                  + [jax.ShapeDtypeStruct(s.shape, BF16) for s in slabs],
        scratch_shapes=[pltpu.VMEM((HALO_ROWS, D_MODEL), F32)],
        compiler_params=pltpu.CompilerParams(
            dimension_semantics=("arbitrary",), vmem_limit_bytes=VMEM_LIMIT),
    )(xf, g, w_in, conv_w, conv_b, wa, *slabs)
    za, proj, eg, eu, ed = outs
    return za, proj, eg.reshape(w_gate.shape), eu.reshape(w_up.shape), ed.reshape(w_down.shape)


def _attn_kernel(sink_ref, q_ref, k_ref, v_ref, kp_ref, vp_ref, gb_ref, za_ref, wb_ref, o_ref,
                 *, first_tile_index, tiles_per_seq):
    first_tile = ((first_tile_index + pl.program_id(0)) % tiles_per_seq) == 0
    ks = lax.broadcasted_iota(I32, (WINDOW, WINDOW), 0)
    qq = lax.broadcasted_iota(I32, (WINDOW, WINDOW), 1)
    own = ks <= qq
    dist = jnp.where(own, qq - ks, qq - ks + WINDOW).astype(F32)
    visible0 = jnp.logical_or(own, jnp.logical_not(first_tile))
    log2e = math.log2(math.e)
    c_scale = log2e / math.sqrt(HEAD_DIM)
    nt = (((1,), (1,)), ((), ()))
    zk = jnp.zeros((2 * WINDOW, HEAD_DIM), BF16)

    def transposed(v_blk):
        return jnp.transpose(v_blk.astype(F32)).astype(BF16)

    def project(rows_p, attn_blk):
        yb = jnp.dot(attn_blk, wb_ref[...], preferred_element_type=F32)
        zb = jax.nn.sigmoid(gb_ref[rows_p, :].astype(F32)) * yb
        o_ref[rows_p, :] = (za_ref[rows_p, :].astype(F32) + zb).astype(BF16)

    pending = None
    prev_k = kp_ref[...]
    prev_vt = transposed(vp_ref[...])
    for sb in range(TQ_ATTN // WINDOW):
        rows = slice(sb * WINDOW, (sb + 1) * WINDOW)
        cur_k = k_ref[rows, :]
        cur_vt = transposed(v_ref[rows, :])
        scores, vcats = [], []
        for kh in range(N_KV_HEADS):
            cols = slice(kh * HEAD_DIM, (kh + 1) * HEAD_DIM)
            kcat = jnp.concatenate([prev_k[:, cols], cur_k[:, cols]], axis=0)
            vcats.append(jnp.concatenate([prev_vt[cols, :], cur_vt[cols, :]], axis=1))
            qg = jnp.concatenate([q_ref[rows, (2 * kh) * LANES:(2 * kh + 1) * LANES],
                                  q_ref[rows, (2 * kh + 1) * LANES:(2 * kh + 2) * LANES]], axis=0)
            k_pad = jnp.concatenate([jnp.concatenate([kcat, zk], axis=1),
                                     jnp.concatenate([zk, kcat], axis=1)], axis=0)
            scores.append(lax.dot_general(k_pad, qg, nt, preferred_element_type=F32))
        if pending is not None:
            project(*pending)
        probs, rdens = [], []
        for kh in range(N_KV_HEADS):
            for pos in range(2):
                pr, rd = [], []
                for half in range(2):
                    h = kh * GROUP + 2 * half + pos
                    slope = 2.0 ** (-8.0 * (h + 1) / N_HEADS)
                    qcols = slice(half * WINDOW, (half + 1) * WINDOW)
                    krow = pos * 2 * WINDOW
                    st = scores[kh]
                    s = (jnp.where(own, st[krow + WINDOW:krow + 2 * WINDOW, qcols],
                                   st[krow:krow + WINDOW, qcols]) * c_scale
                         - (slope * log2e) * dist)
                    if sb == 0:
                        s = jnp.where(visible0, s, -jnp.inf)
                    m = jnp.max(s, axis=0, keepdims=True)
                    p = jnp.exp2(s - m)
                    den = jnp.sum(p, axis=0, keepdims=True) + jnp.exp2(sink_ref[h] * log2e - m)
                    rd.append(1.0 / den)
                    pr.append(jnp.concatenate(
                        [jnp.where(own, 0.0, p).astype(BF16), jnp.where(own, p, 0.0).astype(BF16)],
                        axis=0))
                probs.append(jnp.concatenate(pr, axis=1))
                rdens.append(jnp.concatenate(rd, axis=1))
        out_t = [None] * N_HEADS
        for kh in range(N_KV_HEADS):
            for pos in range(2):
                o2 = jnp.dot(vcats[kh], probs[2 * kh + pos], preferred_element_type=F32)
                o2 = o2 * rdens[2 * kh + pos]
                out_t[kh * GROUP + pos] = o2[:, :WINDOW]
                out_t[kh * GROUP + 2 + pos] = o2[:, WINDOW:]
        pending = (rows, jnp.transpose(jnp.concatenate(out_t, axis=0)).astype(BF16))
        prev_k, prev_vt = cur_k, cur_vt
    project(*pending)


def _attention(proj, za, sinks, wb, seq, chunk, t):
    sub = TQ_ATTN // WINDOW
    n_tiles = t // TQ_ATTN
    first = chunk * n_tiles
    src = lambda w, c: pl.BlockSpec((TQ_ATTN, w), lambda i: (first + i, c))
    prev = lambda c: pl.BlockSpec((WINDOW, KV_WIDTH),
                                  lambda i: (jnp.maximum((first + i) * sub - 1, 0), c))
    return pl.pallas_call(
        functools.partial(_attn_kernel, first_tile_index=first, tiles_per_seq=seq // TQ_ATTN),
        name="swattn",
        grid=(n_tiles,),
        in_specs=[
            pl.BlockSpec(memory_space=pltpu.SMEM),
            src(D_MODEL, COL_Q), src(KV_WIDTH, COL_K), src(KV_WIDTH, COL_V), prev(COL_K), prev(COL_V),
            src(D_MODEL, COL_GB), src(D_MODEL, 0),
            pl.BlockSpec((D_MODEL, D_MODEL), lambda i: (0, 0), pipeline_mode=pl.Buffered(1)),
        ],
        out_specs=pl.BlockSpec((TQ_ATTN, D_MODEL), lambda i: (i, 0)),
        out_shape=jax.ShapeDtypeStruct((t, D_MODEL), BF16),
        compiler_params=pltpu.CompilerParams(
            dimension_semantics=("arbitrary",), vmem_limit_bytes=VMEM_LIMIT),
    )(sinks, proj, proj, proj, proj, proj, proj, za, wb)


def _mix_kernel(merged_ref, x_ref, wo_ref, g_ref, wr_ref, br_ref,
                xmid_ref, h_ref, route_ref, cnt_ref):
    subs = [slice(s * SUB_MIX, (s + 1) * SUB_MIX) for s in range(TM_MIX // SUB_MIX)]
    xm = [x_ref[r, :] + jnp.dot(merged_ref[r, :], wo_ref[...], preferred_element_type=F32)
          for r in subs]
    hs = []
    for r, v in zip(subs, xm):
        xmid_ref[r, :] = _pack_bf16_pairs(v)
        h = _rms(v, g_ref[...])
        h_ref[r, :] = _pack_bf16_pairs(h)
        hs.append(h)
    wr = wr_ref[...]
    logits = []
    for h in hs:
        h_hi = h.astype(BF16)
        h_lo = (h - h_hi.astype(F32)).astype(BF16)
        both = jnp.dot(h_hi, wr, preferred_element_type=F32)
        logits.append(both[:, :LANES] + both[:, LANES:]
                      + jnp.dot(h_lo, wr[:, :LANES], preferred_element_type=F32) + br_ref[...])
    subs_per_cnt = TM_CNT // SUB_MIX
    cnts = [jnp.zeros((8, LANES), F32) for _ in range(TM_MIX // TM_CNT)]
    for s, (r, lg) in enumerate(zip(subs, logits)):
        route, cnt = _route(lg)
        route_ref[r, :] = route
        cnts[s // subs_per_cnt] = cnts[s // subs_per_cnt] + cnt
    for c, cnt in enumerate(cnts):
        cnt_ref[c * 8:(c + 1) * 8, :] = cnt


def _route(logits):
    n = logits.shape[0]
    lt = jnp.transpose(logits)
    sub = lax.broadcasted_iota(I32, (EXPERTS_PER_GROUP, n), 0)
    neg = -jnp.inf
    gl = jnp.where(sub < N_GROUPS, lt[N_EXPERTS:N_EXPERTS + EXPERTS_PER_GROUP], neg)
    gmax = jnp.max(gl, axis=0, keepdims=True)
    g_idx = jnp.min(jnp.where(gl == gmax, sub, EXPERTS_PER_GROUP), axis=0, keepdims=True)
    p_g = 1.0 / jnp.sum(jnp.exp(gl - gmax), axis=0, keepdims=True)
    v1 = v2 = i1 = i2 = None
    for g in range(N_GROUPS):
        eg = lt[g * EXPERTS_PER_GROUP:(g + 1) * EXPERTS_PER_GROUP]
        a1 = jnp.max(eg, axis=0, keepdims=True)
        j1 = jnp.min(jnp.where(eg == a1, sub, EXPERTS_PER_GROUP), axis=0, keepdims=True)
        eg2 = jnp.where(sub == j1, neg, eg)
        a2 = jnp.max(eg2, axis=0, keepdims=True)
        j2 = jnp.min(jnp.where(eg2 == a2, sub, EXPERTS_PER_GROUP), axis=0, keepdims=True)
        if g == 0:
            v1, v2, i1, i2 = a1, a2, j1, j2
        else:
            chosen = g_idx == g
            v1, v2 = jnp.where(chosen, a1, v1), jnp.where(chosen, a2, v2)
            i1, i2 = jnp.where(chosen, j1, i1), jnp.where(chosen, j2, i2)
    e21 = jnp.exp(v2 - v1)
    w1 = p_g / (1.0 + e21)
    w2 = p_g * e21 / (1.0 + e21)
    e1 = g_idx * EXPERTS_PER_GROUP + i1
    e2 = g_idx * EXPERTS_PER_GROUP + i2
    rows8 = jnp.where(sub == 0, e1.astype(F32),
                      jnp.where(sub == 1, e2.astype(F32),
                                jnp.where(sub == 2, w1, jnp.where(sub == 3, w2, 0.0))))
    route_t = jnp.concatenate([rows8, jnp.zeros((LANES - EXPERTS_PER_GROUP, n), F32)], axis=0)
    expert_row = lax.broadcasted_iota(I32, (LANES, n), 0)
    onehot_t = ((expert_row == e1) | (expert_row == e2)).astype(BF16)
    cnt = lax.dot_general(jnp.ones((8, n), BF16), onehot_t, (((1,), (1,)), ((), ())),
                          preferred_element_type=F32)
    return jnp.transpose(route_t), cnt


def _mix(merged, xf, wo, g, wr, br, chunk, t):
    n_tiles = t // TM_MIX
    first = chunk * n_tiles
    cnt_rows = TM_MIX // TM_CNT * 8
    full = lambda shape: pl.BlockSpec(shape, lambda i: (0, 0), pipeline_mode=pl.Buffered(1))
    tile = lambda w=D_MODEL: pl.BlockSpec((TM_MIX, w), lambda i: (i, 0))
    return pl.pallas_call(
        _mix_kernel,
        name="merge_router",
        grid=(n_tiles,),
        in_specs=[
            tile(), pl.BlockSpec((TM_MIX, D_MODEL), lambda i: (first + i, 0)),
            full((D_MODEL, D_MODEL)), full((1, D_MODEL)),
            full((D_MODEL, 2 * LANES)), full((1, LANES)),
        ],
        out_specs=[tile(PACKED), tile(PACKED), tile(LANES),
                   pl.BlockSpec((cnt_rows, LANES), lambda i: (i, 0))],
        out_shape=[
            jax.ShapeDtypeStruct((t, PACKED), I32),
            jax.ShapeDtypeStruct((t, PACKED), I32),
            jax.ShapeDtypeStruct((t, LANES), F32),
            jax.ShapeDtypeStruct((n_tiles * cnt_rows, LANES), F32),
        ],
        compiler_params=pltpu.CompilerParams(
            dimension_semantics=("arbitrary",), vmem_limit_bytes=VMEM_LIMIT),
    )(merged, xf, wo, g, wr, br)


def _pos_kernel(route_ref, base_ref, pos_ref):
    lane = lax.broadcasted_iota(I32, (TM_CNT, LANES), 1)
    r = lax.broadcasted_iota(I32, (TM_CNT, TM_CNT), 0)
    c = lax.broadcasted_iota(I32, (TM_CNT, TM_CNT), 1)
    lower = (c < r).astype(BF16)
    subs = [slice(s * TM_CNT, (s + 1) * TM_CNT) for s in range(POS_TILES)]
    routes = [route_ref[rs, :] for rs in subs]
    e1 = [jnp.sum(jnp.where(lane == 0, rt, 0.0), axis=-1, keepdims=True).astype(I32) for rt in routes]
    e2 = [jnp.sum(jnp.where(lane == 1, rt, 0.0), axis=-1, keepdims=True).astype(I32) for rt in routes]
    onehot = [((lane == a) | (lane == b)).astype(BF16) for a, b in zip(e1, e2)]
    before = [jnp.dot(lower, oh, preferred_element_type=F32) + base_ref[s]
              for s, oh in enumerate(onehot)]
    for s, rs in enumerate(subs):
        p1 = jnp.sum(jnp.where(lane == e1[s], before[s], 0.0), axis=-1, keepdims=True)
        p2 = jnp.sum(jnp.where(lane == e2[s], before[s], 0.0), axis=-1, keepdims=True)
        packed = jnp.where(lane == 0, p1, jnp.where(lane == 1, p2, 0.0))
        pos_ref[:, rs] = jnp.transpose(packed)[0:TOP_K, :].astype(I32)


def _positions(route, base):
    t = route.shape[0]
    n_steps = t // (TM_CNT * POS_TILES)
    return pl.pallas_call(
        _pos_kernel,
        name="positions",
        grid=(n_steps,),
        in_specs=[
            pl.BlockSpec((TM_CNT * POS_TILES, LANES), lambda i: (i, 0)),
            pl.BlockSpec((POS_TILES, 1, LANES), lambda i: (i, 0, 0)),
        ],
        out_specs=pl.BlockSpec((TOP_K, TM_CNT * POS_TILES), lambda i: (0, i)),
        out_shape=jax.ShapeDtypeStruct((TOP_K, t), I32),
        compiler_params=pltpu.CompilerParams(dimension_semantics=("arbitrary",)),
    )(route, base)


def _sc_mesh():
    return plsc.VectorSubcoreMesh(core_axis_name="c", subcore_axis_name="s",
                                  num_cores=SC_CORES, num_subcores=SC_SUBCORES)


def _sc_worker():
    return lax.axis_index("s") * SC_CORES + lax.axis_index("c")


def _dispatch(pos, hp, rows):
    t = hp.shape[0]
    per_w = t // SC_WORKERS
    n_ch = per_w // SC_CHUNK
    pos4 = pos.reshape(TOP_K, SC_WORKERS, n_ch, SC_CHUNK)

    @functools.partial(
        pl.kernel, mesh=_sc_mesh(),
        out_type=jax.ShapeDtypeStruct((rows, PACKED), I32),
        scratch_types=[pltpu.VMEM((TOP_K, n_ch, SC_CHUNK), I32),
                       pltpu.VMEM((SC_CHUNK, PACKED), I32)])
    def scatter(hp_hbm, pos_hbm, xs_hbm, idx_v, rows_v):
        wid = _sc_worker()
        for k in range(TOP_K):
            pltpu.sync_copy(pos_hbm.at[k, wid], idx_v.at[k])

        def body(c, carry):
            start = pl.multiple_of(wid * per_w + c * SC_CHUNK, SC_CHUNK)
            pltpu.sync_copy(hp_hbm.at[pl.ds(start, SC_CHUNK)], rows_v)
            for k in range(TOP_K):
                pltpu.sync_copy(rows_v, xs_hbm.at[idx_v.at[k, c]])
            return carry

        lax.fori_loop(0, n_ch, body, 0)

    return scatter(hp, pos4)


def _gather_rows(table, idx):
    n = idx.shape[0]
    per_w = n // SC_WORKERS
    n_ch = per_w // SC_CHUNK
    idx3 = idx.reshape(SC_WORKERS, n_ch, SC_CHUNK)

    @functools.partial(
        pl.kernel, mesh=_sc_mesh(),
        out_type=jax.ShapeDtypeStruct((n, PACKED), I32),
        scratch_types=[pltpu.VMEM((n_ch, SC_CHUNK), I32),
                       pltpu.VMEM((SC_CHUNK, PACKED), I32)])
    def gather(table_hbm, idx_hbm, out_hbm, idx_v, rows_v):
        wid = _sc_worker()
        pltpu.sync_copy(idx_hbm.at[wid], idx_v)

        def body(c, carry):
            start = pl.multiple_of(wid * per_w + c * SC_CHUNK, SC_CHUNK)
            pltpu.sync_copy(table_hbm.at[idx_v.at[c]], rows_v)
            pltpu.sync_copy(rows_v, out_hbm.at[pl.ds(start, SC_CHUNK)])
            return carry

        lax.fori_loop(0, n_ch, body, 0)

    return gather(table, idx3)


def _expert_kernel(te_ref, nx_ref, sl_ref, ts_ref, tv_ref, xs_ref, wg_hbm, wu_hbm, wd_hbm, o_ref,
                   wg_s, wu_s, wd_s, sem):
    del ts_ref
    i = pl.program_id(0)
    n_valid = tv_ref[i]
    expert = te_ref[i]
    slot = sl_ref[i]

    def weight_copies(e, s):
        return (pltpu.make_async_copy(wg_hbm.at[e], wg_s.at[s], sem.at[s, 0]),
                pltpu.make_async_copy(wu_hbm.at[e], wu_s.at[s], sem.at[s, 1]),
                pltpu.make_async_copy(wd_hbm.at[e], wd_s.at[s], sem.at[s, 2]))

    @pl.when(i == 0)
    def _():
        for copy in weight_copies(expert, slot):
            copy.start()

    @pl.when(jnp.logical_or(i == 0, expert != te_ref[jnp.maximum(i - 1, 0)]))
    def _():
        for copy in weight_copies(expert, slot):
            copy.wait()
        nxt = nx_ref[i]

        @pl.when(nxt >= 0)
        def _():
            for copy in weight_copies(nxt, 1 - slot):
                copy.start()

    def mlp(n_sub):
        wg, wu, wd = wg_s.at[slot], wu_s.at[slot], wd_s.at[slot]
        subs = [slice(s * SUB_EXP, (s + 1) * SUB_EXP) for s in range(n_sub)]
        xin = []
        for r in subs:
            rid = r.start + lax.broadcasted_iota(I32, (SUB_EXP, PACKED), 0)
            lo, hi = _unpack_bf16_pairs(jnp.where(rid < n_valid, xs_ref[r, :], 0))
            xin.append((lo.astype(BF16), hi.astype(BF16)))
        ab = [(jnp.dot(lo, wg[:PACKED, :], preferred_element_type=F32)
               + jnp.dot(hi, wg[PACKED:, :], preferred_element_type=F32),
               jnp.dot(lo, wu[:PACKED, :], preferred_element_type=F32)
               + jnp.dot(hi, wu[PACKED:, :], preferred_element_type=F32)) for lo, hi in xin]
        for r, (ai, bi) in zip(subs, ab):
            hm = (ai * jax.nn.sigmoid(ai) * bi).astype(BF16)
            o_ref[r, :] = _pack_bf16_pairs(jnp.dot(hm, wd[...], preferred_element_type=F32))
        if n_sub * SUB_EXP < TM_EXP:
            o_ref[n_sub * SUB_EXP:, :] = jnp.zeros((TM_EXP - n_sub * SUB_EXP, PACKED), I32)

    n_subs = TM_EXP // SUB_EXP
    for n_sub in range(n_subs + 1):
        lo_rows = (n_sub - 1) * SUB_EXP if n_sub else -1
        in_range = jnp.logical_and(n_valid > lo_rows, n_valid <= n_sub * SUB_EXP)
        pl.when(in_range)(functools.partial(mlp, n_sub))


def _experts(tile_expert, next_expert, tile_slot, tile_src, tile_valid, xs, wg, wu, wd):
    rows = xs.shape[0]
    hbm = pl.BlockSpec(memory_space=pl.ANY)
    grid_spec = pltpu.PrefetchScalarGridSpec(
        num_scalar_prefetch=5,
        grid=(rows // TM_EXP,),
        in_specs=[pl.BlockSpec((TM_EXP, PACKED), lambda i, te, nx, sl, ts, tv: (ts[i], 0)),
                  hbm, hbm, hbm],
        out_specs=pl.BlockSpec((TM_EXP, PACKED), lambda i, te, nx, sl, ts, tv: (i, 0)),
        scratch_shapes=[
            pltpu.VMEM((2, D_MODEL, D_FF), BF16), pltpu.VMEM((2, D_MODEL, D_FF), BF16),
            pltpu.VMEM((2, D_FF, D_MODEL), BF16),
            pltpu.SemaphoreType.DMA((2, 3)),
        ],
    )
    return pl.pallas_call(
        _expert_kernel,
        name="experts",
        grid_spec=grid_spec,
        out_shape=jax.ShapeDtypeStruct((rows, PACKED), I32),
        compiler_params=pltpu.CompilerParams(
            dimension_semantics=("arbitrary",), vmem_limit_bytes=VMEM_LIMIT),
    )(tile_expert, next_expert, tile_slot, tile_src, tile_valid, xs, wg, wu, wd)


def _combine_kernel(y1_ref, y2_ref, route_ref, xmid_ref, g_ref, *rest):
    o_ref = rest[-1]
    route = route_ref[...]
    lane = lax.broadcasted_iota(I32, route.shape, 1)
    w1 = jnp.sum(jnp.where(lane == 2, route, 0.0), axis=-1, keepdims=True)
    w2 = jnp.sum(jnp.where(lane == 3, route, 0.0), axis=-1, keepdims=True)
    lo1, hi1 = _unpack_bf16_pairs(y1_ref[...])
    lo2, hi2 = _unpack_bf16_pairs(y2_ref[...])
    lox, hix = _unpack_bf16_pairs(xmid_ref[...])
    x_out = jnp.concatenate([lox + (lo1 * w1 + lo2 * w2), hix + (hi1 * w1 + hi2 * w2)], axis=1)
    o_ref[...] = _rms(x_out, g_ref[...])


def _combine(yg, route, xmid, g, chunk, t_total, out_prev):
    t = xmid.shape[0]
    n_tiles = t // TM_CMB
    first = chunk * n_tiles
    in_specs = [
        pl.BlockSpec((TM_CMB, PACKED), lambda i: (i, 0)),
        pl.BlockSpec((TM_CMB, PACKED), lambda i: (n_tiles + i, 0)),
        pl.BlockSpec((TM_CMB, LANES), lambda i: (i, 0)),
        pl.BlockSpec((TM_CMB, PACKED), lambda i: (i, 0)),
        pl.BlockSpec((1, D_MODEL), lambda i: (0, 0)),
    ]
    args = [yg, yg, route, xmid, g]
    aliases = {}
    if out_prev is not None:
        in_specs.append(pl.BlockSpec(memory_space=pl.ANY))
        aliases = {len(args): 0}
        args.append(out_prev)
    return pl.pallas_call(
        _combine_kernel,
        name="combine",
        grid=(n_tiles,),
        in_specs=in_specs,
        out_specs=pl.BlockSpec((TM_CMB, D_MODEL), lambda i: (first + i, 0)),
        out_shape=jax.ShapeDtypeStruct((t_total, D_MODEL), F32),
        input_output_aliases=aliases,
        compiler_params=pltpu.CompilerParams(
            dimension_semantics=("arbitrary",), vmem_limit_bytes=VMEM_LIMIT),
    )(*args)


def _split_bf16(w):
    hi = w.astype(BF16)
    lo = (w - hi.astype(F32)).astype(BF16)
    return hi, lo


def kernel(x, norm_mix, w_in, conv_w, conv_b, w_a_out, sinks, w_b_out, w_o, norm_ffn, w_group,
           b_group, w_expert, b_expert, w_gate, w_up, w_down, norm_final):
    bsz, seq, d = x.shape
    t = bsz * seq
    assert d == D_MODEL and seq % TM_PROJ == 0 and seq % TQ_ATTN == 0
    xf = x.reshape(t, d)
    row = lambda v: v.reshape(1, -1)

    assert w_in.shape == (D_MODEL, IN_COLS)
    za, proj, w_gate, w_up, w_down = _inproj(xf, row(norm_mix), w_in.astype(BF16), conv_w,
                                             row(conv_b), w_a_out.astype(BF16), seq,
                                             w_gate, w_up, w_down)
    pad = LANES - N_GROUPS - N_EXPERTS
    w_r = jnp.concatenate([w_expert, w_group, jnp.zeros((d, pad), F32)], axis=1)
    b_r = jnp.concatenate([b_expert, b_group, jnp.zeros((pad,), F32)]).reshape(1, LANES)
    wr = jnp.concatenate(_split_bf16(w_r), axis=1)
    wb = w_b_out.astype(BF16)
    wo = w_o.astype(BF16)

    t_chunk = t // MOE_CHUNKS
    assert t == t_chunk * MOE_CHUNKS and all(
        t_chunk % step == 0
        for step in (TQ_ATTN, TM_MIX, TM_CNT * POS_TILES, TM_CMB, SC_WORKERS * SC_CHUNK))
    out = None
    for chunk in range(MOE_CHUNKS):
        merged = _attention(proj, za, sinks, wb, seq, chunk, t_chunk)
        xmid, h2, route, cnt = _mix(merged, xf, wo, row(norm_ffn), wr, b_r, chunk, t_chunk)
        out = _moe_chunk(xmid, h2, route, cnt, w_gate, w_up, w_down, row(norm_final), chunk, t, out)
    return out.reshape(bsz, seq, d)


def _moe_chunk(xmid, h2, route, cnt, w_gate, w_up, w_down, g_final, chunk, t_total, out_prev):
    t = xmid.shape[0]
    n_tiles = t // TM_CNT
    cnt = cnt.reshape(n_tiles, 8, LANES)[:, 0, :N_EXPERTS].astype(I32)
    totals = jnp.sum(cnt, axis=0)
    tiles_e = (totals + TM_EXP - 1) // TM_EXP
    tile_end = jnp.cumsum(tiles_e)
    offset = (tile_end - tiles_e) * TM_EXP
    base = offset[None, :] + jnp.cumsum(cnt, axis=0) - cnt
    base = jnp.pad(base, ((0, 0), (0, LANES - N_EXPERTS))).astype(F32).reshape(n_tiles, 1, LANES)
    rows = t * TOP_K + N_EXPERTS * TM_EXP
    n_active = tile_end[-1]
    tile_id = jnp.arange(rows // TM_EXP, dtype=I32)
    tile_src = jnp.minimum(tile_id, n_active - 1)
    tile_expert = jnp.sum((tile_src[:, None] >= tile_end[None, :]).astype(I32), axis=1)
    tile_expert = jnp.minimum(tile_expert, N_EXPERTS - 1)
    row_in_expert = (tile_id - (tile_end - tiles_e)[tile_expert]) * TM_EXP
    tile_valid = jnp.clip(totals[tile_expert] - row_in_expert, 0, TM_EXP)
    tile_valid = jnp.where(tile_id < n_active, tile_valid, 0).astype(I32)
    after = tile_end[tile_expert]
    next_expert = jnp.where(after < n_active, tile_expert[jnp.minimum(after, n_active - 1)], -1)
    first_of_expert = jnp.concatenate(
        [jnp.ones((1,), I32), (tile_expert[1:] != tile_expert[:-1]).astype(I32)])
    tile_slot = (jnp.cumsum(first_of_expert) - 1) % 2

    pos = _positions(route, base)
    xs = _dispatch(pos, h2, rows)
    ys = _experts(tile_expert.astype(I32), next_expert.astype(I32), tile_slot.astype(I32),
                  tile_src.astype(I32), tile_valid, xs, w_gate, w_up, w_down)
    yg = _gather_rows(ys, pos.reshape(TOP_K * t))
    return _combine(yg, route, xmid, g_final, chunk, t_total, out_prev)
```

```python
import functools
import math

import jax
import jax.numpy as jnp
from jax import lax
from jax.experimental import pallas as pl
from jax.experimental.pallas import tpu as pltpu
from jax.experimental.pallas import tpu_sc as plsc

F32 = jnp.float32
BF16 = jnp.bfloat16
I32 = jnp.int32

D_MODEL = 1024
HEAD_DIM = 64
N_HEADS = 16
N_KV_HEADS = 4
GROUP = N_HEADS // N_KV_HEADS
KV_WIDTH = N_KV_HEADS * HEAD_DIM
WINDOW = 128
N_GROUPS = 4
EXPERTS_PER_GROUP = 8
N_EXPERTS = N_GROUPS * EXPERTS_PER_GROUP
TOP_K = 2
D_FF = 512
EPS = 1e-6
LANES = 128

COL_WIDTH = {"b": D_MODEL, "c": D_MODEL, "u": D_MODEL, "q": D_MODEL, "k": KV_WIDTH, "v": KV_WIDTH,
             "ga": D_MODEL, "gb": D_MODEL}
COL_START = dict(zip(COL_WIDTH, (sum(list(COL_WIDTH.values())[:i]) for i in range(len(COL_WIDTH)))))
IN_COLS = sum(COL_WIDTH.values())
REST_COLS = 2 * D_MODEL + 2 * KV_WIDTH
COL_GB, COL_Q = 0, 1
COL_K, COL_V = 2 * D_MODEL // KV_WIDTH, 2 * D_MODEL // KV_WIDTH + 1

TM_PROJ = 512
SUB_PROJ = 256
TQ_ATTN = 1024
TM_MIX = 1024
SUB_MIX = 256
TM_CNT = 512
POS_TILES = 8
MOE_CHUNKS = 2
TM_EXP = 512
SUB_EXP = 256
TM_CMB = 2048
HALO_ROWS = 8
VMEM_LIMIT = 56 * 1024 * 1024
PACKED = D_MODEL // 2

SC_CORES = 2
SC_SUBCORES = 16
SC_WORKERS = SC_CORES * SC_SUBCORES
SC_CHUNK = 64


def _rms(x, g):
    r = lax.rsqrt(jnp.mean(x * x, axis=-1, keepdims=True) + EPS)
    return (x * r) * g


def _pack_bf16_pairs(x):
    n = x.shape[1] // 2
    lo = lax.bitcast_convert_type(x[:, :n].astype(BF16).astype(F32), I32)
    hi = lax.bitcast_convert_type(x[:, n:].astype(BF16).astype(F32), I32)
    return (hi & jnp.int32(-65536)) | lax.shift_right_logical(lo, 16)


def _unpack_bf16_pairs(p):
    lo = lax.bitcast_convert_type(lax.shift_left(p, 16), F32)
    hi = lax.bitcast_convert_type(p & jnp.int32(-65536), F32)
    return lo, hi


def _inproj_kernel(x_ref, g_ref, w_ref, cw_ref, cb_ref, wa_ref,
                   eg_ref, eu_ref, ed_ref,
                   za_ref, proj_ref, eg_out, eu_out, ed_out, halo_ref, *, tiles_per_seq):
    i = pl.program_id(0)
    eg_out[...] = eg_ref[...].astype(BF16)
    eu_out[...] = eu_ref[...].astype(BF16)
    ed_out[...] = ed_ref[...].astype(BF16)
    halves = [slice(s * SUB_PROJ, (s + 1) * SUB_PROJ) for s in range(TM_PROJ // SUB_PROJ)]
    hs = [_rms(x_ref[r, :], g_ref[...]).astype(BF16) for r in halves]
    col = lambda a, b=None: slice(COL_START[a], COL_START[b or a] + COL_WIDTH[b or a])
    proj = lambda a, b=None: [jnp.dot(h, w_ref[:, col(a, b)], preferred_element_type=F32) for h in hs]
    pcu = jnp.concatenate(proj("c", "u"), axis=0)
    pb = jnp.concatenate(proj("b"), axis=0)
    pga = jnp.concatenate(proj("ga"), axis=0)
    for r, pgb, pqkv in zip(halves, proj("gb"), proj("q", "v")):
        proj_ref[r, :D_MODEL] = pgb.astype(BF16)
        proj_ref[r, D_MODEL:] = pqkv.astype(BF16)
    cu = pcu[:, :D_MODEL] * pcu[:, D_MODEL:]
    first = (i % tiles_per_seq) == 0
    hist = jnp.where(first, 0.0, halo_ref[...])
    prev1 = hist[HALO_ROWS - 1:HALO_ROWS]
    prev2 = hist[HALO_ROWS - 2:HALO_ROWS - 1]
    halo_ref[...] = cu[TM_PROJ - HALO_ROWS:, :]
    row = lax.broadcasted_iota(I32, cu.shape, 0)
    cu1 = jnp.where(row == 0, prev1, pltpu.roll(cu, 1, 0))
    cu2 = jnp.where(row == 0, prev2, jnp.where(row == 1, prev1, pltpu.roll(cu, 2, 0)))
    cw = cw_ref[...]
    y = cw[0:1] * cu2 + cw[1:2] * cu1 + cw[2:3] * cu + cb_ref[...]
    ya = (pb * y).astype(BF16)
    z = jnp.dot(ya, wa_ref[...], preferred_element_type=F32)
    za_ref[...] = (jax.nn.sigmoid(pga) * z).astype(BF16)


def _inproj(xf, g, w_in, conv_w, conv_b, wa, seq, w_gate, w_up, w_down):
    t = xf.shape[0]
    n_tiles = t // TM_PROJ
    const = lambda shape: pl.BlockSpec(shape, lambda i: (0, 0), pipeline_mode=pl.Buffered(1))
    slabs = [w.reshape(-1, w.shape[-1]) for w in (w_gate, w_up, w_down)]
    slab_rows = [s.shape[0] // n_tiles for s in slabs]
    assert all(s.shape[0] == r * n_tiles and r % 16 == 0 for s, r in zip(slabs, slab_rows))
    slab_specs = [pl.BlockSpec((r, s.shape[1]), lambda i: (i, 0)) for s, r in zip(slabs, slab_rows)]
    outs = pl.pallas_call(
        functools.partial(_inproj_kernel, tiles_per_seq=seq // TM_PROJ),
        name="inproj_conv",
        grid=(n_tiles,),
        in_specs=[
            pl.BlockSpec((TM_PROJ, D_MODEL), lambda i: (i, 0)),
            const((1, D_MODEL)),
            const((D_MODEL, IN_COLS)),
            const((3, D_MODEL)), const((1, D_MODEL)), const((D_MODEL, D_MODEL)),
        ] + slab_specs,
        out_specs=[pl.BlockSpec((TM_PROJ, D_MODEL), lambda i: (i, 0)),
                   pl.BlockSpec((TM_PROJ, REST_COLS), lambda i: (i, 0))] + slab_specs,
        out_shape=[jax.ShapeDtypeStruct((t, D_MODEL), BF16),
                   jax.ShapeDtypeStruct((t, REST_COLS), BF16)]
                  + [jax.ShapeDtypeStruct(s.shape, BF16) for s in slabs],
        scratch_shapes=[pltpu.VMEM((HALO_ROWS, D_MODEL), F32)],
        compiler_params=pltpu.CompilerParams(
            dimension_semantics=("arbitrary",), vmem_limit_bytes=VMEM_LIMIT),
    )(xf, g, w_in, conv_w, conv_b, wa, *slabs)
    za, proj, eg, eu, ed = outs
    return za, proj, eg.reshape(w_gate.shape), eu.reshape(w_up.shape), ed.reshape(w_down.shape)


def _attn_kernel(sink_ref, q_ref, k_ref, v_ref, kp_ref, vp_ref, gb_ref, za_ref, wb_ref, o_ref,
                 *, first_tile_index, tiles_per_seq):
    first_tile = ((first_tile_index + pl.program_id(0)) % tiles_per_seq) == 0
    ks = lax.broadcasted_iota(I32, (WINDOW, WINDOW), 0)
    qq = lax.broadcasted_iota(I32, (WINDOW, WINDOW), 1)
    own = ks <= qq
    dist = jnp.where(own, qq - ks, qq - ks + WINDOW).astype(F32)
    visible0 = jnp.logical_or(own, jnp.logical_not(first_tile))
    log2e = math.log2(math.e)
    c_scale = log2e / math.sqrt(HEAD_DIM)
    nt = (((1,), (1,)), ((), ()))
    zk = jnp.zeros((2 * WINDOW, HEAD_DIM), BF16)

    def transposed(v_blk):
        return jnp.transpose(v_blk.astype(F32)).astype(BF16)

    def project(rows_p, attn_blk):
        yb = jnp.dot(attn_blk, wb_ref[...], preferred_element_type=F32)
        zb = jax.nn.sigmoid(gb_ref[rows_p, :].astype(F32)) * yb
        o_ref[rows_p, :] = (za_ref[rows_p, :].astype(F32) + zb).astype(BF16)

    pending = None
    prev_k = kp_ref[...]
    prev_vt = transposed(vp_ref[...])
    for sb in range(TQ_ATTN // WINDOW):
        rows = slice(sb * WINDOW, (sb + 1) * WINDOW)
        cur_k = k_ref[rows, :]
        cur_vt = transposed(v_ref[rows, :])
        scores, vcats = [], []
        for kh in range(N_KV_HEADS):
            cols = slice(kh * HEAD_DIM, (kh + 1) * HEAD_DIM)
            kcat = jnp.concatenate([prev_k[:, cols], cur_k[:, cols]], axis=0)
            vcats.append(jnp.concatenate([prev_vt[cols, :], cur_vt[cols, :]], axis=1))
            qg = jnp.concatenate([q_ref[rows, (2 * kh) * LANES:(2 * kh + 1) * LANES],
                                  q_ref[rows, (2 * kh + 1) * LANES:(2 * kh + 2) * LANES]], axis=0)
            k_pad = jnp.concatenate([jnp.concatenate([kcat, zk], axis=1),
                                     jnp.concatenate([zk, kcat], axis=1)], axis=0)
            scores.append(lax.dot_general(k_pad, qg, nt, preferred_element_type=F32))
        if pending is not None:
            project(*pending)
        probs, rdens = [], []
        for kh in range(N_KV_HEADS):
            for pos in range(2):
                pr, rd = [], []
                for half in range(2):
                    h = kh * GROUP + 2 * half + pos
                    slope = 2.0 ** (-8.0 * (h + 1) / N_HEADS)
                    qcols = slice(half * WINDOW, (half + 1) * WINDOW)
                    krow = pos * 2 * WINDOW
                    st = scores[kh]
                    s = (jnp.where(own, st[krow + WINDOW:krow + 2 * WINDOW, qcols],
                                   st[krow:krow + WINDOW, qcols]) * c_scale
                         - (slope * log2e) * dist)
                    if sb == 0:
                        s = jnp.where(visible0, s, -jnp.inf)
                    m = jnp.max(s, axis=0, keepdims=True)
                    p = jnp.exp2(s - m)
                    den = jnp.sum(p, axis=0, keepdims=True) + jnp.exp2(sink_ref[h] * log2e - m)
                    rd.append(1.0 / den)
                    pr.append(jnp.concatenate(
                        [jnp.where(own, 0.0, p).astype(BF16), jnp.where(own, p, 0.0).astype(BF16)],
                        axis=0))
                probs.append(jnp.concatenate(pr, axis=1))
                rdens.append(jnp.concatenate(rd, axis=1))
        out_t = [None] * N_HEADS
        for kh in range(N_KV_HEADS):
            for pos in range(2):
                o2 = jnp.dot(vcats[kh], probs[2 * kh + pos], preferred_element_type=F32)
                o2 = o2 * rdens[2 * kh + pos]
                out_t[kh * GROUP + pos] = o2[:, :WINDOW]
                out_t[kh * GROUP + 2 + pos] = o2[:, WINDOW:]
        pending = (rows, jnp.transpose(jnp.concatenate(out_t, axis=0)).astype(BF16))
        prev_k, prev_vt = cur_k, cur_vt
    project(*pending)


def _attention(proj, za, sinks, wb, seq, chunk, t):
    sub = TQ_ATTN // WINDOW
    n_tiles = t // TQ_ATTN
    first = chunk * n_tiles
    src = lambda w, c: pl.BlockSpec((TQ_ATTN, w), lambda i: (first + i, c))
    prev = lambda c: pl.BlockSpec((WINDOW, KV_WIDTH),
                                  lambda i: (jnp.maximum((first + i) * sub - 1, 0), c))
    return pl.pallas_call(
        functools.partial(_attn_kernel, first_tile_index=first, tiles_per_seq=seq // TQ_ATTN),
        name="swattn",
        grid=(n_tiles,),
        in_specs=[
            pl.BlockSpec(memory_space=pltpu.SMEM),
            src(D_MODEL, COL_Q), src(KV_WIDTH, COL_K), src(KV_WIDTH, COL_V), prev(COL_K), prev(COL_V),
            src(D_MODEL, COL_GB), src(D_MODEL, 0),
            pl.BlockSpec((D_MODEL, D_MODEL), lambda i: (0, 0), pipeline_mode=pl.Buffered(1)),
        ],
        out_specs=pl.BlockSpec((TQ_ATTN, D_MODEL), lambda i: (i, 0)),
        out_shape=jax.ShapeDtypeStruct((t, D_MODEL), BF16),
        compiler_params=pltpu.CompilerParams(
            dimension_semantics=("arbitrary",), vmem_limit_bytes=VMEM_LIMIT),
    )(sinks, proj, proj, proj, proj, proj, proj, za, wb)


def _mix_kernel(merged_ref, x_ref, wo_ref, g_ref, wr_ref, br_ref,
                xmid_ref, h_ref, route_ref, cnt_ref):
    subs = [slice(s * SUB_MIX, (s + 1) * SUB_MIX) for s in range(TM_MIX // SUB_MIX)]
    xm = [x_ref[r, :] + jnp.dot(merged_ref[r, :], wo_ref[...], preferred_element_type=F32)
          for r in subs]
    hs = []
    for r, v in zip(subs, xm):
        xmid_ref[r, :] = _pack_bf16_pairs(v)
        h = _rms(v, g_ref[...])
        h_ref[r, :] = _pack_bf16_pairs(h)
        hs.append(h)
    wr = wr_ref[...]
    logits = []
    for h in hs:
        h_hi = h.astype(BF16)
        h_lo = (h - h_hi.astype(F32)).astype(BF16)
        both = jnp.dot(h_hi, wr, preferred_element_type=F32)
        logits.append(both[:, :LANES] + both[:, LANES:]
                      + jnp.dot(h_lo, wr[:, :LANES], preferred_element_type=F32) + br_ref[...])
    subs_per_cnt = TM_CNT // SUB_MIX
    cnts = [jnp.zeros((8, LANES), F32) for _ in range(TM_MIX // TM_CNT)]
    for s, (r, lg) in enumerate(zip(subs, logits)):
        route, cnt = _route(lg)
        route_ref[r, :] = route
        cnts[s // subs_per_cnt] = cnts[s // subs_per_cnt] + cnt
    for c, cnt in enumerate(cnts):
        cnt_ref[c * 8:(c + 1) * 8, :] = cnt


def _route(logits):
    n = logits.shape[0]
    lt = jnp.transpose(logits)
    sub = lax.broadcasted_iota(I32, (EXPERTS_PER_GROUP, n), 0)
    neg = -jnp.inf
    gl = jnp.where(sub < N_GROUPS, lt[N_EXPERTS:N_EXPERTS + EXPERTS_PER_GROUP], neg)
    gmax = jnp.max(gl, axis=0, keepdims=True)
    g_idx = jnp.min(jnp.where(gl == gmax, sub, EXPERTS_PER_GROUP), axis=0, keepdims=True)
    p_g = 1.0 / jnp.sum(jnp.exp(gl - gmax), axis=0, keepdims=True)
    v1 = v2 = i1 = i2 = None
    for g in range(N_GROUPS):
        eg = lt[g * EXPERTS_PER_GROUP:(g + 1) * EXPERTS_PER_GROUP]
        a1 = jnp.max(eg, axis=0, keepdims=True)
        j1 = jnp.min(jnp.where(eg == a1, sub, EXPERTS_PER_GROUP), axis=0, keepdims=True)
        eg2 = jnp.where(sub == j1, neg, eg)
        a2 = jnp.max(eg2, axis=0, keepdims=True)
        j2 = jnp.min(jnp.where(eg2 == a2, sub, EXPERTS_PER_GROUP), axis=0, keepdims=True)
        if g == 0:
            v1, v2, i1, i2 = a1, a2, j1, j2
        else:
            chosen = g_idx == g
            v1, v2 = jnp.where(chosen, a1, v1), jnp.where(chosen, a2, v2)
            i1, i2 = jnp.where(chosen, j1, i1), jnp.where(chosen, j2, i2)
    e21 = jnp.exp(v2 - v1)
    w1 = p_g / (1.0 + e21)
    w2 = p_g * e21 / (1.0 + e21)
    e1 = g_idx * EXPERTS_PER_GROUP + i1
    e2 = g_idx * EXPERTS_PER_GROUP + i2
    rows8 = jnp.where(sub == 0, e1.astype(F32),
                      jnp.where(sub == 1, e2.astype(F32),
                                jnp.where(sub == 2, w1, jnp.where(sub == 3, w2, 0.0))))
    route_t = jnp.concatenate([rows8, jnp.zeros((LANES - EXPERTS_PER_GROUP, n), F32)], axis=0)
    expert_row = lax.broadcasted_iota(I32, (LANES, n), 0)
    onehot_t = ((expert_row == e1) | (expert_row == e2)).astype(BF16)
    cnt = lax.dot_general(jnp.ones((8, n), BF16), onehot_t, (((1,), (1,)), ((), ())),
                          preferred_element_type=F32)
    return jnp.transpose(route_t), cnt


def _mix(merged, xf, wo, g, wr, br, chunk, t):
    n_tiles = t // TM_MIX
    first = chunk * n_tiles
    cnt_rows = TM_MIX // TM_CNT * 8
    full = lambda shape: pl.BlockSpec(shape, lambda i: (0, 0), pipeline_mode=pl.Buffered(1))
    tile = lambda w=D_MODEL: pl.BlockSpec((TM_MIX, w), lambda i: (i, 0))
    return pl.pallas_call(
        _mix_kernel,
        name="merge_router",
        grid=(n_tiles,),
        in_specs=[
            tile(), pl.BlockSpec((TM_MIX, D_MODEL), lambda i: (first + i, 0)),
            full((D_MODEL, D_MODEL)), full((1, D_MODEL)),
            full((D_MODEL, 2 * LANES)), full((1, LANES)),
        ],
        out_specs=[tile(PACKED), tile(PACKED), tile(LANES),
                   pl.BlockSpec((cnt_rows, LANES), lambda i: (i, 0))],
        out_shape=[
            jax.ShapeDtypeStruct((t, PACKED), I32),
            jax.ShapeDtypeStruct((t, PACKED), I32),
            jax.ShapeDtypeStruct((t, LANES), F32),
            jax.ShapeDtypeStruct((n_tiles * cnt_rows, LANES), F32),
        ],
        compiler_params=pltpu.CompilerParams(
            dimension_semantics=("arbitrary",), vmem_limit_bytes=VMEM_LIMIT),
    )(merged, xf, wo, g, wr, br)


def _pos_kernel(route_ref, base_ref, pos_ref):
    lane = lax.broadcasted_iota(I32, (TM_CNT, LANES), 1)
    r = lax.broadcasted_iota(I32, (TM_CNT, TM_CNT), 0)
    c = lax.broadcasted_iota(I32, (TM_CNT, TM_CNT), 1)
    lower = (c < r).astype(BF16)
    subs = [slice(s * TM_CNT, (s + 1) * TM_CNT) for s in range(POS_TILES)]
    routes = [route_ref[rs, :] for rs in subs]
    e1 = [jnp.sum(jnp.where(lane == 0, rt, 0.0), axis=-1, keepdims=True).astype(I32) for rt in routes]
    e2 = [jnp.sum(jnp.where(lane == 1, rt, 0.0), axis=-1, keepdims=True).astype(I32) for rt in routes]
    onehot = [((lane == a) | (lane == b)).astype(BF16) for a, b in zip(e1, e2)]
    before = [jnp.dot(lower, oh, preferred_element_type=F32) + base_ref[s]
              for s, oh in enumerate(onehot)]
    for s, rs in enumerate(subs):
        p1 = jnp.sum(jnp.where(lane == e1[s], before[s], 0.0), axis=-1, keepdims=True)
        p2 = jnp.sum(jnp.where(lane == e2[s], before[s], 0.0), axis=-1, keepdims=True)
        packed = jnp.where(lane == 0, p1, jnp.where(lane == 1, p2, 0.0))
        pos_ref[:, rs] = jnp.transpose(packed)[0:TOP_K, :].astype(I32)


def _positions(route, base):
    t = route.shape[0]
    n_steps = t // (TM_CNT * POS_TILES)
    return pl.pallas_call(
        _pos_kernel,
        name="positions",
        grid=(n_steps,),
        in_specs=[
            pl.BlockSpec((TM_CNT * POS_TILES, LANES), lambda i: (i, 0)),
            pl.BlockSpec((POS_TILES, 1, LANES), lambda i: (i, 0, 0)),
        ],
        out_specs=pl.BlockSpec((TOP_K, TM_CNT * POS_TILES), lambda i: (0, i)),
        out_shape=jax.ShapeDtypeStruct((TOP_K, t), I32),
        compiler_params=pltpu.CompilerParams(dimension_semantics=("arbitrary",)),
    )(route, base)


def _sc_mesh():
    return plsc.VectorSubcoreMesh(core_axis_name="c", subcore_axis_name="s",
                                  num_cores=SC_CORES, num_subcores=SC_SUBCORES)


def _sc_worker():
    return lax.axis_index("s") * SC_CORES + lax.axis_index("c")


def _dispatch(pos, hp, rows):
    t = hp.shape[0]
    per_w = t // SC_WORKERS
    n_ch = per_w // SC_CHUNK
    pos4 = pos.reshape(TOP_K, SC_WORKERS, n_ch, SC_CHUNK)

    @functools.partial(
        pl.kernel, mesh=_sc_mesh(),
        out_type=jax.ShapeDtypeStruct((rows, PACKED), I32),
        scratch_types=[pltpu.VMEM((TOP_K, n_ch, SC_CHUNK), I32),
                       pltpu.VMEM((SC_CHUNK, PACKED), I32)])
    def scatter(hp_hbm, pos_hbm, xs_hbm, idx_v, rows_v):
        wid = _sc_worker()
        for k in range(TOP_K):
            pltpu.sync_copy(pos_hbm.at[k, wid], idx_v.at[k])

        def body(c, carry):
            start = pl.multiple_of(wid * per_w + c * SC_CHUNK, SC_CHUNK)
            pltpu.sync_copy(hp_hbm.at[pl.ds(start, SC_CHUNK)], rows_v)
            for k in range(TOP_K):
                pltpu.sync_copy(rows_v, xs_hbm.at[idx_v.at[k, c]])
            return carry

        lax.fori_loop(0, n_ch, body, 0)

    return scatter(hp, pos4)


def _gather_rows(table, idx):
    n = idx.shape[0]
    per_w = n // SC_WORKERS
    n_ch = per_w // SC_CHUNK
    idx3 = idx.reshape(SC_WORKERS, n_ch, SC_CHUNK)

    @functools.partial(
        pl.kernel, mesh=_sc_mesh(),
        out_type=jax.ShapeDtypeStruct((n, PACKED), I32),
        scratch_types=[pltpu.VMEM((n_ch, SC_CHUNK), I32),
                       pltpu.VMEM((SC_CHUNK, PACKED), I32)])
    def gather(table_hbm, idx_hbm, out_hbm, idx_v, rows_v):
        wid = _sc_worker()
        pltpu.sync_copy(idx_hbm.at[wid], idx_v)

        def body(c, carry):
            start = pl.multiple_of(wid * per_w + c * SC_CHUNK, SC_CHUNK)
            pltpu.sync_copy(table_hbm.at[idx_v.at[c]], rows_v)
            pltpu.sync_copy(rows_v, out_hbm.at[pl.ds(start, SC_CHUNK)])
            return carry

        lax.fori_loop(0, n_ch, body, 0)

    return gather(table, idx3)


def _expert_kernel(te_ref, nx_ref, sl_ref, ts_ref, tv_ref, xs_ref, wg_hbm, wu_hbm, wd_hbm, o_ref,
                   wg_s, wu_s, wd_s, sem):
    del ts_ref
    i = pl.program_id(0)
    n_valid = tv_ref[i]
    expert = te_ref[i]
    slot = sl_ref[i]

    def weight_copies(e, s):
        return (pltpu.make_async_copy(wg_hbm.at[e], wg_s.at[s], sem.at[s, 0]),
                pltpu.make_async_copy(wu_hbm.at[e], wu_s.at[s], sem.at[s, 1]),
                pltpu.make_async_copy(wd_hbm.at[e], wd_s.at[s], sem.at[s, 2]))

    @pl.when(i == 0)
    def _():
        for copy in weight_copies(expert, slot):
            copy.start()

    @pl.when(jnp.logical_or(i == 0, expert != te_ref[jnp.maximum(i - 1, 0)]))
    def _():
        for copy in weight_copies(expert, slot):
            copy.wait()
        nxt = nx_ref[i]

        @pl.when(nxt >= 0)
        def _():
            for copy in weight_copies(nxt, 1 - slot):
                copy.start()

    def mlp(n_sub):
        wg, wu, wd = wg_s.at[slot], wu_s.at[slot], wd_s.at[slot]
        subs = [slice(s * SUB_EXP, (s + 1) * SUB_EXP) for s in range(n_sub)]
        xin = []
        for r in subs:
            rid = r.start + lax.broadcasted_iota(I32, (SUB_EXP, PACKED), 0)
            lo, hi = _unpack_bf16_pairs(jnp.where(rid < n_valid, xs_ref[r, :], 0))
            xin.append((lo.astype(BF16), hi.astype(BF16)))
        ab = [(jnp.dot(lo, wg[:PACKED, :], preferred_element_type=F32)
               + jnp.dot(hi, wg[PACKED:, :], preferred_element_type=F32),
               jnp.dot(lo, wu[:PACKED, :], preferred_element_type=F32)
               + jnp.dot(hi, wu[PACKED:, :], preferred_element_type=F32)) for lo, hi in xin]
        for r, (ai, bi) in zip(subs, ab):
            hm = (ai * jax.nn.sigmoid(ai) * bi).astype(BF16)
            o_ref[r, :] = _pack_bf16_pairs(jnp.dot(hm, wd[...], preferred_element_type=F32))
        if n_sub * SUB_EXP < TM_EXP:
            o_ref[n_sub * SUB_EXP:, :] = jnp.zeros((TM_EXP - n_sub * SUB_EXP, PACKED), I32)

    n_subs = TM_EXP // SUB_EXP
    for n_sub in range(n_subs + 1):
        lo_rows = (n_sub - 1) * SUB_EXP if n_sub else -1
        in_range = jnp.logical_and(n_valid > lo_rows, n_valid <= n_sub * SUB_EXP)
        pl.when(in_range)(functools.partial(mlp, n_sub))


def _experts(tile_expert, next_expert, tile_slot, tile_src, tile_valid, xs, wg, wu, wd):
    rows = xs.shape[0]
    hbm = pl.BlockSpec(memory_space=pl.ANY)
    grid_spec = pltpu.PrefetchScalarGridSpec(
        num_scalar_prefetch=5,
        grid=(rows // TM_EXP,),
        in_specs=[pl.BlockSpec((TM_EXP, PACKED), lambda i, te, nx, sl, ts, tv: (ts[i], 0)),
                  hbm, hbm, hbm],
        out_specs=pl.BlockSpec((TM_EXP, PACKED), lambda i, te, nx, sl, ts, tv: (i, 0)),
        scratch_shapes=[
            pltpu.VMEM((2, D_MODEL, D_FF), BF16), pltpu.VMEM((2, D_MODEL, D_FF), BF16),
            pltpu.VMEM((2, D_FF, D_MODEL), BF16),
            pltpu.SemaphoreType.DMA((2, 3)),
        ],
    )
    return pl.pallas_call(
        _expert_kernel,
        name="experts",
        grid_spec=grid_spec,
        out_shape=jax.ShapeDtypeStruct((rows, PACKED), I32),
        compiler_params=pltpu.CompilerParams(
            dimension_semantics=("arbitrary",), vmem_limit_bytes=VMEM_LIMIT),
    )(tile_expert, next_expert, tile_slot, tile_src, tile_valid, xs, wg, wu, wd)


def _combine_kernel(y1_ref, y2_ref, route_ref, xmid_ref, g_ref, *rest):
    o_ref = rest[-1]
    route = route_ref[...]
    lane = lax.broadcasted_iota(I32, route.shape, 1)
    w1 = jnp.sum(jnp.where(lane == 2, route, 0.0), axis=-1, keepdims=True)
    w2 = jnp.sum(jnp.where(lane == 3, route, 0.0), axis=-1, keepdims=True)
    lo1, hi1 = _unpack_bf16_pairs(y1_ref[...])
    lo2, hi2 = _unpack_bf16_pairs(y2_ref[...])
    lox, hix = _unpack_bf16_pairs(xmid_ref[...])
    x_out = jnp.concatenate([lox + (lo1 * w1 + lo2 * w2), hix + (hi1 * w1 + hi2 * w2)], axis=1)
    o_ref[...] = _rms(x_out, g_ref[...])


def _combine(yg, route, xmid, g, chunk, t_total, out_prev):
    t = xmid.shape[0]
    n_tiles = t // TM_CMB
    first = chunk * n_tiles
    in_specs = [
        pl.BlockSpec((TM_CMB, PACKED), lambda i: (i, 0)),
        pl.BlockSpec((TM_CMB, PACKED), lambda i: (n_tiles + i, 0)),
        pl.BlockSpec((TM_CMB, LANES), lambda i: (i, 0)),
        pl.BlockSpec((TM_CMB, PACKED), lambda i: (i, 0)),
        pl.BlockSpec((1, D_MODEL), lambda i: (0, 0)),
    ]
    args = [yg, yg, route, xmid, g]
    aliases = {}
    if out_prev is not None:
        in_specs.append(pl.BlockSpec(memory_space=pl.ANY))
        aliases = {len(args): 0}
        args.append(out_prev)
    return pl.pallas_call(
        _combine_kernel,
        name="combine",
        grid=(n_tiles,),
        in_specs=in_specs,
        out_specs=pl.BlockSpec((TM_CMB, D_MODEL), lambda i: (first + i, 0)),
        out_shape=jax.ShapeDtypeStruct((t_total, D_MODEL), F32),
        input_output_aliases=aliases,
        compiler_params=pltpu.CompilerParams(
            dimension_semantics=("arbitrary",), vmem_limit_bytes=VMEM_LIMIT),
    )(*args)


def _split_bf16(w):
    hi = w.astype(BF16)
    lo = (w - hi.astype(F32)).astype(BF16)
    return hi, lo


def kernel(x, norm_mix, w_in, conv_w, conv_b, w_a_out, sinks, w_b_out, w_o, norm_ffn, w_group,
           b_group, w_expert, b_expert, w_gate, w_up, w_down, norm_final):
    bsz, seq, d = x.shape
    t = bsz * seq
    assert d == D_MODEL and seq % TM_PROJ == 0 and seq % TQ_ATTN == 0
    xf = x.reshape(t, d)
    row = lambda v: v.reshape(1, -1)

    assert w_in.shape == (D_MODEL, IN_COLS)
    za, proj, w_gate, w_up, w_down = _inproj(xf, row(norm_mix), w_in.astype(BF16), conv_w,
                                             row(conv_b), w_a_out.astype(BF16), seq,
                                             w_gate, w_up, w_down)
    pad = LANES - N_GROUPS - N_EXPERTS
    w_r = jnp.concatenate([w_expert, w_group, jnp.zeros((d, pad), F32)], axis=1)
    b_r = jnp.concatenate([b_expert, b_group, jnp.zeros((pad,), F32)]).reshape(1, LANES)
    wr = jnp.concatenate(_split_bf16(w_r), axis=1)
    wb = w_b_out.astype(BF16)
    wo = w_o.astype(BF16)

    t_chunk = t // MOE_CHUNKS
    assert t == t_chunk * MOE_CHUNKS and all(
        t_chunk % step == 0
        for step in (TQ_ATTN, TM_MIX, TM_CNT * POS_TILES, TM_CMB, SC_WORKERS * SC_CHUNK))
    out = None
    for chunk in range(MOE_CHUNKS):
        merged = _attention(proj, za, sinks, wb, seq, chunk, t_chunk)
        xmid, h2, route, cnt = _mix(merged, xf, wo, row(norm_ffn), wr, b_r, chunk, t_chunk)
        out = _moe_chunk(xmid, h2, route, cnt, w_gate, w_up, w_down, row(norm_final), chunk, t, out)
    return out.reshape(bsz, seq, d)


def _moe_chunk(xmid, h2, route, cnt, w_gate, w_up, w_down, g_final, chunk, t_total, out_prev):
    t = xmid.shape[0]
    n_tiles = t // TM_CNT
    cnt = cnt.reshape(n_tiles, 8, LANES)[:, 0, :N_EXPERTS].astype(I32)
    totals = jnp.sum(cnt, axis=0)
    tiles_e = (totals + TM_EXP - 1) // TM_EXP
    tile_end = jnp.cumsum(tiles_e)
    offset = (tile_end - tiles_e) * TM_EXP
    base = offset[None, :] + jnp.cumsum(cnt, axis=0) - cnt
    base = jnp.pad(base, ((0, 0), (0, LANES - N_EXPERTS))).astype(F32).reshape(n_tiles, 1, LANES)
    rows = t * TOP_K + N_EXPERTS * TM_EXP
    n_active = tile_end[-1]
    tile_id = jnp.arange(rows // TM_EXP, dtype=I32)
    tile_src = jnp.minimum(tile_id, n_active - 1)
    tile_start = tile_end - tiles_e
    owns = (tile_src[:, None] >= tile_start[None, :]) & (tile_src[:, None] < tile_end[None, :])
    pick = lambda per_expert: jnp.sum(jnp.where(owns, per_expert[None, :], 0), axis=1)
    experts = jnp.arange(N_EXPERTS, dtype=I32)
    tile_expert = pick(experts)
    tile_valid = jnp.clip(pick(totals) - (tile_id - pick(tile_start)) * TM_EXP, 0, TM_EXP)
    tile_valid = jnp.where(tile_id < n_active, tile_valid, 0).astype(I32)
    has_tiles = tiles_e > 0
    later = (experts[None, :] > experts[:, None]) & has_tiles[None, :]
    nxt = jnp.min(jnp.where(later, experts[None, :], N_EXPERTS), axis=1)
    next_expert = pick(jnp.where(nxt < N_EXPERTS, nxt, -1))
    tile_slot = pick((jnp.cumsum(has_tiles.astype(I32)) - 1) % 2)

    pos = _positions(route, base)
    xs = _dispatch(pos, h2, rows)
    ys = _experts(tile_expert.astype(I32), next_expert.astype(I32), tile_slot.astype(I32),
                  tile_src.astype(I32), tile_valid, xs, w_gate, w_up, w_down)
    yg = _gather_rows(ys, pos.reshape(TOP_K * t))
    return _combine(yg, route, xmid, g_final, chunk, t_total, out_prev)
```

```python
import functools
import math

import jax
import jax.numpy as jnp
from jax import lax
from jax.experimental import pallas as pl
from jax.experimental.pallas import tpu as pltpu
from jax.experimental.pallas import tpu_sc as plsc

F32 = jnp.float32
BF16 = jnp.bfloat16
I32 = jnp.int32

D_MODEL = 1024
HEAD_DIM = 64
N_HEADS = 16
N_KV_HEADS = 4
GROUP = N_HEADS // N_KV_HEADS
KV_WIDTH = N_KV_HEADS * HEAD_DIM
WINDOW = 128
N_GROUPS = 4
EXPERTS_PER_GROUP = 8
N_EXPERTS = N_GROUPS * EXPERTS_PER_GROUP
TOP_K = 2
D_FF = 512
EPS = 1e-6
LANES = 128

COL_WIDTH = {"b": D_MODEL, "c": D_MODEL, "u": D_MODEL, "q": D_MODEL, "k": KV_WIDTH, "v": KV_WIDTH,
             "ga": D_MODEL, "gb": D_MODEL}
COL_START = dict(zip(COL_WIDTH, (sum(list(COL_WIDTH.values())[:i]) for i in range(len(COL_WIDTH)))))
IN_COLS = sum(COL_WIDTH.values())
REST_COLS = 2 * D_MODEL + 2 * KV_WIDTH
COL_GB, COL_Q = 0, 1
COL_K, COL_V = 2 * D_MODEL // KV_WIDTH, 2 * D_MODEL // KV_WIDTH + 1

TM_PROJ = 512
SUB_PROJ = 256
TQ_ATTN = 1024
TM_MIX = 1024
SUB_MIX = 256
TM_CNT = 512
POS_TILES = 8
MOE_CHUNKS = 2
TM_EXP = 512
SUB_EXP = 256
TM_CMB = 2048
HALO_ROWS = 8
VMEM_LIMIT = 56 * 1024 * 1024
PACKED = D_MODEL // 2

SC_CORES = 2
SC_SUBCORES = 16
SC_WORKERS = SC_CORES * SC_SUBCORES
SC_CHUNK = 128


def _rms(x, g):
    r = lax.rsqrt(jnp.mean(x * x, axis=-1, keepdims=True) + EPS)
    return (x * r) * g


def _pack_bf16_pairs(x):
    n = x.shape[1] // 2
    lo = lax.bitcast_convert_type(x[:, :n].astype(BF16).astype(F32), I32)
    hi = lax.bitcast_convert_type(x[:, n:].astype(BF16).astype(F32), I32)
    return (hi & jnp.int32(-65536)) | lax.shift_right_logical(lo, 16)


def _unpack_bf16_pairs(p):
    lo = lax.bitcast_convert_type(lax.shift_left(p, 16), F32)
    hi = lax.bitcast_convert_type(p & jnp.int32(-65536), F32)
    return lo, hi


def _inproj_kernel(x_ref, g_ref, w_ref, cw_ref, cb_ref, wa_ref,
                   eg_ref, eu_ref, ed_ref,
                   za_ref, proj_ref, eg_out, eu_out, ed_out, halo_ref, *, tiles_per_seq):
    i = pl.program_id(0)
    eg_out[...] = eg_ref[...].astype(BF16)
    eu_out[...] = eu_ref[...].astype(BF16)
    ed_out[...] = ed_ref[...].astype(BF16)
    halves = [slice(s * SUB_PROJ, (s + 1) * SUB_PROJ) for s in range(TM_PROJ // SUB_PROJ)]
    hs = [_rms(x_ref[r, :], g_ref[...]).astype(BF16) for r in halves]
    col = lambda a, b=None: slice(COL_START[a], COL_START[b or a] + COL_WIDTH[b or a])
    proj = lambda a, b=None: [jnp.dot(h, w_ref[:, col(a, b)], preferred_element_type=F32) for h in hs]
    pcu = jnp.concatenate(proj("c", "u"), axis=0)
    pb = jnp.concatenate(proj("b"), axis=0)
    pga = jnp.concatenate(proj("ga"), axis=0)
    for r, pgb, pqkv in zip(halves, proj("gb"), proj("q", "v")):
        proj_ref[r, :D_MODEL] = pgb.astype(BF16)
        proj_ref[r, D_MODEL:] = pqkv.astype(BF16)
    cu = pcu[:, :D_MODEL] * pcu[:, D_MODEL:]
    first = (i % tiles_per_seq) == 0
    hist = jnp.where(first, 0.0, halo_ref[...])
    prev1 = hist[HALO_ROWS - 1:HALO_ROWS]
    prev2 = hist[HALO_ROWS - 2:HALO_ROWS - 1]
    halo_ref[...] = cu[TM_PROJ - HALO_ROWS:, :]
    row = lax.broadcasted_iota(I32, cu.shape, 0)
    cu1 = jnp.where(row == 0, prev1, pltpu.roll(cu, 1, 0))
    cu2 = jnp.where(row == 0, prev2, jnp.where(row == 1, prev1, pltpu.roll(cu, 2, 0)))
    cw = cw_ref[...]
    y = cw[0:1] * cu2 + cw[1:2] * cu1 + cw[2:3] * cu + cb_ref[...]
    ya = (pb * y).astype(BF16)
    z = jnp.dot(ya, wa_ref[...], preferred_element_type=F32)
    za_ref[...] = (jax.nn.sigmoid(pga) * z).astype(BF16)


def _inproj(xf, g, w_in, conv_w, conv_b, wa, seq, w_gate, w_up, w_down):
    t = xf.shape[0]
    n_tiles = t // TM_PROJ
    const = lambda shape: pl.BlockSpec(shape, lambda i: (0, 0), pipeline_mode=pl.Buffered(1))
    slabs = [w.reshape(-1, w.shape[-1]) for w in (w_gate, w_up, w_down)]
    slab_rows = [s.shape[0] // n_tiles for s in slabs]
    assert all(s.shape[0] == r * n_tiles and r % 16 == 0 for s, r in zip(slabs, slab_rows))
    slab_specs = [pl.BlockSpec((r, s.shape[1]), lambda i: (i, 0)) for s, r in zip(slabs, slab_rows)]
    outs = pl.pallas_call(
        functools.partial(_inproj_kernel, tiles_per_seq=seq // TM_PROJ),
        name="inproj_conv",
        grid=(n_tiles,),
        in_specs=[
            pl.BlockSpec((TM_PROJ, D_MODEL), lambda i: (i, 0)),
            const((1, D_MODEL)),
            const((D_MODEL, IN_COLS)),
            const((3, D_MODEL)), const((1, D_MODEL)), const((D_MODEL, D_MODEL)),
        ] + slab_specs,
        out_specs=[pl.BlockSpec((TM_PROJ, D_MODEL), lambda i: (i, 0)),
                   pl.BlockSpec((TM_PROJ, REST_COLS), lambda i: (i, 0))] + slab_specs,
        out_shape=[jax.ShapeDtypeStruct((t, D_MODEL), BF16),
                   jax.ShapeDtypeStruct((t, REST_COLS), BF16)]
                  + [jax.ShapeDtypeStruct(s.shape, BF16) for s in slabs],
        scratch_shapes=[pltpu.VMEM((HALO_ROWS, D_MODEL), F32)],
        compiler_params=pltpu.CompilerParams(
            dimension_semantics=("arbitrary",), vmem_limit_bytes=VMEM_LIMIT),
    )(xf, g, w_in, conv_w, conv_b, wa, *slabs)
    za, proj, eg, eu, ed = outs
    return za, proj, eg.reshape(w_gate.shape), eu.reshape(w_up.shape), ed.reshape(w_down.shape)


def _attn_kernel(sink_ref, q_ref, k_ref, v_ref, kp_ref, vp_ref, gb_ref, za_ref, wb_ref, o_ref,
                 *, first_tile_index, tiles_per_seq):
    first_tile = ((first_tile_index + pl.program_id(0)) % tiles_per_seq) == 0
    ks = lax.broadcasted_iota(I32, (WINDOW, WINDOW), 0)
    qq = lax.broadcasted_iota(I32, (WINDOW, WINDOW), 1)
    own = ks <= qq
    dist = jnp.where(own, qq - ks, qq - ks + WINDOW).astype(F32)
    visible0 = jnp.logical_or(own, jnp.logical_not(first_tile))
    log2e = math.log2(math.e)
    c_scale = log2e / math.sqrt(HEAD_DIM)
    nt = (((1,), (1,)), ((), ()))
    zk = jnp.zeros((2 * WINDOW, HEAD_DIM), BF16)

    def transposed(v_blk):
        return jnp.transpose(v_blk.astype(F32)).astype(BF16)

    def project(rows_p, attn_blk):
        yb = jnp.dot(attn_blk, wb_ref[...], preferred_element_type=F32)
        zb = jax.nn.sigmoid(gb_ref[rows_p, :].astype(F32)) * yb
        o_ref[rows_p, :] = (za_ref[rows_p, :].astype(F32) + zb).astype(BF16)

    pending = None
    prev_k = kp_ref[...]
    prev_vt = transposed(vp_ref[...])
    for sb in range(TQ_ATTN // WINDOW):
        rows = slice(sb * WINDOW, (sb + 1) * WINDOW)
        cur_k = k_ref[rows, :]
        cur_vt = transposed(v_ref[rows, :])
        scores, vcats = [], []
        for kh in range(N_KV_HEADS):
            cols = slice(kh * HEAD_DIM, (kh + 1) * HEAD_DIM)
            kcat = jnp.concatenate([prev_k[:, cols], cur_k[:, cols]], axis=0)
            vcats.append(jnp.concatenate([prev_vt[cols, :], cur_vt[cols, :]], axis=1))
            qg = jnp.concatenate([q_ref[rows, (2 * kh) * LANES:(2 * kh + 1) * LANES],
                                  q_ref[rows, (2 * kh + 1) * LANES:(2 * kh + 2) * LANES]], axis=0)
            k_pad = jnp.concatenate([jnp.concatenate([kcat, zk], axis=1),
                                     jnp.concatenate([zk, kcat], axis=1)], axis=0)
            scores.append(lax.dot_general(k_pad, qg, nt, preferred_element_type=F32))
        if pending is not None:
            project(*pending)
        probs, rdens = [], []
        for kh in range(N_KV_HEADS):
            for pos in range(2):
                pr, rd = [], []
                for half in range(2):
                    h = kh * GROUP + 2 * half + pos
                    slope = 2.0 ** (-8.0 * (h + 1) / N_HEADS)
                    qcols = slice(half * WINDOW, (half + 1) * WINDOW)
                    krow = pos * 2 * WINDOW
                    st = scores[kh]
                    s = (jnp.where(own, st[krow + WINDOW:krow + 2 * WINDOW, qcols],
                                   st[krow:krow + WINDOW, qcols]) * c_scale
                         - (slope * log2e) * dist)
                    if sb == 0:
                        s = jnp.where(visible0, s, -jnp.inf)
                    m = jnp.max(s, axis=0, keepdims=True)
                    p = jnp.exp2(s - m)
                    den = jnp.sum(p, axis=0, keepdims=True) + jnp.exp2(sink_ref[h] * log2e - m)
                    rd.append(1.0 / den)
                    pr.append(jnp.concatenate(
                        [jnp.where(own, 0.0, p).astype(BF16), jnp.where(own, p, 0.0).astype(BF16)],
                        axis=0))
                probs.append(jnp.concatenate(pr, axis=1))
                rdens.append(jnp.concatenate(rd, axis=1))
        out_t = [None] * N_HEADS
        for kh in range(N_KV_HEADS):
            for pos in range(2):
                o2 = jnp.dot(vcats[kh], probs[2 * kh + pos], preferred_element_type=F32)
                o2 = o2 * rdens[2 * kh + pos]
                out_t[kh * GROUP + pos] = o2[:, :WINDOW]
                out_t[kh * GROUP + 2 + pos] = o2[:, WINDOW:]
        pending = (rows, jnp.transpose(jnp.concatenate(out_t, axis=0)).astype(BF16))
        prev_k, prev_vt = cur_k, cur_vt
    project(*pending)


def _attention(proj, za, sinks, wb, seq, chunk, t):
    sub = TQ_ATTN // WINDOW
    n_tiles = t // TQ_ATTN
    first = chunk * n_tiles
    src = lambda w, c: pl.BlockSpec((TQ_ATTN, w), lambda i: (first + i, c))
    prev = lambda c: pl.BlockSpec((WINDOW, KV_WIDTH),
                                  lambda i: (jnp.maximum((first + i) * sub - 1, 0), c))
    return pl.pallas_call(
        functools.partial(_attn_kernel, first_tile_index=first, tiles_per_seq=seq // TQ_ATTN),
        name="swattn",
        grid=(n_tiles,),
        in_specs=[
            pl.BlockSpec(memory_space=pltpu.SMEM),
            src(D_MODEL, COL_Q), src(KV_WIDTH, COL_K), src(KV_WIDTH, COL_V), prev(COL_K), prev(COL_V),
            src(D_MODEL, COL_GB), src(D_MODEL, 0),
            pl.BlockSpec((D_MODEL, D_MODEL), lambda i: (0, 0), pipeline_mode=pl.Buffered(1)),
        ],
        out_specs=pl.BlockSpec((TQ_ATTN, D_MODEL), lambda i: (i, 0)),
        out_shape=jax.ShapeDtypeStruct((t, D_MODEL), BF16),
        compiler_params=pltpu.CompilerParams(
            dimension_semantics=("arbitrary",), vmem_limit_bytes=VMEM_LIMIT),
    )(sinks, proj, proj, proj, proj, proj, proj, za, wb)


def _mix_kernel(merged_ref, x_ref, wo_ref, g_ref, wr_ref, br_ref,
                xmid_ref, h_ref, route_ref, cnt_ref):
    subs = [slice(s * SUB_MIX, (s + 1) * SUB_MIX) for s in range(TM_MIX // SUB_MIX)]
    xm = [x_ref[r, :] + jnp.dot(merged_ref[r, :], wo_ref[...], preferred_element_type=F32)
          for r in subs]
    hs = []
    for r, v in zip(subs, xm):
        xmid_ref[r, :] = _pack_bf16_pairs(v)
        h = _rms(v, g_ref[...])
        h_ref[r, :] = _pack_bf16_pairs(h)
        hs.append(h)
    wr = wr_ref[...]
    logits = []
    for h in hs:
        h_hi = h.astype(BF16)
        h_lo = (h - h_hi.astype(F32)).astype(BF16)
        both = jnp.dot(h_hi, wr, preferred_element_type=F32)
        logits.append(both[:, :LANES] + both[:, LANES:]
                      + jnp.dot(h_lo, wr[:, :LANES], preferred_element_type=F32) + br_ref[...])
    subs_per_cnt = TM_CNT // SUB_MIX
    cnts = [jnp.zeros((8, LANES), F32) for _ in range(TM_MIX // TM_CNT)]
    for s, (r, lg) in enumerate(zip(subs, logits)):
        route, cnt = _route(lg)
        route_ref[r, :] = route
        cnts[s // subs_per_cnt] = cnts[s // subs_per_cnt] + cnt
    for c, cnt in enumerate(cnts):
        cnt_ref[c * 8:(c + 1) * 8, :] = cnt


def _route(logits):
    n = logits.shape[0]
    lt = jnp.transpose(logits)
    sub = lax.broadcasted_iota(I32, (EXPERTS_PER_GROUP, n), 0)
    neg = -jnp.inf
    gl = jnp.where(sub < N_GROUPS, lt[N_EXPERTS:N_EXPERTS + EXPERTS_PER_GROUP], neg)
    gmax = jnp.max(gl, axis=0, keepdims=True)
    g_idx = jnp.min(jnp.where(gl == gmax, sub, EXPERTS_PER_GROUP), axis=0, keepdims=True)
    p_g = 1.0 / jnp.sum(jnp.exp(gl - gmax), axis=0, keepdims=True)
    v1 = v2 = i1 = i2 = None
    for g in range(N_GROUPS):
        eg = lt[g * EXPERTS_PER_GROUP:(g + 1) * EXPERTS_PER_GROUP]
        a1 = jnp.max(eg, axis=0, keepdims=True)
        j1 = jnp.min(jnp.where(eg == a1, sub, EXPERTS_PER_GROUP), axis=0, keepdims=True)
        eg2 = jnp.where(sub == j1, neg, eg)
        a2 = jnp.max(eg2, axis=0, keepdims=True)
        j2 = jnp.min(jnp.where(eg2 == a2, sub, EXPERTS_PER_GROUP), axis=0, keepdims=True)
        if g == 0:
            v1, v2, i1, i2 = a1, a2, j1, j2
        else:
            chosen = g_idx == g
            v1, v2 = jnp.where(chosen, a1, v1), jnp.where(chosen, a2, v2)
            i1, i2 = jnp.where(chosen, j1, i1), jnp.where(chosen, j2, i2)
    e21 = jnp.exp(v2 - v1)
    w1 = p_g / (1.0 + e21)
    w2 = p_g * e21 / (1.0 + e21)
    e1 = g_idx * EXPERTS_PER_GROUP + i1
    e2 = g_idx * EXPERTS_PER_GROUP + i2
    rows8 = jnp.where(sub == 0, e1.astype(F32),
                      jnp.where(sub == 1, e2.astype(F32),
                                jnp.where(sub == 2, w1, jnp.where(sub == 3, w2, 0.0))))
    route_t = jnp.concatenate([rows8, jnp.zeros((LANES - EXPERTS_PER_GROUP, n), F32)], axis=0)
    expert_row = lax.broadcasted_iota(I32, (LANES, n), 0)
    onehot_t = ((expert_row == e1) | (expert_row == e2)).astype(BF16)
    cnt = lax.dot_general(jnp.ones((8, n), BF16), onehot_t, (((1,), (1,)), ((), ())),
                          preferred_element_type=F32)
    return jnp.transpose(route_t), cnt


def _mix(merged, xf, wo, g, wr, br, chunk, t):
    n_tiles = t // TM_MIX
    first = chunk * n_tiles
    cnt_rows = TM_MIX // TM_CNT * 8
    full = lambda shape: pl.BlockSpec(shape, lambda i: (0, 0), pipeline_mode=pl.Buffered(1))
    tile = lambda w=D_MODEL: pl.BlockSpec((TM_MIX, w), lambda i: (i, 0))
    return pl.pallas_call(
        _mix_kernel,
        name="merge_router",
        grid=(n_tiles,),
        in_specs=[
            tile(), pl.BlockSpec((TM_MIX, D_MODEL), lambda i: (first + i, 0)),
            full((D_MODEL, D_MODEL)), full((1, D_MODEL)),
            full((D_MODEL, 2 * LANES)), full((1, LANES)),
        ],
        out_specs=[tile(PACKED), tile(PACKED), tile(LANES),
                   pl.BlockSpec((cnt_rows, LANES), lambda i: (i, 0))],
        out_shape=[
            jax.ShapeDtypeStruct((t, PACKED), I32),
            jax.ShapeDtypeStruct((t, PACKED), I32),
            jax.ShapeDtypeStruct((t, LANES), F32),
            jax.ShapeDtypeStruct((n_tiles * cnt_rows, LANES), F32),
        ],
        compiler_params=pltpu.CompilerParams(
            dimension_semantics=("arbitrary",), vmem_limit_bytes=VMEM_LIMIT),
    )(merged, xf, wo, g, wr, br)


def _pos_kernel(route_ref, base_ref, pos_ref):
    lane = lax.broadcasted_iota(I32, (TM_CNT, LANES), 1)
    r = lax.broadcasted_iota(I32, (TM_CNT, TM_CNT), 0)
    c = lax.broadcasted_iota(I32, (TM_CNT, TM_CNT), 1)
    lower = (c < r).astype(BF16)
    subs = [slice(s * TM_CNT, (s + 1) * TM_CNT) for s in range(POS_TILES)]
    routes = [route_ref[rs, :] for rs in subs]
    e1 = [jnp.sum(jnp.where(lane == 0, rt, 0.0), axis=-1, keepdims=True).astype(I32) for rt in routes]
    e2 = [jnp.sum(jnp.where(lane == 1, rt, 0.0), axis=-1, keepdims=True).astype(I32) for rt in routes]
    onehot = [((lane == a) | (lane == b)).astype(BF16) for a, b in zip(e1, e2)]
    before = [jnp.dot(lower, oh, preferred_element_type=F32) + base_ref[s]
              for s, oh in enumerate(onehot)]
    for s, rs in enumerate(subs):
        p1 = jnp.sum(jnp.where(lane == e1[s], before[s], 0.0), axis=-1, keepdims=True)
        p2 = jnp.sum(jnp.where(lane == e2[s], before[s], 0.0), axis=-1, keepdims=True)
        packed = jnp.where(lane == 0, p1, jnp.where(lane == 1, p2, 0.0))
        pos_ref[:, rs] = jnp.transpose(packed)[0:TOP_K, :].astype(I32)


def _positions(route, base):
    t = route.shape[0]
    n_steps = t // (TM_CNT * POS_TILES)
    return pl.pallas_call(
        _pos_kernel,
        name="positions",
        grid=(n_steps,),
        in_specs=[
            pl.BlockSpec((TM_CNT * POS_TILES, LANES), lambda i: (i, 0)),
            pl.BlockSpec((POS_TILES, 1, LANES), lambda i: (i, 0, 0)),
        ],
        out_specs=pl.BlockSpec((TOP_K, TM_CNT * POS_TILES), lambda i: (0, i)),
        out_shape=jax.ShapeDtypeStruct((TOP_K, t), I32),
        compiler_params=pltpu.CompilerParams(dimension_semantics=("arbitrary",)),
    )(route, base)


def _sc_mesh():
    return plsc.VectorSubcoreMesh(core_axis_name="c", subcore_axis_name="s",
                                  num_cores=SC_CORES, num_subcores=SC_SUBCORES)


def _sc_worker():
    return lax.axis_index("s") * SC_CORES + lax.axis_index("c")


def _dispatch(pos, hp, rows):
    t = hp.shape[0]
    per_w = t // SC_WORKERS
    n_ch = per_w // SC_CHUNK
    pos4 = pos.reshape(TOP_K, SC_WORKERS, n_ch, SC_CHUNK)

    @functools.partial(
        pl.kernel, mesh=_sc_mesh(),
        out_type=jax.ShapeDtypeStruct((rows, PACKED), I32),
        scratch_types=[pltpu.VMEM((TOP_K, n_ch, SC_CHUNK), I32),
                       pltpu.VMEM((SC_CHUNK, PACKED), I32)])
    def scatter(hp_hbm, pos_hbm, xs_hbm, idx_v, rows_v):
        wid = _sc_worker()
        for k in range(TOP_K):
            pltpu.sync_copy(pos_hbm.at[k, wid], idx_v.at[k])

        def body(c, carry):
            start = pl.multiple_of(wid * per_w + c * SC_CHUNK, SC_CHUNK)
            pltpu.sync_copy(hp_hbm.at[pl.ds(start, SC_CHUNK)], rows_v)
            for k in range(TOP_K):
                pltpu.sync_copy(rows_v, xs_hbm.at[idx_v.at[k, c]])
            return carry

        lax.fori_loop(0, n_ch, body, 0)

    return scatter(hp, pos4)


def _gather_rows(table, idx):
    n = idx.shape[0]
    per_w = n // SC_WORKERS
    n_ch = per_w // SC_CHUNK
    idx3 = idx.reshape(SC_WORKERS, n_ch, SC_CHUNK)

    @functools.partial(
        pl.kernel, mesh=_sc_mesh(),
        out_type=jax.ShapeDtypeStruct((n, PACKED), I32),
        scratch_types=[pltpu.VMEM((n_ch, SC_CHUNK), I32),
                       pltpu.VMEM((SC_CHUNK, PACKED), I32)])
    def gather(table_hbm, idx_hbm, out_hbm, idx_v, rows_v):
        wid = _sc_worker()
        pltpu.sync_copy(idx_hbm.at[wid], idx_v)

        def body(c, carry):
            start = pl.multiple_of(wid * per_w + c * SC_CHUNK, SC_CHUNK)
            pltpu.sync_copy(table_hbm.at[idx_v.at[c]], rows_v)
            pltpu.sync_copy(rows_v, out_hbm.at[pl.ds(start, SC_CHUNK)])
            return carry

        lax.fori_loop(0, n_ch, body, 0)

    return gather(table, idx3)


def _expert_kernel(te_ref, nx_ref, sl_ref, ts_ref, tv_ref, xs_ref, wg_hbm, wu_hbm, wd_hbm, o_ref,
                   wg_s, wu_s, wd_s, sem):
    del ts_ref
    i = pl.program_id(0)
    n_valid = tv_ref[i]
    expert = te_ref[i]
    slot = sl_ref[i]

    def weight_copies(e, s):
        return (pltpu.make_async_copy(wg_hbm.at[e], wg_s.at[s], sem.at[s, 0]),
                pltpu.make_async_copy(wu_hbm.at[e], wu_s.at[s], sem.at[s, 1]),
                pltpu.make_async_copy(wd_hbm.at[e], wd_s.at[s], sem.at[s, 2]))

    @pl.when(i == 0)
    def _():
        for copy in weight_copies(expert, slot):
            copy.start()

    @pl.when(jnp.logical_or(i == 0, expert != te_ref[jnp.maximum(i - 1, 0)]))
    def _():
        for copy in weight_copies(expert, slot):
            copy.wait()
        nxt = nx_ref[i]

        @pl.when(nxt >= 0)
        def _():
            for copy in weight_copies(nxt, 1 - slot):
                copy.start()

    def mlp(n_sub):
        wg, wu, wd = wg_s.at[slot], wu_s.at[slot], wd_s.at[slot]
        subs = [slice(s * SUB_EXP, (s + 1) * SUB_EXP) for s in range(n_sub)]
        xin = []
        for r in subs:
            rid = r.start + lax.broadcasted_iota(I32, (SUB_EXP, PACKED), 0)
            lo, hi = _unpack_bf16_pairs(jnp.where(rid < n_valid, xs_ref[r, :], 0))
            xin.append((lo.astype(BF16), hi.astype(BF16)))
        ab = [(jnp.dot(lo, wg[:PACKED, :], preferred_element_type=F32)
               + jnp.dot(hi, wg[PACKED:, :], preferred_element_type=F32),
               jnp.dot(lo, wu[:PACKED, :], preferred_element_type=F32)
               + jnp.dot(hi, wu[PACKED:, :], preferred_element_type=F32)) for lo, hi in xin]
        for r, (ai, bi) in zip(subs, ab):
            hm = (ai * jax.nn.sigmoid(ai) * bi).astype(BF16)
            o_ref[r, :] = _pack_bf16_pairs(jnp.dot(hm, wd[...], preferred_element_type=F32))
        if n_sub * SUB_EXP < TM_EXP:
            o_ref[n_sub * SUB_EXP:, :] = jnp.zeros((TM_EXP - n_sub * SUB_EXP, PACKED), I32)

    n_subs = TM_EXP // SUB_EXP
    for n_sub in range(n_subs + 1):
        lo_rows = (n_sub - 1) * SUB_EXP if n_sub else -1
        in_range = jnp.logical_and(n_valid > lo_rows, n_valid <= n_sub * SUB_EXP)
        pl.when(in_range)(functools.partial(mlp, n_sub))


def _experts(tile_expert, next_expert, tile_slot, tile_src, tile_valid, xs, wg, wu, wd):
    rows = xs.shape[0]
    hbm = pl.BlockSpec(memory_space=pl.ANY)
    grid_spec = pltpu.PrefetchScalarGridSpec(
        num_scalar_prefetch=5,
        grid=(rows // TM_EXP,),
        in_specs=[pl.BlockSpec((TM_EXP, PACKED), lambda i, te, nx, sl, ts, tv: (ts[i], 0)),
                  hbm, hbm, hbm],
        out_specs=pl.BlockSpec((TM_EXP, PACKED), lambda i, te, nx, sl, ts, tv: (i, 0)),
        scratch_shapes=[
            pltpu.VMEM((2, D_MODEL, D_FF), BF16), pltpu.VMEM((2, D_MODEL, D_FF), BF16),
            pltpu.VMEM((2, D_FF, D_MODEL), BF16),
            pltpu.SemaphoreType.DMA((2, 3)),
        ],
    )
    return pl.pallas_call(
        _expert_kernel,
        name="experts",
        grid_spec=grid_spec,
        out_shape=jax.ShapeDtypeStruct((rows, PACKED), I32),
        compiler_params=pltpu.CompilerParams(
            dimension_semantics=("arbitrary",), vmem_limit_bytes=VMEM_LIMIT),
    )(tile_expert, next_expert, tile_slot, tile_src, tile_valid, xs, wg, wu, wd)


def _combine_kernel(y1_ref, y2_ref, route_ref, xmid_ref, g_ref, *rest):
    o_ref = rest[-1]
    route = route_ref[...]
    lane = lax.broadcasted_iota(I32, route.shape, 1)
    w1 = jnp.sum(jnp.where(lane == 2, route, 0.0), axis=-1, keepdims=True)
    w2 = jnp.sum(jnp.where(lane == 3, route, 0.0), axis=-1, keepdims=True)
    lo1, hi1 = _unpack_bf16_pairs(y1_ref[...])
    lo2, hi2 = _unpack_bf16_pairs(y2_ref[...])
    lox, hix = _unpack_bf16_pairs(xmid_ref[...])
    x_out = jnp.concatenate([lox + (lo1 * w1 + lo2 * w2), hix + (hi1 * w1 + hi2 * w2)], axis=1)
    o_ref[...] = _rms(x_out, g_ref[...])


def _combine(yg, route, xmid, g, chunk, t_total, out_prev):
    t = xmid.shape[0]
    n_tiles = t // TM_CMB
    first = chunk * n_tiles
    in_specs = [
        pl.BlockSpec((TM_CMB, PACKED), lambda i: (i, 0)),
        pl.BlockSpec((TM_CMB, PACKED), lambda i: (n_tiles + i, 0)),
        pl.BlockSpec((TM_CMB, LANES), lambda i: (i, 0)),
        pl.BlockSpec((TM_CMB, PACKED), lambda i: (i, 0)),
        pl.BlockSpec((1, D_MODEL), lambda i: (0, 0)),
    ]
    args = [yg, yg, route, xmid, g]
    aliases = {}
    if out_prev is not None:
        in_specs.append(pl.BlockSpec(memory_space=pl.ANY))
        aliases = {len(args): 0}
        args.append(out_prev)
    return pl.pallas_call(
        _combine_kernel,
        name="combine",
        grid=(n_tiles,),
        in_specs=in_specs,
        out_specs=pl.BlockSpec((TM_CMB, D_MODEL), lambda i: (first + i, 0)),
        out_shape=jax.ShapeDtypeStruct((t_total, D_MODEL), F32),
        input_output_aliases=aliases,
        compiler_params=pltpu.CompilerParams(
            dimension_semantics=("arbitrary",), vmem_limit_bytes=VMEM_LIMIT),
    )(*args)


def _split_bf16(w):
    hi = w.astype(BF16)
    lo = (w - hi.astype(F32)).astype(BF16)
    return hi, lo


def kernel(x, norm_mix, w_in, conv_w, conv_b, w_a_out, sinks, w_b_out, w_o, norm_ffn, w_group,
           b_group, w_expert, b_expert, w_gate, w_up, w_down, norm_final):
    bsz, seq, d = x.shape
    t = bsz * seq
    assert d == D_MODEL and seq % TM_PROJ == 0 and seq % TQ_ATTN == 0
    xf = x.reshape(t, d)
    row = lambda v: v.reshape(1, -1)

    assert w_in.shape == (D_MODEL, IN_COLS)
    za, proj, w_gate, w_up, w_down = _inproj(xf, row(norm_mix), w_in.astype(BF16), conv_w,
                                             row(conv_b), w_a_out.astype(BF16), seq,
                                             w_gate, w_up, w_down)
    pad = LANES - N_GROUPS - N_EXPERTS
    w_r = jnp.concatenate([w_expert, w_group, jnp.zeros((d, pad), F32)], axis=1)
    b_r = jnp.concatenate([b_expert, b_group, jnp.zeros((pad,), F32)]).reshape(1, LANES)
    wr = jnp.concatenate(_split_bf16(w_r), axis=1)
    wb = w_b_out.astype(BF16)
    wo = w_o.astype(BF16)

    t_chunk = t // MOE_CHUNKS
    assert t == t_chunk * MOE_CHUNKS and all(
        t_chunk % step == 0
        for step in (TQ_ATTN, TM_MIX, TM_CNT * POS_TILES, TM_CMB, SC_WORKERS * SC_CHUNK))
    out = None
    for chunk in range(MOE_CHUNKS):
        merged = _attention(proj, za, sinks, wb, seq, chunk, t_chunk)
        xmid, h2, route, cnt = _mix(merged, xf, wo, row(norm_ffn), wr, b_r, chunk, t_chunk)
        out = _moe_chunk(xmid, h2, route, cnt, w_gate, w_up, w_down, row(norm_final), chunk, t, out)
    return out.reshape(bsz, seq, d)


def _moe_chunk(xmid, h2, route, cnt, w_gate, w_up, w_down, g_final, chunk, t_total, out_prev):
    t = xmid.shape[0]
    n_tiles = t // TM_CNT
    cnt = cnt.reshape(n_tiles, 8, LANES)[:, 0, :N_EXPERTS].astype(I32)
    totals = jnp.sum(cnt, axis=0)
    tiles_e = (totals + TM_EXP - 1) // TM_EXP
    tile_end = jnp.cumsum(tiles_e)
    offset = (tile_end - tiles_e) * TM_EXP
    base = offset[None, :] + jnp.cumsum(cnt, axis=0) - cnt
    base = jnp.pad(base, ((0, 0), (0, LANES - N_EXPERTS))).astype(F32).reshape(n_tiles, 1, LANES)
    rows = t * TOP_K + N_EXPERTS * TM_EXP
    n_active = tile_end[-1]
    tile_id = jnp.arange(rows // TM_EXP, dtype=I32)
    tile_src = jnp.minimum(tile_id, n_active - 1)
    tile_expert = jnp.sum((tile_src[:, None] >= tile_end[None, :]).astype(I32), axis=1)
    tile_expert = jnp.minimum(tile_expert, N_EXPERTS - 1)
    row_in_expert = (tile_id - (tile_end - tiles_e)[tile_expert]) * TM_EXP
    tile_valid = jnp.clip(totals[tile_expert] - row_in_expert, 0, TM_EXP)
    tile_valid = jnp.where(tile_id < n_active, tile_valid, 0).astype(I32)
    after = tile_end[tile_expert]
    next_expert = jnp.where(after < n_active, tile_expert[jnp.minimum(after, n_active - 1)], -1)
    first_of_expert = jnp.concatenate(
        [jnp.ones((1,), I32), (tile_expert[1:] != tile_expert[:-1]).astype(I32)])
    tile_slot = (jnp.cumsum(first_of_expert) - 1) % 2

    pos = _positions(route, base)
    xs = _dispatch(pos, h2, rows)
    ys = _experts(tile_expert.astype(I32), next_expert.astype(I32), tile_slot.astype(I32),
                  tile_src.astype(I32), tile_valid, xs, w_gate, w_up, w_down)
    yg = _gather_rows(ys, pos.reshape(TOP_K * t))
    return _combine(yg, route, xmid, g_final, chunk, t_total, out_prev)
```

```python
import functools
import math

import jax
import jax.numpy as jnp
from jax import lax
from jax.experimental import pallas as pl
from jax.experimental.pallas import tpu as pltpu
from jax.experimental.pallas import tpu_sc as plsc

F32 = jnp.float32
BF16 = jnp.bfloat16
I32 = jnp.int32

D_MODEL = 1024
HEAD_DIM = 64
N_HEADS = 16
N_KV_HEADS = 4
GROUP = N_HEADS // N_KV_HEADS
KV_WIDTH = N_KV_HEADS * HEAD_DIM
WINDOW = 128
N_GROUPS = 4
EXPERTS_PER_GROUP = 8
N_EXPERTS = N_GROUPS * EXPERTS_PER_GROUP
TOP_K = 2
D_FF = 512
EPS = 1e-6
LANES = 128

COL_WIDTH = {"b": D_MODEL, "c": D_MODEL, "u": D_MODEL, "q": D_MODEL, "k": KV_WIDTH, "v": KV_WIDTH,
             "ga": D_MODEL, "gb": D_MODEL}
COL_START = dict(zip(COL_WIDTH, (sum(list(COL_WIDTH.values())[:i]) for i in range(len(COL_WIDTH)))))
IN_COLS = sum(COL_WIDTH.values())
REST_COLS = 2 * D_MODEL + 2 * KV_WIDTH
COL_GB, COL_Q = 0, 1
COL_K, COL_V = 2 * D_MODEL // KV_WIDTH, 2 * D_MODEL // KV_WIDTH + 1

TM_PROJ = 512
SUB_PROJ = 256
TQ_ATTN = 1024
TM_MIX = 1024
SUB_MIX = 256
X_RING = 3
TM_CNT = 512
POS_TILES = 8
MOE_CHUNKS = 2
TM_EXP = 512
SUB_EXP = 256
TM_CMB = 2048
HALO_ROWS = 8
VMEM_LIMIT = 56 * 1024 * 1024
PACKED = D_MODEL // 2

SC_CORES = 2
SC_SUBCORES = 16
SC_WORKERS = SC_CORES * SC_SUBCORES
SC_CHUNK = 64


def _rms(x, g):
    r = lax.rsqrt(jnp.mean(x * x, axis=-1, keepdims=True) + EPS)
    return (x * r) * g


def _pack_bf16_pairs(x):
    n = x.shape[1] // 2
    lo = lax.bitcast_convert_type(x[:, :n].astype(BF16).astype(F32), I32)
    hi = lax.bitcast_convert_type(x[:, n:].astype(BF16).astype(F32), I32)
    return (hi & jnp.int32(-65536)) | lax.shift_right_logical(lo, 16)


def _unpack_bf16_pairs(p):
    lo = lax.bitcast_convert_type(lax.shift_left(p, 16), F32)
    hi = lax.bitcast_convert_type(p & jnp.int32(-65536), F32)
    return lo, hi


def _inproj_kernel(x_ref, g_ref, w_ref, cw_ref, cb_ref, wa_ref,
                   eg_ref, eu_ref, ed_ref,
                   za_ref, proj_ref, eg_out, eu_out, ed_out, halo_ref, *, tiles_per_seq):
    i = pl.program_id(0)
    eg_out[...] = eg_ref[...].astype(BF16)
    eu_out[...] = eu_ref[...].astype(BF16)
    ed_out[...] = ed_ref[...].astype(BF16)
    halves = [slice(s * SUB_PROJ, (s + 1) * SUB_PROJ) for s in range(TM_PROJ // SUB_PROJ)]
    hs = [_rms(x_ref[r, :], g_ref[...]).astype(BF16) for r in halves]
    col = lambda a, b=None: slice(COL_START[a], COL_START[b or a] + COL_WIDTH[b or a])
    proj = lambda a, b=None: [jnp.dot(h, w_ref[:, col(a, b)], preferred_element_type=F32) for h in hs]
    pcu = jnp.concatenate(proj("c", "u"), axis=0)
    pb = jnp.concatenate(proj("b"), axis=0)
    pga = jnp.concatenate(proj("ga"), axis=0)
    for r, pgb, pqkv in zip(halves, proj("gb"), proj("q", "v")):
        proj_ref[r, :D_MODEL] = pgb.astype(BF16)
        proj_ref[r, D_MODEL:] = pqkv.astype(BF16)
    cu = pcu[:, :D_MODEL] * pcu[:, D_MODEL:]
    first = (i % tiles_per_seq) == 0
    hist = jnp.where(first, 0.0, halo_ref[...])
    prev1 = hist[HALO_ROWS - 1:HALO_ROWS]
    prev2 = hist[HALO_ROWS - 2:HALO_ROWS - 1]
    halo_ref[...] = cu[TM_PROJ - HALO_ROWS:, :]
    row = lax.broadcasted_iota(I32, cu.shape, 0)
    cu1 = jnp.where(row == 0, prev1, pltpu.roll(cu, 1, 0))
    cu2 = jnp.where(row == 0, prev2, jnp.where(row == 1, prev1, pltpu.roll(cu, 2, 0)))
    cw = cw_ref[...]
    y = cw[0:1] * cu2 + cw[1:2] * cu1 + cw[2:3] * cu + cb_ref[...]
    ya = (pb * y).astype(BF16)
    z = jnp.dot(ya, wa_ref[...], preferred_element_type=F32)
    za_ref[...] = (jax.nn.sigmoid(pga) * z).astype(BF16)


def _inproj(xf, g, w_in, conv_w, conv_b, wa, seq, w_gate, w_up, w_down):
    t = xf.shape[0]
    n_tiles = t // TM_PROJ
    const = lambda shape: pl.BlockSpec(shape, lambda i: (0, 0), pipeline_mode=pl.Buffered(1))
    slabs = [w.reshape(-1, w.shape[-1]) for w in (w_gate, w_up, w_down)]
    slab_rows = [s.shape[0] // n_tiles for s in slabs]
    assert all(s.shape[0] == r * n_tiles and r % 16 == 0 for s, r in zip(slabs, slab_rows))
    slab_specs = [pl.BlockSpec((r, s.shape[1]), lambda i: (i, 0)) for s, r in zip(slabs, slab_rows)]
    outs = pl.pallas_call(
        functools.partial(_inproj_kernel, tiles_per_seq=seq // TM_PROJ),
        name="inproj_conv",
        grid=(n_tiles,),
        in_specs=[
            pl.BlockSpec((TM_PROJ, D_MODEL), lambda i: (i, 0)),
            const((1, D_MODEL)),
            const((D_MODEL, IN_COLS)),
            const((3, D_MODEL)), const((1, D_MODEL)), const((D_MODEL, D_MODEL)),
        ] + slab_specs,
        out_specs=[pl.BlockSpec((TM_PROJ, D_MODEL), lambda i: (i, 0)),
                   pl.BlockSpec((TM_PROJ, REST_COLS), lambda i: (i, 0))] + slab_specs,
        out_shape=[jax.ShapeDtypeStruct((t, D_MODEL), BF16),
                   jax.ShapeDtypeStruct((t, REST_COLS), BF16)]
                  + [jax.ShapeDtypeStruct(s.shape, BF16) for s in slabs],
        scratch_shapes=[pltpu.VMEM((HALO_ROWS, D_MODEL), F32)],
        compiler_params=pltpu.CompilerParams(
            dimension_semantics=("arbitrary",), vmem_limit_bytes=VMEM_LIMIT),
    )(xf, g, w_in, conv_w, conv_b, wa, *slabs)
    za, proj, eg, eu, ed = outs
    return za, proj, eg.reshape(w_gate.shape), eu.reshape(w_up.shape), ed.reshape(w_down.shape)


def _attn_kernel(sink_ref, q_ref, k_ref, v_ref, kp_ref, vp_ref, gb_ref, za_ref, wb_ref, o_ref,
                 *, first_tile_index, tiles_per_seq):
    first_tile = ((first_tile_index + pl.program_id(0)) % tiles_per_seq) == 0
    ks = lax.broadcasted_iota(I32, (WINDOW, WINDOW), 0)
    qq = lax.broadcasted_iota(I32, (WINDOW, WINDOW), 1)
    own = ks <= qq
    dist = jnp.where(own, qq - ks, qq - ks + WINDOW).astype(F32)
    visible0 = jnp.logical_or(own, jnp.logical_not(first_tile))
    log2e = math.log2(math.e)
    c_scale = log2e / math.sqrt(HEAD_DIM)
    nt = (((1,), (1,)), ((), ()))
    zk = jnp.zeros((2 * WINDOW, HEAD_DIM), BF16)

    def transposed(v_blk):
        return jnp.transpose(v_blk.astype(F32)).astype(BF16)

    def project(rows_p, attn_blk):
        yb = jnp.dot(attn_blk, wb_ref[...], preferred_element_type=F32)
        zb = jax.nn.sigmoid(gb_ref[rows_p, :].astype(F32)) * yb
        o_ref[rows_p, :] = (za_ref[rows_p, :].astype(F32) + zb).astype(BF16)

    pending = None
    prev_k = kp_ref[...]
    prev_vt = transposed(vp_ref[...])
    for sb in range(TQ_ATTN // WINDOW):
        rows = slice(sb * WINDOW, (sb + 1) * WINDOW)
        cur_k = k_ref[rows, :]
        cur_vt = transposed(v_ref[rows, :])
        scores, vcats = [], []
        for kh in range(N_KV_HEADS):
            cols = slice(kh * HEAD_DIM, (kh + 1) * HEAD_DIM)
            kcat = jnp.concatenate([prev_k[:, cols], cur_k[:, cols]], axis=0)
            vcats.append(jnp.concatenate([prev_vt[cols, :], cur_vt[cols, :]], axis=1))
            qg = jnp.concatenate([q_ref[rows, (2 * kh) * LANES:(2 * kh + 1) * LANES],
                                  q_ref[rows, (2 * kh + 1) * LANES:(2 * kh + 2) * LANES]], axis=0)
            k_pad = jnp.concatenate([jnp.concatenate([kcat, zk], axis=1),
                                     jnp.concatenate([zk, kcat], axis=1)], axis=0)
            scores.append(lax.dot_general(k_pad, qg, nt, preferred_element_type=F32))
        if pending is not None:
            project(*pending)
        probs, rdens = [], []
        for kh in range(N_KV_HEADS):
            for pos in range(2):
                pr, rd = [], []
                for half in range(2):
                    h = kh * GROUP + 2 * half + pos
                    slope = 2.0 ** (-8.0 * (h + 1) / N_HEADS)
                    qcols = slice(half * WINDOW, (half + 1) * WINDOW)
                    krow = pos * 2 * WINDOW
                    st = scores[kh]
                    s = (jnp.where(own, st[krow + WINDOW:krow + 2 * WINDOW, qcols],
                                   st[krow:krow + WINDOW, qcols]) * c_scale
                         - (slope * log2e) * dist)
                    if sb == 0:
                        s = jnp.where(visible0, s, -jnp.inf)
                    m = jnp.max(s, axis=0, keepdims=True)
                    p = jnp.exp2(s - m)
                    den = jnp.sum(p, axis=0, keepdims=True) + jnp.exp2(sink_ref[h] * log2e - m)
                    rd.append(1.0 / den)
                    pr.append(jnp.concatenate(
                        [jnp.where(own, 0.0, p).astype(BF16), jnp.where(own, p, 0.0).astype(BF16)],
                        axis=0))
                probs.append(jnp.concatenate(pr, axis=1))
                rdens.append(jnp.concatenate(rd, axis=1))
        out_t = [None] * N_HEADS
        for kh in range(N_KV_HEADS):
            for pos in range(2):
                o2 = jnp.dot(vcats[kh], probs[2 * kh + pos], preferred_element_type=F32)
                o2 = o2 * rdens[2 * kh + pos]
                out_t[kh * GROUP + pos] = o2[:, :WINDOW]
                out_t[kh * GROUP + 2 + pos] = o2[:, WINDOW:]
        pending = (rows, jnp.transpose(jnp.concatenate(out_t, axis=0)).astype(BF16))
        prev_k, prev_vt = cur_k, cur_vt
    project(*pending)


def _attention(proj, za, sinks, wb, seq, chunk, t):
    sub = TQ_ATTN // WINDOW
    n_tiles = t // TQ_ATTN
    first = chunk * n_tiles
    src = lambda w, c: pl.BlockSpec((TQ_ATTN, w), lambda i: (first + i, c))
    prev = lambda c: pl.BlockSpec((WINDOW, KV_WIDTH),
                                  lambda i: (jnp.maximum((first + i) * sub - 1, 0), c))
    return pl.pallas_call(
        functools.partial(_attn_kernel, first_tile_index=first, tiles_per_seq=seq // TQ_ATTN),
        name="swattn",
        grid=(n_tiles,),
        in_specs=[
            pl.BlockSpec(memory_space=pltpu.SMEM),
            src(D_MODEL, COL_Q), src(KV_WIDTH, COL_K), src(KV_WIDTH, COL_V), prev(COL_K), prev(COL_V),
            src(D_MODEL, COL_GB), src(D_MODEL, 0),
            pl.BlockSpec((D_MODEL, D_MODEL), lambda i: (0, 0), pipeline_mode=pl.Buffered(1)),
        ],
        out_specs=pl.BlockSpec((TQ_ATTN, D_MODEL), lambda i: (i, 0)),
        out_shape=jax.ShapeDtypeStruct((t, D_MODEL), BF16),
        compiler_params=pltpu.CompilerParams(
            dimension_semantics=("arbitrary",), vmem_limit_bytes=VMEM_LIMIT),
    )(sinks, proj, proj, proj, proj, proj, proj, za, wb)


def _mix_kernel(merged_ref, x_hbm, wo_ref, g_ref, wr_ref, br_ref,
                xmid_ref, h_ref, route_ref, cnt_ref, x_ring, x_sem, *, first_tile, n_tiles):
    i = pl.program_id(0)

    def x_copy(tile):
        rows = pl.ds(pl.multiple_of((first_tile + tile) * TM_MIX, TM_MIX), TM_MIX)
        return pltpu.make_async_copy(x_hbm.at[rows, :], x_ring.at[tile % X_RING],
                                     x_sem.at[tile % X_RING])

    @pl.when(i == 0)
    def _():
        for ahead in range(min(X_RING - 1, n_tiles)):
            x_copy(ahead).start()

    @pl.when(i + X_RING - 1 < n_tiles)
    def _():
        x_copy(i + X_RING - 1).start()

    x_copy(i).wait()
    x_ref = x_ring.at[i % X_RING]
    subs = [slice(s * SUB_MIX, (s + 1) * SUB_MIX) for s in range(TM_MIX // SUB_MIX)]
    xm = [x_ref[r, :] + jnp.dot(merged_ref[r, :], wo_ref[...], preferred_element_type=F32)
          for r in subs]
    hs = []
    for r, v in zip(subs, xm):
        xmid_ref[r, :] = _pack_bf16_pairs(v)
        h = _rms(v, g_ref[...])
        h_ref[r, :] = _pack_bf16_pairs(h)
        hs.append(h)
    wr = wr_ref[...]
    logits = []
    for h in hs:
        h_hi = h.astype(BF16)
        h_lo = (h - h_hi.astype(F32)).astype(BF16)
        both = jnp.dot(h_hi, wr, preferred_element_type=F32)
        logits.append(both[:, :LANES] + both[:, LANES:]
                      + jnp.dot(h_lo, wr[:, :LANES], preferred_element_type=F32) + br_ref[...])
    subs_per_cnt = TM_CNT // SUB_MIX
    cnts = [jnp.zeros((8, LANES), F32) for _ in range(TM_MIX // TM_CNT)]
    for s, (r, lg) in enumerate(zip(subs, logits)):
        route, cnt = _route(lg)
        route_ref[r, :] = route
        cnts[s // subs_per_cnt] = cnts[s // subs_per_cnt] + cnt
    for c, cnt in enumerate(cnts):
        cnt_ref[c * 8:(c + 1) * 8, :] = cnt


def _route(logits):
    n = logits.shape[0]
    lt = jnp.transpose(logits)
    sub = lax.broadcasted_iota(I32, (EXPERTS_PER_GROUP, n), 0)
    neg = -jnp.inf
    gl = jnp.where(sub < N_GROUPS, lt[N_EXPERTS:N_EXPERTS + EXPERTS_PER_GROUP], neg)
    gmax = jnp.max(gl, axis=0, keepdims=True)
    g_idx = jnp.min(jnp.where(gl == gmax, sub, EXPERTS_PER_GROUP), axis=0, keepdims=True)
    p_g = 1.0 / jnp.sum(jnp.exp(gl - gmax), axis=0, keepdims=True)
    v1 = v2 = i1 = i2 = None
    for g in range(N_GROUPS):
        eg = lt[g * EXPERTS_PER_GROUP:(g + 1) * EXPERTS_PER_GROUP]
        a1 = jnp.max(eg, axis=0, keepdims=True)
        j1 = jnp.min(jnp.where(eg == a1, sub, EXPERTS_PER_GROUP), axis=0, keepdims=True)
        eg2 = jnp.where(sub == j1, neg, eg)
        a2 = jnp.max(eg2, axis=0, keepdims=True)
        j2 = jnp.min(jnp.where(eg2 == a2, sub, EXPERTS_PER_GROUP), axis=0, keepdims=True)
        if g == 0:
            v1, v2, i1, i2 = a1, a2, j1, j2
        else:
            chosen = g_idx == g
            v1, v2 = jnp.where(chosen, a1, v1), jnp.where(chosen, a2, v2)
            i1, i2 = jnp.where(chosen, j1, i1), jnp.where(chosen, j2, i2)
    e21 = jnp.exp(v2 - v1)
    w1 = p_g / (1.0 + e21)
    w2 = p_g * e21 / (1.0 + e21)
    e1 = g_idx * EXPERTS_PER_GROUP + i1
    e2 = g_idx * EXPERTS_PER_GROUP + i2
    rows8 = jnp.where(sub == 0, e1.astype(F32),
                      jnp.where(sub == 1, e2.astype(F32),
                                jnp.where(sub == 2, w1, jnp.where(sub == 3, w2, 0.0))))
    route_t = jnp.concatenate([rows8, jnp.zeros((LANES - EXPERTS_PER_GROUP, n), F32)], axis=0)
    expert_row = lax.broadcasted_iota(I32, (LANES, n), 0)
    onehot_t = ((expert_row == e1) | (expert_row == e2)).astype(BF16)
    cnt = lax.dot_general(jnp.ones((8, n), BF16), onehot_t, (((1,), (1,)), ((), ())),
                          preferred_element_type=F32)
    return jnp.transpose(route_t), cnt


def _mix(merged, xf, wo, g, wr, br, chunk, t):
    n_tiles = t // TM_MIX
    first = chunk * n_tiles
    cnt_rows = TM_MIX // TM_CNT * 8
    full = lambda shape: pl.BlockSpec(shape, lambda i: (0, 0), pipeline_mode=pl.Buffered(1))
    tile = lambda w=D_MODEL: pl.BlockSpec((TM_MIX, w), lambda i: (i, 0))
    return pl.pallas_call(
        functools.partial(_mix_kernel, first_tile=first, n_tiles=n_tiles),
        name="merge_router",
        grid=(n_tiles,),
        in_specs=[
            tile(), pl.BlockSpec(memory_space=pl.ANY),
            full((D_MODEL, D_MODEL)), full((1, D_MODEL)),
            full((D_MODEL, 2 * LANES)), full((1, LANES)),
        ],
        out_specs=[tile(PACKED), tile(PACKED), tile(LANES),
                   pl.BlockSpec((cnt_rows, LANES), lambda i: (i, 0))],
        out_shape=[
            jax.ShapeDtypeStruct((t, PACKED), I32),
            jax.ShapeDtypeStruct((t, PACKED), I32),
            jax.ShapeDtypeStruct((t, LANES), F32),
            jax.ShapeDtypeStruct((n_tiles * cnt_rows, LANES), F32),
        ],
        scratch_shapes=[pltpu.VMEM((X_RING, TM_MIX, D_MODEL), F32), pltpu.SemaphoreType.DMA((X_RING,))],
        compiler_params=pltpu.CompilerParams(
            dimension_semantics=("arbitrary",), vmem_limit_bytes=VMEM_LIMIT),
    )(merged, xf, wo, g, wr, br)


def _pos_kernel(route_ref, base_ref, pos_ref):
    lane = lax.broadcasted_iota(I32, (TM_CNT, LANES), 1)
    r = lax.broadcasted_iota(I32, (TM_CNT, TM_CNT), 0)
    c = lax.broadcasted_iota(I32, (TM_CNT, TM_CNT), 1)
    lower = (c < r).astype(BF16)
    subs = [slice(s * TM_CNT, (s + 1) * TM_CNT) for s in range(POS_TILES)]
    routes = [route_ref[rs, :] for rs in subs]
    e1 = [jnp.sum(jnp.where(lane == 0, rt, 0.0), axis=-1, keepdims=True).astype(I32) for rt in routes]
    e2 = [jnp.sum(jnp.where(lane == 1, rt, 0.0), axis=-1, keepdims=True).astype(I32) for rt in routes]
    onehot = [((lane == a) | (lane == b)).astype(BF16) for a, b in zip(e1, e2)]
    before = [jnp.dot(lower, oh, preferred_element_type=F32) + base_ref[s]
              for s, oh in enumerate(onehot)]
    for s, rs in enumerate(subs):
        p1 = jnp.sum(jnp.where(lane == e1[s], before[s], 0.0), axis=-1, keepdims=True)
        p2 = jnp.sum(jnp.where(lane == e2[s], before[s], 0.0), axis=-1, keepdims=True)
        packed = jnp.where(lane == 0, p1, jnp.where(lane == 1, p2, 0.0))
        pos_ref[:, rs] = jnp.transpose(packed)[0:TOP_K, :].astype(I32)


def _positions(route, base):
    t = route.shape[0]
    n_steps = t // (TM_CNT * POS_TILES)
    return pl.pallas_call(
        _pos_kernel,
        name="positions",
        grid=(n_steps,),
        in_specs=[
            pl.BlockSpec((TM_CNT * POS_TILES, LANES), lambda i: (i, 0)),
            pl.BlockSpec((POS_TILES, 1, LANES), lambda i: (i, 0, 0)),
        ],
        out_specs=pl.BlockSpec((TOP_K, TM_CNT * POS_TILES), lambda i: (0, i)),
        out_shape=jax.ShapeDtypeStruct((TOP_K, t), I32),
        compiler_params=pltpu.CompilerParams(dimension_semantics=("arbitrary",)),
    )(route, base)


def _sc_mesh():
    return plsc.VectorSubcoreMesh(core_axis_name="c", subcore_axis_name="s",
                                  num_cores=SC_CORES, num_subcores=SC_SUBCORES)


def _sc_worker():
    return lax.axis_index("s") * SC_CORES + lax.axis_index("c")


def _dispatch(pos, hp, rows):
    t = hp.shape[0]
    per_w = t // SC_WORKERS
    n_ch = per_w // SC_CHUNK
    pos4 = pos.reshape(TOP_K, SC_WORKERS, n_ch, SC_CHUNK)

    @functools.partial(
        pl.kernel, mesh=_sc_mesh(),
        out_type=jax.ShapeDtypeStruct((rows, PACKED), I32),
        scratch_types=[pltpu.VMEM((TOP_K, n_ch, SC_CHUNK), I32),
                       pltpu.VMEM((SC_CHUNK, PACKED), I32)])
    def scatter(hp_hbm, pos_hbm, xs_hbm, idx_v, rows_v):
        wid = _sc_worker()
        for k in range(TOP_K):
            pltpu.sync_copy(pos_hbm.at[k, wid], idx_v.at[k])

        def body(c, carry):
            start = pl.multiple_of(wid * per_w + c * SC_CHUNK, SC_CHUNK)
            pltpu.sync_copy(hp_hbm.at[pl.ds(start, SC_CHUNK)], rows_v)
            for k in range(TOP_K):
                pltpu.sync_copy(rows_v, xs_hbm.at[idx_v.at[k, c]])
            return carry

        lax.fori_loop(0, n_ch, body, 0)

    return scatter(hp, pos4)


def _gather_rows(table, idx):
    n = idx.shape[0]
    per_w = n // SC_WORKERS
    n_ch = per_w // SC_CHUNK
    idx3 = idx.reshape(SC_WORKERS, n_ch, SC_CHUNK)

    @functools.partial(
        pl.kernel, mesh=_sc_mesh(),
        out_type=jax.ShapeDtypeStruct((n, PACKED), I32),
        scratch_types=[pltpu.VMEM((n_ch, SC_CHUNK), I32),
                       pltpu.VMEM((SC_CHUNK, PACKED), I32)])
    def gather(table_hbm, idx_hbm, out_hbm, idx_v, rows_v):
        wid = _sc_worker()
        pltpu.sync_copy(idx_hbm.at[wid], idx_v)

        def body(c, carry):
            start = pl.multiple_of(wid * per_w + c * SC_CHUNK, SC_CHUNK)
            pltpu.sync_copy(table_hbm.at[idx_v.at[c]], rows_v)
            pltpu.sync_copy(rows_v, out_hbm.at[pl.ds(start, SC_CHUNK)])
            return carry

        lax.fori_loop(0, n_ch, body, 0)

    return gather(table, idx3)


def _expert_kernel(te_ref, nx_ref, sl_ref, ts_ref, tv_ref, xs_ref, wg_hbm, wu_hbm, wd_hbm, o_ref,
                   wg_s, wu_s, wd_s, sem):
    del ts_ref
    i = pl.program_id(0)
    n_valid = tv_ref[i]
    expert = te_ref[i]
    slot = sl_ref[i]

    def weight_copies(e, s):
        return (pltpu.make_async_copy(wg_hbm.at[e], wg_s.at[s], sem.at[s, 0]),
                pltpu.make_async_copy(wu_hbm.at[e], wu_s.at[s], sem.at[s, 1]),
                pltpu.make_async_copy(wd_hbm.at[e], wd_s.at[s], sem.at[s, 2]))

    @pl.when(i == 0)
    def _():
        for copy in weight_copies(expert, slot):
            copy.start()

    @pl.when(jnp.logical_or(i == 0, expert != te_ref[jnp.maximum(i - 1, 0)]))
    def _():
        for copy in weight_copies(expert, slot):
            copy.wait()
        nxt = nx_ref[i]

        @pl.when(nxt >= 0)
        def _():
            for copy in weight_copies(nxt, 1 - slot):
                copy.start()

    def mlp(n_sub):
        wg, wu, wd = wg_s.at[slot], wu_s.at[slot], wd_s.at[slot]
        subs = [slice(s * SUB_EXP, (s + 1) * SUB_EXP) for s in range(n_sub)]
        xin = []
        for r in subs:
            rid = r.start + lax.broadcasted_iota(I32, (SUB_EXP, PACKED), 0)
            lo, hi = _unpack_bf16_pairs(jnp.where(rid < n_valid, xs_ref[r, :], 0))
            xin.append((lo.astype(BF16), hi.astype(BF16)))
        ab = [(jnp.dot(lo, wg[:PACKED, :], preferred_element_type=F32)
               + jnp.dot(hi, wg[PACKED:, :], preferred_element_type=F32),
               jnp.dot(lo, wu[:PACKED, :], preferred_element_type=F32)
               + jnp.dot(hi, wu[PACKED:, :], preferred_element_type=F32)) for lo, hi in xin]
        for r, (ai, bi) in zip(subs, ab):
            hm = (ai * jax.nn.sigmoid(ai) * bi).astype(BF16)
            o_ref[r, :] = _pack_bf16_pairs(jnp.dot(hm, wd[...], preferred_element_type=F32))
        if n_sub * SUB_EXP < TM_EXP:
            o_ref[n_sub * SUB_EXP:, :] = jnp.zeros((TM_EXP - n_sub * SUB_EXP, PACKED), I32)

    n_subs = TM_EXP // SUB_EXP
    for n_sub in range(n_subs + 1):
        lo_rows = (n_sub - 1) * SUB_EXP if n_sub else -1
        in_range = jnp.logical_and(n_valid > lo_rows, n_valid <= n_sub * SUB_EXP)
        pl.when(in_range)(functools.partial(mlp, n_sub))


def _experts(tile_expert, next_expert, tile_slot, tile_src, tile_valid, xs, wg, wu, wd):
    rows = xs.shape[0]
    hbm = pl.BlockSpec(memory_space=pl.ANY)
    grid_spec = pltpu.PrefetchScalarGridSpec(
        num_scalar_prefetch=5,
        grid=(rows // TM_EXP,),
        in_specs=[pl.BlockSpec((TM_EXP, PACKED), lambda i, te, nx, sl, ts, tv: (ts[i], 0)),
                  hbm, hbm, hbm],
        out_specs=pl.BlockSpec((TM_EXP, PACKED), lambda i, te, nx, sl, ts, tv: (i, 0)),
        scratch_shapes=[
            pltpu.VMEM((2, D_MODEL, D_FF), BF16), pltpu.VMEM((2, D_MODEL, D_FF), BF16),
            pltpu.VMEM((2, D_FF, D_MODEL), BF16),
            pltpu.SemaphoreType.DMA((2, 3)),
        ],
    )
    return pl.pallas_call(
        _expert_kernel,
        name="experts",
        grid_spec=grid_spec,
        out_shape=jax.ShapeDtypeStruct((rows, PACKED), I32),
        compiler_params=pltpu.CompilerParams(
            dimension_semantics=("arbitrary",), vmem_limit_bytes=VMEM_LIMIT),
    )(tile_expert, next_expert, tile_slot, tile_src, tile_valid, xs, wg, wu, wd)


def _combine_kernel(y1_ref, y2_ref, route_ref, xmid_ref, g_ref, *rest):
    o_ref = rest[-1]
    route = route_ref[...]
    lane = lax.broadcasted_iota(I32, route.shape, 1)
    w1 = jnp.sum(jnp.where(lane == 2, route, 0.0), axis=-1, keepdims=True)
    w2 = jnp.sum(jnp.where(lane == 3, route, 0.0), axis=-1, keepdims=True)
    lo1, hi1 = _unpack_bf16_pairs(y1_ref[...])
    lo2, hi2 = _unpack_bf16_pairs(y2_ref[...])
    lox, hix = _unpack_bf16_pairs(xmid_ref[...])
    x_out = jnp.concatenate([lox + (lo1 * w1 + lo2 * w2), hix + (hi1 * w1 + hi2 * w2)], axis=1)
    o_ref[...] = _rms(x_out, g_ref[...])


def _combine(yg, route, xmid, g, chunk, t_total, out_prev):
    t = xmid.shape[0]
    n_tiles = t // TM_CMB
    first = chunk * n_tiles
    in_specs = [
        pl.BlockSpec((TM_CMB, PACKED), lambda i: (i, 0)),
        pl.BlockSpec((TM_CMB, PACKED), lambda i: (n_tiles + i, 0)),
        pl.BlockSpec((TM_CMB, LANES), lambda i: (i, 0)),
        pl.BlockSpec((TM_CMB, PACKED), lambda i: (i, 0)),
        pl.BlockSpec((1, D_MODEL), lambda i: (0, 0)),
    ]
    args = [yg, yg, route, xmid, g]
    aliases = {}
    if out_prev is not None:
        in_specs.append(pl.BlockSpec(memory_space=pl.ANY))
        aliases = {len(args): 0}
        args.append(out_prev)
    return pl.pallas_call(
        _combine_kernel,
        name="combine",
        grid=(n_tiles,),
        in_specs=in_specs,
        out_specs=pl.BlockSpec((TM_CMB, D_MODEL), lambda i: (first + i, 0)),
        out_shape=jax.ShapeDtypeStruct((t_total, D_MODEL), F32),
        input_output_aliases=aliases,
        compiler_params=pltpu.CompilerParams(
            dimension_semantics=("arbitrary",), vmem_limit_bytes=VMEM_LIMIT),
    )(*args)


def _split_bf16(w):
    hi = w.astype(BF16)
    lo = (w - hi.astype(F32)).astype(BF16)
    return hi, lo


def kernel(x, norm_mix, w_in, conv_w, conv_b, w_a_out, sinks, w_b_out, w_o, norm_ffn, w_group,
           b_group, w_expert, b_expert, w_gate, w_up, w_down, norm_final):
    bsz, seq, d = x.shape
    t = bsz * seq
    assert d == D_MODEL and seq % TM_PROJ == 0 and seq % TQ_ATTN == 0
    xf = x.reshape(t, d)
    row = lambda v: v.reshape(1, -1)

    assert w_in.shape == (D_MODEL, IN_COLS)
    za, proj, w_gate, w_up, w_down = _inproj(xf, row(norm_mix), w_in.astype(BF16), conv_w,
                                             row(conv_b), w_a_out.astype(BF16), seq,
                                             w_gate, w_up, w_down)
    pad = LANES - N_GROUPS - N_EXPERTS
    w_r = jnp.concatenate([w_expert, w_group, jnp.zeros((d, pad), F32)], axis=1)
    b_r = jnp.concatenate([b_expert, b_group, jnp.zeros((pad,), F32)]).reshape(1, LANES)
    wr = jnp.concatenate(_split_bf16(w_r), axis=1)
    wb = w_b_out.astype(BF16)
    wo = w_o.astype(BF16)

    t_chunk = t // MOE_CHUNKS
    assert t == t_chunk * MOE_CHUNKS and all(
        t_chunk % step == 0
        for step in (TQ_ATTN, TM_MIX, TM_CNT * POS_TILES, TM_CMB, SC_WORKERS * SC_CHUNK))
    out = None
    for chunk in range(MOE_CHUNKS):
        merged = _attention(proj, za, sinks, wb, seq, chunk, t_chunk)
        xmid, h2, route, cnt = _mix(merged, xf, wo, row(norm_ffn), wr, b_r, chunk, t_chunk)
        out = _moe_chunk(xmid, h2, route, cnt, w_gate, w_up, w_down, row(norm_final), chunk, t, out)
    return out.reshape(bsz, seq, d)


def _moe_chunk(xmid, h2, route, cnt, w_gate, w_up, w_down, g_final, chunk, t_total, out_prev):
    t = xmid.shape[0]
    n_tiles = t // TM_CNT
    cnt = cnt.reshape(n_tiles, 8, LANES)[:, 0, :N_EXPERTS].astype(I32)
    totals = jnp.sum(cnt, axis=0)
    tiles_e = (totals + TM_EXP - 1) // TM_EXP
    tile_end = jnp.cumsum(tiles_e)
    offset = (tile_end - tiles_e) * TM_EXP
    base = offset[None, :] + jnp.cumsum(cnt, axis=0) - cnt
    base = jnp.pad(base, ((0, 0), (0, LANES - N_EXPERTS))).astype(F32).reshape(n_tiles, 1, LANES)
    rows = t * TOP_K + N_EXPERTS * TM_EXP
    n_active = tile_end[-1]
    tile_id = jnp.arange(rows // TM_EXP, dtype=I32)
    tile_src = jnp.minimum(tile_id, n_active - 1)
    tile_expert = jnp.sum((tile_src[:, None] >= tile_end[None, :]).astype(I32), axis=1)
    tile_expert = jnp.minimum(tile_expert, N_EXPERTS - 1)
    row_in_expert = (tile_id - (tile_end - tiles_e)[tile_expert]) * TM_EXP
    tile_valid = jnp.clip(totals[tile_expert] - row_in_expert, 0, TM_EXP)
    tile_valid = jnp.where(tile_id < n_active, tile_valid, 0).astype(I32)
    after = tile_end[tile_expert]
    next_expert = jnp.where(after < n_active, tile_expert[jnp.minimum(after, n_active - 1)], -1)
    first_of_expert = jnp.concatenate(
        [jnp.ones((1,), I32), (tile_expert[1:] != tile_expert[:-1]).astype(I32)])
    tile_slot = (jnp.cumsum(first_of_expert) - 1) % 2

    pos = _positions(route, base)
    xs = _dispatch(pos, h2, rows)
    ys = _experts(tile_expert.astype(I32), next_expert.astype(I32), tile_slot.astype(I32),
                  tile_src.astype(I32), tile_valid, xs, w_gate, w_up, w_down)
    yg = _gather_rows(ys, pos.reshape(TOP_K * t))
    return _combine(yg, route, xmid, g_final, chunk, t_total, out_prev)
```

```python
import functools
import math

import jax
import jax.numpy as jnp
from jax import lax
from jax.experimental import pallas as pl
from jax.experimental.pallas import tpu as pltpu
from jax.experimental.pallas import tpu_sc as plsc

F32 = jnp.float32
BF16 = jnp.bfloat16
I32 = jnp.int32

D_MODEL = 1024
HEAD_DIM = 64
N_HEADS = 16
N_KV_HEADS = 4
GROUP = N_HEADS // N_KV_HEADS
KV_WIDTH = N_KV_HEADS * HEAD_DIM
WINDOW = 128
N_GROUPS = 4
EXPERTS_PER_GROUP = 8
N_EXPERTS = N_GROUPS * EXPERTS_PER_GROUP
TOP_K = 2
D_FF = 512
EPS = 1e-6
LANES = 128

COL_WIDTH = {"b": D_MODEL, "c": D_MODEL, "u": D_MODEL, "q": D_MODEL, "k": KV_WIDTH, "v": KV_WIDTH,
             "ga": D_MODEL, "gb": D_MODEL}
COL_START = dict(zip(COL_WIDTH, (sum(list(COL_WIDTH.values())[:i]) for i in range(len(COL_WIDTH)))))
IN_COLS = sum(COL_WIDTH.values())
REST_COLS = 2 * D_MODEL + 2 * KV_WIDTH
COL_GB, COL_Q = 0, 1
COL_K, COL_V = 2 * D_MODEL // KV_WIDTH, 2 * D_MODEL // KV_WIDTH + 1

TM_PROJ = 512
SUB_PROJ = 256
W_ROWS = 64
TQ_ATTN = 1024
TM_MIX = 1024
SUB_MIX = 256
TM_CNT = 512
POS_TILES = 8
MOE_CHUNKS = 2
TM_EXP = 512
SUB_EXP = 256
TM_CMB = 2048
HALO_ROWS = 8
VMEM_LIMIT = 56 * 1024 * 1024
PACKED = D_MODEL // 2

SC_CORES = 2
SC_SUBCORES = 16
SC_WORKERS = SC_CORES * SC_SUBCORES
SC_CHUNK = 64


def _rms(x, g):
    r = lax.rsqrt(jnp.mean(x * x, axis=-1, keepdims=True) + EPS)
    return (x * r) * g


def _pack_bf16_pairs(x):
    n = x.shape[1] // 2
    lo = lax.bitcast_convert_type(x[:, :n].astype(BF16).astype(F32), I32)
    hi = lax.bitcast_convert_type(x[:, n:].astype(BF16).astype(F32), I32)
    return (hi & jnp.int32(-65536)) | lax.shift_right_logical(lo, 16)


def _unpack_bf16_pairs(p):
    lo = lax.bitcast_convert_type(lax.shift_left(p, 16), F32)
    hi = lax.bitcast_convert_type(p & jnp.int32(-65536), F32)
    return lo, hi


def _inproj_kernel(x_ref, g_ref, w_hbm, cw_ref, cb_ref, wa_ref,
                   eg_ref, eu_ref, ed_ref,
                   za_ref, proj_ref, eg_out, eu_out, ed_out, halo_ref, w_ref, w_stage, w_sem,
                   *, tiles_per_seq):
    i = pl.program_id(0)

    @pl.when(i == 0)
    def _():
        n_copies = D_MODEL // W_ROWS

        def w_copy(c):
            return pltpu.make_async_copy(w_hbm.at[c * W_ROWS:(c + 1) * W_ROWS, :],
                                         w_stage.at[c % 2], w_sem.at[c % 2])

        w_copy(0).start()
        for c in range(n_copies):
            if c + 1 < n_copies:
                w_copy(c + 1).start()
            w_copy(c).wait()
            w_ref[c * W_ROWS:(c + 1) * W_ROWS, :] = w_stage[c % 2].astype(BF16)

    eg_out[...] = eg_ref[...].astype(BF16)
    eu_out[...] = eu_ref[...].astype(BF16)
    ed_out[...] = ed_ref[...].astype(BF16)
    halves = [slice(s * SUB_PROJ, (s + 1) * SUB_PROJ) for s in range(TM_PROJ // SUB_PROJ)]
    hs = [_rms(x_ref[r, :], g_ref[...]).astype(BF16) for r in halves]
    col = lambda a, b=None: slice(COL_START[a], COL_START[b or a] + COL_WIDTH[b or a])
    proj = lambda a, b=None: [jnp.dot(h, w_ref[:, col(a, b)], preferred_element_type=F32) for h in hs]
    pcu = jnp.concatenate(proj("c", "u"), axis=0)
    pb = jnp.concatenate(proj("b"), axis=0)
    pga = jnp.concatenate(proj("ga"), axis=0)
    for r, pgb, pqkv in zip(halves, proj("gb"), proj("q", "v")):
        proj_ref[r, :D_MODEL] = pgb.astype(BF16)
        proj_ref[r, D_MODEL:] = pqkv.astype(BF16)
    cu = pcu[:, :D_MODEL] * pcu[:, D_MODEL:]
    first = (i % tiles_per_seq) == 0
    hist = jnp.where(first, 0.0, halo_ref[...])
    prev1 = hist[HALO_ROWS - 1:HALO_ROWS]
    prev2 = hist[HALO_ROWS - 2:HALO_ROWS - 1]
    halo_ref[...] = cu[TM_PROJ - HALO_ROWS:, :]
    row = lax.broadcasted_iota(I32, cu.shape, 0)
    cu1 = jnp.where(row == 0, prev1, pltpu.roll(cu, 1, 0))
    cu2 = jnp.where(row == 0, prev2, jnp.where(row == 1, prev1, pltpu.roll(cu, 2, 0)))
    cw = cw_ref[...]
    y = cw[0:1] * cu2 + cw[1:2] * cu1 + cw[2:3] * cu + cb_ref[...]
    ya = (pb * y).astype(BF16)
    z = jnp.dot(ya, wa_ref[...], preferred_element_type=F32)
    za_ref[...] = (jax.nn.sigmoid(pga) * z).astype(BF16)


def _inproj(xf, g, w_in, conv_w, conv_b, wa, seq, w_gate, w_up, w_down):
    t = xf.shape[0]
    n_tiles = t // TM_PROJ
    const = lambda shape: pl.BlockSpec(shape, lambda i: (0, 0), pipeline_mode=pl.Buffered(1))
    slabs = [w.reshape(-1, w.shape[-1]) for w in (w_gate, w_up, w_down)]
    slab_rows = [s.shape[0] // n_tiles for s in slabs]
    assert all(s.shape[0] == r * n_tiles and r % 16 == 0 for s, r in zip(slabs, slab_rows))
    slab_specs = [pl.BlockSpec((r, s.shape[1]), lambda i: (i, 0)) for s, r in zip(slabs, slab_rows)]
    outs = pl.pallas_call(
        functools.partial(_inproj_kernel, tiles_per_seq=seq // TM_PROJ),
        name="inproj_conv",
        grid=(n_tiles,),
        in_specs=[
            pl.BlockSpec((TM_PROJ, D_MODEL), lambda i: (i, 0)),
            const((1, D_MODEL)),
            pl.BlockSpec(memory_space=pl.ANY),
            const((3, D_MODEL)), const((1, D_MODEL)), const((D_MODEL, D_MODEL)),
        ] + slab_specs,
        out_specs=[pl.BlockSpec((TM_PROJ, D_MODEL), lambda i: (i, 0)),
                   pl.BlockSpec((TM_PROJ, REST_COLS), lambda i: (i, 0))] + slab_specs,
        out_shape=[jax.ShapeDtypeStruct((t, D_MODEL), BF16),
                   jax.ShapeDtypeStruct((t, REST_COLS), BF16)]
                  + [jax.ShapeDtypeStruct(s.shape, BF16) for s in slabs],
        scratch_shapes=[pltpu.VMEM((HALO_ROWS, D_MODEL), F32),
                        pltpu.VMEM((D_MODEL, IN_COLS), BF16),
                        pltpu.VMEM((2, W_ROWS, IN_COLS), F32),
                        pltpu.SemaphoreType.DMA((2,))],
        compiler_params=pltpu.CompilerParams(
            dimension_semantics=("arbitrary",), vmem_limit_bytes=VMEM_LIMIT),
    )(xf, g, w_in, conv_w, conv_b, wa, *slabs)
    za, proj, eg, eu, ed = outs
    return za, proj, eg.reshape(w_gate.shape), eu.reshape(w_up.shape), ed.reshape(w_down.shape)


def _attn_kernel(sink_ref, q_ref, k_ref, v_ref, kp_ref, vp_ref, gb_ref, za_ref, wb_ref, o_ref,
                 *, first_tile_index, tiles_per_seq):
    first_tile = ((first_tile_index + pl.program_id(0)) % tiles_per_seq) == 0
    ks = lax.broadcasted_iota(I32, (WINDOW, WINDOW), 0)
    qq = lax.broadcasted_iota(I32, (WINDOW, WINDOW), 1)
    own = ks <= qq
    dist = jnp.where(own, qq - ks, qq - ks + WINDOW).astype(F32)
    visible0 = jnp.logical_or(own, jnp.logical_not(first_tile))
    log2e = math.log2(math.e)
    c_scale = log2e / math.sqrt(HEAD_DIM)
    nt = (((1,), (1,)), ((), ()))
    zk = jnp.zeros((2 * WINDOW, HEAD_DIM), BF16)

    def transposed(v_blk):
        return jnp.transpose(v_blk.astype(F32)).astype(BF16)

    def project(rows_p, attn_blk):
        yb = jnp.dot(attn_blk, wb_ref[...], preferred_element_type=F32)
        zb = jax.nn.sigmoid(gb_ref[rows_p, :].astype(F32)) * yb
        o_ref[rows_p, :] = (za_ref[rows_p, :].astype(F32) + zb).astype(BF16)

    pending = None
    prev_k = kp_ref[...]
    prev_vt = transposed(vp_ref[...])
    for sb in range(TQ_ATTN // WINDOW):
        rows = slice(sb * WINDOW, (sb + 1) * WINDOW)
        cur_k = k_ref[rows, :]
        cur_vt = transposed(v_ref[rows, :])
        scores, vcats = [], []
        for kh in range(N_KV_HEADS):
            cols = slice(kh * HEAD_DIM, (kh + 1) * HEAD_DIM)
            kcat = jnp.concatenate([prev_k[:, cols], cur_k[:, cols]], axis=0)
            vcats.append(jnp.concatenate([prev_vt[cols, :], cur_vt[cols, :]], axis=1))
            qg = jnp.concatenate([q_ref[rows, (2 * kh) * LANES:(2 * kh + 1) * LANES],
                                  q_ref[rows, (2 * kh + 1) * LANES:(2 * kh + 2) * LANES]], axis=0)
            k_pad = jnp.concatenate([jnp.concatenate([kcat, zk], axis=1),
                                     jnp.concatenate([zk, kcat], axis=1)], axis=0)
            scores.append(lax.dot_general(k_pad, qg, nt, preferred_element_type=F32))
        if pending is not None:
            project(*pending)
        probs, rdens = [], []
        for kh in range(N_KV_HEADS):
            for pos in range(2):
                pr, rd = [], []
                for half in range(2):
                    h = kh * GROUP + 2 * half + pos
                    slope = 2.0 ** (-8.0 * (h + 1) / N_HEADS)
                    qcols = slice(half * WINDOW, (half + 1) * WINDOW)
                    krow = pos * 2 * WINDOW
                    st = scores[kh]
                    s = (jnp.where(own, st[krow + WINDOW:krow + 2 * WINDOW, qcols],
                                   st[krow:krow + WINDOW, qcols]) * c_scale
                         - (slope * log2e) * dist)
                    if sb == 0:
                        s = jnp.where(visible0, s, -jnp.inf)
                    m = jnp.max(s, axis=0, keepdims=True)
                    p = jnp.exp2(s - m)
                    den = jnp.sum(p, axis=0, keepdims=True) + jnp.exp2(sink_ref[h] * log2e - m)
                    rd.append(1.0 / den)
                    pr.append(jnp.concatenate(
                        [jnp.where(own, 0.0, p).astype(BF16), jnp.where(own, p, 0.0).astype(BF16)],
                        axis=0))
                probs.append(jnp.concatenate(pr, axis=1))
                rdens.append(jnp.concatenate(rd, axis=1))
        out_t = [None] * N_HEADS
        for kh in range(N_KV_HEADS):
            for pos in range(2):
                o2 = jnp.dot(vcats[kh], probs[2 * kh + pos], preferred_element_type=F32)
                o2 = o2 * rdens[2 * kh + pos]
                out_t[kh * GROUP + pos] = o2[:, :WINDOW]
                out_t[kh * GROUP + 2 + pos] = o2[:, WINDOW:]
        pending = (rows, jnp.transpose(jnp.concatenate(out_t, axis=0)).astype(BF16))
        prev_k, prev_vt = cur_k, cur_vt
    project(*pending)


def _attention(proj, za, sinks, wb, seq, chunk, t):
    sub = TQ_ATTN // WINDOW
    n_tiles = t // TQ_ATTN
    first = chunk * n_tiles
    src = lambda w, c: pl.BlockSpec((TQ_ATTN, w), lambda i: (first + i, c))
    prev = lambda c: pl.BlockSpec((WINDOW, KV_WIDTH),
                                  lambda i: (jnp.maximum((first + i) * sub - 1, 0), c))
    return pl.pallas_call(
        functools.partial(_attn_kernel, first_tile_index=first, tiles_per_seq=seq // TQ_ATTN),
        name="swattn",
        grid=(n_tiles,),
        in_specs=[
            pl.BlockSpec(memory_space=pltpu.SMEM),
            src(D_MODEL, COL_Q), src(KV_WIDTH, COL_K), src(KV_WIDTH, COL_V), prev(COL_K), prev(COL_V),
            src(D_MODEL, COL_GB), src(D_MODEL, 0),
            pl.BlockSpec((D_MODEL, D_MODEL), lambda i: (0, 0), pipeline_mode=pl.Buffered(1)),
        ],
        out_specs=pl.BlockSpec((TQ_ATTN, D_MODEL), lambda i: (i, 0)),
        out_shape=jax.ShapeDtypeStruct((t, D_MODEL), BF16),
        compiler_params=pltpu.CompilerParams(
            dimension_semantics=("arbitrary",), vmem_limit_bytes=VMEM_LIMIT),
    )(sinks, proj, proj, proj, proj, proj, proj, za, wb)


def _mix_kernel(merged_ref, x_ref, wo_ref, g_ref, wr_ref, br_ref,
                xmid_ref, h_ref, route_ref, cnt_ref):
    subs = [slice(s * SUB_MIX, (s + 1) * SUB_MIX) for s in range(TM_MIX // SUB_MIX)]
    xm = [x_ref[r, :] + jnp.dot(merged_ref[r, :], wo_ref[...], preferred_element_type=F32)
          for r in subs]
    hs = []
    for r, v in zip(subs, xm):
        xmid_ref[r, :] = _pack_bf16_pairs(v)
        h = _rms(v, g_ref[...])
        h_ref[r, :] = _pack_bf16_pairs(h)
        hs.append(h)
    wr = wr_ref[...]
    logits = []
    for h in hs:
        h_hi = h.astype(BF16)
        h_lo = (h - h_hi.astype(F32)).astype(BF16)
        both = jnp.dot(h_hi, wr, preferred_element_type=F32)
        logits.append(both[:, :LANES] + both[:, LANES:]
                      + jnp.dot(h_lo, wr[:, :LANES], preferred_element_type=F32) + br_ref[...])
    subs_per_cnt = TM_CNT // SUB_MIX
    cnts = [jnp.zeros((8, LANES), F32) for _ in range(TM_MIX // TM_CNT)]
    for s, (r, lg) in enumerate(zip(subs, logits)):
        route, cnt = _route(lg)
        route_ref[r, :] = route
        cnts[s // subs_per_cnt] = cnts[s // subs_per_cnt] + cnt
    for c, cnt in enumerate(cnts):
        cnt_ref[c * 8:(c + 1) * 8, :] = cnt


def _route(logits):
    n = logits.shape[0]
    lt = jnp.transpose(logits)
    sub = lax.broadcasted_iota(I32, (EXPERTS_PER_GROUP, n), 0)
    neg = -jnp.inf
    gl = jnp.where(sub < N_GROUPS, lt[N_EXPERTS:N_EXPERTS + EXPERTS_PER_GROUP], neg)
    gmax = jnp.max(gl, axis=0, keepdims=True)
    g_idx = jnp.min(jnp.where(gl == gmax, sub, EXPERTS_PER_GROUP), axis=0, keepdims=True)
    p_g = 1.0 / jnp.sum(jnp.exp(gl - gmax), axis=0, keepdims=True)
    v1 = v2 = i1 = i2 = None
    for g in range(N_GROUPS):
        eg = lt[g * EXPERTS_PER_GROUP:(g + 1) * EXPERTS_PER_GROUP]
        a1 = jnp.max(eg, axis=0, keepdims=True)
        j1 = jnp.min(jnp.where(eg == a1, sub, EXPERTS_PER_GROUP), axis=0, keepdims=True)
        eg2 = jnp.where(sub == j1, neg, eg)
        a2 = jnp.max(eg2, axis=0, keepdims=True)
        j2 = jnp.min(jnp.where(eg2 == a2, sub, EXPERTS_PER_GROUP), axis=0, keepdims=True)
        if g == 0:
            v1, v2, i1, i2 = a1, a2, j1, j2
        else:
            chosen = g_idx == g
            v1, v2 = jnp.where(chosen, a1, v1), jnp.where(chosen, a2, v2)
            i1, i2 = jnp.where(chosen, j1, i1), jnp.where(chosen, j2, i2)
    e21 = jnp.exp(v2 - v1)
    w1 = p_g / (1.0 + e21)
    w2 = p_g * e21 / (1.0 + e21)
    e1 = g_idx * EXPERTS_PER_GROUP + i1
    e2 = g_idx * EXPERTS_PER_GROUP + i2
    rows8 = jnp.where(sub == 0, e1.astype(F32),
                      jnp.where(sub == 1, e2.astype(F32),
                                jnp.where(sub == 2, w1, jnp.where(sub == 3, w2, 0.0))))
    route_t = jnp.concatenate([rows8, jnp.zeros((LANES - EXPERTS_PER_GROUP, n), F32)], axis=0)
    expert_row = lax.broadcasted_iota(I32, (LANES, n), 0)
    onehot_t = ((expert_row == e1) | (expert_row == e2)).astype(BF16)
    cnt = lax.dot_general(jnp.ones((8, n), BF16), onehot_t, (((1,), (1,)), ((), ())),
                          preferred_element_type=F32)
    return jnp.transpose(route_t), cnt


def _mix(merged, xf, wo, g, wr, br, chunk, t):
    n_tiles = t // TM_MIX
    first = chunk * n_tiles
    cnt_rows = TM_MIX // TM_CNT * 8
    full = lambda shape: pl.BlockSpec(shape, lambda i: (0, 0), pipeline_mode=pl.Buffered(1))
    tile = lambda w=D_MODEL: pl.BlockSpec((TM_MIX, w), lambda i: (i, 0))
    return pl.pallas_call(
        _mix_kernel,
        name="merge_router",
        grid=(n_tiles,),
        in_specs=[
            tile(), pl.BlockSpec((TM_MIX, D_MODEL), lambda i: (first + i, 0)),
            full((D_MODEL, D_MODEL)), full((1, D_MODEL)),
            full((D_MODEL, 2 * LANES)), full((1, LANES)),
        ],
        out_specs=[tile(PACKED), tile(PACKED), tile(LANES),
                   pl.BlockSpec((cnt_rows, LANES), lambda i: (i, 0))],
        out_shape=[
            jax.ShapeDtypeStruct((t, PACKED), I32),
            jax.ShapeDtypeStruct((t, PACKED), I32),
            jax.ShapeDtypeStruct((t, LANES), F32),
            jax.ShapeDtypeStruct((n_tiles * cnt_rows, LANES), F32),
        ],
        compiler_params=pltpu.CompilerParams(
            dimension_semantics=("arbitrary",), vmem_limit_bytes=VMEM_LIMIT),
    )(merged, xf, wo, g, wr, br)


def _pos_kernel(route_ref, base_ref, pos_ref):
    lane = lax.broadcasted_iota(I32, (TM_CNT, LANES), 1)
    r = lax.broadcasted_iota(I32, (TM_CNT, TM_CNT), 0)
    c = lax.broadcasted_iota(I32, (TM_CNT, TM_CNT), 1)
    lower = (c < r).astype(BF16)
    subs = [slice(s * TM_CNT, (s + 1) * TM_CNT) for s in range(POS_TILES)]
    routes = [route_ref[rs, :] for rs in subs]
    e1 = [jnp.sum(jnp.where(lane == 0, rt, 0.0), axis=-1, keepdims=True).astype(I32) for rt in routes]
    e2 = [jnp.sum(jnp.where(lane == 1, rt, 0.0), axis=-1, keepdims=True).astype(I32) for rt in routes]
    onehot = [((lane == a) | (lane == b)).astype(BF16) for a, b in zip(e1, e2)]
    before = [jnp.dot(lower, oh, preferred_element_type=F32) + base_ref[s]
              for s, oh in enumerate(onehot)]
    for s, rs in enumerate(subs):
        p1 = jnp.sum(jnp.where(lane == e1[s], before[s], 0.0), axis=-1, keepdims=True)
        p2 = jnp.sum(jnp.where(lane == e2[s], before[s], 0.0), axis=-1, keepdims=True)
        packed = jnp.where(lane == 0, p1, jnp.where(lane == 1, p2, 0.0))
        pos_ref[:, rs] = jnp.transpose(packed)[0:TOP_K, :].astype(I32)


def _positions(route, base):
    t = route.shape[0]
    n_steps = t // (TM_CNT * POS_TILES)
    return pl.pallas_call(
        _pos_kernel,
        name="positions",
        grid=(n_steps,),
        in_specs=[
            pl.BlockSpec((TM_CNT * POS_TILES, LANES), lambda i: (i, 0)),
            pl.BlockSpec((POS_TILES, 1, LANES), lambda i: (i, 0, 0)),
        ],
        out_specs=pl.BlockSpec((TOP_K, TM_CNT * POS_TILES), lambda i: (0, i)),
        out_shape=jax.ShapeDtypeStruct((TOP_K, t), I32),
        compiler_params=pltpu.CompilerParams(dimension_semantics=("arbitrary",)),
    )(route, base)


def _sc_mesh():
    return plsc.VectorSubcoreMesh(core_axis_name="c", subcore_axis_name="s",
                                  num_cores=SC_CORES, num_subcores=SC_SUBCORES)


def _sc_worker():
    return lax.axis_index("s") * SC_CORES + lax.axis_index("c")


def _dispatch(pos, hp, rows):
    t = hp.shape[0]
    per_w = t // SC_WORKERS
    n_ch = per_w // SC_CHUNK
    pos4 = pos.reshape(TOP_K, SC_WORKERS, n_ch, SC_CHUNK)

    @functools.partial(
        pl.kernel, mesh=_sc_mesh(),
        out_type=jax.ShapeDtypeStruct((rows, PACKED), I32),
        scratch_types=[pltpu.VMEM((TOP_K, n_ch, SC_CHUNK), I32),
                       pltpu.VMEM((SC_CHUNK, PACKED), I32)])
    def scatter(hp_hbm, pos_hbm, xs_hbm, idx_v, rows_v):
        wid = _sc_worker()
        for k in range(TOP_K):
            pltpu.sync_copy(pos_hbm.at[k, wid], idx_v.at[k])

        def body(c, carry):
            start = pl.multiple_of(wid * per_w + c * SC_CHUNK, SC_CHUNK)
            pltpu.sync_copy(hp_hbm.at[pl.ds(start, SC_CHUNK)], rows_v)
            for k in range(TOP_K):
                pltpu.sync_copy(rows_v, xs_hbm.at[idx_v.at[k, c]])
            return carry

        lax.fori_loop(0, n_ch, body, 0)

    return scatter(hp, pos4)


def _gather_rows(table, idx):
    n = idx.shape[0]
    per_w = n // SC_WORKERS
    n_ch = per_w // SC_CHUNK
    idx3 = idx.reshape(SC_WORKERS, n_ch, SC_CHUNK)

    @functools.partial(
        pl.kernel, mesh=_sc_mesh(),
        out_type=jax.ShapeDtypeStruct((n, PACKED), I32),
        scratch_types=[pltpu.VMEM((n_ch, SC_CHUNK), I32),
                       pltpu.VMEM((SC_CHUNK, PACKED), I32)])
    def gather(table_hbm, idx_hbm, out_hbm, idx_v, rows_v):
        wid = _sc_worker()
        pltpu.sync_copy(idx_hbm.at[wid], idx_v)

        def body(c, carry):
            start = pl.multiple_of(wid * per_w + c * SC_CHUNK, SC_CHUNK)
            pltpu.sync_copy(table_hbm.at[idx_v.at[c]], rows_v)
            pltpu.sync_copy(rows_v, out_hbm.at[pl.ds(start, SC_CHUNK)])
            return carry

        lax.fori_loop(0, n_ch, body, 0)

    return gather(table, idx3)


def _expert_kernel(te_ref, nx_ref, sl_ref, ts_ref, tv_ref, xs_ref, wg_hbm, wu_hbm, wd_hbm, o_ref,
                   wg_s, wu_s, wd_s, sem):
    del ts_ref
    i = pl.program_id(0)
    n_valid = tv_ref[i]
    expert = te_ref[i]
    slot = sl_ref[i]

    def weight_copies(e, s):
        return (pltpu.make_async_copy(wg_hbm.at[e], wg_s.at[s], sem.at[s, 0]),
                pltpu.make_async_copy(wu_hbm.at[e], wu_s.at[s], sem.at[s, 1]),
                pltpu.make_async_copy(wd_hbm.at[e], wd_s.at[s], sem.at[s, 2]))

    @pl.when(i == 0)
    def _():
        for copy in weight_copies(expert, slot):
            copy.start()

    @pl.when(jnp.logical_or(i == 0, expert != te_ref[jnp.maximum(i - 1, 0)]))
    def _():
        for copy in weight_copies(expert, slot):
            copy.wait()
        nxt = nx_ref[i]

        @pl.when(nxt >= 0)
        def _():
            for copy in weight_copies(nxt, 1 - slot):
                copy.start()

    def mlp(n_sub):
        wg, wu, wd = wg_s.at[slot], wu_s.at[slot], wd_s.at[slot]
        subs = [slice(s * SUB_EXP, (s + 1) * SUB_EXP) for s in range(n_sub)]
        xin = []
        for r in subs:
            rid = r.start + lax.broadcasted_iota(I32, (SUB_EXP, PACKED), 0)
            lo, hi = _unpack_bf16_pairs(jnp.where(rid < n_valid, xs_ref[r, :], 0))
            xin.append((lo.astype(BF16), hi.astype(BF16)))
        ab = [(jnp.dot(lo, wg[:PACKED, :], preferred_element_type=F32)
               + jnp.dot(hi, wg[PACKED:, :], preferred_element_type=F32),
               jnp.dot(lo, wu[:PACKED, :], preferred_element_type=F32)
               + jnp.dot(hi, wu[PACKED:, :], preferred_element_type=F32)) for lo, hi in xin]
        for r, (ai, bi) in zip(subs, ab):
            hm = (ai * jax.nn.sigmoid(ai) * bi).astype(BF16)
            o_ref[r, :] = _pack_bf16_pairs(jnp.dot(hm, wd[...], preferred_element_type=F32))
        if n_sub * SUB_EXP < TM_EXP:
            o_ref[n_sub * SUB_EXP:, :] = jnp.zeros((TM_EXP - n_sub * SUB_EXP, PACKED), I32)

    n_subs = TM_EXP // SUB_EXP
    for n_sub in range(n_subs + 1):
        lo_rows = (n_sub - 1) * SUB_EXP if n_sub else -1
        in_range = jnp.logical_and(n_valid > lo_rows, n_valid <= n_sub * SUB_EXP)
        pl.when(in_range)(functools.partial(mlp, n_sub))


def _experts(tile_expert, next_expert, tile_slot, tile_src, tile_valid, xs, wg, wu, wd):
    rows = xs.shape[0]
    hbm = pl.BlockSpec(memory_space=pl.ANY)
    grid_spec = pltpu.PrefetchScalarGridSpec(
        num_scalar_prefetch=5,
        grid=(rows // TM_EXP,),
        in_specs=[pl.BlockSpec((TM_EXP, PACKED), lambda i, te, nx, sl, ts, tv: (ts[i], 0)),
                  hbm, hbm, hbm],
        out_specs=pl.BlockSpec((TM_EXP, PACKED), lambda i, te, nx, sl, ts, tv: (i, 0)),
        scratch_shapes=[
            pltpu.VMEM((2, D_MODEL, D_FF), BF16), pltpu.VMEM((2, D_MODEL, D_FF), BF16),
            pltpu.VMEM((2, D_FF, D_MODEL), BF16),
            pltpu.SemaphoreType.DMA((2, 3)),
        ],
    )
    return pl.pallas_call(
        _expert_kernel,
        name="experts",
        grid_spec=grid_spec,
        out_shape=jax.ShapeDtypeStruct((rows, PACKED), I32),
        compiler_params=pltpu.CompilerParams(
            dimension_semantics=("arbitrary",), vmem_limit_bytes=VMEM_LIMIT),
    )(tile_expert, next_expert, tile_slot, tile_src, tile_valid, xs, wg, wu, wd)


def _combine_kernel(y1_ref, y2_ref, route_ref, xmid_ref, g_ref, *rest):
    o_ref = rest[-1]
    route = route_ref[...]
    lane = lax.broadcasted_iota(I32, route.shape, 1)
    w1 = jnp.sum(jnp.where(lane == 2, route, 0.0), axis=-1, keepdims=True)
    w2 = jnp.sum(jnp.where(lane == 3, route, 0.0), axis=-1, keepdims=True)
    lo1, hi1 = _unpack_bf16_pairs(y1_ref[...])
    lo2, hi2 = _unpack_bf16_pairs(y2_ref[...])
    lox, hix = _unpack_bf16_pairs(xmid_ref[...])
    x_out = jnp.concatenate([lox + (lo1 * w1 + lo2 * w2), hix + (hi1 * w1 + hi2 * w2)], axis=1)
    o_ref[...] = _rms(x_out, g_ref[...])


def _combine(yg, route, xmid, g, chunk, t_total, out_prev):
    t = xmid.shape[0]
    n_tiles = t // TM_CMB
    first = chunk * n_tiles
    in_specs = [
        pl.BlockSpec((TM_CMB, PACKED), lambda i: (i, 0)),
        pl.BlockSpec((TM_CMB, PACKED), lambda i: (n_tiles + i, 0)),
        pl.BlockSpec((TM_CMB, LANES), lambda i: (i, 0)),
        pl.BlockSpec((TM_CMB, PACKED), lambda i: (i, 0)),
        pl.BlockSpec((1, D_MODEL), lambda i: (0, 0)),
    ]
    args = [yg, yg, route, xmid, g]
    aliases = {}
    if out_prev is not None:
        in_specs.append(pl.BlockSpec(memory_space=pl.ANY))
        aliases = {len(args): 0}
        args.append(out_prev)
    return pl.pallas_call(
        _combine_kernel,
        name="combine",
        grid=(n_tiles,),
        in_specs=in_specs,
        out_specs=pl.BlockSpec((TM_CMB, D_MODEL), lambda i: (first + i, 0)),
        out_shape=jax.ShapeDtypeStruct((t_total, D_MODEL), F32),
        input_output_aliases=aliases,
        compiler_params=pltpu.CompilerParams(
            dimension_semantics=("arbitrary",), vmem_limit_bytes=VMEM_LIMIT),
    )(*args)


def _split_bf16(w):
    hi = w.astype(BF16)
    lo = (w - hi.astype(F32)).astype(BF16)
    return hi, lo


def kernel(x, norm_mix, w_in, conv_w, conv_b, w_a_out, sinks, w_b_out, w_o, norm_ffn, w_group,
           b_group, w_expert, b_expert, w_gate, w_up, w_down, norm_final):
    bsz, seq, d = x.shape
    t = bsz * seq
    assert d == D_MODEL and seq % TM_PROJ == 0 and seq % TQ_ATTN == 0
    xf = x.reshape(t, d)
    row = lambda v: v.reshape(1, -1)

    assert w_in.shape == (D_MODEL, IN_COLS)
    za, proj, w_gate, w_up, w_down = _inproj(xf, row(norm_mix), w_in.astype(F32), conv_w,
                                             row(conv_b), w_a_out.astype(BF16), seq,
                                             w_gate, w_up, w_down)
    pad = LANES - N_GROUPS - N_EXPERTS
    w_r = jnp.concatenate([w_expert, w_group, jnp.zeros((d, pad), F32)], axis=1)
    b_r = jnp.concatenate([b_expert, b_group, jnp.zeros((pad,), F32)]).reshape(1, LANES)
    wr = jnp.concatenate(_split_bf16(w_r), axis=1)
    wb = w_b_out.astype(BF16)
    wo = w_o.astype(BF16)

    t_chunk = t // MOE_CHUNKS
    assert t == t_chunk * MOE_CHUNKS and all(
        t_chunk % step == 0
        for step in (TQ_ATTN, TM_MIX, TM_CNT * POS_TILES, TM_CMB, SC_WORKERS * SC_CHUNK))
    out = None
    for chunk in range(MOE_CHUNKS):
        merged = _attention(proj, za, sinks, wb, seq, chunk, t_chunk)
        xmid, h2, route, cnt = _mix(merged, xf, wo, row(norm_ffn), wr, b_r, chunk, t_chunk)
        out = _moe_chunk(xmid, h2, route, cnt, w_gate, w_up, w_down, row(norm_final), chunk, t, out)
    return out.reshape(bsz, seq, d)


def _moe_chunk(xmid, h2, route, cnt, w_gate, w_up, w_down, g_final, chunk, t_total, out_prev):
    t = xmid.shape[0]
    n_tiles = t // TM_CNT
    cnt = cnt.reshape(n_tiles, 8, LANES)[:, 0, :N_EXPERTS].astype(I32)
    totals = jnp.sum(cnt, axis=0)
    tiles_e = (totals + TM_EXP - 1) // TM_EXP
    tile_end = jnp.cumsum(tiles_e)
    offset = (tile_end - tiles_e) * TM_EXP
    base = offset[None, :] + jnp.cumsum(cnt, axis=0) - cnt
    base = jnp.pad(base, ((0, 0), (0, LANES - N_EXPERTS))).astype(F32).reshape(n_tiles, 1, LANES)
    rows = t * TOP_K + N_EXPERTS * TM_EXP
    n_active = tile_end[-1]
    tile_id = jnp.arange(rows // TM_EXP, dtype=I32)
    tile_src = jnp.minimum(tile_id, n_active - 1)
    tile_expert = jnp.sum((tile_src[:, None] >= tile_end[None, :]).astype(I32), axis=1)
    tile_expert = jnp.minimum(tile_expert, N_EXPERTS - 1)
    row_in_expert = (tile_id - (tile_end - tiles_e)[tile_expert]) * TM_EXP
    tile_valid = jnp.clip(totals[tile_expert] - row_in_expert, 0, TM_EXP)
    tile_valid = jnp.where(tile_id < n_active, tile_valid, 0).astype(I32)
    after = tile_end[tile_expert]
    next_expert = jnp.where(after < n_active, tile_expert[jnp.minimum(after, n_active - 1)], -1)
    first_of_expert = jnp.concatenate(
        [jnp.ones((1,), I32), (tile_expert[1:] != tile_expert[:-1]).astype(I32)])
    tile_slot = (jnp.cumsum(first_of_expert) - 1) % 2

    pos = _positions(route, base)
    xs = _dispatch(pos, h2, rows)
    ys = _experts(tile_expert.astype(I32), next_expert.astype(I32), tile_slot.astype(I32),
                  tile_src.astype(I32), tile_valid, xs, w_gate, w_up, w_down)
    yg = _gather_rows(ys, pos.reshape(TOP_K * t))
    return _combine(yg, route, xmid, g_final, chunk, t_total, out_prev)
```

```python
import functools
import math

import jax
import jax.numpy as jnp
from jax import lax
from jax.experimental import pallas as pl
from jax.experimental.pallas import tpu as pltpu
from jax.experimental.pallas import tpu_sc as plsc

F32 = jnp.float32
BF16 = jnp.bfloat16
I32 = jnp.int32

D_MODEL = 1024
HEAD_DIM = 64
N_HEADS = 16
N_KV_HEADS = 4
GROUP = N_HEADS // N_KV_HEADS
KV_WIDTH = N_KV_HEADS * HEAD_DIM
WINDOW = 128
N_GROUPS = 4
EXPERTS_PER_GROUP = 8
N_EXPERTS = N_GROUPS * EXPERTS_PER_GROUP
TOP_K = 2
D_FF = 512
EPS = 1e-6
LANES = 128

COL_WIDTH = {"b": D_MODEL, "c": D_MODEL, "u": D_MODEL, "q": D_MODEL, "k": KV_WIDTH, "v": KV_WIDTH,
             "ga": D_MODEL, "gb": D_MODEL}
COL_START = dict(zip(COL_WIDTH, (sum(list(COL_WIDTH.values())[:i]) for i in range(len(COL_WIDTH)))))
IN_COLS = sum(COL_WIDTH.values())
REST_COLS = 2 * D_MODEL + 2 * KV_WIDTH
COL_GB, COL_Q = 0, 1
COL_K, COL_V = 2 * D_MODEL // KV_WIDTH, 2 * D_MODEL // KV_WIDTH + 1

TM_PROJ = 512
SUB_PROJ = 256
TQ_ATTN = 1024
TM_MIX = 1024
SUB_MIX = 256
TM_CNT = 512
POS_TILES = 8
MOE_CHUNKS = 2
TM_EXP = 512
SUB_EXP = 256
TM_CMB = 2048
HALO_ROWS = 8
VMEM_LIMIT = 56 * 1024 * 1024
PACKED = D_MODEL // 2

SC_CORES = 2
SC_SUBCORES = 16
SC_WORKERS = SC_CORES * SC_SUBCORES
SC_CHUNK = 64


def _rms(x, g):
    r = lax.rsqrt(jnp.mean(x * x, axis=-1, keepdims=True) + EPS)
    return (x * r) * g


def _pack_bf16_pairs(x):
    n = x.shape[1] // 2
    lo = lax.bitcast_convert_type(x[:, :n].astype(BF16).astype(F32), I32)
    hi = lax.bitcast_convert_type(x[:, n:].astype(BF16).astype(F32), I32)
    return (hi & jnp.int32(-65536)) | lax.shift_right_logical(lo, 16)


def _unpack_bf16_pairs(p):
    lo = lax.bitcast_convert_type(lax.shift_left(p, 16), F32)
    hi = lax.bitcast_convert_type(p & jnp.int32(-65536), F32)
    return lo, hi


def _inproj_kernel(x_ref, g_ref, w_ref, cw_ref, cb_ref, wa_ref,
                   eg_ref, eu_ref, ed_ref, wb_in, wo_in,
                   za_ref, proj_ref, eg_out, eu_out, ed_out, wb_out, wo_out, halo_ref, *, tiles_per_seq):
    i = pl.program_id(0)
    eg_out[...] = eg_ref[...].astype(BF16)
    eu_out[...] = eu_ref[...].astype(BF16)
    ed_out[...] = ed_ref[...].astype(BF16)
    wb_out[...] = wb_in[...].astype(BF16)
    wo_out[...] = wo_in[...].astype(BF16)
    halves = [slice(s * SUB_PROJ, (s + 1) * SUB_PROJ) for s in range(TM_PROJ // SUB_PROJ)]
    hs = [_rms(x_ref[r, :], g_ref[...]).astype(BF16) for r in halves]
    col = lambda a, b=None: slice(COL_START[a], COL_START[b or a] + COL_WIDTH[b or a])
    proj = lambda a, b=None: [jnp.dot(h, w_ref[:, col(a, b)], preferred_element_type=F32) for h in hs]
    pcu = jnp.concatenate(proj("c", "u"), axis=0)
    pb = jnp.concatenate(proj("b"), axis=0)
    pga = jnp.concatenate(proj("ga"), axis=0)
    for r, pgb, pqkv in zip(halves, proj("gb"), proj("q", "v")):
        proj_ref[r, :D_MODEL] = pgb.astype(BF16)
        proj_ref[r, D_MODEL:] = pqkv.astype(BF16)
    cu = pcu[:, :D_MODEL] * pcu[:, D_MODEL:]
    first = (i % tiles_per_seq) == 0
    hist = jnp.where(first, 0.0, halo_ref[...])
    prev1 = hist[HALO_ROWS - 1:HALO_ROWS]
    prev2 = hist[HALO_ROWS - 2:HALO_ROWS - 1]
    halo_ref[...] = cu[TM_PROJ - HALO_ROWS:, :]
    row = lax.broadcasted_iota(I32, cu.shape, 0)
    cu1 = jnp.where(row == 0, prev1, pltpu.roll(cu, 1, 0))
    cu2 = jnp.where(row == 0, prev2, jnp.where(row == 1, prev1, pltpu.roll(cu, 2, 0)))
    cw = cw_ref[...]
    y = cw[0:1] * cu2 + cw[1:2] * cu1 + cw[2:3] * cu + cb_ref[...]
    ya = (pb * y).astype(BF16)
    z = jnp.dot(ya, wa_ref[...], preferred_element_type=F32)
    za_ref[...] = (jax.nn.sigmoid(pga) * z).astype(BF16)


def _inproj(xf, g, w_in, conv_w, conv_b, wa, seq, w_gate, w_up, w_down, w_b, w_o):
    t = xf.shape[0]
    n_tiles = t // TM_PROJ
    const = lambda shape: pl.BlockSpec(shape, lambda i: (0, 0), pipeline_mode=pl.Buffered(1))
    slabs = [w.reshape(-1, w.shape[-1]) for w in (w_gate, w_up, w_down, w_b, w_o)]
    slab_rows = [s.shape[0] // n_tiles for s in slabs]
    assert all(s.shape[0] == r * n_tiles and r % 16 == 0 for s, r in zip(slabs, slab_rows))
    slab_specs = [pl.BlockSpec((r, s.shape[1]), lambda i: (i, 0)) for s, r in zip(slabs, slab_rows)]
    outs = pl.pallas_call(
        functools.partial(_inproj_kernel, tiles_per_seq=seq // TM_PROJ),
        name="inproj_conv",
        grid=(n_tiles,),
        in_specs=[
            pl.BlockSpec((TM_PROJ, D_MODEL), lambda i: (i, 0)),
            const((1, D_MODEL)),
            const((D_MODEL, IN_COLS)),
            const((3, D_MODEL)), const((1, D_MODEL)), const((D_MODEL, D_MODEL)),
        ] + slab_specs,
        out_specs=[pl.BlockSpec((TM_PROJ, D_MODEL), lambda i: (i, 0)),
                   pl.BlockSpec((TM_PROJ, REST_COLS), lambda i: (i, 0))] + slab_specs,
        out_shape=[jax.ShapeDtypeStruct((t, D_MODEL), BF16),
                   jax.ShapeDtypeStruct((t, REST_COLS), BF16)]
                  + [jax.ShapeDtypeStruct(s.shape, BF16) for s in slabs],
        scratch_shapes=[pltpu.VMEM((HALO_ROWS, D_MODEL), F32)],
        compiler_params=pltpu.CompilerParams(
            dimension_semantics=("arbitrary",), vmem_limit_bytes=VMEM_LIMIT),
    )(xf, g, w_in, conv_w, conv_b, wa, *slabs)
    za, proj, eg, eu, ed, wb, wo = outs
    return za, proj, eg.reshape(w_gate.shape), eu.reshape(w_up.shape), ed.reshape(w_down.shape), wb, wo


def _attn_kernel(sink_ref, q_ref, k_ref, v_ref, kp_ref, vp_ref, gb_ref, za_ref, wb_ref, o_ref,
                 *, first_tile_index, tiles_per_seq):
    first_tile = ((first_tile_index + pl.program_id(0)) % tiles_per_seq) == 0
    ks = lax.broadcasted_iota(I32, (WINDOW, WINDOW), 0)
    qq = lax.broadcasted_iota(I32, (WINDOW, WINDOW), 1)
    own = ks <= qq
    dist = jnp.where(own, qq - ks, qq - ks + WINDOW).astype(F32)
    visible0 = jnp.logical_or(own, jnp.logical_not(first_tile))
    log2e = math.log2(math.e)
    c_scale = log2e / math.sqrt(HEAD_DIM)
    nt = (((1,), (1,)), ((), ()))
    zk = jnp.zeros((2 * WINDOW, HEAD_DIM), BF16)

    def transposed(v_blk):
        return jnp.transpose(v_blk.astype(F32)).astype(BF16)

    def project(rows_p, attn_blk):
        yb = jnp.dot(attn_blk, wb_ref[...], preferred_element_type=F32)
        zb = jax.nn.sigmoid(gb_ref[rows_p, :].astype(F32)) * yb
        o_ref[rows_p, :] = (za_ref[rows_p, :].astype(F32) + zb).astype(BF16)

    pending = None
    prev_k = kp_ref[...]
    prev_vt = transposed(vp_ref[...])
    for sb in range(TQ_ATTN // WINDOW):
        rows = slice(sb * WINDOW, (sb + 1) * WINDOW)
        cur_k = k_ref[rows, :]
        cur_vt = transposed(v_ref[rows, :])
        scores, vcats = [], []
        for kh in range(N_KV_HEADS):
            cols = slice(kh * HEAD_DIM, (kh + 1) * HEAD_DIM)
            kcat = jnp.concatenate([prev_k[:, cols], cur_k[:, cols]], axis=0)
            vcats.append(jnp.concatenate([prev_vt[cols, :], cur_vt[cols, :]], axis=1))
            qg = jnp.concatenate([q_ref[rows, (2 * kh) * LANES:(2 * kh + 1) * LANES],
                                  q_ref[rows, (2 * kh + 1) * LANES:(2 * kh + 2) * LANES]], axis=0)
            k_pad = jnp.concatenate([jnp.concatenate([kcat, zk], axis=1),
                                     jnp.concatenate([zk, kcat], axis=1)], axis=0)
            scores.append(lax.dot_general(k_pad, qg, nt, preferred_element_type=F32))
        if pending is not None:
            project(*pending)
        probs, rdens = [], []
        for kh in range(N_KV_HEADS):
            for pos in range(2):
                pr, rd = [], []
                for half in range(2):
                    h = kh * GROUP + 2 * half + pos
                    slope = 2.0 ** (-8.0 * (h + 1) / N_HEADS)
                    qcols = slice(half * WINDOW, (half + 1) * WINDOW)
                    krow = pos * 2 * WINDOW
                    st = scores[kh]
                    s = (jnp.where(own, st[krow + WINDOW:krow + 2 * WINDOW, qcols],
                                   st[krow:krow + WINDOW, qcols]) * c_scale
                         - (slope * log2e) * dist)
                    if sb == 0:
                        s = jnp.where(visible0, s, -jnp.inf)
                    m = jnp.max(s, axis=0, keepdims=True)
                    p = jnp.exp2(s - m)
                    den = jnp.sum(p, axis=0, keepdims=True) + jnp.exp2(sink_ref[h] * log2e - m)
                    rd.append(1.0 / den)
                    pr.append(jnp.concatenate(
                        [jnp.where(own, 0.0, p).astype(BF16), jnp.where(own, p, 0.0).astype(BF16)],
                        axis=0))
                probs.append(jnp.concatenate(pr, axis=1))
                rdens.append(jnp.concatenate(rd, axis=1))
        out_t = [None] * N_HEADS
        for kh in range(N_KV_HEADS):
            for pos in range(2):
                o2 = jnp.dot(vcats[kh], probs[2 * kh + pos], preferred_element_type=F32)
                o2 = o2 * rdens[2 * kh + pos]
                out_t[kh * GROUP + pos] = o2[:, :WINDOW]
                out_t[kh * GROUP + 2 + pos] = o2[:, WINDOW:]
        pending = (rows, jnp.transpose(jnp.concatenate(out_t, axis=0)).astype(BF16))
        prev_k, prev_vt = cur_k, cur_vt
    project(*pending)


def _attention(proj, za, sinks, wb, seq, chunk, t):
    sub = TQ_ATTN // WINDOW
    n_tiles = t // TQ_ATTN
    first = chunk * n_tiles
    src = lambda w, c: pl.BlockSpec((TQ_ATTN, w), lambda i: (first + i, c))
    prev = lambda c: pl.BlockSpec((WINDOW, KV_WIDTH),
                                  lambda i: (jnp.maximum((first + i) * sub - 1, 0), c))
    return pl.pallas_call(
        functools.partial(_attn_kernel, first_tile_index=first, tiles_per_seq=seq // TQ_ATTN),
        name="swattn",
        grid=(n_tiles,),
        in_specs=[
            pl.BlockSpec(memory_space=pltpu.SMEM),
            src(D_MODEL, COL_Q), src(KV_WIDTH, COL_K), src(KV_WIDTH, COL_V), prev(COL_K), prev(COL_V),
            src(D_MODEL, COL_GB), src(D_MODEL, 0),
            pl.BlockSpec((D_MODEL, D_MODEL), lambda i: (0, 0), pipeline_mode=pl.Buffered(1)),
        ],
        out_specs=pl.BlockSpec((TQ_ATTN, D_MODEL), lambda i: (i, 0)),
        out_shape=jax.ShapeDtypeStruct((t, D_MODEL), BF16),
        compiler_params=pltpu.CompilerParams(
            dimension_semantics=("arbitrary",), vmem_limit_bytes=VMEM_LIMIT),
    )(sinks, proj, proj, proj, proj, proj, proj, za, wb)


def _mix_kernel(merged_ref, x_ref, wo_ref, g_ref, wr_ref, br_ref,
                xmid_ref, h_ref, route_ref, cnt_ref):
    subs = [slice(s * SUB_MIX, (s + 1) * SUB_MIX) for s in range(TM_MIX // SUB_MIX)]
    xm = [x_ref[r, :] + jnp.dot(merged_ref[r, :], wo_ref[...], preferred_element_type=F32)
          for r in subs]
    hs = []
    for r, v in zip(subs, xm):
        xmid_ref[r, :] = _pack_bf16_pairs(v)
        h = _rms(v, g_ref[...])
        h_ref[r, :] = _pack_bf16_pairs(h)
        hs.append(h)
    wr = wr_ref[...]
    logits = []
    for h in hs:
        h_hi = h.astype(BF16)
        h_lo = (h - h_hi.astype(F32)).astype(BF16)
        both = jnp.dot(h_hi, wr, preferred_element_type=F32)
        logits.append(both[:, :LANES] + both[:, LANES:]
                      + jnp.dot(h_lo, wr[:, :LANES], preferred_element_type=F32) + br_ref[...])
    subs_per_cnt = TM_CNT // SUB_MIX
    cnts = [jnp.zeros((8, LANES), F32) for _ in range(TM_MIX // TM_CNT)]
    for s, (r, lg) in enumerate(zip(subs, logits)):
        route, cnt = _route(lg)
        route_ref[r, :] = route
        cnts[s // subs_per_cnt] = cnts[s // subs_per_cnt] + cnt
    for c, cnt in enumerate(cnts):
        cnt_ref[c * 8:(c + 1) * 8, :] = cnt


def _route(logits):
    n = logits.shape[0]
    lt = jnp.transpose(logits)
    sub = lax.broadcasted_iota(I32, (EXPERTS_PER_GROUP, n), 0)
    neg = -jnp.inf
    gl = jnp.where(sub < N_GROUPS, lt[N_EXPERTS:N_EXPERTS + EXPERTS_PER_GROUP], neg)
    gmax = jnp.max(gl, axis=0, keepdims=True)
    g_idx = jnp.min(jnp.where(gl == gmax, sub, EXPERTS_PER_GROUP), axis=0, keepdims=True)
    p_g = 1.0 / jnp.sum(jnp.exp(gl - gmax), axis=0, keepdims=True)
    v1 = v2 = i1 = i2 = None
    for g in range(N_GROUPS):
        eg = lt[g * EXPERTS_PER_GROUP:(g + 1) * EXPERTS_PER_GROUP]
        a1 = jnp.max(eg, axis=0, keepdims=True)
        j1 = jnp.min(jnp.where(eg == a1, sub, EXPERTS_PER_GROUP), axis=0, keepdims=True)
        eg2 = jnp.where(sub == j1, neg, eg)
        a2 = jnp.max(eg2, axis=0, keepdims=True)
        j2 = jnp.min(jnp.where(eg2 == a2, sub, EXPERTS_PER_GROUP), axis=0, keepdims=True)
        if g == 0:
            v1, v2, i1, i2 = a1, a2, j1, j2
        else:
            chosen = g_idx == g
            v1, v2 = jnp.where(chosen, a1, v1), jnp.where(chosen, a2, v2)
            i1, i2 = jnp.where(chosen, j1, i1), jnp.where(chosen, j2, i2)
    e21 = jnp.exp(v2 - v1)
    w1 = p_g / (1.0 + e21)
    w2 = p_g * e21 / (1.0 + e21)
    e1 = g_idx * EXPERTS_PER_GROUP + i1
    e2 = g_idx * EXPERTS_PER_GROUP + i2
    rows8 = jnp.where(sub == 0, e1.astype(F32),
                      jnp.where(sub == 1, e2.astype(F32),
                                jnp.where(sub == 2, w1, jnp.where(sub == 3, w2, 0.0))))
    route_t = jnp.concatenate([rows8, jnp.zeros((LANES - EXPERTS_PER_GROUP, n), F32)], axis=0)
    expert_row = lax.broadcasted_iota(I32, (LANES, n), 0)
    onehot_t = ((expert_row == e1) | (expert_row == e2)).astype(BF16)
    cnt = lax.dot_general(jnp.ones((8, n), BF16), onehot_t, (((1,), (1,)), ((), ())),
                          preferred_element_type=F32)
    return jnp.transpose(route_t), cnt


def _mix(merged, xf, wo, g, wr, br, chunk, t):
    n_tiles = t // TM_MIX
    first = chunk * n_tiles
    cnt_rows = TM_MIX // TM_CNT * 8
    full = lambda shape: pl.BlockSpec(shape, lambda i: (0, 0), pipeline_mode=pl.Buffered(1))
    tile = lambda w=D_MODEL: pl.BlockSpec((TM_MIX, w), lambda i: (i, 0))
    return pl.pallas_call(
        _mix_kernel,
        name="merge_router",
        grid=(n_tiles,),
        in_specs=[
            tile(), pl.BlockSpec((TM_MIX, D_MODEL), lambda i: (first + i, 0)),
            full((D_MODEL, D_MODEL)), full((1, D_MODEL)),
            full((D_MODEL, 2 * LANES)), full((1, LANES)),
        ],
        out_specs=[tile(PACKED), tile(PACKED), tile(LANES),
                   pl.BlockSpec((cnt_rows, LANES), lambda i: (i, 0))],
        out_shape=[
            jax.ShapeDtypeStruct((t, PACKED), I32),
            jax.ShapeDtypeStruct((t, PACKED), I32),
            jax.ShapeDtypeStruct((t, LANES), F32),
            jax.ShapeDtypeStruct((n_tiles * cnt_rows, LANES), F32),
        ],
        compiler_params=pltpu.CompilerParams(
            dimension_semantics=("arbitrary",), vmem_limit_bytes=VMEM_LIMIT),
    )(merged, xf, wo, g, wr, br)


def _pos_kernel(route_ref, base_ref, pos_ref):
    lane = lax.broadcasted_iota(I32, (TM_CNT, LANES), 1)
    r = lax.broadcasted_iota(I32, (TM_CNT, TM_CNT), 0)
    c = lax.broadcasted_iota(I32, (TM_CNT, TM_CNT), 1)
    lower = (c < r).astype(BF16)
    subs = [slice(s * TM_CNT, (s + 1) * TM_CNT) for s in range(POS_TILES)]
    routes = [route_ref[rs, :] for rs in subs]
    e1 = [jnp.sum(jnp.where(lane == 0, rt, 0.0), axis=-1, keepdims=True).astype(I32) for rt in routes]
    e2 = [jnp.sum(jnp.where(lane == 1, rt, 0.0), axis=-1, keepdims=True).astype(I32) for rt in routes]
    onehot = [((lane == a) | (lane == b)).astype(BF16) for a, b in zip(e1, e2)]
    before = [jnp.dot(lower, oh, preferred_element_type=F32) + base_ref[s]
              for s, oh in enumerate(onehot)]
    for s, rs in enumerate(subs):
        p1 = jnp.sum(jnp.where(lane == e1[s], before[s], 0.0), axis=-1, keepdims=True)
        p2 = jnp.sum(jnp.where(lane == e2[s], before[s], 0.0), axis=-1, keepdims=True)
        packed = jnp.where(lane == 0, p1, jnp.where(lane == 1, p2, 0.0))
        pos_ref[:, rs] = jnp.transpose(packed)[0:TOP_K, :].astype(I32)


def _positions(route, base):
    t = route.shape[0]
    n_steps = t // (TM_CNT * POS_TILES)
    return pl.pallas_call(
        _pos_kernel,
        name="positions",
        grid=(n_steps,),
        in_specs=[
            pl.BlockSpec((TM_CNT * POS_TILES, LANES), lambda i: (i, 0)),
            pl.BlockSpec((POS_TILES, 1, LANES), lambda i: (i, 0, 0)),
        ],
        out_specs=pl.BlockSpec((TOP_K, TM_CNT * POS_TILES), lambda i: (0, i)),
        out_shape=jax.ShapeDtypeStruct((TOP_K, t), I32),
        compiler_params=pltpu.CompilerParams(dimension_semantics=("arbitrary",)),
    )(route, base)


def _sc_mesh():
    return plsc.VectorSubcoreMesh(core_axis_name="c", subcore_axis_name="s",
                                  num_cores=SC_CORES, num_subcores=SC_SUBCORES)


def _sc_worker():
    return lax.axis_index("s") * SC_CORES + lax.axis_index("c")


def _dispatch(pos, hp, rows):
    t = hp.shape[0]
    per_w = t // SC_WORKERS
    n_ch = per_w // SC_CHUNK
    pos4 = pos.reshape(TOP_K, SC_WORKERS, n_ch, SC_CHUNK)

    @functools.partial(
        pl.kernel, mesh=_sc_mesh(),
        out_type=jax.ShapeDtypeStruct((rows, PACKED), I32),
        scratch_types=[pltpu.VMEM((TOP_K, n_ch, SC_CHUNK), I32),
                       pltpu.VMEM((SC_CHUNK, PACKED), I32)])
    def scatter(hp_hbm, pos_hbm, xs_hbm, idx_v, rows_v):
        wid = _sc_worker()
        for k in range(TOP_K):
            pltpu.sync_copy(pos_hbm.at[k, wid], idx_v.at[k])

        def body(c, carry):
            start = pl.multiple_of(wid * per_w + c * SC_CHUNK, SC_CHUNK)
            pltpu.sync_copy(hp_hbm.at[pl.ds(start, SC_CHUNK)], rows_v)
            for k in range(TOP_K):
                pltpu.sync_copy(rows_v, xs_hbm.at[idx_v.at[k, c]])
            return carry

        lax.fori_loop(0, n_ch, body, 0)

    return scatter(hp, pos4)


def _gather_rows(table, idx):
    n = idx.shape[0]
    per_w = n // SC_WORKERS
    n_ch = per_w // SC_CHUNK
    idx3 = idx.reshape(SC_WORKERS, n_ch, SC_CHUNK)

    @functools.partial(
        pl.kernel, mesh=_sc_mesh(),
        out_type=jax.ShapeDtypeStruct((n, PACKED), I32),
        scratch_types=[pltpu.VMEM((n_ch, SC_CHUNK), I32),
                       pltpu.VMEM((SC_CHUNK, PACKED), I32)])
    def gather(table_hbm, idx_hbm, out_hbm, idx_v, rows_v):
        wid = _sc_worker()
        pltpu.sync_copy(idx_hbm.at[wid], idx_v)

        def body(c, carry):
            start = pl.multiple_of(wid * per_w + c * SC_CHUNK, SC_CHUNK)
            pltpu.sync_copy(table_hbm.at[idx_v.at[c]], rows_v)
            pltpu.sync_copy(rows_v, out_hbm.at[pl.ds(start, SC_CHUNK)])
            return carry

        lax.fori_loop(0, n_ch, body, 0)

    return gather(table, idx3)


def _expert_kernel(te_ref, nx_ref, sl_ref, ts_ref, tv_ref, xs_ref, wg_hbm, wu_hbm, wd_hbm, o_ref,
                   wg_s, wu_s, wd_s, sem):
    del ts_ref
    i = pl.program_id(0)
    n_valid = tv_ref[i]
    expert = te_ref[i]
    slot = sl_ref[i]

    def weight_copies(e, s):
        return (pltpu.make_async_copy(wg_hbm.at[e], wg_s.at[s], sem.at[s, 0]),
                pltpu.make_async_copy(wu_hbm.at[e], wu_s.at[s], sem.at[s, 1]),
                pltpu.make_async_copy(wd_hbm.at[e], wd_s.at[s], sem.at[s, 2]))

    @pl.when(i == 0)
    def _():
        for copy in weight_copies(expert, slot):
            copy.start()

    @pl.when(jnp.logical_or(i == 0, expert != te_ref[jnp.maximum(i - 1, 0)]))
    def _():
        for copy in weight_copies(expert, slot):
            copy.wait()
        nxt = nx_ref[i]

        @pl.when(nxt >= 0)
        def _():
            for copy in weight_copies(nxt, 1 - slot):
                copy.start()

    def mlp(n_sub):
        wg, wu, wd = wg_s.at[slot], wu_s.at[slot], wd_s.at[slot]
        subs = [slice(s * SUB_EXP, (s + 1) * SUB_EXP) for s in range(n_sub)]
        xin = []
        for r in subs:
            rid = r.start + lax.broadcasted_iota(I32, (SUB_EXP, PACKED), 0)
            lo, hi = _unpack_bf16_pairs(jnp.where(rid < n_valid, xs_ref[r, :], 0))
            xin.append((lo.astype(BF16), hi.astype(BF16)))
        ab = [(jnp.dot(lo, wg[:PACKED, :], preferred_element_type=F32)
               + jnp.dot(hi, wg[PACKED:, :], preferred_element_type=F32),
               jnp.dot(lo, wu[:PACKED, :], preferred_element_type=F32)
               + jnp.dot(hi, wu[PACKED:, :], preferred_element_type=F32)) for lo, hi in xin]
        for r, (ai, bi) in zip(subs, ab):
            hm = (ai * jax.nn.sigmoid(ai) * bi).astype(BF16)
            o_ref[r, :] = _pack_bf16_pairs(jnp.dot(hm, wd[...], preferred_element_type=F32))
        if n_sub * SUB_EXP < TM_EXP:
            o_ref[n_sub * SUB_EXP:, :] = jnp.zeros((TM_EXP - n_sub * SUB_EXP, PACKED), I32)

    n_subs = TM_EXP // SUB_EXP
    for n_sub in range(n_subs + 1):
        lo_rows = (n_sub - 1) * SUB_EXP if n_sub else -1
        in_range = jnp.logical_and(n_valid > lo_rows, n_valid <= n_sub * SUB_EXP)
        pl.when(in_range)(functools.partial(mlp, n_sub))


def _experts(tile_expert, next_expert, tile_slot, tile_src, tile_valid, xs, wg, wu, wd):
    rows = xs.shape[0]
    hbm = pl.BlockSpec(memory_space=pl.ANY)
    grid_spec = pltpu.PrefetchScalarGridSpec(
        num_scalar_prefetch=5,
        grid=(rows // TM_EXP,),
        in_specs=[pl.BlockSpec((TM_EXP, PACKED), lambda i, te, nx, sl, ts, tv: (ts[i], 0)),
                  hbm, hbm, hbm],
        out_specs=pl.BlockSpec((TM_EXP, PACKED), lambda i, te, nx, sl, ts, tv: (i, 0)),
        scratch_shapes=[
            pltpu.VMEM((2, D_MODEL, D_FF), BF16), pltpu.VMEM((2, D_MODEL, D_FF), BF16),
            pltpu.VMEM((2, D_FF, D_MODEL), BF16),
            pltpu.SemaphoreType.DMA((2, 3)),
        ],
    )
    return pl.pallas_call(
        _expert_kernel,
        name="experts",
        grid_spec=grid_spec,
        out_shape=jax.ShapeDtypeStruct((rows, PACKED), I32),
        compiler_params=pltpu.CompilerParams(
            dimension_semantics=("arbitrary",), vmem_limit_bytes=VMEM_LIMIT),
    )(tile_expert, next_expert, tile_slot, tile_src, tile_valid, xs, wg, wu, wd)


def _combine_kernel(y1_ref, y2_ref, route_ref, xmid_ref, g_ref, *rest):
    o_ref = rest[-1]
    route = route_ref[...]
    lane = lax.broadcasted_iota(I32, route.shape, 1)
    w1 = jnp.sum(jnp.where(lane == 2, route, 0.0), axis=-1, keepdims=True)
    w2 = jnp.sum(jnp.where(lane == 3, route, 0.0), axis=-1, keepdims=True)
    lo1, hi1 = _unpack_bf16_pairs(y1_ref[...])
    lo2, hi2 = _unpack_bf16_pairs(y2_ref[...])
    lox, hix = _unpack_bf16_pairs(xmid_ref[...])
    x_out = jnp.concatenate([lox + (lo1 * w1 + lo2 * w2), hix + (hi1 * w1 + hi2 * w2)], axis=1)
    o_ref[...] = _rms(x_out, g_ref[...])


def _combine(yg, route, xmid, g, chunk, t_total, out_prev):
    t = xmid.shape[0]
    n_tiles = t // TM_CMB
    first = chunk * n_tiles
    in_specs = [
        pl.BlockSpec((TM_CMB, PACKED), lambda i: (i, 0)),
        pl.BlockSpec((TM_CMB, PACKED), lambda i: (n_tiles + i, 0)),
        pl.BlockSpec((TM_CMB, LANES), lambda i: (i, 0)),
        pl.BlockSpec((TM_CMB, PACKED), lambda i: (i, 0)),
        pl.BlockSpec((1, D_MODEL), lambda i: (0, 0)),
    ]
    args = [yg, yg, route, xmid, g]
    aliases = {}
    if out_prev is not None:
        in_specs.append(pl.BlockSpec(memory_space=pl.ANY))
        aliases = {len(args): 0}
        args.append(out_prev)
    return pl.pallas_call(
        _combine_kernel,
        name="combine",
        grid=(n_tiles,),
        in_specs=in_specs,
        out_specs=pl.BlockSpec((TM_CMB, D_MODEL), lambda i: (first + i, 0)),
        out_shape=jax.ShapeDtypeStruct((t_total, D_MODEL), F32),
        input_output_aliases=aliases,
        compiler_params=pltpu.CompilerParams(
            dimension_semantics=("arbitrary",), vmem_limit_bytes=VMEM_LIMIT),
    )(*args)


def _split_bf16(w):
    hi = w.astype(BF16)
    lo = (w - hi.astype(F32)).astype(BF16)
    return hi, lo


def kernel(x, norm_mix, w_in, conv_w, conv_b, w_a_out, sinks, w_b_out, w_o, norm_ffn, w_group,
           b_group, w_expert, b_expert, w_gate, w_up, w_down, norm_final):
    bsz, seq, d = x.shape
    t = bsz * seq
    assert d == D_MODEL and seq % TM_PROJ == 0 and seq % TQ_ATTN == 0
    xf = x.reshape(t, d)
    row = lambda v: v.reshape(1, -1)

    assert w_in.shape == (D_MODEL, IN_COLS)
    za, proj, w_gate, w_up, w_down, wb, wo = _inproj(xf, row(norm_mix), w_in.astype(BF16), conv_w,
                                                     row(conv_b), w_a_out.astype(BF16), seq,
                                                     w_gate, w_up, w_down, w_b_out, w_o)
    pad = LANES - N_GROUPS - N_EXPERTS
    w_r = jnp.concatenate([w_expert, w_group, jnp.zeros((d, pad), F32)], axis=1)
    b_r = jnp.concatenate([b_expert, b_group, jnp.zeros((pad,), F32)]).reshape(1, LANES)
    wr = jnp.concatenate(_split_bf16(w_r), axis=1)

    t_chunk = t // MOE_CHUNKS
    assert t == t_chunk * MOE_CHUNKS and all(
        t_chunk % step == 0
        for step in (TQ_ATTN, TM_MIX, TM_CNT * POS_TILES, TM_CMB, SC_WORKERS * SC_CHUNK))
    out = None
    for chunk in range(MOE_CHUNKS):
        merged = _attention(proj, za, sinks, wb, seq, chunk, t_chunk)
        xmid, h2, route, cnt = _mix(merged, xf, wo, row(norm_ffn), wr, b_r, chunk, t_chunk)
        out = _moe_chunk(xmid, h2, route, cnt, w_gate, w_up, w_down, row(norm_final), chunk, t, out)
    return out.reshape(bsz, seq, d)


def _moe_chunk(xmid, h2, route, cnt, w_gate, w_up, w_down, g_final, chunk, t_total, out_prev):
    t = xmid.shape[0]
    n_tiles = t // TM_CNT
    cnt = cnt.reshape(n_tiles, 8, LANES)[:, 0, :N_EXPERTS].astype(I32)
    totals = jnp.sum(cnt, axis=0)
    tiles_e = (totals + TM_EXP - 1) // TM_EXP
    tile_end = jnp.cumsum(tiles_e)
    offset = (tile_end - tiles_e) * TM_EXP
    base = offset[None, :] + jnp.cumsum(cnt, axis=0) - cnt
    base = jnp.pad(base, ((0, 0), (0, LANES - N_EXPERTS))).astype(F32).reshape(n_tiles, 1, LANES)
    rows = t * TOP_K + N_EXPERTS * TM_EXP
    n_active = tile_end[-1]
    tile_id = jnp.arange(rows // TM_EXP, dtype=I32)
    tile_src = jnp.minimum(tile_id, n_active - 1)
    tile_expert = jnp.sum((tile_src[:, None] >= tile_end[None, :]).astype(I32), axis=1)
    tile_expert = jnp.minimum(tile_expert, N_EXPERTS - 1)
    row_in_expert = (tile_id - (tile_end - tiles_e)[tile_expert]) * TM_EXP
    tile_valid = jnp.clip(totals[tile_expert] - row_in_expert, 0, TM_EXP)
    tile_valid = jnp.where(tile_id < n_active, tile_valid, 0).astype(I32)
    after = tile_end[tile_expert]
    next_expert = jnp.where(after < n_active, tile_expert[jnp.minimum(after, n_active - 1)], -1)
    first_of_expert = jnp.concatenate(
        [jnp.ones((1,), I32), (tile_expert[1:] != tile_expert[:-1]).astype(I32)])
    tile_slot = (jnp.cumsum(first_of_expert) - 1) % 2

    pos = _positions(route, base)
    xs = _dispatch(pos, h2, rows)
    ys = _experts(tile_expert.astype(I32), next_expert.astype(I32), tile_slot.astype(I32),
                  tile_src.astype(I32), tile_valid, xs, w_gate, w_up, w_down)
    yg = _gather_rows(ys, pos.reshape(TOP_K * t))
    return _combine(yg, route, xmid, g_final, chunk, t_total, out_prev)
```

```python
import functools
import math

import jax
import jax.numpy as jnp
from jax import lax
from jax.experimental import pallas as pl
from jax.experimental.pallas import tpu as pltpu
from jax.experimental.pallas import tpu_sc as plsc

F32 = jnp.float32
BF16 = jnp.bfloat16
I32 = jnp.int32

D_MODEL = 1024
HEAD_DIM = 64
N_HEADS = 16
N_KV_HEADS = 4
GROUP = N_HEADS // N_KV_HEADS
KV_WIDTH = N_KV_HEADS * HEAD_DIM
WINDOW = 128
N_GROUPS = 4
EXPERTS_PER_GROUP = 8
N_EXPERTS = N_GROUPS * EXPERTS_PER_GROUP
TOP_K = 2
D_FF = 512
EPS = 1e-6
LANES = 128

COL_WIDTH = {"b": D_MODEL, "c": D_MODEL, "u": D_MODEL, "q": D_MODEL, "k": KV_WIDTH, "v": KV_WIDTH,
             "ga": D_MODEL, "gb": D_MODEL}
COL_START = dict(zip(COL_WIDTH, (sum(list(COL_WIDTH.values())[:i]) for i in range(len(COL_WIDTH)))))
IN_COLS = sum(COL_WIDTH.values())
REST_COLS = 2 * D_MODEL + 2 * KV_WIDTH
COL_GB, COL_Q = 0, 1
COL_K, COL_V = 2 * D_MODEL // KV_WIDTH, 2 * D_MODEL // KV_WIDTH + 1

TM_PROJ = 512
SUB_PROJ = 256
W_ROWS = 64
TQ_ATTN = 1024
TM_MIX = 1024
SUB_MIX = 256
TM_CNT = 512
POS_TILES = 8
MOE_CHUNKS = 2
TM_EXP = 512
SUB_EXP = 256
TM_CMB = 2048
HALO_ROWS = 8
VMEM_LIMIT = 56 * 1024 * 1024
PACKED = D_MODEL // 2

SC_CORES = 2
SC_SUBCORES = 16
SC_WORKERS = SC_CORES * SC_SUBCORES
SC_CHUNK = 64


def _rms(x, g):
    r = lax.rsqrt(jnp.mean(x * x, axis=-1, keepdims=True) + EPS)
    return (x * r) * g


def _pack_bf16_pairs(x):
    n = x.shape[1] // 2
    lo = lax.bitcast_convert_type(x[:, :n].astype(BF16).astype(F32), I32)
    hi = lax.bitcast_convert_type(x[:, n:].astype(BF16).astype(F32), I32)
    return (hi & jnp.int32(-65536)) | lax.shift_right_logical(lo, 16)


def _unpack_bf16_pairs(p):
    lo = lax.bitcast_convert_type(lax.shift_left(p, 16), F32)
    hi = lax.bitcast_convert_type(p & jnp.int32(-65536), F32)
    return lo, hi


def _inproj_kernel(x_ref, g_ref, w_hbm, cw_ref, cb_ref, wa_ref,
                   eg_ref, eu_ref, ed_ref, wb_in, wo_in,
                   za_ref, proj_ref, eg_out, eu_out, ed_out, wb_out, wo_out, halo_ref,
                   w_ref, w_stage, w_sem, *, tiles_per_seq):
    i = pl.program_id(0)

    @pl.when(i == 0)
    def _():
        n_copies = D_MODEL // W_ROWS

        def w_copy(c):
            return pltpu.make_async_copy(w_hbm.at[c * W_ROWS:(c + 1) * W_ROWS, :],
                                         w_stage.at[c % 2], w_sem.at[c % 2])

        w_copy(0).start()
        for c in range(n_copies):
            if c + 1 < n_copies:
                w_copy(c + 1).start()
            w_copy(c).wait()
            w_ref[c * W_ROWS:(c + 1) * W_ROWS, :] = w_stage[c % 2].astype(BF16)

    eg_out[...] = eg_ref[...].astype(BF16)
    eu_out[...] = eu_ref[...].astype(BF16)
    ed_out[...] = ed_ref[...].astype(BF16)
    wb_out[...] = wb_in[...].astype(BF16)
    wo_out[...] = wo_in[...].astype(BF16)
    halves = [slice(s * SUB_PROJ, (s + 1) * SUB_PROJ) for s in range(TM_PROJ // SUB_PROJ)]
    hs = [_rms(x_ref[r, :], g_ref[...]).astype(BF16) for r in halves]
    col = lambda a, b=None: slice(COL_START[a], COL_START[b or a] + COL_WIDTH[b or a])
    proj = lambda a, b=None: [jnp.dot(h, w_ref[:, col(a, b)], preferred_element_type=F32) for h in hs]
    pcu = jnp.concatenate(proj("c", "u"), axis=0)
    pb = jnp.concatenate(proj("b"), axis=0)
    pga = jnp.concatenate(proj("ga"), axis=0)
    for r, pgb, pqkv in zip(halves, proj("gb"), proj("q", "v")):
        proj_ref[r, :D_MODEL] = pgb.astype(BF16)
        proj_ref[r, D_MODEL:] = pqkv.astype(BF16)
    cu = pcu[:, :D_MODEL] * pcu[:, D_MODEL:]
    first = (i % tiles_per_seq) == 0
    hist = jnp.where(first, 0.0, halo_ref[...])
    prev1 = hist[HALO_ROWS - 1:HALO_ROWS]
    prev2 = hist[HALO_ROWS - 2:HALO_ROWS - 1]
    halo_ref[...] = cu[TM_PROJ - HALO_ROWS:, :]
    row = lax.broadcasted_iota(I32, cu.shape, 0)
    cu1 = jnp.where(row == 0, prev1, pltpu.roll(cu, 1, 0))
    cu2 = jnp.where(row == 0, prev2, jnp.where(row == 1, prev1, pltpu.roll(cu, 2, 0)))
    cw = cw_ref[...]
    y = cw[0:1] * cu2 + cw[1:2] * cu1 + cw[2:3] * cu + cb_ref[...]
    ya = (pb * y).astype(BF16)
    z = jnp.dot(ya, wa_ref[...], preferred_element_type=F32)
    za_ref[...] = (jax.nn.sigmoid(pga) * z).astype(BF16)


def _inproj(xf, g, w_in, conv_w, conv_b, wa, seq, w_gate, w_up, w_down, w_b, w_o):
    t = xf.shape[0]
    n_tiles = t // TM_PROJ
    const = lambda shape: pl.BlockSpec(shape, lambda i: (0, 0), pipeline_mode=pl.Buffered(1))
    slabs = [w.reshape(-1, w.shape[-1]) for w in (w_gate, w_up, w_down, w_b, w_o)]
    slab_rows = [s.shape[0] // n_tiles for s in slabs]
    assert all(s.shape[0] == r * n_tiles and r % 16 == 0 for s, r in zip(slabs, slab_rows))
    slab_specs = [pl.BlockSpec((r, s.shape[1]), lambda i: (i, 0)) for s, r in zip(slabs, slab_rows)]
    outs = pl.pallas_call(
        functools.partial(_inproj_kernel, tiles_per_seq=seq // TM_PROJ),
        name="inproj_conv",
        grid=(n_tiles,),
        in_specs=[
            pl.BlockSpec((TM_PROJ, D_MODEL), lambda i: (i, 0)),
            const((1, D_MODEL)),
            pl.BlockSpec(memory_space=pl.ANY),
            const((3, D_MODEL)), const((1, D_MODEL)), const((D_MODEL, D_MODEL)),
        ] + slab_specs,
        out_specs=[pl.BlockSpec((TM_PROJ, D_MODEL), lambda i: (i, 0)),
                   pl.BlockSpec((TM_PROJ, REST_COLS), lambda i: (i, 0))] + slab_specs,
        out_shape=[jax.ShapeDtypeStruct((t, D_MODEL), BF16),
                   jax.ShapeDtypeStruct((t, REST_COLS), BF16)]
                  + [jax.ShapeDtypeStruct(s.shape, BF16) for s in slabs],
        scratch_shapes=[pltpu.VMEM((HALO_ROWS, D_MODEL), F32),
                        pltpu.VMEM((D_MODEL, IN_COLS), BF16),
                        pltpu.VMEM((2, W_ROWS, IN_COLS), F32),
                        pltpu.SemaphoreType.DMA((2,))],
        compiler_params=pltpu.CompilerParams(
            dimension_semantics=("arbitrary",), vmem_limit_bytes=VMEM_LIMIT),
    )(xf, g, w_in, conv_w, conv_b, wa, *slabs)
    za, proj, eg, eu, ed, wb, wo = outs
    return za, proj, eg.reshape(w_gate.shape), eu.reshape(w_up.shape), ed.reshape(w_down.shape), wb, wo


def _attn_kernel(sink_ref, q_ref, k_ref, v_ref, kp_ref, vp_ref, gb_ref, za_ref, wb_ref, o_ref,
                 *, first_tile_index, tiles_per_seq):
    first_tile = ((first_tile_index + pl.program_id(0)) % tiles_per_seq) == 0
    ks = lax.broadcasted_iota(I32, (WINDOW, WINDOW), 0)
    qq = lax.broadcasted_iota(I32, (WINDOW, WINDOW), 1)
    own = ks <= qq
    dist = jnp.where(own, qq - ks, qq - ks + WINDOW).astype(F32)
    visible0 = jnp.logical_or(own, jnp.logical_not(first_tile))
    log2e = math.log2(math.e)
    c_scale = log2e / math.sqrt(HEAD_DIM)
    nt = (((1,), (1,)), ((), ()))
    zk = jnp.zeros((2 * WINDOW, HEAD_DIM), BF16)

    def transposed(v_blk):
        return jnp.transpose(v_blk.astype(F32)).astype(BF16)

    def project(rows_p, attn_blk):
        yb = jnp.dot(attn_blk, wb_ref[...], preferred_element_type=F32)
        zb = jax.nn.sigmoid(gb_ref[rows_p, :].astype(F32)) * yb
        o_ref[rows_p, :] = (za_ref[rows_p, :].astype(F32) + zb).astype(BF16)

    pending = None
    prev_k = kp_ref[...]
    prev_vt = transposed(vp_ref[...])
    for sb in range(TQ_ATTN // WINDOW):
        rows = slice(sb * WINDOW, (sb + 1) * WINDOW)
        cur_k = k_ref[rows, :]
        cur_vt = transposed(v_ref[rows, :])
        scores, vcats = [], []
        for kh in range(N_KV_HEADS):
            cols = slice(kh * HEAD_DIM, (kh + 1) * HEAD_DIM)
            kcat = jnp.concatenate([prev_k[:, cols], cur_k[:, cols]], axis=0)
            vcats.append(jnp.concatenate([prev_vt[cols, :], cur_vt[cols, :]], axis=1))
            qg = jnp.concatenate([q_ref[rows, (2 * kh) * LANES:(2 * kh + 1) * LANES],
                                  q_ref[rows, (2 * kh + 1) * LANES:(2 * kh + 2) * LANES]], axis=0)
            k_pad = jnp.concatenate([jnp.concatenate([kcat, zk], axis=1),
                                     jnp.concatenate([zk, kcat], axis=1)], axis=0)
            scores.append(lax.dot_general(k_pad, qg, nt, preferred_element_type=F32))
        if pending is not None:
            project(*pending)
        probs, rdens = [], []
        for kh in range(N_KV_HEADS):
            for pos in range(2):
                pr, rd = [], []
                for half in range(2):
                    h = kh * GROUP + 2 * half + pos
                    slope = 2.0 ** (-8.0 * (h + 1) / N_HEADS)
                    qcols = slice(half * WINDOW, (half + 1) * WINDOW)
                    krow = pos * 2 * WINDOW
                    st = scores[kh]
                    s = (jnp.where(own, st[krow + WINDOW:krow + 2 * WINDOW, qcols],
                                   st[krow:krow + WINDOW, qcols]) * c_scale
                         - (slope * log2e) * dist)
                    if sb == 0:
                        s = jnp.where(visible0, s, -jnp.inf)
                    m = jnp.max(s, axis=0, keepdims=True)
                    p = jnp.exp2(s - m)
                    den = jnp.sum(p, axis=0, keepdims=True) + jnp.exp2(sink_ref[h] * log2e - m)
                    rd.append(1.0 / den)
                    pr.append(jnp.concatenate(
                        [jnp.where(own, 0.0, p).astype(BF16), jnp.where(own, p, 0.0).astype(BF16)],
                        axis=0))
                probs.append(jnp.concatenate(pr, axis=1))
                rdens.append(jnp.concatenate(rd, axis=1))
        out_t = [None] * N_HEADS
        for kh in range(N_KV_HEADS):
            for pos in range(2):
                o2 = jnp.dot(vcats[kh], probs[2 * kh + pos], preferred_element_type=F32)
                o2 = o2 * rdens[2 * kh + pos]
                out_t[kh * GROUP + pos] = o2[:, :WINDOW]
                out_t[kh * GROUP + 2 + pos] = o2[:, WINDOW:]
        pending = (rows, jnp.transpose(jnp.concatenate(out_t, axis=0)).astype(BF16))
        prev_k, prev_vt = cur_k, cur_vt
    project(*pending)


def _attention(proj, za, sinks, wb, seq, chunk, t):
    sub = TQ_ATTN // WINDOW
    n_tiles = t // TQ_ATTN
    first = chunk * n_tiles
    src = lambda w, c: pl.BlockSpec((TQ_ATTN, w), lambda i: (first + i, c))
    prev = lambda c: pl.BlockSpec((WINDOW, KV_WIDTH),
                                  lambda i: (jnp.maximum((first + i) * sub - 1, 0), c))
    return pl.pallas_call(
        functools.partial(_attn_kernel, first_tile_index=first, tiles_per_seq=seq // TQ_ATTN),
        name="swattn",
        grid=(n_tiles,),
        in_specs=[
            pl.BlockSpec(memory_space=pltpu.SMEM),
            src(D_MODEL, COL_Q), src(KV_WIDTH, COL_K), src(KV_WIDTH, COL_V), prev(COL_K), prev(COL_V),
            src(D_MODEL, COL_GB), src(D_MODEL, 0),
            pl.BlockSpec((D_MODEL, D_MODEL), lambda i: (0, 0), pipeline_mode=pl.Buffered(1)),
        ],
        out_specs=pl.BlockSpec((TQ_ATTN, D_MODEL), lambda i: (i, 0)),
        out_shape=jax.ShapeDtypeStruct((t, D_MODEL), BF16),
        compiler_params=pltpu.CompilerParams(
            dimension_semantics=("arbitrary",), vmem_limit_bytes=VMEM_LIMIT),
    )(sinks, proj, proj, proj, proj, proj, proj, za, wb)


def _mix_kernel(merged_ref, x_ref, wo_ref, g_ref, wr_ref, br_ref,
                xmid_ref, h_ref, route_ref, cnt_ref):
    subs = [slice(s * SUB_MIX, (s + 1) * SUB_MIX) for s in range(TM_MIX // SUB_MIX)]
    xm = [x_ref[r, :] + jnp.dot(merged_ref[r, :], wo_ref[...], preferred_element_type=F32)
          for r in subs]
    hs = []
    for r, v in zip(subs, xm):
        xmid_ref[r, :] = _pack_bf16_pairs(v)
        h = _rms(v, g_ref[...])
        h_ref[r, :] = _pack_bf16_pairs(h)
        hs.append(h)
    wr = wr_ref[...]
    logits = []
    for h in hs:
        h_hi = h.astype(BF16)
        h_lo = (h - h_hi.astype(F32)).astype(BF16)
        both = jnp.dot(h_hi, wr, preferred_element_type=F32)
        logits.append(both[:, :LANES] + both[:, LANES:]
                      + jnp.dot(h_lo, wr[:, :LANES], preferred_element_type=F32) + br_ref[...])
    subs_per_cnt = TM_CNT // SUB_MIX
    cnts = [jnp.zeros((8, LANES), F32) for _ in range(TM_MIX // TM_CNT)]
    for s, (r, lg) in enumerate(zip(subs, logits)):
        route, cnt = _route(lg)
        route_ref[r, :] = route
        cnts[s // subs_per_cnt] = cnts[s // subs_per_cnt] + cnt
    for c, cnt in enumerate(cnts):
        cnt_ref[c * 8:(c + 1) * 8, :] = cnt


def _route(logits):
    n = logits.shape[0]
    lt = jnp.transpose(logits)
    sub = lax.broadcasted_iota(I32, (EXPERTS_PER_GROUP, n), 0)
    neg = -jnp.inf
    gl = jnp.where(sub < N_GROUPS, lt[N_EXPERTS:N_EXPERTS + EXPERTS_PER_GROUP], neg)
    gmax = jnp.max(gl, axis=0, keepdims=True)
    g_idx = jnp.min(jnp.where(gl == gmax, sub, EXPERTS_PER_GROUP), axis=0, keepdims=True)
    p_g = 1.0 / jnp.sum(jnp.exp(gl - gmax), axis=0, keepdims=True)
    v1 = v2 = i1 = i2 = None
    for g in range(N_GROUPS):
        eg = lt[g * EXPERTS_PER_GROUP:(g + 1) * EXPERTS_PER_GROUP]
        a1 = jnp.max(eg, axis=0, keepdims=True)
        j1 = jnp.min(jnp.where(eg == a1, sub, EXPERTS_PER_GROUP), axis=0, keepdims=True)
        eg2 = jnp.where(sub == j1, neg, eg)
        a2 = jnp.max(eg2, axis=0, keepdims=True)
        j2 = jnp.min(jnp.where(eg2 == a2, sub, EXPERTS_PER_GROUP), axis=0, keepdims=True)
        if g == 0:
            v1, v2, i1, i2 = a1, a2, j1, j2
        else:
            chosen = g_idx == g
            v1, v2 = jnp.where(chosen, a1, v1), jnp.where(chosen, a2, v2)
            i1, i2 = jnp.where(chosen, j1, i1), jnp.where(chosen, j2, i2)
    e21 = jnp.exp(v2 - v1)
    w1 = p_g / (1.0 + e21)
    w2 = p_g * e21 / (1.0 + e21)
    e1 = g_idx * EXPERTS_PER_GROUP + i1
    e2 = g_idx * EXPERTS_PER_GROUP + i2
    rows8 = jnp.where(sub == 0, e1.astype(F32),
                      jnp.where(sub == 1, e2.astype(F32),
                                jnp.where(sub == 2, w1, jnp.where(sub == 3, w2, 0.0))))
    route_t = jnp.concatenate([rows8, jnp.zeros((LANES - EXPERTS_PER_GROUP, n), F32)], axis=0)
    expert_row = lax.broadcasted_iota(I32, (LANES, n), 0)
    onehot_t = ((expert_row == e1) | (expert_row == e2)).astype(BF16)
    cnt = lax.dot_general(jnp.ones((8, n), BF16), onehot_t, (((1,), (1,)), ((), ())),
                          preferred_element_type=F32)
    return jnp.transpose(route_t), cnt


def _mix(merged, xf, wo, g, wr, br, chunk, t):
    n_tiles = t // TM_MIX
    first = chunk * n_tiles
    cnt_rows = TM_MIX // TM_CNT * 8
    full = lambda shape: pl.BlockSpec(shape, lambda i: (0, 0), pipeline_mode=pl.Buffered(1))
    tile = lambda w=D_MODEL: pl.BlockSpec((TM_MIX, w), lambda i: (i, 0))
    return pl.pallas_call(
        _mix_kernel,
        name="merge_router",
        grid=(n_tiles,),
        in_specs=[
            tile(), pl.BlockSpec((TM_MIX, D_MODEL), lambda i: (first + i, 0)),
            full((D_MODEL, D_MODEL)), full((1, D_MODEL)),
            full((D_MODEL, 2 * LANES)), full((1, LANES)),
        ],
        out_specs=[tile(PACKED), tile(PACKED), tile(LANES),
                   pl.BlockSpec((cnt_rows, LANES), lambda i: (i, 0))],
        out_shape=[
            jax.ShapeDtypeStruct((t, PACKED), I32),
            jax.ShapeDtypeStruct((t, PACKED), I32),
            jax.ShapeDtypeStruct((t, LANES), F32),
            jax.ShapeDtypeStruct((n_tiles * cnt_rows, LANES), F32),
        ],
        compiler_params=pltpu.CompilerParams(
            dimension_semantics=("arbitrary",), vmem_limit_bytes=VMEM_LIMIT),
    )(merged, xf, wo, g, wr, br)


def _pos_kernel(route_ref, base_ref, pos_ref):
    lane = lax.broadcasted_iota(I32, (TM_CNT, LANES), 1)
    r = lax.broadcasted_iota(I32, (TM_CNT, TM_CNT), 0)
    c = lax.broadcasted_iota(I32, (TM_CNT, TM_CNT), 1)
    lower = (c < r).astype(BF16)
    subs = [slice(s * TM_CNT, (s + 1) * TM_CNT) for s in range(POS_TILES)]
    routes = [route_ref[rs, :] for rs in subs]
    e1 = [jnp.sum(jnp.where(lane == 0, rt, 0.0), axis=-1, keepdims=True).astype(I32) for rt in routes]
    e2 = [jnp.sum(jnp.where(lane == 1, rt, 0.0), axis=-1, keepdims=True).astype(I32) for rt in routes]
    onehot = [((lane == a) | (lane == b)).astype(BF16) for a, b in zip(e1, e2)]
    before = [jnp.dot(lower, oh, preferred_element_type=F32) + base_ref[s]
              for s, oh in enumerate(onehot)]
    for s, rs in enumerate(subs):
        p1 = jnp.sum(jnp.where(lane == e1[s], before[s], 0.0), axis=-1, keepdims=True)
        p2 = jnp.sum(jnp.where(lane == e2[s], before[s], 0.0), axis=-1, keepdims=True)
        packed = jnp.where(lane == 0, p1, jnp.where(lane == 1, p2, 0.0))
        pos_ref[:, rs] = jnp.transpose(packed)[0:TOP_K, :].astype(I32)


def _positions(route, base):
    t = route.shape[0]
    n_steps = t // (TM_CNT * POS_TILES)
    return pl.pallas_call(
        _pos_kernel,
        name="positions",
        grid=(n_steps,),
        in_specs=[
            pl.BlockSpec((TM_CNT * POS_TILES, LANES), lambda i: (i, 0)),
            pl.BlockSpec((POS_TILES, 1, LANES), lambda i: (i, 0, 0)),
        ],
        out_specs=pl.BlockSpec((TOP_K, TM_CNT * POS_TILES), lambda i: (0, i)),
        out_shape=jax.ShapeDtypeStruct((TOP_K, t), I32),
        compiler_params=pltpu.CompilerParams(dimension_semantics=("arbitrary",)),
    )(route, base)


def _sc_mesh():
    return plsc.VectorSubcoreMesh(core_axis_name="c", subcore_axis_name="s",
                                  num_cores=SC_CORES, num_subcores=SC_SUBCORES)


def _sc_worker():
    return lax.axis_index("s") * SC_CORES + lax.axis_index("c")


def _dispatch(pos, hp, rows):
    t = hp.shape[0]
    per_w = t // SC_WORKERS
    n_ch = per_w // SC_CHUNK
    pos4 = pos.reshape(TOP_K, SC_WORKERS, n_ch, SC_CHUNK)

    @functools.partial(
        pl.kernel, mesh=_sc_mesh(),
        out_type=jax.ShapeDtypeStruct((rows, PACKED), I32),
        scratch_types=[pltpu.VMEM((TOP_K, n_ch, SC_CHUNK), I32),
                       pltpu.VMEM((SC_CHUNK, PACKED), I32)])
    def scatter(hp_hbm, pos_hbm, xs_hbm, idx_v, rows_v):
        wid = _sc_worker()
        for k in range(TOP_K):
            pltpu.sync_copy(pos_hbm.at[k, wid], idx_v.at[k])

        def body(c, carry):
            start = pl.multiple_of(wid * per_w + c * SC_CHUNK, SC_CHUNK)
            pltpu.sync_copy(hp_hbm.at[pl.ds(start, SC_CHUNK)], rows_v)
            for k in range(TOP_K):
                pltpu.sync_copy(rows_v, xs_hbm.at[idx_v.at[k, c]])
            return carry

        lax.fori_loop(0, n_ch, body, 0)

    return scatter(hp, pos4)


def _gather_rows(table, idx):
    n = idx.shape[0]
    per_w = n // SC_WORKERS
    n_ch = per_w // SC_CHUNK
    idx3 = idx.reshape(SC_WORKERS, n_ch, SC_CHUNK)

    @functools.partial(
        pl.kernel, mesh=_sc_mesh(),
        out_type=jax.ShapeDtypeStruct((n, PACKED), I32),
        scratch_types=[pltpu.VMEM((n_ch, SC_CHUNK), I32),
                       pltpu.VMEM((SC_CHUNK, PACKED), I32)])
    def gather(table_hbm, idx_hbm, out_hbm, idx_v, rows_v):
        wid = _sc_worker()
        pltpu.sync_copy(idx_hbm.at[wid], idx_v)

        def body(c, carry):
            start = pl.multiple_of(wid * per_w + c * SC_CHUNK, SC_CHUNK)
            pltpu.sync_copy(table_hbm.at[idx_v.at[c]], rows_v)
            pltpu.sync_copy(rows_v, out_hbm.at[pl.ds(start, SC_CHUNK)])
            return carry

        lax.fori_loop(0, n_ch, body, 0)

    return gather(table, idx3)


def _expert_kernel(te_ref, nx_ref, sl_ref, ts_ref, tv_ref, xs_ref, wg_hbm, wu_hbm, wd_hbm, o_ref,
                   wg_s, wu_s, wd_s, sem):
    del ts_ref
    i = pl.program_id(0)
    n_valid = tv_ref[i]
    expert = te_ref[i]
    slot = sl_ref[i]

    def weight_copies(e, s):
        return (pltpu.make_async_copy(wg_hbm.at[e], wg_s.at[s], sem.at[s, 0]),
                pltpu.make_async_copy(wu_hbm.at[e], wu_s.at[s], sem.at[s, 1]),
                pltpu.make_async_copy(wd_hbm.at[e], wd_s.at[s], sem.at[s, 2]))

    @pl.when(i == 0)
    def _():
        for copy in weight_copies(expert, slot):
            copy.start()

    @pl.when(jnp.logical_or(i == 0, expert != te_ref[jnp.maximum(i - 1, 0)]))
    def _():
        for copy in weight_copies(expert, slot):
            copy.wait()
        nxt = nx_ref[i]

        @pl.when(nxt >= 0)
        def _():
            for copy in weight_copies(nxt, 1 - slot):
                copy.start()

    def mlp(n_sub):
        wg, wu, wd = wg_s.at[slot], wu_s.at[slot], wd_s.at[slot]
        subs = [slice(s * SUB_EXP, (s + 1) * SUB_EXP) for s in range(n_sub)]
        xin = []
        for r in subs:
            rid = r.start + lax.broadcasted_iota(I32, (SUB_EXP, PACKED), 0)
            lo, hi = _unpack_bf16_pairs(jnp.where(rid < n_valid, xs_ref[r, :], 0))
            xin.append((lo.astype(BF16), hi.astype(BF16)))
        ab = [(jnp.dot(lo, wg[:PACKED, :], preferred_element_type=F32)
               + jnp.dot(hi, wg[PACKED:, :], preferred_element_type=F32),
               jnp.dot(lo, wu[:PACKED, :], preferred_element_type=F32)
               + jnp.dot(hi, wu[PACKED:, :], preferred_element_type=F32)) for lo, hi in xin]
        for r, (ai, bi) in zip(subs, ab):
            hm = (ai * jax.nn.sigmoid(ai) * bi).astype(BF16)
            o_ref[r, :] = _pack_bf16_pairs(jnp.dot(hm, wd[...], preferred_element_type=F32))
        if n_sub * SUB_EXP < TM_EXP:
            o_ref[n_sub * SUB_EXP:, :] = jnp.zeros((TM_EXP - n_sub * SUB_EXP, PACKED), I32)

    n_subs = TM_EXP // SUB_EXP
    for n_sub in range(n_subs + 1):
        lo_rows = (n_sub - 1) * SUB_EXP if n_sub else -1
        in_range = jnp.logical_and(n_valid > lo_rows, n_valid <= n_sub * SUB_EXP)
        pl.when(in_range)(functools.partial(mlp, n_sub))


def _experts(tile_expert, next_expert, tile_slot, tile_src, tile_valid, xs, wg, wu, wd):
    rows = xs.shape[0]
    hbm = pl.BlockSpec(memory_space=pl.ANY)
    grid_spec = pltpu.PrefetchScalarGridSpec(
        num_scalar_prefetch=5,
        grid=(rows // TM_EXP,),
        in_specs=[pl.BlockSpec((TM_EXP, PACKED), lambda i, te, nx, sl, ts, tv: (ts[i], 0)),
                  hbm, hbm, hbm],
        out_specs=pl.BlockSpec((TM_EXP, PACKED), lambda i, te, nx, sl, ts, tv: (i, 0)),
        scratch_shapes=[
            pltpu.VMEM((2, D_MODEL, D_FF), BF16), pltpu.VMEM((2, D_MODEL, D_FF), BF16),
            pltpu.VMEM((2, D_FF, D_MODEL), BF16),
            pltpu.SemaphoreType.DMA((2, 3)),
        ],
    )
    return pl.pallas_call(
        _expert_kernel,
        name="experts",
        grid_spec=grid_spec,
        out_shape=jax.ShapeDtypeStruct((rows, PACKED), I32),
        compiler_params=pltpu.CompilerParams(
            dimension_semantics=("arbitrary",), vmem_limit_bytes=VMEM_LIMIT),
    )(tile_expert, next_expert, tile_slot, tile_src, tile_valid, xs, wg, wu, wd)


def _combine_kernel(y1_ref, y2_ref, route_ref, xmid_ref, g_ref, *rest):
    o_ref = rest[-1]
    route = route_ref[...]
    lane = lax.broadcasted_iota(I32, route.shape, 1)
    w1 = jnp.sum(jnp.where(lane == 2, route, 0.0), axis=-1, keepdims=True)
    w2 = jnp.sum(jnp.where(lane == 3, route, 0.0), axis=-1, keepdims=True)
    lo1, hi1 = _unpack_bf16_pairs(y1_ref[...])
    lo2, hi2 = _unpack_bf16_pairs(y2_ref[...])
    lox, hix = _unpack_bf16_pairs(xmid_ref[...])
    x_out = jnp.concatenate([lox + (lo1 * w1 + lo2 * w2), hix + (hi1 * w1 + hi2 * w2)], axis=1)
    o_ref[...] = _rms(x_out, g_ref[...])


def _combine(yg, route, xmid, g, chunk, t_total, out_prev):
    t = xmid.shape[0]
    n_tiles = t // TM_CMB
    first = chunk * n_tiles
    in_specs = [
        pl.BlockSpec((TM_CMB, PACKED), lambda i: (i, 0)),
        pl.BlockSpec((TM_CMB, PACKED), lambda i: (n_tiles + i, 0)),
        pl.BlockSpec((TM_CMB, LANES), lambda i: (i, 0)),
        pl.BlockSpec((TM_CMB, PACKED), lambda i: (i, 0)),
        pl.BlockSpec((1, D_MODEL), lambda i: (0, 0)),
    ]
    args = [yg, yg, route, xmid, g]
    aliases = {}
    if out_prev is not None:
        in_specs.append(pl.BlockSpec(memory_space=pl.ANY))
        aliases = {len(args): 0}
        args.append(out_prev)
    return pl.pallas_call(
        _combine_kernel,
        name="combine",
        grid=(n_tiles,),
        in_specs=in_specs,
        out_specs=pl.BlockSpec((TM_CMB, D_MODEL), lambda i: (first + i, 0)),
        out_shape=jax.ShapeDtypeStruct((t_total, D_MODEL), F32),
        input_output_aliases=aliases,
        compiler_params=pltpu.CompilerParams(
            dimension_semantics=("arbitrary",), vmem_limit_bytes=VMEM_LIMIT),
    )(*args)


def _split_bf16(w):
    hi = w.astype(BF16)
    lo = (w - hi.astype(F32)).astype(BF16)
    return hi, lo


def kernel(x, norm_mix, w_in, conv_w, conv_b, w_a_out, sinks, w_b_out, w_o, norm_ffn, w_group,
           b_group, w_expert, b_expert, w_gate, w_up, w_down, norm_final):
    bsz, seq, d = x.shape
    t = bsz * seq
    assert d == D_MODEL and seq % TM_PROJ == 0 and seq % TQ_ATTN == 0
    xf = x.reshape(t, d)
    row = lambda v: v.reshape(1, -1)

    assert w_in.shape == (D_MODEL, IN_COLS)
    za, proj, w_gate, w_up, w_down, wb, wo = _inproj(xf, row(norm_mix), w_in.astype(F32), conv_w,
                                                     row(conv_b), w_a_out.astype(BF16), seq,
                                                     w_gate, w_up, w_down, w_b_out, w_o)
    pad = LANES - N_GROUPS - N_EXPERTS
    w_r = jnp.concatenate([w_expert, w_group, jnp.zeros((d, pad), F32)], axis=1)
    b_r = jnp.concatenate([b_expert, b_group, jnp.zeros((pad,), F32)]).reshape(1, LANES)
    wr = jnp.concatenate(_split_bf16(w_r), axis=1)

    t_chunk = t // MOE_CHUNKS
    assert t == t_chunk * MOE_CHUNKS and all(
        t_chunk % step == 0
        for step in (TQ_ATTN, TM_MIX, TM_CNT * POS_TILES, TM_CMB, SC_WORKERS * SC_CHUNK))
    out = None
    for chunk in range(MOE_CHUNKS):
        merged = _attention(proj, za, sinks, wb, seq, chunk, t_chunk)
        xmid, h2, route, cnt = _mix(merged, xf, wo, row(norm_ffn), wr, b_r, chunk, t_chunk)
        out = _moe_chunk(xmid, h2, route, cnt, w_gate, w_up, w_down, row(norm_final), chunk, t, out)
    return out.reshape(bsz, seq, d)


def _moe_chunk(xmid, h2, route, cnt, w_gate, w_up, w_down, g_final, chunk, t_total, out_prev):
    t = xmid.shape[0]
    n_tiles = t // TM_CNT
    cnt = cnt.reshape(n_tiles, 8, LANES)[:, 0, :N_EXPERTS].astype(I32)
    totals = jnp.sum(cnt, axis=0)
    tiles_e = (totals + TM_EXP - 1) // TM_EXP
    tile_end = jnp.cumsum(tiles_e)
    offset = (tile_end - tiles_e) * TM_EXP
    base = offset[None, :] + jnp.cumsum(cnt, axis=0) - cnt
    base = jnp.pad(base, ((0, 0), (0, LANES - N_EXPERTS))).astype(F32).reshape(n_tiles, 1, LANES)
    rows = t * TOP_K + N_EXPERTS * TM_EXP
    n_active = tile_end[-1]
    tile_id = jnp.arange(rows // TM_EXP, dtype=I32)
    tile_src = jnp.minimum(tile_id, n_active - 1)
    tile_expert = jnp.sum((tile_src[:, None] >= tile_end[None, :]).astype(I32), axis=1)
    tile_expert = jnp.minimum(tile_expert, N_EXPERTS - 1)
    row_in_expert = (tile_id - (tile_end - tiles_e)[tile_expert]) * TM_EXP
    tile_valid = jnp.clip(totals[tile_expert] - row_in_expert, 0, TM_EXP)
    tile_valid = jnp.where(tile_id < n_active, tile_valid, 0).astype(I32)
    after = tile_end[tile_expert]
    next_expert = jnp.where(after < n_active, tile_expert[jnp.minimum(after, n_active - 1)], -1)
    first_of_expert = jnp.concatenate(
        [jnp.ones((1,), I32), (tile_expert[1:] != tile_expert[:-1]).astype(I32)])
    tile_slot = (jnp.cumsum(first_of_expert) - 1) % 2

    pos = _positions(route, base)
    xs = _dispatch(pos, h2, rows)
    ys = _experts(tile_expert.astype(I32), next_expert.astype(I32), tile_slot.astype(I32),
                  tile_src.astype(I32), tile_valid, xs, w_gate, w_up, w_down)
    yg = _gather_rows(ys, pos.reshape(TOP_K * t))
    return _combine(yg, route, xmid, g_final, chunk, t_total, out_prev)
```

```python
import functools
import math

import jax
import jax.numpy as jnp
from jax import lax
from jax.experimental import pallas as pl
from jax.experimental.pallas import tpu as pltpu
from jax.experimental.pallas import tpu_sc as plsc

F32 = jnp.float32
BF16 = jnp.bfloat16
I32 = jnp.int32

D_MODEL = 1024
HEAD_DIM = 64
N_HEADS = 16
N_KV_HEADS = 4
GROUP = N_HEADS // N_KV_HEADS
KV_WIDTH = N_KV_HEADS * HEAD_DIM
WINDOW = 128
N_GROUPS = 4
EXPERTS_PER_GROUP = 8
N_EXPERTS = N_GROUPS * EXPERTS_PER_GROUP
TOP_K = 2
D_FF = 512
EPS = 1e-6
LANES = 128

COL_WIDTH = {"b": D_MODEL, "c": D_MODEL, "u": D_MODEL, "q": D_MODEL, "k": KV_WIDTH, "v": KV_WIDTH,
             "ga": D_MODEL, "gb": D_MODEL}
COL_START = dict(zip(COL_WIDTH, (sum(list(COL_WIDTH.values())[:i]) for i in range(len(COL_WIDTH)))))
IN_COLS = sum(COL_WIDTH.values())
REST_COLS = 2 * D_MODEL + 2 * KV_WIDTH
COL_GB, COL_Q = 0, 1
COL_K, COL_V = 2 * D_MODEL // KV_WIDTH, 2 * D_MODEL // KV_WIDTH + 1

TM_PROJ = 512
SUB_PROJ = 256
W_ROWS = 64
TQ_ATTN = 1024
TM_MIX = 1024
SUB_MIX = 256
TM_CNT = 512
POS_TILES = 8
MOE_CHUNKS = 2
TM_EXP = 512
SUB_EXP = 256
TM_CMB = 2048
HALO_ROWS = 8
VMEM_LIMIT = 56 * 1024 * 1024
PACKED = D_MODEL // 2

SC_CORES = 2
SC_SUBCORES = 16
SC_WORKERS = SC_CORES * SC_SUBCORES
SC_CHUNK = 64


def _rms(x, g):
    r = lax.rsqrt(jnp.mean(x * x, axis=-1, keepdims=True) + EPS)
    return (x * r) * g


def _pack_bf16_pairs(x):
    n = x.shape[1] // 2
    lo = lax.bitcast_convert_type(x[:, :n].astype(BF16).astype(F32), I32)
    hi = lax.bitcast_convert_type(x[:, n:].astype(BF16).astype(F32), I32)
    return (hi & jnp.int32(-65536)) | lax.shift_right_logical(lo, 16)


def _unpack_bf16_pairs(p):
    lo = lax.bitcast_convert_type(lax.shift_left(p, 16), F32)
    hi = lax.bitcast_convert_type(p & jnp.int32(-65536), F32)
    return lo, hi


def _inproj_kernel(x_ref, g_ref, w_hbm, cw_ref, cb_ref, wa_in,
                   eg_ref, eu_ref, ed_ref, wb_in, wo_in,
                   za_ref, proj_ref, eg_out, eu_out, ed_out, wb_out, wo_out, halo_ref,
                   w_ref, w_stage, w_sem, wa_ref, *, tiles_per_seq):
    i = pl.program_id(0)

    @pl.when(i == 0)
    def _():
        n_copies = D_MODEL // W_ROWS

        def w_copy(c):
            return pltpu.make_async_copy(w_hbm.at[c * W_ROWS:(c + 1) * W_ROWS, :],
                                         w_stage.at[c % 2], w_sem.at[c % 2])

        w_copy(0).start()
        wa_ref[...] = wa_in[...].astype(BF16)
        for c in range(n_copies):
            if c + 1 < n_copies:
                w_copy(c + 1).start()
            w_copy(c).wait()
            w_ref[c * W_ROWS:(c + 1) * W_ROWS, :] = w_stage[c % 2].astype(BF16)

    eg_out[...] = eg_ref[...].astype(BF16)
    eu_out[...] = eu_ref[...].astype(BF16)
    ed_out[...] = ed_ref[...].astype(BF16)
    wb_out[...] = wb_in[...].astype(BF16)
    wo_out[...] = wo_in[...].astype(BF16)
    halves = [slice(s * SUB_PROJ, (s + 1) * SUB_PROJ) for s in range(TM_PROJ // SUB_PROJ)]
    hs = [_rms(x_ref[r, :], g_ref[...]).astype(BF16) for r in halves]
    col = lambda a, b=None: slice(COL_START[a], COL_START[b or a] + COL_WIDTH[b or a])
    proj = lambda a, b=None: [jnp.dot(h, w_ref[:, col(a, b)], preferred_element_type=F32) for h in hs]
    pcu = jnp.concatenate(proj("c", "u"), axis=0)
    pb = jnp.concatenate(proj("b"), axis=0)
    pga = jnp.concatenate(proj("ga"), axis=0)
    for r, pgb, pqkv in zip(halves, proj("gb"), proj("q", "v")):
        proj_ref[r, :D_MODEL] = pgb.astype(BF16)
        proj_ref[r, D_MODEL:] = pqkv.astype(BF16)
    cu = pcu[:, :D_MODEL] * pcu[:, D_MODEL:]
    first = (i % tiles_per_seq) == 0
    hist = jnp.where(first, 0.0, halo_ref[...])
    prev1 = hist[HALO_ROWS - 1:HALO_ROWS]
    prev2 = hist[HALO_ROWS - 2:HALO_ROWS - 1]
    halo_ref[...] = cu[TM_PROJ - HALO_ROWS:, :]
    row = lax.broadcasted_iota(I32, cu.shape, 0)
    cu1 = jnp.where(row == 0, prev1, pltpu.roll(cu, 1, 0))
    cu2 = jnp.where(row == 0, prev2, jnp.where(row == 1, prev1, pltpu.roll(cu, 2, 0)))
    cw = cw_ref[...]
    y = cw[0:1] * cu2 + cw[1:2] * cu1 + cw[2:3] * cu + cb_ref[...]
    ya = (pb * y).astype(BF16)
    z = jnp.dot(ya, wa_ref[...], preferred_element_type=F32)
    za_ref[...] = (jax.nn.sigmoid(pga) * z).astype(BF16)


def _inproj(xf, g, w_in, conv_w, conv_b, wa, seq, w_gate, w_up, w_down, w_b, w_o):
    t = xf.shape[0]
    n_tiles = t // TM_PROJ
    const = lambda shape: pl.BlockSpec(shape, lambda i: (0, 0), pipeline_mode=pl.Buffered(1))
    slabs = [w.reshape(-1, w.shape[-1]) for w in (w_gate, w_up, w_down, w_b, w_o)]
    slab_rows = [s.shape[0] // n_tiles for s in slabs]
    assert all(s.shape[0] == r * n_tiles and r % 16 == 0 for s, r in zip(slabs, slab_rows))
    slab_specs = [pl.BlockSpec((r, s.shape[1]), lambda i: (i, 0)) for s, r in zip(slabs, slab_rows)]
    outs = pl.pallas_call(
        functools.partial(_inproj_kernel, tiles_per_seq=seq // TM_PROJ),
        name="inproj_conv",
        grid=(n_tiles,),
        in_specs=[
            pl.BlockSpec((TM_PROJ, D_MODEL), lambda i: (i, 0)),
            const((1, D_MODEL)),
            pl.BlockSpec(memory_space=pl.ANY),
            const((3, D_MODEL)), const((1, D_MODEL)), const((D_MODEL, D_MODEL)),
        ] + slab_specs,
        out_specs=[pl.BlockSpec((TM_PROJ, D_MODEL), lambda i: (i, 0)),
                   pl.BlockSpec((TM_PROJ, REST_COLS), lambda i: (i, 0))] + slab_specs,
        out_shape=[jax.ShapeDtypeStruct((t, D_MODEL), BF16),
                   jax.ShapeDtypeStruct((t, REST_COLS), BF16)]
                  + [jax.ShapeDtypeStruct(s.shape, BF16) for s in slabs],
        scratch_shapes=[pltpu.VMEM((HALO_ROWS, D_MODEL), F32),
                        pltpu.VMEM((D_MODEL, IN_COLS), BF16),
                        pltpu.VMEM((2, W_ROWS, IN_COLS), F32),
                        pltpu.SemaphoreType.DMA((2,)),
                        pltpu.VMEM((D_MODEL, D_MODEL), BF16)],
        compiler_params=pltpu.CompilerParams(
            dimension_semantics=("arbitrary",), vmem_limit_bytes=VMEM_LIMIT),
    )(xf, g, w_in, conv_w, conv_b, wa, *slabs)
    za, proj, eg, eu, ed, wb, wo = outs
    return za, proj, eg.reshape(w_gate.shape), eu.reshape(w_up.shape), ed.reshape(w_down.shape), wb, wo


def _attn_kernel(sink_ref, q_ref, k_ref, v_ref, kp_ref, vp_ref, gb_ref, za_ref, wb_ref, o_ref,
                 *, first_tile_index, tiles_per_seq):
    first_tile = ((first_tile_index + pl.program_id(0)) % tiles_per_seq) == 0
    ks = lax.broadcasted_iota(I32, (WINDOW, WINDOW), 0)
    qq = lax.broadcasted_iota(I32, (WINDOW, WINDOW), 1)
    own = ks <= qq
    dist = jnp.where(own, qq - ks, qq - ks + WINDOW).astype(F32)
    visible0 = jnp.logical_or(own, jnp.logical_not(first_tile))
    log2e = math.log2(math.e)
    c_scale = log2e / math.sqrt(HEAD_DIM)
    nt = (((1,), (1,)), ((), ()))
    zk = jnp.zeros((2 * WINDOW, HEAD_DIM), BF16)

    def transposed(v_blk):
        return jnp.transpose(v_blk.astype(F32)).astype(BF16)

    def project(rows_p, attn_blk):
        yb = jnp.dot(attn_blk, wb_ref[...], preferred_element_type=F32)
        zb = jax.nn.sigmoid(gb_ref[rows_p, :].astype(F32)) * yb
        o_ref[rows_p, :] = (za_ref[rows_p, :].astype(F32) + zb).astype(BF16)

    pending = None
    prev_k = kp_ref[...]
    prev_vt = transposed(vp_ref[...])
    for sb in range(TQ_ATTN // WINDOW):
        rows = slice(sb * WINDOW, (sb + 1) * WINDOW)
        cur_k = k_ref[rows, :]
        cur_vt = transposed(v_ref[rows, :])
        scores, vcats = [], []
        for kh in range(N_KV_HEADS):
            cols = slice(kh * HEAD_DIM, (kh + 1) * HEAD_DIM)
            kcat = jnp.concatenate([prev_k[:, cols], cur_k[:, cols]], axis=0)
            vcats.append(jnp.concatenate([prev_vt[cols, :], cur_vt[cols, :]], axis=1))
            qg = jnp.concatenate([q_ref[rows, (2 * kh) * LANES:(2 * kh + 1) * LANES],
                                  q_ref[rows, (2 * kh + 1) * LANES:(2 * kh + 2) * LANES]], axis=0)
            k_pad = jnp.concatenate([jnp.concatenate([kcat, zk], axis=1),
                                     jnp.concatenate([zk, kcat], axis=1)], axis=0)
            scores.append(lax.dot_general(k_pad, qg, nt, preferred_element_type=F32))
        if pending is not None:
            project(*pending)
        probs, rdens = [], []
        for kh in range(N_KV_HEADS):
            for pos in range(2):
                pr, rd = [], []
                for half in range(2):
                    h = kh * GROUP + 2 * half + pos
                    slope = 2.0 ** (-8.0 * (h + 1) / N_HEADS)
                    qcols = slice(half * WINDOW, (half + 1) * WINDOW)
                    krow = pos * 2 * WINDOW
                    st = scores[kh]
                    s = (jnp.where(own, st[krow + WINDOW:krow + 2 * WINDOW, qcols],
                                   st[krow:krow + WINDOW, qcols]) * c_scale
                         - (slope * log2e) * dist)
                    if sb == 0:
                        s = jnp.where(visible0, s, -jnp.inf)
                    m = jnp.max(s, axis=0, keepdims=True)
                    p = jnp.exp2(s - m)
                    den = jnp.sum(p, axis=0, keepdims=True) + jnp.exp2(sink_ref[h] * log2e - m)
                    rd.append(1.0 / den)
                    pr.append(jnp.concatenate(
                        [jnp.where(own, 0.0, p).astype(BF16), jnp.where(own, p, 0.0).astype(BF16)],
                        axis=0))
                probs.append(jnp.concatenate(pr, axis=1))
                rdens.append(jnp.concatenate(rd, axis=1))
        out_t = [None] * N_HEADS
        for kh in range(N_KV_HEADS):
            for pos in range(2):
                o2 = jnp.dot(vcats[kh], probs[2 * kh + pos], preferred_element_type=F32)
                o2 = o2 * rdens[2 * kh + pos]
                out_t[kh * GROUP + pos] = o2[:, :WINDOW]
                out_t[kh * GROUP + 2 + pos] = o2[:, WINDOW:]
        pending = (rows, jnp.transpose(jnp.concatenate(out_t, axis=0)).astype(BF16))
        prev_k, prev_vt = cur_k, cur_vt
    project(*pending)


def _attention(proj, za, sinks, wb, seq, chunk, t):
    sub = TQ_ATTN // WINDOW
    n_tiles = t // TQ_ATTN
    first = chunk * n_tiles
    src = lambda w, c: pl.BlockSpec((TQ_ATTN, w), lambda i: (first + i, c))
    prev = lambda c: pl.BlockSpec((WINDOW, KV_WIDTH),
                                  lambda i: (jnp.maximum((first + i) * sub - 1, 0), c))
    return pl.pallas_call(
        functools.partial(_attn_kernel, first_tile_index=first, tiles_per_seq=seq // TQ_ATTN),
        name="swattn",
        grid=(n_tiles,),
        in_specs=[
            pl.BlockSpec(memory_space=pltpu.SMEM),
            src(D_MODEL, COL_Q), src(KV_WIDTH, COL_K), src(KV_WIDTH, COL_V), prev(COL_K), prev(COL_V),
            src(D_MODEL, COL_GB), src(D_MODEL, 0),
            pl.BlockSpec((D_MODEL, D_MODEL), lambda i: (0, 0), pipeline_mode=pl.Buffered(1)),
        ],
        out_specs=pl.BlockSpec((TQ_ATTN, D_MODEL), lambda i: (i, 0)),
        out_shape=jax.ShapeDtypeStruct((t, D_MODEL), BF16),
        compiler_params=pltpu.CompilerParams(
            dimension_semantics=("arbitrary",), vmem_limit_bytes=VMEM_LIMIT),
    )(sinks, proj, proj, proj, proj, proj, proj, za, wb)


def _mix_kernel(merged_ref, x_ref, wo_ref, g_ref, wr_ref, br_ref,
                xmid_ref, h_ref, route_ref, cnt_ref):
    subs = [slice(s * SUB_MIX, (s + 1) * SUB_MIX) for s in range(TM_MIX // SUB_MIX)]
    xm = [x_ref[r, :] + jnp.dot(merged_ref[r, :], wo_ref[...], preferred_element_type=F32)
          for r in subs]
    hs = []
    for r, v in zip(subs, xm):
        xmid_ref[r, :] = _pack_bf16_pairs(v)
        h = _rms(v, g_ref[...])
        h_ref[r, :] = _pack_bf16_pairs(h)
        hs.append(h)
    wr = wr_ref[...]
    logits = []
    for h in hs:
        h_hi = h.astype(BF16)
        h_lo = (h - h_hi.astype(F32)).astype(BF16)
        both = jnp.dot(h_hi, wr, preferred_element_type=F32)
        logits.append(both[:, :LANES] + both[:, LANES:]
                      + jnp.dot(h_lo, wr[:, :LANES], preferred_element_type=F32) + br_ref[...])
    subs_per_cnt = TM_CNT // SUB_MIX
    cnts = [jnp.zeros((8, LANES), F32) for _ in range(TM_MIX // TM_CNT)]
    for s, (r, lg) in enumerate(zip(subs, logits)):
        route, cnt = _route(lg)
        route_ref[r, :] = route
        cnts[s // subs_per_cnt] = cnts[s // subs_per_cnt] + cnt
    for c, cnt in enumerate(cnts):
        cnt_ref[c * 8:(c + 1) * 8, :] = cnt


def _route(logits):
    n = logits.shape[0]
    lt = jnp.transpose(logits)
    sub = lax.broadcasted_iota(I32, (EXPERTS_PER_GROUP, n), 0)
    neg = -jnp.inf
    gl = jnp.where(sub < N_GROUPS, lt[N_EXPERTS:N_EXPERTS + EXPERTS_PER_GROUP], neg)
    gmax = jnp.max(gl, axis=0, keepdims=True)
    g_idx = jnp.min(jnp.where(gl == gmax, sub, EXPERTS_PER_GROUP), axis=0, keepdims=True)
    p_g = 1.0 / jnp.sum(jnp.exp(gl - gmax), axis=0, keepdims=True)
    v1 = v2 = i1 = i2 = None
    for g in range(N_GROUPS):
        eg = lt[g * EXPERTS_PER_GROUP:(g + 1) * EXPERTS_PER_GROUP]
        a1 = jnp.max(eg, axis=0, keepdims=True)
        j1 = jnp.min(jnp.where(eg == a1, sub, EXPERTS_PER_GROUP), axis=0, keepdims=True)
        eg2 = jnp.where(sub == j1, neg, eg)
        a2 = jnp.max(eg2, axis=0, keepdims=True)
        j2 = jnp.min(jnp.where(eg2 == a2, sub, EXPERTS_PER_GROUP), axis=0, keepdims=True)
        if g == 0:
            v1, v2, i1, i2 = a1, a2, j1, j2
        else:
            chosen = g_idx == g
            v1, v2 = jnp.where(chosen, a1, v1), jnp.where(chosen, a2, v2)
            i1, i2 = jnp.where(chosen, j1, i1), jnp.where(chosen, j2, i2)
    e21 = jnp.exp(v2 - v1)
    w1 = p_g / (1.0 + e21)
    w2 = p_g * e21 / (1.0 + e21)
    e1 = g_idx * EXPERTS_PER_GROUP + i1
    e2 = g_idx * EXPERTS_PER_GROUP + i2
    rows8 = jnp.where(sub == 0, e1.astype(F32),
                      jnp.where(sub == 1, e2.astype(F32),
                                jnp.where(sub == 2, w1, jnp.where(sub == 3, w2, 0.0))))
    route_t = jnp.concatenate([rows8, jnp.zeros((LANES - EXPERTS_PER_GROUP, n), F32)], axis=0)
    expert_row = lax.broadcasted_iota(I32, (LANES, n), 0)
    onehot_t = ((expert_row == e1) | (expert_row == e2)).astype(BF16)
    cnt = lax.dot_general(jnp.ones((8, n), BF16), onehot_t, (((1,), (1,)), ((), ())),
                          preferred_element_type=F32)
    return jnp.transpose(route_t), cnt


def _mix(merged, xf, wo, g, wr, br, chunk, t):
    n_tiles = t // TM_MIX
    first = chunk * n_tiles
    cnt_rows = TM_MIX // TM_CNT * 8
    full = lambda shape: pl.BlockSpec(shape, lambda i: (0, 0), pipeline_mode=pl.Buffered(1))
    tile = lambda w=D_MODEL: pl.BlockSpec((TM_MIX, w), lambda i: (i, 0))
    return pl.pallas_call(
        _mix_kernel,
        name="merge_router",
        grid=(n_tiles,),
        in_specs=[
            tile(), pl.BlockSpec((TM_MIX, D_MODEL), lambda i: (first + i, 0)),
            full((D_MODEL, D_MODEL)), full((1, D_MODEL)),
            full((D_MODEL, 2 * LANES)), full((1, LANES)),
        ],
        out_specs=[tile(PACKED), tile(PACKED), tile(LANES),
                   pl.BlockSpec((cnt_rows, LANES), lambda i: (i, 0))],
        out_shape=[
            jax.ShapeDtypeStruct((t, PACKED), I32),
            jax.ShapeDtypeStruct((t, PACKED), I32),
            jax.ShapeDtypeStruct((t, LANES), F32),
            jax.ShapeDtypeStruct((n_tiles * cnt_rows, LANES), F32),
        ],
        compiler_params=pltpu.CompilerParams(
            dimension_semantics=("arbitrary",), vmem_limit_bytes=VMEM_LIMIT),
    )(merged, xf, wo, g, wr, br)


def _pos_kernel(route_ref, base_ref, pos_ref):
    lane = lax.broadcasted_iota(I32, (TM_CNT, LANES), 1)
    r = lax.broadcasted_iota(I32, (TM_CNT, TM_CNT), 0)
    c = lax.broadcasted_iota(I32, (TM_CNT, TM_CNT), 1)
    lower = (c < r).astype(BF16)
    subs = [slice(s * TM_CNT, (s + 1) * TM_CNT) for s in range(POS_TILES)]
    routes = [route_ref[rs, :] for rs in subs]
    e1 = [jnp.sum(jnp.where(lane == 0, rt, 0.0), axis=-1, keepdims=True).astype(I32) for rt in routes]
    e2 = [jnp.sum(jnp.where(lane == 1, rt, 0.0), axis=-1, keepdims=True).astype(I32) for rt in routes]
    onehot = [((lane == a) | (lane == b)).astype(BF16) for a, b in zip(e1, e2)]
    before = [jnp.dot(lower, oh, preferred_element_type=F32) + base_ref[s]
              for s, oh in enumerate(onehot)]
    for s, rs in enumerate(subs):
        p1 = jnp.sum(jnp.where(lane == e1[s], before[s], 0.0), axis=-1, keepdims=True)
        p2 = jnp.sum(jnp.where(lane == e2[s], before[s], 0.0), axis=-1, keepdims=True)
        packed = jnp.where(lane == 0, p1, jnp.where(lane == 1, p2, 0.0))
        pos_ref[:, rs] = jnp.transpose(packed)[0:TOP_K, :].astype(I32)


def _positions(route, base):
    t = route.shape[0]
    n_steps = t // (TM_CNT * POS_TILES)
    return pl.pallas_call(
        _pos_kernel,
        name="positions",
        grid=(n_steps,),
        in_specs=[
            pl.BlockSpec((TM_CNT * POS_TILES, LANES), lambda i: (i, 0)),
            pl.BlockSpec((POS_TILES, 1, LANES), lambda i: (i, 0, 0)),
        ],
        out_specs=pl.BlockSpec((TOP_K, TM_CNT * POS_TILES), lambda i: (0, i)),
        out_shape=jax.ShapeDtypeStruct((TOP_K, t), I32),
        compiler_params=pltpu.CompilerParams(dimension_semantics=("arbitrary",)),
    )(route, base)


def _sc_mesh():
    return plsc.VectorSubcoreMesh(core_axis_name="c", subcore_axis_name="s",
                                  num_cores=SC_CORES, num_subcores=SC_SUBCORES)


def _sc_worker():
    return lax.axis_index("s") * SC_CORES + lax.axis_index("c")


def _dispatch(pos, hp, rows):
    t = hp.shape[0]
    per_w = t // SC_WORKERS
    n_ch = per_w // SC_CHUNK
    pos4 = pos.reshape(TOP_K, SC_WORKERS, n_ch, SC_CHUNK)

    @functools.partial(
        pl.kernel, mesh=_sc_mesh(),
        out_type=jax.ShapeDtypeStruct((rows, PACKED), I32),
        scratch_types=[pltpu.VMEM((TOP_K, n_ch, SC_CHUNK), I32),
                       pltpu.VMEM((SC_CHUNK, PACKED), I32)])
    def scatter(hp_hbm, pos_hbm, xs_hbm, idx_v, rows_v):
        wid = _sc_worker()
        for k in range(TOP_K):
            pltpu.sync_copy(pos_hbm.at[k, wid], idx_v.at[k])

        def body(c, carry):
            start = pl.multiple_of(wid * per_w + c * SC_CHUNK, SC_CHUNK)
            pltpu.sync_copy(hp_hbm.at[pl.ds(start, SC_CHUNK)], rows_v)
            for k in range(TOP_K):
                pltpu.sync_copy(rows_v, xs_hbm.at[idx_v.at[k, c]])
            return carry

        lax.fori_loop(0, n_ch, body, 0)

    return scatter(hp, pos4)


def _gather_rows(table, idx):
    n = idx.shape[0]
    per_w = n // SC_WORKERS
    n_ch = per_w // SC_CHUNK
    idx3 = idx.reshape(SC_WORKERS, n_ch, SC_CHUNK)

    @functools.partial(
        pl.kernel, mesh=_sc_mesh(),
        out_type=jax.ShapeDtypeStruct((n, PACKED), I32),
        scratch_types=[pltpu.VMEM((n_ch, SC_CHUNK), I32),
                       pltpu.VMEM((SC_CHUNK, PACKED), I32)])
    def gather(table_hbm, idx_hbm, out_hbm, idx_v, rows_v):
        wid = _sc_worker()
        pltpu.sync_copy(idx_hbm.at[wid], idx_v)

        def body(c, carry):
            start = pl.multiple_of(wid * per_w + c * SC_CHUNK, SC_CHUNK)
            pltpu.sync_copy(table_hbm.at[idx_v.at[c]], rows_v)
            pltpu.sync_copy(rows_v, out_hbm.at[pl.ds(start, SC_CHUNK)])
            return carry

        lax.fori_loop(0, n_ch, body, 0)

    return gather(table, idx3)


def _expert_kernel(te_ref, nx_ref, sl_ref, ts_ref, tv_ref, xs_ref, wg_hbm, wu_hbm, wd_hbm, o_ref,
                   wg_s, wu_s, wd_s, sem):
    del ts_ref
    i = pl.program_id(0)
    n_valid = tv_ref[i]
    expert = te_ref[i]
    slot = sl_ref[i]

    def weight_copies(e, s):
        return (pltpu.make_async_copy(wg_hbm.at[e], wg_s.at[s], sem.at[s, 0]),
                pltpu.make_async_copy(wu_hbm.at[e], wu_s.at[s], sem.at[s, 1]),
                pltpu.make_async_copy(wd_hbm.at[e], wd_s.at[s], sem.at[s, 2]))

    @pl.when(i == 0)
    def _():
        for copy in weight_copies(expert, slot):
            copy.start()

    @pl.when(jnp.logical_or(i == 0, expert != te_ref[jnp.maximum(i - 1, 0)]))
    def _():
        for copy in weight_copies(expert, slot):
            copy.wait()
        nxt = nx_ref[i]

        @pl.when(nxt >= 0)
        def _():
            for copy in weight_copies(nxt, 1 - slot):
                copy.start()

    def mlp(n_sub):
        wg, wu, wd = wg_s.at[slot], wu_s.at[slot], wd_s.at[slot]
        subs = [slice(s * SUB_EXP, (s + 1) * SUB_EXP) for s in range(n_sub)]
        xin = []
        for r in subs:
            rid = r.start + lax.broadcasted_iota(I32, (SUB_EXP, PACKED), 0)
            lo, hi = _unpack_bf16_pairs(jnp.where(rid < n_valid, xs_ref[r, :], 0))
            xin.append((lo.astype(BF16), hi.astype(BF16)))
        ab = [(jnp.dot(lo, wg[:PACKED, :], preferred_element_type=F32)
               + jnp.dot(hi, wg[PACKED:, :], preferred_element_type=F32),
               jnp.dot(lo, wu[:PACKED, :], preferred_element_type=F32)
               + jnp.dot(hi, wu[PACKED:, :], preferred_element_type=F32)) for lo, hi in xin]
        for r, (ai, bi) in zip(subs, ab):
            hm = (ai * jax.nn.sigmoid(ai) * bi).astype(BF16)
            o_ref[r, :] = _pack_bf16_pairs(jnp.dot(hm, wd[...], preferred_element_type=F32))
        if n_sub * SUB_EXP < TM_EXP:
            o_ref[n_sub * SUB_EXP:, :] = jnp.zeros((TM_EXP - n_sub * SUB_EXP, PACKED), I32)

    n_subs = TM_EXP // SUB_EXP
    for n_sub in range(n_subs + 1):
        lo_rows = (n_sub - 1) * SUB_EXP if n_sub else -1
        in_range = jnp.logical_and(n_valid > lo_rows, n_valid <= n_sub * SUB_EXP)
        pl.when(in_range)(functools.partial(mlp, n_sub))


def _experts(tile_expert, next_expert, tile_slot, tile_src, tile_valid, xs, wg, wu, wd):
    rows = xs.shape[0]
    hbm = pl.BlockSpec(memory_space=pl.ANY)
    grid_spec = pltpu.PrefetchScalarGridSpec(
        num_scalar_prefetch=5,
        grid=(rows // TM_EXP,),
        in_specs=[pl.BlockSpec((TM_EXP, PACKED), lambda i, te, nx, sl, ts, tv: (ts[i], 0)),
                  hbm, hbm, hbm],
        out_specs=pl.BlockSpec((TM_EXP, PACKED), lambda i, te, nx, sl, ts, tv: (i, 0)),
        scratch_shapes=[
            pltpu.VMEM((2, D_MODEL, D_FF), BF16), pltpu.VMEM((2, D_MODEL, D_FF), BF16),
            pltpu.VMEM((2, D_FF, D_MODEL), BF16),
            pltpu.SemaphoreType.DMA((2, 3)),
        ],
    )
    return pl.pallas_call(
        _expert_kernel,
        name="experts",
        grid_spec=grid_spec,
        out_shape=jax.ShapeDtypeStruct((rows, PACKED), I32),
        compiler_params=pltpu.CompilerParams(
            dimension_semantics=("arbitrary",), vmem_limit_bytes=VMEM_LIMIT),
    )(tile_expert, next_expert, tile_slot, tile_src, tile_valid, xs, wg, wu, wd)


def _combine_kernel(y1_ref, y2_ref, route_ref, xmid_ref, g_ref, *rest):
    o_ref = rest[-1]
    route = route_ref[...]
    lane = lax.broadcasted_iota(I32, route.shape, 1)
    w1 = jnp.sum(jnp.where(lane == 2, route, 0.0), axis=-1, keepdims=True)
    w2 = jnp.sum(jnp.where(lane == 3, route, 0.0), axis=-1, keepdims=True)
    lo1, hi1 = _unpack_bf16_pairs(y1_ref[...])
    lo2, hi2 = _unpack_bf16_pairs(y2_ref[...])
    lox, hix = _unpack_bf16_pairs(xmid_ref[...])
    x_out = jnp.concatenate([lox + (lo1 * w1 + lo2 * w2), hix + (hi1 * w1 + hi2 * w2)], axis=1)
    o_ref[...] = _rms(x_out, g_ref[...])


def _combine(yg, route, xmid, g, chunk, t_total, out_prev):
    t = xmid.shape[0]
    n_tiles = t // TM_CMB
    first = chunk * n_tiles
    in_specs = [
        pl.BlockSpec((TM_CMB, PACKED), lambda i: (i, 0)),
        pl.BlockSpec((TM_CMB, PACKED), lambda i: (n_tiles + i, 0)),
        pl.BlockSpec((TM_CMB, LANES), lambda i: (i, 0)),
        pl.BlockSpec((TM_CMB, PACKED), lambda i: (i, 0)),
        pl.BlockSpec((1, D_MODEL), lambda i: (0, 0)),
    ]
    args = [yg, yg, route, xmid, g]
    aliases = {}
    if out_prev is not None:
        in_specs.append(pl.BlockSpec(memory_space=pl.ANY))
        aliases = {len(args): 0}
        args.append(out_prev)
    return pl.pallas_call(
        _combine_kernel,
        name="combine",
        grid=(n_tiles,),
        in_specs=in_specs,
        out_specs=pl.BlockSpec((TM_CMB, D_MODEL), lambda i: (first + i, 0)),
        out_shape=jax.ShapeDtypeStruct((t_total, D_MODEL), F32),
        input_output_aliases=aliases,
        compiler_params=pltpu.CompilerParams(
            dimension_semantics=("arbitrary",), vmem_limit_bytes=VMEM_LIMIT),
    )(*args)


def _split_bf16(w):
    hi = w.astype(BF16)
    lo = (w - hi.astype(F32)).astype(BF16)
    return hi, lo


def kernel(x, norm_mix, w_in, conv_w, conv_b, w_a_out, sinks, w_b_out, w_o, norm_ffn, w_group,
           b_group, w_expert, b_expert, w_gate, w_up, w_down, norm_final):
    bsz, seq, d = x.shape
    t = bsz * seq
    assert d == D_MODEL and seq % TM_PROJ == 0 and seq % TQ_ATTN == 0
    xf = x.reshape(t, d)
    row = lambda v: v.reshape(1, -1)

    assert w_in.shape == (D_MODEL, IN_COLS)
    za, proj, w_gate, w_up, w_down, wb, wo = _inproj(xf, row(norm_mix), w_in.astype(F32), conv_w,
                                                     row(conv_b), w_a_out.astype(F32), seq,
                                                     w_gate, w_up, w_down, w_b_out, w_o)
    pad = LANES - N_GROUPS - N_EXPERTS
    w_r = jnp.concatenate([w_expert, w_group, jnp.zeros((d, pad), F32)], axis=1)
    b_r = jnp.concatenate([b_expert, b_group, jnp.zeros((pad,), F32)]).reshape(1, LANES)
    wr = jnp.concatenate(_split_bf16(w_r), axis=1)

    t_chunk = t // MOE_CHUNKS
    assert t == t_chunk * MOE_CHUNKS and all(
        t_chunk % step == 0
        for step in (TQ_ATTN, TM_MIX, TM_CNT * POS_TILES, TM_CMB, SC_WORKERS * SC_CHUNK))
    out = None
    for chunk in range(MOE_CHUNKS):
        merged = _attention(proj, za, sinks, wb, seq, chunk, t_chunk)
        xmid, h2, route, cnt = _mix(merged, xf, wo, row(norm_ffn), wr, b_r, chunk, t_chunk)
        out = _moe_chunk(xmid, h2, route, cnt, w_gate, w_up, w_down, row(norm_final), chunk, t, out)
    return out.reshape(bsz, seq, d)


def _moe_chunk(xmid, h2, route, cnt, w_gate, w_up, w_down, g_final, chunk, t_total, out_prev):
    t = xmid.shape[0]
    n_tiles = t // TM_CNT
    cnt = cnt.reshape(n_tiles, 8, LANES)[:, 0, :N_EXPERTS].astype(I32)
    totals = jnp.sum(cnt, axis=0)
    tiles_e = (totals + TM_EXP - 1) // TM_EXP
    tile_end = jnp.cumsum(tiles_e)
    offset = (tile_end - tiles_e) * TM_EXP
    base = offset[None, :] + jnp.cumsum(cnt, axis=0) - cnt
    base = jnp.pad(base, ((0, 0), (0, LANES - N_EXPERTS))).astype(F32).reshape(n_tiles, 1, LANES)
    rows = t * TOP_K + N_EXPERTS * TM_EXP
    n_active = tile_end[-1]
    tile_id = jnp.arange(rows // TM_EXP, dtype=I32)
    tile_src = jnp.minimum(tile_id, n_active - 1)
    tile_expert = jnp.sum((tile_src[:, None] >= tile_end[None, :]).astype(I32), axis=1)
    tile_expert = jnp.minimum(tile_expert, N_EXPERTS - 1)
    row_in_expert = (tile_id - (tile_end - tiles_e)[tile_expert]) * TM_EXP
    tile_valid = jnp.clip(totals[tile_expert] - row_in_expert, 0, TM_EXP)
    tile_valid = jnp.where(tile_id < n_active, tile_valid, 0).astype(I32)
    after = tile_end[tile_expert]
    next_expert = jnp.where(after < n_active, tile_expert[jnp.minimum(after, n_active - 1)], -1)
    first_of_expert = jnp.concatenate(
        [jnp.ones((1,), I32), (tile_expert[1:] != tile_expert[:-1]).astype(I32)])
    tile_slot = (jnp.cumsum(first_of_expert) - 1) % 2

    pos = _positions(route, base)
    xs = _dispatch(pos, h2, rows)
    ys = _experts(tile_expert.astype(I32), next_expert.astype(I32), tile_slot.astype(I32),
                  tile_src.astype(I32), tile_valid, xs, w_gate, w_up, w_down)
    yg = _gather_rows(ys, pos.reshape(TOP_K * t))
    return _combine(yg, route, xmid, g_final, chunk, t_total, out_prev)
```
